```python
import math
import jax, jax.numpy as jnp
from jax import lax
import numpy as np

D_MODEL = 1024
BATCH = 8
SEQ = 2048
DEPTH = 1
DEC_BATCH = 32
DEC_SEQ = 8
PAST_LEN = 8192
PAGE_SIZE = 128

NSA_HEADS = 8
NSA_KV_HEADS = 2
NSA_HD = 64
CMP_BLOCK = 32
CMP_STRIDE = 16
SEL_BLOCK = 64
SEL_TOPN = 8
WINDOW = 512
Q_BLOCK = 128
FORCED_BONUS = 1e4
GDN_HEADS = 4
GDN_HD = 128
GDN_CONV = 4
GDN_CHUNK = 64
MEM_TOKENS = 256
XA_HEADS = 4
XA_HD = D_MODEL // XA_HEADS
D_FF = 2816
FFN_CONV = 3

ROPE_THETA = 10000.0
EPS = 1e-6
NEG = -1e30

NSA_Q_W = NSA_HEADS * NSA_HD
NSA_KV_W = 3 * 2 * NSA_KV_HEADS * NSA_HD
NSA_GATE_W = 3 * NSA_HEADS
GDN_W = GDN_HEADS * GDN_HD
MIX_W = NSA_Q_W + GDN_W
XA_W = XA_HEADS * XA_HD
IN_SPLITS = (NSA_Q_W, NSA_KV_W, NSA_GATE_W, 3 * GDN_W, GDN_W, GDN_HEADS, GDN_HEADS)
IN_COLS = NSA_Q_W + NSA_KV_W + NSA_GATE_W + 4 * GDN_W + 2 * GDN_HEADS

kernel_name = 'hymba_nsa_gdn_convffn_step'


def _rms(x, w):
    xf = x.astype(jnp.float32)
    y = xf * lax.rsqrt(jnp.mean(xf * xf, axis=-1, keepdims=True) + EPS)
    return (y * w.astype(jnp.float32)).astype(x.dtype)


def _l2n(x):
    xf = x.astype(jnp.float32)
    return xf * lax.rsqrt(jnp.sum(xf * xf, axis=-1, keepdims=True) + EPS)


def _rope(x, pos):
    half = x.shape[-1] // 2
    inv = jnp.power(ROPE_THETA, -jnp.arange(half, dtype=jnp.float32) / half)
    ang = pos.astype(jnp.float32)[:, None] * inv[None, :]
    cos = jnp.cos(ang)[None, :, None, :]
    sin = jnp.sin(ang)[None, :, None, :]
    xf = x.astype(jnp.float32)
    x1, x2 = xf[..., :half], xf[..., half:]
    return jnp.concatenate([x1 * cos - x2 * sin, x1 * sin + x2 * cos], axis=-1).astype(x.dtype)


def _causal_dwconv(x, buf, w):
    K = w.shape[0]
    T = x.shape[1]
    xp = jnp.concatenate([buf.astype(x.dtype), x], axis=1)
    y = xp[:, 0:T] * w[0]
    for i in range(1, K):
        y = y + xp[:, i:i + T] * w[i]
    return y, xp[:, xp.shape[1] - (K - 1):]


def _masked_probs(s, valid):
    s = jnp.where(valid, s, NEG)
    p = jnp.where(valid, jnp.exp(s - jnp.max(s, axis=-1, keepdims=True)), 0.0)
    return p / jnp.maximum(jnp.sum(p, axis=-1, keepdims=True), 1e-30)


def _compress(x, pe, w):
    B, L = x.shape[:2]
    n_chunk = L // CMP_STRIDE
    r = CMP_BLOCK // CMP_STRIDE
    n_cmp = n_chunk - r + 1
    ch = x.reshape(B, n_chunk, CMP_STRIDE, NSA_KV_HEADS, NSA_HD)
    blocks = jnp.concatenate([ch[:, i:i + n_cmp] for i in range(r)], axis=2)
    blocks = blocks + pe[None, None, :, None, :]
    return jnp.einsum('bcjkd,jde->bcke', blocks, w)


def _cmp_to_sel(n_cmp, n_sel):
    cs = jnp.arange(n_cmp)[:, None] * CMP_STRIDE
    ss = jnp.arange(n_sel)[None, :] * SEL_BLOCK
    return ((cs < ss + SEL_BLOCK) & (cs + CMP_BLOCK > ss)).astype(jnp.float32)


def _sel_blocks(x):
    B, L = x.shape[:2]
    return x.reshape(B, L // SEL_BLOCK, SEL_BLOCK, NSA_KV_HEADS, NSA_HD).transpose(0, 3, 1, 2, 4)


def _nsa_core(qc, qr, gates, q_pos, k_cmp, v_cmp, c2s, ks_b, vs_b, kw, vw, kw_pos):
    f32 = jnp.float32
    B, Tq = qc.shape[:2]
    K = NSA_KV_HEADS
    G = NSA_HEADS // K
    scale = NSA_HD ** -0.5
    qcg = qc.reshape(B, Tq, K, G, NSA_HD).astype(f32)
    qrg = qr.reshape(B, Tq, K, G, NSA_HD).astype(f32)
    n_cmp = k_cmp.shape[1]
    cmp_end = jnp.arange(n_cmp) * CMP_STRIDE + (CMP_BLOCK - 1)
    s = jnp.einsum('btkgd,bckd->bkgtc', qcg, k_cmp.astype(f32)) * scale
    p_cmp = _masked_probs(s, cmp_end[None, :] <= q_pos[:, None])
    o_cmp = jnp.einsum('bkgtc,bckd->btkgd', p_cmp, v_cmp.astype(f32))
    n_sel = ks_b.shape[2]
    imp = jnp.einsum('bkgtc,cs->bkts', p_cmp, c2s)
    blk = jnp.arange(n_sel)[None, :]
    q_blk = (q_pos // SEL_BLOCK)[:, None]
    visible = blk <= q_blk
    forced = (blk == 0) | (blk == q_blk) | (blk == q_blk - 1)
    score = jnp.where(visible, imp + jnp.where(forced, FORCED_BONUS, 0.0), NEG)
    n_top = min(SEL_TOPN, n_sel)
    top_val, top_idx = lax.top_k(score, n_top)
    bi = jnp.arange(B)[:, None, None, None]
    ki = jnp.arange(K)[None, :, None, None]
    k_sel = ks_b[bi, ki, top_idx].astype(f32)
    v_sel = vs_b[bi, ki, top_idx].astype(f32)
    key_pos = top_idx[..., None] * SEL_BLOCK + jnp.arange(SEL_BLOCK)
    sel_valid = (top_val > 0.5 * NEG)[..., None] & (key_pos <= q_pos[None, None, :, None, None])
    m = n_top * SEL_BLOCK
    s = jnp.einsum('btkgd,bktnjd->bkgtnj', qrg, k_sel).reshape(B, K, G, Tq, m) * scale
    p_sel = _masked_probs(s, sel_valid.reshape(B, K, 1, Tq, m))
    o_sel = jnp.einsum('bkgtm,bktmd->btkgd', p_sel, v_sel.reshape(B, K, Tq, m, NSA_HD))
    s = jnp.einsum('btkgd,blkd->bkgtl', qrg, kw.astype(f32)) * scale
    rel = q_pos[:, None] - kw_pos[None, :]
    p_win = _masked_probs(s, (rel >= 0) & (rel < WINDOW) & (kw_pos[None, :] >= 0))
    o_win = jnp.einsum('bkgtl,blkd->btkgd', p_win, vw.astype(f32))
    g = gates.reshape(B, Tq, K, G, 3).astype(f32)
    o = o_cmp * g[..., 0:1] + o_sel * g[..., 1:2] + o_win * g[..., 2:3]
    return o.reshape(B, Tq, NSA_Q_W).astype(qc.dtype)


def _nsa_prompt(qc, qr, gates, rows, wrows, cmp_pe, cmp_w):
    B, T = qc.shape[:2]
    k_cmp = _compress(rows[:, :, 0], cmp_pe[0], cmp_w[0])
    v_cmp = _compress(rows[:, :, 1], cmp_pe[1], cmp_w[1])
    ks_b = _sel_blocks(rows[:, :, 2])
    vs_b = _sel_blocks(rows[:, :, 3])
    c2s = _cmp_to_sel(k_cmp.shape[1], ks_b.shape[2])
    wpad = jnp.pad(wrows, ((0, 0), (WINDOW, 0), (0, 0), (0, 0), (0, 0)))
    n_blk = T // Q_BLOCK

    def one_block(args):
        b, qc_b, qr_b, g_b = args
        start = b * Q_BLOCK
        q_pos = start + jnp.arange(Q_BLOCK)
        band = lax.dynamic_slice_in_dim(wpad, start, WINDOW + Q_BLOCK, axis=1)
        kw_pos = start - WINDOW + jnp.arange(WINDOW + Q_BLOCK)
        return _nsa_core(qc_b, qr_b, g_b, q_pos, k_cmp, v_cmp, c2s, ks_b, vs_b,
                         band[:, :, 0], band[:, :, 1], kw_pos)

    def split(a):
        return a.reshape(B, n_blk, Q_BLOCK, *a.shape[2:]).swapaxes(0, 1)

    out = lax.map(one_block, (jnp.arange(n_blk), split(qc), split(qr), split(gates)))
    return out.swapaxes(0, 1).reshape(B, T, NSA_Q_W)


def _nsa_sample(qc, qr, gates, rows, wrows, past_rows, win_buf, cmp_pe, cmp_w):
    B, T = qc.shape[:2]
    past = past_rows.shape[1]
    full = jnp.concatenate([past_rows.astype(rows.dtype), rows], axis=1)
    L = full.shape[1]
    Lp = -(-L // SEL_BLOCK) * SEL_BLOCK
    full = jnp.pad(full, ((0, 0), (0, Lp - L), (0, 0), (0, 0), (0, 0)))
    k_cmp = _compress(full[:, :, 0], cmp_pe[0], cmp_w[0])
    v_cmp = _compress(full[:, :, 1], cmp_pe[1], cmp_w[1])
    ks_b = _sel_blocks(full[:, :, 2])
    vs_b = _sel_blocks(full[:, :, 3])
    c2s = _cmp_to_sel(k_cmp.shape[1], ks_b.shape[2])
    wb = win_buf.shape[1]
    wall = jnp.concatenate([win_buf.astype(wrows.dtype), wrows], axis=1)
    kw_pos = past - wb + jnp.arange(wb + T)
    q_pos = past + jnp.arange(T)
    o = _nsa_core(qc, qr, gates, q_pos, k_cmp, v_cmp, c2s, ks_b, vs_b,
                  wall[:, :, 0], wall[:, :, 1], kw_pos)
    keep = min(WINDOW, wb + T)
    return o, wall[:, wb + T - keep:]


def _gated_delta(q, k, v, beta, g, s0):
    f32 = jnp.float32
    B, T, H, dk = q.shape
    dv = v.shape[-1]
    C = GDN_CHUNK
    n = -(-T // C)
    pad = n * C - T

    def chunk(a):
        a = jnp.pad(a.astype(f32), [(0, 0), (0, pad)] + [(0, 0)] * (a.ndim - 2))
        return a.reshape(B, n, C, *a.shape[2:]).swapaxes(2, 3)

    qc, kc, vc, bc = chunk(q), chunk(k), chunk(v), chunk(beta)
    gc = jnp.cumsum(chunk(g), axis=-1)
    causal = jnp.tril(jnp.ones((C, C), bool))
    strict = jnp.tril(jnp.ones((C, C), bool), -1)
    diff = gc[..., :, None] - gc[..., None, :]
    decay = jnp.where(causal, jnp.exp(jnp.where(causal, diff, 0.0)), 0.0)
    kb = kc * bc[..., None]
    a_mat = jnp.where(strict, jnp.einsum('bnhid,bnhjd->bnhij', kb, kc) * decay, 0.0)
    rhs = jnp.concatenate([vc * bc[..., None], kb * jnp.exp(gc)[..., None]], axis=-1)
    sol = lax.linalg.triangular_solve(jnp.eye(C, dtype=f32) + a_mat, rhs, left_side=True,
                                      lower=True, unit_diagonal=True)
    u, w = sol[..., :dv], sol[..., dv:]
    qk = jnp.einsum('bnhid,bnhjd->bnhij', qc, kc) * decay
    q_dec = qc * jnp.exp(gc)[..., None]
    g_last = gc[..., -1]
    k_dec = kc * jnp.exp(g_last[..., None] - gc)[..., None]

    def step(S, xs):
        u_c, w_c, qd_c, qk_c, kd_c, gl_c = xs
        v_new = u_c - jnp.einsum('bhck,bhkv->bhcv', w_c, S)
        o_c = jnp.einsum('bhck,bhkv->bhcv', qd_c, S) + jnp.einsum('bhij,bhjv->bhiv', qk_c, v_new)
        S = S * jnp.exp(gl_c)[..., None, None] + jnp.einsum('bhck,bhcv->bhkv', kd_c, v_new)
        return S, o_c

    xs = tuple(a.swapaxes(0, 1) for a in (u, w, q_dec, qk, k_dec, g_last))
    S, o = lax.scan(step, s0.astype(f32), xs)
    o = o.transpose(1, 0, 3, 2, 4).reshape(B, n * C, H, dv)[:, :T]
    return o, S


def _gdn(qkv_pre, z_pre, b_pre, a_pre, conv_buf, s0, conv_w, a_log, dt_bias, norm_w):
    f32 = jnp.float32
    B, T = qkv_pre.shape[:2]
    qkv, buf_new = _causal_dwconv(qkv_pre, conv_buf, conv_w)
    qkv = jax.nn.silu(qkv)
    q, k, v = jnp.split(qkv, 3, axis=-1)
    shp = (B, T, GDN_HEADS, GDN_HD)
    q = _l2n(q.reshape(shp)) * (GDN_HD ** -0.5)
    k = _l2n(k.reshape(shp))
    v = v.reshape(shp).astype(f32)
    beta = jax.nn.sigmoid(b_pre.astype(f32))
    g = -jnp.exp(a_log.astype(f32)) * jax.nn.softplus(a_pre.astype(f32) + dt_bias.astype(f32))
    o, s_new = _gated_delta(q, k, v, beta, g, s0)
    o = _rms(o, norm_w) * jax.nn.silu(z_pre.reshape(shp).astype(f32))
    return o.reshape(B, T, GDN_W).astype(qkv_pre.dtype), buf_new, s_new.astype(s0.dtype)


def _mem_kv(mem, mem_norm_w, w_xk, w_xv, xk_norm_w):
    B, M = mem.shape[:2]
    mn = _rms(mem, mem_norm_w)
    k = _rms((mn @ w_xk).reshape(B, M, XA_HEADS, XA_HD), xk_norm_w)
    v = (mn @ w_xv).reshape(B, M, XA_HEADS, XA_HD)
    return jnp.stack([k, v], axis=2)


def _cross_attn(x, mem_kv, norm_w, w_xq, xq_norm_w, w_xo):
    f32 = jnp.float32
    B, T = x.shape[:2]
    q = _rms((_rms(x, norm_w) @ w_xq).reshape(B, T, XA_HEADS, XA_HD), xq_norm_w)
    s = jnp.einsum('bthd,bmhd->bhtm', q.astype(f32), mem_kv[:, :, 0].astype(f32)) * (XA_HD ** -0.5)
    p = jax.nn.softmax(s, axis=-1)
    o = jnp.einsum('bhtm,bmhd->bthd', p, mem_kv[:, :, 1].astype(f32))
    return o.reshape(B, T, XA_W).astype(x.dtype) @ w_xo


def _conv_ffn(x, buf, norm_w, w_up, conv_w, conv_b, w_down):
    up = _rms(x, norm_w) @ w_up
    h, buf_new = _causal_dwconv(up, buf, conv_w)
    a, u = jnp.split(h + conv_b, 2, axis=-1)
    return (jax.nn.silu(a) * u) @ w_down, buf_new


def _layer(x, pos, nsa_attend, gdn_buf, gdn_s0, mem_kv, ffn_buf,
           attn_norm_w, w_in, nsa_q_norm_w, nsa_k_norm_w, gdn_conv_w, gdn_a_log,
           gdn_dt_bias, gdn_norm_w, w_out, xattn_norm_w, w_xq, xq_norm_w, w_xo,
           ffn_norm_w, w_up, ffn_conv_w, ffn_conv_b, w_down):
    B, T = x.shape[:2]
    xn = _rms(x, attn_norm_w)
    cuts, acc = [], 0
    for width in IN_SPLITS[:-1]:
        acc += width
        cuts.append(acc)
    q_pre, kv_pre, gate_pre, qkv_pre, z_pre, b_pre, a_pre = jnp.split(xn @ w_in, cuts, axis=-1)
    q = _rms(q_pre.reshape(B, T, NSA_HEADS, NSA_HD), nsa_q_norm_w)
    qr = _rope(q, pos)
    kv = kv_pre.reshape(B, T, 3, 2, NSA_KV_HEADS, NSA_HD)
    k = _rms(kv[:, :, :, 0], nsa_k_norm_w[:, None, :])
    v = kv[:, :, :, 1]
    rows = jnp.stack([k[:, :, 0], v[:, :, 0], _rope(k[:, :, 1], pos), v[:, :, 1]], axis=2)
    wrows = jnp.stack([_rope(k[:, :, 2], pos), v[:, :, 2]], axis=2)
    gates = jax.nn.sigmoid(gate_pre.reshape(B, T, NSA_HEADS, 3))
    o_nsa, win_state = nsa_attend(q, qr, gates, rows, wrows)
    o_gdn, gdn_buf_new, gdn_s = _gdn(qkv_pre, z_pre, b_pre, a_pre, gdn_buf, gdn_s0,
                                     gdn_conv_w, gdn_a_log, gdn_dt_bias, gdn_norm_w)
    h = x + jnp.concatenate([o_nsa, o_gdn.astype(o_nsa.dtype)], axis=-1) @ w_out
    h = h + _cross_attn(h, mem_kv, xattn_norm_w, w_xq, xq_norm_w, w_xo)
    f, ffn_buf_new = _conv_ffn(h, ffn_buf, ffn_norm_w, w_up, ffn_conv_w, ffn_conv_b, w_down)
    return h + f, rows, win_state, gdn_s, gdn_buf_new, ffn_buf_new


def setup_inputs(seed: int = 0) -> dict:
    key = jax.random.key(seed)
    ks = iter(jax.random.split(key, 48))

    def nrm(shape, scale):
        return jax.random.normal(next(ks), shape, jnp.float32) * scale

    def gain(shape):
        return 1.0 + 0.02 * jax.random.normal(next(ks), shape, jnp.float32)

    n_pages = PAST_LEN // PAGE_SIZE
    n_used = DEC_BATCH * n_pages
    n_pool = n_used + n_used // 4
    win_len = min(WINDOW, PAST_LEN)
    page_table = jax.random.permutation(next(ks), n_pool)[:n_used].reshape(DEC_BATCH, n_pages).astype(jnp.int32)
    a_log = jnp.log(jax.random.uniform(next(ks), (DEPTH, GDN_HEADS), jnp.float32, 1.0, 16.0))
    dt = jnp.exp(jax.random.uniform(next(ks), (DEPTH, GDN_HEADS), jnp.float32,
                                    math.log(1e-3), math.log(1e-1)))
    dt_bias = dt + jnp.log(-jnp.expm1(-dt))
    return {
        'x_prompt': nrm((BATCH, SEQ, D_MODEL), 1.0),
        'x_sample': nrm((DEC_BATCH, DEC_SEQ, D_MODEL), 1.0),
        'mem_prompt': nrm((BATCH, MEM_TOKENS, D_MODEL), 1.0),
        'cache_nsa_kv': nrm((DEPTH, n_pool, PAGE_SIZE, 4, NSA_KV_HEADS, NSA_HD), 1.0),
        'cache_nsa_win': nrm((DEPTH, DEC_BATCH, win_len, 2, NSA_KV_HEADS, NSA_HD), 1.0),
        'state_gdn': nrm((DEPTH, DEC_BATCH, GDN_HEADS, GDN_HD, GDN_HD), 1.0),
        'cache_gdn_conv': nrm((DEPTH, DEC_BATCH, GDN_CONV - 1, 3 * GDN_W), 1.0),
        'cache_ffn_conv': nrm((DEPTH, DEC_BATCH, FFN_CONV - 1, 2 * D_FF), 1.0),
        'cache_mem_kv': nrm((DEPTH, DEC_BATCH, MEM_TOKENS, 2, XA_HEADS, XA_HD), 1.0),
        'page_table': page_table,
        'attn_norm_w': gain((DEPTH, D_MODEL)),
        'w_in': nrm((DEPTH, D_MODEL, IN_COLS), D_MODEL ** -0.5),
        'nsa_q_norm_w': gain((DEPTH, NSA_HD)),
        'nsa_k_norm_w': gain((DEPTH, 3, NSA_HD)),
        'cmp_pe': nrm((DEPTH, 2, CMP_BLOCK, NSA_HD), 0.1),
        'cmp_w': nrm((DEPTH, 2, CMP_BLOCK, NSA_HD, NSA_HD), (CMP_BLOCK * NSA_HD) ** -0.5),
        'gdn_conv_w': nrm((DEPTH, GDN_CONV, 3 * GDN_W), GDN_CONV ** -0.5),
        'gdn_a_log': a_log,
        'gdn_dt_bias': dt_bias,
        'gdn_norm_w': gain((DEPTH, GDN_HD)),
        'w_out': nrm((DEPTH, MIX_W, D_MODEL), MIX_W ** -0.5),
        'mem_norm_w': gain((DEPTH, D_MODEL)),
        'w_xk': nrm((DEPTH, D_MODEL, XA_W), D_MODEL ** -0.5),
        'w_xv': nrm((DEPTH, D_MODEL, XA_W), D_MODEL ** -0.5),
        'xk_norm_w': gain((DEPTH, XA_HD)),
        'xattn_norm_w': gain((DEPTH, D_MODEL)),
        'w_xq': nrm((DEPTH, D_MODEL, XA_W), D_MODEL ** -0.5),
        'xq_norm_w': gain((DEPTH, XA_HD)),
        'w_xo': nrm((DEPTH, XA_W, D_MODEL), XA_W ** -0.5),
        'ffn_norm_w': gain((DEPTH, D_MODEL)),
        'w_up': nrm((DEPTH, D_MODEL, 2 * D_FF), D_MODEL ** -0.5),
        'ffn_conv_w': nrm((DEPTH, FFN_CONV, 2 * D_FF), FFN_CONV ** -0.5),
        'ffn_conv_b': nrm((DEPTH, 2 * D_FF), 0.01),
        'w_down': nrm((DEPTH, D_FF, D_MODEL), D_FF ** -0.5),
    }


def reference(x_prompt, x_sample, mem_prompt, cache_nsa_kv, cache_nsa_win, state_gdn,
              cache_gdn_conv, cache_ffn_conv, cache_mem_kv, page_table,
              attn_norm_w, w_in, nsa_q_norm_w, nsa_k_norm_w, cmp_pe, cmp_w, gdn_conv_w,
              gdn_a_log, gdn_dt_bias, gdn_norm_w, w_out, mem_norm_w, w_xk, w_xv, xk_norm_w,
              xattn_norm_w, w_xq, xq_norm_w, w_xo, ffn_norm_w, w_up, ffn_conv_w, ffn_conv_b, w_down):
    B, T = x_prompt.shape[:2]
    Bs, Ts = x_sample.shape[:2]
    past_len = page_table.shape[1] * cache_nsa_kv.shape[2]
    pos_p = jnp.arange(T)
    pos_s = past_len + jnp.arange(Ts)
    hp, hs = x_prompt, x_sample
    kv_p, win_p, st_p, gc_p, fc_p, mkv_p = [], [], [], [], [], []
    kv_s, win_s, st_s, gc_s, fc_s = [], [], [], [], []
    for l in range(DEPTH):
        def run(x, pos, attend, gbuf, gs0, mkv, fbuf):
            return _layer(x, pos, attend, gbuf, gs0, mkv, fbuf,
                          attn_norm_w[l], w_in[l], nsa_q_norm_w[l], nsa_k_norm_w[l], gdn_conv_w[l],
                          gdn_a_log[l], gdn_dt_bias[l], gdn_norm_w[l], w_out[l], xattn_norm_w[l],
                          w_xq[l], xq_norm_w[l], w_xo[l], ffn_norm_w[l], w_up[l], ffn_conv_w[l],
                          ffn_conv_b[l], w_down[l])

        def attend_prompt(q, qr, g, rows, wrows):
            n = rows.shape[1]
            keep = min(WINDOW, n)
            return _nsa_prompt(q, qr, g, rows, wrows, cmp_pe[l], cmp_w[l]), wrows[:, n - keep:]

        past_rows = cache_nsa_kv[l][page_table].reshape(Bs, past_len, 4, NSA_KV_HEADS, NSA_HD)

        def attend_sample(q, qr, g, rows, wrows):
            return _nsa_sample(q, qr, g, rows, wrows, past_rows, cache_nsa_win[l], cmp_pe[l], cmp_w[l])

        mem_kv = _mem_kv(mem_prompt, mem_norm_w[l], w_xk[l], w_xv[l], xk_norm_w[l])
        hp, a1, a2, a3, a4, a5 = run(hp, pos_p, attend_prompt,
                                     jnp.zeros((B, GDN_CONV - 1, 3 * GDN_W), hp.dtype),
                                     jnp.zeros((B, GDN_HEADS, GDN_HD, GDN_HD), hp.dtype),
                                     mem_kv,
                                     jnp.zeros((B, FFN_CONV - 1, 2 * D_FF), hp.dtype))
        kv_p.append(a1); win_p.append(a2); st_p.append(a3); gc_p.append(a4); fc_p.append(a5)
        mkv_p.append(mem_kv)
        hs, b1, b2, b3, b4, b5 = run(hs, pos_s, attend_sample, cache_gdn_conv[l], state_gdn[l],
                                     cache_mem_kv[l], cache_ffn_conv[l])
        kv_s.append(b1); win_s.append(b2); st_s.append(b3); gc_s.append(b4); fc_s.append(b5)
    nsa_kv_p = jnp.stack(kv_p)
    nsa_win_p = jnp.stack(win_p)
    gdn_state_p = jnp.stack(st_p)
    gdn_conv_p = jnp.stack(gc_p)
    ffn_conv_p = jnp.stack(fc_p)
    mem_kv_p = jnp.stack(mkv_p)
    nsa_kv_s = jnp.stack(kv_s)
    nsa_win_s = jnp.stack(win_s)
    gdn_state_s = jnp.stack(st_s)
    gdn_conv_s = jnp.stack(gc_s)
    ffn_conv_s = jnp.stack(fc_s)
    return (hp, hs, nsa_kv_p, nsa_win_p, gdn_state_p, gdn_conv_p, ffn_conv_p, mem_kv_p,
            nsa_kv_s, nsa_win_s, gdn_state_s, gdn_conv_s, ffn_conv_s)
```

```python
import functools
import math

import numpy as np
import jax
import jax.numpy as jnp
from jax import lax
from jax.experimental import pallas as pl
from jax.experimental.pallas import tpu as pltpu

F32 = jnp.float32
BF16 = jnp.bfloat16

D_MODEL = 1024
NSA_HEADS = 8
NSA_KV_HEADS = 2
NSA_GROUP = NSA_HEADS // NSA_KV_HEADS
NSA_HD = 64
CMP_BLOCK = 32
CMP_STRIDE = 16
SEL_BLOCK = 64
SEL_TOPN = 8
WINDOW = 512
FORCED_BONUS = 1e4
GDN_HEADS = 4
GDN_HD = 128
GDN_CONV = 4
GDN_CHUNK = 64
GDN_W = GDN_HEADS * GDN_HD
XA_HEADS = 4
XA_HD = 256
D_FF = 2816
FFN_CONV = 3
ROPE_THETA = 10000.0
EPS = 1e-6
NEG = -1e30

NSA_Q_W = NSA_HEADS * NSA_HD
NSA_KV_W = 3 * 2 * NSA_KV_HEADS * NSA_HD
NSA_GATE_W = 3 * NSA_HEADS
LANE = 128
MIB = 1 << 20


def _cparams(sem, vmem_mib):
    return pltpu.CompilerParams(dimension_semantics=sem, vmem_limit_bytes=vmem_mib * MIB)


def _sigmoid(x):
    return 1.0 / (1.0 + jnp.exp(-x))


def _dot(a, b):
    return jnp.dot(a, b, preferred_element_type=F32)


def _dot_t(a, b):
    return lax.dot_general(a, b, (((1,), (1,)), ((), ())), preferred_element_type=F32)


def _split_bf16(a):
    hi = a.astype(BF16)
    lo = (a - hi.astype(F32)).astype(BF16)
    return hi, lo


def _dot3(a, b):
    ah, al = _split_bf16(a)
    bh, bl = _split_bf16(b)
    return _dot(ah, bh) + (_dot(ah, bl) + _dot(al, bh))


def _col_chunk(n):
    for c in (512, 256, 128):
        if n % c == 0:
            return c
    return n


def _rms_mm_kernel(x_ref, g_ref, *refs, n_out):
    w_refs, o_refs = refs[:n_out], refs[n_out:]
    x = x_ref[...]
    xn = (x * lax.rsqrt(jnp.mean(x * x, axis=-1, keepdims=True) + EPS) * g_ref[...]).astype(BF16)
    for w_ref, o_ref in zip(w_refs, o_refs):
        n = w_ref.shape[1]
        ch = _col_chunk(n)
        for c in range(0, n, ch):
            o_ref[:, c:c + ch] = _dot(xn, w_ref[:, c:c + ch])


def rms_matmul(x, g, ws, tm):
    n_rows, d = x.shape
    tm = min(tm, n_rows)
    in_specs = [pl.BlockSpec((tm, d), lambda i: (i, 0)), pl.BlockSpec((1, d), lambda i: (0, 0))]
    in_specs += [pl.BlockSpec(w.shape, lambda i: (0, 0)) for w in ws]
    out_specs = [pl.BlockSpec((tm, w.shape[1]), lambda i: (i, 0)) for w in ws]
    out_shape = [jax.ShapeDtypeStruct((n_rows, w.shape[1]), F32) for w in ws]
    return pl.pallas_call(
        functools.partial(_rms_mm_kernel, n_out=len(ws)),
        grid=(n_rows // tm,), in_specs=in_specs, out_specs=out_specs, out_shape=out_shape,
        compiler_params=_cparams(("parallel",), 56), name="rms_matmul",
    )(x, g.reshape(1, d), *ws)


def _mm_res_kernel(a_ref, w_ref, r_ref, o_ref):
    a = a_ref[...].astype(BF16)
    n = w_ref.shape[1]
    ch = _col_chunk(n)
    for c in range(0, n, ch):
        o_ref[:, c:c + ch] = r_ref[:, c:c + ch] + _dot(a, w_ref[:, c:c + ch])


def matmul_res(a, w, res, tm):
    n_rows, k = a.shape
    n = w.shape[1]
    tm = min(tm, n_rows)
    return pl.pallas_call(
        _mm_res_kernel, grid=(n_rows // tm,),
        in_specs=[pl.BlockSpec((tm, k), lambda i: (i, 0)), pl.BlockSpec((k, n), lambda i: (0, 0)),
                  pl.BlockSpec((tm, n), lambda i: (i, 0))],
        out_specs=pl.BlockSpec((tm, n), lambda i: (i, 0)),
        out_shape=jax.ShapeDtypeStruct((n_rows, n), F32),
        compiler_params=_cparams(("parallel",), 40), name="matmul_res",
    )(a, w, res)


def _memkv_kernel(x_ref, g_ref, wk_ref, wv_ref, kn_ref, o_ref):
    x = x_ref[...]
    xn = (x * lax.rsqrt(jnp.mean(x * x, axis=-1, keepdims=True) + EPS) * g_ref[...]).astype(BF16)
    xa_w = XA_HEADS * XA_HD
    for h in range(XA_HEADS):
        sl = slice(h * XA_HD, (h + 1) * XA_HD)
        k = _dot(xn, wk_ref[:, sl])
        k = k * lax.rsqrt(jnp.mean(k * k, axis=-1, keepdims=True) + EPS) * kn_ref[...]
        o_ref[:, sl] = k
        o_ref[:, xa_w + h * XA_HD: xa_w + (h + 1) * XA_HD] = _dot(xn, wv_ref[:, sl])


def mem_kv_proj(mem, g, wk, wv, kn, tm=256):
    n_rows, d = mem.shape
    xa_w = XA_HEADS * XA_HD
    return pl.pallas_call(
        _memkv_kernel, grid=(n_rows // tm,),
        in_specs=[pl.BlockSpec((tm, d), lambda i: (i, 0)), pl.BlockSpec((1, d), lambda i: (0, 0)),
                  pl.BlockSpec((d, xa_w), lambda i: (0, 0)), pl.BlockSpec((d, xa_w), lambda i: (0, 0)),
                  pl.BlockSpec((1, XA_HD), lambda i: (0, 0))],
        out_specs=pl.BlockSpec((tm, 2 * xa_w), lambda i: (i, 0)),
        out_shape=jax.ShapeDtypeStruct((n_rows, 2 * xa_w), F32),
        compiler_params=_cparams(("parallel",), 40), name="mem_kv_proj",
    )(mem, g.reshape(1, d), wk, wv, kn.reshape(1, XA_HD))


def _xattn_kernel(q_ref, kv_ref, qn_ref, o_ref):
    xa_w = XA_HEADS * XA_HD
    for h in range(XA_HEADS):
        sl = slice(h * XA_HD, (h + 1) * XA_HD)
        q = q_ref[:, sl]
        q = q * lax.rsqrt(jnp.mean(q * q, axis=-1, keepdims=True) + EPS) * qn_ref[...]
        qb = (q * (XA_HD ** -0.5)).astype(BF16)
        k = kv_ref[0, :, sl].astype(BF16)
        v = kv_ref[0, :, xa_w + h * XA_HD: xa_w + (h + 1) * XA_HD].astype(BF16)
        s = _dot_t(qb, k)
        p = jnp.exp(s - jnp.max(s, axis=-1, keepdims=True))
        o = _dot(p.astype(BF16), v) / jnp.sum(p, axis=-1, keepdims=True)
        o_ref[:, sl] = o


def xattn_core(q_pre, kv, qn, n_batch, tq):
    n_rows, xa_w = q_pre.shape
    t = n_rows // n_batch
    tq = min(tq, t)
    nt = t // tq
    m = kv.shape[1]
    return pl.pallas_call(
        _xattn_kernel, grid=(n_batch, nt),
        in_specs=[pl.BlockSpec((tq, xa_w), lambda b, i: (b * nt + i, 0)),
                  pl.BlockSpec((1, m, 2 * xa_w), lambda b, i: (b, 0, 0)),
                  pl.BlockSpec((1, XA_HD), lambda b, i: (0, 0))],
        out_specs=pl.BlockSpec((tq, xa_w), lambda b, i: (b * nt + i, 0)),
        out_shape=jax.ShapeDtypeStruct((n_rows, xa_w), F32),
        compiler_params=_cparams(("parallel", "parallel"), 40), name="xattn_core",
    )(q_pre, kv, qn.reshape(1, XA_HD))


FFN_ACT_CHUNK = 256


def _ffn_kernel(h_ref, g_ref, wup_ref, cw_ref, cb_ref, wdn_ref, prev_ref, o_ref, buf_ref, xs_scr,
                *, tm, shift, base):
    t = pl.program_id(1)
    p0 = base - 2 * shift

    @pl.when(t == 0)
    def _():
        xs_scr[p0:base, :] = prev_ref[0]

    x = h_ref[...]
    xn = (x * lax.rsqrt(jnp.mean(x * x, axis=-1, keepdims=True) + EPS) * g_ref[...]).astype(BF16)
    acc = jnp.zeros((tm, D_MODEL), F32)
    for j in range(D_FF // FFN_ACT_CHUNK):
        halves = []
        for c0 in (j * FFN_ACT_CHUNK, D_FF + j * FFN_ACT_CHUNK):
            sl = slice(c0, c0 + FFN_ACT_CHUNK)
            xs_scr[base:base + tm, sl] = _dot(xn, wup_ref[:, sl])
            y = (cw_ref[0:1, sl] * xs_scr[p0:p0 + tm, sl]
                 + cw_ref[1:2, sl] * xs_scr[p0 + shift:p0 + shift + tm, sl]
                 + cw_ref[2:3, sl] * xs_scr[base:base + tm, sl])
            halves.append(y + cb_ref[:, sl])
        a, u = halves
        act = (a * _sigmoid(a) * u).astype(BF16)
        acc = acc + _dot(act, wdn_ref[j * FFN_ACT_CHUNK:(j + 1) * FFN_ACT_CHUNK, :])
    o_ref[...] = x + acc
    last = xs_scr[base + tm - 2 * shift: base + tm, :]
    buf_ref[0] = last
    xs_scr[p0:base, :] = last


def conv_ffn(h, g, wup, cw, cb, wdn, prev, n_seq, shift, tm):
    n_rows, d = h.shape
    t_rows = n_rows // n_seq
    tm = min(tm, t_rows)
    nt = t_rows // tm
    base = -(-2 * shift // 8) * 8
    kern = functools.partial(_ffn_kernel, tm=tm, shift=shift, base=base)
    const = lambda b, i: (0, 0)
    return pl.pallas_call(
        kern, grid=(n_seq, nt),
        in_specs=[pl.BlockSpec((tm, d), lambda b, i: (b * nt + i, 0)),
                  pl.BlockSpec((1, d), const),
                  pl.BlockSpec((d, 2 * D_FF), const),
                  pl.BlockSpec((FFN_CONV, 2 * D_FF), const),
                  pl.BlockSpec((1, 2 * D_FF), const),
                  pl.BlockSpec((D_FF, d), const),
                  pl.BlockSpec((1, 2 * shift, 2 * D_FF), lambda b, i: (b, 0, 0))],
        out_specs=[pl.BlockSpec((tm, d), lambda b, i: (b * nt + i, 0)),
                   pl.BlockSpec((1, 2 * shift, 2 * D_FF), lambda b, i: (b, 0, 0))],
        out_shape=[jax.ShapeDtypeStruct((n_rows, d), F32),
                   jax.ShapeDtypeStruct((n_seq, 2 * shift, 2 * D_FF), F32)],
        scratch_shapes=[pltpu.VMEM((base + tm, 2 * D_FF), F32)],
        compiler_params=_cparams(("arbitrary", "arbitrary"), 56), name="conv_ffn",
    )(h, g.reshape(1, d), wup, cw, cb.reshape(1, 2 * D_FF), wdn, prev)


GDN_STACK = GDN_HEADS * GDN_CHUNK
GDN_C3 = 3 * GDN_W
GDN_PREV0 = 8 - (GDN_CONV - 1)


def _gdn_masks():
    r = np.arange(GDN_STACK)[:, None]
    c = np.arange(GDN_STACK)[None, :]
    same = lambda n: (r // n) == (c // n)
    m = [same(2) & (c < r)]
    for n in (4, 8, 16, 32, 64):
        m.append(same(n) & ~same(n // 2) & (c < r))
    m.append(same(GDN_CHUNK) & (c <= r))
    m.append(same(GDN_CHUNK) & (c < r))
    m.append(r == c)
    return np.stack(m).astype(np.float32)


def _unit_lower_inverse(a, m_ref):
    t = m_ref[8] - a * m_ref[0]
    for lvl in range(1, 6):
        off = a * m_ref[lvl]
        t = t - _dot3(t, _dot3(off, t))
    return t


def _gdn_kernel(qkv_ref, z_ref, ba_ref, prev_ref, s0_ref, cw_ref, par_ref, nw_ref, m_ref,
                o_ref, buf_ref, s_ref, xs_scr, s_scr, *, tt):
    t = pl.program_id(1)
    nt = pl.num_programs(1)
    row0 = 8

    @pl.when(t == 0)
    def _():
        xs_scr[GDN_PREV0:row0, :] = prev_ref[0]
        s_scr[...] = s0_ref[0]

    xs_scr[row0:row0 + tt, :] = qkv_ref[...]
    causal = m_ref[6]
    neg_ea = -jnp.exp(par_ref[0:1, :])
    dtb = par_ref[1:2, :]
    rv = min(GDN_CHUNK, tt)
    pad = GDN_CHUNK - rv

    def padrows(a):
        if pad == 0:
            return a
        return jnp.concatenate([a, jnp.zeros((pad, a.shape[1]), a.dtype)], axis=0)

    for n in range(-(-tt // GDN_CHUNK)):
        r0 = n * GDN_CHUNK
        y = cw_ref[0:1, :] * xs_scr[GDN_PREV0 + r0:GDN_PREV0 + r0 + rv, :]
        for i in range(1, GDN_CONV):
            y = y + cw_ref[i:i + 1, :] * xs_scr[GDN_PREV0 + i + r0:GDN_PREV0 + i + r0 + rv, :]
        y = y * _sigmoid(y)
        qs, ks, vs, betas, gs = [], [], [], [], []
        for h in range(GDN_HEADS):
            q = y[:, h * GDN_HD:(h + 1) * GDN_HD]
            k = y[:, GDN_W + h * GDN_HD:GDN_W + (h + 1) * GDN_HD]
            v = y[:, 2 * GDN_W + h * GDN_HD:2 * GDN_W + (h + 1) * GDN_HD]
            q = q * lax.rsqrt(jnp.sum(q * q, axis=-1, keepdims=True) + EPS) * (GDN_HD ** -0.5)
            k = k * lax.rsqrt(jnp.sum(k * k, axis=-1, keepdims=True) + EPS)
            bcol = ba_ref[r0:r0 + rv, h:h + 1]
            acol = ba_ref[r0:r0 + rv, GDN_HEADS + h:GDN_HEADS + h + 1] + dtb[:, h:h + 1]
            softplus = jnp.maximum(acol, 0.0) + jnp.log1p(jnp.exp(-jnp.abs(acol)))
            qs.append(padrows(q)); ks.append(padrows(k)); vs.append(padrows(v))
            betas.append(padrows(_sigmoid(bcol)))
            gs.append(padrows(neg_ea[:, h:h + 1] * softplus))
        qm = jnp.concatenate(qs, axis=0)
        km = jnp.concatenate(ks, axis=0)
        vm = jnp.concatenate(vs, axis=0)
        beta = jnp.concatenate(betas, axis=0)
        g = jnp.concatenate(gs, axis=0)
        gc = jnp.dot(causal, jnp.broadcast_to(g, (GDN_STACK, LANE)), preferred_element_type=F32,
                     precision=lax.Precision.HIGHEST)[:, 0:1]
        gc_row = jnp.sum(m_ref[8] * gc, axis=0, keepdims=True)
        diff = gc - gc_row
        decay = jnp.exp(jnp.where(causal > 0, diff, 0.0)) * causal
        kb = km * beta
        kmb = km.astype(BF16)
        a_mat = _dot_t(kb.astype(BF16), kmb) * decay * m_ref[7]
        qk = _dot_t(qm.astype(BF16), kmb) * decay
        tinv = _unit_lower_inverse(a_mat, m_ref)
        egc = jnp.exp(gc)
        sol = _dot3(tinv, jnp.concatenate([vm * beta, kb * egc], axis=1))
        u_all, w_all = sol[:, :GDN_HD], sol[:, GDN_HD:]
        qd = qm * egc
        v_news, o_inter, kds, gls = [], [], [], []
        for h in range(GDN_HEADS):
            rs = slice(h * GDN_CHUNK, (h + 1) * GDN_CHUNK)
            sb = s_scr[h].astype(BF16)
            v_new = u_all[rs] - _dot(w_all[rs].astype(BF16), sb)
            o_inter.append(_dot(qd[rs].astype(BF16), sb))
            gl = gc[h * GDN_CHUNK + GDN_CHUNK - 1:(h + 1) * GDN_CHUNK, :]
            kds.append(km[rs] * jnp.exp(gl - gc[rs]))
            v_news.append(v_new)
            gls.append(gl)
        v_stack = jnp.concatenate(v_news, axis=0).astype(BF16)
        o_intra = _dot(qk.astype(BF16), v_stack)
        for h in range(GDN_HEADS):
            rs = slice(h * GDN_CHUNK, (h + 1) * GDN_CHUNK)
            s_scr[h] = s_scr[h] * jnp.exp(gls[h]) + _dot(kds[h].T.astype(BF16), v_stack[rs])
            o = (o_inter[h] + o_intra[rs])[:rv]
            on = o * lax.rsqrt(jnp.mean(o * o, axis=-1, keepdims=True) + EPS) * nw_ref[...]
            zz = z_ref[r0:r0 + rv, h * GDN_HD:(h + 1) * GDN_HD]
            o_ref[r0:r0 + rv, h * GDN_HD:(h + 1) * GDN_HD] = on * (zz * _sigmoid(zz))

    last = xs_scr[row0 + tt - (GDN_CONV - 1):row0 + tt, :]
    buf_ref[0] = last
    xs_scr[GDN_PREV0:row0, :] = last

    @pl.when(t == nt - 1)
    def _():
        s_ref[0] = s_scr[...]


def gdn(qkv_pre, z_pre, ba_pre, prev, s0, conv_w, a_log, dt_bias, norm_w, n_seq, tt):
    n_rows = qkv_pre.shape[0]
    t_rows = n_rows // n_seq
    tt = min(tt, t_rows)
    nt = t_rows // tt
    par = jnp.zeros((8, LANE), F32).at[0, :GDN_HEADS].set(a_log).at[1, :GDN_HEADS].set(dt_bias)
    masks = jnp.asarray(_gdn_masks())
    const2 = lambda b, i: (0, 0)
    rows = lambda b, i: (b * nt + i, 0)
    return pl.pallas_call(
        functools.partial(_gdn_kernel, tt=tt), grid=(n_seq, nt),
        in_specs=[pl.BlockSpec((tt, GDN_C3), rows), pl.BlockSpec((tt, GDN_W), rows),
                  pl.BlockSpec((tt, LANE), rows),
                  pl.BlockSpec((1, GDN_CONV - 1, GDN_C3), lambda b, i: (b, 0, 0)),
                  pl.BlockSpec((1, GDN_HEADS, GDN_HD, GDN_HD), lambda b, i: (b, 0, 0, 0)),
                  pl.BlockSpec((GDN_CONV, GDN_C3), const2), pl.BlockSpec((8, LANE), const2),
                  pl.BlockSpec((1, GDN_HD), const2),
                  pl.BlockSpec(masks.shape, lambda b, i: (0, 0, 0))],
        out_specs=[pl.BlockSpec((tt, GDN_W), rows),
                   pl.BlockSpec((1, GDN_CONV - 1, GDN_C3), lambda b, i: (b, 0, 0)),
                   pl.BlockSpec((1, GDN_HEADS, GDN_HD, GDN_HD), lambda b, i: (b, 0, 0, 0))],
        out_shape=[jax.ShapeDtypeStruct((n_rows, GDN_W), F32),
                   jax.ShapeDtypeStruct((n_seq, GDN_CONV - 1, GDN_C3), F32),
                   jax.ShapeDtypeStruct((n_seq, GDN_HEADS, GDN_HD, GDN_HD), F32)],
        scratch_shapes=[pltpu.VMEM((8 + tt, GDN_C3), F32), pltpu.VMEM((GDN_HEADS, GDN_HD, GDN_HD), F32)],
        compiler_params=_cparams(("arbitrary", "arbitrary"), 48), name="gdn",
    )(qkv_pre, z_pre, ba_pre, prev, s0, conv_w, par, norm_w.reshape(1, GDN_HD), masks)


CMP_W = 2 * NSA_KV_HEADS * NSA_HD


def _cmp_weights(cmp_pe, cmp_w):
    def bd(w):
        z = jnp.zeros_like(w)
        return jnp.concatenate([jnp.concatenate([w, z], 2), jnp.concatenate([z, w], 2)], 1)
    w = jnp.stack([jnp.stack([bd(cmp_w[s, :CMP_STRIDE]), bd(cmp_w[s, CMP_STRIDE:])]) for s in range(2)]).astype(BF16)
    pe2 = jnp.concatenate([cmp_pe, cmp_pe], axis=-1)
    pe = jnp.stack([pe2[:, :CMP_STRIDE], pe2[:, CMP_STRIDE:]], axis=1)
    return w, pe


def _compress(read_k, read_v, w_ref, pe_ref, n_chunk):
    outs = []
    for s, read in enumerate((read_k, read_v)):
        lo = jnp.zeros((n_chunk, LANE), F32)
        hi = jnp.zeros((n_chunk, LANE), F32)
        for j in range(CMP_STRIDE):
            x = read(j)
            lo = lo + _dot((x + pe_ref[s, 0, j:j + 1, :]).astype(BF16), w_ref[s, 0, j])
            hi = hi + _dot((x + pe_ref[s, 1, j:j + 1, :]).astype(BF16), w_ref[s, 1, j])
        outs.append(lo + pltpu.roll(hi, n_chunk - 1, axis=0))
    return jnp.concatenate(outs, axis=1)


def _cmp_kernel(k_ref, v_ref, w_ref, pe_ref, o_ref, *, n_chunk):
    o_ref[0] = _compress(lambda j: k_ref[0, pl.ds(j, n_chunk, stride=CMP_STRIDE), :],
                         lambda j: v_ref[0, pl.ds(j, n_chunk, stride=CMP_STRIDE), :], w_ref, pe_ref, n_chunk)


def compress_prompt(rows_k, rows_v, w, pe):
    b, t, _ = rows_k.shape
    n_chunk = t // CMP_STRIDE
    xspec = pl.BlockSpec((1, t, LANE), lambda i: (i, 0, 0))
    return pl.pallas_call(
        functools.partial(_cmp_kernel, n_chunk=n_chunk), grid=(b,),
        in_specs=[xspec, xspec, pl.BlockSpec(w.shape, lambda i: (0, 0, 0, 0, 0)),
                  pl.BlockSpec(pe.shape, lambda i: (0, 0, 0, 0))],
        out_specs=pl.BlockSpec((1, n_chunk, CMP_W), lambda i: (i, 0, 0)),
        out_shape=jax.ShapeDtypeStruct((b, n_chunk, CMP_W), F32),
        compiler_params=_cparams(("parallel",), 32), name="nsa_compress",
    )(rows_k, rows_v, w, pe)


def _c2s(n_cmp_pad, n_sel_pad):
    cs = np.arange(n_cmp_pad)[:, None] * CMP_STRIDE
    ss = np.arange(n_sel_pad)[None, :] * SEL_BLOCK
    return ((cs < ss + SEL_BLOCK) & (cs + CMP_BLOCK > ss)).astype(np.float32)


def _expand(n_sel_pad, n_keys):
    s = np.arange(n_sel_pad)[:, None]
    k = np.arange(n_keys)[None, :]
    return (k // SEL_BLOCK == s).astype(np.float32)


def _cmp_probs(s, valid):
    s = jnp.where(valid, s, NEG)
    p = jnp.where(valid, jnp.exp(s - jnp.max(s, axis=-1, keepdims=True)), 0.0)
    return p / jnp.maximum(jnp.sum(p, axis=-1, keepdims=True), 1e-30)


def _select_blocks(imp, q_blk, n_sel):
    lane = lax.broadcasted_iota(jnp.int32, imp.shape, 1)
    visible = lane <= q_blk
    forced = (lane == 0) | (lane == q_blk) | (lane == q_blk - 1)
    score = jnp.where(visible, imp + jnp.where(forced, FORCED_BONUS, 0.0), NEG)
    rank = jnp.zeros(imp.shape, F32)
    for sp in range(n_sel):
        col = score[:, sp:sp + 1]
        beats = (col > score) | ((col == score) & (lane > sp))
        rank = rank + jnp.where(beats, 1.0, 0.0)
    return jnp.where(visible & (rank < SEL_TOPN), 1.0, 0.0)


def _flash(q, k_ref, v_ref, j_lo, j_hi, mask_fn, tq, tk):
    def body(j, carry):
        m, l, acc = carry
        off = pl.multiple_of(j * tk, tk)
        s = _dot_t(q, k_ref[pl.ds(off, tk), :])
        s = jnp.where(mask_fn(j), s, NEG)
        m_new = jnp.maximum(m, jnp.max(s, axis=-1, keepdims=True))
        alpha = jnp.exp(m - m_new)
        p = jnp.exp(s - m_new)
        l = alpha * l + jnp.sum(p, axis=-1, keepdims=True)
        acc = alpha * acc + _dot(p.astype(BF16), v_ref[pl.ds(off, tk), :])
        return m_new, l, acc

    init = (jnp.full((tq, 1), NEG, F32), jnp.zeros((tq, 1), F32), jnp.zeros((tq, NSA_HD), F32))
    _, l, acc = lax.fori_loop(j_lo, j_hi, body, init)
    return acc / l


def _nsa_prompt_kernel(qc_ref, qr_ref, kc_ref, vc_ref, ks_ref, vs_ref, kw_ref, vw_ref, g_ref, c2s_ref, e_ref,
                       o_ref, mask_scr, *, tq, n_sel, n_cmp_pad):
    i = pl.program_id(2)
    t0 = i * tq
    qpos = t0 + lax.broadcasted_iota(jnp.int32, (tq, 1), 0)
    cend = lax.broadcasted_iota(jnp.int32, (1, n_cmp_pad), 1) * CMP_STRIDE + (CMP_BLOCK - 1)
    valid_c = cend <= qpos
    psum = jnp.zeros((tq, n_cmp_pad), F32)
    o_cmp = []
    for g in range(NSA_GROUP):
        p = _cmp_probs(_dot_t(qc_ref[0, g], kc_ref[0, 0]), valid_c)
        psum = psum + p
        o_cmp.append(_dot(p.astype(BF16), vc_ref[0, 0]))
    imp = jnp.dot(psum, c2s_ref[...], preferred_element_type=F32, precision=lax.Precision.HIGHEST)
    sel = _select_blocks(imp, lax.shift_right_logical(qpos, 6), n_sel).astype(BF16)
    n_kt = mask_scr.shape[0]
    for jj in range(n_kt):
        mask_scr[jj] = _dot(sel, e_ref[:, jj * tq:(jj + 1) * tq])
    kiota = lax.broadcasted_iota(jnp.int32, (1, tq), 1)

    def sel_mask(j):
        return (mask_scr[j] > 0.5) & (j * tq + kiota <= qpos)

    def win_mask(j):
        rel = qpos - (j * tq + kiota)
        return (rel >= 0) & (rel < WINDOW)

    w_tiles = WINDOW // tq
    for g in range(NSA_GROUP):
        q = qr_ref[0, g]
        o_sel = _flash(q, ks_ref.at[0, 0], vs_ref.at[0, 0], 0, i + 1, sel_mask, tq, tq)
        o_win = _flash(q, kw_ref.at[0, 0], vw_ref.at[0, 0], jnp.maximum(i - w_tiles, 0), i + 1, win_mask, tq, tq)
        gt = g_ref[0, g]
        o = o_cmp[g] * gt[:, 0:1] + o_sel * gt[:, 1:2] + o_win * gt[:, 2:3]
        o_ref[0, :, g * NSA_HD:(g + 1) * NSA_HD] = o


def nsa_prompt_attn(qc, qr, kcmp, vcmp, ksel, vsel, kwin, vwin, gates, tq):
    b, _, t, _ = qc.shape
    tq = min(tq, t)
    n_cmp_pad = kcmp.shape[2]
    n_sel = t // SEL_BLOCK
    n_sel_pad = -(-n_sel // LANE) * LANE
    c2s = jnp.asarray(_c2s(n_cmp_pad, n_sel_pad))
    e = jnp.asarray(_expand(n_sel_pad, t), dtype=BF16)
    kern = functools.partial(_nsa_prompt_kernel, tq=tq, n_sel=n_sel, n_cmp_pad=n_cmp_pad)
    qspec = pl.BlockSpec((1, NSA_GROUP, tq, NSA_HD), lambda bb, k, i: (bb, k, i, 0))
    kvspec = lambda n: pl.BlockSpec((1, 1, n, NSA_HD), lambda bb, k, i: (bb, k, 0, 0))
    return pl.pallas_call(
        kern, grid=(b, NSA_KV_HEADS, t // tq),
        in_specs=[qspec, qspec, kvspec(n_cmp_pad), kvspec(n_cmp_pad), kvspec(t), kvspec(t), kvspec(t), kvspec(t),
                  pl.BlockSpec((1, NSA_GROUP, tq, 3), lambda bb, k, i: (bb, k, i, 0)),
                  pl.BlockSpec(c2s.shape, lambda bb, k, i: (0, 0)), pl.BlockSpec(e.shape, lambda bb, k, i: (0, 0))],
        out_specs=pl.BlockSpec((1, tq, NSA_GROUP * NSA_HD), lambda bb, k, i: (bb, i, k)),
        out_shape=jax.ShapeDtypeStruct((b, t, NSA_Q_W), F32),
        scratch_shapes=[pltpu.VMEM((t // tq, tq, tq), F32)],
        compiler_params=_cparams(("parallel", "parallel", "parallel"), 40), name="nsa_prompt_attn",
    )(qc, qr, kcmp, vcmp, ksel, vsel, kwin, vwin, gates, c2s, e)


SROWS = NSA_HEADS * 8
HALF_W = 2 * NSA_KV_HEADS * NSA_HD


def _page_copies(cache_ref, pt_ref, targets, sem, b, slot, n_pages, page):
    return [pltpu.make_async_copy(cache_ref.at[pt_ref[b, pg], :, pl.ds(col0, buf.shape[2])],
                                  buf.at[slot, pl.ds(pg * page, page), :], sem.at[slot])
            for pg in range(n_pages) for col0, buf in targets]


def _stream_pages(cache_ref, pt_ref, targets, sem, n_pages, page):
    b = pl.program_id(0)
    nb = pl.num_programs(0)
    slot = lax.rem(b, 2)

    @pl.when(b == 0)
    def _():
        for c in _page_copies(cache_ref, pt_ref, targets, sem, 0, 0, n_pages, page):
            c.start()

    @pl.when(b + 1 < nb)
    def _():
        for c in _page_copies(cache_ref, pt_ref, targets, sem, b + 1, 1 - slot, n_pages, page):
            c.start()

    for c in _page_copies(cache_ref, pt_ref, targets, sem, b, slot, n_pages, page):
        c.wait()
    return slot


def _nsa_s1_kernel(pt_ref, cache_ref, q_ref, w_ref, pe_ref, c2s_ref, ocmp_ref, sel_ref, bufk, bufv, sem,
                   *, n_pages, page, ts, n_sel):
    past = n_pages * page
    slot = _stream_pages(cache_ref, pt_ref, [(0, bufk), (LANE, bufv)], sem, n_pages, page)
    n_chunk = past // CMP_STRIDE
    kv = _compress(lambda j: bufk[slot, pl.ds(j, n_chunk, stride=CMP_STRIDE), :],
                   lambda j: bufv[slot, pl.ds(j, n_chunk, stride=CMP_STRIDE), :], w_ref, pe_ref, n_chunk)
    kc = kv[:, :LANE].astype(BF16)
    vc = kv[:, LANE:].astype(BF16)
    tok = lax.broadcasted_iota(jnp.int32, (SROWS, 1), 0) & (ts - 1)
    qpos = past + tok
    cend = lax.broadcasted_iota(jnp.int32, (1, n_chunk), 1) * CMP_STRIDE + (CMP_BLOCK - 1)
    p = _cmp_probs(_dot_t(q_ref[0], kc), cend <= qpos)
    ocmp_ref[0] = _dot(p.astype(BF16), vc)
    psum = []
    for k in range(NSA_KV_HEADS):
        acc = p[k * NSA_GROUP * ts:k * NSA_GROUP * ts + ts]
        for g in range(1, NSA_GROUP):
            r = (k * NSA_GROUP + g) * ts
            acc = acc + p[r:r + ts]
        psum.append(acc)
    psum = jnp.concatenate(psum, axis=0)
    imp = jnp.dot(psum, c2s_ref[...], preferred_element_type=F32, precision=lax.Precision.HIGHEST)
    q_blk = lax.shift_right_logical(past + (lax.broadcasted_iota(jnp.int32, (NSA_KV_HEADS * ts, 1), 0) & (ts - 1)), 6)
    sel_ref[0] = _select_blocks(imp, q_blk, n_sel)


def _nsa_s2_kernel(pt_ref, cache_ref, q_ref, sel_ref, new_ref, win_ref, wnew_ref, ocmp_ref, g_ref, e_ref,
                   o_ref, buf, sem, *, n_pages, page, ts, wb):
    past = n_pages * page
    slot = _stream_pages(cache_ref, pt_ref, [(HALF_W, buf)], sem, n_pages, page)
    q = q_ref[0]
    tok = lax.broadcasted_iota(jnp.int32, (SROWS, 1), 0) & (ts - 1)
    sel = sel_ref[0]
    sel_rows = jnp.concatenate([sel[k * ts:(k + 1) * ts] for k in range(NSA_KV_HEADS) for _ in range(NSA_GROUP)],
                               axis=0)
    n_new = new_ref.shape[1]
    jnew = lax.broadcasted_iota(jnp.int32, (1, n_new), 1)

    def attend(s_past, valid_past, v_past, s_new, valid_new, v_new):
        s_past = jnp.where(valid_past, s_past, NEG)
        s_new = jnp.where(valid_new, s_new, NEG)
        m = jnp.maximum(jnp.max(s_past, axis=-1, keepdims=True), jnp.max(s_new, axis=-1, keepdims=True))
        p_past = jnp.exp(s_past - m)
        p_new = jnp.exp(s_new - m)
        l = jnp.sum(p_past, axis=-1, keepdims=True) + jnp.sum(p_new, axis=-1, keepdims=True)
        return (_dot(p_past.astype(BF16), v_past) + _dot(p_new.astype(BF16), v_new)) / l

    k_past = buf[slot, :, 0:LANE].astype(BF16)
    v_past = buf[slot, :, LANE:HALF_W].astype(BF16)
    mask_past = _dot(sel_rows.astype(BF16), e_ref[...]) > 0.5
    blk_new = past // SEL_BLOCK
    valid_new = (sel_rows[:, blk_new:blk_new + 1] > 0.5) & (jnew <= tok)
    k_new = new_ref[0, :, 0:LANE].astype(BF16)
    v_new = new_ref[0, :, LANE:HALF_W].astype(BF16)
    o_sel = attend(_dot_t(q, k_past), mask_past, v_past, _dot_t(q, k_new), valid_new, v_new)
    u = lax.broadcasted_iota(jnp.int32, (1, wb), 1)
    rel = wb + tok - u
    valid_w = (rel >= 0) & (rel < WINDOW) & (past - wb + u >= 0)
    kw = win_ref[0, :, 0:LANE].astype(BF16)
    vw = win_ref[0, :, LANE:HALF_W].astype(BF16)
    kwn = wnew_ref[0, :, 0:LANE].astype(BF16)
    vwn = wnew_ref[0, :, LANE:HALF_W].astype(BF16)
    o_win = attend(_dot_t(q, kw), valid_w, vw, _dot_t(q, kwn), jnew <= tok, vwn)
    gt = g_ref[0]
    o = ocmp_ref[0] * gt[:, 0:1] + o_sel * gt[:, 1:2] + o_win * gt[:, 2:3]
    half = SROWS // NSA_KV_HEADS
    for k in range(NSA_KV_HEADS):
        o_ref[0, k * half:(k + 1) * half, :] = o[k * half:(k + 1) * half, k * NSA_HD:(k + 1) * NSA_HD]


def nsa_sample_attn(page_table, cache, qc2, qr2, new_sel, win, new_win, gates, cmp_w, cmp_pe, ts):
    bs, n_pages = page_table.shape
    page = cache.shape[1]
    past = n_pages * page
    wb = win.shape[1]
    assert past % SEL_BLOCK == 0 and ts <= CMP_STRIDE and ts & (ts - 1) == 0 and SROWS == NSA_HEADS * ts
    n_chunk = past // CMP_STRIDE
    n_sel = past // SEL_BLOCK + 1
    n_sel_pad = -(-n_sel // LANE) * LANE
    c2s = jnp.asarray(_c2s(n_chunk, n_sel_pad))
    e = jnp.asarray(_expand(n_sel_pad, past), dtype=BF16)
    sems = pltpu.SemaphoreType.DMA((2,))
    scratch1 = [pltpu.VMEM((2, past, LANE), F32), pltpu.VMEM((2, past, LANE), F32), sems]
    scratch2 = [pltpu.VMEM((2, past, HALF_W), F32), sems]
    seq3 = lambda b, pt: (b, 0, 0)
    o_cmp, sel = pl.pallas_call(
        functools.partial(_nsa_s1_kernel, n_pages=n_pages, page=page, ts=ts, n_sel=n_sel),
        grid_spec=pltpu.PrefetchScalarGridSpec(
            num_scalar_prefetch=1, grid=(bs,),
            in_specs=[pl.BlockSpec(memory_space=pl.ANY),
                      pl.BlockSpec((1, SROWS, LANE), seq3),
                      pl.BlockSpec(cmp_w.shape, lambda b, pt: (0, 0, 0, 0, 0)),
                      pl.BlockSpec(cmp_pe.shape, lambda b, pt: (0, 0, 0, 0)),
                      pl.BlockSpec(c2s.shape, lambda b, pt: (0, 0))],
            out_specs=[pl.BlockSpec((1, SROWS, LANE), seq3),
                       pl.BlockSpec((1, NSA_KV_HEADS * ts, n_sel_pad), seq3)],
            scratch_shapes=scratch1),
        out_shape=[jax.ShapeDtypeStruct((bs, SROWS, LANE), F32),
                   jax.ShapeDtypeStruct((bs, NSA_KV_HEADS * ts, n_sel_pad), F32)],
        compiler_params=_cparams(("arbitrary",), 48), name="nsa_sample_cmp",
    )(page_table, cache, qc2, cmp_w, cmp_pe, c2s)
    n_new = new_sel.shape[1]
    return pl.pallas_call(
        functools.partial(_nsa_s2_kernel, n_pages=n_pages, page=page, ts=ts, wb=wb),
        grid_spec=pltpu.PrefetchScalarGridSpec(
            num_scalar_prefetch=1, grid=(bs,),
            in_specs=[pl.BlockSpec(memory_space=pl.ANY),
                      pl.BlockSpec((1, SROWS, LANE), seq3),
                      pl.BlockSpec((1, NSA_KV_HEADS * ts, n_sel_pad), seq3),
                      pl.BlockSpec((1, n_new, HALF_W), seq3),
                      pl.BlockSpec((1, wb, HALF_W), seq3),
                      pl.BlockSpec((1, n_new, HALF_W), seq3),
                      pl.BlockSpec((1, SROWS, LANE), seq3),
                      pl.BlockSpec((1, SROWS, 3), seq3),
                      pl.BlockSpec(e.shape, lambda b, pt: (0, 0))],
            out_specs=pl.BlockSpec((1, SROWS, NSA_HD), seq3),
            scratch_shapes=scratch2),
        out_shape=jax.ShapeDtypeStruct((bs, SROWS, NSA_HD), F32),
        compiler_params=_cparams(("arbitrary",), 56), name="nsa_sample_sel",
    )(page_table, cache, qr2, sel, new_sel, win, new_win, o_cmp, gates, e)


def _rms_heads(x, w):
    return x * lax.rsqrt(jnp.mean(x * x, axis=-1, keepdims=True) + EPS) * w


def _rope(x, pos):
    half = x.shape[-1] // 2
    inv = jnp.power(ROPE_THETA, -jnp.arange(half, dtype=F32) / half)
    ang = pos.astype(F32)[:, None] * inv[None, :]
    cos = jnp.cos(ang)[None, :, None, :]
    sin = jnp.sin(ang)[None, :, None, :]
    x1, x2 = x[..., :half], x[..., half:]
    return jnp.concatenate([x1 * cos - x2 * sin, x1 * sin + x2 * cos], axis=-1)


def _nsa_head_prep(q_pre, kv_pre, gate_pre, pos, nb, t, qn, kn):
    q = _rms_heads(q_pre.reshape(nb, t, NSA_HEADS, NSA_HD), qn) * (NSA_HD ** -0.5)
    qr = _rope(q, pos)
    kv = kv_pre.reshape(nb, t, 3, 2, NSA_KV_HEADS, NSA_HD)
    k = _rms_heads(kv[:, :, :, 0], kn[:, None, :])
    v = kv[:, :, :, 1]
    rows = jnp.stack([k[:, :, 0], v[:, :, 0], _rope(k[:, :, 1], pos), v[:, :, 1]], axis=2)
    wrows = jnp.stack([_rope(k[:, :, 2], pos), v[:, :, 2]], axis=2)
    gates = _sigmoid(gate_pre[:, :NSA_GATE_W]).reshape(nb, t, NSA_HEADS, 3)
    return q, qr, rows, wrows, gates


def _in_proj(x2d, p, tm):
    return rms_matmul(x2d, p["attn_norm_w"], p["w_in_parts"], tm)


def _post_mixer(x2d, o_nsa, o_gdn, mem_kv, p, n_seq, tq):
    h = matmul_res(jnp.concatenate([o_nsa, o_gdn], axis=-1), p["w_out"], x2d, 512)
    (qx,) = rms_matmul(h, p["xattn_norm_w"], [p["w_xq"]], 512)
    return matmul_res(xattn_core(qx, mem_kv, p["xq_norm_w"], n_seq, tq), p["w_xo"], h, 512)


def _ffn(h, p, prev, n_seq, shift, tm):
    return conv_ffn(h, p["ffn_norm_w"], p["w_up"], p["ffn_conv_w"], p["ffn_conv_b"], p["w_down"], prev,
                    n_seq, shift, tm)


def _head_major(x):
    return x.transpose(0, 2, 1, 3).astype(BF16)


def _layer_prompt(x, mem, p):
    nb, t, d = x.shape
    x2d = x.reshape(nb * t, d)
    q_pre, kv_pre, gate_pre, qkv_pre, z_pre, ba_pre = _in_proj(x2d, p, 512)
    q, qr, rows, wrows, gates = _nsa_head_prep(q_pre, kv_pre, gate_pre, jnp.arange(t), nb, t,
                                               p["nsa_q_norm_w"], p["nsa_k_norm_w"])
    kvc = compress_prompt(rows[:, :, 0].reshape(nb, t, LANE), rows[:, :, 1].reshape(nb, t, LANE),
                          p["cmp_bd"], p["cmp_pe2"])
    n_c = kvc.shape[1]
    kvc = kvc.reshape(nb, n_c, 2, NSA_KV_HEADS, NSA_HD)
    o_nsa = nsa_prompt_attn(_head_major(q), _head_major(qr), _head_major(kvc[:, :, 0]), _head_major(kvc[:, :, 1]),
                            _head_major(rows[:, :, 2]), _head_major(rows[:, :, 3]),
                            _head_major(wrows[:, :, 0]), _head_major(wrows[:, :, 1]),
                            gates.transpose(0, 2, 1, 3), 256)
    o_gdn, gdn_buf, gdn_s = gdn(qkv_pre, z_pre, ba_pre, jnp.zeros((nb, GDN_CONV - 1, GDN_C3), F32),
                                jnp.zeros((nb, GDN_HEADS, GDN_HD, GDN_HD), F32), p["gdn_conv_w"], p["gdn_a_log"],
                                p["gdn_dt_bias"], p["gdn_norm_w"], nb, 256)
    mem_kv = mem_kv_proj(mem.reshape(-1, d), p["mem_norm_w"], p["w_xk"], p["w_xv"], p["xk_norm_w"])
    mem_kv = mem_kv.reshape(nb, mem.shape[1], 2 * XA_HEADS * XA_HD)
    h = _post_mixer(x2d, o_nsa.reshape(nb * t, NSA_Q_W), o_gdn, mem_kv, p, nb, 256)
    y, ffn_buf = _ffn(h, p, jnp.zeros((nb, FFN_CONV - 1, 2 * D_FF), F32), nb, 1, 256)
    keep = min(WINDOW, t)
    return (y.reshape(nb, t, d), rows, wrows[:, t - keep:], gdn_s, gdn_buf, ffn_buf,
            mem_kv.reshape(nb, mem.shape[1], 2, XA_HEADS, XA_HD))


def _layer_sample(x, cache_kv, cache_win, state, gdn_cache, ffn_cache, mem_kv, page_table, p):
    bs, ts, d = x.shape
    n_pool, page = cache_kv.shape[:2]
    past = page_table.shape[1] * page
    x2d = x.reshape(bs * ts, d)
    q_pre, kv_pre, gate_pre, qkv_pre, z_pre, ba_pre = _in_proj(x2d, p, 256)
    q, qr, rows, wrows, gates = _nsa_head_prep(q_pre, kv_pre, gate_pre, past + jnp.arange(ts), bs, ts,
                                               p["nsa_q_norm_w"], p["nsa_k_norm_w"])

    def stack_q(a):
        a = a.transpose(0, 2, 1, 3).reshape(bs, NSA_KV_HEADS, NSA_GROUP * ts, NSA_HD)
        z = jnp.zeros_like(a[:, 0])
        return jnp.concatenate([jnp.concatenate([a[:, 0], z], -1), jnp.concatenate([z, a[:, 1]], -1)], 1).astype(BF16)

    def pad_new(a):
        return jnp.pad(a.reshape(bs, ts, HALF_W), ((0, 0), (0, LANE - ts), (0, 0)))

    wb = cache_win.shape[1]
    o2 = nsa_sample_attn(page_table, cache_kv.reshape(n_pool, page, 2 * HALF_W), stack_q(q), stack_q(qr),
                         pad_new(rows[:, :, 2:4]), cache_win.reshape(bs, wb, HALF_W), pad_new(wrows),
                         gates.transpose(0, 2, 1, 3).reshape(bs, SROWS, 3), p["cmp_bd"], p["cmp_pe2"], ts)
    o_nsa = o2.reshape(bs, NSA_HEADS, ts, NSA_HD).transpose(0, 2, 1, 3).reshape(bs * ts, NSA_Q_W)
    o_gdn, gdn_buf, gdn_s = gdn(qkv_pre, z_pre, ba_pre, gdn_cache, state, p["gdn_conv_w"], p["gdn_a_log"],
                                p["gdn_dt_bias"], p["gdn_norm_w"], bs, ts)
    h = _post_mixer(x2d, o_nsa, o_gdn, mem_kv.reshape(bs, mem_kv.shape[1], -1), p, bs, ts)
    h_tm = h.reshape(bs, ts, d).transpose(1, 0, 2).reshape(ts * bs, d)
    prev = ffn_cache.transpose(1, 0, 2).reshape(1, (FFN_CONV - 1) * bs, 2 * D_FF)
    y, ffn_buf = _ffn(h_tm, p, prev, 1, bs, ts * bs)
    y = y.reshape(ts, bs, d).transpose(1, 0, 2)
    ffn_buf = ffn_buf.reshape(FFN_CONV - 1, bs, 2 * D_FF).transpose(1, 0, 2)
    wall = jnp.concatenate([cache_win, wrows], axis=1)
    keep = min(WINDOW, wb + ts)
    return y, rows, wall[:, wb + ts - keep:], gdn_s, gdn_buf, ffn_buf


def _prep_params(l, attn_norm_w, w_in, nsa_q_norm_w, nsa_k_norm_w, cmp_pe, cmp_w, gdn_conv_w, gdn_a_log,
                 gdn_dt_bias, gdn_norm_w, w_out, mem_norm_w, w_xk, w_xv, xk_norm_w, xattn_norm_w, w_xq,
                 xq_norm_w, w_xo, ffn_norm_w, w_up, ffn_conv_w, ffn_conv_b, w_down):
    wi = w_in[l]
    cuts = np.cumsum([0, NSA_Q_W, NSA_KV_W, NSA_GATE_W, 3 * GDN_W, GDN_W, 2 * GDN_HEADS])
    parts = []
    for a, b in zip(cuts[:-1], cuts[1:]):
        w = wi[:, a:b]
        padn = -(-(b - a) // LANE) * LANE - (b - a)
        parts.append(jnp.pad(w, ((0, 0), (0, padn))).astype(BF16))
    cmp_bd, cmp_pe2 = _cmp_weights(cmp_pe[l], cmp_w[l])
    bf = lambda w: w[l].astype(BF16)
    return dict(attn_norm_w=attn_norm_w[l], w_in_parts=parts, nsa_q_norm_w=nsa_q_norm_w[l],
                nsa_k_norm_w=nsa_k_norm_w[l], cmp_bd=cmp_bd, cmp_pe2=cmp_pe2, gdn_conv_w=gdn_conv_w[l],
                gdn_a_log=gdn_a_log[l], gdn_dt_bias=gdn_dt_bias[l], gdn_norm_w=gdn_norm_w[l], w_out=bf(w_out),
                mem_norm_w=mem_norm_w[l], w_xk=bf(w_xk), w_xv=bf(w_xv), xk_norm_w=xk_norm_w[l],
                xattn_norm_w=xattn_norm_w[l], w_xq=bf(w_xq), xq_norm_w=xq_norm_w[l], w_xo=bf(w_xo),
                ffn_norm_w=ffn_norm_w[l], w_up=bf(w_up), ffn_conv_w=ffn_conv_w[l], ffn_conv_b=ffn_conv_b[l],
                w_down=bf(w_down))


def kernel(x_prompt, x_sample, mem_prompt, cache_nsa_kv, cache_nsa_win, state_gdn, cache_gdn_conv, cache_ffn_conv,
           cache_mem_kv, page_table, attn_norm_w, w_in, nsa_q_norm_w, nsa_k_norm_w, cmp_pe, cmp_w, gdn_conv_w,
           gdn_a_log, gdn_dt_bias, gdn_norm_w, w_out, mem_norm_w, w_xk, w_xv, xk_norm_w, xattn_norm_w, w_xq,
           xq_norm_w, w_xo, ffn_norm_w, w_up, ffn_conv_w, ffn_conv_b, w_down):
    weights = (attn_norm_w, w_in, nsa_q_norm_w, nsa_k_norm_w, cmp_pe, cmp_w, gdn_conv_w, gdn_a_log, gdn_dt_bias,
               gdn_norm_w, w_out, mem_norm_w, w_xk, w_xv, xk_norm_w, xattn_norm_w, w_xq, xq_norm_w, w_xo,
               ffn_norm_w, w_up, ffn_conv_w, ffn_conv_b, w_down)
    depth = cache_nsa_kv.shape[0]
    hp, hs = x_prompt, x_sample
    outs_p, outs_s = [], []
    for l in range(depth):
        p = _prep_params(l, *weights)
        res_p = _layer_prompt(hp, mem_prompt, p)
        hp = res_p[0]
        outs_p.append(res_p[1:])
        res_s = _layer_sample(hs, cache_nsa_kv[l], cache_nsa_win[l], state_gdn[l], cache_gdn_conv[l],
                              cache_ffn_conv[l], cache_mem_kv[l], page_table, p)
        hs = res_s[0]
        outs_s.append(res_s[1:])
    stack = lambda outs, i: jnp.stack([o[i] for o in outs])
    return ((hp, hs) + tuple(stack(outs_p, i) for i in range(6)) + tuple(stack(outs_s, i) for i in range(5)))
```

```python
import functools
import math

import numpy as np
import jax
import jax.numpy as jnp
from jax import lax
from jax.experimental import pallas as pl
from jax.experimental.pallas import tpu as pltpu

F32 = jnp.float32
BF16 = jnp.bfloat16

D_MODEL = 1024
NSA_HEADS = 8
NSA_KV_HEADS = 2
NSA_GROUP = NSA_HEADS // NSA_KV_HEADS
NSA_HD = 64
CMP_BLOCK = 32
CMP_STRIDE = 16
SEL_BLOCK = 64
SEL_TOPN = 8
WINDOW = 512
FORCED_BONUS = 1e4
GDN_HEADS = 4
GDN_HD = 128
GDN_CONV = 4
GDN_CHUNK = 64
GDN_W = GDN_HEADS * GDN_HD
XA_HEADS = 4
XA_HD = 256
D_FF = 2816
FFN_CONV = 3
ROPE_THETA = 10000.0
EPS = 1e-6
NEG = -1e30

NSA_Q_W = NSA_HEADS * NSA_HD
NSA_KV_W = 3 * 2 * NSA_KV_HEADS * NSA_HD
NSA_GATE_W = 3 * NSA_HEADS
LANE = 128
MIB = 1 << 20


def _cparams(sem, vmem_mib):
    return pltpu.CompilerParams(dimension_semantics=sem, vmem_limit_bytes=vmem_mib * MIB)


def _sigmoid(x):
    return 1.0 / (1.0 + jnp.exp(-x))


def _dot(a, b):
    return jnp.dot(a, b, preferred_element_type=F32)


def _dot_t(a, b):
    return lax.dot_general(a, b, (((1,), (1,)), ((), ())), preferred_element_type=F32)


def _split_bf16(a):
    hi = a.astype(BF16)
    lo = (a - hi.astype(F32)).astype(BF16)
    return hi, lo


def _dot3(a, b):
    ah, al = _split_bf16(a)
    bh, bl = _split_bf16(b)
    return _dot(ah, bh) + (_dot(ah, bl) + _dot(al, bh))


def _col_chunk(n):
    for c in (512, 256, 128):
        if n % c == 0:
            return c
    return n


def _rms_mm_kernel(x_ref, g_ref, *refs, n_out):
    w_refs, o_refs = refs[:n_out], refs[n_out:]
    x = x_ref[...]
    xn = (x * lax.rsqrt(jnp.mean(x * x, axis=-1, keepdims=True) + EPS) * g_ref[...]).astype(BF16)
    for w_ref, o_ref in zip(w_refs, o_refs):
        n = w_ref.shape[1]
        ch = _col_chunk(n)
        for c in range(0, n, ch):
            o_ref[:, c:c + ch] = _dot(xn, w_ref[:, c:c + ch])


def rms_matmul(x, g, ws, tm):
    n_rows, d = x.shape
    tm = min(tm, n_rows)
    in_specs = [pl.BlockSpec((tm, d), lambda i: (i, 0)), pl.BlockSpec((1, d), lambda i: (0, 0))]
    in_specs += [pl.BlockSpec(w.shape, lambda i: (0, 0)) for w in ws]
    out_specs = [pl.BlockSpec((tm, w.shape[1]), lambda i: (i, 0)) for w in ws]
    out_shape = [jax.ShapeDtypeStruct((n_rows, w.shape[1]), F32) for w in ws]
    return pl.pallas_call(
        functools.partial(_rms_mm_kernel, n_out=len(ws)),
        grid=(n_rows // tm,), in_specs=in_specs, out_specs=out_specs, out_shape=out_shape,
        compiler_params=_cparams(("parallel",), 56), name="rms_matmul",
    )(x, g.reshape(1, d), *ws)


def _mm_res_kernel(*refs, n_in):
    a_refs, (w_ref, r_ref, o_ref) = refs[:n_in], refs[n_in:]
    a = [a_ref[...].astype(BF16) for a_ref in a_refs]
    n = w_ref.shape[1]
    ch = _col_chunk(n)
    for c in range(0, n, ch):
        acc = r_ref[:, c:c + ch]
        k0 = 0
        for x in a:
            acc = acc + _dot(x, w_ref[k0:k0 + x.shape[1], c:c + ch])
            k0 += x.shape[1]
        o_ref[:, c:c + ch] = acc


def matmul_res(a_list, w, res, tm):
    n_rows = res.shape[0]
    k, n = w.shape
    assert sum(a.shape[1] for a in a_list) == k
    tm = min(tm, n_rows)
    return pl.pallas_call(
        functools.partial(_mm_res_kernel, n_in=len(a_list)), grid=(n_rows // tm,),
        in_specs=[pl.BlockSpec((tm, a.shape[1]), lambda i: (i, 0)) for a in a_list]
        + [pl.BlockSpec((k, n), lambda i: (0, 0)), pl.BlockSpec((tm, n), lambda i: (i, 0))],
        out_specs=pl.BlockSpec((tm, n), lambda i: (i, 0)),
        out_shape=jax.ShapeDtypeStruct((n_rows, n), F32),
        compiler_params=_cparams(("parallel",), 40), name="matmul_res",
    )(*a_list, w, res)


def _memkv_kernel(x_ref, g_ref, wk_ref, wv_ref, kn_ref, o_ref):
    x = x_ref[...]
    xn = (x * lax.rsqrt(jnp.mean(x * x, axis=-1, keepdims=True) + EPS) * g_ref[...]).astype(BF16)
    xa_w = XA_HEADS * XA_HD
    for h in range(XA_HEADS):
        sl = slice(h * XA_HD, (h + 1) * XA_HD)
        k = _dot(xn, wk_ref[:, sl])
        k = k * lax.rsqrt(jnp.mean(k * k, axis=-1, keepdims=True) + EPS) * kn_ref[...]
        o_ref[:, sl] = k
        o_ref[:, xa_w + h * XA_HD: xa_w + (h + 1) * XA_HD] = _dot(xn, wv_ref[:, sl])


def mem_kv_proj(mem, g, wk, wv, kn, tm=256):
    n_rows, d = mem.shape
    xa_w = XA_HEADS * XA_HD
    return pl.pallas_call(
        _memkv_kernel, grid=(n_rows // tm,),
        in_specs=[pl.BlockSpec((tm, d), lambda i: (i, 0)), pl.BlockSpec((1, d), lambda i: (0, 0)),
                  pl.BlockSpec((d, xa_w), lambda i: (0, 0)), pl.BlockSpec((d, xa_w), lambda i: (0, 0)),
                  pl.BlockSpec((1, XA_HD), lambda i: (0, 0))],
        out_specs=pl.BlockSpec((tm, 2 * xa_w), lambda i: (i, 0)),
        out_shape=jax.ShapeDtypeStruct((n_rows, 2 * xa_w), F32),
        compiler_params=_cparams(("parallel",), 40), name="mem_kv_proj",
    )(mem, g.reshape(1, d), wk, wv, kn.reshape(1, XA_HD))


def _xattn_kernel(q_ref, kv_ref, qn_ref, o_ref):
    xa_w = XA_HEADS * XA_HD
    for h in range(XA_HEADS):
        sl = slice(h * XA_HD, (h + 1) * XA_HD)
        q = q_ref[:, sl]
        q = q * lax.rsqrt(jnp.mean(q * q, axis=-1, keepdims=True) + EPS) * qn_ref[...]
        qb = (q * (XA_HD ** -0.5)).astype(BF16)
        k = kv_ref[0, :, sl].astype(BF16)
        v = kv_ref[0, :, xa_w + h * XA_HD: xa_w + (h + 1) * XA_HD].astype(BF16)
        s = _dot_t(qb, k)
        p = jnp.exp(s - jnp.max(s, axis=-1, keepdims=True))
        o = _dot(p.astype(BF16), v) / jnp.sum(p, axis=-1, keepdims=True)
        o_ref[:, sl] = o


def xattn_core(q_pre, kv, qn, n_batch, tq):
    n_rows, xa_w = q_pre.shape
    t = n_rows // n_batch
    tq = min(tq, t)
    nt = t // tq
    m = kv.shape[1]
    return pl.pallas_call(
        _xattn_kernel, grid=(n_batch, nt),
        in_specs=[pl.BlockSpec((tq, xa_w), lambda b, i: (b * nt + i, 0)),
                  pl.BlockSpec((1, m, 2 * xa_w), lambda b, i: (b, 0, 0)),
                  pl.BlockSpec((1, XA_HD), lambda b, i: (0, 0))],
        out_specs=pl.BlockSpec((tq, xa_w), lambda b, i: (b * nt + i, 0)),
        out_shape=jax.ShapeDtypeStruct((n_rows, xa_w), F32),
        compiler_params=_cparams(("parallel", "parallel"), 40), name="xattn_core",
    )(q_pre, kv, qn.reshape(1, XA_HD))


FFN_ACT_CHUNK = 256


def _ffn_kernel(h_ref, g_ref, wup_ref, cw_ref, cb_ref, wdn_ref, prev_ref, o_ref, buf_ref, xs_scr,
                *, tm, shift, base):
    t = pl.program_id(1)
    p0 = base - 2 * shift

    @pl.when(t == 0)
    def _():
        xs_scr[p0:base, :] = prev_ref[0]

    x = h_ref[...]
    xn = (x * lax.rsqrt(jnp.mean(x * x, axis=-1, keepdims=True) + EPS) * g_ref[...]).astype(BF16)
    acc = jnp.zeros((tm, D_MODEL), F32)
    for j in range(D_FF // FFN_ACT_CHUNK):
        halves = []
        for c0 in (j * FFN_ACT_CHUNK, D_FF + j * FFN_ACT_CHUNK):
            sl = slice(c0, c0 + FFN_ACT_CHUNK)
            xs_scr[base:base + tm, sl] = _dot(xn, wup_ref[:, sl])
            y = (cw_ref[0:1, sl] * xs_scr[p0:p0 + tm, sl]
                 + cw_ref[1:2, sl] * xs_scr[p0 + shift:p0 + shift + tm, sl]
                 + cw_ref[2:3, sl] * xs_scr[base:base + tm, sl])
            halves.append(y + cb_ref[:, sl])
        a, u = halves
        act = (a * _sigmoid(a) * u).astype(BF16)
        acc = acc + _dot(act, wdn_ref[j * FFN_ACT_CHUNK:(j + 1) * FFN_ACT_CHUNK, :])
    o_ref[...] = x + acc
    last = xs_scr[base + tm - 2 * shift: base + tm, :]
    buf_ref[0] = last
    xs_scr[p0:base, :] = last


def conv_ffn(h, g, wup, cw, cb, wdn, prev, n_seq, shift, tm):
    n_rows, d = h.shape
    t_rows = n_rows // n_seq
    tm = min(tm, t_rows)
    nt = t_rows // tm
    base = -(-2 * shift // 8) * 8
    kern = functools.partial(_ffn_kernel, tm=tm, shift=shift, base=base)
    const = lambda b, i: (0, 0)
    return pl.pallas_call(
        kern, grid=(n_seq, nt),
        in_specs=[pl.BlockSpec((tm, d), lambda b, i: (b * nt + i, 0)),
                  pl.BlockSpec((1, d), const),
                  pl.BlockSpec((d, 2 * D_FF), const),
                  pl.BlockSpec((FFN_CONV, 2 * D_FF), const),
                  pl.BlockSpec((1, 2 * D_FF), const),
                  pl.BlockSpec((D_FF, d), const),
                  pl.BlockSpec((1, 2 * shift, 2 * D_FF), lambda b, i: (b, 0, 0))],
        out_specs=[pl.BlockSpec((tm, d), lambda b, i: (b * nt + i, 0)),
                   pl.BlockSpec((1, 2 * shift, 2 * D_FF), lambda b, i: (b, 0, 0))],
        out_shape=[jax.ShapeDtypeStruct((n_rows, d), F32),
                   jax.ShapeDtypeStruct((n_seq, 2 * shift, 2 * D_FF), F32)],
        scratch_shapes=[pltpu.VMEM((base + tm, 2 * D_FF), F32)],
        compiler_params=_cparams(("arbitrary", "arbitrary"), 56), name="conv_ffn",
    )(h, g.reshape(1, d), wup, cw, cb.reshape(1, 2 * D_FF), wdn, prev)


GDN_STACK = GDN_HEADS * GDN_CHUNK
GDN_C3 = 3 * GDN_W
GDN_PREV0 = 8 - (GDN_CONV - 1)


def _gdn_masks():
    r = np.arange(GDN_STACK)[:, None]
    c = np.arange(GDN_STACK)[None, :]
    same = lambda n: (r // n) == (c // n)
    m = [same(2) & (c < r)]
    for n in (4, 8, 16, 32, 64):
        m.append(same(n) & ~same(n // 2) & (c < r))
    m.append(same(GDN_CHUNK) & (c <= r))
    m.append(same(GDN_CHUNK) & (c < r))
    m.append(r == c)
    return np.stack(m).astype(np.float32)


def _unit_lower_inverse(a, m_ref):
    t = m_ref[8] - a * m_ref[0]
    for lvl in range(1, 6):
        off = a * m_ref[lvl]
        t = t - _dot3(t, _dot3(off, t))
    return t


def _gdn_kernel_v1(qkv_ref, z_ref, ba_ref, prev_ref, s0_ref, cw_ref, par_ref, nw_ref, m_ref,
                   o_ref, buf_ref, s_ref, xs_scr, s_scr, *, tt):
    t = pl.program_id(1)
    nt = pl.num_programs(1)
    row0 = 8

    @pl.when(t == 0)
    def _():
        xs_scr[GDN_PREV0:row0, :] = prev_ref[0]
        s_scr[...] = s0_ref[0]

    xs_scr[row0:row0 + tt, :] = qkv_ref[...]
    causal = m_ref[6]
    neg_ea = -jnp.exp(par_ref[0:1, :])
    dtb = par_ref[1:2, :]
    rv = min(GDN_CHUNK, tt)
    pad = GDN_CHUNK - rv

    def padrows(a):
        if pad == 0:
            return a
        return jnp.concatenate([a, jnp.zeros((pad, a.shape[1]), a.dtype)], axis=0)

    for n in range(-(-tt // GDN_CHUNK)):
        r0 = n * GDN_CHUNK
        y = cw_ref[0:1, :] * xs_scr[GDN_PREV0 + r0:GDN_PREV0 + r0 + rv, :]
        for i in range(1, GDN_CONV):
            y = y + cw_ref[i:i + 1, :] * xs_scr[GDN_PREV0 + i + r0:GDN_PREV0 + i + r0 + rv, :]
        y = y * _sigmoid(y)
        qs, ks, vs, betas, gs = [], [], [], [], []
        for h in range(GDN_HEADS):
            q = y[:, h * GDN_HD:(h + 1) * GDN_HD]
            k = y[:, GDN_W + h * GDN_HD:GDN_W + (h + 1) * GDN_HD]
            v = y[:, 2 * GDN_W + h * GDN_HD:2 * GDN_W + (h + 1) * GDN_HD]
            q = q * lax.rsqrt(jnp.sum(q * q, axis=-1, keepdims=True) + EPS) * (GDN_HD ** -0.5)
            k = k * lax.rsqrt(jnp.sum(k * k, axis=-1, keepdims=True) + EPS)
            bcol = ba_ref[r0:r0 + rv, h:h + 1]
            acol = ba_ref[r0:r0 + rv, GDN_HEADS + h:GDN_HEADS + h + 1] + dtb[:, h:h + 1]
            softplus = jnp.maximum(acol, 0.0) + jnp.log1p(jnp.exp(-jnp.abs(acol)))
            qs.append(padrows(q)); ks.append(padrows(k)); vs.append(padrows(v))
            betas.append(padrows(_sigmoid(bcol)))
            gs.append(padrows(neg_ea[:, h:h + 1] * softplus))
        qm = jnp.concatenate(qs, axis=0)
        km = jnp.concatenate(ks, axis=0)
        vm = jnp.concatenate(vs, axis=0)
        beta = jnp.concatenate(betas, axis=0)
        g = jnp.concatenate(gs, axis=0)
        gc = jnp.dot(causal, jnp.broadcast_to(g, (GDN_STACK, LANE)), preferred_element_type=F32,
                     precision=lax.Precision.HIGHEST)[:, 0:1]
        gc_row = jnp.sum(m_ref[8] * gc, axis=0, keepdims=True)
        diff = gc - gc_row
        decay = jnp.exp(jnp.where(causal > 0, diff, 0.0)) * causal
        kb = km * beta
        kmb = km.astype(BF16)
        a_mat = _dot_t(kb.astype(BF16), kmb) * decay * m_ref[7]
        qk = _dot_t(qm.astype(BF16), kmb) * decay
        tinv = _unit_lower_inverse(a_mat, m_ref)
        egc = jnp.exp(gc)
        sol = _dot3(tinv, jnp.concatenate([vm * beta, kb * egc], axis=1))
        u_all, w_all = sol[:, :GDN_HD], sol[:, GDN_HD:]
        qd = qm * egc
        v_news, o_inter, kds, gls = [], [], [], []
        for h in range(GDN_HEADS):
            rs = slice(h * GDN_CHUNK, (h + 1) * GDN_CHUNK)
            sb = s_scr[h].astype(BF16)
            v_new = u_all[rs] - _dot(w_all[rs].astype(BF16), sb)
            o_inter.append(_dot(qd[rs].astype(BF16), sb))
            gl = gc[h * GDN_CHUNK + GDN_CHUNK - 1:(h + 1) * GDN_CHUNK, :]
            kds.append(km[rs] * jnp.exp(gl - gc[rs]))
            v_news.append(v_new)
            gls.append(gl)
        v_stack = jnp.concatenate(v_news, axis=0).astype(BF16)
        o_intra = _dot(qk.astype(BF16), v_stack)
        for h in range(GDN_HEADS):
            rs = slice(h * GDN_CHUNK, (h + 1) * GDN_CHUNK)
            s_scr[h] = s_scr[h] * jnp.exp(gls[h]) + _dot(kds[h].T.astype(BF16), v_stack[rs])
            o = (o_inter[h] + o_intra[rs])[:rv]
            on = o * lax.rsqrt(jnp.mean(o * o, axis=-1, keepdims=True) + EPS) * nw_ref[...]
            zz = z_ref[r0:r0 + rv, h * GDN_HD:(h + 1) * GDN_HD]
            o_ref[r0:r0 + rv, h * GDN_HD:(h + 1) * GDN_HD] = on * (zz * _sigmoid(zz))

    last = xs_scr[row0 + tt - (GDN_CONV - 1):row0 + tt, :]
    buf_ref[0] = last
    xs_scr[GDN_PREV0:row0, :] = last

    @pl.when(t == nt - 1)
    def _():
        s_ref[0] = s_scr[...]


def _gdn_kernel(qkv_ref, z_ref, ba_ref, prev_ref, s0_ref, cw_ref, par_ref, nw_ref, m_ref,
                o_ref, buf_ref, s_ref, xs_scr, s_scr, *, tt):
    t = pl.program_id(1)
    nt = pl.num_programs(1)
    row0 = 8

    @pl.when(t == 0)
    def _():
        xs_scr[GDN_PREV0:row0, :] = prev_ref[0]
        s_scr[...] = s0_ref[0]

    xs_scr[row0:row0 + tt, :] = qkv_ref[...]
    causal, strict, eye = m_ref[6], m_ref[7], m_ref[8]
    neg_ea = -jnp.exp(par_ref[0:1, :])
    dtb = par_ref[1:2, :]
    rv = min(GDN_CHUNK, tt)
    pad = GDN_CHUNK - rv
    n_ch = -(-tt // GDN_CHUNK)

    def padrows(a):
        if pad == 0:
            return a
        return jnp.concatenate([a, jnp.zeros((pad, a.shape[1]), a.dtype)], axis=0)

    qms, kms, vms, betas, gcols = [], [], [], [], []
    for n in range(n_ch):
        r0 = n * GDN_CHUNK
        y = cw_ref[0:1, :] * xs_scr[GDN_PREV0 + r0:GDN_PREV0 + r0 + rv, :]
        for i in range(1, GDN_CONV):
            y = y + cw_ref[i:i + 1, :] * xs_scr[GDN_PREV0 + i + r0:GDN_PREV0 + i + r0 + rv, :]
        y = y * _sigmoid(y)
        qs, ks, vs, bs, gs = [], [], [], [], []
        for h in range(GDN_HEADS):
            q = y[:, h * GDN_HD:(h + 1) * GDN_HD]
            k = y[:, GDN_W + h * GDN_HD:GDN_W + (h + 1) * GDN_HD]
            v = y[:, 2 * GDN_W + h * GDN_HD:2 * GDN_W + (h + 1) * GDN_HD]
            q = q * lax.rsqrt(jnp.sum(q * q, axis=-1, keepdims=True) + EPS) * (GDN_HD ** -0.5)
            k = k * lax.rsqrt(jnp.sum(k * k, axis=-1, keepdims=True) + EPS)
            bcol = ba_ref[r0:r0 + rv, h:h + 1]
            acol = ba_ref[r0:r0 + rv, GDN_HEADS + h:GDN_HEADS + h + 1] + dtb[:, h:h + 1]
            softplus = jnp.maximum(acol, 0.0) + jnp.log1p(jnp.exp(-jnp.abs(acol)))
            qs.append(padrows(q)); ks.append(padrows(k)); vs.append(padrows(v))
            bs.append(padrows(_sigmoid(bcol)))
            gs.append(padrows(neg_ea[:, h:h + 1] * softplus))
        qms.append(jnp.concatenate(qs, axis=0))
        kms.append(jnp.concatenate(ks, axis=0))
        vms.append(jnp.concatenate(vs, axis=0))
        betas.append(jnp.concatenate(bs, axis=0))
        gcols.append(jnp.concatenate(gs, axis=0))
    gmat = jnp.concatenate(gcols + [jnp.zeros((GDN_STACK, LANE - n_ch), F32)], axis=1)
    gc_all = jnp.dot(causal, gmat, preferred_element_type=F32, precision=lax.Precision.HIGHEST)

    a_mats, qks, rhss, gcs, tinv = [], [], [], [], []
    for n in range(n_ch):
        gc = gc_all[:, n:n + 1]
        gc_row = jnp.sum(eye * gc, axis=0, keepdims=True)
        decay = jnp.exp(jnp.where(causal > 0, gc - gc_row, 0.0)) * causal
        kb = kms[n] * betas[n]
        kmb = kms[n].astype(BF16)
        a_mat = _dot_t(kb.astype(BF16), kmb) * decay * strict
        qks.append((_dot_t(qms[n].astype(BF16), kmb) * decay).astype(BF16))
        rhss.append(jnp.concatenate([vms[n] * betas[n], kb * jnp.exp(gc)], axis=1).astype(BF16))
        a_mats.append(a_mat)
        gcs.append(gc)
        tinv.append(eye - a_mat * m_ref[0])
    for lvl in range(1, 6):
        tb = [x.astype(BF16) for x in tinv]
        xs = [_dot((a_mats[n] * m_ref[lvl]).astype(BF16), tb[n]).astype(BF16) for n in range(n_ch)]
        tinv = [tinv[n] - _dot(tb[n], xs[n]) for n in range(n_ch)]
    sols = [_dot(tinv[n].astype(BF16), rhss[n]) for n in range(n_ch)]

    for n in range(n_ch):
        r0 = n * GDN_CHUNK
        gc = gcs[n]
        egc = jnp.exp(gc)
        u_all, w_all = sols[n][:, :GDN_HD], sols[n][:, GDN_HD:]
        qd = qms[n] * egc
        v_news, o_inter, gls = [], [], []
        for h in range(GDN_HEADS):
            rs = slice(h * GDN_CHUNK, (h + 1) * GDN_CHUNK)
            sb = s_scr[h].astype(BF16)
            both = _dot(jnp.concatenate([w_all[rs], qd[rs]], axis=0).astype(BF16), sb)
            v_news.append(u_all[rs] - both[:GDN_CHUNK])
            o_inter.append(both[GDN_CHUNK:])
            gls.append(gc[h * GDN_CHUNK + GDN_CHUNK - 1:(h + 1) * GDN_CHUNK, :])
        v_stack = jnp.concatenate(v_news, axis=0).astype(BF16)
        o_intra = _dot(qks[n], v_stack)
        for h in range(GDN_HEADS):
            rs = slice(h * GDN_CHUNK, (h + 1) * GDN_CHUNK)
            kd = kms[n][rs] * jnp.exp(gls[h] - gc[rs])
            s_scr[h] = s_scr[h] * jnp.exp(gls[h]) + _dot(kd.T.astype(BF16), v_stack[rs])
            o = (o_inter[h] + o_intra[rs])[:rv]
            on = o * lax.rsqrt(jnp.mean(o * o, axis=-1, keepdims=True) + EPS) * nw_ref[...]
            zz = z_ref[r0:r0 + rv, h * GDN_HD:(h + 1) * GDN_HD]
            o_ref[r0:r0 + rv, h * GDN_HD:(h + 1) * GDN_HD] = on * (zz * _sigmoid(zz))

    last = xs_scr[row0 + tt - (GDN_CONV - 1):row0 + tt, :]
    buf_ref[0] = last
    xs_scr[GDN_PREV0:row0, :] = last

    @pl.when(t == nt - 1)
    def _():
        s_ref[0] = s_scr[...]


def gdn(qkv_pre, z_pre, ba_pre, prev, s0, conv_w, a_log, dt_bias, norm_w, n_seq, tt):
    n_rows = qkv_pre.shape[0]
    t_rows = n_rows // n_seq
    tt = min(tt, t_rows)
    nt = t_rows // tt
    par = jnp.zeros((8, LANE), F32).at[0, :GDN_HEADS].set(a_log).at[1, :GDN_HEADS].set(dt_bias)
    masks = jnp.asarray(_gdn_masks())
    const2 = lambda b, i: (0, 0)
    rows = lambda b, i: (b * nt + i, 0)
    return pl.pallas_call(
        functools.partial(_gdn_kernel, tt=tt), grid=(n_seq, nt),
        in_specs=[pl.BlockSpec((tt, GDN_C3), rows), pl.BlockSpec((tt, GDN_W), rows),
                  pl.BlockSpec((tt, LANE), rows),
                  pl.BlockSpec((1, GDN_CONV - 1, GDN_C3), lambda b, i: (b, 0, 0)),
                  pl.BlockSpec((1, GDN_HEADS, GDN_HD, GDN_HD), lambda b, i: (b, 0, 0, 0)),
                  pl.BlockSpec((GDN_CONV, GDN_C3), const2), pl.BlockSpec((8, LANE), const2),
                  pl.BlockSpec((1, GDN_HD), const2),
                  pl.BlockSpec(masks.shape, lambda b, i: (0, 0, 0))],
        out_specs=[pl.BlockSpec((tt, GDN_W), rows),
                   pl.BlockSpec((1, GDN_CONV - 1, GDN_C3), lambda b, i: (b, 0, 0)),
                   pl.BlockSpec((1, GDN_HEADS, GDN_HD, GDN_HD), lambda b, i: (b, 0, 0, 0))],
        out_shape=[jax.ShapeDtypeStruct((n_rows, GDN_W), F32),
                   jax.ShapeDtypeStruct((n_seq, GDN_CONV - 1, GDN_C3), F32),
                   jax.ShapeDtypeStruct((n_seq, GDN_HEADS, GDN_HD, GDN_HD), F32)],
        scratch_shapes=[pltpu.VMEM((8 + tt, GDN_C3), F32), pltpu.VMEM((GDN_HEADS, GDN_HD, GDN_HD), F32)],
        compiler_params=_cparams(("arbitrary", "arbitrary"), 48), name="gdn",
    )(qkv_pre, z_pre, ba_pre, prev, s0, conv_w, par, norm_w.reshape(1, GDN_HD), masks)


CMP_W = 2 * NSA_KV_HEADS * NSA_HD


def _cmp_weights(cmp_pe, cmp_w):
    def bd(w):
        z = jnp.zeros_like(w)
        return jnp.concatenate([jnp.concatenate([w, z], 2), jnp.concatenate([z, w], 2)], 1)
    w = jnp.stack([jnp.stack([bd(cmp_w[s, :CMP_STRIDE]), bd(cmp_w[s, CMP_STRIDE:])]) for s in range(2)]).astype(BF16)
    pe2 = jnp.concatenate([cmp_pe, cmp_pe], axis=-1)
    pe = jnp.stack([pe2[:, :CMP_STRIDE], pe2[:, CMP_STRIDE:]], axis=1)
    return w, pe


def _compress(read_k, read_v, w_ref, pe_ref, n_chunk):
    outs = []
    for s, read in enumerate((read_k, read_v)):
        lo = jnp.zeros((n_chunk, LANE), F32)
        hi = jnp.zeros((n_chunk, LANE), F32)
        for j in range(CMP_STRIDE):
            x = read(j)
            lo = lo + _dot((x + pe_ref[s, 0, j:j + 1, :]).astype(BF16), w_ref[s, 0, j])
            hi = hi + _dot((x + pe_ref[s, 1, j:j + 1, :]).astype(BF16), w_ref[s, 1, j])
        outs.append(lo + pltpu.roll(hi, n_chunk - 1, axis=0))
    return jnp.concatenate(outs, axis=1)


def _cmp_kernel(k_ref, v_ref, w_ref, pe_ref, o_ref, *, n_chunk):
    o_ref[0] = _compress(lambda j: k_ref[0, pl.ds(j, n_chunk, stride=CMP_STRIDE), :],
                         lambda j: v_ref[0, pl.ds(j, n_chunk, stride=CMP_STRIDE), :], w_ref, pe_ref, n_chunk)


def compress_prompt(rows, w, pe):
    b, t, _ = rows.shape
    n_chunk = t // CMP_STRIDE
    return pl.pallas_call(
        functools.partial(_cmp_kernel, n_chunk=n_chunk), grid=(b,),
        in_specs=[pl.BlockSpec((1, t, LANE), lambda i: (i, 0, 0)), pl.BlockSpec((1, t, LANE), lambda i: (i, 0, 1)),
                  pl.BlockSpec(w.shape, lambda i: (0, 0, 0, 0, 0)),
                  pl.BlockSpec(pe.shape, lambda i: (0, 0, 0, 0))],
        out_specs=pl.BlockSpec((1, n_chunk, CMP_W), lambda i: (i, 0, 0)),
        out_shape=jax.ShapeDtypeStruct((b, n_chunk, CMP_W), F32),
        compiler_params=_cparams(("parallel",), 32), name="nsa_compress",
    )(rows, rows, w, pe)


def _c2s(n_cmp_pad, n_sel_pad):
    cs = np.arange(n_cmp_pad)[:, None] * CMP_STRIDE
    ss = np.arange(n_sel_pad)[None, :] * SEL_BLOCK
    return ((cs < ss + SEL_BLOCK) & (cs + CMP_BLOCK > ss)).astype(np.float32)


def _expand(n_sel_pad, n_keys):
    s = np.arange(n_sel_pad)[:, None]
    k = np.arange(n_keys)[None, :]
    return (k // SEL_BLOCK == s).astype(np.float32)


def _cmp_probs(s, valid):
    s = jnp.where(valid, s, NEG)
    p = jnp.where(valid, jnp.exp(s - jnp.max(s, axis=-1, keepdims=True)), 0.0)
    return p / jnp.maximum(jnp.sum(p, axis=-1, keepdims=True), 1e-30)


def _select_blocks(imp, q_blk, n_sel):
    lane = lax.broadcasted_iota(jnp.int32, imp.shape, 1)
    visible = lane <= q_blk
    forced = (lane == 0) | (lane == q_blk) | (lane == q_blk - 1)
    score = jnp.where(visible, imp + jnp.where(forced, FORCED_BONUS, 0.0), NEG)
    rank = jnp.zeros(imp.shape, F32)
    for sp in range(n_sel):
        col = score[:, sp:sp + 1]
        beats = (col > score) | ((col == score) & (lane > sp))
        rank = rank + jnp.where(beats, 1.0, 0.0)
    return jnp.where(visible & (rank < SEL_TOPN), 1.0, 0.0)


def _flash(q, k_ref, v_ref, j_lo, j_hi, mask_fn, tq, tk):
    def body(j, carry):
        m, l, acc = carry
        off = pl.multiple_of(j * tk, tk)
        s = _dot_t(q, k_ref[pl.ds(off, tk), :])
        s = jnp.where(mask_fn(j), s, NEG)
        m_new = jnp.maximum(m, jnp.max(s, axis=-1, keepdims=True))
        alpha = jnp.exp(m - m_new)
        p = jnp.exp(s - m_new)
        l = alpha * l + jnp.sum(p, axis=-1, keepdims=True)
        acc = alpha * acc + _dot(p.astype(BF16), v_ref[pl.ds(off, tk), :])
        return m_new, l, acc

    init = (jnp.full((tq, 1), NEG, F32), jnp.zeros((tq, 1), F32), jnp.zeros((tq, NSA_HD), F32))
    _, l, acc = lax.fori_loop(j_lo, j_hi, body, init)
    return acc / l


def _nsa_prompt_kernel_v1(qc_ref, qr_ref, kc_ref, vc_ref, ks_ref, vs_ref, kw_ref, vw_ref, g_ref, c2s_ref, e_ref,
                       o_ref, mask_scr, *, tq, n_sel, n_cmp_pad):
    i = pl.program_id(2)
    t0 = i * tq
    qpos = t0 + lax.broadcasted_iota(jnp.int32, (tq, 1), 0)
    cend = lax.broadcasted_iota(jnp.int32, (1, n_cmp_pad), 1) * CMP_STRIDE + (CMP_BLOCK - 1)
    valid_c = cend <= qpos
    psum = jnp.zeros((tq, n_cmp_pad), F32)
    o_cmp = []
    for g in range(NSA_GROUP):
        p = _cmp_probs(_dot_t(qc_ref[0, g], kc_ref[0, 0]), valid_c)
        psum = psum + p
        o_cmp.append(_dot(p.astype(BF16), vc_ref[0, 0]))
    imp = jnp.dot(psum, c2s_ref[...], preferred_element_type=F32, precision=lax.Precision.HIGHEST)
    sel = _select_blocks(imp, lax.shift_right_logical(qpos, 6), n_sel).astype(BF16)
    n_kt = mask_scr.shape[0]
    for jj in range(n_kt):
        mask_scr[jj] = _dot(sel, e_ref[:, jj * tq:(jj + 1) * tq])
    kiota = lax.broadcasted_iota(jnp.int32, (1, tq), 1)

    def sel_mask(j):
        return (mask_scr[j] > 0.5) & (j * tq + kiota <= qpos)

    def win_mask(j):
        rel = qpos - (j * tq + kiota)
        return (rel >= 0) & (rel < WINDOW)

    w_tiles = WINDOW // tq
    for g in range(NSA_GROUP):
        q = qr_ref[0, g]
        o_sel = _flash(q, ks_ref.at[0, 0], vs_ref.at[0, 0], 0, i + 1, sel_mask, tq, tq)
        o_win = _flash(q, kw_ref.at[0, 0], vw_ref.at[0, 0], jnp.maximum(i - w_tiles, 0), i + 1, win_mask, tq, tq)
        gt = g_ref[0, g]
        o = o_cmp[g] * gt[:, 0:1] + o_sel * gt[:, 1:2] + o_win * gt[:, 2:3]
        o_ref[0, :, g * NSA_HD:(g + 1) * NSA_HD] = o


def nsa_prompt_attn_v1(qc, qr, kcmp, vcmp, ksel, vsel, kwin, vwin, gates, tq):
    b, _, t, _ = qc.shape
    tq = min(tq, t)
    n_cmp_pad = kcmp.shape[2]
    n_sel = t // SEL_BLOCK
    n_sel_pad = -(-n_sel // LANE) * LANE
    c2s = jnp.asarray(_c2s(n_cmp_pad, n_sel_pad))
    e = jnp.asarray(_expand(n_sel_pad, t), dtype=BF16)
    kern = functools.partial(_nsa_prompt_kernel_v1, tq=tq, n_sel=n_sel, n_cmp_pad=n_cmp_pad)
    qspec = pl.BlockSpec((1, NSA_GROUP, tq, NSA_HD), lambda bb, k, i: (bb, k, i, 0))
    kvspec = lambda n: pl.BlockSpec((1, 1, n, NSA_HD), lambda bb, k, i: (bb, k, 0, 0))
    return pl.pallas_call(
        kern, grid=(b, NSA_KV_HEADS, t // tq),
        in_specs=[qspec, qspec, kvspec(n_cmp_pad), kvspec(n_cmp_pad), kvspec(t), kvspec(t), kvspec(t), kvspec(t),
                  pl.BlockSpec((1, NSA_GROUP, tq, 3), lambda bb, k, i: (bb, k, i, 0)),
                  pl.BlockSpec(c2s.shape, lambda bb, k, i: (0, 0)), pl.BlockSpec(e.shape, lambda bb, k, i: (0, 0))],
        out_specs=pl.BlockSpec((1, tq, NSA_GROUP * NSA_HD), lambda bb, k, i: (bb, i, k)),
        out_shape=jax.ShapeDtypeStruct((b, t, NSA_Q_W), F32),
        scratch_shapes=[pltpu.VMEM((t // tq, tq, tq), F32)],
        compiler_params=_cparams(("parallel", "parallel", "parallel"), 40), name="nsa_prompt_attn",
    )(qc, qr, kcmp, vcmp, ksel, vsel, kwin, vwin, gates, c2s, e)


def _pair_rms(x, w):
    lo = lax.broadcasted_iota(jnp.int32, x.shape, 1) < NSA_HD
    x2 = x * x
    s_lo = jnp.sum(jnp.where(lo, x2, 0.0), axis=-1, keepdims=True)
    s_hi = jnp.sum(jnp.where(lo, 0.0, x2), axis=-1, keepdims=True)
    ms = jnp.where(lo, s_lo, s_hi) * (1.0 / NSA_HD)
    return x * lax.rsqrt(ms + EPS) * w


def _pair_rope(x, cos_t, sin_t):
    lane = lax.broadcasted_iota(jnp.int32, x.shape, 1)
    first = (lane & (NSA_HD - 1)) < NSA_HD // 2
    partner = jnp.where(first, pltpu.roll(x, LANE - NSA_HD // 2, axis=1), pltpu.roll(x, NSA_HD // 2, axis=1))
    return x * cos_t + partner * sin_t


def _rope_tables(pos):
    half = NSA_HD // 2
    inv = jnp.power(ROPE_THETA, -jnp.arange(half, dtype=F32) / half)
    ang = pos.astype(F32)[:, None] * inv[None, :]
    c, s = jnp.cos(ang), jnp.sin(ang)
    return jnp.concatenate([c, c, c, c], axis=-1), jnp.concatenate([-s, s, -s, s], axis=-1)


def _nsa_prep_kernel(q_ref, kv_ref, g_ref, cos_ref, sin_ref, qn_ref, kn_ref,
                     rows_ref, wrows_ref, kvb_ref, qc_ref, qr_ref, gate_ref):
    cos_t, sin_t = cos_ref[...], sin_ref[...]
    lo = lax.broadcasted_iota(jnp.int32, cos_t.shape, 1) < NSA_HD
    for c in range(NSA_HEADS // 2):
        y = _pair_rms(q_ref[:, c * LANE:(c + 1) * LANE], qn_ref[...]) * (NSA_HD ** -0.5)
        in_low_lanes = (2 * c) // NSA_GROUP == 0
        for src, dst in ((y, qc_ref), (_pair_rope(y, cos_t, sin_t), qr_ref)):
            swapped = pltpu.roll(src, NSA_HD, axis=1)
            if in_low_lanes:
                even, odd = jnp.where(lo, src, 0.0), jnp.where(lo, swapped, 0.0)
            else:
                even, odd = jnp.where(lo, 0.0, swapped), jnp.where(lo, 0.0, src)
            dst[:, (2 * c) * LANE:(2 * c + 1) * LANE] = even.astype(BF16)
            dst[:, (2 * c + 1) * LANE:(2 * c + 2) * LANE] = odd.astype(BF16)
    for br in range(3):
        k = _pair_rms(kv_ref[:, br * 2 * LANE:br * 2 * LANE + LANE], kn_ref[br:br + 1, :])
        if br > 0:
            k = _pair_rope(k, cos_t, sin_t)
        v = kv_ref[:, br * 2 * LANE + LANE:(br + 1) * 2 * LANE]
        dst, off = (rows_ref, br * 2 * LANE) if br < 2 else (wrows_ref, 0)
        dst[:, off:off + LANE] = k
        dst[:, off + LANE:off + 2 * LANE] = v
        kvb_ref[:, br * 2 * LANE:br * 2 * LANE + LANE] = k.astype(BF16)
        kvb_ref[:, br * 2 * LANE + LANE:(br + 1) * 2 * LANE] = v.astype(BF16)
    gate_ref[...] = _sigmoid(g_ref[...])


def nsa_prep(q_pre, kv_pre, gate_pre, cos_t, sin_t, qn, kn, tm):
    n_rows = q_pre.shape[0]
    tm = min(tm, n_rows)
    n_tab = cos_t.shape[0] // tm
    rows = lambda w: pl.BlockSpec((tm, w), lambda i: (i, 0))
    tab = pl.BlockSpec((tm, LANE), lambda i: (i % n_tab, 0))
    widths = (NSA_Q_W, 2 * LANE, NSA_KV_W, NSA_HEADS * LANE, NSA_HEADS * LANE, LANE)
    dtypes = (F32, F32, BF16, BF16, BF16, F32)
    return pl.pallas_call(
        _nsa_prep_kernel, grid=(n_rows // tm,),
        in_specs=[rows(NSA_Q_W), rows(NSA_KV_W), rows(LANE), tab, tab,
                  pl.BlockSpec((1, LANE), lambda i: (0, 0)), pl.BlockSpec((3, LANE), lambda i: (0, 0))],
        out_specs=[rows(w) for w in widths],
        out_shape=[jax.ShapeDtypeStruct((n_rows, w), dt) for w, dt in zip(widths, dtypes)],
        compiler_params=_cparams(("parallel",), 40), name="nsa_prep",
    )(q_pre, kv_pre, gate_pre, cos_t, sin_t, jnp.tile(qn.reshape(1, NSA_HD), (1, 2)), jnp.tile(kn, (1, 2)))


def _select_blocks_t(imp_t, q_blk, n_sel):
    blk = lax.broadcasted_iota(jnp.int32, imp_t.shape, 0)
    visible = blk <= q_blk
    forced = jnp.where(blk == 0, 1.0, 0.0) + jnp.where(blk == q_blk, 1.0, 0.0) + jnp.where(blk == q_blk - 1, 1.0, 0.0)
    score = jnp.where(visible, imp_t + jnp.where(forced > 0.0, FORCED_BONUS, 0.0), NEG)
    rank = jnp.zeros(imp_t.shape, F32)
    for sp in range(n_sel):
        row = score[sp:sp + 1, :]
        tie = jnp.where(blk > sp, jnp.where(row == score, 1.0, 0.0), 0.0)
        rank = rank + jnp.where(row > score, 1.0, 0.0) + tie
    return jnp.where(visible, jnp.where(rank < SEL_TOPN, 1.0, 0.0), 0.0)


def _flash8(q, k_ref, v_ref, j_lo, j_hi, valid_fn, tq, tk):
    n_rows = NSA_HEADS * tq

    def body(j, carry):
        m, l, acc = carry
        off = pl.multiple_of(j * tk, tk)
        s = _dot_t(q, k_ref[pl.ds(off, tk), :]).reshape(NSA_KV_HEADS, NSA_GROUP, tq, tk)
        s = jnp.where(valid_fn(j)[:, None], s, NEG).reshape(n_rows, tk)
        m_new = jnp.maximum(m, jnp.max(s, axis=-1, keepdims=True))
        alpha = jnp.exp(m - m_new)
        p = jnp.exp(s - m_new)
        l = alpha * l + jnp.sum(p, axis=-1, keepdims=True)
        acc = alpha * acc + _dot(p.astype(BF16), v_ref[pl.ds(off, tk), :])
        return m_new, l, acc

    init = (jnp.full((n_rows, 1), NEG, F32), jnp.zeros((n_rows, 1), F32), jnp.zeros((n_rows, LANE), F32))
    _, l, acc = lax.fori_loop(j_lo, j_hi, body, init)
    return acc / l


def _nsa_prompt_kernel(qc_ref, qr_ref, kvc_ref, ks_ref, vs_ref, kw_ref, vw_ref, g_ref, c2s_ref, e_ref, o_ref,
                       *, tq, n_sel):
    i = pl.program_id(1)
    t0 = i * tq
    n_rows = NSA_HEADS * tq
    stack = lambda ref: jnp.concatenate([ref[:, h * LANE:(h + 1) * LANE] for h in range(NSA_HEADS)], axis=0)
    qpos = t0 + lax.broadcasted_iota(jnp.int32, (tq, 1), 0)
    n_cmp = kvc_ref.shape[1]
    kc = kvc_ref[0, :, :LANE].astype(BF16)
    vc = kvc_ref[0, :, LANE:].astype(BF16)
    cend = lax.broadcasted_iota(jnp.int32, (1, n_cmp), 1) * CMP_STRIDE + (CMP_BLOCK - 1)
    s = _dot_t(stack(qc_ref), kc).reshape(NSA_KV_HEADS, NSA_GROUP, tq, n_cmp)
    p = _cmp_probs(s, (cend <= qpos)[None, None])
    o_cmp = _dot(p.reshape(n_rows, n_cmp).astype(BF16), vc)
    psum = p[:, 0]
    for g in range(1, NSA_GROUP):
        psum = psum + p[:, g]
    imp = jnp.dot(psum.reshape(NSA_KV_HEADS * tq, n_cmp), c2s_ref[...], preferred_element_type=F32,
                  precision=lax.Precision.HIGHEST)
    n_blk_rows = -(-n_sel // 8) * 8
    col = lax.broadcasted_iota(jnp.int32, (1, NSA_KV_HEADS * tq), 1)
    q_blk = lax.shift_right_logical(t0 + (col & (tq - 1)), 6)
    sel_t = _select_blocks_t(imp.T[:n_blk_rows], q_blk, n_sel)
    sel_t = jnp.concatenate([sel_t, jnp.zeros((LANE - n_blk_rows, NSA_KV_HEADS * tq), F32)], axis=0)
    sel = sel_t.T.astype(BF16)
    kiota = lax.broadcasted_iota(jnp.int32, (1, tq), 1)

    def sel_valid(j):
        causal = (j * tq + kiota) <= qpos
        hit = jnp.stack([_dot(sel[k * tq:(k + 1) * tq], e_ref[j]) for k in range(NSA_KV_HEADS)])
        return jnp.where(causal[None], hit, 0.0) > 0.5

    def win_valid(j):
        rel = qpos - (j * tq + kiota)
        ok = jnp.where(rel >= 0, rel, WINDOW) < WINDOW
        return jnp.broadcast_to(ok[None], (NSA_KV_HEADS, tq, tq))

    qr = stack(qr_ref)
    o_sel = _flash8(qr, ks_ref.at[0], vs_ref.at[0], 0, i + 1, sel_valid, tq, tq)
    o_win = _flash8(qr, kw_ref.at[0], vw_ref.at[0], jnp.maximum(i - WINDOW // tq, 0), i + 1, win_valid, tq, tq)
    lo = lax.broadcasted_iota(jnp.int32, (tq, LANE), 1) < NSA_HD
    for c in range(NSA_HEADS // 2):
        pair = []
        for h in (2 * c, 2 * c + 1):
            rs = slice(h * tq, (h + 1) * tq)
            gt = g_ref[:, 3 * h:3 * h + 3]
            pair.append(o_cmp[rs] * gt[:, 0:1] + o_sel[rs] * gt[:, 1:2] + o_win[rs] * gt[:, 2:3])
        even, odd = pair
        if (2 * c) // NSA_GROUP == 0:
            blk = jnp.where(lo, even, pltpu.roll(odd, NSA_HD, axis=1))
        else:
            blk = jnp.where(lo, pltpu.roll(even, NSA_HD, axis=1), odd)
        o_ref[:, c * LANE:(c + 1) * LANE] = blk


def nsa_prompt_attn(qc, qr, kvc, kvb, gates, n_batch, tq):
    n_rows = qc.shape[0]
    t = n_rows // n_batch
    tq = min(tq, t)
    nt = t // tq
    n_cmp = kvc.shape[1]
    n_sel = t // SEL_BLOCK
    assert n_cmp % LANE == 0 and n_sel <= LANE and tq & (tq - 1) == 0 and WINDOW % tq == 0
    c2s = jnp.asarray(_c2s(n_cmp, LANE))
    e = jnp.asarray(_expand(LANE, t).reshape(LANE, nt, tq).transpose(1, 0, 2), dtype=BF16)
    kvb3 = kvb.reshape(n_batch, t, NSA_KV_W)
    rows = lambda w: pl.BlockSpec((tq, w), lambda b, i: (b * nt + i, 0))
    kv = lambda c: pl.BlockSpec((1, t, LANE), lambda b, i: (b, 0, c))
    return pl.pallas_call(
        functools.partial(_nsa_prompt_kernel, tq=tq, n_sel=n_sel), grid=(n_batch, nt),
        in_specs=[rows(NSA_HEADS * LANE), rows(NSA_HEADS * LANE),
                  pl.BlockSpec((1, n_cmp, CMP_W), lambda b, i: (b, 0, 0)),
                  kv(2), kv(3), kv(4), kv(5), rows(LANE),
                  pl.BlockSpec(c2s.shape, lambda b, i: (0, 0)), pl.BlockSpec(e.shape, lambda b, i: (0, 0, 0))],
        out_specs=rows(NSA_Q_W),
        out_shape=jax.ShapeDtypeStruct((n_rows, NSA_Q_W), F32),
        compiler_params=_cparams(("parallel", "parallel"), 48), name="nsa_prompt_attn",
    )(qc, qr, kvc, kvb3, kvb3, kvb3, kvb3, gates, c2s, e)


SROWS = NSA_HEADS * 8
HALF_W = 2 * NSA_KV_HEADS * NSA_HD


def _page_copies(cache_ref, pt_ref, targets, sem, b, slot, n_pages, page):
    return [pltpu.make_async_copy(cache_ref.at[pt_ref[b, pg], :, pl.ds(col0, buf.shape[2])],
                                  buf.at[slot, pl.ds(pg * page, page), :], sem.at[slot])
            for pg in range(n_pages) for col0, buf in targets]


def _stream_pages(cache_ref, pt_ref, targets, sem, n_pages, page):
    b = pl.program_id(0)
    nb = pl.num_programs(0)
    slot = lax.rem(b, 2)

    @pl.when(b == 0)
    def _():
        for c in _page_copies(cache_ref, pt_ref, targets, sem, 0, 0, n_pages, page):
            c.start()

    @pl.when(b + 1 < nb)
    def _():
        for c in _page_copies(cache_ref, pt_ref, targets, sem, b + 1, 1 - slot, n_pages, page):
            c.start()

    for c in _page_copies(cache_ref, pt_ref, targets, sem, b, slot, n_pages, page):
        c.wait()
    return slot


def _nsa_s1_kernel(pt_ref, cache_ref, q_ref, w_ref, pe_ref, c2s_ref, ocmp_ref, sel_ref, bufk, bufv, sem,
                   *, n_pages, page, ts, n_sel):
    past = n_pages * page
    slot = _stream_pages(cache_ref, pt_ref, [(0, bufk), (LANE, bufv)], sem, n_pages, page)
    n_chunk = past // CMP_STRIDE
    kv = _compress(lambda j: bufk[slot, pl.ds(j, n_chunk, stride=CMP_STRIDE), :],
                   lambda j: bufv[slot, pl.ds(j, n_chunk, stride=CMP_STRIDE), :], w_ref, pe_ref, n_chunk)
    kc = kv[:, :LANE].astype(BF16)
    vc = kv[:, LANE:].astype(BF16)
    tok = lax.broadcasted_iota(jnp.int32, (SROWS, 1), 0) & (ts - 1)
    qpos = past + tok
    cend = lax.broadcasted_iota(jnp.int32, (1, n_chunk), 1) * CMP_STRIDE + (CMP_BLOCK - 1)
    p = _cmp_probs(_dot_t(q_ref[0], kc), cend <= qpos)
    ocmp_ref[0] = _dot(p.astype(BF16), vc)
    psum = []
    for k in range(NSA_KV_HEADS):
        acc = p[k * NSA_GROUP * ts:k * NSA_GROUP * ts + ts]
        for g in range(1, NSA_GROUP):
            r = (k * NSA_GROUP + g) * ts
            acc = acc + p[r:r + ts]
        psum.append(acc)
    psum = jnp.concatenate(psum, axis=0)
    imp = jnp.dot(psum, c2s_ref[...], preferred_element_type=F32, precision=lax.Precision.HIGHEST)
    q_blk = lax.shift_right_logical(past + (lax.broadcasted_iota(jnp.int32, (NSA_KV_HEADS * ts, 1), 0) & (ts - 1)), 6)
    sel_ref[0] = _select_blocks(imp, q_blk, n_sel)


def _nsa_s2_kernel(pt_ref, cache_ref, q_ref, sel_ref, new_ref, win_ref, wnew_ref, ocmp_ref, g_ref, e_ref,
                   o_ref, buf, sem, *, n_pages, page, ts, wb):
    past = n_pages * page
    slot = _stream_pages(cache_ref, pt_ref, [(HALF_W, buf)], sem, n_pages, page)
    q = q_ref[0]
    tok = lax.broadcasted_iota(jnp.int32, (SROWS, 1), 0) & (ts - 1)
    sel = sel_ref[0]
    sel_rows = jnp.concatenate([sel[k * ts:(k + 1) * ts] for k in range(NSA_KV_HEADS) for _ in range(NSA_GROUP)],
                               axis=0)
    n_new = new_ref.shape[1]
    jnew = lax.broadcasted_iota(jnp.int32, (1, n_new), 1)

    def attend(s_past, valid_past, v_past, s_new, valid_new, v_new):
        s_past = jnp.where(valid_past, s_past, NEG)
        s_new = jnp.where(valid_new, s_new, NEG)
        m = jnp.maximum(jnp.max(s_past, axis=-1, keepdims=True), jnp.max(s_new, axis=-1, keepdims=True))
        p_past = jnp.exp(s_past - m)
        p_new = jnp.exp(s_new - m)
        l = jnp.sum(p_past, axis=-1, keepdims=True) + jnp.sum(p_new, axis=-1, keepdims=True)
        return (_dot(p_past.astype(BF16), v_past) + _dot(p_new.astype(BF16), v_new)) / l

    k_past = buf[slot, :, 0:LANE].astype(BF16)
    v_past = buf[slot, :, LANE:HALF_W].astype(BF16)
    mask_past = _dot(sel_rows.astype(BF16), e_ref[...]) > 0.5
    blk_new = past // SEL_BLOCK
    valid_new = (sel_rows[:, blk_new:blk_new + 1] > 0.5) & (jnew <= tok)
    k_new = new_ref[0, :, 0:LANE].astype(BF16)
    v_new = new_ref[0, :, LANE:HALF_W].astype(BF16)
    o_sel = attend(_dot_t(q, k_past), mask_past, v_past, _dot_t(q, k_new), valid_new, v_new)
    u = lax.broadcasted_iota(jnp.int32, (1, wb), 1)
    rel = wb + tok - u
    valid_w = (rel >= 0) & (rel < WINDOW) & (past - wb + u >= 0)
    kw = win_ref[0, :, 0:LANE].astype(BF16)
    vw = win_ref[0, :, LANE:HALF_W].astype(BF16)
    kwn = wnew_ref[0, :, 0:LANE].astype(BF16)
    vwn = wnew_ref[0, :, LANE:HALF_W].astype(BF16)
    o_win = attend(_dot_t(q, kw), valid_w, vw, _dot_t(q, kwn), jnew <= tok, vwn)
    gt = g_ref[0]
    o = ocmp_ref[0] * gt[:, 0:1] + o_sel * gt[:, 1:2] + o_win * gt[:, 2:3]
    half = SROWS // NSA_KV_HEADS
    for k in range(NSA_KV_HEADS):
        o_ref[0, k * half:(k + 1) * half, :] = o[k * half:(k + 1) * half, k * NSA_HD:(k + 1) * NSA_HD]


def nsa_sample_attn(page_table, cache, qc2, qr2, new_sel, win, new_win, gates, cmp_w, cmp_pe, ts):
    bs, n_pages = page_table.shape
    page = cache.shape[1]
    past = n_pages * page
    wb = win.shape[1]
    assert past % SEL_BLOCK == 0 and ts <= CMP_STRIDE and ts & (ts - 1) == 0 and SROWS == NSA_HEADS * ts
    n_chunk = past // CMP_STRIDE
    n_sel = past // SEL_BLOCK + 1
    n_sel_pad = -(-n_sel // LANE) * LANE
    c2s = jnp.asarray(_c2s(n_chunk, n_sel_pad))
    e = jnp.asarray(_expand(n_sel_pad, past), dtype=BF16)
    sems = pltpu.SemaphoreType.DMA((2,))
    scratch1 = [pltpu.VMEM((2, past, LANE), F32), pltpu.VMEM((2, past, LANE), F32), sems]
    scratch2 = [pltpu.VMEM((2, past, HALF_W), F32), sems]
    seq3 = lambda b, pt: (b, 0, 0)
    o_cmp, sel = pl.pallas_call(
        functools.partial(_nsa_s1_kernel, n_pages=n_pages, page=page, ts=ts, n_sel=n_sel),
        grid_spec=pltpu.PrefetchScalarGridSpec(
            num_scalar_prefetch=1, grid=(bs,),
            in_specs=[pl.BlockSpec(memory_space=pl.ANY),
                      pl.BlockSpec((1, SROWS, LANE), seq3),
                      pl.BlockSpec(cmp_w.shape, lambda b, pt: (0, 0, 0, 0, 0)),
                      pl.BlockSpec(cmp_pe.shape, lambda b, pt: (0, 0, 0, 0)),
                      pl.BlockSpec(c2s.shape, lambda b, pt: (0, 0))],
            out_specs=[pl.BlockSpec((1, SROWS, LANE), seq3),
                       pl.BlockSpec((1, NSA_KV_HEADS * ts, n_sel_pad), seq3)],
            scratch_shapes=scratch1),
        out_shape=[jax.ShapeDtypeStruct((bs, SROWS, LANE), F32),
                   jax.ShapeDtypeStruct((bs, NSA_KV_HEADS * ts, n_sel_pad), F32)],
        compiler_params=_cparams(("arbitrary",), 48), name="nsa_sample_cmp",
    )(page_table, cache, qc2, cmp_w, cmp_pe, c2s)
    n_new = new_sel.shape[1]
    return pl.pallas_call(
        functools.partial(_nsa_s2_kernel, n_pages=n_pages, page=page, ts=ts, wb=wb),
        grid_spec=pltpu.PrefetchScalarGridSpec(
            num_scalar_prefetch=1, grid=(bs,),
            in_specs=[pl.BlockSpec(memory_space=pl.ANY),
                      pl.BlockSpec((1, SROWS, LANE), seq3),
                      pl.BlockSpec((1, NSA_KV_HEADS * ts, n_sel_pad), seq3),
                      pl.BlockSpec((1, n_new, HALF_W), seq3),
                      pl.BlockSpec((1, wb, HALF_W), seq3),
                      pl.BlockSpec((1, n_new, HALF_W), seq3),
                      pl.BlockSpec((1, SROWS, LANE), seq3),
                      pl.BlockSpec((1, SROWS, 3), seq3),
                      pl.BlockSpec(e.shape, lambda b, pt: (0, 0))],
            out_specs=pl.BlockSpec((1, SROWS, NSA_HD), seq3),
            scratch_shapes=scratch2),
        out_shape=jax.ShapeDtypeStruct((bs, SROWS, NSA_HD), F32),
        compiler_params=_cparams(("arbitrary",), 56), name="nsa_sample_sel",
    )(page_table, cache, qr2, sel, new_sel, win, new_win, o_cmp, gates, e)


def _in_proj(x2d, pos_rows, p, tm):
    q_pre, kv_pre, gate_pre, qkv_pre, z_pre, ba_pre = rms_matmul(x2d, p["attn_norm_w"], p["w_in_parts"], tm)
    cos_t, sin_t = _rope_tables(pos_rows)
    nsa = nsa_prep(q_pre, kv_pre, gate_pre, cos_t, sin_t, p["nsa_q_norm_w"], p["nsa_k_norm_w"], tm)
    return nsa, qkv_pre, z_pre, ba_pre


def _post_mixer(x2d, o_nsa, o_gdn, mem_kv, p, n_seq, tq):
    h = matmul_res([o_nsa, o_gdn], p["w_out"], x2d, 512)
    (qx,) = rms_matmul(h, p["xattn_norm_w"], [p["w_xq"]], 512)
    return matmul_res([xattn_core(qx, mem_kv, p["xq_norm_w"], n_seq, tq)], p["w_xo"], h, 512)


def _ffn(h, p, prev, n_seq, shift, tm):
    return conv_ffn(h, p["ffn_norm_w"], p["w_up"], p["ffn_conv_w"], p["ffn_conv_b"], p["w_down"], prev,
                    n_seq, shift, tm)


def _layer_prompt(x, mem, p):
    nb, t, d = x.shape
    x2d = x.reshape(nb * t, d)
    (rows, wrows, kvb, qc, qr, gates), qkv_pre, z_pre, ba_pre = _in_proj(x2d, jnp.arange(t), p, 512)
    kvc = compress_prompt(rows.reshape(nb, t, 2 * CMP_W), p["cmp_bd"], p["cmp_pe2"])
    o_nsa = nsa_prompt_attn(qc, qr, kvc, kvb, gates, nb, 256)
    o_gdn, gdn_buf, gdn_s = gdn(qkv_pre, z_pre, ba_pre, jnp.zeros((nb, GDN_CONV - 1, GDN_C3), F32),
                                jnp.zeros((nb, GDN_HEADS, GDN_HD, GDN_HD), F32), p["gdn_conv_w"], p["gdn_a_log"],
                                p["gdn_dt_bias"], p["gdn_norm_w"], nb, 256)
    mem_kv = mem_kv_proj(mem.reshape(-1, d), p["mem_norm_w"], p["w_xk"], p["w_xv"], p["xk_norm_w"])
    mem_kv = mem_kv.reshape(nb, mem.shape[1], 2 * XA_HEADS * XA_HD)
    h = _post_mixer(x2d, o_nsa, o_gdn, mem_kv, p, nb, 256)
    y, ffn_buf = _ffn(h, p, jnp.zeros((nb, FFN_CONV - 1, 2 * D_FF), F32), nb, 1, 256)
    keep = min(WINDOW, t)
    win_state = wrows.reshape(nb, t, 2 * LANE)[:, t - keep:].reshape(nb, keep, 2, NSA_KV_HEADS, NSA_HD)
    return (y.reshape(nb, t, d), rows.reshape(nb, t, 4, NSA_KV_HEADS, NSA_HD), win_state, gdn_s, gdn_buf, ffn_buf,
            mem_kv.reshape(nb, mem.shape[1], 2, XA_HEADS, XA_HD))


def _layer_sample(x, cache_kv, cache_win, state, gdn_cache, ffn_cache, mem_kv, page_table, p):
    bs, ts, d = x.shape
    n_pool, page = cache_kv.shape[:2]
    past = page_table.shape[1] * page
    x2d = x.reshape(bs * ts, d)
    pos_rows = jnp.tile(past + jnp.arange(ts), bs)
    (rows, wrows, _, qc, qr, gates), qkv_pre, z_pre, ba_pre = _in_proj(x2d, pos_rows, p, bs * ts)

    def stack_q(a):
        return a.reshape(bs, ts, NSA_HEADS, LANE).transpose(0, 2, 1, 3).reshape(bs, SROWS, LANE)

    def pad_new(a):
        return jnp.pad(a.reshape(bs, ts, HALF_W), ((0, 0), (0, LANE - ts), (0, 0)))

    wb = cache_win.shape[1]
    gates3 = gates[:, :NSA_GATE_W].reshape(bs, ts, NSA_HEADS, 3).transpose(0, 2, 1, 3).reshape(bs, SROWS, 3)
    o2 = nsa_sample_attn(page_table, cache_kv.reshape(n_pool, page, 2 * HALF_W), stack_q(qc), stack_q(qr),
                         pad_new(rows[:, HALF_W:]), cache_win.reshape(bs, wb, HALF_W), pad_new(wrows),
                         gates3, p["cmp_bd"], p["cmp_pe2"], ts)
    rows = rows.reshape(bs, ts, 4, NSA_KV_HEADS, NSA_HD)
    wrows = wrows.reshape(bs, ts, 2, NSA_KV_HEADS, NSA_HD)
    o_nsa = o2.reshape(bs, NSA_HEADS, ts, NSA_HD).transpose(0, 2, 1, 3).reshape(bs * ts, NSA_Q_W)
    o_gdn, gdn_buf, gdn_s = gdn(qkv_pre, z_pre, ba_pre, gdn_cache, state, p["gdn_conv_w"], p["gdn_a_log"],
                                p["gdn_dt_bias"], p["gdn_norm_w"], bs, ts)
    h = _post_mixer(x2d, o_nsa, o_gdn, mem_kv.reshape(bs, mem_kv.shape[1], -1), p, bs, ts)
    h_tm = h.reshape(bs, ts, d).transpose(1, 0, 2).reshape(ts * bs, d)
    prev = ffn_cache.transpose(1, 0, 2).reshape(1, (FFN_CONV - 1) * bs, 2 * D_FF)
    y, ffn_buf = _ffn(h_tm, p, prev, 1, bs, ts * bs)
    y = y.reshape(ts, bs, d).transpose(1, 0, 2)
    ffn_buf = ffn_buf.reshape(FFN_CONV - 1, bs, 2 * D_FF).transpose(1, 0, 2)
    wall = jnp.concatenate([cache_win, wrows], axis=1)
    keep = min(WINDOW, wb + ts)
    return y, rows, wall[:, wb + ts - keep:], gdn_s, gdn_buf, ffn_buf


def _prep_params(l, attn_norm_w, w_in, nsa_q_norm_w, nsa_k_norm_w, cmp_pe, cmp_w, gdn_conv_w, gdn_a_log,
                 gdn_dt_bias, gdn_norm_w, w_out, mem_norm_w, w_xk, w_xv, xk_norm_w, xattn_norm_w, w_xq,
                 xq_norm_w, w_xo, ffn_norm_w, w_up, ffn_conv_w, ffn_conv_b, w_down):
    wi = w_in[l]
    cuts = np.cumsum([0, NSA_Q_W, NSA_KV_W, NSA_GATE_W, 3 * GDN_W, GDN_W, 2 * GDN_HEADS])
    parts = []
    for a, b in zip(cuts[:-1], cuts[1:]):
        w = wi[:, a:b]
        padn = -(-(b - a) // LANE) * LANE - (b - a)
        parts.append(jnp.pad(w, ((0, 0), (0, padn))).astype(BF16))
    cmp_bd, cmp_pe2 = _cmp_weights(cmp_pe[l], cmp_w[l])
    bf = lambda w: w[l].astype(BF16)
    return dict(attn_norm_w=attn_norm_w[l], w_in_parts=parts, nsa_q_norm_w=nsa_q_norm_w[l],
                nsa_k_norm_w=nsa_k_norm_w[l], cmp_bd=cmp_bd, cmp_pe2=cmp_pe2, gdn_conv_w=gdn_conv_w[l],
                gdn_a_log=gdn_a_log[l], gdn_dt_bias=gdn_dt_bias[l], gdn_norm_w=gdn_norm_w[l], w_out=bf(w_out),
                mem_norm_w=mem_norm_w[l], w_xk=bf(w_xk), w_xv=bf(w_xv), xk_norm_w=xk_norm_w[l],
                xattn_norm_w=xattn_norm_w[l], w_xq=bf(w_xq), xq_norm_w=xq_norm_w[l], w_xo=bf(w_xo),
                ffn_norm_w=ffn_norm_w[l], w_up=bf(w_up), ffn_conv_w=ffn_conv_w[l], ffn_conv_b=ffn_conv_b[l],
                w_down=bf(w_down))


def kernel(x_prompt, x_sample, mem_prompt, cache_nsa_kv, cache_nsa_win, state_gdn, cache_gdn_conv, cache_ffn_conv,
           cache_mem_kv, page_table, attn_norm_w, w_in, nsa_q_norm_w, nsa_k_norm_w, cmp_pe, cmp_w, gdn_conv_w,
           gdn_a_log, gdn_dt_bias, gdn_norm_w, w_out, mem_norm_w, w_xk, w_xv, xk_norm_w, xattn_norm_w, w_xq,
           xq_norm_w, w_xo, ffn_norm_w, w_up, ffn_conv_w, ffn_conv_b, w_down):
    weights = (attn_norm_w, w_in, nsa_q_norm_w, nsa_k_norm_w, cmp_pe, cmp_w, gdn_conv_w, gdn_a_log, gdn_dt_bias,
               gdn_norm_w, w_out, mem_norm_w, w_xk, w_xv, xk_norm_w, xattn_norm_w, w_xq, xq_norm_w, w_xo,
               ffn_norm_w, w_up, ffn_conv_w, ffn_conv_b, w_down)
    depth = cache_nsa_kv.shape[0]
    hp, hs = x_prompt, x_sample
    outs_p, outs_s = [], []
    for l in range(depth):
        p = _prep_params(l, *weights)
        res_p = _layer_prompt(hp, mem_prompt, p)
        hp = res_p[0]
        outs_p.append(res_p[1:])
        res_s = _layer_sample(hs, cache_nsa_kv[l], cache_nsa_win[l], state_gdn[l], cache_gdn_conv[l],
                              cache_ffn_conv[l], cache_mem_kv[l], page_table, p)
        hs = res_s[0]
        outs_s.append(res_s[1:])
    stack = lambda outs, i: jnp.stack([o[i] for o in outs])
    return ((hp, hs) + tuple(stack(outs_p, i) for i in range(6)) + tuple(stack(outs_s, i) for i in range(5)))
```

```python
import functools
import math

import numpy as np
import jax
import jax.numpy as jnp
from jax import lax
from jax.experimental import pallas as pl
from jax.experimental.pallas import tpu as pltpu

F32 = jnp.float32
BF16 = jnp.bfloat16

D_MODEL = 1024
NSA_HEADS = 8
NSA_KV_HEADS = 2
NSA_GROUP = NSA_HEADS // NSA_KV_HEADS
NSA_HD = 64
CMP_BLOCK = 32
CMP_STRIDE = 16
SEL_BLOCK = 64
SEL_TOPN = 8
WINDOW = 512
FORCED_BONUS = 1e4
GDN_HEADS = 4
GDN_HD = 128
GDN_CONV = 4
GDN_CHUNK = 64
GDN_W = GDN_HEADS * GDN_HD
XA_HEADS = 4
XA_HD = 256
D_FF = 2816
FFN_CONV = 3
ROPE_THETA = 10000.0
EPS = 1e-6
NEG = -1e30

NSA_Q_W = NSA_HEADS * NSA_HD
NSA_KV_W = 3 * 2 * NSA_KV_HEADS * NSA_HD
NSA_GATE_W = 3 * NSA_HEADS
LANE = 128
MIB = 1 << 20


def _cparams(sem, vmem_mib):
    return pltpu.CompilerParams(dimension_semantics=sem, vmem_limit_bytes=vmem_mib * MIB)


def _sigmoid(x):
    return 1.0 / (1.0 + jnp.exp(-x))


def _dot(a, b):
    return jnp.dot(a, b, preferred_element_type=F32)


def _dot_t(a, b):
    return lax.dot_general(a, b, (((1,), (1,)), ((), ())), preferred_element_type=F32)


def _split_bf16(a):
    hi = a.astype(BF16)
    lo = (a - hi.astype(F32)).astype(BF16)
    return hi, lo


def _dot3(a, b):
    ah, al = _split_bf16(a)
    bh, bl = _split_bf16(b)
    return _dot(ah, bh) + (_dot(ah, bl) + _dot(al, bh))


def _col_chunk(n):
    for c in (512, 256, 128):
        if n % c == 0:
            return c
    return n


def _rms_mm_kernel(x_ref, g_ref, *refs, n_out):
    w_refs, o_refs = refs[:n_out], refs[n_out:]
    x = x_ref[...]
    xn = (x * lax.rsqrt(jnp.mean(x * x, axis=-1, keepdims=True) + EPS) * g_ref[...]).astype(BF16)
    for w_ref, o_ref in zip(w_refs, o_refs):
        n = w_ref.shape[1]
        ch = _col_chunk(n)
        for c in range(0, n, ch):
            o_ref[:, c:c + ch] = _dot(xn, w_ref[:, c:c + ch])


def rms_matmul(x, g, ws, tm):
    n_rows, d = x.shape
    tm = min(tm, n_rows)
    in_specs = [pl.BlockSpec((tm, d), lambda i: (i, 0)), pl.BlockSpec((1, d), lambda i: (0, 0))]
    in_specs += [pl.BlockSpec(w.shape, lambda i: (0, 0)) for w in ws]
    out_specs = [pl.BlockSpec((tm, w.shape[1]), lambda i: (i, 0)) for w in ws]
    out_shape = [jax.ShapeDtypeStruct((n_rows, w.shape[1]), F32) for w in ws]
    return pl.pallas_call(
        functools.partial(_rms_mm_kernel, n_out=len(ws)),
        grid=(n_rows // tm,), in_specs=in_specs, out_specs=out_specs, out_shape=out_shape,
        compiler_params=_cparams(("parallel",), 56), name="rms_matmul",
    )(x, g.reshape(1, d), *ws)


def _mm_res_kernel(*refs, n_in):
    a_refs, (w_ref, r_ref, o_ref) = refs[:n_in], refs[n_in:]
    a = [a_ref[...].astype(BF16) for a_ref in a_refs]
    n = w_ref.shape[1]
    ch = _col_chunk(n)
    for c in range(0, n, ch):
        acc = r_ref[:, c:c + ch]
        k0 = 0
        for x in a:
            acc = acc + _dot(x, w_ref[k0:k0 + x.shape[1], c:c + ch])
            k0 += x.shape[1]
        o_ref[:, c:c + ch] = acc


def matmul_res(a_list, w, res, tm):
    n_rows = res.shape[0]
    k, n = w.shape
    assert sum(a.shape[1] for a in a_list) == k
    tm = min(tm, n_rows)
    return pl.pallas_call(
        functools.partial(_mm_res_kernel, n_in=len(a_list)), grid=(n_rows // tm,),
        in_specs=[pl.BlockSpec((tm, a.shape[1]), lambda i: (i, 0)) for a in a_list]
        + [pl.BlockSpec((k, n), lambda i: (0, 0)), pl.BlockSpec((tm, n), lambda i: (i, 0))],
        out_specs=pl.BlockSpec((tm, n), lambda i: (i, 0)),
        out_shape=jax.ShapeDtypeStruct((n_rows, n), F32),
        compiler_params=_cparams(("parallel",), 40), name="matmul_res",
    )(*a_list, w, res)


def _memkv_kernel(x_ref, g_ref, wk_ref, wv_ref, kn_ref, o_ref):
    x = x_ref[...]
    xn = (x * lax.rsqrt(jnp.mean(x * x, axis=-1, keepdims=True) + EPS) * g_ref[...]).astype(BF16)
    xa_w = XA_HEADS * XA_HD
    for h in range(XA_HEADS):
        sl = slice(h * XA_HD, (h + 1) * XA_HD)
        k = _dot(xn, wk_ref[:, sl])
        k = k * lax.rsqrt(jnp.mean(k * k, axis=-1, keepdims=True) + EPS) * kn_ref[...]
        o_ref[:, sl] = k
        o_ref[:, xa_w + h * XA_HD: xa_w + (h + 1) * XA_HD] = _dot(xn, wv_ref[:, sl])


def mem_kv_proj(mem, g, wk, wv, kn, tm=256):
    n_rows, d = mem.shape
    xa_w = XA_HEADS * XA_HD
    return pl.pallas_call(
        _memkv_kernel, grid=(n_rows // tm,),
        in_specs=[pl.BlockSpec((tm, d), lambda i: (i, 0)), pl.BlockSpec((1, d), lambda i: (0, 0)),
                  pl.BlockSpec((d, xa_w), lambda i: (0, 0)), pl.BlockSpec((d, xa_w), lambda i: (0, 0)),
                  pl.BlockSpec((1, XA_HD), lambda i: (0, 0))],
        out_specs=pl.BlockSpec((tm, 2 * xa_w), lambda i: (i, 0)),
        out_shape=jax.ShapeDtypeStruct((n_rows, 2 * xa_w), F32),
        compiler_params=_cparams(("parallel",), 40), name="mem_kv_proj",
    )(mem, g.reshape(1, d), wk, wv, kn.reshape(1, XA_HD))


def _xattn_kernel(q_ref, kv_ref, qn_ref, o_ref):
    xa_w = XA_HEADS * XA_HD
    for h in range(XA_HEADS):
        sl = slice(h * XA_HD, (h + 1) * XA_HD)
        q = q_ref[:, sl]
        q = q * lax.rsqrt(jnp.mean(q * q, axis=-1, keepdims=True) + EPS) * qn_ref[...]
        qb = (q * (XA_HD ** -0.5)).astype(BF16)
        k = kv_ref[0, :, sl].astype(BF16)
        v = kv_ref[0, :, xa_w + h * XA_HD: xa_w + (h + 1) * XA_HD].astype(BF16)
        s = _dot_t(qb, k)
        p = jnp.exp(s - jnp.max(s, axis=-1, keepdims=True))
        o = _dot(p.astype(BF16), v) / jnp.sum(p, axis=-1, keepdims=True)
        o_ref[:, sl] = o


def xattn_core(q_pre, kv, qn, n_batch, tq):
    n_rows, xa_w = q_pre.shape
    t = n_rows // n_batch
    tq = min(tq, t)
    nt = t // tq
    m = kv.shape[1]
    return pl.pallas_call(
        _xattn_kernel, grid=(n_batch, nt),
        in_specs=[pl.BlockSpec((tq, xa_w), lambda b, i: (b * nt + i, 0)),
                  pl.BlockSpec((1, m, 2 * xa_w), lambda b, i: (b, 0, 0)),
                  pl.BlockSpec((1, XA_HD), lambda b, i: (0, 0))],
        out_specs=pl.BlockSpec((tq, xa_w), lambda b, i: (b * nt + i, 0)),
        out_shape=jax.ShapeDtypeStruct((n_rows, xa_w), F32),
        compiler_params=_cparams(("parallel", "parallel"), 40), name="xattn_core",
    )(q_pre, kv, qn.reshape(1, XA_HD))


FFN_ACT_CHUNK = 256


def _ffn_kernel(h_ref, g_ref, wup_ref, cw_ref, cb_ref, wdn_ref, prev_ref, o_ref, buf_ref, xs_scr,
                *, tm, shift, base):
    t = pl.program_id(1)
    p0 = base - 2 * shift

    @pl.when(t == 0)
    def _():
        xs_scr[p0:base, :] = prev_ref[0]

    x = h_ref[...]
    xn = (x * lax.rsqrt(jnp.mean(x * x, axis=-1, keepdims=True) + EPS) * g_ref[...]).astype(BF16)
    acc = jnp.zeros((tm, D_MODEL), F32)
    for j in range(D_FF // FFN_ACT_CHUNK):
        halves = []
        for c0 in (j * FFN_ACT_CHUNK, D_FF + j * FFN_ACT_CHUNK):
            sl = slice(c0, c0 + FFN_ACT_CHUNK)
            xs_scr[base:base + tm, sl] = _dot(xn, wup_ref[:, sl])
            y = (cw_ref[0:1, sl] * xs_scr[p0:p0 + tm, sl]
                 + cw_ref[1:2, sl] * xs_scr[p0 + shift:p0 + shift + tm, sl]
                 + cw_ref[2:3, sl] * xs_scr[base:base + tm, sl])
            halves.append(y + cb_ref[:, sl])
        a, u = halves
        act = (a * _sigmoid(a) * u).astype(BF16)
        acc = acc + _dot(act, wdn_ref[j * FFN_ACT_CHUNK:(j + 1) * FFN_ACT_CHUNK, :])
    o_ref[...] = x + acc
    last = xs_scr[base + tm - 2 * shift: base + tm, :]
    buf_ref[0] = last
    xs_scr[p0:base, :] = last


def conv_ffn(h, g, wup, cw, cb, wdn, prev, n_seq, shift, tm):
    n_rows, d = h.shape
    t_rows = n_rows // n_seq
    tm = min(tm, t_rows)
    nt = t_rows // tm
    base = -(-2 * shift // 8) * 8
    kern = functools.partial(_ffn_kernel, tm=tm, shift=shift, base=base)
    const = lambda b, i: (0, 0)
    return pl.pallas_call(
        kern, grid=(n_seq, nt),
        in_specs=[pl.BlockSpec((tm, d), lambda b, i: (b * nt + i, 0)),
                  pl.BlockSpec((1, d), const),
                  pl.BlockSpec((d, 2 * D_FF), const, pipeline_mode=pl.Buffered(1)),
                  pl.BlockSpec((FFN_CONV, 2 * D_FF), const),
                  pl.BlockSpec((1, 2 * D_FF), const),
                  pl.BlockSpec((D_FF, d), const, pipeline_mode=pl.Buffered(1)),
                  pl.BlockSpec((1, 2 * shift, 2 * D_FF), lambda b, i: (b, 0, 0))],
        out_specs=[pl.BlockSpec((tm, d), lambda b, i: (b * nt + i, 0)),
                   pl.BlockSpec((1, 2 * shift, 2 * D_FF), lambda b, i: (b, 0, 0))],
        out_shape=[jax.ShapeDtypeStruct((n_rows, d), F32),
                   jax.ShapeDtypeStruct((n_seq, 2 * shift, 2 * D_FF), F32)],
        scratch_shapes=[pltpu.VMEM((base + tm, 2 * D_FF), F32)],
        compiler_params=_cparams(("arbitrary", "arbitrary"), 56), name="conv_ffn",
    )(h, g.reshape(1, d), wup, cw, cb.reshape(1, 2 * D_FF), wdn, prev)


GDN_STACK = GDN_HEADS * GDN_CHUNK
GDN_C3 = 3 * GDN_W
GDN_PREV0 = 8 - (GDN_CONV - 1)


def _gdn_masks():
    r = np.arange(GDN_STACK)[:, None]
    c = np.arange(GDN_STACK)[None, :]
    same = lambda n: (r // n) == (c // n)
    m = [same(2) & (c < r)]
    for n in (4, 8, 16, 32, 64):
        m.append(same(n) & ~same(n // 2) & (c < r))
    m.append(same(GDN_CHUNK) & (c <= r))
    m.append(same(GDN_CHUNK) & (c < r))
    m.append(r == c)
    return np.stack(m).astype(np.float32)


def _unit_lower_inverse(a, m_ref):
    t = m_ref[8] - a * m_ref[0]
    for lvl in range(1, 6):
        off = a * m_ref[lvl]
        t = t - _dot3(t, _dot3(off, t))
    return t


def _gdn_kernel_v1(qkv_ref, z_ref, ba_ref, prev_ref, s0_ref, cw_ref, par_ref, nw_ref, m_ref,
                   o_ref, buf_ref, s_ref, xs_scr, s_scr, *, tt):
    t = pl.program_id(1)
    nt = pl.num_programs(1)
    row0 = 8

    @pl.when(t == 0)
    def _():
        xs_scr[GDN_PREV0:row0, :] = prev_ref[0]
        s_scr[...] = s0_ref[0]

    xs_scr[row0:row0 + tt, :] = qkv_ref[...]
    causal = m_ref[6]
    neg_ea = -jnp.exp(par_ref[0:1, :])
    dtb = par_ref[1:2, :]
    rv = min(GDN_CHUNK, tt)
    pad = GDN_CHUNK - rv

    def padrows(a):
        if pad == 0:
            return a
        return jnp.concatenate([a, jnp.zeros((pad, a.shape[1]), a.dtype)], axis=0)

    for n in range(-(-tt // GDN_CHUNK)):
        r0 = n * GDN_CHUNK
        y = cw_ref[0:1, :] * xs_scr[GDN_PREV0 + r0:GDN_PREV0 + r0 + rv, :]
        for i in range(1, GDN_CONV):
            y = y + cw_ref[i:i + 1, :] * xs_scr[GDN_PREV0 + i + r0:GDN_PREV0 + i + r0 + rv, :]
        y = y * _sigmoid(y)
        qs, ks, vs, betas, gs = [], [], [], [], []
        for h in range(GDN_HEADS):
            q = y[:, h * GDN_HD:(h + 1) * GDN_HD]
            k = y[:, GDN_W + h * GDN_HD:GDN_W + (h + 1) * GDN_HD]
            v = y[:, 2 * GDN_W + h * GDN_HD:2 * GDN_W + (h + 1) * GDN_HD]
            q = q * lax.rsqrt(jnp.sum(q * q, axis=-1, keepdims=True) + EPS) * (GDN_HD ** -0.5)
            k = k * lax.rsqrt(jnp.sum(k * k, axis=-1, keepdims=True) + EPS)
            bcol = ba_ref[r0:r0 + rv, h:h + 1]
            acol = ba_ref[r0:r0 + rv, GDN_HEADS + h:GDN_HEADS + h + 1] + dtb[:, h:h + 1]
            softplus = jnp.maximum(acol, 0.0) + jnp.log1p(jnp.exp(-jnp.abs(acol)))
            qs.append(padrows(q)); ks.append(padrows(k)); vs.append(padrows(v))
            betas.append(padrows(_sigmoid(bcol)))
            gs.append(padrows(neg_ea[:, h:h + 1] * softplus))
        qm = jnp.concatenate(qs, axis=0)
        km = jnp.concatenate(ks, axis=0)
        vm = jnp.concatenate(vs, axis=0)
        beta = jnp.concatenate(betas, axis=0)
        g = jnp.concatenate(gs, axis=0)
        gc = jnp.dot(causal, jnp.broadcast_to(g, (GDN_STACK, LANE)), preferred_element_type=F32,
                     precision=lax.Precision.HIGHEST)[:, 0:1]
        gc_row = jnp.sum(m_ref[8] * gc, axis=0, keepdims=True)
        diff = gc - gc_row
        decay = jnp.exp(jnp.where(causal > 0, diff, 0.0)) * causal
        kb = km * beta
        kmb = km.astype(BF16)
        a_mat = _dot_t(kb.astype(BF16), kmb) * decay * m_ref[7]
        qk = _dot_t(qm.astype(BF16), kmb) * decay
        tinv = _unit_lower_inverse(a_mat, m_ref)
        egc = jnp.exp(gc)
        sol = _dot3(tinv, jnp.concatenate([vm * beta, kb * egc], axis=1))
        u_all, w_all = sol[:, :GDN_HD], sol[:, GDN_HD:]
        qd = qm * egc
        v_news, o_inter, kds, gls = [], [], [], []
        for h in range(GDN_HEADS):
            rs = slice(h * GDN_CHUNK, (h + 1) * GDN_CHUNK)
            sb = s_scr[h].astype(BF16)
            v_new = u_all[rs] - _dot(w_all[rs].astype(BF16), sb)
            o_inter.append(_dot(qd[rs].astype(BF16), sb))
            gl = gc[h * GDN_CHUNK + GDN_CHUNK - 1:(h + 1) * GDN_CHUNK, :]
            kds.append(km[rs] * jnp.exp(gl - gc[rs]))
            v_news.append(v_new)
            gls.append(gl)
        v_stack = jnp.concatenate(v_news, axis=0).astype(BF16)
        o_intra = _dot(qk.astype(BF16), v_stack)
        for h in range(GDN_HEADS):
            rs = slice(h * GDN_CHUNK, (h + 1) * GDN_CHUNK)
            s_scr[h] = s_scr[h] * jnp.exp(gls[h]) + _dot(kds[h].T.astype(BF16), v_stack[rs])
            o = (o_inter[h] + o_intra[rs])[:rv]
            on = o * lax.rsqrt(jnp.mean(o * o, axis=-1, keepdims=True) + EPS) * nw_ref[...]
            zz = z_ref[r0:r0 + rv, h * GDN_HD:(h + 1) * GDN_HD]
            o_ref[r0:r0 + rv, h * GDN_HD:(h + 1) * GDN_HD] = on * (zz * _sigmoid(zz))

    last = xs_scr[row0 + tt - (GDN_CONV - 1):row0 + tt, :]
    buf_ref[0] = last
    xs_scr[GDN_PREV0:row0, :] = last

    @pl.when(t == nt - 1)
    def _():
        s_ref[0] = s_scr[...]


def _gdn_kernel(qkv_ref, z_ref, ba_ref, prev_ref, s0_ref, cw_ref, par_ref, nw_ref, m_ref,
                o_ref, buf_ref, s_ref, xs_scr, s_scr, *, tt):
    t = pl.program_id(1)
    nt = pl.num_programs(1)
    row0 = 8

    @pl.when(t == 0)
    def _():
        xs_scr[GDN_PREV0:row0, :] = prev_ref[0]
        s_scr[...] = s0_ref[0]

    xs_scr[row0:row0 + tt, :] = qkv_ref[...]
    causal, strict, eye = m_ref[6], m_ref[7], m_ref[8]
    neg_ea = -jnp.exp(par_ref[0:1, :])
    dtb = par_ref[1:2, :]
    rv = min(GDN_CHUNK, tt)
    pad = GDN_CHUNK - rv
    n_ch = -(-tt // GDN_CHUNK)

    def padrows(a):
        if pad == 0:
            return a
        return jnp.concatenate([a, jnp.zeros((pad, a.shape[1]), a.dtype)], axis=0)

    qms, kms, vms, betas, gcols = [], [], [], [], []
    for n in range(n_ch):
        r0 = n * GDN_CHUNK
        y = cw_ref[0:1, :] * xs_scr[GDN_PREV0 + r0:GDN_PREV0 + r0 + rv, :]
        for i in range(1, GDN_CONV):
            y = y + cw_ref[i:i + 1, :] * xs_scr[GDN_PREV0 + i + r0:GDN_PREV0 + i + r0 + rv, :]
        y = y * _sigmoid(y)
        qs, ks, vs, bs, gs = [], [], [], [], []
        for h in range(GDN_HEADS):
            q = y[:, h * GDN_HD:(h + 1) * GDN_HD]
            k = y[:, GDN_W + h * GDN_HD:GDN_W + (h + 1) * GDN_HD]
            v = y[:, 2 * GDN_W + h * GDN_HD:2 * GDN_W + (h + 1) * GDN_HD]
            q = q * lax.rsqrt(jnp.sum(q * q, axis=-1, keepdims=True) + EPS) * (GDN_HD ** -0.5)
            k = k * lax.rsqrt(jnp.sum(k * k, axis=-1, keepdims=True) + EPS)
            bcol = ba_ref[r0:r0 + rv, h:h + 1]
            acol = ba_ref[r0:r0 + rv, GDN_HEADS + h:GDN_HEADS + h + 1] + dtb[:, h:h + 1]
            softplus = jnp.maximum(acol, 0.0) + jnp.log1p(jnp.exp(-jnp.abs(acol)))
            qs.append(padrows(q)); ks.append(padrows(k)); vs.append(padrows(v))
            bs.append(padrows(_sigmoid(bcol)))
            gs.append(padrows(neg_ea[:, h:h + 1] * softplus))
        qms.append(jnp.concatenate(qs, axis=0))
        kms.append(jnp.concatenate(ks, axis=0))
        vms.append(jnp.concatenate(vs, axis=0))
        betas.append(jnp.concatenate(bs, axis=0))
        gcols.append(jnp.concatenate(gs, axis=0))
    gmat = jnp.concatenate(gcols + [jnp.zeros((GDN_STACK, LANE - n_ch), F32)], axis=1)
    gc_all = jnp.dot(causal, gmat, preferred_element_type=F32, precision=lax.Precision.HIGHEST)

    a_mats, qks, rhss, gcs, tinv = [], [], [], [], []
    for n in range(n_ch):
        gc = gc_all[:, n:n + 1]
        gc_row = jnp.sum(eye * gc, axis=0, keepdims=True)
        decay = jnp.exp(jnp.where(causal > 0, gc - gc_row, 0.0)) * causal
        kb = kms[n] * betas[n]
        kmb = kms[n].astype(BF16)
        a_mat = _dot_t(kb.astype(BF16), kmb) * decay * strict
        qks.append((_dot_t(qms[n].astype(BF16), kmb) * decay).astype(BF16))
        rhss.append(jnp.concatenate([vms[n] * betas[n], kb * jnp.exp(gc)], axis=1).astype(BF16))
        a_mats.append(a_mat)
        gcs.append(gc)
        tinv.append(eye - a_mat * m_ref[0])
    for lvl in range(1, 6):
        tb = [x.astype(BF16) for x in tinv]
        xs = [_dot((a_mats[n] * m_ref[lvl]).astype(BF16), tb[n]).astype(BF16) for n in range(n_ch)]
        tinv = [tinv[n] - _dot(tb[n], xs[n]) for n in range(n_ch)]
    sols = [_dot(tinv[n].astype(BF16), rhss[n]) for n in range(n_ch)]

    for n in range(n_ch):
        r0 = n * GDN_CHUNK
        gc = gcs[n]
        egc = jnp.exp(gc)
        u_all, w_all = sols[n][:, :GDN_HD], sols[n][:, GDN_HD:]
        qd = qms[n] * egc
        v_news, o_inter, gls = [], [], []
        for h in range(GDN_HEADS):
            rs = slice(h * GDN_CHUNK, (h + 1) * GDN_CHUNK)
            sb = s_scr[h].astype(BF16)
            both = _dot(jnp.concatenate([w_all[rs], qd[rs]], axis=0).astype(BF16), sb)
            v_news.append(u_all[rs] - both[:GDN_CHUNK])
            o_inter.append(both[GDN_CHUNK:])
            gls.append(gc[h * GDN_CHUNK + GDN_CHUNK - 1:(h + 1) * GDN_CHUNK, :])
        v_stack = jnp.concatenate(v_news, axis=0).astype(BF16)
        o_intra = _dot(qks[n], v_stack)
        for h in range(GDN_HEADS):
            rs = slice(h * GDN_CHUNK, (h + 1) * GDN_CHUNK)
            kd = kms[n][rs] * jnp.exp(gls[h] - gc[rs])
            s_scr[h] = s_scr[h] * jnp.exp(gls[h]) + _dot(kd.T.astype(BF16), v_stack[rs])
            o = (o_inter[h] + o_intra[rs])[:rv]
            on = o * lax.rsqrt(jnp.mean(o * o, axis=-1, keepdims=True) + EPS) * nw_ref[...]
            zz = z_ref[r0:r0 + rv, h * GDN_HD:(h + 1) * GDN_HD]
            o_ref[r0:r0 + rv, h * GDN_HD:(h + 1) * GDN_HD] = on * (zz * _sigmoid(zz))

    last = xs_scr[row0 + tt - (GDN_CONV - 1):row0 + tt, :]
    buf_ref[0] = last
    xs_scr[GDN_PREV0:row0, :] = last

    @pl.when(t == nt - 1)
    def _():
        s_ref[0] = s_scr[...]


def gdn(qkv_pre, z_pre, ba_pre, prev, s0, conv_w, a_log, dt_bias, norm_w, n_seq, tt):
    n_rows = qkv_pre.shape[0]
    t_rows = n_rows // n_seq
    tt = min(tt, t_rows)
    nt = t_rows // tt
    par = jnp.zeros((8, LANE), F32).at[0, :GDN_HEADS].set(a_log).at[1, :GDN_HEADS].set(dt_bias)
    masks = jnp.asarray(_gdn_masks())
    const2 = lambda b, i: (0, 0)
    rows = lambda b, i: (b * nt + i, 0)
    return pl.pallas_call(
        functools.partial(_gdn_kernel, tt=tt), grid=(n_seq, nt),
        in_specs=[pl.BlockSpec((tt, GDN_C3), rows), pl.BlockSpec((tt, GDN_W), rows),
                  pl.BlockSpec((tt, LANE), rows),
                  pl.BlockSpec((1, GDN_CONV - 1, GDN_C3), lambda b, i: (b, 0, 0)),
                  pl.BlockSpec((1, GDN_HEADS, GDN_HD, GDN_HD), lambda b, i: (b, 0, 0, 0)),
                  pl.BlockSpec((GDN_CONV, GDN_C3), const2), pl.BlockSpec((8, LANE), const2),
                  pl.BlockSpec((1, GDN_HD), const2),
                  pl.BlockSpec(masks.shape, lambda b, i: (0, 0, 0))],
        out_specs=[pl.BlockSpec((tt, GDN_W), rows),
                   pl.BlockSpec((1, GDN_CONV - 1, GDN_C3), lambda b, i: (b, 0, 0)),
                   pl.BlockSpec((1, GDN_HEADS, GDN_HD, GDN_HD), lambda b, i: (b, 0, 0, 0))],
        out_shape=[jax.ShapeDtypeStruct((n_rows, GDN_W), F32),
                   jax.ShapeDtypeStruct((n_seq, GDN_CONV - 1, GDN_C3), F32),
                   jax.ShapeDtypeStruct((n_seq, GDN_HEADS, GDN_HD, GDN_HD), F32)],
        scratch_shapes=[pltpu.VMEM((8 + tt, GDN_C3), F32), pltpu.VMEM((GDN_HEADS, GDN_HD, GDN_HD), F32)],
        compiler_params=_cparams(("arbitrary", "arbitrary"), 48), name="gdn",
    )(qkv_pre, z_pre, ba_pre, prev, s0, conv_w, par, norm_w.reshape(1, GDN_HD), masks)


CMP_W = 2 * NSA_KV_HEADS * NSA_HD


def _cmp_weights(cmp_pe, cmp_w):
    def bd(w):
        z = jnp.zeros_like(w)
        return jnp.concatenate([jnp.concatenate([w, z], 2), jnp.concatenate([z, w], 2)], 1)
    w = jnp.stack([jnp.stack([bd(cmp_w[s, :CMP_STRIDE]), bd(cmp_w[s, CMP_STRIDE:])]) for s in range(2)]).astype(BF16)
    pe2 = jnp.concatenate([cmp_pe, cmp_pe], axis=-1)
    pe = jnp.stack([pe2[:, :CMP_STRIDE], pe2[:, CMP_STRIDE:]], axis=1)
    return w, pe


def _compress(read_k, read_v, w_ref, pe_ref, n_chunk):
    outs = []
    for s, read in enumerate((read_k, read_v)):
        lo = jnp.zeros((n_chunk, LANE), F32)
        hi = jnp.zeros((n_chunk, LANE), F32)
        for j in range(CMP_STRIDE):
            x = read(j)
            lo = lo + _dot((x + pe_ref[s, 0, j:j + 1, :]).astype(BF16), w_ref[s, 0, j])
            hi = hi + _dot((x + pe_ref[s, 1, j:j + 1, :]).astype(BF16), w_ref[s, 1, j])
        outs.append(lo + pltpu.roll(hi, n_chunk - 1, axis=0))
    return jnp.concatenate(outs, axis=1)


def _cmp_kernel(k_ref, v_ref, w_ref, pe_ref, o_ref, *, n_chunk):
    o_ref[0] = _compress(lambda j: k_ref[0, pl.ds(j, n_chunk, stride=CMP_STRIDE), :],
                         lambda j: v_ref[0, pl.ds(j, n_chunk, stride=CMP_STRIDE), :], w_ref, pe_ref, n_chunk)


def compress_prompt(rows, w, pe):
    b, t, _ = rows.shape
    n_chunk = t // CMP_STRIDE
    return pl.pallas_call(
        functools.partial(_cmp_kernel, n_chunk=n_chunk), grid=(b,),
        in_specs=[pl.BlockSpec((1, t, LANE), lambda i: (i, 0, 0)), pl.BlockSpec((1, t, LANE), lambda i: (i, 0, 1)),
                  pl.BlockSpec(w.shape, lambda i: (0, 0, 0, 0, 0)),
                  pl.BlockSpec(pe.shape, lambda i: (0, 0, 0, 0))],
        out_specs=pl.BlockSpec((1, n_chunk, CMP_W), lambda i: (i, 0, 0)),
        out_shape=jax.ShapeDtypeStruct((b, n_chunk, CMP_W), F32),
        compiler_params=_cparams(("parallel",), 32), name="nsa_compress",
    )(rows, rows, w, pe)


def _c2s(n_cmp_pad, n_sel_pad):
    cs = np.arange(n_cmp_pad)[:, None] * CMP_STRIDE
    ss = np.arange(n_sel_pad)[None, :] * SEL_BLOCK
    return ((cs < ss + SEL_BLOCK) & (cs + CMP_BLOCK > ss)).astype(np.float32)


def _expand(n_sel_pad, n_keys):
    s = np.arange(n_sel_pad)[:, None]
    k = np.arange(n_keys)[None, :]
    return (k // SEL_BLOCK == s).astype(np.float32)


def _cmp_probs(s, valid):
    s = jnp.where(valid, s, NEG)
    p = jnp.where(valid, jnp.exp(s - jnp.max(s, axis=-1, keepdims=True)), 0.0)
    return p / jnp.maximum(jnp.sum(p, axis=-1, keepdims=True), 1e-30)


def _select_blocks(imp, q_blk, n_sel):
    lane = lax.broadcasted_iota(jnp.int32, imp.shape, 1)
    visible = lane <= q_blk
    forced = (lane == 0) | (lane == q_blk) | (lane == q_blk - 1)
    score = jnp.where(visible, imp + jnp.where(forced, FORCED_BONUS, 0.0), NEG)
    rank = jnp.zeros(imp.shape, F32)
    for sp in range(n_sel):
        col = score[:, sp:sp + 1]
        beats = (col > score) | ((col == score) & (lane > sp))
        rank = rank + jnp.where(beats, 1.0, 0.0)
    return jnp.where(visible & (rank < SEL_TOPN), 1.0, 0.0)


def _flash(q, k_ref, v_ref, j_lo, j_hi, mask_fn, tq, tk):
    def body(j, carry):
        m, l, acc = carry
        off = pl.multiple_of(j * tk, tk)
        s = _dot_t(q, k_ref[pl.ds(off, tk), :])
        s = jnp.where(mask_fn(j), s, NEG)
        m_new = jnp.maximum(m, jnp.max(s, axis=-1, keepdims=True))
        alpha = jnp.exp(m - m_new)
        p = jnp.exp(s - m_new)
        l = alpha * l + jnp.sum(p, axis=-1, keepdims=True)
        acc = alpha * acc + _dot(p.astype(BF16), v_ref[pl.ds(off, tk), :])
        return m_new, l, acc

    init = (jnp.full((tq, 1), NEG, F32), jnp.zeros((tq, 1), F32), jnp.zeros((tq, NSA_HD), F32))
    _, l, acc = lax.fori_loop(j_lo, j_hi, body, init)
    return acc / l


def _nsa_prompt_kernel_v1(qc_ref, qr_ref, kc_ref, vc_ref, ks_ref, vs_ref, kw_ref, vw_ref, g_ref, c2s_ref, e_ref,
                       o_ref, mask_scr, *, tq, n_sel, n_cmp_pad):
    i = pl.program_id(2)
    t0 = i * tq
    qpos = t0 + lax.broadcasted_iota(jnp.int32, (tq, 1), 0)
    cend = lax.broadcasted_iota(jnp.int32, (1, n_cmp_pad), 1) * CMP_STRIDE + (CMP_BLOCK - 1)
    valid_c = cend <= qpos
    psum = jnp.zeros((tq, n_cmp_pad), F32)
    o_cmp = []
    for g in range(NSA_GROUP):
        p = _cmp_probs(_dot_t(qc_ref[0, g], kc_ref[0, 0]), valid_c)
        psum = psum + p
        o_cmp.append(_dot(p.astype(BF16), vc_ref[0, 0]))
    imp = jnp.dot(psum, c2s_ref[...], preferred_element_type=F32, precision=lax.Precision.HIGHEST)
    sel = _select_blocks(imp, lax.shift_right_logical(qpos, 6), n_sel).astype(BF16)
    n_kt = mask_scr.shape[0]
    for jj in range(n_kt):
        mask_scr[jj] = _dot(sel, e_ref[:, jj * tq:(jj + 1) * tq])
    kiota = lax.broadcasted_iota(jnp.int32, (1, tq), 1)

    def sel_mask(j):
        return (mask_scr[j] > 0.5) & (j * tq + kiota <= qpos)

    def win_mask(j):
        rel = qpos - (j * tq + kiota)
        return (rel >= 0) & (rel < WINDOW)

    w_tiles = WINDOW // tq
    for g in range(NSA_GROUP):
        q = qr_ref[0, g]
        o_sel = _flash(q, ks_ref.at[0, 0], vs_ref.at[0, 0], 0, i + 1, sel_mask, tq, tq)
        o_win = _flash(q, kw_ref.at[0, 0], vw_ref.at[0, 0], jnp.maximum(i - w_tiles, 0), i + 1, win_mask, tq, tq)
        gt = g_ref[0, g]
        o = o_cmp[g] * gt[:, 0:1] + o_sel * gt[:, 1:2] + o_win * gt[:, 2:3]
        o_ref[0, :, g * NSA_HD:(g + 1) * NSA_HD] = o


def nsa_prompt_attn_v1(qc, qr, kcmp, vcmp, ksel, vsel, kwin, vwin, gates, tq):
    b, _, t, _ = qc.shape
    tq = min(tq, t)
    n_cmp_pad = kcmp.shape[2]
    n_sel = t // SEL_BLOCK
    n_sel_pad = -(-n_sel // LANE) * LANE
    c2s = jnp.asarray(_c2s(n_cmp_pad, n_sel_pad))
    e = jnp.asarray(_expand(n_sel_pad, t), dtype=BF16)
    kern = functools.partial(_nsa_prompt_kernel_v1, tq=tq, n_sel=n_sel, n_cmp_pad=n_cmp_pad)
    qspec = pl.BlockSpec((1, NSA_GROUP, tq, NSA_HD), lambda bb, k, i: (bb, k, i, 0))
    kvspec = lambda n: pl.BlockSpec((1, 1, n, NSA_HD), lambda bb, k, i: (bb, k, 0, 0))
    return pl.pallas_call(
        kern, grid=(b, NSA_KV_HEADS, t // tq),
        in_specs=[qspec, qspec, kvspec(n_cmp_pad), kvspec(n_cmp_pad), kvspec(t), kvspec(t), kvspec(t), kvspec(t),
                  pl.BlockSpec((1, NSA_GROUP, tq, 3), lambda bb, k, i: (bb, k, i, 0)),
                  pl.BlockSpec(c2s.shape, lambda bb, k, i: (0, 0)), pl.BlockSpec(e.shape, lambda bb, k, i: (0, 0))],
        out_specs=pl.BlockSpec((1, tq, NSA_GROUP * NSA_HD), lambda bb, k, i: (bb, i, k)),
        out_shape=jax.ShapeDtypeStruct((b, t, NSA_Q_W), F32),
        scratch_shapes=[pltpu.VMEM((t // tq, tq, tq), F32)],
        compiler_params=_cparams(("parallel", "parallel", "parallel"), 40), name="nsa_prompt_attn",
    )(qc, qr, kcmp, vcmp, ksel, vsel, kwin, vwin, gates, c2s, e)


def _pair_rms(x, w):
    lo = lax.broadcasted_iota(jnp.int32, x.shape, 1) < NSA_HD
    x2 = x * x
    s_lo = jnp.sum(jnp.where(lo, x2, 0.0), axis=-1, keepdims=True)
    s_hi = jnp.sum(jnp.where(lo, 0.0, x2), axis=-1, keepdims=True)
    ms = jnp.where(lo, s_lo, s_hi) * (1.0 / NSA_HD)
    return x * lax.rsqrt(ms + EPS) * w


def _pair_rope(x, cos_t, sin_t):
    lane = lax.broadcasted_iota(jnp.int32, x.shape, 1)
    first = (lane & (NSA_HD - 1)) < NSA_HD // 2
    partner = jnp.where(first, pltpu.roll(x, LANE - NSA_HD // 2, axis=1), pltpu.roll(x, NSA_HD // 2, axis=1))
    return x * cos_t + partner * sin_t


def _rope_tables(pos):
    half = NSA_HD // 2
    inv = jnp.power(ROPE_THETA, -jnp.arange(half, dtype=F32) / half)
    ang = pos.astype(F32)[:, None] * inv[None, :]
    c, s = jnp.cos(ang), jnp.sin(ang)
    return jnp.concatenate([c, c, c, c], axis=-1), jnp.concatenate([-s, s, -s, s], axis=-1)


def _nsa_prep_kernel_v2(q_ref, kv_ref, g_ref, cos_ref, sin_ref, qn_ref, kn_ref,
                     rows_ref, wrows_ref, kvb_ref, qc_ref, qr_ref, gate_ref):
    cos_t, sin_t = cos_ref[...], sin_ref[...]
    lo = lax.broadcasted_iota(jnp.int32, cos_t.shape, 1) < NSA_HD
    for c in range(NSA_HEADS // 2):
        y = _pair_rms(q_ref[:, c * LANE:(c + 1) * LANE], qn_ref[...]) * (NSA_HD ** -0.5)
        in_low_lanes = (2 * c) // NSA_GROUP == 0
        for src, dst in ((y, qc_ref), (_pair_rope(y, cos_t, sin_t) * LOG2E, qr_ref)):
            swapped = pltpu.roll(src, NSA_HD, axis=1)
            if in_low_lanes:
                even, odd = jnp.where(lo, src, 0.0), jnp.where(lo, swapped, 0.0)
            else:
                even, odd = jnp.where(lo, 0.0, swapped), jnp.where(lo, 0.0, src)
            dst[:, (2 * c) * LANE:(2 * c + 1) * LANE] = even.astype(BF16)
            dst[:, (2 * c + 1) * LANE:(2 * c + 2) * LANE] = odd.astype(BF16)
    for br in range(3):
        k = _pair_rms(kv_ref[:, br * 2 * LANE:br * 2 * LANE + LANE], kn_ref[br:br + 1, :])
        if br > 0:
            k = _pair_rope(k, cos_t, sin_t)
        v = kv_ref[:, br * 2 * LANE + LANE:(br + 1) * 2 * LANE]
        dst, off = (rows_ref, br * 2 * LANE) if br < 2 else (wrows_ref, 0)
        dst[:, off:off + LANE] = k
        dst[:, off + LANE:off + 2 * LANE] = v
        kvb_ref[:, br * 2 * LANE:br * 2 * LANE + LANE] = k.astype(BF16)
        kvb_ref[:, br * 2 * LANE + LANE:(br + 1) * 2 * LANE] = v.astype(BF16)
    gate_ref[...] = _sigmoid(g_ref[...])


def nsa_prep_v2(q_pre, kv_pre, gate_pre, cos_t, sin_t, qn, kn, tm):
    n_rows = q_pre.shape[0]
    tm = min(tm, n_rows)
    n_tab = cos_t.shape[0] // tm
    rows = lambda w: pl.BlockSpec((tm, w), lambda i: (i, 0))
    tab = pl.BlockSpec((tm, LANE), lambda i: (i % n_tab, 0))
    widths = (NSA_Q_W, 2 * LANE, NSA_KV_W, NSA_HEADS * LANE, NSA_HEADS * LANE, LANE)
    dtypes = (F32, F32, BF16, BF16, BF16, F32)
    return pl.pallas_call(
        _nsa_prep_kernel_v2, grid=(n_rows // tm,),
        in_specs=[rows(NSA_Q_W), rows(NSA_KV_W), rows(LANE), tab, tab,
                  pl.BlockSpec((1, LANE), lambda i: (0, 0)), pl.BlockSpec((3, LANE), lambda i: (0, 0))],
        out_specs=[rows(w) for w in widths],
        out_shape=[jax.ShapeDtypeStruct((n_rows, w), dt) for w, dt in zip(widths, dtypes)],
        compiler_params=_cparams(("parallel",), 40), name="nsa_prep",
    )(q_pre, kv_pre, gate_pre, cos_t, sin_t, jnp.tile(qn.reshape(1, NSA_HD), (1, 2)), jnp.tile(kn, (1, 2)))


def _select_blocks_t(imp_t, q_blk, n_sel):
    blk = lax.broadcasted_iota(jnp.int32, imp_t.shape, 0)
    visible = blk <= q_blk
    forced = jnp.where(blk == 0, 1.0, 0.0) + jnp.where(blk == q_blk, 1.0, 0.0) + jnp.where(blk == q_blk - 1, 1.0, 0.0)
    score = jnp.where(visible, imp_t + jnp.where(forced > 0.0, FORCED_BONUS, 0.0), NEG)
    rank = jnp.zeros(imp_t.shape, F32)
    for sp in range(n_sel):
        row = score[sp:sp + 1, :]
        tie = jnp.where(blk > sp, jnp.where(row == score, 1.0, 0.0), 0.0)
        rank = rank + jnp.where(row > score, 1.0, 0.0) + tie
    return jnp.where(visible, jnp.where(rank < SEL_TOPN, 1.0, 0.0), 0.0)


def _flash8_v2(qs, k_ref, v_ref, j_lo, j_hi, valid_fn, tq, tk):
    def body(j, carry):
        off = pl.multiple_of(j * tk, tk)
        k = k_ref[pl.ds(off, tk), :]
        v = v_ref[pl.ds(off, tk), :]
        valid = valid_fn(j)
        out = []
        for h, (m, l, acc) in enumerate(carry):
            s = jnp.where(valid[h // NSA_GROUP], _dot_t(qs[h], k), NEG)
            m_new = jnp.maximum(m, jnp.max(s, axis=-1, keepdims=True))
            alpha = jnp.exp2(m - m_new)
            p = jnp.exp2(s - m_new)
            l = alpha * l + jnp.sum(p, axis=-1, keepdims=True)
            acc = alpha * acc + _dot(p.astype(BF16), v)
            out.append((m_new, l, acc))
        return tuple(out)

    init = tuple((jnp.full((tq, 1), NEG, F32), jnp.zeros((tq, 1), F32), jnp.zeros((tq, LANE), F32))
                 for _ in range(NSA_HEADS))
    return [acc / l for _, l, acc in lax.fori_loop(j_lo, j_hi, body, init)]


def _nsa_prompt_kernel_v2(qc_ref, qr_ref, kvc_ref, ks_ref, vs_ref, kw_ref, vw_ref, g_ref, c2s_ref, e_ref, o_ref,
                       *, tq, n_sel):
    i = pl.program_id(1)
    t0 = i * tq
    n_rows = NSA_HEADS * tq
    stack = lambda ref: jnp.concatenate([ref[:, h * LANE:(h + 1) * LANE] for h in range(NSA_HEADS)], axis=0)
    qpos = t0 + lax.broadcasted_iota(jnp.int32, (tq, 1), 0)
    n_cmp = kvc_ref.shape[1]
    kc = kvc_ref[0, :, :LANE].astype(BF16)
    vc = kvc_ref[0, :, LANE:].astype(BF16)
    cend = lax.broadcasted_iota(jnp.int32, (1, n_cmp), 1) * CMP_STRIDE + (CMP_BLOCK - 1)
    s = _dot_t(stack(qc_ref), kc).reshape(NSA_KV_HEADS, NSA_GROUP, tq, n_cmp)
    p = _cmp_probs(s, (cend <= qpos)[None, None])
    o_cmp = _dot(p.reshape(n_rows, n_cmp).astype(BF16), vc)
    psum = p[:, 0]
    for g in range(1, NSA_GROUP):
        psum = psum + p[:, g]
    imp = jnp.dot(psum.reshape(NSA_KV_HEADS * tq, n_cmp), c2s_ref[...], preferred_element_type=F32,
                  precision=lax.Precision.HIGHEST)
    n_blk_rows = -(-n_sel // 8) * 8
    col = lax.broadcasted_iota(jnp.int32, (1, NSA_KV_HEADS * tq), 1)
    q_blk = lax.shift_right_logical(t0 + (col & (tq - 1)), 6)
    sel_t = _select_blocks_t(imp.T[:n_blk_rows], q_blk, n_sel)
    sel_t = jnp.concatenate([sel_t, jnp.zeros((LANE - n_blk_rows, NSA_KV_HEADS * tq), F32)], axis=0)
    sel = sel_t.T.astype(BF16)
    kiota = lax.broadcasted_iota(jnp.int32, (1, tq), 1)

    def sel_valid(j):
        causal = (j * tq + kiota) <= qpos
        return [jnp.where(causal, _dot(sel[k * tq:(k + 1) * tq], e_ref[j]), 0.0) > 0.5 for k in range(NSA_KV_HEADS)]

    def win_valid(j):
        rel = qpos - (j * tq + kiota)
        return [jnp.where(rel >= 0, rel, WINDOW) < WINDOW] * NSA_KV_HEADS

    qr = [qr_ref[:, h * LANE:(h + 1) * LANE] for h in range(NSA_HEADS)]
    o_sel = _flash8_v2(qr, ks_ref.at[0], vs_ref.at[0], 0, i + 1, sel_valid, tq, tq)
    o_win = _flash8_v2(qr, kw_ref.at[0], vw_ref.at[0], jnp.maximum(i - WINDOW // tq, 0), i + 1, win_valid, tq, tq)
    lo = lax.broadcasted_iota(jnp.int32, (tq, LANE), 1) < NSA_HD
    for c in range(NSA_HEADS // 2):
        pair = []
        for h in (2 * c, 2 * c + 1):
            rs = slice(h * tq, (h + 1) * tq)
            gt = g_ref[:, 3 * h:3 * h + 3]
            pair.append(o_cmp[rs] * gt[:, 0:1] + o_sel[h] * gt[:, 1:2] + o_win[h] * gt[:, 2:3])
        even, odd = pair
        if (2 * c) // NSA_GROUP == 0:
            blk = jnp.where(lo, even, pltpu.roll(odd, NSA_HD, axis=1))
        else:
            blk = jnp.where(lo, pltpu.roll(even, NSA_HD, axis=1), odd)
        o_ref[:, c * LANE:(c + 1) * LANE] = blk


def nsa_prompt_attn_v2(qc, qr, kvc, kvb, gates, n_batch, tq):
    n_rows = qc.shape[0]
    t = n_rows // n_batch
    tq = min(tq, t)
    nt = t // tq
    n_cmp = kvc.shape[1]
    n_sel = t // SEL_BLOCK
    assert n_cmp % LANE == 0 and n_sel <= LANE and tq & (tq - 1) == 0 and WINDOW % tq == 0
    c2s = jnp.asarray(_c2s(n_cmp, LANE))
    e = jnp.asarray(_expand(LANE, t).reshape(LANE, nt, tq).transpose(1, 0, 2), dtype=BF16)
    kvb3 = kvb.reshape(n_batch, t, NSA_KV_W)
    rows = lambda w: pl.BlockSpec((tq, w), lambda b, i: (b * nt + i, 0))
    kv = lambda c: pl.BlockSpec((1, t, LANE), lambda b, i: (b, 0, c))
    return pl.pallas_call(
        functools.partial(_nsa_prompt_kernel_v2, tq=tq, n_sel=n_sel), grid=(n_batch, nt),
        in_specs=[rows(NSA_HEADS * LANE), rows(NSA_HEADS * LANE),
                  pl.BlockSpec((1, n_cmp, CMP_W), lambda b, i: (b, 0, 0)),
                  kv(2), kv(3), kv(4), kv(5), rows(LANE),
                  pl.BlockSpec(c2s.shape, lambda b, i: (0, 0)), pl.BlockSpec(e.shape, lambda b, i: (0, 0, 0))],
        out_specs=rows(NSA_Q_W),
        out_shape=jax.ShapeDtypeStruct((n_rows, NSA_Q_W), F32),
        compiler_params=_cparams(("parallel", "parallel"), 48), name="nsa_prompt_attn",
    )(qc, qr, kvc, kvb3, kvb3, kvb3, kvb3, gates, c2s, e)


LOG2E = 1.4426950408889634
KEY_TILE = 256


def _head_rows(yt, head_in_pair, kv_head):
    own = yt[head_in_pair * NSA_HD:(head_in_pair + 1) * NSA_HD]
    zero = jnp.zeros_like(own)
    return jnp.concatenate([own, zero] if kv_head == 0 else [zero, own], axis=0)


def _nsa_prep_kernel(q_ref, kv_ref, g_ref, cos_ref, sin_ref, qn_ref, kn_ref,
                     rows_ref, wrows_ref, qct_ref, qrt_ref, gt_ref, kb_ref=None, vt_ref=None, *, tm):
    cos_t, sin_t = cos_ref[...], sin_ref[...]
    for c in range(NSA_HEADS // 2):
        y = _pair_rms(q_ref[:, c * LANE:(c + 1) * LANE], qn_ref[...]) * (NSA_HD ** -0.5)
        yr = _pair_rope(y, cos_t, sin_t) * LOG2E
        for src, dst in ((y, qct_ref), (yr, qrt_ref)):
            st = src.T
            for e in range(2):
                h = 2 * c + e
                dst[h * LANE:(h + 1) * LANE, :] = _head_rows(st, e, h // NSA_GROUP).astype(BF16)
    for br in range(3):
        k = _pair_rms(kv_ref[:, br * 2 * LANE:br * 2 * LANE + LANE], kn_ref[br:br + 1, :])
        if br > 0:
            k = _pair_rope(k, cos_t, sin_t)
        v = kv_ref[:, br * 2 * LANE + LANE:(br + 1) * 2 * LANE]
        dst, off = (rows_ref, br * 2 * LANE) if br < 2 else (wrows_ref, 0)
        dst[:, off:off + LANE] = k
        dst[:, off + LANE:off + 2 * LANE] = v
        if br > 0 and kb_ref is not None:
            kb_ref[:, (br - 1) * LANE:br * LANE] = k.astype(BF16)
            vt = v.T.astype(BF16)
            for jt in range(tm // KEY_TILE):
                vt_ref[br - 1, jt] = vt[:, jt * KEY_TILE:(jt + 1) * KEY_TILE]
    gt_ref[...] = _sigmoid(g_ref[...]).T


def nsa_prep(q_pre, kv_pre, gate_pre, cos_t, sin_t, qn, kn, tm, attn_operands):
    n_rows = q_pre.shape[0]
    tm = min(tm, n_rows)
    n_tab = cos_t.shape[0] // tm
    rows = lambda w: pl.BlockSpec((tm, w), lambda i: (i, 0))
    cols = lambda r: pl.BlockSpec((r, tm), lambda i: (0, i))
    tab = pl.BlockSpec((tm, LANE), lambda i: (i % n_tab, 0))
    out_specs = [rows(NSA_Q_W), rows(2 * LANE), cols(NSA_HEADS * LANE), cols(NSA_HEADS * LANE), cols(LANE)]
    out_shape = [jax.ShapeDtypeStruct((n_rows, NSA_Q_W), F32), jax.ShapeDtypeStruct((n_rows, 2 * LANE), F32),
                 jax.ShapeDtypeStruct((NSA_HEADS * LANE, n_rows), BF16),
                 jax.ShapeDtypeStruct((NSA_HEADS * LANE, n_rows), BF16),
                 jax.ShapeDtypeStruct((LANE, n_rows), F32)]
    if attn_operands:
        assert tm % KEY_TILE == 0
        out_specs += [rows(2 * LANE), pl.BlockSpec((2, tm // KEY_TILE, LANE, KEY_TILE), lambda i: (0, i, 0, 0))]
        out_shape += [jax.ShapeDtypeStruct((n_rows, 2 * LANE), BF16),
                      jax.ShapeDtypeStruct((2, n_rows // KEY_TILE, LANE, KEY_TILE), BF16)]
    return pl.pallas_call(
        functools.partial(_nsa_prep_kernel, tm=tm), grid=(n_rows // tm,),
        in_specs=[rows(NSA_Q_W), rows(NSA_KV_W), rows(LANE), tab, tab,
                  pl.BlockSpec((1, LANE), lambda i: (0, 0)), pl.BlockSpec((3, LANE), lambda i: (0, 0))],
        out_specs=out_specs, out_shape=out_shape,
        compiler_params=_cparams(("parallel",), 40), name="nsa_prep",
    )(q_pre, kv_pre, gate_pre, cos_t, sin_t, jnp.tile(qn.reshape(1, NSA_HD), (1, 2)), jnp.tile(kn, (1, 2)))


def _flash_t(qts, k_ref, vt_ref, j_lo, j_hi, valid_fn, tq):
    def body(j, carry):
        k = k_ref[pl.ds(pl.multiple_of(j * KEY_TILE, KEY_TILE), KEY_TILE), :]
        vt = vt_ref[j]
        valid = valid_fn(j)
        out = []
        for h, (m, l, acc) in enumerate(carry):
            s = jnp.where(valid[h // NSA_GROUP], _dot(k, qts[h]), NEG)
            m_new = jnp.maximum(m, jnp.max(s, axis=0, keepdims=True))
            alpha = jnp.exp2(m - m_new)
            p = jnp.exp2(s - m_new)
            l = alpha * l + jnp.sum(p, axis=0, keepdims=True)
            acc = alpha * acc + _dot(vt, p.astype(BF16))
            out.append((m_new, l, acc))
        return tuple(out)

    init = tuple((jnp.full((1, tq), NEG, F32), jnp.zeros((1, tq), F32), jnp.zeros((LANE, tq), F32))
                 for _ in range(NSA_HEADS))
    return [acc / l for _, l, acc in lax.fori_loop(j_lo, j_hi, body, init)]


def _nsa_prompt_kernel(qct_ref, qrt_ref, kvc_ref, ks_ref, kw_ref, vst_ref, vwt_ref, gt_ref, c2st_ref, et_ref,
                       ot_ref, *, tq, n_sel):
    i = pl.program_id(1)
    t0 = i * tq
    qpos = t0 + lax.broadcasted_iota(jnp.int32, (1, tq), 1)
    n_cmp = kvc_ref.shape[1]
    kc = kvc_ref[0, :, :LANE].astype(BF16)
    vct = kvc_ref[0, :, LANE:].T.astype(BF16)
    cend = lax.broadcasted_iota(jnp.int32, (n_cmp, 1), 0) * CMP_STRIDE + (CMP_BLOCK - 1)
    valid_c = cend <= qpos
    psum = [jnp.zeros((n_cmp, tq), F32) for _ in range(NSA_KV_HEADS)]
    o_cmp = []
    for h in range(NSA_HEADS):
        s = jnp.where(valid_c, _dot(kc, qct_ref[h * LANE:(h + 1) * LANE, :]), NEG)
        p = jnp.where(valid_c, jnp.exp(s - jnp.max(s, axis=0, keepdims=True)), 0.0)
        p = p / jnp.maximum(jnp.sum(p, axis=0, keepdims=True), 1e-30)
        psum[h // NSA_GROUP] = psum[h // NSA_GROUP] + p
        o_cmp.append(_dot(vct, p.astype(BF16)))
    n_blk_rows = -(-n_sel // 8) * 8
    imp_t = jnp.concatenate(
        [jnp.dot(c2st_ref[...], ps, preferred_element_type=F32, precision=lax.Precision.HIGHEST)[:n_blk_rows]
         for ps in psum], axis=1)
    col = lax.broadcasted_iota(jnp.int32, (1, NSA_KV_HEADS * tq), 1)
    sel_t = _select_blocks_t(imp_t, lax.shift_right_logical(t0 + (col & (tq - 1)), 6), n_sel)
    sel_t = jnp.concatenate([sel_t, jnp.zeros((LANE - n_blk_rows, NSA_KV_HEADS * tq), F32)], axis=0).astype(BF16)
    krow = lax.broadcasted_iota(jnp.int32, (KEY_TILE, 1), 0)

    def sel_valid(j):
        causal = (j * KEY_TILE + krow) <= qpos
        return [jnp.where(causal, _dot(et_ref[j], sel_t[:, k * tq:(k + 1) * tq]), 0.0) > 0.5
                for k in range(NSA_KV_HEADS)]

    def win_valid(j):
        rel = qpos - (j * KEY_TILE + krow)
        return [jnp.where(rel >= 0, rel, WINDOW) < WINDOW] * NSA_KV_HEADS

    qrt = [qrt_ref[h * LANE:(h + 1) * LANE, :] for h in range(NSA_HEADS)]
    n_kt = (t0 + tq) // KEY_TILE
    o_sel = _flash_t(qrt, ks_ref.at[0], vst_ref.at[0], 0, n_kt, sel_valid, tq)
    o_win = _flash_t(qrt, kw_ref.at[0], vwt_ref.at[0], jnp.maximum((t0 - WINDOW) // KEY_TILE, 0), n_kt,
                     win_valid, tq)
    for h in range(NSA_HEADS):
        g = gt_ref[3 * h:3 * h + 3, :]
        o = o_cmp[h] * g[0:1] + o_sel[h] * g[1:2] + o_win[h] * g[2:3]
        k = h // NSA_GROUP
        ot_ref[h * NSA_HD:(h + 1) * NSA_HD, :] = o[k * NSA_HD:(k + 1) * NSA_HD]


def nsa_prompt_attn(qct, qrt, kvc, kb, vt, gates_t, n_batch, tq):
    n_rows = qct.shape[1]
    t = n_rows // n_batch
    tq = min(tq, t)
    nt = t // tq
    n_kt = t // KEY_TILE
    n_cmp = kvc.shape[1]
    n_sel = t // SEL_BLOCK
    assert n_cmp % LANE == 0 and n_sel <= LANE and tq & (tq - 1) == 0 and tq % KEY_TILE == 0
    c2st = jnp.asarray(_c2s(n_cmp, LANE).T)
    et = jnp.asarray(_expand(LANE, t).T.reshape(n_kt, KEY_TILE, LANE), dtype=BF16)
    kb3 = kb.reshape(n_batch, t, 2 * LANE)
    vt4 = vt.reshape(2, n_batch, n_kt, LANE, KEY_TILE)
    cols = lambda r: pl.BlockSpec((r, tq), lambda b, i: (0, b * nt + i))
    kspec = lambda c: pl.BlockSpec((1, t, LANE), lambda b, i: (b, 0, c))
    vspec = lambda c: pl.BlockSpec((None, 1, n_kt, LANE, KEY_TILE), lambda b, i: (c, b, 0, 0, 0))
    return pl.pallas_call(
        functools.partial(_nsa_prompt_kernel, tq=tq, n_sel=n_sel), grid=(n_batch, nt),
        in_specs=[cols(NSA_HEADS * LANE), cols(NSA_HEADS * LANE),
                  pl.BlockSpec((1, n_cmp, CMP_W), lambda b, i: (b, 0, 0)),
                  kspec(0), kspec(1), vspec(0), vspec(1), cols(LANE),
                  pl.BlockSpec(c2st.shape, lambda b, i: (0, 0)), pl.BlockSpec(et.shape, lambda b, i: (0, 0, 0))],
        out_specs=cols(NSA_Q_W),
        out_shape=jax.ShapeDtypeStruct((NSA_Q_W, n_rows), F32),
        compiler_params=_cparams(("parallel", "parallel"), 48), name="nsa_prompt_attn",
    )(qct, qrt, kvc, kb3, kb3, vt4, vt4, gates_t, c2st, et)


SROWS = NSA_HEADS * 8
HALF_W = 2 * NSA_KV_HEADS * NSA_HD


def _page_copies(cache_ref, pt_ref, targets, sem, b, slot, n_pages, page):
    return [pltpu.make_async_copy(cache_ref.at[pt_ref[b, pg], :, pl.ds(col0, buf.shape[2])],
                                  buf.at[slot, pl.ds(pg * page, page), :], sem.at[slot])
            for pg in range(n_pages) for col0, buf in targets]


def _stream_pages(cache_ref, pt_ref, targets, sem, n_pages, page):
    b = pl.program_id(0)
    nb = pl.num_programs(0)
    slot = lax.rem(b, 2)

    @pl.when(b == 0)
    def _():
        for c in _page_copies(cache_ref, pt_ref, targets, sem, 0, 0, n_pages, page):
            c.start()

    @pl.when(b + 1 < nb)
    def _():
        for c in _page_copies(cache_ref, pt_ref, targets, sem, b + 1, 1 - slot, n_pages, page):
            c.start()

    for c in _page_copies(cache_ref, pt_ref, targets, sem, b, slot, n_pages, page):
        c.wait()
    return slot


def _nsa_s1_kernel(pt_ref, cache_ref, q_ref, w_ref, pe_ref, c2s_ref, ocmp_ref, sel_ref, bufk, bufv, sem,
                   *, n_pages, page, ts, n_sel):
    past = n_pages * page
    slot = _stream_pages(cache_ref, pt_ref, [(0, bufk), (LANE, bufv)], sem, n_pages, page)
    n_chunk = past // CMP_STRIDE
    kv = _compress(lambda j: bufk[slot, pl.ds(j, n_chunk, stride=CMP_STRIDE), :],
                   lambda j: bufv[slot, pl.ds(j, n_chunk, stride=CMP_STRIDE), :], w_ref, pe_ref, n_chunk)
    kc = kv[:, :LANE].astype(BF16)
    vc = kv[:, LANE:].astype(BF16)
    tok = lax.broadcasted_iota(jnp.int32, (SROWS, 1), 0) & (ts - 1)
    qpos = past + tok
    cend = lax.broadcasted_iota(jnp.int32, (1, n_chunk), 1) * CMP_STRIDE + (CMP_BLOCK - 1)
    p = _cmp_probs(_dot_t(q_ref[0], kc), cend <= qpos)
    ocmp_ref[0] = _dot(p.astype(BF16), vc)
    psum = []
    for k in range(NSA_KV_HEADS):
        acc = p[k * NSA_GROUP * ts:k * NSA_GROUP * ts + ts]
        for g in range(1, NSA_GROUP):
            r = (k * NSA_GROUP + g) * ts
            acc = acc + p[r:r + ts]
        psum.append(acc)
    psum = jnp.concatenate(psum, axis=0)
    imp = jnp.dot(psum, c2s_ref[...], preferred_element_type=F32, precision=lax.Precision.HIGHEST)
    q_blk = lax.shift_right_logical(past + (lax.broadcasted_iota(jnp.int32, (NSA_KV_HEADS * ts, 1), 0) & (ts - 1)), 6)
    sel_ref[0] = _select_blocks(imp, q_blk, n_sel)


def _nsa_s2_kernel(pt_ref, cache_ref, q_ref, sel_ref, new_ref, win_ref, wnew_ref, ocmp_ref, g_ref, e_ref,
                   o_ref, buf, sem, *, n_pages, page, ts, wb):
    past = n_pages * page
    slot = _stream_pages(cache_ref, pt_ref, [(HALF_W, buf)], sem, n_pages, page)
    q = q_ref[0]
    tok = lax.broadcasted_iota(jnp.int32, (SROWS, 1), 0) & (ts - 1)
    sel = sel_ref[0]
    sel_rows = jnp.concatenate([sel[k * ts:(k + 1) * ts] for k in range(NSA_KV_HEADS) for _ in range(NSA_GROUP)],
                               axis=0)
    n_new = new_ref.shape[1]
    jnew = lax.broadcasted_iota(jnp.int32, (1, n_new), 1)

    def attend(s_past, valid_past, v_past, s_new, valid_new, v_new):
        s_past = jnp.where(valid_past, s_past, NEG)
        s_new = jnp.where(valid_new, s_new, NEG)
        m = jnp.maximum(jnp.max(s_past, axis=-1, keepdims=True), jnp.max(s_new, axis=-1, keepdims=True))
        p_past = jnp.exp2(s_past - m)
        p_new = jnp.exp2(s_new - m)
        l = jnp.sum(p_past, axis=-1, keepdims=True) + jnp.sum(p_new, axis=-1, keepdims=True)
        return (_dot(p_past.astype(BF16), v_past) + _dot(p_new.astype(BF16), v_new)) / l

    k_past = buf[slot, :, 0:LANE].astype(BF16)
    v_past = buf[slot, :, LANE:HALF_W].astype(BF16)
    mask_past = _dot(sel_rows.astype(BF16), e_ref[...]) > 0.5
    blk_new = past // SEL_BLOCK
    valid_new = (sel_rows[:, blk_new:blk_new + 1] > 0.5) & (jnew <= tok)
    k_new = new_ref[0, :, 0:LANE].astype(BF16)
    v_new = new_ref[0, :, LANE:HALF_W].astype(BF16)
    o_sel = attend(_dot_t(q, k_past), mask_past, v_past, _dot_t(q, k_new), valid_new, v_new)
    u = lax.broadcasted_iota(jnp.int32, (1, wb), 1)
    rel = wb + tok - u
    valid_w = (rel >= 0) & (rel < WINDOW) & (past - wb + u >= 0)
    kw = win_ref[0, :, 0:LANE].astype(BF16)
    vw = win_ref[0, :, LANE:HALF_W].astype(BF16)
    kwn = wnew_ref[0, :, 0:LANE].astype(BF16)
    vwn = wnew_ref[0, :, LANE:HALF_W].astype(BF16)
    o_win = attend(_dot_t(q, kw), valid_w, vw, _dot_t(q, kwn), jnew <= tok, vwn)
    gt = g_ref[0]
    o = ocmp_ref[0] * gt[:, 0:1] + o_sel * gt[:, 1:2] + o_win * gt[:, 2:3]
    half = SROWS // NSA_KV_HEADS
    for k in range(NSA_KV_HEADS):
        o_ref[0, k * half:(k + 1) * half, :] = o[k * half:(k + 1) * half, k * NSA_HD:(k + 1) * NSA_HD]


def nsa_sample_attn(page_table, cache, qc2, qr2, new_sel, win, new_win, gates, cmp_w, cmp_pe, ts):
    bs, n_pages = page_table.shape
    page = cache.shape[1]
    past = n_pages * page
    wb = win.shape[1]
    assert past % SEL_BLOCK == 0 and ts <= CMP_STRIDE and ts & (ts - 1) == 0 and SROWS == NSA_HEADS * ts
    n_chunk = past // CMP_STRIDE
    n_sel = past // SEL_BLOCK + 1
    n_sel_pad = -(-n_sel // LANE) * LANE
    c2s = jnp.asarray(_c2s(n_chunk, n_sel_pad))
    e = jnp.asarray(_expand(n_sel_pad, past), dtype=BF16)
    sems = pltpu.SemaphoreType.DMA((2,))
    scratch1 = [pltpu.VMEM((2, past, LANE), F32), pltpu.VMEM((2, past, LANE), F32), sems]
    scratch2 = [pltpu.VMEM((2, past, HALF_W), F32), sems]
    seq3 = lambda b, pt: (b, 0, 0)
    o_cmp, sel = pl.pallas_call(
        functools.partial(_nsa_s1_kernel, n_pages=n_pages, page=page, ts=ts, n_sel=n_sel),
        grid_spec=pltpu.PrefetchScalarGridSpec(
            num_scalar_prefetch=1, grid=(bs,),
            in_specs=[pl.BlockSpec(memory_space=pl.ANY),
                      pl.BlockSpec((1, SROWS, LANE), seq3),
                      pl.BlockSpec(cmp_w.shape, lambda b, pt: (0, 0, 0, 0, 0)),
                      pl.BlockSpec(cmp_pe.shape, lambda b, pt: (0, 0, 0, 0)),
                      pl.BlockSpec(c2s.shape, lambda b, pt: (0, 0))],
            out_specs=[pl.BlockSpec((1, SROWS, LANE), seq3),
                       pl.BlockSpec((1, NSA_KV_HEADS * ts, n_sel_pad), seq3)],
            scratch_shapes=scratch1),
        out_shape=[jax.ShapeDtypeStruct((bs, SROWS, LANE), F32),
                   jax.ShapeDtypeStruct((bs, NSA_KV_HEADS * ts, n_sel_pad), F32)],
        compiler_params=_cparams(("arbitrary",), 48), name="nsa_sample_cmp",
    )(page_table, cache, qc2, cmp_w, cmp_pe, c2s)
    n_new = new_sel.shape[1]
    return pl.pallas_call(
        functools.partial(_nsa_s2_kernel, n_pages=n_pages, page=page, ts=ts, wb=wb),
        grid_spec=pltpu.PrefetchScalarGridSpec(
            num_scalar_prefetch=1, grid=(bs,),
            in_specs=[pl.BlockSpec(memory_space=pl.ANY),
                      pl.BlockSpec((1, SROWS, LANE), seq3),
                      pl.BlockSpec((1, NSA_KV_HEADS * ts, n_sel_pad), seq3),
                      pl.BlockSpec((1, n_new, HALF_W), seq3),
                      pl.BlockSpec((1, wb, HALF_W), seq3),
                      pl.BlockSpec((1, n_new, HALF_W), seq3),
                      pl.BlockSpec((1, SROWS, LANE), seq3),
                      pl.BlockSpec((1, SROWS, 3), seq3),
                      pl.BlockSpec(e.shape, lambda b, pt: (0, 0))],
            out_specs=pl.BlockSpec((1, SROWS, NSA_HD), seq3),
            scratch_shapes=scratch2),
        out_shape=jax.ShapeDtypeStruct((bs, SROWS, NSA_HD), F32),
        compiler_params=_cparams(("arbitrary",), 56), name="nsa_sample_sel",
    )(page_table, cache, qr2, sel, new_sel, win, new_win, o_cmp, gates, e)


def _in_proj(x2d, pos_rows, p, tm):
    q_pre, kv_pre, gate_pre, qkv_pre, z_pre, ba_pre = rms_matmul(x2d, p["attn_norm_w"], p["w_in_parts"], tm)
    cos_t, sin_t = _rope_tables(pos_rows)
    nsa = nsa_prep_v2(q_pre, kv_pre, gate_pre, cos_t, sin_t, p["nsa_q_norm_w"], p["nsa_k_norm_w"], tm)
    return nsa, qkv_pre, z_pre, ba_pre


def _post_mixer(x2d, o_nsa, o_gdn, mem_kv, p, n_seq, tq):
    h = matmul_res([o_nsa, o_gdn], p["w_out"], x2d, 512)
    (qx,) = rms_matmul(h, p["xattn_norm_w"], [p["w_xq"]], 512)
    return matmul_res([xattn_core(qx, mem_kv, p["xq_norm_w"], n_seq, tq)], p["w_xo"], h, 512)


def _ffn(h, p, prev, n_seq, shift, tm):
    return conv_ffn(h, p["ffn_norm_w"], p["w_up"], p["ffn_conv_w"], p["ffn_conv_b"], p["w_down"], prev,
                    n_seq, shift, tm)


def _layer_prompt(x, mem, p):
    nb, t, d = x.shape
    x2d = x.reshape(nb * t, d)
    (rows, wrows, kvb, qc, qr, gates), qkv_pre, z_pre, ba_pre = _in_proj(x2d, jnp.arange(t), p, 512)
    kvc = compress_prompt(rows.reshape(nb, t, 2 * CMP_W), p["cmp_bd"], p["cmp_pe2"])
    o_nsa = nsa_prompt_attn_v2(qc, qr, kvc, kvb, gates, nb, 256)
    o_gdn, gdn_buf, gdn_s = gdn(qkv_pre, z_pre, ba_pre, jnp.zeros((nb, GDN_CONV - 1, GDN_C3), F32),
                                jnp.zeros((nb, GDN_HEADS, GDN_HD, GDN_HD), F32), p["gdn_conv_w"], p["gdn_a_log"],
                                p["gdn_dt_bias"], p["gdn_norm_w"], nb, 256)
    mem_kv = mem_kv_proj(mem.reshape(-1, d), p["mem_norm_w"], p["w_xk"], p["w_xv"], p["xk_norm_w"])
    mem_kv = mem_kv.reshape(nb, mem.shape[1], 2 * XA_HEADS * XA_HD)
    h = _post_mixer(x2d, o_nsa, o_gdn, mem_kv, p, nb, 256)
    y, ffn_buf = _ffn(h, p, jnp.zeros((nb, FFN_CONV - 1, 2 * D_FF), F32), nb, 1, 512)
    keep = min(WINDOW, t)
    win_state = wrows.reshape(nb, t, 2 * LANE)[:, t - keep:].reshape(nb, keep, 2, NSA_KV_HEADS, NSA_HD)
    return (y.reshape(nb, t, d), rows.reshape(nb, t, 4, NSA_KV_HEADS, NSA_HD), win_state, gdn_s, gdn_buf, ffn_buf,
            mem_kv.reshape(nb, mem.shape[1], 2, XA_HEADS, XA_HD))


def _layer_sample(x, cache_kv, cache_win, state, gdn_cache, ffn_cache, mem_kv, page_table, p):
    bs, ts, d = x.shape
    n_pool, page = cache_kv.shape[:2]
    past = page_table.shape[1] * page
    x2d = x.reshape(bs * ts, d)
    pos_rows = jnp.tile(past + jnp.arange(ts), bs)
    (rows, wrows, _, qc, qr, gates), qkv_pre, z_pre, ba_pre = _in_proj(x2d, pos_rows, p, bs * ts)

    def stack_q(a):
        return a.reshape(bs, ts, NSA_HEADS, LANE).transpose(0, 2, 1, 3).reshape(bs, SROWS, LANE)

    def pad_new(a):
        return jnp.pad(a.reshape(bs, ts, HALF_W), ((0, 0), (0, LANE - ts), (0, 0)))

    wb = cache_win.shape[1]
    gates3 = gates[:, :NSA_GATE_W].reshape(bs, ts, NSA_HEADS, 3).transpose(0, 2, 1, 3).reshape(bs, SROWS, 3)
    o2 = nsa_sample_attn(page_table, cache_kv.reshape(n_pool, page, 2 * HALF_W), stack_q(qc), stack_q(qr),
                         pad_new(rows[:, HALF_W:]), cache_win.reshape(bs, wb, HALF_W), pad_new(wrows),
                         gates3, p["cmp_bd"], p["cmp_pe2"], ts)
    rows = rows.reshape(bs, ts, 4, NSA_KV_HEADS, NSA_HD)
    wrows = wrows.reshape(bs, ts, 2, NSA_KV_HEADS, NSA_HD)
    o_nsa = o2.reshape(bs, NSA_HEADS, ts, NSA_HD).transpose(0, 2, 1, 3).reshape(bs * ts, NSA_Q_W)
    o_gdn, gdn_buf, gdn_s = gdn(qkv_pre, z_pre, ba_pre, gdn_cache, state, p["gdn_conv_w"], p["gdn_a_log"],
                                p["gdn_dt_bias"], p["gdn_norm_w"], bs, ts)
    h = _post_mixer(x2d, o_nsa, o_gdn, mem_kv.reshape(bs, mem_kv.shape[1], -1), p, bs, ts)
    h_tm = h.reshape(bs, ts, d).transpose(1, 0, 2).reshape(ts * bs, d)
    prev = ffn_cache.transpose(1, 0, 2).reshape(1, (FFN_CONV - 1) * bs, 2 * D_FF)
    y, ffn_buf = _ffn(h_tm, p, prev, 1, bs, ts * bs)
    y = y.reshape(ts, bs, d).transpose(1, 0, 2)
    ffn_buf = ffn_buf.reshape(FFN_CONV - 1, bs, 2 * D_FF).transpose(1, 0, 2)
    wall = jnp.concatenate([cache_win, wrows], axis=1)
    keep = min(WINDOW, wb + ts)
    return y, rows, wall[:, wb + ts - keep:], gdn_s, gdn_buf, ffn_buf


def _prep_params(l, attn_norm_w, w_in, nsa_q_norm_w, nsa_k_norm_w, cmp_pe, cmp_w, gdn_conv_w, gdn_a_log,
                 gdn_dt_bias, gdn_norm_w, w_out, mem_norm_w, w_xk, w_xv, xk_norm_w, xattn_norm_w, w_xq,
                 xq_norm_w, w_xo, ffn_norm_w, w_up, ffn_conv_w, ffn_conv_b, w_down):
    wi = w_in[l]
    cuts = np.cumsum([0, NSA_Q_W, NSA_KV_W, NSA_GATE_W, 3 * GDN_W, GDN_W, 2 * GDN_HEADS])
    parts = []
    for a, b in zip(cuts[:-1], cuts[1:]):
        w = wi[:, a:b]
        padn = -(-(b - a) // LANE) * LANE - (b - a)
        parts.append(jnp.pad(w, ((0, 0), (0, padn))).astype(BF16))
    cmp_bd, cmp_pe2 = _cmp_weights(cmp_pe[l], cmp_w[l])
    bf = lambda w: w[l].astype(BF16)
    return dict(attn_norm_w=attn_norm_w[l], w_in_parts=parts, nsa_q_norm_w=nsa_q_norm_w[l],
                nsa_k_norm_w=nsa_k_norm_w[l], cmp_bd=cmp_bd, cmp_pe2=cmp_pe2, gdn_conv_w=gdn_conv_w[l],
                gdn_a_log=gdn_a_log[l], gdn_dt_bias=gdn_dt_bias[l], gdn_norm_w=gdn_norm_w[l], w_out=bf(w_out),
                mem_norm_w=mem_norm_w[l], w_xk=bf(w_xk), w_xv=bf(w_xv), xk_norm_w=xk_norm_w[l],
                xattn_norm_w=xattn_norm_w[l], w_xq=bf(w_xq), xq_norm_w=xq_norm_w[l], w_xo=bf(w_xo),
                ffn_norm_w=ffn_norm_w[l], w_up=bf(w_up), ffn_conv_w=ffn_conv_w[l], ffn_conv_b=ffn_conv_b[l],
                w_down=bf(w_down))


def kernel(x_prompt, x_sample, mem_prompt, cache_nsa_kv, cache_nsa_win, state_gdn, cache_gdn_conv, cache_ffn_conv,
           cache_mem_kv, page_table, attn_norm_w, w_in, nsa_q_norm_w, nsa_k_norm_w, cmp_pe, cmp_w, gdn_conv_w,
           gdn_a_log, gdn_dt_bias, gdn_norm_w, w_out, mem_norm_w, w_xk, w_xv, xk_norm_w, xattn_norm_w, w_xq,
           xq_norm_w, w_xo, ffn_norm_w, w_up, ffn_conv_w, ffn_conv_b, w_down):
    weights = (attn_norm_w, w_in, nsa_q_norm_w, nsa_k_norm_w, cmp_pe, cmp_w, gdn_conv_w, gdn_a_log, gdn_dt_bias,
               gdn_norm_w, w_out, mem_norm_w, w_xk, w_xv, xk_norm_w, xattn_norm_w, w_xq, xq_norm_w, w_xo,
               ffn_norm_w, w_up, ffn_conv_w, ffn_conv_b, w_down)
    depth = cache_nsa_kv.shape[0]
    hp, hs = x_prompt, x_sample
    outs_p, outs_s = [], []
    for l in range(depth):
        p = _prep_params(l, *weights)
        res_p = _layer_prompt(hp, mem_prompt, p)
        hp = res_p[0]
        outs_p.append(res_p[1:])
        res_s = _layer_sample(hs, cache_nsa_kv[l], cache_nsa_win[l], state_gdn[l], cache_gdn_conv[l],
                              cache_ffn_conv[l], cache_mem_kv[l], page_table, p)
        hs = res_s[0]
        outs_s.append(res_s[1:])
    stack = lambda outs, i: jnp.stack([o[i] for o in outs])
    return ((hp, hs) + tuple(stack(outs_p, i) for i in range(6)) + tuple(stack(outs_s, i) for i in range(5)))
```

```python
import functools
import math

import numpy as np
import jax
import jax.numpy as jnp
from jax import lax
from jax.experimental import pallas as pl
from jax.experimental.pallas import tpu as pltpu

F32 = jnp.float32
BF16 = jnp.bfloat16

D_MODEL = 1024
NSA_HEADS = 8
NSA_KV_HEADS = 2
NSA_GROUP = NSA_HEADS // NSA_KV_HEADS
NSA_HD = 64
CMP_BLOCK = 32
CMP_STRIDE = 16
SEL_BLOCK = 64
SEL_TOPN = 8
WINDOW = 512
FORCED_BONUS = 1e4
GDN_HEADS = 4
GDN_HD = 128
GDN_CONV = 4
GDN_CHUNK = 64
GDN_W = GDN_HEADS * GDN_HD
XA_HEADS = 4
XA_HD = 256
D_FF = 2816
FFN_CONV = 3
ROPE_THETA = 10000.0
EPS = 1e-6
NEG = -1e30

NSA_Q_W = NSA_HEADS * NSA_HD
NSA_KV_W = 3 * 2 * NSA_KV_HEADS * NSA_HD
NSA_GATE_W = 3 * NSA_HEADS
LANE = 128
MIB = 1 << 20


def _cparams(sem, vmem_mib):
    return pltpu.CompilerParams(dimension_semantics=sem, vmem_limit_bytes=vmem_mib * MIB)


def _sigmoid(x):
    return 1.0 / (1.0 + jnp.exp(-x))


def _dot(a, b):
    return jnp.dot(a, b, preferred_element_type=F32)


def _dot_t(a, b):
    return lax.dot_general(a, b, (((1,), (1,)), ((), ())), preferred_element_type=F32)


def _split_bf16(a):
    hi = a.astype(BF16)
    lo = (a - hi.astype(F32)).astype(BF16)
    return hi, lo


def _dot3(a, b):
    ah, al = _split_bf16(a)
    bh, bl = _split_bf16(b)
    return _dot(ah, bh) + (_dot(ah, bl) + _dot(al, bh))


def _col_chunk(n):
    for c in (512, 256, 128):
        if n % c == 0:
            return c
    return n


def _rms_mm_kernel(x_ref, g_ref, *refs, n_out):
    w_refs, o_refs = refs[:n_out], refs[n_out:]
    x = x_ref[...]
    xn = (x * lax.rsqrt(jnp.mean(x * x, axis=-1, keepdims=True) + EPS) * g_ref[...]).astype(BF16)
    for w_ref, o_ref in zip(w_refs, o_refs):
        n = w_ref.shape[1]
        ch = _col_chunk(n)
        for c in range(0, n, ch):
            o_ref[:, c:c + ch] = _dot(xn, w_ref[:, c:c + ch])


def rms_matmul(x, g, ws, tm):
    n_rows, d = x.shape
    tm = min(tm, n_rows)
    in_specs = [pl.BlockSpec((tm, d), lambda i: (i, 0)), pl.BlockSpec((1, d), lambda i: (0, 0))]
    in_specs += [pl.BlockSpec(w.shape, lambda i: (0, 0)) for w in ws]
    out_specs = [pl.BlockSpec((tm, w.shape[1]), lambda i: (i, 0)) for w in ws]
    out_shape = [jax.ShapeDtypeStruct((n_rows, w.shape[1]), F32) for w in ws]
    return pl.pallas_call(
        functools.partial(_rms_mm_kernel, n_out=len(ws)),
        grid=(n_rows // tm,), in_specs=in_specs, out_specs=out_specs, out_shape=out_shape,
        compiler_params=_cparams(("parallel",), 56), name="rms_matmul",
    )(x, g.reshape(1, d), *ws)


def _mm_res_kernel(*refs, n_in):
    a_refs, (w_ref, r_ref, o_ref) = refs[:n_in], refs[n_in:]
    a = [a_ref[...].astype(BF16) for a_ref in a_refs]
    n = w_ref.shape[1]
    ch = _col_chunk(n)
    for c in range(0, n, ch):
        acc = r_ref[:, c:c + ch]
        k0 = 0
        for x in a:
            acc = acc + _dot(x, w_ref[k0:k0 + x.shape[1], c:c + ch])
            k0 += x.shape[1]
        o_ref[:, c:c + ch] = acc


def matmul_res(a_list, w, res, tm):
    n_rows = res.shape[0]
    k, n = w.shape
    assert sum(a.shape[1] for a in a_list) == k
    tm = min(tm, n_rows)
    return pl.pallas_call(
        functools.partial(_mm_res_kernel, n_in=len(a_list)), grid=(n_rows // tm,),
        in_specs=[pl.BlockSpec((tm, a.shape[1]), lambda i: (i, 0)) for a in a_list]
        + [pl.BlockSpec((k, n), lambda i: (0, 0)), pl.BlockSpec((tm, n), lambda i: (i, 0))],
        out_specs=pl.BlockSpec((tm, n), lambda i: (i, 0)),
        out_shape=jax.ShapeDtypeStruct((n_rows, n), F32),
        compiler_params=_cparams(("parallel",), 40), name="matmul_res",
    )(*a_list, w, res)


def _memkv_kernel(x_ref, g_ref, wk_ref, wv_ref, kn_ref, o_ref):
    x = x_ref[...]
    xn = (x * lax.rsqrt(jnp.mean(x * x, axis=-1, keepdims=True) + EPS) * g_ref[...]).astype(BF16)
    xa_w = XA_HEADS * XA_HD
    for h in range(XA_HEADS):
        sl = slice(h * XA_HD, (h + 1) * XA_HD)
        k = _dot(xn, wk_ref[:, sl])
        k = k * lax.rsqrt(jnp.mean(k * k, axis=-1, keepdims=True) + EPS) * kn_ref[...]
        o_ref[:, sl] = k
        o_ref[:, xa_w + h * XA_HD: xa_w + (h + 1) * XA_HD] = _dot(xn, wv_ref[:, sl])


def mem_kv_proj(mem, g, wk, wv, kn, tm=256):
    n_rows, d = mem.shape
    xa_w = XA_HEADS * XA_HD
    return pl.pallas_call(
        _memkv_kernel, grid=(n_rows // tm,),
        in_specs=[pl.BlockSpec((tm, d), lambda i: (i, 0)), pl.BlockSpec((1, d), lambda i: (0, 0)),
                  pl.BlockSpec((d, xa_w), lambda i: (0, 0)), pl.BlockSpec((d, xa_w), lambda i: (0, 0)),
                  pl.BlockSpec((1, XA_HD), lambda i: (0, 0))],
        out_specs=pl.BlockSpec((tm, 2 * xa_w), lambda i: (i, 0)),
        out_shape=jax.ShapeDtypeStruct((n_rows, 2 * xa_w), F32),
        compiler_params=_cparams(("parallel",), 40), name="mem_kv_proj",
    )(mem, g.reshape(1, d), wk, wv, kn.reshape(1, XA_HD))


def _xattn_kernel(q_ref, kv_ref, qn_ref, o_ref):
    xa_w = XA_HEADS * XA_HD
    for h in range(XA_HEADS):
        sl = slice(h * XA_HD, (h + 1) * XA_HD)
        q = q_ref[:, sl]
        q = q * lax.rsqrt(jnp.mean(q * q, axis=-1, keepdims=True) + EPS) * qn_ref[...]
        qb = (q * (XA_HD ** -0.5)).astype(BF16)
        k = kv_ref[0, :, sl].astype(BF16)
        v = kv_ref[0, :, xa_w + h * XA_HD: xa_w + (h + 1) * XA_HD].astype(BF16)
        s = _dot_t(qb, k)
        p = jnp.exp(s - jnp.max(s, axis=-1, keepdims=True))
        o = _dot(p.astype(BF16), v) / jnp.sum(p, axis=-1, keepdims=True)
        o_ref[:, sl] = o


def xattn_core(q_pre, kv, qn, n_batch, tq):
    n_rows, xa_w = q_pre.shape
    t = n_rows // n_batch
    tq = min(tq, t)
    nt = t // tq
    m = kv.shape[1]
    return pl.pallas_call(
        _xattn_kernel, grid=(n_batch, nt),
        in_specs=[pl.BlockSpec((tq, xa_w), lambda b, i: (b * nt + i, 0)),
                  pl.BlockSpec((1, m, 2 * xa_w), lambda b, i: (b, 0, 0)),
                  pl.BlockSpec((1, XA_HD), lambda b, i: (0, 0))],
        out_specs=pl.BlockSpec((tq, xa_w), lambda b, i: (b * nt + i, 0)),
        out_shape=jax.ShapeDtypeStruct((n_rows, xa_w), F32),
        compiler_params=_cparams(("parallel", "parallel"), 40), name="xattn_core",
    )(q_pre, kv, qn.reshape(1, XA_HD))


FFN_ACT_CHUNK = 256


def _ffn_kernel(h_ref, g_ref, wup_ref, cw_ref, cb_ref, wdn_ref, prev_ref, o_ref, buf_ref, xs_scr,
                *, tm, shift, base):
    t = pl.program_id(1)
    p0 = base - 2 * shift

    @pl.when(t == 0)
    def _():
        xs_scr[p0:base, :] = prev_ref[0]

    x = h_ref[...]
    xn = (x * lax.rsqrt(jnp.mean(x * x, axis=-1, keepdims=True) + EPS) * g_ref[...]).astype(BF16)
    acc = jnp.zeros((tm, D_MODEL), F32)
    for j in range(D_FF // FFN_ACT_CHUNK):
        halves = []
        for c0 in (j * FFN_ACT_CHUNK, D_FF + j * FFN_ACT_CHUNK):
            sl = slice(c0, c0 + FFN_ACT_CHUNK)
            xs_scr[base:base + tm, sl] = _dot(xn, wup_ref[:, sl])
            y = (cw_ref[0:1, sl] * xs_scr[p0:p0 + tm, sl]
                 + cw_ref[1:2, sl] * xs_scr[p0 + shift:p0 + shift + tm, sl]
                 + cw_ref[2:3, sl] * xs_scr[base:base + tm, sl])
            halves.append(y + cb_ref[:, sl])
        a, u = halves
        act = (a * _sigmoid(a) * u).astype(BF16)
        acc = acc + _dot(act, wdn_ref[j * FFN_ACT_CHUNK:(j + 1) * FFN_ACT_CHUNK, :])
    o_ref[...] = x + acc
    last = xs_scr[base + tm - 2 * shift: base + tm, :]
    buf_ref[0] = last
    xs_scr[p0:base, :] = last


def conv_ffn(h, g, wup, cw, cb, wdn, prev, n_seq, shift, tm):
    n_rows, d = h.shape
    t_rows = n_rows // n_seq
    tm = min(tm, t_rows)
    nt = t_rows // tm
    base = -(-2 * shift // 8) * 8
    kern = functools.partial(_ffn_kernel, tm=tm, shift=shift, base=base)
    const = lambda b, i: (0, 0)
    return pl.pallas_call(
        kern, grid=(n_seq, nt),
        in_specs=[pl.BlockSpec((tm, d), lambda b, i: (b * nt + i, 0)),
                  pl.BlockSpec((1, d), const),
                  pl.BlockSpec((d, 2 * D_FF), const, pipeline_mode=pl.Buffered(1)),
                  pl.BlockSpec((FFN_CONV, 2 * D_FF), const),
                  pl.BlockSpec((1, 2 * D_FF), const),
                  pl.BlockSpec((D_FF, d), const, pipeline_mode=pl.Buffered(1)),
                  pl.BlockSpec((1, 2 * shift, 2 * D_FF), lambda b, i: (b, 0, 0))],
        out_specs=[pl.BlockSpec((tm, d), lambda b, i: (b * nt + i, 0)),
                   pl.BlockSpec((1, 2 * shift, 2 * D_FF), lambda b, i: (b, 0, 0))],
        out_shape=[jax.ShapeDtypeStruct((n_rows, d), F32),
                   jax.ShapeDtypeStruct((n_seq, 2 * shift, 2 * D_FF), F32)],
        scratch_shapes=[pltpu.VMEM((base + tm, 2 * D_FF), F32)],
        compiler_params=_cparams(("arbitrary", "arbitrary"), 56), name="conv_ffn",
    )(h, g.reshape(1, d), wup, cw, cb.reshape(1, 2 * D_FF), wdn, prev)


GDN_STACK = GDN_HEADS * GDN_CHUNK
GDN_C3 = 3 * GDN_W
GDN_PREV0 = 8 - (GDN_CONV - 1)


def _gdn_masks():
    r = np.arange(GDN_STACK)[:, None]
    c = np.arange(GDN_STACK)[None, :]
    same = lambda n: (r // n) == (c // n)
    m = [same(2) & (c < r)]
    for n in (4, 8, 16, 32, 64):
        m.append(same(n) & ~same(n // 2) & (c < r))
    m.append(same(GDN_CHUNK) & (c <= r))
    m.append(same(GDN_CHUNK) & (c < r))
    m.append(r == c)
    return np.stack(m).astype(np.float32)


def _unit_lower_inverse(a, m_ref):
    t = m_ref[8] - a * m_ref[0]
    for lvl in range(1, 6):
        off = a * m_ref[lvl]
        t = t - _dot3(t, _dot3(off, t))
    return t


def _gdn_kernel_v1(qkv_ref, z_ref, ba_ref, prev_ref, s0_ref, cw_ref, par_ref, nw_ref, m_ref,
                   o_ref, buf_ref, s_ref, xs_scr, s_scr, *, tt):
    t = pl.program_id(1)
    nt = pl.num_programs(1)
    row0 = 8

    @pl.when(t == 0)
    def _():
        xs_scr[GDN_PREV0:row0, :] = prev_ref[0]
        s_scr[...] = s0_ref[0]

    xs_scr[row0:row0 + tt, :] = qkv_ref[...]
    causal = m_ref[6]
    neg_ea = -jnp.exp(par_ref[0:1, :])
    dtb = par_ref[1:2, :]
    rv = min(GDN_CHUNK, tt)
    pad = GDN_CHUNK - rv

    def padrows(a):
        if pad == 0:
            return a
        return jnp.concatenate([a, jnp.zeros((pad, a.shape[1]), a.dtype)], axis=0)

    for n in range(-(-tt // GDN_CHUNK)):
        r0 = n * GDN_CHUNK
        y = cw_ref[0:1, :] * xs_scr[GDN_PREV0 + r0:GDN_PREV0 + r0 + rv, :]
        for i in range(1, GDN_CONV):
            y = y + cw_ref[i:i + 1, :] * xs_scr[GDN_PREV0 + i + r0:GDN_PREV0 + i + r0 + rv, :]
        y = y * _sigmoid(y)
        qs, ks, vs, betas, gs = [], [], [], [], []
        for h in range(GDN_HEADS):
            q = y[:, h * GDN_HD:(h + 1) * GDN_HD]
            k = y[:, GDN_W + h * GDN_HD:GDN_W + (h + 1) * GDN_HD]
            v = y[:, 2 * GDN_W + h * GDN_HD:2 * GDN_W + (h + 1) * GDN_HD]
            q = q * lax.rsqrt(jnp.sum(q * q, axis=-1, keepdims=True) + EPS) * (GDN_HD ** -0.5)
            k = k * lax.rsqrt(jnp.sum(k * k, axis=-1, keepdims=True) + EPS)
            bcol = ba_ref[r0:r0 + rv, h:h + 1]
            acol = ba_ref[r0:r0 + rv, GDN_HEADS + h:GDN_HEADS + h + 1] + dtb[:, h:h + 1]
            softplus = jnp.maximum(acol, 0.0) + jnp.log1p(jnp.exp(-jnp.abs(acol)))
            qs.append(padrows(q)); ks.append(padrows(k)); vs.append(padrows(v))
            betas.append(padrows(_sigmoid(bcol)))
            gs.append(padrows(neg_ea[:, h:h + 1] * softplus))
        qm = jnp.concatenate(qs, axis=0)
        km = jnp.concatenate(ks, axis=0)
        vm = jnp.concatenate(vs, axis=0)
        beta = jnp.concatenate(betas, axis=0)
        g = jnp.concatenate(gs, axis=0)
        gc = jnp.dot(causal, jnp.broadcast_to(g, (GDN_STACK, LANE)), preferred_element_type=F32,
                     precision=lax.Precision.HIGHEST)[:, 0:1]
        gc_row = jnp.sum(m_ref[8] * gc, axis=0, keepdims=True)
        diff = gc - gc_row
        decay = jnp.exp(jnp.where(causal > 0, diff, 0.0)) * causal
        kb = km * beta
        kmb = km.astype(BF16)
        a_mat = _dot_t(kb.astype(BF16), kmb) * decay * m_ref[7]
        qk = _dot_t(qm.astype(BF16), kmb) * decay
        tinv = _unit_lower_inverse(a_mat, m_ref)
        egc = jnp.exp(gc)
        sol = _dot3(tinv, jnp.concatenate([vm * beta, kb * egc], axis=1))
        u_all, w_all = sol[:, :GDN_HD], sol[:, GDN_HD:]
        qd = qm * egc
        v_news, o_inter, kds, gls = [], [], [], []
        for h in range(GDN_HEADS):
            rs = slice(h * GDN_CHUNK, (h + 1) * GDN_CHUNK)
            sb = s_scr[h].astype(BF16)
            v_new = u_all[rs] - _dot(w_all[rs].astype(BF16), sb)
            o_inter.append(_dot(qd[rs].astype(BF16), sb))
            gl = gc[h * GDN_CHUNK + GDN_CHUNK - 1:(h + 1) * GDN_CHUNK, :]
            kds.append(km[rs] * jnp.exp(gl - gc[rs]))
            v_news.append(v_new)
            gls.append(gl)
        v_stack = jnp.concatenate(v_news, axis=0).astype(BF16)
        o_intra = _dot(qk.astype(BF16), v_stack)
        for h in range(GDN_HEADS):
            rs = slice(h * GDN_CHUNK, (h + 1) * GDN_CHUNK)
            s_scr[h] = s_scr[h] * jnp.exp(gls[h]) + _dot(kds[h].T.astype(BF16), v_stack[rs])
            o = (o_inter[h] + o_intra[rs])[:rv]
            on = o * lax.rsqrt(jnp.mean(o * o, axis=-1, keepdims=True) + EPS) * nw_ref[...]
            zz = z_ref[r0:r0 + rv, h * GDN_HD:(h + 1) * GDN_HD]
            o_ref[r0:r0 + rv, h * GDN_HD:(h + 1) * GDN_HD] = on * (zz * _sigmoid(zz))

    last = xs_scr[row0 + tt - (GDN_CONV - 1):row0 + tt, :]
    buf_ref[0] = last
    xs_scr[GDN_PREV0:row0, :] = last

    @pl.when(t == nt - 1)
    def _():
        s_ref[0] = s_scr[...]


def _gdn_kernel(qkv_ref, z_ref, ba_ref, prev_ref, s0_ref, cw_ref, par_ref, nw_ref, m_ref,
                o_ref, buf_ref, s_ref, xs_scr, s_scr, *, tt):
    t = pl.program_id(1)
    nt = pl.num_programs(1)
    row0 = 8

    @pl.when(t == 0)
    def _():
        xs_scr[GDN_PREV0:row0, :] = prev_ref[0]
        s_scr[...] = s0_ref[0]

    xs_scr[row0:row0 + tt, :] = qkv_ref[...]
    causal, strict, eye = m_ref[6], m_ref[7], m_ref[8]
    neg_ea = -jnp.exp(par_ref[0:1, :])
    dtb = par_ref[1:2, :]
    rv = min(GDN_CHUNK, tt)
    pad = GDN_CHUNK - rv
    n_ch = -(-tt // GDN_CHUNK)

    def padrows(a):
        if pad == 0:
            return a
        return jnp.concatenate([a, jnp.zeros((pad, a.shape[1]), a.dtype)], axis=0)

    qms, kms, vms, betas, gcols = [], [], [], [], []
    for n in range(n_ch):
        r0 = n * GDN_CHUNK
        y = cw_ref[0:1, :] * xs_scr[GDN_PREV0 + r0:GDN_PREV0 + r0 + rv, :]
        for i in range(1, GDN_CONV):
            y = y + cw_ref[i:i + 1, :] * xs_scr[GDN_PREV0 + i + r0:GDN_PREV0 + i + r0 + rv, :]
        y = y * _sigmoid(y)
        qs, ks, vs, bs, gs = [], [], [], [], []
        for h in range(GDN_HEADS):
            q = y[:, h * GDN_HD:(h + 1) * GDN_HD]
            k = y[:, GDN_W + h * GDN_HD:GDN_W + (h + 1) * GDN_HD]
            v = y[:, 2 * GDN_W + h * GDN_HD:2 * GDN_W + (h + 1) * GDN_HD]
            q = q * lax.rsqrt(jnp.sum(q * q, axis=-1, keepdims=True) + EPS) * (GDN_HD ** -0.5)
            k = k * lax.rsqrt(jnp.sum(k * k, axis=-1, keepdims=True) + EPS)
            bcol = ba_ref[r0:r0 + rv, h:h + 1]
            acol = ba_ref[r0:r0 + rv, GDN_HEADS + h:GDN_HEADS + h + 1] + dtb[:, h:h + 1]
            softplus = jnp.maximum(acol, 0.0) + jnp.log1p(jnp.exp(-jnp.abs(acol)))
            qs.append(padrows(q)); ks.append(padrows(k)); vs.append(padrows(v))
            bs.append(padrows(_sigmoid(bcol)))
            gs.append(padrows(neg_ea[:, h:h + 1] * softplus))
        qms.append(jnp.concatenate(qs, axis=0))
        kms.append(jnp.concatenate(ks, axis=0))
        vms.append(jnp.concatenate(vs, axis=0))
        betas.append(jnp.concatenate(bs, axis=0))
        gcols.append(jnp.concatenate(gs, axis=0))
    gmat = jnp.concatenate(gcols + [jnp.zeros((GDN_STACK, LANE - n_ch), F32)], axis=1)
    gc_all = jnp.dot(causal, gmat, preferred_element_type=F32, precision=lax.Precision.HIGHEST)

    a_mats, qks, rhss, gcs, tinv = [], [], [], [], []
    for n in range(n_ch):
        gc = gc_all[:, n:n + 1]
        gc_row = jnp.sum(eye * gc, axis=0, keepdims=True)
        decay = jnp.exp(jnp.where(causal > 0, gc - gc_row, 0.0)) * causal
        kb = kms[n] * betas[n]
        kmb = kms[n].astype(BF16)
        a_mat = _dot_t(kb.astype(BF16), kmb) * decay * strict
        qks.append((_dot_t(qms[n].astype(BF16), kmb) * decay).astype(BF16))
        rhss.append(jnp.concatenate([vms[n] * betas[n], kb * jnp.exp(gc)], axis=1).astype(BF16))
        a_mats.append(a_mat)
        gcs.append(gc)
        tinv.append(eye - a_mat * m_ref[0])
    for lvl in range(1, 6):
        tb = [x.astype(BF16) for x in tinv]
        xs = [_dot((a_mats[n] * m_ref[lvl]).astype(BF16), tb[n]).astype(BF16) for n in range(n_ch)]
        tinv = [tinv[n] - _dot(tb[n], xs[n]) for n in range(n_ch)]
    sols = [_dot(tinv[n].astype(BF16), rhss[n]) for n in range(n_ch)]

    for n in range(n_ch):
        r0 = n * GDN_CHUNK
        gc = gcs[n]
        egc = jnp.exp(gc)
        u_all, w_all = sols[n][:, :GDN_HD], sols[n][:, GDN_HD:]
        qd = qms[n] * egc
        v_news, o_inter, gls = [], [], []
        for h in range(GDN_HEADS):
            rs = slice(h * GDN_CHUNK, (h + 1) * GDN_CHUNK)
            sb = s_scr[h].astype(BF16)
            both = _dot(jnp.concatenate([w_all[rs], qd[rs]], axis=0).astype(BF16), sb)
            v_news.append(u_all[rs] - both[:GDN_CHUNK])
            o_inter.append(both[GDN_CHUNK:])
            gls.append(gc[h * GDN_CHUNK + GDN_CHUNK - 1:(h + 1) * GDN_CHUNK, :])
        v_stack = jnp.concatenate(v_news, axis=0).astype(BF16)
        o_intra = _dot(qks[n], v_stack)
        for h in range(GDN_HEADS):
            rs = slice(h * GDN_CHUNK, (h + 1) * GDN_CHUNK)
            kd = kms[n][rs] * jnp.exp(gls[h] - gc[rs])
            s_scr[h] = s_scr[h] * jnp.exp(gls[h]) + _dot(kd.T.astype(BF16), v_stack[rs])
            o = (o_inter[h] + o_intra[rs])[:rv]
            on = o * lax.rsqrt(jnp.mean(o * o, axis=-1, keepdims=True) + EPS) * nw_ref[...]
            zz = z_ref[r0:r0 + rv, h * GDN_HD:(h + 1) * GDN_HD]
            o_ref[r0:r0 + rv, h * GDN_HD:(h + 1) * GDN_HD] = on * (zz * _sigmoid(zz))

    last = xs_scr[row0 + tt - (GDN_CONV - 1):row0 + tt, :]
    buf_ref[0] = last
    xs_scr[GDN_PREV0:row0, :] = last

    @pl.when(t == nt - 1)
    def _():
        s_ref[0] = s_scr[...]


def gdn(qkv_pre, z_pre, ba_pre, prev, s0, conv_w, a_log, dt_bias, norm_w, n_seq, tt):
    n_rows = qkv_pre.shape[0]
    t_rows = n_rows // n_seq
    tt = min(tt, t_rows)
    nt = t_rows // tt
    par = jnp.zeros((8, LANE), F32).at[0, :GDN_HEADS].set(a_log).at[1, :GDN_HEADS].set(dt_bias)
    masks = jnp.asarray(_gdn_masks())
    const2 = lambda b, i: (0, 0)
    rows = lambda b, i: (b * nt + i, 0)
    return pl.pallas_call(
        functools.partial(_gdn_kernel, tt=tt), grid=(n_seq, nt),
        in_specs=[pl.BlockSpec((tt, GDN_C3), rows), pl.BlockSpec((tt, GDN_W), rows),
                  pl.BlockSpec((tt, LANE), rows),
                  pl.BlockSpec((1, GDN_CONV - 1, GDN_C3), lambda b, i: (b, 0, 0)),
                  pl.BlockSpec((1, GDN_HEADS, GDN_HD, GDN_HD), lambda b, i: (b, 0, 0, 0)),
                  pl.BlockSpec((GDN_CONV, GDN_C3), const2), pl.BlockSpec((8, LANE), const2),
                  pl.BlockSpec((1, GDN_HD), const2),
                  pl.BlockSpec(masks.shape, lambda b, i: (0, 0, 0))],
        out_specs=[pl.BlockSpec((tt, GDN_W), rows),
                   pl.BlockSpec((1, GDN_CONV - 1, GDN_C3), lambda b, i: (b, 0, 0)),
                   pl.BlockSpec((1, GDN_HEADS, GDN_HD, GDN_HD), lambda b, i: (b, 0, 0, 0))],
        out_shape=[jax.ShapeDtypeStruct((n_rows, GDN_W), F32),
                   jax.ShapeDtypeStruct((n_seq, GDN_CONV - 1, GDN_C3), F32),
                   jax.ShapeDtypeStruct((n_seq, GDN_HEADS, GDN_HD, GDN_HD), F32)],
        scratch_shapes=[pltpu.VMEM((8 + tt, GDN_C3), F32), pltpu.VMEM((GDN_HEADS, GDN_HD, GDN_HD), F32)],
        compiler_params=_cparams(("arbitrary", "arbitrary"), 48), name="gdn",
    )(qkv_pre, z_pre, ba_pre, prev, s0, conv_w, par, norm_w.reshape(1, GDN_HD), masks)


CMP_W = 2 * NSA_KV_HEADS * NSA_HD


def _cmp_weights(cmp_pe, cmp_w):
    def bd(w):
        z = jnp.zeros_like(w)
        return jnp.concatenate([jnp.concatenate([w, z], 2), jnp.concatenate([z, w], 2)], 1)
    w = jnp.stack([jnp.stack([bd(cmp_w[s, :CMP_STRIDE]), bd(cmp_w[s, CMP_STRIDE:])]) for s in range(2)]).astype(BF16)
    pe2 = jnp.concatenate([cmp_pe, cmp_pe], axis=-1)
    pe = jnp.stack([pe2[:, :CMP_STRIDE], pe2[:, CMP_STRIDE:]], axis=1)
    return w, pe


def _compress(read_k, read_v, w_ref, pe_ref, n_chunk):
    outs = []
    for s, read in enumerate((read_k, read_v)):
        lo = jnp.zeros((n_chunk, LANE), F32)
        hi = jnp.zeros((n_chunk, LANE), F32)
        for j in range(CMP_STRIDE):
            x = read(j)
            lo = lo + _dot((x + pe_ref[s, 0, j:j + 1, :]).astype(BF16), w_ref[s, 0, j])
            hi = hi + _dot((x + pe_ref[s, 1, j:j + 1, :]).astype(BF16), w_ref[s, 1, j])
        outs.append(lo + pltpu.roll(hi, n_chunk - 1, axis=0))
    return jnp.concatenate(outs, axis=1)


def _cmp_kernel(k_ref, v_ref, w_ref, pe_ref, o_ref, *, n_chunk):
    o_ref[0] = _compress(lambda j: k_ref[0, pl.ds(j, n_chunk, stride=CMP_STRIDE), :],
                         lambda j: v_ref[0, pl.ds(j, n_chunk, stride=CMP_STRIDE), :], w_ref, pe_ref, n_chunk)


def compress_prompt(rows, w, pe):
    b, t, _ = rows.shape
    n_chunk = t // CMP_STRIDE
    return pl.pallas_call(
        functools.partial(_cmp_kernel, n_chunk=n_chunk), grid=(b,),
        in_specs=[pl.BlockSpec((1, t, LANE), lambda i: (i, 0, 0)), pl.BlockSpec((1, t, LANE), lambda i: (i, 0, 1)),
                  pl.BlockSpec(w.shape, lambda i: (0, 0, 0, 0, 0)),
                  pl.BlockSpec(pe.shape, lambda i: (0, 0, 0, 0))],
        out_specs=pl.BlockSpec((1, n_chunk, CMP_W), lambda i: (i, 0, 0)),
        out_shape=jax.ShapeDtypeStruct((b, n_chunk, CMP_W), F32),
        compiler_params=_cparams(("parallel",), 32), name="nsa_compress",
    )(rows, rows, w, pe)


def _c2s(n_cmp_pad, n_sel_pad):
    cs = np.arange(n_cmp_pad)[:, None] * CMP_STRIDE
    ss = np.arange(n_sel_pad)[None, :] * SEL_BLOCK
    return ((cs < ss + SEL_BLOCK) & (cs + CMP_BLOCK > ss)).astype(np.float32)


def _expand(n_sel_pad, n_keys):
    s = np.arange(n_sel_pad)[:, None]
    k = np.arange(n_keys)[None, :]
    return (k // SEL_BLOCK == s).astype(np.float32)


def _cmp_probs(s, valid):
    s = jnp.where(valid, s, NEG)
    p = jnp.where(valid, jnp.exp(s - jnp.max(s, axis=-1, keepdims=True)), 0.0)
    return p / jnp.maximum(jnp.sum(p, axis=-1, keepdims=True), 1e-30)


def _select_blocks(imp, q_blk, n_sel):
    lane = lax.broadcasted_iota(jnp.int32, imp.shape, 1)
    visible = lane <= q_blk
    forced = (lane == 0) | (lane == q_blk) | (lane == q_blk - 1)
    score = jnp.where(visible, imp + jnp.where(forced, FORCED_BONUS, 0.0), NEG)
    rank = jnp.zeros(imp.shape, F32)
    for sp in range(n_sel):
        col = score[:, sp:sp + 1]
        beats = (col > score) | ((col == score) & (lane > sp))
        rank = rank + jnp.where(beats, 1.0, 0.0)
    return jnp.where(visible & (rank < SEL_TOPN), 1.0, 0.0)


def _flash(q, k_ref, v_ref, j_lo, j_hi, mask_fn, tq, tk):
    def body(j, carry):
        m, l, acc = carry
        off = pl.multiple_of(j * tk, tk)
        s = _dot_t(q, k_ref[pl.ds(off, tk), :])
        s = jnp.where(mask_fn(j), s, NEG)
        m_new = jnp.maximum(m, jnp.max(s, axis=-1, keepdims=True))
        alpha = jnp.exp(m - m_new)
        p = jnp.exp(s - m_new)
        l = alpha * l + jnp.sum(p, axis=-1, keepdims=True)
        acc = alpha * acc + _dot(p.astype(BF16), v_ref[pl.ds(off, tk), :])
        return m_new, l, acc

    init = (jnp.full((tq, 1), NEG, F32), jnp.zeros((tq, 1), F32), jnp.zeros((tq, NSA_HD), F32))
    _, l, acc = lax.fori_loop(j_lo, j_hi, body, init)
    return acc / l


def _nsa_prompt_kernel_v1(qc_ref, qr_ref, kc_ref, vc_ref, ks_ref, vs_ref, kw_ref, vw_ref, g_ref, c2s_ref, e_ref,
                       o_ref, mask_scr, *, tq, n_sel, n_cmp_pad):
    i = pl.program_id(2)
    t0 = i * tq
    qpos = t0 + lax.broadcasted_iota(jnp.int32, (tq, 1), 0)
    cend = lax.broadcasted_iota(jnp.int32, (1, n_cmp_pad), 1) * CMP_STRIDE + (CMP_BLOCK - 1)
    valid_c = cend <= qpos
    psum = jnp.zeros((tq, n_cmp_pad), F32)
    o_cmp = []
    for g in range(NSA_GROUP):
        p = _cmp_probs(_dot_t(qc_ref[0, g], kc_ref[0, 0]), valid_c)
        psum = psum + p
        o_cmp.append(_dot(p.astype(BF16), vc_ref[0, 0]))
    imp = jnp.dot(psum, c2s_ref[...], preferred_element_type=F32, precision=lax.Precision.HIGHEST)
    sel = _select_blocks(imp, lax.shift_right_logical(qpos, 6), n_sel).astype(BF16)
    n_kt = mask_scr.shape[0]
    for jj in range(n_kt):
        mask_scr[jj] = _dot(sel, e_ref[:, jj * tq:(jj + 1) * tq])
    kiota = lax.broadcasted_iota(jnp.int32, (1, tq), 1)

    def sel_mask(j):
        return (mask_scr[j] > 0.5) & (j * tq + kiota <= qpos)

    def win_mask(j):
        rel = qpos - (j * tq + kiota)
        return (rel >= 0) & (rel < WINDOW)

    w_tiles = WINDOW // tq
    for g in range(NSA_GROUP):
        q = qr_ref[0, g]
        o_sel = _flash(q, ks_ref.at[0, 0], vs_ref.at[0, 0], 0, i + 1, sel_mask, tq, tq)
        o_win = _flash(q, kw_ref.at[0, 0], vw_ref.at[0, 0], jnp.maximum(i - w_tiles, 0), i + 1, win_mask, tq, tq)
        gt = g_ref[0, g]
        o = o_cmp[g] * gt[:, 0:1] + o_sel * gt[:, 1:2] + o_win * gt[:, 2:3]
        o_ref[0, :, g * NSA_HD:(g + 1) * NSA_HD] = o


def nsa_prompt_attn_v1(qc, qr, kcmp, vcmp, ksel, vsel, kwin, vwin, gates, tq):
    b, _, t, _ = qc.shape
    tq = min(tq, t)
    n_cmp_pad = kcmp.shape[2]
    n_sel = t // SEL_BLOCK
    n_sel_pad = -(-n_sel // LANE) * LANE
    c2s = jnp.asarray(_c2s(n_cmp_pad, n_sel_pad))
    e = jnp.asarray(_expand(n_sel_pad, t), dtype=BF16)
    kern = functools.partial(_nsa_prompt_kernel_v1, tq=tq, n_sel=n_sel, n_cmp_pad=n_cmp_pad)
    qspec = pl.BlockSpec((1, NSA_GROUP, tq, NSA_HD), lambda bb, k, i: (bb, k, i, 0))
    kvspec = lambda n: pl.BlockSpec((1, 1, n, NSA_HD), lambda bb, k, i: (bb, k, 0, 0))
    return pl.pallas_call(
        kern, grid=(b, NSA_KV_HEADS, t // tq),
        in_specs=[qspec, qspec, kvspec(n_cmp_pad), kvspec(n_cmp_pad), kvspec(t), kvspec(t), kvspec(t), kvspec(t),
                  pl.BlockSpec((1, NSA_GROUP, tq, 3), lambda bb, k, i: (bb, k, i, 0)),
                  pl.BlockSpec(c2s.shape, lambda bb, k, i: (0, 0)), pl.BlockSpec(e.shape, lambda bb, k, i: (0, 0))],
        out_specs=pl.BlockSpec((1, tq, NSA_GROUP * NSA_HD), lambda bb, k, i: (bb, i, k)),
        out_shape=jax.ShapeDtypeStruct((b, t, NSA_Q_W), F32),
        scratch_shapes=[pltpu.VMEM((t // tq, tq, tq), F32)],
        compiler_params=_cparams(("parallel", "parallel", "parallel"), 40), name="nsa_prompt_attn",
    )(qc, qr, kcmp, vcmp, ksel, vsel, kwin, vwin, gates, c2s, e)


def _pair_rms(x, w):
    lo = lax.broadcasted_iota(jnp.int32, x.shape, 1) < NSA_HD
    x2 = x * x
    s_lo = jnp.sum(jnp.where(lo, x2, 0.0), axis=-1, keepdims=True)
    s_hi = jnp.sum(jnp.where(lo, 0.0, x2), axis=-1, keepdims=True)
    ms = jnp.where(lo, s_lo, s_hi) * (1.0 / NSA_HD)
    return x * lax.rsqrt(ms + EPS) * w


def _pair_rope(x, cos_t, sin_t):
    lane = lax.broadcasted_iota(jnp.int32, x.shape, 1)
    first = (lane & (NSA_HD - 1)) < NSA_HD // 2
    partner = jnp.where(first, pltpu.roll(x, LANE - NSA_HD // 2, axis=1), pltpu.roll(x, NSA_HD // 2, axis=1))
    return x * cos_t + partner * sin_t


def _rope_tables(pos):
    half = NSA_HD // 2
    inv = jnp.power(ROPE_THETA, -jnp.arange(half, dtype=F32) / half)
    ang = pos.astype(F32)[:, None] * inv[None, :]
    c, s = jnp.cos(ang), jnp.sin(ang)
    return jnp.concatenate([c, c, c, c], axis=-1), jnp.concatenate([-s, s, -s, s], axis=-1)


def _nsa_prep_kernel_v2(q_ref, kv_ref, g_ref, cos_ref, sin_ref, qn_ref, kn_ref,
                        rows_ref, wrows_ref, kvb_ref, qc_ref, qr_ref, gate_ref, cmp_ref=None):
    cos_t, sin_t = cos_ref[...], sin_ref[...]
    lo = lax.broadcasted_iota(jnp.int32, cos_t.shape, 1) < NSA_HD
    for c in range(NSA_HEADS // 2):
        y = _pair_rms(q_ref[:, c * LANE:(c + 1) * LANE], qn_ref[...]) * (NSA_HD ** -0.5)
        in_low_lanes = (2 * c) // NSA_GROUP == 0
        for src, dst in ((y, qc_ref), (_pair_rope(y, cos_t, sin_t) * LOG2E, qr_ref)):
            swapped = pltpu.roll(src, NSA_HD, axis=1)
            if in_low_lanes:
                even, odd = jnp.where(lo, src, 0.0), jnp.where(lo, swapped, 0.0)
            else:
                even, odd = jnp.where(lo, 0.0, swapped), jnp.where(lo, 0.0, src)
            dst[:, (2 * c) * LANE:(2 * c + 1) * LANE] = even.astype(BF16)
            dst[:, (2 * c + 1) * LANE:(2 * c + 2) * LANE] = odd.astype(BF16)
    for br in range(3):
        k = _pair_rms(kv_ref[:, br * 2 * LANE:br * 2 * LANE + LANE], kn_ref[br:br + 1, :])
        if br > 0:
            k = _pair_rope(k, cos_t, sin_t)
        v = kv_ref[:, br * 2 * LANE + LANE:(br + 1) * 2 * LANE]
        if cmp_ref is None:
            dst, off = (rows_ref, br * 2 * LANE) if br < 2 else (wrows_ref, 0)
            dst[:, off:off + LANE] = k
            dst[:, off + LANE:off + 2 * LANE] = v
        else:
            dst, w0 = (rows_ref, 2 * br) if br < 2 else (wrows_ref, 0)
            for w, x in ((w0, k), (w0 + 1, v)):
                xt = x.T
                for h in range(NSA_KV_HEADS):
                    dst[0, w, h] = xt[h * NSA_HD:(h + 1) * NSA_HD]
            if br == 0:
                cmp_ref[:, :LANE] = k
                cmp_ref[:, LANE:] = v
        kvb_ref[:, br * 2 * LANE:br * 2 * LANE + LANE] = k.astype(BF16)
        kvb_ref[:, br * 2 * LANE + LANE:(br + 1) * 2 * LANE] = v.astype(BF16)
    gate_ref[...] = _sigmoid(g_ref[...])


def nsa_prep_v2(q_pre, kv_pre, gate_pre, cos_t, sin_t, qn, kn, tm, dims_major):
    n_rows = q_pre.shape[0]
    tm = min(tm, n_rows)
    n_tab = cos_t.shape[0] // tm
    rows = lambda w: pl.BlockSpec((tm, w), lambda i: (i, 0))
    tab = pl.BlockSpec((tm, LANE), lambda i: (i % n_tab, 0))
    widths = (NSA_KV_W, NSA_HEADS * LANE, NSA_HEADS * LANE, LANE)
    dtypes = (BF16, BF16, BF16, F32)
    out_specs = [rows(w) for w in widths]
    out_shape = [jax.ShapeDtypeStruct((n_rows, w), dt) for w, dt in zip(widths, dtypes)]
    if dims_major:
        n_seq, t = n_rows // cos_t.shape[0], cos_t.shape[0]
        kv_spec = lambda n: pl.BlockSpec((1, n, NSA_KV_HEADS, NSA_HD, tm), lambda i: (i // n_tab, 0, 0, 0, i % n_tab))
        kv_shape = lambda n: jax.ShapeDtypeStruct((n_seq, n, NSA_KV_HEADS, NSA_HD, t), F32)
        out_specs = [kv_spec(4), kv_spec(2)] + out_specs + [rows(2 * LANE)]
        out_shape = [kv_shape(4), kv_shape(2)] + out_shape + [jax.ShapeDtypeStruct((n_rows, 2 * LANE), F32)]
    else:
        out_specs = [rows(NSA_Q_W), rows(2 * LANE)] + out_specs
        out_shape = [jax.ShapeDtypeStruct((n_rows, NSA_Q_W), F32),
                     jax.ShapeDtypeStruct((n_rows, 2 * LANE), F32)] + out_shape
    return pl.pallas_call(
        _nsa_prep_kernel_v2, grid=(n_rows // tm,),
        in_specs=[rows(NSA_Q_W), rows(NSA_KV_W), rows(LANE), tab, tab,
                  pl.BlockSpec((1, LANE), lambda i: (0, 0)), pl.BlockSpec((3, LANE), lambda i: (0, 0))],
        out_specs=out_specs, out_shape=out_shape,
        compiler_params=_cparams(("parallel",), 40), name="nsa_prep",
    )(q_pre, kv_pre, gate_pre, cos_t, sin_t, jnp.tile(qn.reshape(1, NSA_HD), (1, 2)), jnp.tile(kn, (1, 2)))


def _select_blocks_t(imp_t, q_blk, n_sel):
    blk = lax.broadcasted_iota(jnp.int32, imp_t.shape, 0)
    visible = blk <= q_blk
    forced = jnp.where(blk == 0, 1.0, 0.0) + jnp.where(blk == q_blk, 1.0, 0.0) + jnp.where(blk == q_blk - 1, 1.0, 0.0)
    score = jnp.where(visible, imp_t + jnp.where(forced > 0.0, FORCED_BONUS, 0.0), NEG)
    rank = jnp.zeros(imp_t.shape, F32)
    for sp in range(n_sel):
        row = score[sp:sp + 1, :]
        tie = jnp.where(blk > sp, jnp.where(row == score, 1.0, 0.0), 0.0)
        rank = rank + jnp.where(row > score, 1.0, 0.0) + tie
    return jnp.where(visible, jnp.where(rank < SEL_TOPN, 1.0, 0.0), 0.0)


def _flash8_v2(qs, k_ref, v_ref, j_lo, j_hi, valid_fn, tq, tk):
    def body(j, carry):
        off = pl.multiple_of(j * tk, tk)
        k = k_ref[pl.ds(off, tk), :]
        v = v_ref[pl.ds(off, tk), :]
        valid = valid_fn(j)
        out = []
        for h, (m, l, acc) in enumerate(carry):
            s = jnp.where(valid[h // NSA_GROUP], _dot_t(qs[h], k), NEG)
            m_new = jnp.maximum(m, jnp.max(s, axis=-1, keepdims=True))
            alpha = jnp.exp2(m - m_new)
            p = jnp.exp2(s - m_new)
            l = alpha * l + jnp.sum(p, axis=-1, keepdims=True)
            acc = alpha * acc + _dot(p.astype(BF16), v)
            out.append((m_new, l, acc))
        return tuple(out)

    init = tuple((jnp.full((tq, 1), NEG, F32), jnp.zeros((tq, 1), F32), jnp.zeros((tq, LANE), F32))
                 for _ in range(NSA_HEADS))
    return [acc / l for _, l, acc in lax.fori_loop(j_lo, j_hi, body, init)]


def _nsa_prompt_kernel_v2(qc_ref, qr_ref, kvc_ref, ks_ref, vs_ref, kw_ref, vw_ref, g_ref, c2s_ref, e_ref, o_ref,
                       *, tq, n_sel):
    i = pl.program_id(1)
    t0 = i * tq
    n_rows = NSA_HEADS * tq
    stack = lambda ref: jnp.concatenate([ref[:, h * LANE:(h + 1) * LANE] for h in range(NSA_HEADS)], axis=0)
    qpos = t0 + lax.broadcasted_iota(jnp.int32, (tq, 1), 0)
    n_cmp = kvc_ref.shape[1]
    kc = kvc_ref[0, :, :LANE].astype(BF16)
    vc = kvc_ref[0, :, LANE:].astype(BF16)
    cend = lax.broadcasted_iota(jnp.int32, (1, n_cmp), 1) * CMP_STRIDE + (CMP_BLOCK - 1)
    s = _dot_t(stack(qc_ref), kc).reshape(NSA_KV_HEADS, NSA_GROUP, tq, n_cmp)
    p = _cmp_probs(s, (cend <= qpos)[None, None])
    o_cmp = _dot(p.reshape(n_rows, n_cmp).astype(BF16), vc)
    psum = p[:, 0]
    for g in range(1, NSA_GROUP):
        psum = psum + p[:, g]
    imp = jnp.dot(psum.reshape(NSA_KV_HEADS * tq, n_cmp), c2s_ref[...], preferred_element_type=F32,
                  precision=lax.Precision.HIGHEST)
    n_blk_rows = -(-n_sel // 8) * 8
    col = lax.broadcasted_iota(jnp.int32, (1, NSA_KV_HEADS * tq), 1)
    q_blk = lax.shift_right_logical(t0 + (col & (tq - 1)), 6)
    sel_t = _select_blocks_t(imp.T[:n_blk_rows], q_blk, n_sel)
    sel_t = jnp.concatenate([sel_t, jnp.zeros((LANE - n_blk_rows, NSA_KV_HEADS * tq), F32)], axis=0)
    sel = sel_t.T.astype(BF16)
    kiota = lax.broadcasted_iota(jnp.int32, (1, tq), 1)

    def sel_valid(j):
        causal = (j * tq + kiota) <= qpos
        return [jnp.where(causal, _dot(sel[k * tq:(k + 1) * tq], e_ref[j]), 0.0) > 0.5 for k in range(NSA_KV_HEADS)]

    def win_valid(j):
        rel = qpos - (j * tq + kiota)
        return [jnp.where(rel >= 0, rel, WINDOW) < WINDOW] * NSA_KV_HEADS

    qr = [qr_ref[:, h * LANE:(h + 1) * LANE] for h in range(NSA_HEADS)]
    o_sel = _flash8_v2(qr, ks_ref.at[0], vs_ref.at[0], 0, i + 1, sel_valid, tq, tq)
    o_win = _flash8_v2(qr, kw_ref.at[0], vw_ref.at[0], jnp.maximum(i - WINDOW // tq, 0), i + 1, win_valid, tq, tq)
    lo = lax.broadcasted_iota(jnp.int32, (tq, LANE), 1) < NSA_HD
    for c in range(NSA_HEADS // 2):
        pair = []
        for h in (2 * c, 2 * c + 1):
            rs = slice(h * tq, (h + 1) * tq)
            gt = g_ref[:, 3 * h:3 * h + 3]
            pair.append(o_cmp[rs] * gt[:, 0:1] + o_sel[h] * gt[:, 1:2] + o_win[h] * gt[:, 2:3])
        even, odd = pair
        if (2 * c) // NSA_GROUP == 0:
            blk = jnp.where(lo, even, pltpu.roll(odd, NSA_HD, axis=1))
        else:
            blk = jnp.where(lo, pltpu.roll(even, NSA_HD, axis=1), odd)
        o_ref[:, c * LANE:(c + 1) * LANE] = blk


def nsa_prompt_attn_v2(qc, qr, kvc, kvb, gates, n_batch, tq):
    n_rows = qc.shape[0]
    t = n_rows // n_batch
    tq = min(tq, t)
    nt = t // tq
    n_cmp = kvc.shape[1]
    n_sel = t // SEL_BLOCK
    assert n_cmp % LANE == 0 and n_sel <= LANE and tq & (tq - 1) == 0 and WINDOW % tq == 0
    c2s = jnp.asarray(_c2s(n_cmp, LANE))
    e = jnp.asarray(_expand(LANE, t).reshape(LANE, nt, tq).transpose(1, 0, 2), dtype=BF16)
    kvb3 = kvb.reshape(n_batch, t, NSA_KV_W)
    rows = lambda w: pl.BlockSpec((tq, w), lambda b, i: (b * nt + i, 0))
    kv = lambda c: pl.BlockSpec((1, t, LANE), lambda b, i: (b, 0, c))
    return pl.pallas_call(
        functools.partial(_nsa_prompt_kernel_v2, tq=tq, n_sel=n_sel), grid=(n_batch, nt),
        in_specs=[rows(NSA_HEADS * LANE), rows(NSA_HEADS * LANE),
                  pl.BlockSpec((1, n_cmp, CMP_W), lambda b, i: (b, 0, 0)),
                  kv(2), kv(3), kv(4), kv(5), rows(LANE),
                  pl.BlockSpec(c2s.shape, lambda b, i: (0, 0)), pl.BlockSpec(e.shape, lambda b, i: (0, 0, 0))],
        out_specs=rows(NSA_Q_W),
        out_shape=jax.ShapeDtypeStruct((n_rows, NSA_Q_W), F32),
        compiler_params=_cparams(("parallel", "parallel"), 48), name="nsa_prompt_attn",
    )(qc, qr, kvc, kvb3, kvb3, kvb3, kvb3, gates, c2s, e)


LOG2E = 1.4426950408889634
KEY_TILE = 256


def _head_rows(yt, head_in_pair, kv_head):
    own = yt[head_in_pair * NSA_HD:(head_in_pair + 1) * NSA_HD]
    zero = jnp.zeros_like(own)
    return jnp.concatenate([own, zero] if kv_head == 0 else [zero, own], axis=0)


def _nsa_prep_kernel(q_ref, kv_ref, g_ref, cos_ref, sin_ref, qn_ref, kn_ref,
                     rows_ref, wrows_ref, qct_ref, qrt_ref, gt_ref, kb_ref=None, vt_ref=None, *, tm):
    cos_t, sin_t = cos_ref[...], sin_ref[...]
    for c in range(NSA_HEADS // 2):
        y = _pair_rms(q_ref[:, c * LANE:(c + 1) * LANE], qn_ref[...]) * (NSA_HD ** -0.5)
        yr = _pair_rope(y, cos_t, sin_t) * LOG2E
        for src, dst in ((y, qct_ref), (yr, qrt_ref)):
            st = src.T
            for e in range(2):
                h = 2 * c + e
                dst[h * LANE:(h + 1) * LANE, :] = _head_rows(st, e, h // NSA_GROUP).astype(BF16)
    for br in range(3):
        k = _pair_rms(kv_ref[:, br * 2 * LANE:br * 2 * LANE + LANE], kn_ref[br:br + 1, :])
        if br > 0:
            k = _pair_rope(k, cos_t, sin_t)
        v = kv_ref[:, br * 2 * LANE + LANE:(br + 1) * 2 * LANE]
        dst, off = (rows_ref, br * 2 * LANE) if br < 2 else (wrows_ref, 0)
        dst[:, off:off + LANE] = k
        dst[:, off + LANE:off + 2 * LANE] = v
        if br > 0 and kb_ref is not None:
            kb_ref[:, (br - 1) * LANE:br * LANE] = k.astype(BF16)
            vt = v.T.astype(BF16)
            for jt in range(tm // KEY_TILE):
                vt_ref[br - 1, jt] = vt[:, jt * KEY_TILE:(jt + 1) * KEY_TILE]
    gt_ref[...] = _sigmoid(g_ref[...]).T


def nsa_prep(q_pre, kv_pre, gate_pre, cos_t, sin_t, qn, kn, tm, attn_operands):
    n_rows = q_pre.shape[0]
    tm = min(tm, n_rows)
    n_tab = cos_t.shape[0] // tm
    rows = lambda w: pl.BlockSpec((tm, w), lambda i: (i, 0))
    cols = lambda r: pl.BlockSpec((r, tm), lambda i: (0, i))
    tab = pl.BlockSpec((tm, LANE), lambda i: (i % n_tab, 0))
    out_specs = [rows(NSA_Q_W), rows(2 * LANE), cols(NSA_HEADS * LANE), cols(NSA_HEADS * LANE), cols(LANE)]
    out_shape = [jax.ShapeDtypeStruct((n_rows, NSA_Q_W), F32), jax.ShapeDtypeStruct((n_rows, 2 * LANE), F32),
                 jax.ShapeDtypeStruct((NSA_HEADS * LANE, n_rows), BF16),
                 jax.ShapeDtypeStruct((NSA_HEADS * LANE, n_rows), BF16),
                 jax.ShapeDtypeStruct((LANE, n_rows), F32)]
    if attn_operands:
        assert tm % KEY_TILE == 0
        out_specs += [rows(2 * LANE), pl.BlockSpec((2, tm // KEY_TILE, LANE, KEY_TILE), lambda i: (0, i, 0, 0))]
        out_shape += [jax.ShapeDtypeStruct((n_rows, 2 * LANE), BF16),
                      jax.ShapeDtypeStruct((2, n_rows // KEY_TILE, LANE, KEY_TILE), BF16)]
    return pl.pallas_call(
        functools.partial(_nsa_prep_kernel, tm=tm), grid=(n_rows // tm,),
        in_specs=[rows(NSA_Q_W), rows(NSA_KV_W), rows(LANE), tab, tab,
                  pl.BlockSpec((1, LANE), lambda i: (0, 0)), pl.BlockSpec((3, LANE), lambda i: (0, 0))],
        out_specs=out_specs, out_shape=out_shape,
        compiler_params=_cparams(("parallel",), 40), name="nsa_prep",
    )(q_pre, kv_pre, gate_pre, cos_t, sin_t, jnp.tile(qn.reshape(1, NSA_HD), (1, 2)), jnp.tile(kn, (1, 2)))


def _flash_t(qts, k_ref, vt_ref, j_lo, j_hi, valid_fn, tq):
    def body(j, carry):
        k = k_ref[pl.ds(pl.multiple_of(j * KEY_TILE, KEY_TILE), KEY_TILE), :]
        vt = vt_ref[j]
        valid = valid_fn(j)
        out = []
        for h, (m, l, acc) in enumerate(carry):
            s = jnp.where(valid[h // NSA_GROUP], _dot(k, qts[h]), NEG)
            m_new = jnp.maximum(m, jnp.max(s, axis=0, keepdims=True))
            alpha = jnp.exp2(m - m_new)
            p = jnp.exp2(s - m_new)
            l = alpha * l + jnp.sum(p, axis=0, keepdims=True)
            acc = alpha * acc + _dot(vt, p.astype(BF16))
            out.append((m_new, l, acc))
        return tuple(out)

    init = tuple((jnp.full((1, tq), NEG, F32), jnp.zeros((1, tq), F32), jnp.zeros((LANE, tq), F32))
                 for _ in range(NSA_HEADS))
    return [acc / l for _, l, acc in lax.fori_loop(j_lo, j_hi, body, init)]


def _nsa_prompt_kernel(qct_ref, qrt_ref, kvc_ref, ks_ref, kw_ref, vst_ref, vwt_ref, gt_ref, c2st_ref, et_ref,
                       ot_ref, *, tq, n_sel):
    i = pl.program_id(1)
    t0 = i * tq
    qpos = t0 + lax.broadcasted_iota(jnp.int32, (1, tq), 1)
    n_cmp = kvc_ref.shape[1]
    kc = kvc_ref[0, :, :LANE].astype(BF16)
    vct = kvc_ref[0, :, LANE:].T.astype(BF16)
    cend = lax.broadcasted_iota(jnp.int32, (n_cmp, 1), 0) * CMP_STRIDE + (CMP_BLOCK - 1)
    valid_c = cend <= qpos
    psum = [jnp.zeros((n_cmp, tq), F32) for _ in range(NSA_KV_HEADS)]
    o_cmp = []
    for h in range(NSA_HEADS):
        s = jnp.where(valid_c, _dot(kc, qct_ref[h * LANE:(h + 1) * LANE, :]), NEG)
        p = jnp.where(valid_c, jnp.exp(s - jnp.max(s, axis=0, keepdims=True)), 0.0)
        p = p / jnp.maximum(jnp.sum(p, axis=0, keepdims=True), 1e-30)
        psum[h // NSA_GROUP] = psum[h // NSA_GROUP] + p
        o_cmp.append(_dot(vct, p.astype(BF16)))
    n_blk_rows = -(-n_sel // 8) * 8
    imp_t = jnp.concatenate(
        [jnp.dot(c2st_ref[...], ps, preferred_element_type=F32, precision=lax.Precision.HIGHEST)[:n_blk_rows]
         for ps in psum], axis=1)
    col = lax.broadcasted_iota(jnp.int32, (1, NSA_KV_HEADS * tq), 1)
    sel_t = _select_blocks_t(imp_t, lax.shift_right_logical(t0 + (col & (tq - 1)), 6), n_sel)
    sel_t = jnp.concatenate([sel_t, jnp.zeros((LANE - n_blk_rows, NSA_KV_HEADS * tq), F32)], axis=0).astype(BF16)
    krow = lax.broadcasted_iota(jnp.int32, (KEY_TILE, 1), 0)

    def sel_valid(j):
        causal = (j * KEY_TILE + krow) <= qpos
        return [jnp.where(causal, _dot(et_ref[j], sel_t[:, k * tq:(k + 1) * tq]), 0.0) > 0.5
                for k in range(NSA_KV_HEADS)]

    def win_valid(j):
        rel = qpos - (j * KEY_TILE + krow)
        return [jnp.where(rel >= 0, rel, WINDOW) < WINDOW] * NSA_KV_HEADS

    qrt = [qrt_ref[h * LANE:(h + 1) * LANE, :] for h in range(NSA_HEADS)]
    n_kt = (t0 + tq) // KEY_TILE
    o_sel = _flash_t(qrt, ks_ref.at[0], vst_ref.at[0], 0, n_kt, sel_valid, tq)
    o_win = _flash_t(qrt, kw_ref.at[0], vwt_ref.at[0], jnp.maximum((t0 - WINDOW) // KEY_TILE, 0), n_kt,
                     win_valid, tq)
    for h in range(NSA_HEADS):
        g = gt_ref[3 * h:3 * h + 3, :]
        o = o_cmp[h] * g[0:1] + o_sel[h] * g[1:2] + o_win[h] * g[2:3]
        k = h // NSA_GROUP
        ot_ref[h * NSA_HD:(h + 1) * NSA_HD, :] = o[k * NSA_HD:(k + 1) * NSA_HD]


def nsa_prompt_attn(qct, qrt, kvc, kb, vt, gates_t, n_batch, tq):
    n_rows = qct.shape[1]
    t = n_rows // n_batch
    tq = min(tq, t)
    nt = t // tq
    n_kt = t // KEY_TILE
    n_cmp = kvc.shape[1]
    n_sel = t // SEL_BLOCK
    assert n_cmp % LANE == 0 and n_sel <= LANE and tq & (tq - 1) == 0 and tq % KEY_TILE == 0
    c2st = jnp.asarray(_c2s(n_cmp, LANE).T)
    et = jnp.asarray(_expand(LANE, t).T.reshape(n_kt, KEY_TILE, LANE), dtype=BF16)
    kb3 = kb.reshape(n_batch, t, 2 * LANE)
    vt4 = vt.reshape(2, n_batch, n_kt, LANE, KEY_TILE)
    cols = lambda r: pl.BlockSpec((r, tq), lambda b, i: (0, b * nt + i))
    kspec = lambda c: pl.BlockSpec((1, t, LANE), lambda b, i: (b, 0, c))
    vspec = lambda c: pl.BlockSpec((None, 1, n_kt, LANE, KEY_TILE), lambda b, i: (c, b, 0, 0, 0))
    return pl.pallas_call(
        functools.partial(_nsa_prompt_kernel, tq=tq, n_sel=n_sel), grid=(n_batch, nt),
        in_specs=[cols(NSA_HEADS * LANE), cols(NSA_HEADS * LANE),
                  pl.BlockSpec((1, n_cmp, CMP_W), lambda b, i: (b, 0, 0)),
                  kspec(0), kspec(1), vspec(0), vspec(1), cols(LANE),
                  pl.BlockSpec(c2st.shape, lambda b, i: (0, 0)), pl.BlockSpec(et.shape, lambda b, i: (0, 0, 0))],
        out_specs=cols(NSA_Q_W),
        out_shape=jax.ShapeDtypeStruct((NSA_Q_W, n_rows), F32),
        compiler_params=_cparams(("parallel", "parallel"), 48), name="nsa_prompt_attn",
    )(qct, qrt, kvc, kb3, kb3, vt4, vt4, gates_t, c2st, et)


SROWS = NSA_HEADS * 8
HALF_W = 2 * NSA_KV_HEADS * NSA_HD


def _page_copies(cache_ref, pt_ref, targets, sem, b, slot, n_pages, page):
    return [pltpu.make_async_copy(cache_ref.at[pt_ref[b, pg], kind],
                                  buf.at[slot, :, :, pl.ds(pg * page, page)], sem.at[slot])
            for pg in range(n_pages) for kind, buf in targets]


def _stream_pages(cache_ref, pt_ref, targets, sem, n_pages, page):
    b = pl.program_id(0)
    nb = pl.num_programs(0)
    slot = lax.rem(b, 2)

    @pl.when(b == 0)
    def _():
        for c in _page_copies(cache_ref, pt_ref, targets, sem, 0, 0, n_pages, page):
            c.start()

    @pl.when(b + 1 < nb)
    def _():
        for c in _page_copies(cache_ref, pt_ref, targets, sem, b + 1, 1 - slot, n_pages, page):
            c.start()

    for c in _page_copies(cache_ref, pt_ref, targets, sem, b, slot, n_pages, page):
        c.wait()
    return slot


def _nsa_s1_kernel(pt_ref, cache_ref, q_ref, w_ref, pe_ref, c2s_ref, ocmp_ref, sel_ref, bufk, bufv, xk, xv, sem,
                   *, n_pages, page, ts, n_sel):
    past = n_pages * page
    slot = _stream_pages(cache_ref, pt_ref, [(0, bufk), (1, bufv)], sem, n_pages, page)
    n_chunk = past // CMP_STRIDE
    tch = min(past, 8 * LANE)
    for src, dst in ((bufk, xk), (bufv, xv)):
        for c in range(past // tch):
            dst[c * tch:(c + 1) * tch, :] = src[slot, :, :, c * tch:(c + 1) * tch].reshape(LANE, tch).T
    kv = _compress(lambda j: xk[pl.ds(j, n_chunk, stride=CMP_STRIDE), :],
                   lambda j: xv[pl.ds(j, n_chunk, stride=CMP_STRIDE), :], w_ref, pe_ref, n_chunk)
    kc = kv[:, :LANE].astype(BF16)
    vc = kv[:, LANE:].astype(BF16)
    tok = lax.broadcasted_iota(jnp.int32, (SROWS, 1), 0) & (ts - 1)
    qpos = past + tok
    cend = lax.broadcasted_iota(jnp.int32, (1, n_chunk), 1) * CMP_STRIDE + (CMP_BLOCK - 1)
    p = _cmp_probs(_dot_t(q_ref[0], kc), cend <= qpos)
    ocmp_ref[0] = _dot(p.astype(BF16), vc)
    psum = []
    for k in range(NSA_KV_HEADS):
        acc = p[k * NSA_GROUP * ts:k * NSA_GROUP * ts + ts]
        for g in range(1, NSA_GROUP):
            r = (k * NSA_GROUP + g) * ts
            acc = acc + p[r:r + ts]
        psum.append(acc)
    psum = jnp.concatenate(psum, axis=0)
    imp = jnp.dot(psum, c2s_ref[...], preferred_element_type=F32, precision=lax.Precision.HIGHEST)
    q_blk = lax.shift_right_logical(past + (lax.broadcasted_iota(jnp.int32, (NSA_KV_HEADS * ts, 1), 0) & (ts - 1)), 6)
    sel_ref[0] = _select_blocks(imp, q_blk, n_sel)


def _nsa_s2_kernel(pt_ref, cache_ref, q_ref, sel_ref, new_ref, win_ref, wnew_ref, ocmp_ref, g_ref, e_ref,
                   o_ref, bufk, bufv, sem, *, n_pages, page, ts, wb):
    past = n_pages * page
    slot = _stream_pages(cache_ref, pt_ref, [(2, bufk), (3, bufv)], sem, n_pages, page)
    q = q_ref[0]
    tok = lax.broadcasted_iota(jnp.int32, (SROWS, 1), 0) & (ts - 1)
    sel = sel_ref[0]
    sel_rows = jnp.concatenate([sel[k * ts:(k + 1) * ts] for k in range(NSA_KV_HEADS) for _ in range(NSA_GROUP)],
                               axis=0)
    n_new = new_ref.shape[1]
    jnew = lax.broadcasted_iota(jnp.int32, (1, n_new), 1)

    def attend(kt_past, vt_past, valid_past, k_new, v_new, valid_new):
        s_past = jnp.where(valid_past, _dot(q, kt_past), NEG)
        s_new = jnp.where(valid_new, _dot_t(q, k_new), NEG)
        m = jnp.maximum(jnp.max(s_past, axis=-1, keepdims=True), jnp.max(s_new, axis=-1, keepdims=True))
        p_past = jnp.exp2(s_past - m)
        p_new = jnp.exp2(s_new - m)
        l = jnp.sum(p_past, axis=-1, keepdims=True) + jnp.sum(p_new, axis=-1, keepdims=True)
        return (_dot_t(p_past.astype(BF16), vt_past) + _dot(p_new.astype(BF16), v_new)) / l

    mask_past = _dot(sel_rows.astype(BF16), e_ref[...]) > 0.5
    blk_new = past // SEL_BLOCK
    valid_new = (sel_rows[:, blk_new:blk_new + 1] > 0.5) & (jnew <= tok)
    o_sel = attend(bufk[slot].reshape(LANE, past).astype(BF16), bufv[slot].reshape(LANE, past).astype(BF16),
                   mask_past, new_ref[0, :, 0:LANE].astype(BF16), new_ref[0, :, LANE:HALF_W].astype(BF16), valid_new)
    u = lax.broadcasted_iota(jnp.int32, (1, wb), 1)
    rel = wb + tok - u
    valid_w = (rel >= 0) & (rel < WINDOW) & (past - wb + u >= 0)
    o_win = attend(win_ref[0, 0].reshape(LANE, wb).astype(BF16), win_ref[0, 1].reshape(LANE, wb).astype(BF16),
                   valid_w, wnew_ref[0, :, 0:LANE].astype(BF16), wnew_ref[0, :, LANE:HALF_W].astype(BF16),
                   jnew <= tok)
    gt = g_ref[0]
    o = ocmp_ref[0] * gt[:, 0:1] + o_sel * gt[:, 1:2] + o_win * gt[:, 2:3]
    half = SROWS // NSA_KV_HEADS
    for k in range(NSA_KV_HEADS):
        o_ref[0, k * half:(k + 1) * half, :] = o[k * half:(k + 1) * half, k * NSA_HD:(k + 1) * NSA_HD]


def nsa_sample_attn(page_table, cache, qc2, qr2, new_sel, win, new_win, gates, cmp_w, cmp_pe, ts):
    bs, n_pages = page_table.shape
    page = cache.shape[-1]
    past = n_pages * page
    wb = win.shape[-1]
    assert past % SEL_BLOCK == 0 and ts <= CMP_STRIDE and ts & (ts - 1) == 0 and SROWS == NSA_HEADS * ts
    n_chunk = past // CMP_STRIDE
    n_sel = past // SEL_BLOCK + 1
    n_sel_pad = -(-n_sel // LANE) * LANE
    c2s = jnp.asarray(_c2s(n_chunk, n_sel_pad))
    e = jnp.asarray(_expand(n_sel_pad, past), dtype=BF16)
    sems = pltpu.SemaphoreType.DMA((2,))
    pages = pltpu.VMEM((2, NSA_KV_HEADS, NSA_HD, past), F32)
    scratch1 = [pages, pages, pltpu.VMEM((past, LANE), F32), pltpu.VMEM((past, LANE), F32), sems]
    scratch2 = [pages, pages, sems]
    seq3 = lambda b, pt: (b, 0, 0)
    o_cmp, sel = pl.pallas_call(
        functools.partial(_nsa_s1_kernel, n_pages=n_pages, page=page, ts=ts, n_sel=n_sel),
        grid_spec=pltpu.PrefetchScalarGridSpec(
            num_scalar_prefetch=1, grid=(bs,),
            in_specs=[pl.BlockSpec(memory_space=pl.ANY),
                      pl.BlockSpec((1, SROWS, LANE), seq3),
                      pl.BlockSpec(cmp_w.shape, lambda b, pt: (0, 0, 0, 0, 0)),
                      pl.BlockSpec(cmp_pe.shape, lambda b, pt: (0, 0, 0, 0)),
                      pl.BlockSpec(c2s.shape, lambda b, pt: (0, 0))],
            out_specs=[pl.BlockSpec((1, SROWS, LANE), seq3),
                       pl.BlockSpec((1, NSA_KV_HEADS * ts, n_sel_pad), seq3)],
            scratch_shapes=scratch1),
        out_shape=[jax.ShapeDtypeStruct((bs, SROWS, LANE), F32),
                   jax.ShapeDtypeStruct((bs, NSA_KV_HEADS * ts, n_sel_pad), F32)],
        compiler_params=_cparams(("arbitrary",), 48), name="nsa_sample_cmp",
    )(page_table, cache, qc2, cmp_w, cmp_pe, c2s)
    n_new = new_sel.shape[1]
    return pl.pallas_call(
        functools.partial(_nsa_s2_kernel, n_pages=n_pages, page=page, ts=ts, wb=wb),
        grid_spec=pltpu.PrefetchScalarGridSpec(
            num_scalar_prefetch=1, grid=(bs,),
            in_specs=[pl.BlockSpec(memory_space=pl.ANY),
                      pl.BlockSpec((1, SROWS, LANE), seq3),
                      pl.BlockSpec((1, NSA_KV_HEADS * ts, n_sel_pad), seq3),
                      pl.BlockSpec((1, n_new, HALF_W), seq3),
                      pl.BlockSpec((1, 2, NSA_KV_HEADS, NSA_HD, wb), lambda b, pt: (b, 0, 0, 0, 0)),
                      pl.BlockSpec((1, n_new, HALF_W), seq3),
                      pl.BlockSpec((1, SROWS, LANE), seq3),
                      pl.BlockSpec((1, SROWS, 3), seq3),
                      pl.BlockSpec(e.shape, lambda b, pt: (0, 0))],
            out_specs=pl.BlockSpec((1, SROWS, NSA_HD), seq3),
            scratch_shapes=scratch2),
        out_shape=jax.ShapeDtypeStruct((bs, SROWS, NSA_HD), F32),
        compiler_params=_cparams(("arbitrary",), 56), name="nsa_sample_sel",
    )(page_table, cache, qr2, sel, new_sel, win, new_win, o_cmp, gates, e)


def _in_proj(x2d, pos_rows, p, tm, dims_major):
    q_pre, kv_pre, gate_pre, qkv_pre, z_pre, ba_pre = rms_matmul(x2d, p["attn_norm_w"], p["w_in_parts"], tm)
    cos_t, sin_t = _rope_tables(pos_rows)
    nsa = nsa_prep_v2(q_pre, kv_pre, gate_pre, cos_t, sin_t, p["nsa_q_norm_w"], p["nsa_k_norm_w"], tm, dims_major)
    return nsa, qkv_pre, z_pre, ba_pre


def _tokens_major(a):
    return a.transpose(0, 4, 1, 2, 3)


def _post_mixer(x2d, o_nsa, o_gdn, mem_kv, p, n_seq, tq):
    h = matmul_res([o_nsa, o_gdn], p["w_out"], x2d, 512)
    (qx,) = rms_matmul(h, p["xattn_norm_w"], [p["w_xq"]], 512)
    return matmul_res([xattn_core(qx, mem_kv, p["xq_norm_w"], n_seq, tq)], p["w_xo"], h, 512)


def _ffn(h, p, prev, n_seq, shift, tm):
    return conv_ffn(h, p["ffn_norm_w"], p["w_up"], p["ffn_conv_w"], p["ffn_conv_b"], p["w_down"], prev,
                    n_seq, shift, tm)


def _layer_prompt(x, mem, p):
    nb, t, d = x.shape
    x2d = x.reshape(nb * t, d)
    (rows_t, wrows_t, kvb, qc, qr, gates, cmp_rows), qkv_pre, z_pre, ba_pre = _in_proj(x2d, jnp.arange(t), p, 512, True)
    kvc = compress_prompt(cmp_rows.reshape(nb, t, CMP_W), p["cmp_bd"], p["cmp_pe2"])
    o_nsa = nsa_prompt_attn_v2(qc, qr, kvc, kvb, gates, nb, 256)
    o_gdn, gdn_buf, gdn_s = gdn(qkv_pre, z_pre, ba_pre, jnp.zeros((nb, GDN_CONV - 1, GDN_C3), F32),
                                jnp.zeros((nb, GDN_HEADS, GDN_HD, GDN_HD), F32), p["gdn_conv_w"], p["gdn_a_log"],
                                p["gdn_dt_bias"], p["gdn_norm_w"], nb, 256)
    mem_kv = mem_kv_proj(mem.reshape(-1, d), p["mem_norm_w"], p["w_xk"], p["w_xv"], p["xk_norm_w"])
    mem_kv = mem_kv.reshape(nb, mem.shape[1], 2 * XA_HEADS * XA_HD)
    h = _post_mixer(x2d, o_nsa, o_gdn, mem_kv, p, nb, 256)
    y, ffn_buf = _ffn(h, p, jnp.zeros((nb, FFN_CONV - 1, 2 * D_FF), F32), nb, 1, 512)
    keep = min(WINDOW, t)
    return (y.reshape(nb, t, d), _tokens_major(rows_t), _tokens_major(wrows_t[..., t - keep:]), gdn_s, gdn_buf,
            ffn_buf, mem_kv.reshape(nb, mem.shape[1], 2, XA_HEADS, XA_HD))


def _layer_sample(x, cache_kv, cache_win, state, gdn_cache, ffn_cache, mem_kv, page_table, p):
    bs, ts, d = x.shape
    n_pool, page = cache_kv.shape[:2]
    past = page_table.shape[1] * page
    x2d = x.reshape(bs * ts, d)
    pos_rows = jnp.tile(past + jnp.arange(ts), bs)
    (rows, wrows, _, qc, qr, gates), qkv_pre, z_pre, ba_pre = _in_proj(x2d, pos_rows, p, bs * ts, False)
    dims_major = lambda a: a.transpose(0, 2, 3, 4, 1)

    def stack_q(a):
        return a.reshape(bs, ts, NSA_HEADS, LANE).transpose(0, 2, 1, 3).reshape(bs, SROWS, LANE)

    def pad_new(a):
        return jnp.pad(a.reshape(bs, ts, HALF_W), ((0, 0), (0, LANE - ts), (0, 0)))

    wb = cache_win.shape[1]
    gates3 = gates[:, :NSA_GATE_W].reshape(bs, ts, NSA_HEADS, 3).transpose(0, 2, 1, 3).reshape(bs, SROWS, 3)
    win_t = dims_major(cache_win)
    o2 = nsa_sample_attn(page_table, dims_major(cache_kv), stack_q(qc), stack_q(qr),
                         pad_new(rows[:, HALF_W:]), win_t, pad_new(wrows), gates3, p["cmp_bd"], p["cmp_pe2"], ts)
    rows = rows.reshape(bs, ts, 4, NSA_KV_HEADS, NSA_HD)
    wrows = wrows.reshape(bs, ts, 2, NSA_KV_HEADS, NSA_HD)
    o_nsa = o2.reshape(bs, NSA_HEADS, ts, NSA_HD).transpose(0, 2, 1, 3).reshape(bs * ts, NSA_Q_W)
    o_gdn, gdn_buf, gdn_s = gdn(qkv_pre, z_pre, ba_pre, gdn_cache, state, p["gdn_conv_w"], p["gdn_a_log"],
                                p["gdn_dt_bias"], p["gdn_norm_w"], bs, ts)
    h = _post_mixer(x2d, o_nsa, o_gdn, mem_kv.reshape(bs, mem_kv.shape[1], -1), p, bs, ts)
    h_tm = h.reshape(bs, ts, d).transpose(1, 0, 2).reshape(ts * bs, d)
    prev = ffn_cache.transpose(1, 0, 2).reshape(1, (FFN_CONV - 1) * bs, 2 * D_FF)
    y, ffn_buf = _ffn(h_tm, p, prev, 1, bs, ts * bs)
    y = y.reshape(ts, bs, d).transpose(1, 0, 2)
    ffn_buf = ffn_buf.reshape(FFN_CONV - 1, bs, 2 * D_FF).transpose(1, 0, 2)
    wall_t = jnp.concatenate([win_t, dims_major(wrows)], axis=-1)
    keep = min(WINDOW, wb + ts)
    return y, rows, _tokens_major(wall_t[..., wb + ts - keep:]), gdn_s, gdn_buf, ffn_buf


def _prep_params(l, attn_norm_w, w_in, nsa_q_norm_w, nsa_k_norm_w, cmp_pe, cmp_w, gdn_conv_w, gdn_a_log,
                 gdn_dt_bias, gdn_norm_w, w_out, mem_norm_w, w_xk, w_xv, xk_norm_w, xattn_norm_w, w_xq,
                 xq_norm_w, w_xo, ffn_norm_w, w_up, ffn_conv_w, ffn_conv_b, w_down):
    wi = w_in[l]
    cuts = np.cumsum([0, NSA_Q_W, NSA_KV_W, NSA_GATE_W, 3 * GDN_W, GDN_W, 2 * GDN_HEADS])
    parts = []
    for a, b in zip(cuts[:-1], cuts[1:]):
        w = wi[:, a:b]
        padn = -(-(b - a) // LANE) * LANE - (b - a)
        parts.append(jnp.pad(w, ((0, 0), (0, padn))).astype(BF16))
    cmp_bd, cmp_pe2 = _cmp_weights(cmp_pe[l], cmp_w[l])
    bf = lambda w: w[l].astype(BF16)
    return dict(attn_norm_w=attn_norm_w[l], w_in_parts=parts, nsa_q_norm_w=nsa_q_norm_w[l],
                nsa_k_norm_w=nsa_k_norm_w[l], cmp_bd=cmp_bd, cmp_pe2=cmp_pe2, gdn_conv_w=gdn_conv_w[l],
                gdn_a_log=gdn_a_log[l], gdn_dt_bias=gdn_dt_bias[l], gdn_norm_w=gdn_norm_w[l], w_out=bf(w_out),
                mem_norm_w=mem_norm_w[l], w_xk=bf(w_xk), w_xv=bf(w_xv), xk_norm_w=xk_norm_w[l],
                xattn_norm_w=xattn_norm_w[l], w_xq=bf(w_xq), xq_norm_w=xq_norm_w[l], w_xo=bf(w_xo),
                ffn_norm_w=ffn_norm_w[l], w_up=bf(w_up), ffn_conv_w=ffn_conv_w[l], ffn_conv_b=ffn_conv_b[l],
                w_down=bf(w_down))


def kernel(x_prompt, x_sample, mem_prompt, cache_nsa_kv, cache_nsa_win, state_gdn, cache_gdn_conv, cache_ffn_conv,
           cache_mem_kv, page_table, attn_norm_w, w_in, nsa_q_norm_w, nsa_k_norm_w, cmp_pe, cmp_w, gdn_conv_w,
           gdn_a_log, gdn_dt_bias, gdn_norm_w, w_out, mem_norm_w, w_xk, w_xv, xk_norm_w, xattn_norm_w, w_xq,
           xq_norm_w, w_xo, ffn_norm_w, w_up, ffn_conv_w, ffn_conv_b, w_down):
    weights = (attn_norm_w, w_in, nsa_q_norm_w, nsa_k_norm_w, cmp_pe, cmp_w, gdn_conv_w, gdn_a_log, gdn_dt_bias,
               gdn_norm_w, w_out, mem_norm_w, w_xk, w_xv, xk_norm_w, xattn_norm_w, w_xq, xq_norm_w, w_xo,
               ffn_norm_w, w_up, ffn_conv_w, ffn_conv_b, w_down)
    depth = cache_nsa_kv.shape[0]
    hp, hs = x_prompt, x_sample
    outs_p, outs_s = [], []
    for l in range(depth):
        p = _prep_params(l, *weights)
        res_p = _layer_prompt(hp, mem_prompt, p)
        hp = res_p[0]
        outs_p.append(res_p[1:])
        res_s = _layer_sample(hs, cache_nsa_kv[l], cache_nsa_win[l], state_gdn[l], cache_gdn_conv[l],
                              cache_ffn_conv[l], cache_mem_kv[l], page_table, p)
        hs = res_s[0]
        outs_s.append(res_s[1:])
    stack = lambda outs, i: jnp.stack([o[i] for o in outs])
    return ((hp, hs) + tuple(stack(outs_p, i) for i in range(6)) + tuple(stack(outs_s, i) for i in range(5)))
```

```python
import functools
import math

import numpy as np
import jax
import jax.numpy as jnp
from jax import lax
from jax.experimental import pallas as pl
from jax.experimental.pallas import tpu as pltpu

F32 = jnp.float32
BF16 = jnp.bfloat16

D_MODEL = 1024
NSA_HEADS = 8
NSA_KV_HEADS = 2
NSA_GROUP = NSA_HEADS // NSA_KV_HEADS
NSA_HD = 64
CMP_BLOCK = 32
CMP_STRIDE = 16
SEL_BLOCK = 64
SEL_TOPN = 8
WINDOW = 512
FORCED_BONUS = 1e4
GDN_HEADS = 4
GDN_HD = 128
GDN_CONV = 4
GDN_CHUNK = 64
GDN_W = GDN_HEADS * GDN_HD
XA_HEADS = 4
XA_HD = 256
D_FF = 2816
FFN_CONV = 3
ROPE_THETA = 10000.0
EPS = 1e-6
NEG = -1e30

NSA_Q_W = NSA_HEADS * NSA_HD
NSA_KV_W = 3 * 2 * NSA_KV_HEADS * NSA_HD
NSA_GATE_W = 3 * NSA_HEADS
LANE = 128
MIB = 1 << 20


def _cparams(sem, vmem_mib):
    return pltpu.CompilerParams(dimension_semantics=sem, vmem_limit_bytes=vmem_mib * MIB)


def _sigmoid(x):
    return 1.0 / (1.0 + jnp.exp(-x))


def _dot(a, b):
    return jnp.dot(a, b, preferred_element_type=F32)


def _dot_t(a, b):
    return lax.dot_general(a, b, (((1,), (1,)), ((), ())), preferred_element_type=F32)


def _split_bf16(a):
    hi = a.astype(BF16)
    lo = (a - hi.astype(F32)).astype(BF16)
    return hi, lo


def _dot3(a, b):
    ah, al = _split_bf16(a)
    bh, bl = _split_bf16(b)
    return _dot(ah, bh) + (_dot(ah, bl) + _dot(al, bh))


def _col_chunk(n):
    for c in (512, 256, 128):
        if n % c == 0:
            return c
    return n


def _rms_mm_kernel(x_ref, g_ref, *refs, n_out):
    w_refs, o_refs = refs[:n_out], refs[n_out:]
    x = x_ref[...]
    xn = (x * lax.rsqrt(jnp.mean(x * x, axis=-1, keepdims=True) + EPS) * g_ref[...]).astype(BF16)
    for w_ref, o_ref in zip(w_refs, o_refs):
        n = w_ref.shape[1]
        ch = _col_chunk(n)
        for c in range(0, n, ch):
            o_ref[:, c:c + ch] = _dot(xn, w_ref[:, c:c + ch])


def rms_matmul(x, g, ws, tm):
    n_rows, d = x.shape
    tm = min(tm, n_rows)
    in_specs = [pl.BlockSpec((tm, d), lambda i: (i, 0)), pl.BlockSpec((1, d), lambda i: (0, 0))]
    in_specs += [pl.BlockSpec(w.shape, lambda i: (0, 0)) for w in ws]
    out_specs = [pl.BlockSpec((tm, w.shape[1]), lambda i: (i, 0)) for w in ws]
    out_shape = [jax.ShapeDtypeStruct((n_rows, w.shape[1]), F32) for w in ws]
    return pl.pallas_call(
        functools.partial(_rms_mm_kernel, n_out=len(ws)),
        grid=(n_rows // tm,), in_specs=in_specs, out_specs=out_specs, out_shape=out_shape,
        compiler_params=_cparams(("parallel",), 56), name="rms_matmul",
    )(x, g.reshape(1, d), *ws)


def _mm_res_kernel(*refs, n_in):
    a_refs, (w_ref, r_ref, o_ref) = refs[:n_in], refs[n_in:]
    a = [a_ref[...].astype(BF16) for a_ref in a_refs]
    n = w_ref.shape[1]
    ch = _col_chunk(n)
    for c in range(0, n, ch):
        acc = r_ref[:, c:c + ch]
        k0 = 0
        for x in a:
            acc = acc + _dot(x, w_ref[k0:k0 + x.shape[1], c:c + ch])
            k0 += x.shape[1]
        o_ref[:, c:c + ch] = acc


def matmul_res(a_list, w, res, tm):
    n_rows = res.shape[0]
    k, n = w.shape
    assert sum(a.shape[1] for a in a_list) == k
    tm = min(tm, n_rows)
    return pl.pallas_call(
        functools.partial(_mm_res_kernel, n_in=len(a_list)), grid=(n_rows // tm,),
        in_specs=[pl.BlockSpec((tm, a.shape[1]), lambda i: (i, 0)) for a in a_list]
        + [pl.BlockSpec((k, n), lambda i: (0, 0)), pl.BlockSpec((tm, n), lambda i: (i, 0))],
        out_specs=pl.BlockSpec((tm, n), lambda i: (i, 0)),
        out_shape=jax.ShapeDtypeStruct((n_rows, n), F32),
        compiler_params=_cparams(("parallel",), 40), name="matmul_res",
    )(*a_list, w, res)


def _memkv_kernel(x_ref, g_ref, wk_ref, wv_ref, kn_ref, o_ref):
    x = x_ref[...]
    xn = (x * lax.rsqrt(jnp.mean(x * x, axis=-1, keepdims=True) + EPS) * g_ref[...]).astype(BF16)
    xa_w = XA_HEADS * XA_HD
    for h in range(XA_HEADS):
        sl = slice(h * XA_HD, (h + 1) * XA_HD)
        k = _dot(xn, wk_ref[:, sl])
        k = k * lax.rsqrt(jnp.mean(k * k, axis=-1, keepdims=True) + EPS) * kn_ref[...]
        o_ref[:, sl] = k
        o_ref[:, xa_w + h * XA_HD: xa_w + (h + 1) * XA_HD] = _dot(xn, wv_ref[:, sl])


def mem_kv_proj(mem, g, wk, wv, kn, tm=256):
    n_rows, d = mem.shape
    xa_w = XA_HEADS * XA_HD
    return pl.pallas_call(
        _memkv_kernel, grid=(n_rows // tm,),
        in_specs=[pl.BlockSpec((tm, d), lambda i: (i, 0)), pl.BlockSpec((1, d), lambda i: (0, 0)),
                  pl.BlockSpec((d, xa_w), lambda i: (0, 0)), pl.BlockSpec((d, xa_w), lambda i: (0, 0)),
                  pl.BlockSpec((1, XA_HD), lambda i: (0, 0))],
        out_specs=pl.BlockSpec((tm, 2 * xa_w), lambda i: (i, 0)),
        out_shape=jax.ShapeDtypeStruct((n_rows, 2 * xa_w), F32),
        compiler_params=_cparams(("parallel",), 40), name="mem_kv_proj",
    )(mem, g.reshape(1, d), wk, wv, kn.reshape(1, XA_HD))


def _xattn_kernel(q_ref, kv_ref, qn_ref, o_ref):
    xa_w = XA_HEADS * XA_HD
    for h in range(XA_HEADS):
        sl = slice(h * XA_HD, (h + 1) * XA_HD)
        q = q_ref[:, sl]
        q = q * lax.rsqrt(jnp.mean(q * q, axis=-1, keepdims=True) + EPS) * qn_ref[...]
        qb = (q * (XA_HD ** -0.5)).astype(BF16)
        k = kv_ref[0, :, sl].astype(BF16)
        v = kv_ref[0, :, xa_w + h * XA_HD: xa_w + (h + 1) * XA_HD].astype(BF16)
        s = _dot_t(qb, k)
        p = jnp.exp(s - jnp.max(s, axis=-1, keepdims=True))
        o = _dot(p.astype(BF16), v) / jnp.sum(p, axis=-1, keepdims=True)
        o_ref[:, sl] = o


def xattn_core(q_pre, kv, qn, n_batch, tq):
    n_rows, xa_w = q_pre.shape
    t = n_rows // n_batch
    tq = min(tq, t)
    nt = t // tq
    m = kv.shape[1]
    return pl.pallas_call(
        _xattn_kernel, grid=(n_batch, nt),
        in_specs=[pl.BlockSpec((tq, xa_w), lambda b, i: (b * nt + i, 0)),
                  pl.BlockSpec((1, m, 2 * xa_w), lambda b, i: (b, 0, 0)),
                  pl.BlockSpec((1, XA_HD), lambda b, i: (0, 0))],
        out_specs=pl.BlockSpec((tq, xa_w), lambda b, i: (b * nt + i, 0)),
        out_shape=jax.ShapeDtypeStruct((n_rows, xa_w), F32),
        compiler_params=_cparams(("parallel", "parallel"), 40), name="xattn_core",
    )(q_pre, kv, qn.reshape(1, XA_HD))


FFN_ACT_CHUNK = 256


def _ffn_kernel(h_ref, g_ref, wup_ref, cw_ref, cb_ref, wdn_ref, prev_ref, o_ref, buf_ref, xs_scr,
                *, tm, shift, base):
    t = pl.program_id(1)
    p0 = base - 2 * shift

    @pl.when(t == 0)
    def _():
        xs_scr[p0:base, :] = prev_ref[0]

    x = h_ref[...]
    xn = (x * lax.rsqrt(jnp.mean(x * x, axis=-1, keepdims=True) + EPS) * g_ref[...]).astype(BF16)
    acc = jnp.zeros((tm, D_MODEL), F32)
    for j in range(D_FF // FFN_ACT_CHUNK):
        halves = []
        for c0 in (j * FFN_ACT_CHUNK, D_FF + j * FFN_ACT_CHUNK):
            sl = slice(c0, c0 + FFN_ACT_CHUNK)
            xs_scr[base:base + tm, sl] = _dot(xn, wup_ref[:, sl])
            y = (cw_ref[0:1, sl] * xs_scr[p0:p0 + tm, sl]
                 + cw_ref[1:2, sl] * xs_scr[p0 + shift:p0 + shift + tm, sl]
                 + cw_ref[2:3, sl] * xs_scr[base:base + tm, sl])
            halves.append(y + cb_ref[:, sl])
        a, u = halves
        act = (a * _sigmoid(a) * u).astype(BF16)
        acc = acc + _dot(act, wdn_ref[j * FFN_ACT_CHUNK:(j + 1) * FFN_ACT_CHUNK, :])
    o_ref[...] = x + acc
    last = xs_scr[base + tm - 2 * shift: base + tm, :]
    buf_ref[0] = last
    xs_scr[p0:base, :] = last


def conv_ffn(h, g, wup, cw, cb, wdn, prev, n_seq, shift, tm):
    n_rows, d = h.shape
    t_rows = n_rows // n_seq
    tm = min(tm, t_rows)
    nt = t_rows // tm
    base = -(-2 * shift // 8) * 8
    kern = functools.partial(_ffn_kernel, tm=tm, shift=shift, base=base)
    const = lambda b, i: (0, 0)
    return pl.pallas_call(
        kern, grid=(n_seq, nt),
        in_specs=[pl.BlockSpec((tm, d), lambda b, i: (b * nt + i, 0)),
                  pl.BlockSpec((1, d), const),
                  pl.BlockSpec((d, 2 * D_FF), const, pipeline_mode=pl.Buffered(1)),
                  pl.BlockSpec((FFN_CONV, 2 * D_FF), const),
                  pl.BlockSpec((1, 2 * D_FF), const),
                  pl.BlockSpec((D_FF, d), const, pipeline_mode=pl.Buffered(1)),
                  pl.BlockSpec((1, 2 * shift, 2 * D_FF), lambda b, i: (b, 0, 0))],
        out_specs=[pl.BlockSpec((tm, d), lambda b, i: (b * nt + i, 0)),
                   pl.BlockSpec((1, 2 * shift, 2 * D_FF), lambda b, i: (b, 0, 0))],
        out_shape=[jax.ShapeDtypeStruct((n_rows, d), F32),
                   jax.ShapeDtypeStruct((n_seq, 2 * shift, 2 * D_FF), F32)],
        scratch_shapes=[pltpu.VMEM((base + tm, 2 * D_FF), F32)],
        compiler_params=_cparams(("arbitrary", "arbitrary"), 56), name="conv_ffn",
    )(h, g.reshape(1, d), wup, cw, cb.reshape(1, 2 * D_FF), wdn, prev)


GDN_STACK = GDN_HEADS * GDN_CHUNK
GDN_C3 = 3 * GDN_W
GDN_PREV0 = 8 - (GDN_CONV - 1)


def _gdn_masks():
    r = np.arange(GDN_STACK)[:, None]
    c = np.arange(GDN_STACK)[None, :]
    same = lambda n: (r // n) == (c // n)
    m = [same(2) & (c < r)]
    for n in (4, 8, 16, 32, 64):
        m.append(same(n) & ~same(n // 2) & (c < r))
    m.append(same(GDN_CHUNK) & (c <= r))
    m.append(same(GDN_CHUNK) & (c < r))
    m.append(r == c)
    return np.stack(m).astype(np.float32)


def _unit_lower_inverse(a, m_ref):
    t = m_ref[8] - a * m_ref[0]
    for lvl in range(1, 6):
        off = a * m_ref[lvl]
        t = t - _dot3(t, _dot3(off, t))
    return t


def _gdn_kernel_v1(qkv_ref, z_ref, ba_ref, prev_ref, s0_ref, cw_ref, par_ref, nw_ref, m_ref,
                   o_ref, buf_ref, s_ref, xs_scr, s_scr, *, tt):
    t = pl.program_id(1)
    nt = pl.num_programs(1)
    row0 = 8

    @pl.when(t == 0)
    def _():
        xs_scr[GDN_PREV0:row0, :] = prev_ref[0]
        s_scr[...] = s0_ref[0]

    xs_scr[row0:row0 + tt, :] = qkv_ref[...]
    causal = m_ref[6]
    neg_ea = -jnp.exp(par_ref[0:1, :])
    dtb = par_ref[1:2, :]
    rv = min(GDN_CHUNK, tt)
    pad = GDN_CHUNK - rv

    def padrows(a):
        if pad == 0:
            return a
        return jnp.concatenate([a, jnp.zeros((pad, a.shape[1]), a.dtype)], axis=0)

    for n in range(-(-tt // GDN_CHUNK)):
        r0 = n * GDN_CHUNK
        y = cw_ref[0:1, :] * xs_scr[GDN_PREV0 + r0:GDN_PREV0 + r0 + rv, :]
        for i in range(1, GDN_CONV):
            y = y + cw_ref[i:i + 1, :] * xs_scr[GDN_PREV0 + i + r0:GDN_PREV0 + i + r0 + rv, :]
        y = y * _sigmoid(y)
        qs, ks, vs, betas, gs = [], [], [], [], []
        for h in range(GDN_HEADS):
            q = y[:, h * GDN_HD:(h + 1) * GDN_HD]
            k = y[:, GDN_W + h * GDN_HD:GDN_W + (h + 1) * GDN_HD]
            v = y[:, 2 * GDN_W + h * GDN_HD:2 * GDN_W + (h + 1) * GDN_HD]
            q = q * lax.rsqrt(jnp.sum(q * q, axis=-1, keepdims=True) + EPS) * (GDN_HD ** -0.5)
            k = k * lax.rsqrt(jnp.sum(k * k, axis=-1, keepdims=True) + EPS)
            bcol = ba_ref[r0:r0 + rv, h:h + 1]
            acol = ba_ref[r0:r0 + rv, GDN_HEADS + h:GDN_HEADS + h + 1] + dtb[:, h:h + 1]
            softplus = jnp.maximum(acol, 0.0) + jnp.log1p(jnp.exp(-jnp.abs(acol)))
            qs.append(padrows(q)); ks.append(padrows(k)); vs.append(padrows(v))
            betas.append(padrows(_sigmoid(bcol)))
            gs.append(padrows(neg_ea[:, h:h + 1] * softplus))
        qm = jnp.concatenate(qs, axis=0)
        km = jnp.concatenate(ks, axis=0)
        vm = jnp.concatenate(vs, axis=0)
        beta = jnp.concatenate(betas, axis=0)
        g = jnp.concatenate(gs, axis=0)
        gc = jnp.dot(causal, jnp.broadcast_to(g, (GDN_STACK, LANE)), preferred_element_type=F32,
                     precision=lax.Precision.HIGHEST)[:, 0:1]
        gc_row = jnp.sum(m_ref[8] * gc, axis=0, keepdims=True)
        diff = gc - gc_row
        decay = jnp.exp(jnp.where(causal > 0, diff, 0.0)) * causal
        kb = km * beta
        kmb = km.astype(BF16)
        a_mat = _dot_t(kb.astype(BF16), kmb) * decay * m_ref[7]
        qk = _dot_t(qm.astype(BF16), kmb) * decay
        tinv = _unit_lower_inverse(a_mat, m_ref)
        egc = jnp.exp(gc)
        sol = _dot3(tinv, jnp.concatenate([vm * beta, kb * egc], axis=1))
        u_all, w_all = sol[:, :GDN_HD], sol[:, GDN_HD:]
        qd = qm * egc
        v_news, o_inter, kds, gls = [], [], [], []
        for h in range(GDN_HEADS):
            rs = slice(h * GDN_CHUNK, (h + 1) * GDN_CHUNK)
            sb = s_scr[h].astype(BF16)
            v_new = u_all[rs] - _dot(w_all[rs].astype(BF16), sb)
            o_inter.append(_dot(qd[rs].astype(BF16), sb))
            gl = gc[h * GDN_CHUNK + GDN_CHUNK - 1:(h + 1) * GDN_CHUNK, :]
            kds.append(km[rs] * jnp.exp(gl - gc[rs]))
            v_news.append(v_new)
            gls.append(gl)
        v_stack = jnp.concatenate(v_news, axis=0).astype(BF16)
        o_intra = _dot(qk.astype(BF16), v_stack)
        for h in range(GDN_HEADS):
            rs = slice(h * GDN_CHUNK, (h + 1) * GDN_CHUNK)
            s_scr[h] = s_scr[h] * jnp.exp(gls[h]) + _dot(kds[h].T.astype(BF16), v_stack[rs])
            o = (o_inter[h] + o_intra[rs])[:rv]
            on = o * lax.rsqrt(jnp.mean(o * o, axis=-1, keepdims=True) + EPS) * nw_ref[...]
            zz = z_ref[r0:r0 + rv, h * GDN_HD:(h + 1) * GDN_HD]
            o_ref[r0:r0 + rv, h * GDN_HD:(h + 1) * GDN_HD] = on * (zz * _sigmoid(zz))

    last = xs_scr[row0 + tt - (GDN_CONV - 1):row0 + tt, :]
    buf_ref[0] = last
    xs_scr[GDN_PREV0:row0, :] = last

    @pl.when(t == nt - 1)
    def _():
        s_ref[0] = s_scr[...]


def _gdn_kernel(qkv_ref, z_ref, ba_ref, prev_ref, s0_ref, cw_ref, par_ref, nw_ref, m_ref,
                o_ref, buf_ref, s_ref, xs_scr, s_scr, *, tt):
    t = pl.program_id(1)
    nt = pl.num_programs(1)
    row0 = 8

    @pl.when(t == 0)
    def _():
        xs_scr[GDN_PREV0:row0, :] = prev_ref[0]
        s_scr[...] = s0_ref[0]

    xs_scr[row0:row0 + tt, :] = qkv_ref[...]
    causal, strict, eye = m_ref[6], m_ref[7], m_ref[8]
    neg_ea = -jnp.exp(par_ref[0:1, :])
    dtb = par_ref[1:2, :]
    rv = min(GDN_CHUNK, tt)
    pad = GDN_CHUNK - rv
    n_ch = -(-tt // GDN_CHUNK)

    def padrows(a):
        if pad == 0:
            return a
        return jnp.concatenate([a, jnp.zeros((pad, a.shape[1]), a.dtype)], axis=0)

    qms, kms, vms, betas, gcols = [], [], [], [], []
    for n in range(n_ch):
        r0 = n * GDN_CHUNK
        y = cw_ref[0:1, :] * xs_scr[GDN_PREV0 + r0:GDN_PREV0 + r0 + rv, :]
        for i in range(1, GDN_CONV):
            y = y + cw_ref[i:i + 1, :] * xs_scr[GDN_PREV0 + i + r0:GDN_PREV0 + i + r0 + rv, :]
        y = y * _sigmoid(y)
        qs, ks, vs, bs, gs = [], [], [], [], []
        for h in range(GDN_HEADS):
            q = y[:, h * GDN_HD:(h + 1) * GDN_HD]
            k = y[:, GDN_W + h * GDN_HD:GDN_W + (h + 1) * GDN_HD]
            v = y[:, 2 * GDN_W + h * GDN_HD:2 * GDN_W + (h + 1) * GDN_HD]
            q = q * lax.rsqrt(jnp.sum(q * q, axis=-1, keepdims=True) + EPS) * (GDN_HD ** -0.5)
            k = k * lax.rsqrt(jnp.sum(k * k, axis=-1, keepdims=True) + EPS)
            bcol = ba_ref[r0:r0 + rv, h:h + 1]
            acol = ba_ref[r0:r0 + rv, GDN_HEADS + h:GDN_HEADS + h + 1] + dtb[:, h:h + 1]
            softplus = jnp.maximum(acol, 0.0) + jnp.log1p(jnp.exp(-jnp.abs(acol)))
            qs.append(padrows(q)); ks.append(padrows(k)); vs.append(padrows(v))
            bs.append(padrows(_sigmoid(bcol)))
            gs.append(padrows(neg_ea[:, h:h + 1] * softplus))
        qms.append(jnp.concatenate(qs, axis=0))
        kms.append(jnp.concatenate(ks, axis=0))
        vms.append(jnp.concatenate(vs, axis=0))
        betas.append(jnp.concatenate(bs, axis=0))
        gcols.append(jnp.concatenate(gs, axis=0))
    gmat = jnp.concatenate(gcols + [jnp.zeros((GDN_STACK, LANE - n_ch), F32)], axis=1)
    gc_all = jnp.dot(causal, gmat, preferred_element_type=F32, precision=lax.Precision.HIGHEST)

    a_mats, qks, rhss, gcs, tinv = [], [], [], [], []
    for n in range(n_ch):
        gc = gc_all[:, n:n + 1]
        gc_row = jnp.sum(eye * gc, axis=0, keepdims=True)
        decay = jnp.exp(jnp.where(causal > 0, gc - gc_row, 0.0)) * causal
        kb = kms[n] * betas[n]
        kmb = kms[n].astype(BF16)
        a_mat = _dot_t(kb.astype(BF16), kmb) * decay * strict
        qks.append((_dot_t(qms[n].astype(BF16), kmb) * decay).astype(BF16))
        rhss.append(jnp.concatenate([vms[n] * betas[n], kb * jnp.exp(gc)], axis=1).astype(BF16))
        a_mats.append(a_mat)
        gcs.append(gc)
        tinv.append(eye - a_mat * m_ref[0])
    for lvl in range(1, 6):
        tb = [x.astype(BF16) for x in tinv]
        xs = [_dot((a_mats[n] * m_ref[lvl]).astype(BF16), tb[n]).astype(BF16) for n in range(n_ch)]
        tinv = [tinv[n] - _dot(tb[n], xs[n]) for n in range(n_ch)]
    sols = [_dot(tinv[n].astype(BF16), rhss[n]) for n in range(n_ch)]

    for n in range(n_ch):
        r0 = n * GDN_CHUNK
        gc = gcs[n]
        egc = jnp.exp(gc)
        u_all, w_all = sols[n][:, :GDN_HD], sols[n][:, GDN_HD:]
        qd = qms[n] * egc
        v_news, o_inter, gls = [], [], []
        for h in range(GDN_HEADS):
            rs = slice(h * GDN_CHUNK, (h + 1) * GDN_CHUNK)
            sb = s_scr[h].astype(BF16)
            both = _dot(jnp.concatenate([w_all[rs], qd[rs]], axis=0).astype(BF16), sb)
            v_news.append(u_all[rs] - both[:GDN_CHUNK])
            o_inter.append(both[GDN_CHUNK:])
            gls.append(gc[h * GDN_CHUNK + GDN_CHUNK - 1:(h + 1) * GDN_CHUNK, :])
        v_stack = jnp.concatenate(v_news, axis=0).astype(BF16)
        o_intra = _dot(qks[n], v_stack)
        for h in range(GDN_HEADS):
            rs = slice(h * GDN_CHUNK, (h + 1) * GDN_CHUNK)
            kd = kms[n][rs] * jnp.exp(gls[h] - gc[rs])
            s_scr[h] = s_scr[h] * jnp.exp(gls[h]) + _dot(kd.T.astype(BF16), v_stack[rs])
            o = (o_inter[h] + o_intra[rs])[:rv]
            on = o * lax.rsqrt(jnp.mean(o * o, axis=-1, keepdims=True) + EPS) * nw_ref[...]
            zz = z_ref[r0:r0 + rv, h * GDN_HD:(h + 1) * GDN_HD]
            o_ref[r0:r0 + rv, h * GDN_HD:(h + 1) * GDN_HD] = on * (zz * _sigmoid(zz))

    last = xs_scr[row0 + tt - (GDN_CONV - 1):row0 + tt, :]
    buf_ref[0] = last
    xs_scr[GDN_PREV0:row0, :] = last

    @pl.when(t == nt - 1)
    def _():
        s_ref[0] = s_scr[...]


def gdn(qkv_pre, z_pre, ba_pre, prev, s0, conv_w, a_log, dt_bias, norm_w, n_seq, tt):
    n_rows = qkv_pre.shape[0]
    t_rows = n_rows // n_seq
    tt = min(tt, t_rows)
    nt = t_rows // tt
    par = jnp.zeros((8, LANE), F32).at[0, :GDN_HEADS].set(a_log).at[1, :GDN_HEADS].set(dt_bias)
    masks = jnp.asarray(_gdn_masks())
    const2 = lambda b, i: (0, 0)
    rows = lambda b, i: (b * nt + i, 0)
    return pl.pallas_call(
        functools.partial(_gdn_kernel, tt=tt), grid=(n_seq, nt),
        in_specs=[pl.BlockSpec((tt, GDN_C3), rows), pl.BlockSpec((tt, GDN_W), rows),
                  pl.BlockSpec((tt, LANE), rows),
                  pl.BlockSpec((1, GDN_CONV - 1, GDN_C3), lambda b, i: (b, 0, 0)),
                  pl.BlockSpec((1, GDN_HEADS, GDN_HD, GDN_HD), lambda b, i: (b, 0, 0, 0)),
                  pl.BlockSpec((GDN_CONV, GDN_C3), const2), pl.BlockSpec((8, LANE), const2),
                  pl.BlockSpec((1, GDN_HD), const2),
                  pl.BlockSpec(masks.shape, lambda b, i: (0, 0, 0))],
        out_specs=[pl.BlockSpec((tt, GDN_W), rows),
                   pl.BlockSpec((1, GDN_CONV - 1, GDN_C3), lambda b, i: (b, 0, 0)),
                   pl.BlockSpec((1, GDN_HEADS, GDN_HD, GDN_HD), lambda b, i: (b, 0, 0, 0))],
        out_shape=[jax.ShapeDtypeStruct((n_rows, GDN_W), F32),
                   jax.ShapeDtypeStruct((n_seq, GDN_CONV - 1, GDN_C3), F32),
                   jax.ShapeDtypeStruct((n_seq, GDN_HEADS, GDN_HD, GDN_HD), F32)],
        scratch_shapes=[pltpu.VMEM((8 + tt, GDN_C3), F32), pltpu.VMEM((GDN_HEADS, GDN_HD, GDN_HD), F32)],
        compiler_params=_cparams(("arbitrary", "arbitrary"), 48), name="gdn",
    )(qkv_pre, z_pre, ba_pre, prev, s0, conv_w, par, norm_w.reshape(1, GDN_HD), masks)


CMP_W = 2 * NSA_KV_HEADS * NSA_HD


def _cmp_weights(cmp_pe, cmp_w):
    def bd(w):
        z = jnp.zeros_like(w)
        return jnp.concatenate([jnp.concatenate([w, z], 2), jnp.concatenate([z, w], 2)], 1)
    w = jnp.stack([jnp.stack([bd(cmp_w[s, :CMP_STRIDE]), bd(cmp_w[s, CMP_STRIDE:])]) for s in range(2)]).astype(BF16)
    pe2 = jnp.concatenate([cmp_pe, cmp_pe], axis=-1)
    pe = jnp.stack([pe2[:, :CMP_STRIDE], pe2[:, CMP_STRIDE:]], axis=1)
    return w, pe


def _compress(read_k, read_v, w_ref, pe_ref, n_chunk):
    outs = []
    for s, read in enumerate((read_k, read_v)):
        lo = jnp.zeros((n_chunk, LANE), F32)
        hi = jnp.zeros((n_chunk, LANE), F32)
        for j in range(CMP_STRIDE):
            x = read(j)
            lo = lo + _dot((x + pe_ref[s, 0, j:j + 1, :]).astype(BF16), w_ref[s, 0, j])
            hi = hi + _dot((x + pe_ref[s, 1, j:j + 1, :]).astype(BF16), w_ref[s, 1, j])
        outs.append(lo + pltpu.roll(hi, n_chunk - 1, axis=0))
    return jnp.concatenate(outs, axis=1)


def _cmp_kernel(k_ref, v_ref, w_ref, pe_ref, o_ref, *, n_chunk):
    o_ref[0] = _compress(lambda j: k_ref[0, pl.ds(j, n_chunk, stride=CMP_STRIDE), :],
                         lambda j: v_ref[0, pl.ds(j, n_chunk, stride=CMP_STRIDE), :], w_ref, pe_ref, n_chunk)


def compress_prompt(rows, w, pe):
    b, t, _ = rows.shape
    n_chunk = t // CMP_STRIDE
    return pl.pallas_call(
        functools.partial(_cmp_kernel, n_chunk=n_chunk), grid=(b,),
        in_specs=[pl.BlockSpec((1, t, LANE), lambda i: (i, 0, 0)), pl.BlockSpec((1, t, LANE), lambda i: (i, 0, 1)),
                  pl.BlockSpec(w.shape, lambda i: (0, 0, 0, 0, 0)),
                  pl.BlockSpec(pe.shape, lambda i: (0, 0, 0, 0))],
        out_specs=pl.BlockSpec((1, n_chunk, CMP_W), lambda i: (i, 0, 0)),
        out_shape=jax.ShapeDtypeStruct((b, n_chunk, CMP_W), F32),
        compiler_params=_cparams(("parallel",), 32), name="nsa_compress",
    )(rows, rows, w, pe)


def _c2s(n_cmp_pad, n_sel_pad):
    cs = np.arange(n_cmp_pad)[:, None] * CMP_STRIDE
    ss = np.arange(n_sel_pad)[None, :] * SEL_BLOCK
    return ((cs < ss + SEL_BLOCK) & (cs + CMP_BLOCK > ss)).astype(np.float32)


def _expand(n_sel_pad, n_keys):
    s = np.arange(n_sel_pad)[:, None]
    k = np.arange(n_keys)[None, :]
    return (k // SEL_BLOCK == s).astype(np.float32)


def _cmp_probs(s, valid):
    s = jnp.where(valid, s, NEG)
    p = jnp.where(valid, jnp.exp(s - jnp.max(s, axis=-1, keepdims=True)), 0.0)
    return p / jnp.maximum(jnp.sum(p, axis=-1, keepdims=True), 1e-30)


def _select_blocks(imp, q_blk, n_sel):
    lane = lax.broadcasted_iota(jnp.int32, imp.shape, 1)
    visible = lane <= q_blk
    forced = (lane == 0) | (lane == q_blk) | (lane == q_blk - 1)
    score = jnp.where(visible, imp + jnp.where(forced, FORCED_BONUS, 0.0), NEG)
    rank = jnp.zeros(imp.shape, F32)
    for sp in range(n_sel):
        col = score[:, sp:sp + 1]
        beats = (col > score) | ((col == score) & (lane > sp))
        rank = rank + jnp.where(beats, 1.0, 0.0)
    return jnp.where(visible & (rank < SEL_TOPN), 1.0, 0.0)


def _flash(q, k_ref, v_ref, j_lo, j_hi, mask_fn, tq, tk):
    def body(j, carry):
        m, l, acc = carry
        off = pl.multiple_of(j * tk, tk)
        s = _dot_t(q, k_ref[pl.ds(off, tk), :])
        s = jnp.where(mask_fn(j), s, NEG)
        m_new = jnp.maximum(m, jnp.max(s, axis=-1, keepdims=True))
        alpha = jnp.exp(m - m_new)
        p = jnp.exp(s - m_new)
        l = alpha * l + jnp.sum(p, axis=-1, keepdims=True)
        acc = alpha * acc + _dot(p.astype(BF16), v_ref[pl.ds(off, tk), :])
        return m_new, l, acc

    init = (jnp.full((tq, 1), NEG, F32), jnp.zeros((tq, 1), F32), jnp.zeros((tq, NSA_HD), F32))
    _, l, acc = lax.fori_loop(j_lo, j_hi, body, init)
    return acc / l


def _nsa_prompt_kernel_v1(qc_ref, qr_ref, kc_ref, vc_ref, ks_ref, vs_ref, kw_ref, vw_ref, g_ref, c2s_ref, e_ref,
                       o_ref, mask_scr, *, tq, n_sel, n_cmp_pad):
    i = pl.program_id(2)
    t0 = i * tq
    qpos = t0 + lax.broadcasted_iota(jnp.int32, (tq, 1), 0)
    cend = lax.broadcasted_iota(jnp.int32, (1, n_cmp_pad), 1) * CMP_STRIDE + (CMP_BLOCK - 1)
    valid_c = cend <= qpos
    psum = jnp.zeros((tq, n_cmp_pad), F32)
    o_cmp = []
    for g in range(NSA_GROUP):
        p = _cmp_probs(_dot_t(qc_ref[0, g], kc_ref[0, 0]), valid_c)
        psum = psum + p
        o_cmp.append(_dot(p.astype(BF16), vc_ref[0, 0]))
    imp = jnp.dot(psum, c2s_ref[...], preferred_element_type=F32, precision=lax.Precision.HIGHEST)
    sel = _select_blocks(imp, lax.shift_right_logical(qpos, 6), n_sel).astype(BF16)
    n_kt = mask_scr.shape[0]
    for jj in range(n_kt):
        mask_scr[jj] = _dot(sel, e_ref[:, jj * tq:(jj + 1) * tq])
    kiota = lax.broadcasted_iota(jnp.int32, (1, tq), 1)

    def sel_mask(j):
        return (mask_scr[j] > 0.5) & (j * tq + kiota <= qpos)

    def win_mask(j):
        rel = qpos - (j * tq + kiota)
        return (rel >= 0) & (rel < WINDOW)

    w_tiles = WINDOW // tq
    for g in range(NSA_GROUP):
        q = qr_ref[0, g]
        o_sel = _flash(q, ks_ref.at[0, 0], vs_ref.at[0, 0], 0, i + 1, sel_mask, tq, tq)
        o_win = _flash(q, kw_ref.at[0, 0], vw_ref.at[0, 0], jnp.maximum(i - w_tiles, 0), i + 1, win_mask, tq, tq)
        gt = g_ref[0, g]
        o = o_cmp[g] * gt[:, 0:1] + o_sel * gt[:, 1:2] + o_win * gt[:, 2:3]
        o_ref[0, :, g * NSA_HD:(g + 1) * NSA_HD] = o


def nsa_prompt_attn_v1(qc, qr, kcmp, vcmp, ksel, vsel, kwin, vwin, gates, tq):
    b, _, t, _ = qc.shape
    tq = min(tq, t)
    n_cmp_pad = kcmp.shape[2]
    n_sel = t // SEL_BLOCK
    n_sel_pad = -(-n_sel // LANE) * LANE
    c2s = jnp.asarray(_c2s(n_cmp_pad, n_sel_pad))
    e = jnp.asarray(_expand(n_sel_pad, t), dtype=BF16)
    kern = functools.partial(_nsa_prompt_kernel_v1, tq=tq, n_sel=n_sel, n_cmp_pad=n_cmp_pad)
    qspec = pl.BlockSpec((1, NSA_GROUP, tq, NSA_HD), lambda bb, k, i: (bb, k, i, 0))
    kvspec = lambda n: pl.BlockSpec((1, 1, n, NSA_HD), lambda bb, k, i: (bb, k, 0, 0))
    return pl.pallas_call(
        kern, grid=(b, NSA_KV_HEADS, t // tq),
        in_specs=[qspec, qspec, kvspec(n_cmp_pad), kvspec(n_cmp_pad), kvspec(t), kvspec(t), kvspec(t), kvspec(t),
                  pl.BlockSpec((1, NSA_GROUP, tq, 3), lambda bb, k, i: (bb, k, i, 0)),
                  pl.BlockSpec(c2s.shape, lambda bb, k, i: (0, 0)), pl.BlockSpec(e.shape, lambda bb, k, i: (0, 0))],
        out_specs=pl.BlockSpec((1, tq, NSA_GROUP * NSA_HD), lambda bb, k, i: (bb, i, k)),
        out_shape=jax.ShapeDtypeStruct((b, t, NSA_Q_W), F32),
        scratch_shapes=[pltpu.VMEM((t // tq, tq, tq), F32)],
        compiler_params=_cparams(("parallel", "parallel", "parallel"), 40), name="nsa_prompt_attn",
    )(qc, qr, kcmp, vcmp, ksel, vsel, kwin, vwin, gates, c2s, e)


def _pair_rms(x, w):
    lo = lax.broadcasted_iota(jnp.int32, x.shape, 1) < NSA_HD
    x2 = x * x
    s_lo = jnp.sum(jnp.where(lo, x2, 0.0), axis=-1, keepdims=True)
    s_hi = jnp.sum(jnp.where(lo, 0.0, x2), axis=-1, keepdims=True)
    ms = jnp.where(lo, s_lo, s_hi) * (1.0 / NSA_HD)
    return x * lax.rsqrt(ms + EPS) * w


def _pair_rope(x, cos_t, sin_t):
    lane = lax.broadcasted_iota(jnp.int32, x.shape, 1)
    first = (lane & (NSA_HD - 1)) < NSA_HD // 2
    partner = jnp.where(first, pltpu.roll(x, LANE - NSA_HD // 2, axis=1), pltpu.roll(x, NSA_HD // 2, axis=1))
    return x * cos_t + partner * sin_t


def _rope_tables(pos):
    half = NSA_HD // 2
    inv = jnp.power(ROPE_THETA, -jnp.arange(half, dtype=F32) / half)
    ang = pos.astype(F32)[:, None] * inv[None, :]
    c, s = jnp.cos(ang), jnp.sin(ang)
    return jnp.concatenate([c, c, c, c], axis=-1), jnp.concatenate([-s, s, -s, s], axis=-1)


def _nsa_prep_kernel_v2(q_ref, kv_ref, g_ref, cos_ref, sin_ref, qn_ref, kn_ref,
                        rows_ref, wrows_ref, qc_ref, qr_ref, gate_ref, kvb_ref=None, cmp_ref=None):
    cos_t, sin_t = cos_ref[...], sin_ref[...]
    lo = lax.broadcasted_iota(jnp.int32, cos_t.shape, 1) < NSA_HD
    for c in range(NSA_HEADS // 2):
        y = _pair_rms(q_ref[:, c * LANE:(c + 1) * LANE], qn_ref[...]) * (NSA_HD ** -0.5)
        in_low_lanes = (2 * c) // NSA_GROUP == 0
        for src, dst in ((y, qc_ref), (_pair_rope(y, cos_t, sin_t) * LOG2E, qr_ref)):
            swapped = pltpu.roll(src, NSA_HD, axis=1)
            if in_low_lanes:
                even, odd = jnp.where(lo, src, 0.0), jnp.where(lo, swapped, 0.0)
            else:
                even, odd = jnp.where(lo, 0.0, swapped), jnp.where(lo, 0.0, src)
            dst[:, (2 * c) * LANE:(2 * c + 1) * LANE] = even.astype(BF16)
            dst[:, (2 * c + 1) * LANE:(2 * c + 2) * LANE] = odd.astype(BF16)
    for br in range(3):
        k = _pair_rms(kv_ref[:, br * 2 * LANE:br * 2 * LANE + LANE], kn_ref[br:br + 1, :])
        if br > 0:
            k = _pair_rope(k, cos_t, sin_t)
        v = kv_ref[:, br * 2 * LANE + LANE:(br + 1) * 2 * LANE]
        if cmp_ref is None:
            dst, off = (rows_ref, br * 2 * LANE) if br < 2 else (wrows_ref, 0)
            dst[:, off:off + LANE] = k
            dst[:, off + LANE:off + 2 * LANE] = v
        else:
            dst, w0 = (rows_ref, 2 * br) if br < 2 else (wrows_ref, 0)
            for w, x in ((w0, k), (w0 + 1, v)):
                xt = x.T
                for h in range(NSA_KV_HEADS):
                    dst[0, w, h] = xt[h * NSA_HD:(h + 1) * NSA_HD]
            if br == 0:
                cmp_ref[:, :LANE] = k
                cmp_ref[:, LANE:] = v
            else:
                kvb_ref[:, (br - 1) * LANE:br * LANE] = k.astype(BF16)
                kvb_ref[:, 2 * br * LANE:(2 * br + 1) * LANE] = jnp.where(lo, v, 1.0).astype(BF16)
                kvb_ref[:, (2 * br + 1) * LANE:(2 * br + 2) * LANE] = jnp.where(lo, 1.0, v).astype(BF16)
    gate_ref[...] = _sigmoid(g_ref[...])


def nsa_prep_v2(q_pre, kv_pre, gate_pre, cos_t, sin_t, qn, kn, tm, dims_major):
    n_rows = q_pre.shape[0]
    tm = min(tm, n_rows)
    n_tab = cos_t.shape[0] // tm
    rows = lambda w: pl.BlockSpec((tm, w), lambda i: (i, 0))
    tab = pl.BlockSpec((tm, LANE), lambda i: (i % n_tab, 0))
    widths = (NSA_HEADS * LANE, NSA_HEADS * LANE, LANE)
    dtypes = (BF16, BF16, F32)
    out_specs = [rows(w) for w in widths]
    out_shape = [jax.ShapeDtypeStruct((n_rows, w), dt) for w, dt in zip(widths, dtypes)]
    if dims_major:
        n_seq, t = n_rows // cos_t.shape[0], cos_t.shape[0]
        kv_spec = lambda n: pl.BlockSpec((1, n, NSA_KV_HEADS, NSA_HD, tm), lambda i: (i // n_tab, 0, 0, 0, i % n_tab))
        kv_shape = lambda n: jax.ShapeDtypeStruct((n_seq, n, NSA_KV_HEADS, NSA_HD, t), F32)
        out_specs = [kv_spec(4), kv_spec(2)] + out_specs + [rows(NSA_KV_W), rows(2 * LANE)]
        out_shape = [kv_shape(4), kv_shape(2)] + out_shape + [jax.ShapeDtypeStruct((n_rows, NSA_KV_W), BF16),
                                                              jax.ShapeDtypeStruct((n_rows, 2 * LANE), F32)]
    else:
        out_specs = [rows(NSA_Q_W), rows(2 * LANE)] + out_specs
        out_shape = [jax.ShapeDtypeStruct((n_rows, NSA_Q_W), F32),
                     jax.ShapeDtypeStruct((n_rows, 2 * LANE), F32)] + out_shape
    return pl.pallas_call(
        _nsa_prep_kernel_v2, grid=(n_rows // tm,),
        in_specs=[rows(NSA_Q_W), rows(NSA_KV_W), rows(LANE), tab, tab,
                  pl.BlockSpec((1, LANE), lambda i: (0, 0)), pl.BlockSpec((3, LANE), lambda i: (0, 0))],
        out_specs=out_specs, out_shape=out_shape,
        compiler_params=_cparams(("parallel",), 40), name="nsa_prep",
    )(q_pre, kv_pre, gate_pre, cos_t, sin_t, jnp.tile(qn.reshape(1, NSA_HD), (1, 2)), jnp.tile(kn, (1, 2)))


def _select_blocks_t(imp_t, q_blk, n_sel):
    blk = lax.broadcasted_iota(jnp.int32, imp_t.shape, 0)
    visible = blk <= q_blk
    forced = jnp.where(blk == 0, 1.0, 0.0) + jnp.where(blk == q_blk, 1.0, 0.0) + jnp.where(blk == q_blk - 1, 1.0, 0.0)
    score = jnp.where(visible, imp_t + jnp.where(forced > 0.0, FORCED_BONUS, 0.0), NEG)
    rank = jnp.zeros(imp_t.shape, F32)
    for sp in range(n_sel):
        row = score[sp:sp + 1, :]
        tie = jnp.where(blk > sp, jnp.where(row == score, 1.0, 0.0), 0.0)
        rank = rank + jnp.where(row > score, 1.0, 0.0) + tie
    return jnp.where(visible, jnp.where(rank < SEL_TOPN, 1.0, 0.0), 0.0)


def _flash8_v2(qs, k_ref, v_refs, j_lo, j_hi, bias_fn, tq, tk):
    def body(j, carry):
        off = pl.multiple_of(j * tk, tk)
        k = k_ref[pl.ds(off, tk), :]
        vs = [v_ref[pl.ds(off, tk), :] for v_ref in v_refs]
        bias = bias_fn(j)
        out = []
        for h, (m, acc) in enumerate(carry):
            s = _dot_t(qs[h], k).astype(BF16) + bias[h // NSA_GROUP]
            m_new = jnp.maximum(m, jnp.max(s, axis=-1, keepdims=True).astype(F32))
            p = jnp.exp2(s - m_new.astype(BF16))
            acc = jnp.exp2(m - m_new) * acc + _dot(p, vs[h // NSA_GROUP])
            out.append((m_new, acc))
        return tuple(out)

    init = tuple((jnp.full((tq, 1), NEG, F32), jnp.zeros((tq, LANE), F32)) for _ in range(NSA_HEADS))
    outs = []
    for h, (_, acc) in enumerate(lax.fori_loop(j_lo, j_hi, body, init)):
        c = NSA_HD if h // NSA_GROUP == 0 else 0
        outs.append(acc / acc[:, c:c + 1])
    return outs


def _nsa_prompt_kernel_v2(qc_ref, qr_ref, kvc_ref, ks_ref, kw_ref, vs0_ref, vs1_ref, vw0_ref, vw1_ref, g_ref,
                          c2s_ref, e_ref, o_ref, *, tq, n_sel):
    i = pl.program_id(1)
    t0 = i * tq
    n_rows = NSA_HEADS * tq
    stack = lambda ref: jnp.concatenate([ref[:, h * LANE:(h + 1) * LANE] for h in range(NSA_HEADS)], axis=0)
    qpos = t0 + lax.broadcasted_iota(jnp.int32, (tq, 1), 0)
    n_cmp = kvc_ref.shape[1]
    kc = kvc_ref[0, :, :LANE].astype(BF16)
    vc = kvc_ref[0, :, LANE:].astype(BF16)
    cend = lax.broadcasted_iota(jnp.int32, (1, n_cmp), 1) * CMP_STRIDE + (CMP_BLOCK - 1)
    s = _dot_t(stack(qc_ref), kc).reshape(NSA_KV_HEADS, NSA_GROUP, tq, n_cmp)
    p = _cmp_probs(s, (cend <= qpos)[None, None])
    o_cmp = _dot(p.reshape(n_rows, n_cmp).astype(BF16), vc)
    psum = p[:, 0]
    for g in range(1, NSA_GROUP):
        psum = psum + p[:, g]
    imp = jnp.dot(psum.reshape(NSA_KV_HEADS * tq, n_cmp), c2s_ref[...], preferred_element_type=F32,
                  precision=lax.Precision.HIGHEST)
    n_blk_rows = -(-n_sel // 8) * 8
    col = lax.broadcasted_iota(jnp.int32, (1, NSA_KV_HEADS * tq), 1)
    q_blk = lax.shift_right_logical(t0 + (col & (tq - 1)), 6)
    sel_t = _select_blocks_t(imp.T[:n_blk_rows], q_blk, n_sel)
    sel_t = jnp.concatenate([sel_t, jnp.zeros((LANE - n_blk_rows, NSA_KV_HEADS * tq), F32)], axis=0)
    sel = sel_t.T.astype(BF16)
    kiota = lax.broadcasted_iota(jnp.int32, (1, tq), 1)

    def sel_bias(j):
        causal = (j * tq + kiota) <= qpos
        return [((jnp.where(causal, _dot(sel[k * tq:(k + 1) * tq], e_ref[j]), 0.0) - 1.0) * -NEG).astype(BF16)
                for k in range(NSA_KV_HEADS)]

    def win_bias(j):
        rel = qpos - (j * tq + kiota)
        return [jnp.where(jnp.where(rel >= 0, rel, WINDOW) < WINDOW, 0.0, NEG).astype(BF16)] * NSA_KV_HEADS

    qr = [qr_ref[:, h * LANE:(h + 1) * LANE] for h in range(NSA_HEADS)]
    o_sel = _flash8_v2(qr, ks_ref.at[0], [vs0_ref.at[0], vs1_ref.at[0]], 0, i + 1, sel_bias, tq, tq)
    o_win = _flash8_v2(qr, kw_ref.at[0], [vw0_ref.at[0], vw1_ref.at[0]], jnp.maximum(i - WINDOW // tq, 0), i + 1,
                       win_bias, tq, tq)
    lo = lax.broadcasted_iota(jnp.int32, (tq, LANE), 1) < NSA_HD
    for c in range(NSA_HEADS // 2):
        pair = []
        for h in (2 * c, 2 * c + 1):
            rs = slice(h * tq, (h + 1) * tq)
            gt = g_ref[:, 3 * h:3 * h + 3]
            pair.append(o_cmp[rs] * gt[:, 0:1] + o_sel[h] * gt[:, 1:2] + o_win[h] * gt[:, 2:3])
        even, odd = pair
        if (2 * c) // NSA_GROUP == 0:
            blk = jnp.where(lo, even, pltpu.roll(odd, NSA_HD, axis=1))
        else:
            blk = jnp.where(lo, pltpu.roll(even, NSA_HD, axis=1), odd)
        o_ref[:, c * LANE:(c + 1) * LANE] = blk


def nsa_prompt_attn_v2(qc, qr, kvc, kvb, gates, n_batch, tq):
    n_rows = qc.shape[0]
    t = n_rows // n_batch
    tq = min(tq, t)
    nt = t // tq
    n_cmp = kvc.shape[1]
    n_sel = t // SEL_BLOCK
    assert n_cmp % LANE == 0 and n_sel <= LANE and tq & (tq - 1) == 0 and WINDOW % tq == 0
    c2s = jnp.asarray(_c2s(n_cmp, LANE))
    e = jnp.asarray(_expand(LANE, t).reshape(LANE, nt, tq).transpose(1, 0, 2), dtype=BF16)
    kvb3 = kvb.reshape(n_batch, t, NSA_KV_W)
    rows = lambda w: pl.BlockSpec((tq, w), lambda b, i: (b * nt + i, 0))
    kv = lambda c: pl.BlockSpec((1, t, LANE), lambda b, i: (b, 0, c))
    return pl.pallas_call(
        functools.partial(_nsa_prompt_kernel_v2, tq=tq, n_sel=n_sel), grid=(n_batch, nt),
        in_specs=[rows(NSA_HEADS * LANE), rows(NSA_HEADS * LANE),
                  pl.BlockSpec((1, n_cmp, CMP_W), lambda b, i: (b, 0, 0)),
                  kv(0), kv(1), kv(2), kv(3), kv(4), kv(5), rows(LANE),
                  pl.BlockSpec(c2s.shape, lambda b, i: (0, 0)), pl.BlockSpec(e.shape, lambda b, i: (0, 0, 0))],
        out_specs=rows(NSA_Q_W),
        out_shape=jax.ShapeDtypeStruct((n_rows, NSA_Q_W), F32),
        compiler_params=_cparams(("parallel", "parallel"), 48), name="nsa_prompt_attn",
    )(qc, qr, kvc, kvb3, kvb3, kvb3, kvb3, kvb3, kvb3, gates, c2s, e)


LOG2E = 1.4426950408889634
KEY_TILE = 256


def _head_rows(yt, head_in_pair, kv_head):
    own = yt[head_in_pair * NSA_HD:(head_in_pair + 1) * NSA_HD]
    zero = jnp.zeros_like(own)
    return jnp.concatenate([own, zero] if kv_head == 0 else [zero, own], axis=0)


def _nsa_prep_kernel(q_ref, kv_ref, g_ref, cos_ref, sin_ref, qn_ref, kn_ref,
                     rows_ref, wrows_ref, qct_ref, qrt_ref, gt_ref, kb_ref=None, vt_ref=None, *, tm):
    cos_t, sin_t = cos_ref[...], sin_ref[...]
    for c in range(NSA_HEADS // 2):
        y = _pair_rms(q_ref[:, c * LANE:(c + 1) * LANE], qn_ref[...]) * (NSA_HD ** -0.5)
        yr = _pair_rope(y, cos_t, sin_t) * LOG2E
        for src, dst in ((y, qct_ref), (yr, qrt_ref)):
            st = src.T
            for e in range(2):
                h = 2 * c + e
                dst[h * LANE:(h + 1) * LANE, :] = _head_rows(st, e, h // NSA_GROUP).astype(BF16)
    for br in range(3):
        k = _pair_rms(kv_ref[:, br * 2 * LANE:br * 2 * LANE + LANE], kn_ref[br:br + 1, :])
        if br > 0:
            k = _pair_rope(k, cos_t, sin_t)
        v = kv_ref[:, br * 2 * LANE + LANE:(br + 1) * 2 * LANE]
        dst, off = (rows_ref, br * 2 * LANE) if br < 2 else (wrows_ref, 0)
        dst[:, off:off + LANE] = k
        dst[:, off + LANE:off + 2 * LANE] = v
        if br > 0 and kb_ref is not None:
            kb_ref[:, (br - 1) * LANE:br * LANE] = k.astype(BF16)
            vt = v.T.astype(BF16)
            for jt in range(tm // KEY_TILE):
                vt_ref[br - 1, jt] = vt[:, jt * KEY_TILE:(jt + 1) * KEY_TILE]
    gt_ref[...] = _sigmoid(g_ref[...]).T


def nsa_prep(q_pre, kv_pre, gate_pre, cos_t, sin_t, qn, kn, tm, attn_operands):
    n_rows = q_pre.shape[0]
    tm = min(tm, n_rows)
    n_tab = cos_t.shape[0] // tm
    rows = lambda w: pl.BlockSpec((tm, w), lambda i: (i, 0))
    cols = lambda r: pl.BlockSpec((r, tm), lambda i: (0, i))
    tab = pl.BlockSpec((tm, LANE), lambda i: (i % n_tab, 0))
    out_specs = [rows(NSA_Q_W), rows(2 * LANE), cols(NSA_HEADS * LANE), cols(NSA_HEADS * LANE), cols(LANE)]
    out_shape = [jax.ShapeDtypeStruct((n_rows, NSA_Q_W), F32), jax.ShapeDtypeStruct((n_rows, 2 * LANE), F32),
                 jax.ShapeDtypeStruct((NSA_HEADS * LANE, n_rows), BF16),
                 jax.ShapeDtypeStruct((NSA_HEADS * LANE, n_rows), BF16),
                 jax.ShapeDtypeStruct((LANE, n_rows), F32)]
    if attn_operands:
        assert tm % KEY_TILE == 0
        out_specs += [rows(2 * LANE), pl.BlockSpec((2, tm // KEY_TILE, LANE, KEY_TILE), lambda i: (0, i, 0, 0))]
        out_shape += [jax.ShapeDtypeStruct((n_rows, 2 * LANE), BF16),
                      jax.ShapeDtypeStruct((2, n_rows // KEY_TILE, LANE, KEY_TILE), BF16)]
    return pl.pallas_call(
        functools.partial(_nsa_prep_kernel, tm=tm), grid=(n_rows // tm,),
        in_specs=[rows(NSA_Q_W), rows(NSA_KV_W), rows(LANE), tab, tab,
                  pl.BlockSpec((1, LANE), lambda i: (0, 0)), pl.BlockSpec((3, LANE), lambda i: (0, 0))],
        out_specs=out_specs, out_shape=out_shape,
        compiler_params=_cparams(("parallel",), 40), name="nsa_prep",
    )(q_pre, kv_pre, gate_pre, cos_t, sin_t, jnp.tile(qn.reshape(1, NSA_HD), (1, 2)), jnp.tile(kn, (1, 2)))


def _flash_t(qts, k_ref, vt_ref, j_lo, j_hi, valid_fn, tq):
    def body(j, carry):
        k = k_ref[pl.ds(pl.multiple_of(j * KEY_TILE, KEY_TILE), KEY_TILE), :]
        vt = vt_ref[j]
        valid = valid_fn(j)
        out = []
        for h, (m, l, acc) in enumerate(carry):
            s = jnp.where(valid[h // NSA_GROUP], _dot(k, qts[h]), NEG)
            m_new = jnp.maximum(m, jnp.max(s, axis=0, keepdims=True))
            alpha = jnp.exp2(m - m_new)
            p = jnp.exp2(s - m_new)
            l = alpha * l + jnp.sum(p, axis=0, keepdims=True)
            acc = alpha * acc + _dot(vt, p.astype(BF16))
            out.append((m_new, l, acc))
        return tuple(out)

    init = tuple((jnp.full((1, tq), NEG, F32), jnp.zeros((1, tq), F32), jnp.zeros((LANE, tq), F32))
                 for _ in range(NSA_HEADS))
    return [acc / l for _, l, acc in lax.fori_loop(j_lo, j_hi, body, init)]


def _nsa_prompt_kernel(qct_ref, qrt_ref, kvc_ref, ks_ref, kw_ref, vst_ref, vwt_ref, gt_ref, c2st_ref, et_ref,
                       ot_ref, *, tq, n_sel):
    i = pl.program_id(1)
    t0 = i * tq
    qpos = t0 + lax.broadcasted_iota(jnp.int32, (1, tq), 1)
    n_cmp = kvc_ref.shape[1]
    kc = kvc_ref[0, :, :LANE].astype(BF16)
    vct = kvc_ref[0, :, LANE:].T.astype(BF16)
    cend = lax.broadcasted_iota(jnp.int32, (n_cmp, 1), 0) * CMP_STRIDE + (CMP_BLOCK - 1)
    valid_c = cend <= qpos
    psum = [jnp.zeros((n_cmp, tq), F32) for _ in range(NSA_KV_HEADS)]
    o_cmp = []
    for h in range(NSA_HEADS):
        s = jnp.where(valid_c, _dot(kc, qct_ref[h * LANE:(h + 1) * LANE, :]), NEG)
        p = jnp.where(valid_c, jnp.exp(s - jnp.max(s, axis=0, keepdims=True)), 0.0)
        p = p / jnp.maximum(jnp.sum(p, axis=0, keepdims=True), 1e-30)
        psum[h // NSA_GROUP] = psum[h // NSA_GROUP] + p
        o_cmp.append(_dot(vct, p.astype(BF16)))
    n_blk_rows = -(-n_sel // 8) * 8
    imp_t = jnp.concatenate(
        [jnp.dot(c2st_ref[...], ps, preferred_element_type=F32, precision=lax.Precision.HIGHEST)[:n_blk_rows]
         for ps in psum], axis=1)
    col = lax.broadcasted_iota(jnp.int32, (1, NSA_KV_HEADS * tq), 1)
    sel_t = _select_blocks_t(imp_t, lax.shift_right_logical(t0 + (col & (tq - 1)), 6), n_sel)
    sel_t = jnp.concatenate([sel_t, jnp.zeros((LANE - n_blk_rows, NSA_KV_HEADS * tq), F32)], axis=0).astype(BF16)
    krow = lax.broadcasted_iota(jnp.int32, (KEY_TILE, 1), 0)

    def sel_valid(j):
        causal = (j * KEY_TILE + krow) <= qpos
        return [jnp.where(causal, _dot(et_ref[j], sel_t[:, k * tq:(k + 1) * tq]), 0.0) > 0.5
                for k in range(NSA_KV_HEADS)]

    def win_valid(j):
        rel = qpos - (j * KEY_TILE + krow)
        return [jnp.where(rel >= 0, rel, WINDOW) < WINDOW] * NSA_KV_HEADS

    qrt = [qrt_ref[h * LANE:(h + 1) * LANE, :] for h in range(NSA_HEADS)]
    n_kt = (t0 + tq) // KEY_TILE
    o_sel = _flash_t(qrt, ks_ref.at[0], vst_ref.at[0], 0, n_kt, sel_valid, tq)
    o_win = _flash_t(qrt, kw_ref.at[0], vwt_ref.at[0], jnp.maximum((t0 - WINDOW) // KEY_TILE, 0), n_kt,
                     win_valid, tq)
    for h in range(NSA_HEADS):
        g = gt_ref[3 * h:3 * h + 3, :]
        o = o_cmp[h] * g[0:1] + o_sel[h] * g[1:2] + o_win[h] * g[2:3]
        k = h // NSA_GROUP
        ot_ref[h * NSA_HD:(h + 1) * NSA_HD, :] = o[k * NSA_HD:(k + 1) * NSA_HD]


def nsa_prompt_attn(qct, qrt, kvc, kb, vt, gates_t, n_batch, tq):
    n_rows = qct.shape[1]
    t = n_rows // n_batch
    tq = min(tq, t)
    nt = t // tq
    n_kt = t // KEY_TILE
    n_cmp = kvc.shape[1]
    n_sel = t // SEL_BLOCK
    assert n_cmp % LANE == 0 and n_sel <= LANE and tq & (tq - 1) == 0 and tq % KEY_TILE == 0
    c2st = jnp.asarray(_c2s(n_cmp, LANE).T)
    et = jnp.asarray(_expand(LANE, t).T.reshape(n_kt, KEY_TILE, LANE), dtype=BF16)
    kb3 = kb.reshape(n_batch, t, 2 * LANE)
    vt4 = vt.reshape(2, n_batch, n_kt, LANE, KEY_TILE)
    cols = lambda r: pl.BlockSpec((r, tq), lambda b, i: (0, b * nt + i))
    kspec = lambda c: pl.BlockSpec((1, t, LANE), lambda b, i: (b, 0, c))
    vspec = lambda c: pl.BlockSpec((None, 1, n_kt, LANE, KEY_TILE), lambda b, i: (c, b, 0, 0, 0))
    return pl.pallas_call(
        functools.partial(_nsa_prompt_kernel, tq=tq, n_sel=n_sel), grid=(n_batch, nt),
        in_specs=[cols(NSA_HEADS * LANE), cols(NSA_HEADS * LANE),
                  pl.BlockSpec((1, n_cmp, CMP_W), lambda b, i: (b, 0, 0)),
                  kspec(0), kspec(1), vspec(0), vspec(1), cols(LANE),
                  pl.BlockSpec(c2st.shape, lambda b, i: (0, 0)), pl.BlockSpec(et.shape, lambda b, i: (0, 0, 0))],
        out_specs=cols(NSA_Q_W),
        out_shape=jax.ShapeDtypeStruct((NSA_Q_W, n_rows), F32),
        compiler_params=_cparams(("parallel", "parallel"), 48), name="nsa_prompt_attn",
    )(qct, qrt, kvc, kb3, kb3, vt4, vt4, gates_t, c2st, et)


SROWS = NSA_HEADS * 8
HALF_W = 2 * NSA_KV_HEADS * NSA_HD


def _page_copies(cache_ref, pt_ref, targets, sem, b, slot, n_pages, page):
    return [pltpu.make_async_copy(cache_ref.at[pt_ref[b, pg], kind],
                                  buf.at[slot, :, :, pl.ds(pg * page, page)], sem.at[slot])
            for pg in range(n_pages) for kind, buf in targets]


def _stream_pages(cache_ref, pt_ref, targets, sem, n_pages, page):
    b = pl.program_id(0)
    nb = pl.num_programs(0)
    slot = lax.rem(b, 2)

    @pl.when(b == 0)
    def _():
        for c in _page_copies(cache_ref, pt_ref, targets, sem, 0, 0, n_pages, page):
            c.start()

    @pl.when(b + 1 < nb)
    def _():
        for c in _page_copies(cache_ref, pt_ref, targets, sem, b + 1, 1 - slot, n_pages, page):
            c.start()

    for c in _page_copies(cache_ref, pt_ref, targets, sem, b, slot, n_pages, page):
        c.wait()
    return slot


def _nsa_s1_kernel(pt_ref, cache_ref, q_ref, w_ref, pe_ref, c2s_ref, ocmp_ref, sel_ref, bufk, bufv, xk, xv, sem,
                   *, n_pages, page, ts, n_sel):
    past = n_pages * page
    slot = _stream_pages(cache_ref, pt_ref, [(0, bufk), (1, bufv)], sem, n_pages, page)
    n_chunk = past // CMP_STRIDE
    tch = min(past, 8 * LANE)
    for src, dst in ((bufk, xk), (bufv, xv)):
        for c in range(past // tch):
            dst[c * tch:(c + 1) * tch, :] = src[slot, :, :, c * tch:(c + 1) * tch].reshape(LANE, tch).T
    kv = _compress(lambda j: xk[pl.ds(j, n_chunk, stride=CMP_STRIDE), :],
                   lambda j: xv[pl.ds(j, n_chunk, stride=CMP_STRIDE), :], w_ref, pe_ref, n_chunk)
    kc = kv[:, :LANE].astype(BF16)
    vc = kv[:, LANE:].astype(BF16)
    tok = lax.broadcasted_iota(jnp.int32, (SROWS, 1), 0) & (ts - 1)
    qpos = past + tok
    cend = lax.broadcasted_iota(jnp.int32, (1, n_chunk), 1) * CMP_STRIDE + (CMP_BLOCK - 1)
    p = _cmp_probs(_dot_t(q_ref[0], kc), cend <= qpos)
    ocmp_ref[0] = _dot(p.astype(BF16), vc)
    psum = []
    for k in range(NSA_KV_HEADS):
        acc = p[k * NSA_GROUP * ts:k * NSA_GROUP * ts + ts]
        for g in range(1, NSA_GROUP):
            r = (k * NSA_GROUP + g) * ts
            acc = acc + p[r:r + ts]
        psum.append(acc)
    psum = jnp.concatenate(psum, axis=0)
    imp = jnp.dot(psum, c2s_ref[...], preferred_element_type=F32, precision=lax.Precision.HIGHEST)
    q_blk = lax.shift_right_logical(past + (lax.broadcasted_iota(jnp.int32, (NSA_KV_HEADS * ts, 1), 0) & (ts - 1)), 6)
    sel_ref[0] = _select_blocks(imp, q_blk, n_sel)


def _nsa_s2_kernel(pt_ref, cache_ref, q_ref, sel_ref, new_ref, win_ref, wnew_ref, ocmp_ref, g_ref, e_ref,
                   o_ref, bufk, bufv, sem, *, n_pages, page, ts, wb):
    past = n_pages * page
    slot = _stream_pages(cache_ref, pt_ref, [(2, bufk), (3, bufv)], sem, n_pages, page)
    q = q_ref[0]
    tok = lax.broadcasted_iota(jnp.int32, (SROWS, 1), 0) & (ts - 1)
    sel = sel_ref[0]
    sel_rows = jnp.concatenate([sel[k * ts:(k + 1) * ts] for k in range(NSA_KV_HEADS) for _ in range(NSA_GROUP)],
                               axis=0)
    n_new = new_ref.shape[1]
    jnew = lax.broadcasted_iota(jnp.int32, (1, n_new), 1)

    def attend(kt_past, vt_past, valid_past, k_new, v_new, valid_new):
        s_past = jnp.where(valid_past, _dot(q, kt_past), NEG)
        s_new = jnp.where(valid_new, _dot_t(q, k_new), NEG)
        m = jnp.maximum(jnp.max(s_past, axis=-1, keepdims=True), jnp.max(s_new, axis=-1, keepdims=True))
        p_past = jnp.exp2(s_past - m)
        p_new = jnp.exp2(s_new - m)
        l = jnp.sum(p_past, axis=-1, keepdims=True) + jnp.sum(p_new, axis=-1, keepdims=True)
        return (_dot_t(p_past.astype(BF16), vt_past) + _dot(p_new.astype(BF16), v_new)) / l

    mask_past = _dot(sel_rows.astype(BF16), e_ref[...]) > 0.5
    blk_new = past // SEL_BLOCK
    valid_new = (sel_rows[:, blk_new:blk_new + 1] > 0.5) & (jnew <= tok)
    o_sel = attend(bufk[slot].reshape(LANE, past).astype(BF16), bufv[slot].reshape(LANE, past).astype(BF16),
                   mask_past, new_ref[0, :, 0:LANE].astype(BF16), new_ref[0, :, LANE:HALF_W].astype(BF16), valid_new)
    u = lax.broadcasted_iota(jnp.int32, (1, wb), 1)
    rel = wb + tok - u
    valid_w = (rel >= 0) & (rel < WINDOW) & (past - wb + u >= 0)
    o_win = attend(win_ref[0, 0].reshape(LANE, wb).astype(BF16), win_ref[0, 1].reshape(LANE, wb).astype(BF16),
                   valid_w, wnew_ref[0, :, 0:LANE].astype(BF16), wnew_ref[0, :, LANE:HALF_W].astype(BF16),
                   jnew <= tok)
    gt = g_ref[0]
    o = ocmp_ref[0] * gt[:, 0:1] + o_sel * gt[:, 1:2] + o_win * gt[:, 2:3]
    half = SROWS // NSA_KV_HEADS
    for k in range(NSA_KV_HEADS):
        o_ref[0, k * half:(k + 1) * half, :] = o[k * half:(k + 1) * half, k * NSA_HD:(k + 1) * NSA_HD]


def nsa_sample_attn(page_table, cache, qc2, qr2, new_sel, win, new_win, gates, cmp_w, cmp_pe, ts):
    bs, n_pages = page_table.shape
    page = cache.shape[-1]
    past = n_pages * page
    wb = win.shape[-1]
    assert past % SEL_BLOCK == 0 and ts <= CMP_STRIDE and ts & (ts - 1) == 0 and SROWS == NSA_HEADS * ts
    n_chunk = past // CMP_STRIDE
    n_sel = past // SEL_BLOCK + 1
    n_sel_pad = -(-n_sel // LANE) * LANE
    c2s = jnp.asarray(_c2s(n_chunk, n_sel_pad))
    e = jnp.asarray(_expand(n_sel_pad, past), dtype=BF16)
    sems = pltpu.SemaphoreType.DMA((2,))
    pages = pltpu.VMEM((2, NSA_KV_HEADS, NSA_HD, past), F32)
    scratch1 = [pages, pages, pltpu.VMEM((past, LANE), F32), pltpu.VMEM((past, LANE), F32), sems]
    scratch2 = [pages, pages, sems]
    seq3 = lambda b, pt: (b, 0, 0)
    o_cmp, sel = pl.pallas_call(
        functools.partial(_nsa_s1_kernel, n_pages=n_pages, page=page, ts=ts, n_sel=n_sel),
        grid_spec=pltpu.PrefetchScalarGridSpec(
            num_scalar_prefetch=1, grid=(bs,),
            in_specs=[pl.BlockSpec(memory_space=pl.ANY),
                      pl.BlockSpec((1, SROWS, LANE), seq3),
                      pl.BlockSpec(cmp_w.shape, lambda b, pt: (0, 0, 0, 0, 0)),
                      pl.BlockSpec(cmp_pe.shape, lambda b, pt: (0, 0, 0, 0)),
                      pl.BlockSpec(c2s.shape, lambda b, pt: (0, 0))],
            out_specs=[pl.BlockSpec((1, SROWS, LANE), seq3),
                       pl.BlockSpec((1, NSA_KV_HEADS * ts, n_sel_pad), seq3)],
            scratch_shapes=scratch1),
        out_shape=[jax.ShapeDtypeStruct((bs, SROWS, LANE), F32),
                   jax.ShapeDtypeStruct((bs, NSA_KV_HEADS * ts, n_sel_pad), F32)],
        compiler_params=_cparams(("arbitrary",), 48), name="nsa_sample_cmp",
    )(page_table, cache, qc2, cmp_w, cmp_pe, c2s)
    n_new = new_sel.shape[1]
    return pl.pallas_call(
        functools.partial(_nsa_s2_kernel, n_pages=n_pages, page=page, ts=ts, wb=wb),
        grid_spec=pltpu.PrefetchScalarGridSpec(
            num_scalar_prefetch=1, grid=(bs,),
            in_specs=[pl.BlockSpec(memory_space=pl.ANY),
                      pl.BlockSpec((1, SROWS, LANE), seq3),
                      pl.BlockSpec((1, NSA_KV_HEADS * ts, n_sel_pad), seq3),
                      pl.BlockSpec((1, n_new, HALF_W), seq3),
                      pl.BlockSpec((1, 2, NSA_KV_HEADS, NSA_HD, wb), lambda b, pt: (b, 0, 0, 0, 0)),
                      pl.BlockSpec((1, n_new, HALF_W), seq3),
                      pl.BlockSpec((1, SROWS, LANE), seq3),
                      pl.BlockSpec((1, SROWS, 3), seq3),
                      pl.BlockSpec(e.shape, lambda b, pt: (0, 0))],
            out_specs=pl.BlockSpec((1, SROWS, NSA_HD), seq3),
            scratch_shapes=scratch2),
        out_shape=jax.ShapeDtypeStruct((bs, SROWS, NSA_HD), F32),
        compiler_params=_cparams(("arbitrary",), 56), name="nsa_sample_sel",
    )(page_table, cache, qr2, sel, new_sel, win, new_win, o_cmp, gates, e)


def _in_proj(x2d, pos_rows, p, tm, dims_major):
    q_pre, kv_pre, gate_pre, qkv_pre, z_pre, ba_pre = rms_matmul(x2d, p["attn_norm_w"], p["w_in_parts"], tm)
    cos_t, sin_t = _rope_tables(pos_rows)
    nsa = nsa_prep_v2(q_pre, kv_pre, gate_pre, cos_t, sin_t, p["nsa_q_norm_w"], p["nsa_k_norm_w"], tm, dims_major)
    return nsa, qkv_pre, z_pre, ba_pre


def _tokens_major(a):
    return a.transpose(0, 4, 1, 2, 3)


def _post_mixer(x2d, o_nsa, o_gdn, mem_kv, p, n_seq, tq):
    h = matmul_res([o_nsa, o_gdn], p["w_out"], x2d, 512)
    (qx,) = rms_matmul(h, p["xattn_norm_w"], [p["w_xq"]], 512)
    return matmul_res([xattn_core(qx, mem_kv, p["xq_norm_w"], n_seq, tq)], p["w_xo"], h, 512)


def _ffn(h, p, prev, n_seq, shift, tm):
    return conv_ffn(h, p["ffn_norm_w"], p["w_up"], p["ffn_conv_w"], p["ffn_conv_b"], p["w_down"], prev,
                    n_seq, shift, tm)


def _layer_prompt(x, mem, p):
    nb, t, d = x.shape
    x2d = x.reshape(nb * t, d)
    (rows_t, wrows_t, qc, qr, gates, kvb, cmp_rows), qkv_pre, z_pre, ba_pre = _in_proj(x2d, jnp.arange(t), p, 512, True)
    kvc = compress_prompt(cmp_rows.reshape(nb, t, CMP_W), p["cmp_bd"], p["cmp_pe2"])
    o_nsa = nsa_prompt_attn_v2(qc, qr, kvc, kvb, gates, nb, 256)
    o_gdn, gdn_buf, gdn_s = gdn(qkv_pre, z_pre, ba_pre, jnp.zeros((nb, GDN_CONV - 1, GDN_C3), F32),
                                jnp.zeros((nb, GDN_HEADS, GDN_HD, GDN_HD), F32), p["gdn_conv_w"], p["gdn_a_log"],
                                p["gdn_dt_bias"], p["gdn_norm_w"], nb, 256)
    mem_kv = mem_kv_proj(mem.reshape(-1, d), p["mem_norm_w"], p["w_xk"], p["w_xv"], p["xk_norm_w"])
    mem_kv = mem_kv.reshape(nb, mem.shape[1], 2 * XA_HEADS * XA_HD)
    h = _post_mixer(x2d, o_nsa, o_gdn, mem_kv, p, nb, 256)
    y, ffn_buf = _ffn(h, p, jnp.zeros((nb, FFN_CONV - 1, 2 * D_FF), F32), nb, 1, 512)
    keep = min(WINDOW, t)
    return (y.reshape(nb, t, d), _tokens_major(rows_t), _tokens_major(wrows_t[..., t - keep:]), gdn_s, gdn_buf,
            ffn_buf, mem_kv.reshape(nb, mem.shape[1], 2, XA_HEADS, XA_HD))


def _layer_sample(x, cache_kv, cache_win, state, gdn_cache, ffn_cache, mem_kv, page_table, p):
    bs, ts, d = x.shape
    n_pool, page = cache_kv.shape[:2]
    past = page_table.shape[1] * page
    x2d = x.reshape(bs * ts, d)
    pos_rows = jnp.tile(past + jnp.arange(ts), bs)
    (rows, wrows, qc, qr, gates), qkv_pre, z_pre, ba_pre = _in_proj(x2d, pos_rows, p, bs * ts, False)
    dims_major = lambda a: a.transpose(0, 2, 3, 4, 1)

    def stack_q(a):
        return a.reshape(bs, ts, NSA_HEADS, LANE).transpose(0, 2, 1, 3).reshape(bs, SROWS, LANE)

    def pad_new(a):
        return jnp.pad(a.reshape(bs, ts, HALF_W), ((0, 0), (0, LANE - ts), (0, 0)))

    wb = cache_win.shape[1]
    gates3 = gates[:, :NSA_GATE_W].reshape(bs, ts, NSA_HEADS, 3).transpose(0, 2, 1, 3).reshape(bs, SROWS, 3)
    win_t = dims_major(cache_win)
    o2 = nsa_sample_attn(page_table, dims_major(cache_kv), stack_q(qc), stack_q(qr),
                         pad_new(rows[:, HALF_W:]), win_t, pad_new(wrows), gates3, p["cmp_bd"], p["cmp_pe2"], ts)
    rows = rows.reshape(bs, ts, 4, NSA_KV_HEADS, NSA_HD)
    wrows = wrows.reshape(bs, ts, 2, NSA_KV_HEADS, NSA_HD)
    o_nsa = o2.reshape(bs, NSA_HEADS, ts, NSA_HD).transpose(0, 2, 1, 3).reshape(bs * ts, NSA_Q_W)
    o_gdn, gdn_buf, gdn_s = gdn(qkv_pre, z_pre, ba_pre, gdn_cache, state, p["gdn_conv_w"], p["gdn_a_log"],
                                p["gdn_dt_bias"], p["gdn_norm_w"], bs, ts)
    h = _post_mixer(x2d, o_nsa, o_gdn, mem_kv.reshape(bs, mem_kv.shape[1], -1), p, bs, ts)
    h_tm = h.reshape(bs, ts, d).transpose(1, 0, 2).reshape(ts * bs, d)
    prev = ffn_cache.transpose(1, 0, 2).reshape(1, (FFN_CONV - 1) * bs, 2 * D_FF)
    y, ffn_buf = _ffn(h_tm, p, prev, 1, bs, ts * bs)
    y = y.reshape(ts, bs, d).transpose(1, 0, 2)
    ffn_buf = ffn_buf.reshape(FFN_CONV - 1, bs, 2 * D_FF).transpose(1, 0, 2)
    wall_t = jnp.concatenate([win_t, dims_major(wrows)], axis=-1)
    keep = min(WINDOW, wb + ts)
    return y, rows, _tokens_major(wall_t[..., wb + ts - keep:]), gdn_s, gdn_buf, ffn_buf


def _prep_params(l, attn_norm_w, w_in, nsa_q_norm_w, nsa_k_norm_w, cmp_pe, cmp_w, gdn_conv_w, gdn_a_log,
                 gdn_dt_bias, gdn_norm_w, w_out, mem_norm_w, w_xk, w_xv, xk_norm_w, xattn_norm_w, w_xq,
                 xq_norm_w, w_xo, ffn_norm_w, w_up, ffn_conv_w, ffn_conv_b, w_down):
    wi = w_in[l]
    cuts = np.cumsum([0, NSA_Q_W, NSA_KV_W, NSA_GATE_W, 3 * GDN_W, GDN_W, 2 * GDN_HEADS])
    parts = []
    for a, b in zip(cuts[:-1], cuts[1:]):
        w = wi[:, a:b]
        padn = -(-(b - a) // LANE) * LANE - (b - a)
        parts.append(jnp.pad(w, ((0, 0), (0, padn))).astype(BF16))
    cmp_bd, cmp_pe2 = _cmp_weights(cmp_pe[l], cmp_w[l])
    bf = lambda w: w[l].astype(BF16)
    return dict(attn_norm_w=attn_norm_w[l], w_in_parts=parts, nsa_q_norm_w=nsa_q_norm_w[l],
                nsa_k_norm_w=nsa_k_norm_w[l], cmp_bd=cmp_bd, cmp_pe2=cmp_pe2, gdn_conv_w=gdn_conv_w[l],
                gdn_a_log=gdn_a_log[l], gdn_dt_bias=gdn_dt_bias[l], gdn_norm_w=gdn_norm_w[l], w_out=bf(w_out),
                mem_norm_w=mem_norm_w[l], w_xk=bf(w_xk), w_xv=bf(w_xv), xk_norm_w=xk_norm_w[l],
                xattn_norm_w=xattn_norm_w[l], w_xq=bf(w_xq), xq_norm_w=xq_norm_w[l], w_xo=bf(w_xo),
                ffn_norm_w=ffn_norm_w[l], w_up=bf(w_up), ffn_conv_w=ffn_conv_w[l], ffn_conv_b=ffn_conv_b[l],
                w_down=bf(w_down))


def kernel(x_prompt, x_sample, mem_prompt, cache_nsa_kv, cache_nsa_win, state_gdn, cache_gdn_conv, cache_ffn_conv,
           cache_mem_kv, page_table, attn_norm_w, w_in, nsa_q_norm_w, nsa_k_norm_w, cmp_pe, cmp_w, gdn_conv_w,
           gdn_a_log, gdn_dt_bias, gdn_norm_w, w_out, mem_norm_w, w_xk, w_xv, xk_norm_w, xattn_norm_w, w_xq,
           xq_norm_w, w_xo, ffn_norm_w, w_up, ffn_conv_w, ffn_conv_b, w_down):
    weights = (attn_norm_w, w_in, nsa_q_norm_w, nsa_k_norm_w, cmp_pe, cmp_w, gdn_conv_w, gdn_a_log, gdn_dt_bias,
               gdn_norm_w, w_out, mem_norm_w, w_xk, w_xv, xk_norm_w, xattn_norm_w, w_xq, xq_norm_w, w_xo,
               ffn_norm_w, w_up, ffn_conv_w, ffn_conv_b, w_down)
    depth = cache_nsa_kv.shape[0]
    hp, hs = x_prompt, x_sample
    outs_p, outs_s = [], []
    for l in range(depth):
        p = _prep_params(l, *weights)
        res_p = _layer_prompt(hp, mem_prompt, p)
        hp = res_p[0]
        outs_p.append(res_p[1:])
        res_s = _layer_sample(hs, cache_nsa_kv[l], cache_nsa_win[l], state_gdn[l], cache_gdn_conv[l],
                              cache_ffn_conv[l], cache_mem_kv[l], page_table, p)
        hs = res_s[0]
        outs_s.append(res_s[1:])
    stack = lambda outs, i: jnp.stack([o[i] for o in outs])
    return ((hp, hs) + tuple(stack(outs_p, i) for i in range(6)) + tuple(stack(outs_s, i) for i in range(5)))
```

```python
import functools
import math

import numpy as np
import jax
import jax.numpy as jnp
from jax import lax
from jax.experimental import pallas as pl
from jax.experimental.pallas import tpu as pltpu

F32 = jnp.float32
BF16 = jnp.bfloat16

D_MODEL = 1024
NSA_HEADS = 8
NSA_KV_HEADS = 2
NSA_GROUP = NSA_HEADS // NSA_KV_HEADS
NSA_HD = 64
CMP_BLOCK = 32
CMP_STRIDE = 16
SEL_BLOCK = 64
SEL_TOPN = 8
WINDOW = 512
FORCED_BONUS = 1e4
GDN_HEADS = 4
GDN_HD = 128
GDN_CONV = 4
GDN_CHUNK = 64
GDN_W = GDN_HEADS * GDN_HD
XA_HEADS = 4
XA_HD = 256
D_FF = 2816
FFN_CONV = 3
ROPE_THETA = 10000.0
EPS = 1e-6
NEG = -1e30

NSA_Q_W = NSA_HEADS * NSA_HD
NSA_KV_W = 3 * 2 * NSA_KV_HEADS * NSA_HD
NSA_GATE_W = 3 * NSA_HEADS
LANE = 128
MIB = 1 << 20


def _cparams(sem, vmem_mib):
    return pltpu.CompilerParams(dimension_semantics=sem, vmem_limit_bytes=vmem_mib * MIB)


def _sigmoid(x):
    return 1.0 / (1.0 + jnp.exp(-x))


def _dot(a, b):
    return jnp.dot(a, b, preferred_element_type=F32)


def _dot_t(a, b):
    return lax.dot_general(a, b, (((1,), (1,)), ((), ())), preferred_element_type=F32)


def _split_bf16(a):
    hi = a.astype(BF16)
    lo = (a - hi.astype(F32)).astype(BF16)
    return hi, lo


def _dot3(a, b):
    ah, al = _split_bf16(a)
    bh, bl = _split_bf16(b)
    return _dot(ah, bh) + (_dot(ah, bl) + _dot(al, bh))


def _col_chunk(n):
    for c in (512, 256, 128):
        if n % c == 0:
            return c
    return n


def _rms_mm_kernel(x_ref, g_ref, *refs, n_out):
    w_refs, o_refs = refs[:n_out], refs[n_out:]
    x = x_ref[...]
    xn = (x * lax.rsqrt(jnp.mean(x * x, axis=-1, keepdims=True) + EPS) * g_ref[...]).astype(BF16)
    for w_ref, o_ref in zip(w_refs, o_refs):
        n = w_ref.shape[1]
        ch = _col_chunk(n)
        for c in range(0, n, ch):
            o_ref[:, c:c + ch] = _dot(xn, w_ref[:, c:c + ch])


def rms_matmul(x, g, ws, tm):
    n_rows, d = x.shape
    tm = min(tm, n_rows)
    in_specs = [pl.BlockSpec((tm, d), lambda i: (i, 0)), pl.BlockSpec((1, d), lambda i: (0, 0))]
    in_specs += [pl.BlockSpec(w.shape, lambda i: (0, 0)) for w in ws]
    out_specs = [pl.BlockSpec((tm, w.shape[1]), lambda i: (i, 0)) for w in ws]
    out_shape = [jax.ShapeDtypeStruct((n_rows, w.shape[1]), F32) for w in ws]
    return pl.pallas_call(
        functools.partial(_rms_mm_kernel, n_out=len(ws)),
        grid=(n_rows // tm,), in_specs=in_specs, out_specs=out_specs, out_shape=out_shape,
        compiler_params=_cparams(("parallel",), 56), name="rms_matmul",
    )(x, g.reshape(1, d), *ws)


def _mm_res_kernel(*refs, n_in):
    a_refs, (w_ref, r_ref, o_ref) = refs[:n_in], refs[n_in:]
    a = [a_ref[...].astype(BF16) for a_ref in a_refs]
    n = w_ref.shape[1]
    ch = _col_chunk(n)
    for c in range(0, n, ch):
        acc = r_ref[:, c:c + ch]
        k0 = 0
        for x in a:
            acc = acc + _dot(x, w_ref[k0:k0 + x.shape[1], c:c + ch])
            k0 += x.shape[1]
        o_ref[:, c:c + ch] = acc


def matmul_res(a_list, w, res, tm):
    n_rows = res.shape[0]
    k, n = w.shape
    assert sum(a.shape[1] for a in a_list) == k
    tm = min(tm, n_rows)
    return pl.pallas_call(
        functools.partial(_mm_res_kernel, n_in=len(a_list)), grid=(n_rows // tm,),
        in_specs=[pl.BlockSpec((tm, a.shape[1]), lambda i: (i, 0)) for a in a_list]
        + [pl.BlockSpec((k, n), lambda i: (0, 0)), pl.BlockSpec((tm, n), lambda i: (i, 0))],
        out_specs=pl.BlockSpec((tm, n), lambda i: (i, 0)),
        out_shape=jax.ShapeDtypeStruct((n_rows, n), F32),
        compiler_params=_cparams(("parallel",), 40), name="matmul_res",
    )(*a_list, w, res)


def _memkv_kernel(x_ref, g_ref, wk_ref, wv_ref, kn_ref, o_ref):
    x = x_ref[...]
    xn = (x * lax.rsqrt(jnp.mean(x * x, axis=-1, keepdims=True) + EPS) * g_ref[...]).astype(BF16)
    xa_w = XA_HEADS * XA_HD
    for h in range(XA_HEADS):
        sl = slice(h * XA_HD, (h + 1) * XA_HD)
        k = _dot(xn, wk_ref[:, sl])
        k = k * lax.rsqrt(jnp.mean(k * k, axis=-1, keepdims=True) + EPS) * kn_ref[...]
        o_ref[:, sl] = k
        o_ref[:, xa_w + h * XA_HD: xa_w + (h + 1) * XA_HD] = _dot(xn, wv_ref[:, sl])


def mem_kv_proj(mem, g, wk, wv, kn, tm=256):
    n_rows, d = mem.shape
    xa_w = XA_HEADS * XA_HD
    return pl.pallas_call(
        _memkv_kernel, grid=(n_rows // tm,),
        in_specs=[pl.BlockSpec((tm, d), lambda i: (i, 0)), pl.BlockSpec((1, d), lambda i: (0, 0)),
                  pl.BlockSpec((d, xa_w), lambda i: (0, 0)), pl.BlockSpec((d, xa_w), lambda i: (0, 0)),
                  pl.BlockSpec((1, XA_HD), lambda i: (0, 0))],
        out_specs=pl.BlockSpec((tm, 2 * xa_w), lambda i: (i, 0)),
        out_shape=jax.ShapeDtypeStruct((n_rows, 2 * xa_w), F32),
        compiler_params=_cparams(("parallel",), 40), name="mem_kv_proj",
    )(mem, g.reshape(1, d), wk, wv, kn.reshape(1, XA_HD))


def _xattn_kernel(q_ref, kv_ref, qn_ref, o_ref):
    xa_w = XA_HEADS * XA_HD
    for h in range(XA_HEADS):
        sl = slice(h * XA_HD, (h + 1) * XA_HD)
        q = q_ref[:, sl]
        q = q * lax.rsqrt(jnp.mean(q * q, axis=-1, keepdims=True) + EPS) * qn_ref[...]
        qb = (q * (XA_HD ** -0.5)).astype(BF16)
        k = kv_ref[0, :, sl].astype(BF16)
        v = kv_ref[0, :, xa_w + h * XA_HD: xa_w + (h + 1) * XA_HD].astype(BF16)
        s = _dot_t(qb, k)
        p = jnp.exp(s - jnp.max(s, axis=-1, keepdims=True))
        o = _dot(p.astype(BF16), v) / jnp.sum(p, axis=-1, keepdims=True)
        o_ref[:, sl] = o


def xattn_core(q_pre, kv, qn, n_batch, tq):
    n_rows, xa_w = q_pre.shape
    t = n_rows // n_batch
    tq = min(tq, t)
    nt = t // tq
    m = kv.shape[1]
    return pl.pallas_call(
        _xattn_kernel, grid=(n_batch, nt),
        in_specs=[pl.BlockSpec((tq, xa_w), lambda b, i: (b * nt + i, 0)),
                  pl.BlockSpec((1, m, 2 * xa_w), lambda b, i: (b, 0, 0)),
                  pl.BlockSpec((1, XA_HD), lambda b, i: (0, 0))],
        out_specs=pl.BlockSpec((tq, xa_w), lambda b, i: (b * nt + i, 0)),
        out_shape=jax.ShapeDtypeStruct((n_rows, xa_w), F32),
        compiler_params=_cparams(("parallel", "parallel"), 40), name="xattn_core",
    )(q_pre, kv, qn.reshape(1, XA_HD))


FFN_ACT_CHUNK = 256


def _ffn_kernel(h_ref, g_ref, wup_ref, cw_ref, cb_ref, wdn_ref, prev_ref, o_ref, buf_ref, xs_scr,
                *, tm, shift, base):
    t = pl.program_id(1)
    p0 = base - 2 * shift

    @pl.when(t == 0)
    def _():
        xs_scr[p0:base, :] = prev_ref[0]

    x = h_ref[...]
    xn = (x * lax.rsqrt(jnp.mean(x * x, axis=-1, keepdims=True) + EPS) * g_ref[...]).astype(BF16)
    acc = jnp.zeros((tm, D_MODEL), F32)
    for j in range(D_FF // FFN_ACT_CHUNK):
        halves = []
        for c0 in (j * FFN_ACT_CHUNK, D_FF + j * FFN_ACT_CHUNK):
            sl = slice(c0, c0 + FFN_ACT_CHUNK)
            xs_scr[base:base + tm, sl] = _dot(xn, wup_ref[:, sl])
            y = (cw_ref[0:1, sl] * xs_scr[p0:p0 + tm, sl]
                 + cw_ref[1:2, sl] * xs_scr[p0 + shift:p0 + shift + tm, sl]
                 + cw_ref[2:3, sl] * xs_scr[base:base + tm, sl])
            halves.append(y + cb_ref[:, sl])
        a, u = halves
        act = (a * _sigmoid(a) * u).astype(BF16)
        acc = acc + _dot(act, wdn_ref[j * FFN_ACT_CHUNK:(j + 1) * FFN_ACT_CHUNK, :])
    o_ref[...] = x + acc
    last = xs_scr[base + tm - 2 * shift: base + tm, :]
    buf_ref[0] = last
    xs_scr[p0:base, :] = last


def conv_ffn(h, g, wup, cw, cb, wdn, prev, n_seq, shift, tm):
    n_rows, d = h.shape
    t_rows = n_rows // n_seq
    tm = min(tm, t_rows)
    nt = t_rows // tm
    base = -(-2 * shift // 8) * 8
    kern = functools.partial(_ffn_kernel, tm=tm, shift=shift, base=base)
    const = lambda b, i: (0, 0)
    return pl.pallas_call(
        kern, grid=(n_seq, nt),
        in_specs=[pl.BlockSpec((tm, d), lambda b, i: (b * nt + i, 0)),
                  pl.BlockSpec((1, d), const),
                  pl.BlockSpec((d, 2 * D_FF), const, pipeline_mode=pl.Buffered(1)),
                  pl.BlockSpec((FFN_CONV, 2 * D_FF), const),
                  pl.BlockSpec((1, 2 * D_FF), const),
                  pl.BlockSpec((D_FF, d), const, pipeline_mode=pl.Buffered(1)),
                  pl.BlockSpec((1, 2 * shift, 2 * D_FF), lambda b, i: (b, 0, 0))],
        out_specs=[pl.BlockSpec((tm, d), lambda b, i: (b * nt + i, 0)),
                   pl.BlockSpec((1, 2 * shift, 2 * D_FF), lambda b, i: (b, 0, 0))],
        out_shape=[jax.ShapeDtypeStruct((n_rows, d), F32),
                   jax.ShapeDtypeStruct((n_seq, 2 * shift, 2 * D_FF), F32)],
        scratch_shapes=[pltpu.VMEM((base + tm, 2 * D_FF), F32)],
        compiler_params=_cparams(("arbitrary", "arbitrary"), 56), name="conv_ffn",
    )(h, g.reshape(1, d), wup, cw, cb.reshape(1, 2 * D_FF), wdn, prev)


GDN_STACK = GDN_HEADS * GDN_CHUNK
GDN_C3 = 3 * GDN_W
GDN_PREV0 = 8 - (GDN_CONV - 1)


def _gdn_masks():
    r = np.arange(GDN_STACK)[:, None]
    c = np.arange(GDN_STACK)[None, :]
    same = lambda n: (r // n) == (c // n)
    m = [same(2) & (c < r)]
    for n in (4, 8, 16, 32, 64):
        m.append(same(n) & ~same(n // 2) & (c < r))
    m.append(same(GDN_CHUNK) & (c <= r))
    m.append(same(GDN_CHUNK) & (c < r))
    m.append(r == c)
    return np.stack(m).astype(np.float32)


def _unit_lower_inverse(a, m_ref):
    t = m_ref[8] - a * m_ref[0]
    for lvl in range(1, 6):
        off = a * m_ref[lvl]
        t = t - _dot3(t, _dot3(off, t))
    return t


def _gdn_kernel_v1(qkv_ref, z_ref, ba_ref, prev_ref, s0_ref, cw_ref, par_ref, nw_ref, m_ref,
                   o_ref, buf_ref, s_ref, xs_scr, s_scr, *, tt):
    t = pl.program_id(1)
    nt = pl.num_programs(1)
    row0 = 8

    @pl.when(t == 0)
    def _():
        xs_scr[GDN_PREV0:row0, :] = prev_ref[0]
        s_scr[...] = s0_ref[0]

    xs_scr[row0:row0 + tt, :] = qkv_ref[...]
    causal = m_ref[6]
    neg_ea = -jnp.exp(par_ref[0:1, :])
    dtb = par_ref[1:2, :]
    rv = min(GDN_CHUNK, tt)
    pad = GDN_CHUNK - rv

    def padrows(a):
        if pad == 0:
            return a
        return jnp.concatenate([a, jnp.zeros((pad, a.shape[1]), a.dtype)], axis=0)

    for n in range(-(-tt // GDN_CHUNK)):
        r0 = n * GDN_CHUNK
        y = cw_ref[0:1, :] * xs_scr[GDN_PREV0 + r0:GDN_PREV0 + r0 + rv, :]
        for i in range(1, GDN_CONV):
            y = y + cw_ref[i:i + 1, :] * xs_scr[GDN_PREV0 + i + r0:GDN_PREV0 + i + r0 + rv, :]
        y = y * _sigmoid(y)
        qs, ks, vs, betas, gs = [], [], [], [], []
        for h in range(GDN_HEADS):
            q = y[:, h * GDN_HD:(h + 1) * GDN_HD]
            k = y[:, GDN_W + h * GDN_HD:GDN_W + (h + 1) * GDN_HD]
            v = y[:, 2 * GDN_W + h * GDN_HD:2 * GDN_W + (h + 1) * GDN_HD]
            q = q * lax.rsqrt(jnp.sum(q * q, axis=-1, keepdims=True) + EPS) * (GDN_HD ** -0.5)
            k = k * lax.rsqrt(jnp.sum(k * k, axis=-1, keepdims=True) + EPS)
            bcol = ba_ref[r0:r0 + rv, h:h + 1]
            acol = ba_ref[r0:r0 + rv, GDN_HEADS + h:GDN_HEADS + h + 1] + dtb[:, h:h + 1]
            softplus = jnp.maximum(acol, 0.0) + jnp.log1p(jnp.exp(-jnp.abs(acol)))
            qs.append(padrows(q)); ks.append(padrows(k)); vs.append(padrows(v))
            betas.append(padrows(_sigmoid(bcol)))
            gs.append(padrows(neg_ea[:, h:h + 1] * softplus))
        qm = jnp.concatenate(qs, axis=0)
        km = jnp.concatenate(ks, axis=0)
        vm = jnp.concatenate(vs, axis=0)
        beta = jnp.concatenate(betas, axis=0)
        g = jnp.concatenate(gs, axis=0)
        gc = jnp.dot(causal, jnp.broadcast_to(g, (GDN_STACK, LANE)), preferred_element_type=F32,
                     precision=lax.Precision.HIGHEST)[:, 0:1]
        gc_row = jnp.sum(m_ref[8] * gc, axis=0, keepdims=True)
        diff = gc - gc_row
        decay = jnp.exp(jnp.where(causal > 0, diff, 0.0)) * causal
        kb = km * beta
        kmb = km.astype(BF16)
        a_mat = _dot_t(kb.astype(BF16), kmb) * decay * m_ref[7]
        qk = _dot_t(qm.astype(BF16), kmb) * decay
        tinv = _unit_lower_inverse(a_mat, m_ref)
        egc = jnp.exp(gc)
        sol = _dot3(tinv, jnp.concatenate([vm * beta, kb * egc], axis=1))
        u_all, w_all = sol[:, :GDN_HD], sol[:, GDN_HD:]
        qd = qm * egc
        v_news, o_inter, kds, gls = [], [], [], []
        for h in range(GDN_HEADS):
            rs = slice(h * GDN_CHUNK, (h + 1) * GDN_CHUNK)
            sb = s_scr[h].astype(BF16)
            v_new = u_all[rs] - _dot(w_all[rs].astype(BF16), sb)
            o_inter.append(_dot(qd[rs].astype(BF16), sb))
            gl = gc[h * GDN_CHUNK + GDN_CHUNK - 1:(h + 1) * GDN_CHUNK, :]
            kds.append(km[rs] * jnp.exp(gl - gc[rs]))
            v_news.append(v_new)
            gls.append(gl)
        v_stack = jnp.concatenate(v_news, axis=0).astype(BF16)
        o_intra = _dot(qk.astype(BF16), v_stack)
        for h in range(GDN_HEADS):
            rs = slice(h * GDN_CHUNK, (h + 1) * GDN_CHUNK)
            s_scr[h] = s_scr[h] * jnp.exp(gls[h]) + _dot(kds[h].T.astype(BF16), v_stack[rs])
            o = (o_inter[h] + o_intra[rs])[:rv]
            on = o * lax.rsqrt(jnp.mean(o * o, axis=-1, keepdims=True) + EPS) * nw_ref[...]
            zz = z_ref[r0:r0 + rv, h * GDN_HD:(h + 1) * GDN_HD]
            o_ref[r0:r0 + rv, h * GDN_HD:(h + 1) * GDN_HD] = on * (zz * _sigmoid(zz))

    last = xs_scr[row0 + tt - (GDN_CONV - 1):row0 + tt, :]
    buf_ref[0] = last
    xs_scr[GDN_PREV0:row0, :] = last

    @pl.when(t == nt - 1)
    def _():
        s_ref[0] = s_scr[...]


def _gdn_kernel(qkv_ref, z_ref, ba_ref, prev_ref, s0_ref, cw_ref, par_ref, nw_ref, m_ref,
                o_ref, buf_ref, s_ref, xs_scr, s_scr, *, tt):
    t = pl.program_id(1)
    nt = pl.num_programs(1)
    row0 = 8

    @pl.when(t == 0)
    def _():
        xs_scr[GDN_PREV0:row0, :] = prev_ref[0]
        s_scr[...] = s0_ref[0]

    xs_scr[row0:row0 + tt, :] = qkv_ref[...]
    causal, strict, eye = m_ref[6], m_ref[7], m_ref[8]
    neg_ea = -jnp.exp(par_ref[0:1, :])
    dtb = par_ref[1:2, :]
    rv = min(GDN_CHUNK, tt)
    pad = GDN_CHUNK - rv
    n_ch = -(-tt // GDN_CHUNK)

    def padrows(a):
        if pad == 0:
            return a
        return jnp.concatenate([a, jnp.zeros((pad, a.shape[1]), a.dtype)], axis=0)

    qms, kms, vms, betas, gcols = [], [], [], [], []
    for n in range(n_ch):
        r0 = n * GDN_CHUNK
        y = cw_ref[0:1, :] * xs_scr[GDN_PREV0 + r0:GDN_PREV0 + r0 + rv, :]
        for i in range(1, GDN_CONV):
            y = y + cw_ref[i:i + 1, :] * xs_scr[GDN_PREV0 + i + r0:GDN_PREV0 + i + r0 + rv, :]
        y = y * _sigmoid(y)
        qs, ks, vs, bs, gs = [], [], [], [], []
        for h in range(GDN_HEADS):
            q = y[:, h * GDN_HD:(h + 1) * GDN_HD]
            k = y[:, GDN_W + h * GDN_HD:GDN_W + (h + 1) * GDN_HD]
            v = y[:, 2 * GDN_W + h * GDN_HD:2 * GDN_W + (h + 1) * GDN_HD]
            q = q * lax.rsqrt(jnp.sum(q * q, axis=-1, keepdims=True) + EPS) * (GDN_HD ** -0.5)
            k = k * lax.rsqrt(jnp.sum(k * k, axis=-1, keepdims=True) + EPS)
            bcol = ba_ref[r0:r0 + rv, h:h + 1]
            acol = ba_ref[r0:r0 + rv, GDN_HEADS + h:GDN_HEADS + h + 1] + dtb[:, h:h + 1]
            softplus = jnp.maximum(acol, 0.0) + jnp.log1p(jnp.exp(-jnp.abs(acol)))
            qs.append(padrows(q)); ks.append(padrows(k)); vs.append(padrows(v))
            bs.append(padrows(_sigmoid(bcol)))
            gs.append(padrows(neg_ea[:, h:h + 1] * softplus))
        qms.append(jnp.concatenate(qs, axis=0))
        kms.append(jnp.concatenate(ks, axis=0))
        vms.append(jnp.concatenate(vs, axis=0))
        betas.append(jnp.concatenate(bs, axis=0))
        gcols.append(jnp.concatenate(gs, axis=0))
    gmat = jnp.concatenate(gcols + [jnp.zeros((GDN_STACK, LANE - n_ch), F32)], axis=1)
    gc_all = jnp.dot(causal, gmat, preferred_element_type=F32, precision=lax.Precision.HIGHEST)

    a_mats, qks, rhss, gcs, tinv = [], [], [], [], []
    for n in range(n_ch):
        gc = gc_all[:, n:n + 1]
        gc_row = jnp.sum(eye * gc, axis=0, keepdims=True)
        decay = jnp.exp(jnp.where(causal > 0, gc - gc_row, 0.0)) * causal
        kb = kms[n] * betas[n]
        kmb = kms[n].astype(BF16)
        a_mat = _dot_t(kb.astype(BF16), kmb) * decay * strict
        qks.append((_dot_t(qms[n].astype(BF16), kmb) * decay).astype(BF16))
        rhss.append(jnp.concatenate([vms[n] * betas[n], kb * jnp.exp(gc)], axis=1).astype(BF16))
        a_mats.append(a_mat)
        gcs.append(gc)
        tinv.append(eye - a_mat * m_ref[0])
    for lvl in range(1, 6):
        tb = [x.astype(BF16) for x in tinv]
        xs = [_dot((a_mats[n] * m_ref[lvl]).astype(BF16), tb[n]).astype(BF16) for n in range(n_ch)]
        tinv = [tinv[n] - _dot(tb[n], xs[n]) for n in range(n_ch)]
    sols = [_dot(tinv[n].astype(BF16), rhss[n]) for n in range(n_ch)]

    for n in range(n_ch):
        r0 = n * GDN_CHUNK
        gc = gcs[n]
        egc = jnp.exp(gc)
        u_all, w_all = sols[n][:, :GDN_HD], sols[n][:, GDN_HD:]
        qd = qms[n] * egc
        v_news, o_inter, gls = [], [], []
        for h in range(GDN_HEADS):
            rs = slice(h * GDN_CHUNK, (h + 1) * GDN_CHUNK)
            sb = s_scr[h].astype(BF16)
            both = _dot(jnp.concatenate([w_all[rs], qd[rs]], axis=0).astype(BF16), sb)
            v_news.append(u_all[rs] - both[:GDN_CHUNK])
            o_inter.append(both[GDN_CHUNK:])
            gls.append(gc[h * GDN_CHUNK + GDN_CHUNK - 1:(h + 1) * GDN_CHUNK, :])
        v_stack = jnp.concatenate(v_news, axis=0).astype(BF16)
        o_intra = _dot(qks[n], v_stack)
        for h in range(GDN_HEADS):
            rs = slice(h * GDN_CHUNK, (h + 1) * GDN_CHUNK)
            kd = kms[n][rs] * jnp.exp(gls[h] - gc[rs])
            s_scr[h] = s_scr[h] * jnp.exp(gls[h]) + _dot(kd.T.astype(BF16), v_stack[rs])
            o = (o_inter[h] + o_intra[rs])[:rv]
            on = o * lax.rsqrt(jnp.mean(o * o, axis=-1, keepdims=True) + EPS) * nw_ref[...]
            zz = z_ref[r0:r0 + rv, h * GDN_HD:(h + 1) * GDN_HD]
            o_ref[r0:r0 + rv, h * GDN_HD:(h + 1) * GDN_HD] = on * (zz * _sigmoid(zz))

    last = xs_scr[row0 + tt - (GDN_CONV - 1):row0 + tt, :]
    buf_ref[0] = last
    xs_scr[GDN_PREV0:row0, :] = last

    @pl.when(t == nt - 1)
    def _():
        s_ref[0] = s_scr[...]


def _gdn_masks2(chunk):
    n = GDN_HEADS * chunk
    r = np.arange(n)[:, None]
    c = np.arange(n)[None, :]
    same = lambda k: (r // k) == (c // k)
    m = [same(2) & (c < r)]
    k = 4
    while k <= chunk:
        m.append(same(k) & ~same(k // 2) & (c < r))
        k *= 2
    m += [same(chunk) & (c <= r), same(chunk) & (c < r), r == c]
    return np.stack(m).astype(np.float32)


def _gdn_kernel2(qkv_ref, z_ref, ba_ref, prev_ref, s0_ref, cw_ref, par_ref, nw_ref, m_ref,
                 o_ref, buf_ref, s_ref, xs_scr, s_scr, *, rv, chunk, n_units, sequential):
    t = pl.program_id(1)
    nt = pl.num_programs(1)
    stack = GDN_HEADS * chunk
    n_lvl = m_ref.shape[0] - 3
    causal, strict, eye = m_ref[n_lvl], m_ref[n_lvl + 1], m_ref[n_lvl + 2]
    kc = GDN_CONV - 1
    if sequential:
        base = [8 + u * rv for u in range(n_units)]

        @pl.when(t == 0)
        def _():
            xs_scr[8 - kc:8, :] = prev_ref[0]
            s_scr[...] = s0_ref[...]

        xs_scr[8:8 + n_units * rv, :] = qkv_ref[...]
    else:
        base = [u * (8 + rv) + 8 for u in range(n_units)]
        s_scr[...] = s0_ref[...]
        for u in range(n_units):
            xs_scr[base[u] - kc:base[u], :] = prev_ref[u]
            xs_scr[base[u]:base[u] + rv, :] = qkv_ref[u * rv:(u + 1) * rv, :]
    ba = ba_ref[...]
    beta_all = _sigmoid(ba)
    a_sh = ba + par_ref[1:2, :]
    g_all = -jnp.exp(par_ref[0:1, :]) * (jnp.maximum(a_sh, 0.0) + jnp.log1p(jnp.exp(-jnp.abs(a_sh))))
    pad = chunk - rv

    def padrows(a):
        if pad == 0:
            return a
        return jnp.concatenate([a, jnp.zeros((pad, a.shape[1]), a.dtype)], axis=0)

    qms, kms, vms, betas, gcols = [], [], [], [], []
    for u in range(n_units):
        r0 = u * rv
        y = cw_ref[0:1, :] * xs_scr[base[u] - kc:base[u] - kc + rv, :]
        for i in range(1, GDN_CONV):
            y = y + cw_ref[i:i + 1, :] * xs_scr[base[u] - kc + i:base[u] - kc + i + rv, :]
        y = y * _sigmoid(y)
        qs, ks, vs, bs, gs = [], [], [], [], []
        for h in range(GDN_HEADS):
            q = y[:, h * GDN_HD:(h + 1) * GDN_HD]
            k = y[:, GDN_W + h * GDN_HD:GDN_W + (h + 1) * GDN_HD]
            v = y[:, 2 * GDN_W + h * GDN_HD:2 * GDN_W + (h + 1) * GDN_HD]
            q = q * lax.rsqrt(jnp.sum(q * q, axis=-1, keepdims=True) + EPS) * (GDN_HD ** -0.5)
            k = k * lax.rsqrt(jnp.sum(k * k, axis=-1, keepdims=True) + EPS)
            qs.append(padrows(q)); ks.append(padrows(k)); vs.append(padrows(v))
            bs.append(padrows(beta_all[r0:r0 + rv, h:h + 1]))
            gs.append(padrows(g_all[r0:r0 + rv, GDN_HEADS + h:GDN_HEADS + h + 1]))
        qms.append(jnp.concatenate(qs, axis=0))
        kms.append(jnp.concatenate(ks, axis=0))
        vms.append(jnp.concatenate(vs, axis=0))
        betas.append(jnp.concatenate(bs, axis=0))
        gcols.append(jnp.concatenate(gs, axis=0))
    gmat = jnp.concatenate(gcols + [jnp.zeros((stack, LANE - n_units), F32)], axis=1)
    gc_all = jnp.dot(causal, gmat, preferred_element_type=F32, precision=lax.Precision.HIGHEST)

    a_bfs, qks, rhss, gcs, tinv = [], [], [], [], []
    for u in range(n_units):
        gc = gc_all[:, u:u + 1]
        gc_row = jnp.sum(eye * gc, axis=0, keepdims=True)
        decay = jnp.exp(jnp.where(causal > 0, gc - gc_row, 0.0)) * causal
        kb = kms[u] * betas[u]
        kmb = kms[u].astype(BF16)
        a_mat = _dot_t(kb.astype(BF16), kmb) * decay * strict
        qks.append((_dot_t(qms[u].astype(BF16), kmb) * decay).astype(BF16))
        rhss.append(jnp.concatenate([vms[u] * betas[u], kb * jnp.exp(gc)], axis=1).astype(BF16))
        a_bfs.append(a_mat.astype(BF16))
        gcs.append(gc)
        tinv.append(eye - a_mat * m_ref[0])
    for lvl in range(1, n_lvl):
        tb = [x.astype(BF16) for x in tinv]
        xs = [(_dot(a_bfs[u], tb[u]) * m_ref[lvl]).astype(BF16) for u in range(n_units)]
        tinv = [tinv[u] - _dot(tb[u], xs[u]) for u in range(n_units)]
    sols = [_dot(tinv[u].astype(BF16), rhss[u]) for u in range(n_units)]

    for u in range(n_units):
        r0 = u * rv
        st = 0 if sequential else u
        gc = gcs[u]
        u_all, w_all = sols[u][:, :GDN_HD], sols[u][:, GDN_HD:]
        qd = qms[u] * jnp.exp(gc)
        v_news, o_inter, gls = [], [], []
        for h in range(GDN_HEADS):
            rs = slice(h * chunk, (h + 1) * chunk)
            sb = s_scr[st, h].astype(BF16)
            both = _dot(jnp.concatenate([w_all[rs], qd[rs]], axis=0).astype(BF16), sb)
            v_news.append(u_all[rs] - both[:chunk])
            o_inter.append(both[chunk:])
            gls.append(gc[(h + 1) * chunk - 1:(h + 1) * chunk, :])
        v_stack = jnp.concatenate(v_news, axis=0).astype(BF16)
        o_intra = _dot(qks[u], v_stack)
        for h in range(GDN_HEADS):
            rs = slice(h * chunk, (h + 1) * chunk)
            kd = kms[u][rs] * jnp.exp(gls[h] - gc[rs])
            s_scr[st, h] = s_scr[st, h] * jnp.exp(gls[h]) + _dot(kd.T.astype(BF16), v_stack[rs])
            o = (o_inter[h] + o_intra[rs])[:rv]
            on = o * lax.rsqrt(jnp.mean(o * o, axis=-1, keepdims=True) + EPS) * nw_ref[...]
            zz = z_ref[r0:r0 + rv, h * GDN_HD:(h + 1) * GDN_HD]
            o_ref[r0:r0 + rv, h * GDN_HD:(h + 1) * GDN_HD] = on * (zz * _sigmoid(zz))

    if sequential:
        last = xs_scr[8 + n_units * rv - kc:8 + n_units * rv, :]
        buf_ref[0] = last
        xs_scr[8 - kc:8, :] = last

        @pl.when(t == nt - 1)
        def _():
            s_ref[...] = s_scr[...]
    else:
        for u in range(n_units):
            buf_ref[u] = xs_scr[base[u] + rv - kc:base[u] + rv, :]
        s_ref[...] = s_scr[...]


def gdn2(qkv_pre, z_pre, ba_pre, prev, s0, conv_w, a_log, dt_bias, norm_w, n_seq, units):
    n_rows = qkv_pre.shape[0]
    t_rows = n_rows // n_seq
    sequential = t_rows >= GDN_CHUNK
    if sequential:
        rv = chunk = GDN_CHUNK
        units = min(units, t_rows // chunk)
        grid = (n_seq, t_rows // (units * chunk))
        seq_blk, scr_rows = 1, 8 + units * rv
    else:
        rv, chunk = t_rows, 16
        units = math.gcd(units, n_seq)
        assert rv % 8 == 0 and rv <= chunk
        grid = (n_seq // units, 1)
        seq_blk, scr_rows = units, units * (8 + rv)
    tt = units * rv
    nt = grid[1]
    par = jnp.zeros((8, LANE), F32).at[0, GDN_HEADS:2 * GDN_HEADS].set(a_log).at[1, GDN_HEADS:2 * GDN_HEADS].set(dt_bias)
    masks = jnp.asarray(_gdn_masks2(chunk))
    const2 = lambda b, i: (0, 0)
    rows = lambda b, i: (b * nt + i, 0)
    return pl.pallas_call(
        functools.partial(_gdn_kernel2, rv=rv, chunk=chunk, n_units=units, sequential=sequential), grid=grid,
        in_specs=[pl.BlockSpec((tt, GDN_C3), rows), pl.BlockSpec((tt, GDN_W), rows),
                  pl.BlockSpec((tt, LANE), rows),
                  pl.BlockSpec((seq_blk, GDN_CONV - 1, GDN_C3), lambda b, i: (b, 0, 0)),
                  pl.BlockSpec((seq_blk, GDN_HEADS, GDN_HD, GDN_HD), lambda b, i: (b, 0, 0, 0)),
                  pl.BlockSpec((GDN_CONV, GDN_C3), const2), pl.BlockSpec((8, LANE), const2),
                  pl.BlockSpec((1, GDN_HD), const2),
                  pl.BlockSpec(masks.shape, lambda b, i: (0, 0, 0))],
        out_specs=[pl.BlockSpec((tt, GDN_W), rows),
                   pl.BlockSpec((seq_blk, GDN_CONV - 1, GDN_C3), lambda b, i: (b, 0, 0)),
                   pl.BlockSpec((seq_blk, GDN_HEADS, GDN_HD, GDN_HD), lambda b, i: (b, 0, 0, 0))],
        out_shape=[jax.ShapeDtypeStruct((n_rows, GDN_W), F32),
                   jax.ShapeDtypeStruct((n_seq, GDN_CONV - 1, GDN_C3), F32),
                   jax.ShapeDtypeStruct((n_seq, GDN_HEADS, GDN_HD, GDN_HD), F32)],
        scratch_shapes=[pltpu.VMEM((scr_rows, GDN_C3), F32),
                        pltpu.VMEM((seq_blk, GDN_HEADS, GDN_HD, GDN_HD), F32)],
        compiler_params=_cparams(("arbitrary", "arbitrary"), 48), name="gdn",
    )(qkv_pre, z_pre, ba_pre, prev, s0, conv_w, par, norm_w.reshape(1, GDN_HD), masks)


def gdn(qkv_pre, z_pre, ba_pre, prev, s0, conv_w, a_log, dt_bias, norm_w, n_seq, tt):
    n_rows = qkv_pre.shape[0]
    t_rows = n_rows // n_seq
    tt = min(tt, t_rows)
    nt = t_rows // tt
    par = jnp.zeros((8, LANE), F32).at[0, :GDN_HEADS].set(a_log).at[1, :GDN_HEADS].set(dt_bias)
    masks = jnp.asarray(_gdn_masks())
    const2 = lambda b, i: (0, 0)
    rows = lambda b, i: (b * nt + i, 0)
    return pl.pallas_call(
        functools.partial(_gdn_kernel, tt=tt), grid=(n_seq, nt),
        in_specs=[pl.BlockSpec((tt, GDN_C3), rows), pl.BlockSpec((tt, GDN_W), rows),
                  pl.BlockSpec((tt, LANE), rows),
                  pl.BlockSpec((1, GDN_CONV - 1, GDN_C3), lambda b, i: (b, 0, 0)),
                  pl.BlockSpec((1, GDN_HEADS, GDN_HD, GDN_HD), lambda b, i: (b, 0, 0, 0)),
                  pl.BlockSpec((GDN_CONV, GDN_C3), const2), pl.BlockSpec((8, LANE), const2),
                  pl.BlockSpec((1, GDN_HD), const2),
                  pl.BlockSpec(masks.shape, lambda b, i: (0, 0, 0))],
        out_specs=[pl.BlockSpec((tt, GDN_W), rows),
                   pl.BlockSpec((1, GDN_CONV - 1, GDN_C3), lambda b, i: (b, 0, 0)),
                   pl.BlockSpec((1, GDN_HEADS, GDN_HD, GDN_HD), lambda b, i: (b, 0, 0, 0))],
        out_shape=[jax.ShapeDtypeStruct((n_rows, GDN_W), F32),
                   jax.ShapeDtypeStruct((n_seq, GDN_CONV - 1, GDN_C3), F32),
                   jax.ShapeDtypeStruct((n_seq, GDN_HEADS, GDN_HD, GDN_HD), F32)],
        scratch_shapes=[pltpu.VMEM((8 + tt, GDN_C3), F32), pltpu.VMEM((GDN_HEADS, GDN_HD, GDN_HD), F32)],
        compiler_params=_cparams(("arbitrary", "arbitrary"), 48), name="gdn",
    )(qkv_pre, z_pre, ba_pre, prev, s0, conv_w, par, norm_w.reshape(1, GDN_HD), masks)


CMP_W = 2 * NSA_KV_HEADS * NSA_HD


def _cmp_weights(cmp_pe, cmp_w):
    def bd(w):
        z = jnp.zeros_like(w)
        return jnp.concatenate([jnp.concatenate([w, z], 2), jnp.concatenate([z, w], 2)], 1)
    ws, cs = [], []
    for s in range(2):
        lo, hi = bd(cmp_w[s, :CMP_STRIDE]), bd(cmp_w[s, CMP_STRIDE:])
        cat = jnp.concatenate([lo, hi], axis=2)
        ws.append(cat.reshape(CMP_STRIDE // 2, 2 * LANE, 2 * LANE))
        pe2 = jnp.concatenate([cmp_pe[s], cmp_pe[s]], axis=-1)
        const = [jnp.einsum("jd,jde->e", pe2[h * CMP_STRIDE:(h + 1) * CMP_STRIDE], m,
                            precision=lax.Precision.HIGHEST) for h, m in enumerate((lo, hi))]
        cs.append(jnp.concatenate(const))
    return jnp.stack(ws).astype(BF16), jnp.stack(cs)


def _compress(read_k, read_v, w_ref, c_ref, n_chunk):
    outs = []
    for s, read in enumerate((read_k, read_v)):
        acc = jnp.zeros((n_chunk, 2 * LANE), F32)
        for p in range(CMP_STRIDE // 2):
            x = jnp.concatenate([read(2 * p), read(2 * p + 1)], axis=1).astype(BF16)
            acc = acc + _dot(x, w_ref[s, p])
        acc = acc + c_ref[s:s + 1, :]
        outs.append(acc[:, :LANE] + pltpu.roll(acc[:, LANE:], n_chunk - 1, axis=0))
    return jnp.concatenate(outs, axis=1)


def _cmp_kernel(k_ref, v_ref, w_ref, pe_ref, o_ref, *, n_chunk):
    o_ref[0] = _compress(lambda j: k_ref[0, pl.ds(j, n_chunk, stride=CMP_STRIDE), :],
                         lambda j: v_ref[0, pl.ds(j, n_chunk, stride=CMP_STRIDE), :], w_ref, pe_ref, n_chunk)


def compress_prompt(rows, w, pe):
    b, t, _ = rows.shape
    n_chunk = t // CMP_STRIDE
    return pl.pallas_call(
        functools.partial(_cmp_kernel, n_chunk=n_chunk), grid=(b,),
        in_specs=[pl.BlockSpec((1, t, LANE), lambda i: (i, 0, 0)), pl.BlockSpec((1, t, LANE), lambda i: (i, 0, 1)),
                  pl.BlockSpec(w.shape, lambda i: (0, 0, 0, 0)),
                  pl.BlockSpec(pe.shape, lambda i: (0, 0))],
        out_specs=pl.BlockSpec((1, n_chunk, CMP_W), lambda i: (i, 0, 0)),
        out_shape=jax.ShapeDtypeStruct((b, n_chunk, CMP_W), F32),
        compiler_params=_cparams(("parallel",), 32), name="nsa_compress",
    )(rows, rows, w, pe)


def _c2s(n_cmp_pad, n_sel_pad):
    cs = np.arange(n_cmp_pad)[:, None] * CMP_STRIDE
    ss = np.arange(n_sel_pad)[None, :] * SEL_BLOCK
    return ((cs < ss + SEL_BLOCK) & (cs + CMP_BLOCK > ss)).astype(np.float32)


def _expand(n_sel_pad, n_keys):
    s = np.arange(n_sel_pad)[:, None]
    k = np.arange(n_keys)[None, :]
    return (k // SEL_BLOCK == s).astype(np.float32)


def _cmp_probs(s, valid):
    s = jnp.where(valid, s, NEG)
    p = jnp.where(valid, jnp.exp(s - jnp.max(s, axis=-1, keepdims=True)), 0.0)
    return p / jnp.maximum(jnp.sum(p, axis=-1, keepdims=True), 1e-30)


def _select_blocks(imp, q_blk, n_sel):
    lane = lax.broadcasted_iota(jnp.int32, imp.shape, 1)
    visible = lane <= q_blk
    forced = (lane == 0) | (lane == q_blk) | (lane == q_blk - 1)
    score = jnp.where(visible, imp + jnp.where(forced, FORCED_BONUS, 0.0), NEG)
    rank = jnp.zeros(imp.shape, F32)
    for sp in range(n_sel):
        col = score[:, sp:sp + 1]
        beats = (col > score) | ((col == score) & (lane > sp))
        rank = rank + jnp.where(beats, 1.0, 0.0)
    return jnp.where(visible & (rank < SEL_TOPN), 1.0, 0.0)


def _flash(q, k_ref, v_ref, j_lo, j_hi, mask_fn, tq, tk):
    def body(j, carry):
        m, l, acc = carry
        off = pl.multiple_of(j * tk, tk)
        s = _dot_t(q, k_ref[pl.ds(off, tk), :])
        s = jnp.where(mask_fn(j), s, NEG)
        m_new = jnp.maximum(m, jnp.max(s, axis=-1, keepdims=True))
        alpha = jnp.exp(m - m_new)
        p = jnp.exp(s - m_new)
        l = alpha * l + jnp.sum(p, axis=-1, keepdims=True)
        acc = alpha * acc + _dot(p.astype(BF16), v_ref[pl.ds(off, tk), :])
        return m_new, l, acc

    init = (jnp.full((tq, 1), NEG, F32), jnp.zeros((tq, 1), F32), jnp.zeros((tq, NSA_HD), F32))
    _, l, acc = lax.fori_loop(j_lo, j_hi, body, init)
    return acc / l


def _nsa_prompt_kernel_v1(qc_ref, qr_ref, kc_ref, vc_ref, ks_ref, vs_ref, kw_ref, vw_ref, g_ref, c2s_ref, e_ref,
                       o_ref, mask_scr, *, tq, n_sel, n_cmp_pad):
    i = pl.program_id(2)
    t0 = i * tq
    qpos = t0 + lax.broadcasted_iota(jnp.int32, (tq, 1), 0)
    cend = lax.broadcasted_iota(jnp.int32, (1, n_cmp_pad), 1) * CMP_STRIDE + (CMP_BLOCK - 1)
    valid_c = cend <= qpos
    psum = jnp.zeros((tq, n_cmp_pad), F32)
    o_cmp = []
    for g in range(NSA_GROUP):
        p = _cmp_probs(_dot_t(qc_ref[0, g], kc_ref[0, 0]), valid_c)
        psum = psum + p
        o_cmp.append(_dot(p.astype(BF16), vc_ref[0, 0]))
    imp = jnp.dot(psum, c2s_ref[...], preferred_element_type=F32, precision=lax.Precision.HIGHEST)
    sel = _select_blocks(imp, lax.shift_right_logical(qpos, 6), n_sel).astype(BF16)
    n_kt = mask_scr.shape[0]
    for jj in range(n_kt):
        mask_scr[jj] = _dot(sel, e_ref[:, jj * tq:(jj + 1) * tq])
    kiota = lax.broadcasted_iota(jnp.int32, (1, tq), 1)

    def sel_mask(j):
        return (mask_scr[j] > 0.5) & (j * tq + kiota <= qpos)

    def win_mask(j):
        rel = qpos - (j * tq + kiota)
        return (rel >= 0) & (rel < WINDOW)

    w_tiles = WINDOW // tq
    for g in range(NSA_GROUP):
        q = qr_ref[0, g]
        o_sel = _flash(q, ks_ref.at[0, 0], vs_ref.at[0, 0], 0, i + 1, sel_mask, tq, tq)
        o_win = _flash(q, kw_ref.at[0, 0], vw_ref.at[0, 0], jnp.maximum(i - w_tiles, 0), i + 1, win_mask, tq, tq)
        gt = g_ref[0, g]
        o = o_cmp[g] * gt[:, 0:1] + o_sel * gt[:, 1:2] + o_win * gt[:, 2:3]
        o_ref[0, :, g * NSA_HD:(g + 1) * NSA_HD] = o


def nsa_prompt_attn_v1(qc, qr, kcmp, vcmp, ksel, vsel, kwin, vwin, gates, tq):
    b, _, t, _ = qc.shape
    tq = min(tq, t)
    n_cmp_pad = kcmp.shape[2]
    n_sel = t // SEL_BLOCK
    n_sel_pad = -(-n_sel // LANE) * LANE
    c2s = jnp.asarray(_c2s(n_cmp_pad, n_sel_pad))
    e = jnp.asarray(_expand(n_sel_pad, t), dtype=BF16)
    kern = functools.partial(_nsa_prompt_kernel_v1, tq=tq, n_sel=n_sel, n_cmp_pad=n_cmp_pad)
    qspec = pl.BlockSpec((1, NSA_GROUP, tq, NSA_HD), lambda bb, k, i: (bb, k, i, 0))
    kvspec = lambda n: pl.BlockSpec((1, 1, n, NSA_HD), lambda bb, k, i: (bb, k, 0, 0))
    return pl.pallas_call(
        kern, grid=(b, NSA_KV_HEADS, t // tq),
        in_specs=[qspec, qspec, kvspec(n_cmp_pad), kvspec(n_cmp_pad), kvspec(t), kvspec(t), kvspec(t), kvspec(t),
                  pl.BlockSpec((1, NSA_GROUP, tq, 3), lambda bb, k, i: (bb, k, i, 0)),
                  pl.BlockSpec(c2s.shape, lambda bb, k, i: (0, 0)), pl.BlockSpec(e.shape, lambda bb, k, i: (0, 0))],
        out_specs=pl.BlockSpec((1, tq, NSA_GROUP * NSA_HD), lambda bb, k, i: (bb, i, k)),
        out_shape=jax.ShapeDtypeStruct((b, t, NSA_Q_W), F32),
        scratch_shapes=[pltpu.VMEM((t // tq, tq, tq), F32)],
        compiler_params=_cparams(("parallel", "parallel", "parallel"), 40), name="nsa_prompt_attn",
    )(qc, qr, kcmp, vcmp, ksel, vsel, kwin, vwin, gates, c2s, e)


def _pair_rms(x, w):
    lo = lax.broadcasted_iota(jnp.int32, x.shape, 1) < NSA_HD
    x2 = x * x
    s_lo = jnp.sum(jnp.where(lo, x2, 0.0), axis=-1, keepdims=True)
    s_hi = jnp.sum(jnp.where(lo, 0.0, x2), axis=-1, keepdims=True)
    ms = jnp.where(lo, s_lo, s_hi) * (1.0 / NSA_HD)
    return x * lax.rsqrt(ms + EPS) * w


def _pair_rope(x, cos_t, sin_t):
    lane = lax.broadcasted_iota(jnp.int32, x.shape, 1)
    first = (lane & (NSA_HD - 1)) < NSA_HD // 2
    partner = jnp.where(first, pltpu.roll(x, LANE - NSA_HD // 2, axis=1), pltpu.roll(x, NSA_HD // 2, axis=1))
    return x * cos_t + partner * sin_t


def _rope_tables(pos):
    half = NSA_HD // 2
    inv = jnp.power(ROPE_THETA, -jnp.arange(half, dtype=F32) / half)
    ang = pos.astype(F32)[:, None] * inv[None, :]
    c, s = jnp.cos(ang), jnp.sin(ang)
    return jnp.concatenate([c, c, c, c], axis=-1), jnp.concatenate([-s, s, -s, s], axis=-1)


def _nsa_prep_kernel_v2(q_ref, kv_ref, g_ref, cos_ref, sin_ref, qn_ref, kn_ref,
                        rows_ref, wrows_ref, qc_ref, qr_ref, gate_ref, kvb_ref=None, cmp_ref=None):
    cos_t, sin_t = cos_ref[...], sin_ref[...]
    lo = lax.broadcasted_iota(jnp.int32, cos_t.shape, 1) < NSA_HD
    for c in range(NSA_HEADS // 2):
        y = _pair_rms(q_ref[:, c * LANE:(c + 1) * LANE], qn_ref[...]) * (NSA_HD ** -0.5)
        in_low_lanes = (2 * c) // NSA_GROUP == 0
        for src, dst in ((y, qc_ref), (_pair_rope(y, cos_t, sin_t) * LOG2E, qr_ref)):
            swapped = pltpu.roll(src, NSA_HD, axis=1)
            if in_low_lanes:
                even, odd = jnp.where(lo, src, 0.0), jnp.where(lo, swapped, 0.0)
            else:
                even, odd = jnp.where(lo, 0.0, swapped), jnp.where(lo, 0.0, src)
            dst[:, (2 * c) * LANE:(2 * c + 1) * LANE] = even.astype(BF16)
            dst[:, (2 * c + 1) * LANE:(2 * c + 2) * LANE] = odd.astype(BF16)
    for br in range(3):
        k = _pair_rms(kv_ref[:, br * 2 * LANE:br * 2 * LANE + LANE], kn_ref[br:br + 1, :])
        if br > 0:
            k = _pair_rope(k, cos_t, sin_t)
        v = kv_ref[:, br * 2 * LANE + LANE:(br + 1) * 2 * LANE]
        if cmp_ref is None:
            dst, off = (rows_ref, br * 2 * LANE) if br < 2 else (wrows_ref, 0)
            dst[:, off:off + LANE] = k
            dst[:, off + LANE:off + 2 * LANE] = v
        else:
            dst, w0 = (rows_ref, 2 * br) if br < 2 else (wrows_ref, 0)
            for w, x in ((w0, k), (w0 + 1, v)):
                xt = x.T
                for h in range(NSA_KV_HEADS):
                    dst[0, w, h] = xt[h * NSA_HD:(h + 1) * NSA_HD]
            if br == 0:
                cmp_ref[:, :LANE] = k
                cmp_ref[:, LANE:] = v
            else:
                kvb_ref[:, (br - 1) * LANE:br * LANE] = k.astype(BF16)
                kvb_ref[:, 2 * br * LANE:(2 * br + 1) * LANE] = jnp.where(lo, v, 1.0).astype(BF16)
                kvb_ref[:, (2 * br + 1) * LANE:(2 * br + 2) * LANE] = jnp.where(lo, 1.0, v).astype(BF16)
    gate_ref[...] = _sigmoid(g_ref[...])


def nsa_prep_v2(q_pre, kv_pre, gate_pre, cos_t, sin_t, qn, kn, tm, dims_major):
    n_rows = q_pre.shape[0]
    tm = min(tm, n_rows)
    n_tab = cos_t.shape[0] // tm
    rows = lambda w: pl.BlockSpec((tm, w), lambda i: (i, 0))
    tab = pl.BlockSpec((tm, LANE), lambda i: (i % n_tab, 0))
    widths = (NSA_HEADS * LANE, NSA_HEADS * LANE, LANE)
    dtypes = (BF16, BF16, F32)
    out_specs = [rows(w) for w in widths]
    out_shape = [jax.ShapeDtypeStruct((n_rows, w), dt) for w, dt in zip(widths, dtypes)]
    if dims_major:
        n_seq, t = n_rows // cos_t.shape[0], cos_t.shape[0]
        kv_spec = lambda n: pl.BlockSpec((1, n, NSA_KV_HEADS, NSA_HD, tm), lambda i: (i // n_tab, 0, 0, 0, i % n_tab))
        kv_shape = lambda n: jax.ShapeDtypeStruct((n_seq, n, NSA_KV_HEADS, NSA_HD, t), F32)
        out_specs = [kv_spec(4), kv_spec(2)] + out_specs + [rows(NSA_KV_W), rows(2 * LANE)]
        out_shape = [kv_shape(4), kv_shape(2)] + out_shape + [jax.ShapeDtypeStruct((n_rows, NSA_KV_W), BF16),
                                                              jax.ShapeDtypeStruct((n_rows, 2 * LANE), F32)]
    else:
        out_specs = [rows(NSA_Q_W), rows(2 * LANE)] + out_specs
        out_shape = [jax.ShapeDtypeStruct((n_rows, NSA_Q_W), F32),
                     jax.ShapeDtypeStruct((n_rows, 2 * LANE), F32)] + out_shape
    return pl.pallas_call(
        _nsa_prep_kernel_v2, grid=(n_rows // tm,),
        in_specs=[rows(NSA_Q_W), rows(NSA_KV_W), rows(LANE), tab, tab,
                  pl.BlockSpec((1, LANE), lambda i: (0, 0)), pl.BlockSpec((3, LANE), lambda i: (0, 0))],
        out_specs=out_specs, out_shape=out_shape,
        compiler_params=_cparams(("parallel",), 40), name="nsa_prep",
    )(q_pre, kv_pre, gate_pre, cos_t, sin_t, jnp.tile(qn.reshape(1, NSA_HD), (1, 2)), jnp.tile(kn, (1, 2)))


def _select_blocks_t(imp_t, q_blk, n_sel):
    blk = lax.broadcasted_iota(jnp.int32, imp_t.shape, 0)
    visible = blk <= q_blk
    forced = jnp.where(blk == 0, 1.0, 0.0) + jnp.where(blk == q_blk, 1.0, 0.0) + jnp.where(blk == q_blk - 1, 1.0, 0.0)
    score = jnp.where(visible, imp_t + jnp.where(forced > 0.0, FORCED_BONUS, 0.0), NEG)
    rank = jnp.zeros(imp_t.shape, F32)
    for sp in range(n_sel):
        row = score[sp:sp + 1, :]
        tie = jnp.where(blk > sp, jnp.where(row == score, 1.0, 0.0), 0.0)
        rank = rank + jnp.where(row > score, 1.0, 0.0) + tie
    return jnp.where(visible, jnp.where(rank < SEL_TOPN, 1.0, 0.0), 0.0)


def _flash8_v2(qs, k_ref, v_refs, j_lo, j_hi, bias_fn, tq, tk):
    def body(j, carry):
        off = pl.multiple_of(j * tk, tk)
        k = k_ref[pl.ds(off, tk), :]
        vs = [v_ref[pl.ds(off, tk), :] for v_ref in v_refs]
        bias = bias_fn(j)
        out = []
        for h, (m, acc) in enumerate(carry):
            s = _dot_t(qs[h], k).astype(BF16) + bias[h // NSA_GROUP]
            m_new = jnp.maximum(m, jnp.max(s, axis=-1, keepdims=True).astype(F32))
            p = jnp.exp2(s - m_new.astype(BF16))
            acc = jnp.exp2(m - m_new) * acc + _dot(p, vs[h // NSA_GROUP])
            out.append((m_new, acc))
        return tuple(out)

    init = tuple((jnp.full((tq, 1), NEG, F32), jnp.zeros((tq, LANE), F32)) for _ in range(NSA_HEADS))
    outs = []
    for h, (_, acc) in enumerate(lax.fori_loop(j_lo, j_hi, body, init)):
        c = NSA_HD if h // NSA_GROUP == 0 else 0
        outs.append(acc / acc[:, c:c + 1])
    return outs


def _nsa_prompt_kernel_v2(qc_ref, qr_ref, kvc_ref, ks_ref, kw_ref, vs0_ref, vs1_ref, vw0_ref, vw1_ref, g_ref,
                          c2s_ref, e_ref, o_ref, *, tq, n_sel):
    i = pl.program_id(1)
    t0 = i * tq
    n_rows = NSA_HEADS * tq
    stack = lambda ref: jnp.concatenate([ref[:, h * LANE:(h + 1) * LANE] for h in range(NSA_HEADS)], axis=0)
    qpos = t0 + lax.broadcasted_iota(jnp.int32, (tq, 1), 0)
    n_cmp = kvc_ref.shape[1]
    kc = kvc_ref[0, :, :LANE].astype(BF16)
    vc = kvc_ref[0, :, LANE:].astype(BF16)
    cend = lax.broadcasted_iota(jnp.int32, (1, n_cmp), 1) * CMP_STRIDE + (CMP_BLOCK - 1)
    s = _dot_t(stack(qc_ref), kc).reshape(NSA_KV_HEADS, NSA_GROUP, tq, n_cmp)
    p = _cmp_probs(s, (cend <= qpos)[None, None])
    o_cmp = _dot(p.reshape(n_rows, n_cmp).astype(BF16), vc)
    psum = p[:, 0]
    for g in range(1, NSA_GROUP):
        psum = psum + p[:, g]
    imp = jnp.dot(psum.reshape(NSA_KV_HEADS * tq, n_cmp), c2s_ref[...], preferred_element_type=F32,
                  precision=lax.Precision.HIGHEST)
    n_blk_rows = -(-n_sel // 8) * 8
    col = lax.broadcasted_iota(jnp.int32, (1, NSA_KV_HEADS * tq), 1)
    q_blk = lax.shift_right_logical(t0 + (col & (tq - 1)), 6)
    sel_t = _select_blocks_t(imp.T[:n_blk_rows], q_blk, n_sel)
    sel_t = jnp.concatenate([sel_t, jnp.zeros((LANE - n_blk_rows, NSA_KV_HEADS * tq), F32)], axis=0)
    sel = sel_t.T.astype(BF16)
    kiota = lax.broadcasted_iota(jnp.int32, (1, tq), 1)

    def sel_bias(j):
        causal = (j * tq + kiota) <= qpos
        return [((jnp.where(causal, _dot(sel[k * tq:(k + 1) * tq], e_ref[j]), 0.0) - 1.0) * -NEG).astype(BF16)
                for k in range(NSA_KV_HEADS)]

    def win_bias(j):
        rel = qpos - (j * tq + kiota)
        return [jnp.where(jnp.where(rel >= 0, rel, WINDOW) < WINDOW, 0.0, NEG).astype(BF16)] * NSA_KV_HEADS

    qr = [qr_ref[:, h * LANE:(h + 1) * LANE] for h in range(NSA_HEADS)]
    o_sel = _flash8_v2(qr, ks_ref.at[0], [vs0_ref.at[0], vs1_ref.at[0]], 0, i + 1, sel_bias, tq, tq)
    o_win = _flash8_v2(qr, kw_ref.at[0], [vw0_ref.at[0], vw1_ref.at[0]], jnp.maximum(i - WINDOW // tq, 0), i + 1,
                       win_bias, tq, tq)
    lo = lax.broadcasted_iota(jnp.int32, (tq, LANE), 1) < NSA_HD
    for c in range(NSA_HEADS // 2):
        pair = []
        for h in (2 * c, 2 * c + 1):
            rs = slice(h * tq, (h + 1) * tq)
            gt = g_ref[:, 3 * h:3 * h + 3]
            pair.append(o_cmp[rs] * gt[:, 0:1] + o_sel[h] * gt[:, 1:2] + o_win[h] * gt[:, 2:3])
        even, odd = pair
        if (2 * c) // NSA_GROUP == 0:
            blk = jnp.where(lo, even, pltpu.roll(odd, NSA_HD, axis=1))
        else:
            blk = jnp.where(lo, pltpu.roll(even, NSA_HD, axis=1), odd)
        o_ref[:, c * LANE:(c + 1) * LANE] = blk


def nsa_prompt_attn_v2(qc, qr, kvc, kvb, gates, n_batch, tq):
    n_rows = qc.shape[0]
    t = n_rows // n_batch
    tq = min(tq, t)
    nt = t // tq
    n_cmp = kvc.shape[1]
    n_sel = t // SEL_BLOCK
    assert n_cmp % LANE == 0 and n_sel <= LANE and tq & (tq - 1) == 0 and WINDOW % tq == 0
    c2s = jnp.asarray(_c2s(n_cmp, LANE))
    e = jnp.asarray(_expand(LANE, t).reshape(LANE, nt, tq).transpose(1, 0, 2), dtype=BF16)
    kvb3 = kvb.reshape(n_batch, t, NSA_KV_W)
    rows = lambda w: pl.BlockSpec((tq, w), lambda b, i: (b * nt + i, 0))
    kv = lambda c: pl.BlockSpec((1, t, LANE), lambda b, i: (b, 0, c))
    return pl.pallas_call(
        functools.partial(_nsa_prompt_kernel_v2, tq=tq, n_sel=n_sel), grid=(n_batch, nt),
        in_specs=[rows(NSA_HEADS * LANE), rows(NSA_HEADS * LANE),
                  pl.BlockSpec((1, n_cmp, CMP_W), lambda b, i: (b, 0, 0)),
                  kv(0), kv(1), kv(2), kv(3), kv(4), kv(5), rows(LANE),
                  pl.BlockSpec(c2s.shape, lambda b, i: (0, 0)), pl.BlockSpec(e.shape, lambda b, i: (0, 0, 0))],
        out_specs=rows(NSA_Q_W),
        out_shape=jax.ShapeDtypeStruct((n_rows, NSA_Q_W), F32),
        compiler_params=_cparams(("parallel", "parallel"), 48), name="nsa_prompt_attn",
    )(qc, qr, kvc, kvb3, kvb3, kvb3, kvb3, kvb3, kvb3, gates, c2s, e)


LOG2E = 1.4426950408889634
KEY_TILE = 256


def _head_rows(yt, head_in_pair, kv_head):
    own = yt[head_in_pair * NSA_HD:(head_in_pair + 1) * NSA_HD]
    zero = jnp.zeros_like(own)
    return jnp.concatenate([own, zero] if kv_head == 0 else [zero, own], axis=0)


def _nsa_prep_kernel(q_ref, kv_ref, g_ref, cos_ref, sin_ref, qn_ref, kn_ref,
                     rows_ref, wrows_ref, qct_ref, qrt_ref, gt_ref, kb_ref=None, vt_ref=None, *, tm):
    cos_t, sin_t = cos_ref[...], sin_ref[...]
    for c in range(NSA_HEADS // 2):
        y = _pair_rms(q_ref[:, c * LANE:(c + 1) * LANE], qn_ref[...]) * (NSA_HD ** -0.5)
        yr = _pair_rope(y, cos_t, sin_t) * LOG2E
        for src, dst in ((y, qct_ref), (yr, qrt_ref)):
            st = src.T
            for e in range(2):
                h = 2 * c + e
                dst[h * LANE:(h + 1) * LANE, :] = _head_rows(st, e, h // NSA_GROUP).astype(BF16)
    for br in range(3):
        k = _pair_rms(kv_ref[:, br * 2 * LANE:br * 2 * LANE + LANE], kn_ref[br:br + 1, :])
        if br > 0:
            k = _pair_rope(k, cos_t, sin_t)
        v = kv_ref[:, br * 2 * LANE + LANE:(br + 1) * 2 * LANE]
        dst, off = (rows_ref, br * 2 * LANE) if br < 2 else (wrows_ref, 0)
        dst[:, off:off + LANE] = k
        dst[:, off + LANE:off + 2 * LANE] = v
        if br > 0 and kb_ref is not None:
            kb_ref[:, (br - 1) * LANE:br * LANE] = k.astype(BF16)
            vt = v.T.astype(BF16)
            for jt in range(tm // KEY_TILE):
                vt_ref[br - 1, jt] = vt[:, jt * KEY_TILE:(jt + 1) * KEY_TILE]
    gt_ref[...] = _sigmoid(g_ref[...]).T


def nsa_prep(q_pre, kv_pre, gate_pre, cos_t, sin_t, qn, kn, tm, attn_operands):
    n_rows = q_pre.shape[0]
    tm = min(tm, n_rows)
    n_tab = cos_t.shape[0] // tm
    rows = lambda w: pl.BlockSpec((tm, w), lambda i: (i, 0))
    cols = lambda r: pl.BlockSpec((r, tm), lambda i: (0, i))
    tab = pl.BlockSpec((tm, LANE), lambda i: (i % n_tab, 0))
    out_specs = [rows(NSA_Q_W), rows(2 * LANE), cols(NSA_HEADS * LANE), cols(NSA_HEADS * LANE), cols(LANE)]
    out_shape = [jax.ShapeDtypeStruct((n_rows, NSA_Q_W), F32), jax.ShapeDtypeStruct((n_rows, 2 * LANE), F32),
                 jax.ShapeDtypeStruct((NSA_HEADS * LANE, n_rows), BF16),
                 jax.ShapeDtypeStruct((NSA_HEADS * LANE, n_rows), BF16),
                 jax.ShapeDtypeStruct((LANE, n_rows), F32)]
    if attn_operands:
        assert tm % KEY_TILE == 0
        out_specs += [rows(2 * LANE), pl.BlockSpec((2, tm // KEY_TILE, LANE, KEY_TILE), lambda i: (0, i, 0, 0))]
        out_shape += [jax.ShapeDtypeStruct((n_rows, 2 * LANE), BF16),
                      jax.ShapeDtypeStruct((2, n_rows // KEY_TILE, LANE, KEY_TILE), BF16)]
    return pl.pallas_call(
        functools.partial(_nsa_prep_kernel, tm=tm), grid=(n_rows // tm,),
        in_specs=[rows(NSA_Q_W), rows(NSA_KV_W), rows(LANE), tab, tab,
                  pl.BlockSpec((1, LANE), lambda i: (0, 0)), pl.BlockSpec((3, LANE), lambda i: (0, 0))],
        out_specs=out_specs, out_shape=out_shape,
        compiler_params=_cparams(("parallel",), 40), name="nsa_prep",
    )(q_pre, kv_pre, gate_pre, cos_t, sin_t, jnp.tile(qn.reshape(1, NSA_HD), (1, 2)), jnp.tile(kn, (1, 2)))


def _flash_t(qts, k_ref, vt_ref, j_lo, j_hi, valid_fn, tq):
    def body(j, carry):
        k = k_ref[pl.ds(pl.multiple_of(j * KEY_TILE, KEY_TILE), KEY_TILE), :]
        vt = vt_ref[j]
        valid = valid_fn(j)
        out = []
        for h, (m, l, acc) in enumerate(carry):
            s = jnp.where(valid[h // NSA_GROUP], _dot(k, qts[h]), NEG)
            m_new = jnp.maximum(m, jnp.max(s, axis=0, keepdims=True))
            alpha = jnp.exp2(m - m_new)
            p = jnp.exp2(s - m_new)
            l = alpha * l + jnp.sum(p, axis=0, keepdims=True)
            acc = alpha * acc + _dot(vt, p.astype(BF16))
            out.append((m_new, l, acc))
        return tuple(out)

    init = tuple((jnp.full((1, tq), NEG, F32), jnp.zeros((1, tq), F32), jnp.zeros((LANE, tq), F32))
                 for _ in range(NSA_HEADS))
    return [acc / l for _, l, acc in lax.fori_loop(j_lo, j_hi, body, init)]


def _nsa_prompt_kernel(qct_ref, qrt_ref, kvc_ref, ks_ref, kw_ref, vst_ref, vwt_ref, gt_ref, c2st_ref, et_ref,
                       ot_ref, *, tq, n_sel):
    i = pl.program_id(1)
    t0 = i * tq
    qpos = t0 + lax.broadcasted_iota(jnp.int32, (1, tq), 1)
    n_cmp = kvc_ref.shape[1]
    kc = kvc_ref[0, :, :LANE].astype(BF16)
    vct = kvc_ref[0, :, LANE:].T.astype(BF16)
    cend = lax.broadcasted_iota(jnp.int32, (n_cmp, 1), 0) * CMP_STRIDE + (CMP_BLOCK - 1)
    valid_c = cend <= qpos
    psum = [jnp.zeros((n_cmp, tq), F32) for _ in range(NSA_KV_HEADS)]
    o_cmp = []
    for h in range(NSA_HEADS):
        s = jnp.where(valid_c, _dot(kc, qct_ref[h * LANE:(h + 1) * LANE, :]), NEG)
        p = jnp.where(valid_c, jnp.exp(s - jnp.max(s, axis=0, keepdims=True)), 0.0)
        p = p / jnp.maximum(jnp.sum(p, axis=0, keepdims=True), 1e-30)
        psum[h // NSA_GROUP] = psum[h // NSA_GROUP] + p
        o_cmp.append(_dot(vct, p.astype(BF16)))
    n_blk_rows = -(-n_sel // 8) * 8
    imp_t = jnp.concatenate(
        [jnp.dot(c2st_ref[...], ps, preferred_element_type=F32, precision=lax.Precision.HIGHEST)[:n_blk_rows]
         for ps in psum], axis=1)
    col = lax.broadcasted_iota(jnp.int32, (1, NSA_KV_HEADS * tq), 1)
    sel_t = _select_blocks_t(imp_t, lax.shift_right_logical(t0 + (col & (tq - 1)), 6), n_sel)
    sel_t = jnp.concatenate([sel_t, jnp.zeros((LANE - n_blk_rows, NSA_KV_HEADS * tq), F32)], axis=0).astype(BF16)
    krow = lax.broadcasted_iota(jnp.int32, (KEY_TILE, 1), 0)

    def sel_valid(j):
        causal = (j * KEY_TILE + krow) <= qpos
        return [jnp.where(causal, _dot(et_ref[j], sel_t[:, k * tq:(k + 1) * tq]), 0.0) > 0.5
                for k in range(NSA_KV_HEADS)]

    def win_valid(j):
        rel = qpos - (j * KEY_TILE + krow)
        return [jnp.where(rel >= 0, rel, WINDOW) < WINDOW] * NSA_KV_HEADS

    qrt = [qrt_ref[h * LANE:(h + 1) * LANE, :] for h in range(NSA_HEADS)]
    n_kt = (t0 + tq) // KEY_TILE
    o_sel = _flash_t(qrt, ks_ref.at[0], vst_ref.at[0], 0, n_kt, sel_valid, tq)
    o_win = _flash_t(qrt, kw_ref.at[0], vwt_ref.at[0], jnp.maximum((t0 - WINDOW) // KEY_TILE, 0), n_kt,
                     win_valid, tq)
    for h in range(NSA_HEADS):
        g = gt_ref[3 * h:3 * h + 3, :]
        o = o_cmp[h] * g[0:1] + o_sel[h] * g[1:2] + o_win[h] * g[2:3]
        k = h // NSA_GROUP
        ot_ref[h * NSA_HD:(h + 1) * NSA_HD, :] = o[k * NSA_HD:(k + 1) * NSA_HD]


def nsa_prompt_attn(qct, qrt, kvc, kb, vt, gates_t, n_batch, tq):
    n_rows = qct.shape[1]
    t = n_rows // n_batch
    tq = min(tq, t)
    nt = t // tq
    n_kt = t // KEY_TILE
    n_cmp = kvc.shape[1]
    n_sel = t // SEL_BLOCK
    assert n_cmp % LANE == 0 and n_sel <= LANE and tq & (tq - 1) == 0 and tq % KEY_TILE == 0
    c2st = jnp.asarray(_c2s(n_cmp, LANE).T)
    et = jnp.asarray(_expand(LANE, t).T.reshape(n_kt, KEY_TILE, LANE), dtype=BF16)
    kb3 = kb.reshape(n_batch, t, 2 * LANE)
    vt4 = vt.reshape(2, n_batch, n_kt, LANE, KEY_TILE)
    cols = lambda r: pl.BlockSpec((r, tq), lambda b, i: (0, b * nt + i))
    kspec = lambda c: pl.BlockSpec((1, t, LANE), lambda b, i: (b, 0, c))
    vspec = lambda c: pl.BlockSpec((None, 1, n_kt, LANE, KEY_TILE), lambda b, i: (c, b, 0, 0, 0))
    return pl.pallas_call(
        functools.partial(_nsa_prompt_kernel, tq=tq, n_sel=n_sel), grid=(n_batch, nt),
        in_specs=[cols(NSA_HEADS * LANE), cols(NSA_HEADS * LANE),
                  pl.BlockSpec((1, n_cmp, CMP_W), lambda b, i: (b, 0, 0)),
                  kspec(0), kspec(1), vspec(0), vspec(1), cols(LANE),
                  pl.BlockSpec(c2st.shape, lambda b, i: (0, 0)), pl.BlockSpec(et.shape, lambda b, i: (0, 0, 0))],
        out_specs=cols(NSA_Q_W),
        out_shape=jax.ShapeDtypeStruct((NSA_Q_W, n_rows), F32),
        compiler_params=_cparams(("parallel", "parallel"), 48), name="nsa_prompt_attn",
    )(qct, qrt, kvc, kb3, kb3, vt4, vt4, gates_t, c2st, et)


SROWS = NSA_HEADS * 8
HALF_W = 2 * NSA_KV_HEADS * NSA_HD


def _page_copies(cache_ref, pt_ref, targets, sem, b, slot, n_pages, page):
    return [pltpu.make_async_copy(cache_ref.at[pt_ref[b, pg], kind],
                                  buf.at[slot, :, :, pl.ds(pg * page, page)], sem.at[slot])
            for pg in range(n_pages) for kind, buf in targets]


def _stream_pages(cache_ref, pt_ref, targets, sem, n_pages, page):
    b = pl.program_id(0)
    nb = pl.num_programs(0)
    slot = lax.rem(b, 2)

    @pl.when(b == 0)
    def _():
        for c in _page_copies(cache_ref, pt_ref, targets, sem, 0, 0, n_pages, page):
            c.start()

    @pl.when(b + 1 < nb)
    def _():
        for c in _page_copies(cache_ref, pt_ref, targets, sem, b + 1, 1 - slot, n_pages, page):
            c.start()

    for c in _page_copies(cache_ref, pt_ref, targets, sem, b, slot, n_pages, page):
        c.wait()
    return slot


def _nsa_s1_kernel(pt_ref, cache_ref, q_ref, w_ref, pe_ref, c2s_ref, ocmp_ref, sel_ref, bufk, bufv, xk, xv, sem,
                   *, n_pages, page, ts, n_sel):
    past = n_pages * page
    slot = _stream_pages(cache_ref, pt_ref, [(0, bufk), (1, bufv)], sem, n_pages, page)
    n_chunk = past // CMP_STRIDE
    tch = min(past, 8 * LANE)
    for src, dst in ((bufk, xk), (bufv, xv)):
        for c in range(past // tch):
            dst[c * tch:(c + 1) * tch, :] = src[slot, :, :, c * tch:(c + 1) * tch].reshape(LANE, tch).T
    kv = _compress(lambda j: xk[pl.ds(j, n_chunk, stride=CMP_STRIDE), :],
                   lambda j: xv[pl.ds(j, n_chunk, stride=CMP_STRIDE), :], w_ref, pe_ref, n_chunk)
    kc = kv[:, :LANE].astype(BF16)
    vc = kv[:, LANE:].astype(BF16)
    tok = lax.broadcasted_iota(jnp.int32, (SROWS, 1), 0) & (ts - 1)
    qpos = past + tok
    cend = lax.broadcasted_iota(jnp.int32, (1, n_chunk), 1) * CMP_STRIDE + (CMP_BLOCK - 1)
    p = _cmp_probs(_dot_t(q_ref[0], kc), cend <= qpos)
    ocmp_ref[0] = _dot(p.astype(BF16), vc)
    psum = []
    for k in range(NSA_KV_HEADS):
        acc = p[k * NSA_GROUP * ts:k * NSA_GROUP * ts + ts]
        for g in range(1, NSA_GROUP):
            r = (k * NSA_GROUP + g) * ts
            acc = acc + p[r:r + ts]
        psum.append(acc)
    psum = jnp.concatenate(psum, axis=0)
    imp = jnp.dot(psum, c2s_ref[...], preferred_element_type=F32, precision=lax.Precision.HIGHEST)
    q_blk = lax.shift_right_logical(past + (lax.broadcasted_iota(jnp.int32, (NSA_KV_HEADS * ts, 1), 0) & (ts - 1)), 6)
    sel_ref[0] = _select_blocks(imp, q_blk, n_sel)


def _nsa_s2_kernel(pt_ref, cache_ref, q_ref, sel_ref, new_ref, win_ref, wnew_ref, ocmp_ref, g_ref, e_ref,
                   o_ref, bufk, bufv, sem, *, n_pages, page, ts, wb):
    past = n_pages * page
    slot = _stream_pages(cache_ref, pt_ref, [(2, bufk), (3, bufv)], sem, n_pages, page)
    q = q_ref[0]
    tok = lax.broadcasted_iota(jnp.int32, (SROWS, 1), 0) & (ts - 1)
    sel = sel_ref[0]
    sel_rows = jnp.concatenate([sel[k * ts:(k + 1) * ts] for k in range(NSA_KV_HEADS) for _ in range(NSA_GROUP)],
                               axis=0)
    n_new = new_ref.shape[1]
    jnew = lax.broadcasted_iota(jnp.int32, (1, n_new), 1)

    def attend(kt_past, vt_past, valid_past, k_new, v_new, valid_new):
        s_past = jnp.where(valid_past, _dot(q, kt_past), NEG)
        s_new = jnp.where(valid_new, _dot_t(q, k_new), NEG)
        m = jnp.maximum(jnp.max(s_past, axis=-1, keepdims=True), jnp.max(s_new, axis=-1, keepdims=True))
        p_past = jnp.exp2(s_past - m)
        p_new = jnp.exp2(s_new - m)
        l = jnp.sum(p_past, axis=-1, keepdims=True) + jnp.sum(p_new, axis=-1, keepdims=True)
        return (_dot_t(p_past.astype(BF16), vt_past) + _dot(p_new.astype(BF16), v_new)) / l

    mask_past = _dot(sel_rows.astype(BF16), e_ref[...]) > 0.5
    blk_new = past // SEL_BLOCK
    valid_new = (sel_rows[:, blk_new:blk_new + 1] > 0.5) & (jnew <= tok)
    o_sel = attend(bufk[slot].reshape(LANE, past).astype(BF16), bufv[slot].reshape(LANE, past).astype(BF16),
                   mask_past, new_ref[0, :, 0:LANE].astype(BF16), new_ref[0, :, LANE:HALF_W].astype(BF16), valid_new)
    u = lax.broadcasted_iota(jnp.int32, (1, wb), 1)
    rel = wb + tok - u
    valid_w = (rel >= 0) & (rel < WINDOW) & (past - wb + u >= 0)
    o_win = attend(win_ref[0, 0].reshape(LANE, wb).astype(BF16), win_ref[0, 1].reshape(LANE, wb).astype(BF16),
                   valid_w, wnew_ref[0, :, 0:LANE].astype(BF16), wnew_ref[0, :, LANE:HALF_W].astype(BF16),
                   jnew <= tok)
    gt = g_ref[0]
    o = ocmp_ref[0] * gt[:, 0:1] + o_sel * gt[:, 1:2] + o_win * gt[:, 2:3]
    half = SROWS // NSA_KV_HEADS
    for k in range(NSA_KV_HEADS):
        o_ref[0, k * half:(k + 1) * half, :] = o[k * half:(k + 1) * half, k * NSA_HD:(k + 1) * NSA_HD]


def nsa_sample_attn(page_table, cache, qc2, qr2, new_sel, win, new_win, gates, cmp_w, cmp_pe, ts):
    bs, n_pages = page_table.shape
    page = cache.shape[-1]
    past = n_pages * page
    wb = win.shape[-1]
    assert past % SEL_BLOCK == 0 and ts <= CMP_STRIDE and ts & (ts - 1) == 0 and SROWS == NSA_HEADS * ts
    n_chunk = past // CMP_STRIDE
    n_sel = past // SEL_BLOCK + 1
    n_sel_pad = -(-n_sel // LANE) * LANE
    c2s = jnp.asarray(_c2s(n_chunk, n_sel_pad))
    e = jnp.asarray(_expand(n_sel_pad, past), dtype=BF16)
    sems = pltpu.SemaphoreType.DMA((2,))
    pages = pltpu.VMEM((2, NSA_KV_HEADS, NSA_HD, past), F32)
    scratch1 = [pages, pages, pltpu.VMEM((past, LANE), F32), pltpu.VMEM((past, LANE), F32), sems]
    scratch2 = [pages, pages, sems]
    seq3 = lambda b, pt: (b, 0, 0)
    o_cmp, sel = pl.pallas_call(
        functools.partial(_nsa_s1_kernel, n_pages=n_pages, page=page, ts=ts, n_sel=n_sel),
        grid_spec=pltpu.PrefetchScalarGridSpec(
            num_scalar_prefetch=1, grid=(bs,),
            in_specs=[pl.BlockSpec(memory_space=pl.ANY),
                      pl.BlockSpec((1, SROWS, LANE), seq3),
                      pl.BlockSpec(cmp_w.shape, lambda b, pt: (0, 0, 0, 0)),
                      pl.BlockSpec(cmp_pe.shape, lambda b, pt: (0, 0)),
                      pl.BlockSpec(c2s.shape, lambda b, pt: (0, 0))],
            out_specs=[pl.BlockSpec((1, SROWS, LANE), seq3),
                       pl.BlockSpec((1, NSA_KV_HEADS * ts, n_sel_pad), seq3)],
            scratch_shapes=scratch1),
        out_shape=[jax.ShapeDtypeStruct((bs, SROWS, LANE), F32),
                   jax.ShapeDtypeStruct((bs, NSA_KV_HEADS * ts, n_sel_pad), F32)],
        compiler_params=_cparams(("arbitrary",), 48), name="nsa_sample_cmp",
    )(page_table, cache, qc2, cmp_w, cmp_pe, c2s)
    n_new = new_sel.shape[1]
    return pl.pallas_call(
        functools.partial(_nsa_s2_kernel, n_pages=n_pages, page=page, ts=ts, wb=wb),
        grid_spec=pltpu.PrefetchScalarGridSpec(
            num_scalar_prefetch=1, grid=(bs,),
            in_specs=[pl.BlockSpec(memory_space=pl.ANY),
                      pl.BlockSpec((1, SROWS, LANE), seq3),
                      pl.BlockSpec((1, NSA_KV_HEADS * ts, n_sel_pad), seq3),
                      pl.BlockSpec((1, n_new, HALF_W), seq3),
                      pl.BlockSpec((1, 2, NSA_KV_HEADS, NSA_HD, wb), lambda b, pt: (b, 0, 0, 0, 0)),
                      pl.BlockSpec((1, n_new, HALF_W), seq3),
                      pl.BlockSpec((1, SROWS, LANE), seq3),
                      pl.BlockSpec((1, SROWS, 3), seq3),
                      pl.BlockSpec(e.shape, lambda b, pt: (0, 0))],
            out_specs=pl.BlockSpec((1, SROWS, NSA_HD), seq3),
            scratch_shapes=scratch2),
        out_shape=jax.ShapeDtypeStruct((bs, SROWS, NSA_HD), F32),
        compiler_params=_cparams(("arbitrary",), 56), name="nsa_sample_sel",
    )(page_table, cache, qr2, sel, new_sel, win, new_win, o_cmp, gates, e)


def _in_proj(x2d, pos_rows, p, tm, dims_major):
    q_pre, kv_pre, gate_pre, qkv_pre, z_pre, ba_pre = rms_matmul(x2d, p["attn_norm_w"], p["w_in_parts"], tm)
    cos_t, sin_t = _rope_tables(pos_rows)
    nsa = nsa_prep_v2(q_pre, kv_pre, gate_pre, cos_t, sin_t, p["nsa_q_norm_w"], p["nsa_k_norm_w"], tm, dims_major)
    return nsa, qkv_pre, z_pre, ba_pre


def _tokens_major(a):
    return a.transpose(0, 4, 1, 2, 3)


def _post_mixer(x2d, o_nsa, o_gdn, mem_kv, p, n_seq, tq):
    h = matmul_res([o_nsa, o_gdn], p["w_out"], x2d, 512)
    (qx,) = rms_matmul(h, p["xattn_norm_w"], [p["w_xq"]], 512)
    return matmul_res([xattn_core(qx, mem_kv, p["xq_norm_w"], n_seq, tq)], p["w_xo"], h, 512)


def _ffn(h, p, prev, n_seq, shift, tm):
    return conv_ffn(h, p["ffn_norm_w"], p["w_up"], p["ffn_conv_w"], p["ffn_conv_b"], p["w_down"], prev,
                    n_seq, shift, tm)


def _layer_prompt(x, mem, p):
    nb, t, d = x.shape
    x2d = x.reshape(nb * t, d)
    (rows_t, wrows_t, qc, qr, gates, kvb, cmp_rows), qkv_pre, z_pre, ba_pre = _in_proj(x2d, jnp.arange(t), p, 512, True)
    kvc = compress_prompt(cmp_rows.reshape(nb, t, CMP_W), p["cmp_bd"], p["cmp_pe2"])
    o_nsa = nsa_prompt_attn_v2(qc, qr, kvc, kvb, gates, nb, 256)
    o_gdn, gdn_buf, gdn_s = gdn2(qkv_pre, z_pre, ba_pre, jnp.zeros((nb, GDN_CONV - 1, GDN_C3), F32),
                                 jnp.zeros((nb, GDN_HEADS, GDN_HD, GDN_HD), F32), p["gdn_conv_w"], p["gdn_a_log"],
                                 p["gdn_dt_bias"], p["gdn_norm_w"], nb, 4)
    mem_kv = mem_kv_proj(mem.reshape(-1, d), p["mem_norm_w"], p["w_xk"], p["w_xv"], p["xk_norm_w"])
    mem_kv = mem_kv.reshape(nb, mem.shape[1], 2 * XA_HEADS * XA_HD)
    h = _post_mixer(x2d, o_nsa, o_gdn, mem_kv, p, nb, 256)
    y, ffn_buf = _ffn(h, p, jnp.zeros((nb, FFN_CONV - 1, 2 * D_FF), F32), nb, 1, 512)
    keep = min(WINDOW, t)
    return (y.reshape(nb, t, d), _tokens_major(rows_t), _tokens_major(wrows_t[..., t - keep:]), gdn_s, gdn_buf,
            ffn_buf, mem_kv.reshape(nb, mem.shape[1], 2, XA_HEADS, XA_HD))


def _layer_sample(x, cache_kv, cache_win, state, gdn_cache, ffn_cache, mem_kv, page_table, p):
    bs, ts, d = x.shape
    n_pool, page = cache_kv.shape[:2]
    past = page_table.shape[1] * page
    x2d = x.reshape(bs * ts, d)
    pos_rows = jnp.tile(past + jnp.arange(ts), bs)
    (rows, wrows, qc, qr, gates), qkv_pre, z_pre, ba_pre = _in_proj(x2d, pos_rows, p, bs * ts, False)
    dims_major = lambda a: a.transpose(0, 2, 3, 4, 1)

    def stack_q(a):
        return a.reshape(bs, ts, NSA_HEADS, LANE).transpose(0, 2, 1, 3).reshape(bs, SROWS, LANE)

    def pad_new(a):
        return jnp.pad(a.reshape(bs, ts, HALF_W), ((0, 0), (0, LANE - ts), (0, 0)))

    wb = cache_win.shape[1]
    gates3 = gates[:, :NSA_GATE_W].reshape(bs, ts, NSA_HEADS, 3).transpose(0, 2, 1, 3).reshape(bs, SROWS, 3)
    win_t = dims_major(cache_win)
    o2 = nsa_sample_attn(page_table, dims_major(cache_kv), stack_q(qc), stack_q(qr),
                         pad_new(rows[:, HALF_W:]), win_t, pad_new(wrows), gates3, p["cmp_bd"], p["cmp_pe2"], ts)
    rows = rows.reshape(bs, ts, 4, NSA_KV_HEADS, NSA_HD)
    wrows = wrows.reshape(bs, ts, 2, NSA_KV_HEADS, NSA_HD)
    o_nsa = o2.reshape(bs, NSA_HEADS, ts, NSA_HD).transpose(0, 2, 1, 3).reshape(bs * ts, NSA_Q_W)
    o_gdn, gdn_buf, gdn_s = gdn2(qkv_pre, z_pre, ba_pre, gdn_cache, state, p["gdn_conv_w"], p["gdn_a_log"],
                                 p["gdn_dt_bias"], p["gdn_norm_w"], bs, 4)
    h = _post_mixer(x2d, o_nsa, o_gdn, mem_kv.reshape(bs, mem_kv.shape[1], -1), p, bs, ts)
    h_tm = h.reshape(bs, ts, d).transpose(1, 0, 2).reshape(ts * bs, d)
    prev = ffn_cache.transpose(1, 0, 2).reshape(1, (FFN_CONV - 1) * bs, 2 * D_FF)
    y, ffn_buf = _ffn(h_tm, p, prev, 1, bs, ts * bs)
    y = y.reshape(ts, bs, d).transpose(1, 0, 2)
    ffn_buf = ffn_buf.reshape(FFN_CONV - 1, bs, 2 * D_FF).transpose(1, 0, 2)
    wall_t = jnp.concatenate([win_t, dims_major(wrows)], axis=-1)
    keep = min(WINDOW, wb + ts)
    return y, rows, _tokens_major(wall_t[..., wb + ts - keep:]), gdn_s, gdn_buf, ffn_buf


def _prep_params(l, attn_norm_w, w_in, nsa_q_norm_w, nsa_k_norm_w, cmp_pe, cmp_w, gdn_conv_w, gdn_a_log,
                 gdn_dt_bias, gdn_norm_w, w_out, mem_norm_w, w_xk, w_xv, xk_norm_w, xattn_norm_w, w_xq,
                 xq_norm_w, w_xo, ffn_norm_w, w_up, ffn_conv_w, ffn_conv_b, w_down):
    wi = w_in[l]
    cuts = np.cumsum([0, NSA_Q_W, NSA_KV_W, NSA_GATE_W, 3 * GDN_W, GDN_W, 2 * GDN_HEADS])
    parts = []
    for a, b in zip(cuts[:-1], cuts[1:]):
        w = wi[:, a:b]
        padn = -(-(b - a) // LANE) * LANE - (b - a)
        parts.append(jnp.pad(w, ((0, 0), (0, padn))).astype(BF16))
    cmp_bd, cmp_pe2 = _cmp_weights(cmp_pe[l], cmp_w[l])
    bf = lambda w: w[l].astype(BF16)
    return dict(attn_norm_w=attn_norm_w[l], w_in_parts=parts, nsa_q_norm_w=nsa_q_norm_w[l],
                nsa_k_norm_w=nsa_k_norm_w[l], cmp_bd=cmp_bd, cmp_pe2=cmp_pe2, gdn_conv_w=gdn_conv_w[l],
                gdn_a_log=gdn_a_log[l], gdn_dt_bias=gdn_dt_bias[l], gdn_norm_w=gdn_norm_w[l], w_out=bf(w_out),
                mem_norm_w=mem_norm_w[l], w_xk=bf(w_xk), w_xv=bf(w_xv), xk_norm_w=xk_norm_w[l],
                xattn_norm_w=xattn_norm_w[l], w_xq=bf(w_xq), xq_norm_w=xq_norm_w[l], w_xo=bf(w_xo),
                ffn_norm_w=ffn_norm_w[l], w_up=bf(w_up), ffn_conv_w=ffn_conv_w[l], ffn_conv_b=ffn_conv_b[l],
                w_down=bf(w_down))


def kernel(x_prompt, x_sample, mem_prompt, cache_nsa_kv, cache_nsa_win, state_gdn, cache_gdn_conv, cache_ffn_conv,
           cache_mem_kv, page_table, attn_norm_w, w_in, nsa_q_norm_w, nsa_k_norm_w, cmp_pe, cmp_w, gdn_conv_w,
           gdn_a_log, gdn_dt_bias, gdn_norm_w, w_out, mem_norm_w, w_xk, w_xv, xk_norm_w, xattn_norm_w, w_xq,
           xq_norm_w, w_xo, ffn_norm_w, w_up, ffn_conv_w, ffn_conv_b, w_down):
    weights = (attn_norm_w, w_in, nsa_q_norm_w, nsa_k_norm_w, cmp_pe, cmp_w, gdn_conv_w, gdn_a_log, gdn_dt_bias,
               gdn_norm_w, w_out, mem_norm_w, w_xk, w_xv, xk_norm_w, xattn_norm_w, w_xq, xq_norm_w, w_xo,
               ffn_norm_w, w_up, ffn_conv_w, ffn_conv_b, w_down)
    depth = cache_nsa_kv.shape[0]
    hp, hs = x_prompt, x_sample
    outs_p, outs_s = [], []
    for l in range(depth):
        p = _prep_params(l, *weights)
        res_p = _layer_prompt(hp, mem_prompt, p)
        hp = res_p[0]
        outs_p.append(res_p[1:])
        res_s = _layer_sample(hs, cache_nsa_kv[l], cache_nsa_win[l], state_gdn[l], cache_gdn_conv[l],
                              cache_ffn_conv[l], cache_mem_kv[l], page_table, p)
        hs = res_s[0]
        outs_s.append(res_s[1:])
    stack = lambda outs, i: jnp.stack([o[i] for o in outs])
    return ((hp, hs) + tuple(stack(outs_p, i) for i in range(6)) + tuple(stack(outs_s, i) for i in range(5)))
```

```python
import functools
import math

import numpy as np
import jax
import jax.numpy as jnp
from jax import lax
from jax.experimental import pallas as pl
from jax.experimental.pallas import tpu as pltpu

F32 = jnp.float32
BF16 = jnp.bfloat16

D_MODEL = 1024
NSA_HEADS = 8
NSA_KV_HEADS = 2
NSA_GROUP = NSA_HEADS // NSA_KV_HEADS
NSA_HD = 64
CMP_BLOCK = 32
CMP_STRIDE = 16
SEL_BLOCK = 64
SEL_TOPN = 8
WINDOW = 512
FORCED_BONUS = 1e4
GDN_HEADS = 4
GDN_HD = 128
GDN_CONV = 4
GDN_CHUNK = 64
GDN_W = GDN_HEADS * GDN_HD
XA_HEADS = 4
XA_HD = 256
D_FF = 2816
FFN_CONV = 3
ROPE_THETA = 10000.0
EPS = 1e-6
NEG = -1e30

NSA_Q_W = NSA_HEADS * NSA_HD
NSA_KV_W = 3 * 2 * NSA_KV_HEADS * NSA_HD
NSA_GATE_W = 3 * NSA_HEADS
LANE = 128
MIB = 1 << 20


def _cparams(sem, vmem_mib):
    return pltpu.CompilerParams(dimension_semantics=sem, vmem_limit_bytes=vmem_mib * MIB)


def _sigmoid(x):
    return 1.0 / (1.0 + jnp.exp(-x))


def _dot(a, b):
    return jnp.dot(a, b, preferred_element_type=F32)


def _dot_t(a, b):
    return lax.dot_general(a, b, (((1,), (1,)), ((), ())), preferred_element_type=F32)


def _split_bf16(a):
    hi = a.astype(BF16)
    lo = (a - hi.astype(F32)).astype(BF16)
    return hi, lo


def _dot3(a, b):
    ah, al = _split_bf16(a)
    bh, bl = _split_bf16(b)
    return _dot(ah, bh) + (_dot(ah, bl) + _dot(al, bh))


def _col_chunk(n):
    for c in (512, 256, 128):
        if n % c == 0:
            return c
    return n


def _rms_mm_kernel(x_ref, g_ref, *refs, n_out):
    w_refs, o_refs = refs[:n_out], refs[n_out:]
    x = x_ref[...]
    xn = (x * lax.rsqrt(jnp.mean(x * x, axis=-1, keepdims=True) + EPS) * g_ref[...]).astype(BF16)
    for w_ref, o_ref in zip(w_refs, o_refs):
        n = w_ref.shape[1]
        ch = _col_chunk(n)
        for c in range(0, n, ch):
            o_ref[:, c:c + ch] = _dot(xn, w_ref[:, c:c + ch])


def rms_matmul(x, g, ws, tm):
    n_rows, d = x.shape
    tm = min(tm, n_rows)
    in_specs = [pl.BlockSpec((tm, d), lambda i: (i, 0)), pl.BlockSpec((1, d), lambda i: (0, 0))]
    in_specs += [pl.BlockSpec(w.shape, lambda i: (0, 0)) for w in ws]
    out_specs = [pl.BlockSpec((tm, w.shape[1]), lambda i: (i, 0)) for w in ws]
    out_shape = [jax.ShapeDtypeStruct((n_rows, w.shape[1]), F32) for w in ws]
    return pl.pallas_call(
        functools.partial(_rms_mm_kernel, n_out=len(ws)),
        grid=(n_rows // tm,), in_specs=in_specs, out_specs=out_specs, out_shape=out_shape,
        compiler_params=_cparams(("parallel",), 56), name="rms_matmul",
    )(x, g.reshape(1, d), *ws)


def _mm_res_kernel(*refs, n_in):
    a_refs, (w_ref, r_ref, o_ref) = refs[:n_in], refs[n_in:]
    a = [a_ref[...].astype(BF16) for a_ref in a_refs]
    n = w_ref.shape[1]
    ch = _col_chunk(n)
    for c in range(0, n, ch):
        acc = r_ref[:, c:c + ch]
        k0 = 0
        for x in a:
            acc = acc + _dot(x, w_ref[k0:k0 + x.shape[1], c:c + ch])
            k0 += x.shape[1]
        o_ref[:, c:c + ch] = acc


def matmul_res(a_list, w, res, tm):
    n_rows = res.shape[0]
    k, n = w.shape
    assert sum(a.shape[1] for a in a_list) == k
    tm = min(tm, n_rows)
    return pl.pallas_call(
        functools.partial(_mm_res_kernel, n_in=len(a_list)), grid=(n_rows // tm,),
        in_specs=[pl.BlockSpec((tm, a.shape[1]), lambda i: (i, 0)) for a in a_list]
        + [pl.BlockSpec((k, n), lambda i: (0, 0)), pl.BlockSpec((tm, n), lambda i: (i, 0))],
        out_specs=pl.BlockSpec((tm, n), lambda i: (i, 0)),
        out_shape=jax.ShapeDtypeStruct((n_rows, n), F32),
        compiler_params=_cparams(("parallel",), 40), name="matmul_res",
    )(*a_list, w, res)


def _memkv_kernel(x_ref, g_ref, wk_ref, wv_ref, kn_ref, o_ref, ob_ref):
    x = x_ref[...]
    xn = (x * lax.rsqrt(jnp.mean(x * x, axis=-1, keepdims=True) + EPS) * g_ref[...]).astype(BF16)
    xa_w = XA_HEADS * XA_HD
    for h in range(XA_HEADS):
        sl = slice(h * XA_HD, (h + 1) * XA_HD)
        vl = slice(xa_w + h * XA_HD, xa_w + (h + 1) * XA_HD)
        k = _dot(xn, wk_ref[:, sl])
        k = k * lax.rsqrt(jnp.mean(k * k, axis=-1, keepdims=True) + EPS) * kn_ref[...]
        v = _dot(xn, wv_ref[:, sl])
        o_ref[:, sl] = k
        o_ref[:, vl] = v
        ob_ref[:, sl] = k.astype(BF16)
        ob_ref[:, vl] = v.astype(BF16)


def mem_kv_proj(mem, g, wk, wv, kn, tm=256):
    n_rows, d = mem.shape
    xa_w = XA_HEADS * XA_HD
    out = pl.BlockSpec((tm, 2 * xa_w), lambda i: (i, 0))
    return pl.pallas_call(
        _memkv_kernel, grid=(n_rows // tm,),
        in_specs=[pl.BlockSpec((tm, d), lambda i: (i, 0)), pl.BlockSpec((1, d), lambda i: (0, 0)),
                  pl.BlockSpec((d, xa_w), lambda i: (0, 0)), pl.BlockSpec((d, xa_w), lambda i: (0, 0)),
                  pl.BlockSpec((1, XA_HD), lambda i: (0, 0))],
        out_specs=[out, out],
        out_shape=[jax.ShapeDtypeStruct((n_rows, 2 * xa_w), F32), jax.ShapeDtypeStruct((n_rows, 2 * xa_w), BF16)],
        compiler_params=_cparams(("parallel",), 40), name="mem_kv_proj",
    )(mem, g.reshape(1, d), wk, wv, kn.reshape(1, XA_HD))


XA_W = XA_HEADS * XA_HD
MEM_ROWS = 2 * XA_W // LANE


def _xattn_head(q, qn, k, v):
    q = q * lax.rsqrt(jnp.mean(q * q, axis=-1, keepdims=True) + EPS) * qn
    s = _dot_t((q * (XA_HD ** -0.5)).astype(BF16), k)
    p = jnp.exp(s - jnp.max(s, axis=-1, keepdims=True))
    return _dot(p.astype(BF16), v) / jnp.sum(p, axis=-1, keepdims=True)


def _xattn_cache_kernel(q_ref, kv_ref, qn_ref, o_ref, *, m):
    def head_block(kind, h):
        halves = [kv_ref[0, pl.ds(kind * (MEM_ROWS // 2) + half * XA_HEADS + h, m, stride=MEM_ROWS), :]
                  for half in range(XA_HD // LANE)]
        return jnp.concatenate(halves, axis=1).astype(BF16)

    for h in range(XA_HEADS):
        sl = slice(h * XA_HD, (h + 1) * XA_HD)
        o_ref[:, sl] = _xattn_head(q_ref[:, sl], qn_ref[...], head_block(0, h), head_block(1, h))


def xattn_cache(q_pre, kv_rows, qn, n_batch):
    n_rows = q_pre.shape[0]
    tq = n_rows // n_batch
    m = kv_rows.shape[1] // MEM_ROWS
    return pl.pallas_call(
        functools.partial(_xattn_cache_kernel, m=m), grid=(n_batch,),
        in_specs=[pl.BlockSpec((tq, XA_W), lambda b: (b, 0)),
                  pl.BlockSpec((1, m * MEM_ROWS, LANE), lambda b: (b, 0, 0)),
                  pl.BlockSpec((1, XA_HD), lambda b: (0, 0))],
        out_specs=pl.BlockSpec((tq, XA_W), lambda b: (b, 0)),
        out_shape=jax.ShapeDtypeStruct((n_rows, XA_W), F32),
        compiler_params=_cparams(("parallel",), 40), name="xattn_cache",
    )(q_pre, kv_rows, qn.reshape(1, XA_HD))


def _xattn_fused_kernel(h_ref, g_ref, wq_ref, qn_ref, kv_ref, wo_ref, o_ref):
    x = h_ref[...]
    xn = (x * lax.rsqrt(jnp.mean(x * x, axis=-1, keepdims=True) + EPS) * g_ref[...]).astype(BF16)
    heads = []
    for h in range(XA_HEADS):
        sl = slice(h * XA_HD, (h + 1) * XA_HD)
        o = _xattn_head(_dot(xn, wq_ref[:, sl]), qn_ref[...], kv_ref[0, :, sl],
                        kv_ref[0, :, XA_W + h * XA_HD:XA_W + (h + 1) * XA_HD])
        heads.append(o.astype(BF16))
    a = jnp.concatenate(heads, axis=1)
    ch = _col_chunk(D_MODEL)
    for c in range(0, D_MODEL, ch):
        o_ref[:, c:c + ch] = x[:, c:c + ch] + _dot(a, wo_ref[:, c:c + ch])


def xattn_fused(h, g, wq, qn, kv_bf, wo, n_batch, tq):
    n_rows, d = h.shape
    nt = n_rows // n_batch // tq
    m = kv_bf.shape[1]
    const = lambda b, i: (0, 0)
    return pl.pallas_call(
        _xattn_fused_kernel, grid=(n_batch, nt),
        in_specs=[pl.BlockSpec((tq, d), lambda b, i: (b * nt + i, 0)), pl.BlockSpec((1, d), const),
                  pl.BlockSpec((d, XA_W), const), pl.BlockSpec((1, XA_HD), const),
                  pl.BlockSpec((1, m, 2 * XA_W), lambda b, i: (b, 0, 0)), pl.BlockSpec((XA_W, d), const)],
        out_specs=pl.BlockSpec((tq, d), lambda b, i: (b * nt + i, 0)),
        out_shape=jax.ShapeDtypeStruct((n_rows, d), F32),
        compiler_params=_cparams(("parallel", "parallel"), 40), name="xattn_fused",
    )(h, g.reshape(1, d), wq, qn.reshape(1, XA_HD), kv_bf, wo)


FFN_ACT_CHUNK = 256


def _ffn_kernel(h_ref, g_ref, wup_ref, cw_ref, cb_ref, wdn_ref, prev_ref, o_ref, buf_ref, xs_scr,
                *, tm, shift, base):
    t = pl.program_id(1)
    p0 = base - 2 * shift

    @pl.when(t == 0)
    def _():
        xs_scr[p0:base, :] = prev_ref[0]

    x = h_ref[...]
    xn = (x * lax.rsqrt(jnp.mean(x * x, axis=-1, keepdims=True) + EPS) * g_ref[...]).astype(BF16)
    acc = jnp.zeros((tm, D_MODEL), F32)
    for j in range(D_FF // FFN_ACT_CHUNK):
        halves = []
        for c0 in (j * FFN_ACT_CHUNK, D_FF + j * FFN_ACT_CHUNK):
            sl = slice(c0, c0 + FFN_ACT_CHUNK)
            xs_scr[base:base + tm, sl] = _dot(xn, wup_ref[:, sl])
            y = (cw_ref[0:1, sl] * xs_scr[p0:p0 + tm, sl]
                 + cw_ref[1:2, sl] * xs_scr[p0 + shift:p0 + shift + tm, sl]
                 + cw_ref[2:3, sl] * xs_scr[base:base + tm, sl])
            halves.append(y + cb_ref[:, sl])
        a, u = halves
        act = (a * _sigmoid(a) * u).astype(BF16)
        acc = acc + _dot(act, wdn_ref[j * FFN_ACT_CHUNK:(j + 1) * FFN_ACT_CHUNK, :])
    o_ref[...] = x + acc
    last = xs_scr[base + tm - 2 * shift: base + tm, :]
    buf_ref[0] = last
    xs_scr[p0:base, :] = last


def conv_ffn(h, g, wup, cw, cb, wdn, prev, n_seq, shift, tm):
    n_rows, d = h.shape
    t_rows = n_rows // n_seq
    tm = min(tm, t_rows)
    nt = t_rows // tm
    base = -(-2 * shift // 8) * 8
    kern = functools.partial(_ffn_kernel, tm=tm, shift=shift, base=base)
    const = lambda b, i: (0, 0)
    return pl.pallas_call(
        kern, grid=(n_seq, nt),
        in_specs=[pl.BlockSpec((tm, d), lambda b, i: (b * nt + i, 0)),
                  pl.BlockSpec((1, d), const),
                  pl.BlockSpec((d, 2 * D_FF), const, pipeline_mode=pl.Buffered(1)),
                  pl.BlockSpec((FFN_CONV, 2 * D_FF), const),
                  pl.BlockSpec((1, 2 * D_FF), const),
                  pl.BlockSpec((D_FF, d), const, pipeline_mode=pl.Buffered(1)),
                  pl.BlockSpec((1, 2 * shift, 2 * D_FF), lambda b, i: (b, 0, 0))],
        out_specs=[pl.BlockSpec((tm, d), lambda b, i: (b * nt + i, 0)),
                   pl.BlockSpec((1, 2 * shift, 2 * D_FF), lambda b, i: (b, 0, 0))],
        out_shape=[jax.ShapeDtypeStruct((n_rows, d), F32),
                   jax.ShapeDtypeStruct((n_seq, 2 * shift, 2 * D_FF), F32)],
        scratch_shapes=[pltpu.VMEM((base + tm, 2 * D_FF), F32)],
        compiler_params=_cparams(("arbitrary", "arbitrary"), 56), name="conv_ffn",
    )(h, g.reshape(1, d), wup, cw, cb.reshape(1, 2 * D_FF), wdn, prev)


GDN_STACK = GDN_HEADS * GDN_CHUNK
GDN_C3 = 3 * GDN_W
GDN_PREV0 = 8 - (GDN_CONV - 1)


def _gdn_masks():
    r = np.arange(GDN_STACK)[:, None]
    c = np.arange(GDN_STACK)[None, :]
    same = lambda n: (r // n) == (c // n)
    m = [same(2) & (c < r)]
    for n in (4, 8, 16, 32, 64):
        m.append(same(n) & ~same(n // 2) & (c < r))
    m.append(same(GDN_CHUNK) & (c <= r))
    m.append(same(GDN_CHUNK) & (c < r))
    m.append(r == c)
    return np.stack(m).astype(np.float32)


def _unit_lower_inverse(a, m_ref):
    t = m_ref[8] - a * m_ref[0]
    for lvl in range(1, 6):
        off = a * m_ref[lvl]
        t = t - _dot3(t, _dot3(off, t))
    return t


def _gdn_kernel_v1(qkv_ref, z_ref, ba_ref, prev_ref, s0_ref, cw_ref, par_ref, nw_ref, m_ref,
                   o_ref, buf_ref, s_ref, xs_scr, s_scr, *, tt):
    t = pl.program_id(1)
    nt = pl.num_programs(1)
    row0 = 8

    @pl.when(t == 0)
    def _():
        xs_scr[GDN_PREV0:row0, :] = prev_ref[0]
        s_scr[...] = s0_ref[0]

    xs_scr[row0:row0 + tt, :] = qkv_ref[...]
    causal = m_ref[6]
    neg_ea = -jnp.exp(par_ref[0:1, :])
    dtb = par_ref[1:2, :]
    rv = min(GDN_CHUNK, tt)
    pad = GDN_CHUNK - rv

    def padrows(a):
        if pad == 0:
            return a
        return jnp.concatenate([a, jnp.zeros((pad, a.shape[1]), a.dtype)], axis=0)

    for n in range(-(-tt // GDN_CHUNK)):
        r0 = n * GDN_CHUNK
        y = cw_ref[0:1, :] * xs_scr[GDN_PREV0 + r0:GDN_PREV0 + r0 + rv, :]
        for i in range(1, GDN_CONV):
            y = y + cw_ref[i:i + 1, :] * xs_scr[GDN_PREV0 + i + r0:GDN_PREV0 + i + r0 + rv, :]
        y = y * _sigmoid(y)
        qs, ks, vs, betas, gs = [], [], [], [], []
        for h in range(GDN_HEADS):
            q = y[:, h * GDN_HD:(h + 1) * GDN_HD]
            k = y[:, GDN_W + h * GDN_HD:GDN_W + (h + 1) * GDN_HD]
            v = y[:, 2 * GDN_W + h * GDN_HD:2 * GDN_W + (h + 1) * GDN_HD]
            q = q * lax.rsqrt(jnp.sum(q * q, axis=-1, keepdims=True) + EPS) * (GDN_HD ** -0.5)
            k = k * lax.rsqrt(jnp.sum(k * k, axis=-1, keepdims=True) + EPS)
            bcol = ba_ref[r0:r0 + rv, h:h + 1]
            acol = ba_ref[r0:r0 + rv, GDN_HEADS + h:GDN_HEADS + h + 1] + dtb[:, h:h + 1]
            softplus = jnp.maximum(acol, 0.0) + jnp.log1p(jnp.exp(-jnp.abs(acol)))
            qs.append(padrows(q)); ks.append(padrows(k)); vs.append(padrows(v))
            betas.append(padrows(_sigmoid(bcol)))
            gs.append(padrows(neg_ea[:, h:h + 1] * softplus))
        qm = jnp.concatenate(qs, axis=0)
        km = jnp.concatenate(ks, axis=0)
        vm = jnp.concatenate(vs, axis=0)
        beta = jnp.concatenate(betas, axis=0)
        g = jnp.concatenate(gs, axis=0)
        gc = jnp.dot(causal, jnp.broadcast_to(g, (GDN_STACK, LANE)), preferred_element_type=F32,
                     precision=lax.Precision.HIGHEST)[:, 0:1]
        gc_row = jnp.sum(m_ref[8] * gc, axis=0, keepdims=True)
        diff = gc - gc_row
        decay = jnp.exp(jnp.where(causal > 0, diff, 0.0)) * causal
        kb = km * beta
        kmb = km.astype(BF16)
        a_mat = _dot_t(kb.astype(BF16), kmb) * decay * m_ref[7]
        qk = _dot_t(qm.astype(BF16), kmb) * decay
        tinv = _unit_lower_inverse(a_mat, m_ref)
        egc = jnp.exp(gc)
        sol = _dot3(tinv, jnp.concatenate([vm * beta, kb * egc], axis=1))
        u_all, w_all = sol[:, :GDN_HD], sol[:, GDN_HD:]
        qd = qm * egc
        v_news, o_inter, kds, gls = [], [], [], []
        for h in range(GDN_HEADS):
            rs = slice(h * GDN_CHUNK, (h + 1) * GDN_CHUNK)
            sb = s_scr[h].astype(BF16)
            v_new = u_all[rs] - _dot(w_all[rs].astype(BF16), sb)
            o_inter.append(_dot(qd[rs].astype(BF16), sb))
            gl = gc[h * GDN_CHUNK + GDN_CHUNK - 1:(h + 1) * GDN_CHUNK, :]
            kds.append(km[rs] * jnp.exp(gl - gc[rs]))
            v_news.append(v_new)
            gls.append(gl)
        v_stack = jnp.concatenate(v_news, axis=0).astype(BF16)
        o_intra = _dot(qk.astype(BF16), v_stack)
        for h in range(GDN_HEADS):
            rs = slice(h * GDN_CHUNK, (h + 1) * GDN_CHUNK)
            s_scr[h] = s_scr[h] * jnp.exp(gls[h]) + _dot(kds[h].T.astype(BF16), v_stack[rs])
            o = (o_inter[h] + o_intra[rs])[:rv]
            on = o * lax.rsqrt(jnp.mean(o * o, axis=-1, keepdims=True) + EPS) * nw_ref[...]
            zz = z_ref[r0:r0 + rv, h * GDN_HD:(h + 1) * GDN_HD]
            o_ref[r0:r0 + rv, h * GDN_HD:(h + 1) * GDN_HD] = on * (zz * _sigmoid(zz))

    last = xs_scr[row0 + tt - (GDN_CONV - 1):row0 + tt, :]
    buf_ref[0] = last
    xs_scr[GDN_PREV0:row0, :] = last

    @pl.when(t == nt - 1)
    def _():
        s_ref[0] = s_scr[...]


def _gdn_kernel(qkv_ref, z_ref, ba_ref, prev_ref, s0_ref, cw_ref, par_ref, nw_ref, m_ref,
                o_ref, buf_ref, s_ref, xs_scr, s_scr, *, tt):
    t = pl.program_id(1)
    nt = pl.num_programs(1)
    row0 = 8

    @pl.when(t == 0)
    def _():
        xs_scr[GDN_PREV0:row0, :] = prev_ref[0]
        s_scr[...] = s0_ref[0]

    xs_scr[row0:row0 + tt, :] = qkv_ref[...]
    causal, strict, eye = m_ref[6], m_ref[7], m_ref[8]
    neg_ea = -jnp.exp(par_ref[0:1, :])
    dtb = par_ref[1:2, :]
    rv = min(GDN_CHUNK, tt)
    pad = GDN_CHUNK - rv
    n_ch = -(-tt // GDN_CHUNK)

    def padrows(a):
        if pad == 0:
            return a
        return jnp.concatenate([a, jnp.zeros((pad, a.shape[1]), a.dtype)], axis=0)

    qms, kms, vms, betas, gcols = [], [], [], [], []
    for n in range(n_ch):
        r0 = n * GDN_CHUNK
        y = cw_ref[0:1, :] * xs_scr[GDN_PREV0 + r0:GDN_PREV0 + r0 + rv, :]
        for i in range(1, GDN_CONV):
            y = y + cw_ref[i:i + 1, :] * xs_scr[GDN_PREV0 + i + r0:GDN_PREV0 + i + r0 + rv, :]
        y = y * _sigmoid(y)
        qs, ks, vs, bs, gs = [], [], [], [], []
        for h in range(GDN_HEADS):
            q = y[:, h * GDN_HD:(h + 1) * GDN_HD]
            k = y[:, GDN_W + h * GDN_HD:GDN_W + (h + 1) * GDN_HD]
            v = y[:, 2 * GDN_W + h * GDN_HD:2 * GDN_W + (h + 1) * GDN_HD]
            q = q * lax.rsqrt(jnp.sum(q * q, axis=-1, keepdims=True) + EPS) * (GDN_HD ** -0.5)
            k = k * lax.rsqrt(jnp.sum(k * k, axis=-1, keepdims=True) + EPS)
            bcol = ba_ref[r0:r0 + rv, h:h + 1]
            acol = ba_ref[r0:r0 + rv, GDN_HEADS + h:GDN_HEADS + h + 1] + dtb[:, h:h + 1]
            softplus = jnp.maximum(acol, 0.0) + jnp.log1p(jnp.exp(-jnp.abs(acol)))
            qs.append(padrows(q)); ks.append(padrows(k)); vs.append(padrows(v))
            bs.append(padrows(_sigmoid(bcol)))
            gs.append(padrows(neg_ea[:, h:h + 1] * softplus))
        qms.append(jnp.concatenate(qs, axis=0))
        kms.append(jnp.concatenate(ks, axis=0))
        vms.append(jnp.concatenate(vs, axis=0))
        betas.append(jnp.concatenate(bs, axis=0))
        gcols.append(jnp.concatenate(gs, axis=0))
    gmat = jnp.concatenate(gcols + [jnp.zeros((GDN_STACK, LANE - n_ch), F32)], axis=1)
    gc_all = jnp.dot(causal, gmat, preferred_element_type=F32, precision=lax.Precision.HIGHEST)

    a_mats, qks, rhss, gcs, tinv = [], [], [], [], []
    for n in range(n_ch):
        gc = gc_all[:, n:n + 1]
        gc_row = jnp.sum(eye * gc, axis=0, keepdims=True)
        decay = jnp.exp(jnp.where(causal > 0, gc - gc_row, 0.0)) * causal
        kb = kms[n] * betas[n]
        kmb = kms[n].astype(BF16)
        a_mat = _dot_t(kb.astype(BF16), kmb) * decay * strict
        qks.append((_dot_t(qms[n].astype(BF16), kmb) * decay).astype(BF16))
        rhss.append(jnp.concatenate([vms[n] * betas[n], kb * jnp.exp(gc)], axis=1).astype(BF16))
        a_mats.append(a_mat)
        gcs.append(gc)
        tinv.append(eye - a_mat * m_ref[0])
    for lvl in range(1, 6):
        tb = [x.astype(BF16) for x in tinv]
        xs = [_dot((a_mats[n] * m_ref[lvl]).astype(BF16), tb[n]).astype(BF16) for n in range(n_ch)]
        tinv = [tinv[n] - _dot(tb[n], xs[n]) for n in range(n_ch)]
    sols = [_dot(tinv[n].astype(BF16), rhss[n]) for n in range(n_ch)]

    for n in range(n_ch):
        r0 = n * GDN_CHUNK
        gc = gcs[n]
        egc = jnp.exp(gc)
        u_all, w_all = sols[n][:, :GDN_HD], sols[n][:, GDN_HD:]
        qd = qms[n] * egc
        v_news, o_inter, gls = [], [], []
        for h in range(GDN_HEADS):
            rs = slice(h * GDN_CHUNK, (h + 1) * GDN_CHUNK)
            sb = s_scr[h].astype(BF16)
            both = _dot(jnp.concatenate([w_all[rs], qd[rs]], axis=0).astype(BF16), sb)
            v_news.append(u_all[rs] - both[:GDN_CHUNK])
            o_inter.append(both[GDN_CHUNK:])
            gls.append(gc[h * GDN_CHUNK + GDN_CHUNK - 1:(h + 1) * GDN_CHUNK, :])
        v_stack = jnp.concatenate(v_news, axis=0).astype(BF16)
        o_intra = _dot(qks[n], v_stack)
        for h in range(GDN_HEADS):
            rs = slice(h * GDN_CHUNK, (h + 1) * GDN_CHUNK)
            kd = kms[n][rs] * jnp.exp(gls[h] - gc[rs])
            s_scr[h] = s_scr[h] * jnp.exp(gls[h]) + _dot(kd.T.astype(BF16), v_stack[rs])
            o = (o_inter[h] + o_intra[rs])[:rv]
            on = o * lax.rsqrt(jnp.mean(o * o, axis=-1, keepdims=True) + EPS) * nw_ref[...]
            zz = z_ref[r0:r0 + rv, h * GDN_HD:(h + 1) * GDN_HD]
            o_ref[r0:r0 + rv, h * GDN_HD:(h + 1) * GDN_HD] = on * (zz * _sigmoid(zz))

    last = xs_scr[row0 + tt - (GDN_CONV - 1):row0 + tt, :]
    buf_ref[0] = last
    xs_scr[GDN_PREV0:row0, :] = last

    @pl.when(t == nt - 1)
    def _():
        s_ref[0] = s_scr[...]


def _gdn_masks2(chunk):
    n = GDN_HEADS * chunk
    r = np.arange(n)[:, None]
    c = np.arange(n)[None, :]
    same = lambda k: (r // k) == (c // k)
    m = [same(2) & (c < r)]
    k = 4
    while k <= chunk:
        m.append(same(k) & ~same(k // 2) & (c < r))
        k *= 2
    m += [same(chunk) & (c <= r), same(chunk) & (c < r), r == c]
    return np.stack(m).astype(np.float32)


def _gdn_kernel2(qkv_ref, z_ref, ba_ref, prev_ref, s0_ref, cw_ref, par_ref, nw_ref, m_ref,
                 o_ref, buf_ref, s_ref, xs_scr, s_scr, *, rv, chunk, n_units, sequential):
    t = pl.program_id(1)
    nt = pl.num_programs(1)
    stack = GDN_HEADS * chunk
    n_lvl = m_ref.shape[0] - 3
    causal, strict, eye = m_ref[n_lvl], m_ref[n_lvl + 1], m_ref[n_lvl + 2]
    kc = GDN_CONV - 1
    if sequential:
        base = [8 + u * rv for u in range(n_units)]

        @pl.when(t == 0)
        def _():
            xs_scr[8 - kc:8, :] = prev_ref[0]
            s_scr[...] = s0_ref[...]

        xs_scr[8:8 + n_units * rv, :] = qkv_ref[...]
    else:
        base = [u * (8 + rv) + 8 for u in range(n_units)]
        s_scr[...] = s0_ref[...]
        for u in range(n_units):
            xs_scr[base[u] - kc:base[u], :] = prev_ref[u]
            xs_scr[base[u]:base[u] + rv, :] = qkv_ref[u * rv:(u + 1) * rv, :]
    ba = ba_ref[...]
    beta_all = _sigmoid(ba)
    a_sh = ba + par_ref[1:2, :]
    g_all = -jnp.exp(par_ref[0:1, :]) * (jnp.maximum(a_sh, 0.0) + jnp.log1p(jnp.exp(-jnp.abs(a_sh))))
    pad = chunk - rv

    def padrows(a):
        if pad == 0:
            return a
        return jnp.concatenate([a, jnp.zeros((pad, a.shape[1]), a.dtype)], axis=0)

    qms, kms, vms, betas, gcols = [], [], [], [], []
    for u in range(n_units):
        r0 = u * rv
        y = cw_ref[0:1, :] * xs_scr[base[u] - kc:base[u] - kc + rv, :]
        for i in range(1, GDN_CONV):
            y = y + cw_ref[i:i + 1, :] * xs_scr[base[u] - kc + i:base[u] - kc + i + rv, :]
        y = y * _sigmoid(y)
        qs, ks, vs, bs, gs = [], [], [], [], []
        for h in range(GDN_HEADS):
            q = y[:, h * GDN_HD:(h + 1) * GDN_HD]
            k = y[:, GDN_W + h * GDN_HD:GDN_W + (h + 1) * GDN_HD]
            v = y[:, 2 * GDN_W + h * GDN_HD:2 * GDN_W + (h + 1) * GDN_HD]
            q = q * lax.rsqrt(jnp.sum(q * q, axis=-1, keepdims=True) + EPS) * (GDN_HD ** -0.5)
            k = k * lax.rsqrt(jnp.sum(k * k, axis=-1, keepdims=True) + EPS)
            qs.append(padrows(q)); ks.append(padrows(k)); vs.append(padrows(v))
            bs.append(padrows(beta_all[r0:r0 + rv, h:h + 1]))
            gs.append(padrows(g_all[r0:r0 + rv, GDN_HEADS + h:GDN_HEADS + h + 1]))
        qms.append(jnp.concatenate(qs, axis=0))
        kms.append(jnp.concatenate(ks, axis=0))
        vms.append(jnp.concatenate(vs, axis=0))
        betas.append(jnp.concatenate(bs, axis=0))
        gcols.append(jnp.concatenate(gs, axis=0))
    gmat = jnp.concatenate(gcols + [jnp.zeros((stack, LANE - n_units), F32)], axis=1)
    gc_all = jnp.dot(causal, gmat, preferred_element_type=F32, precision=lax.Precision.HIGHEST)

    a_bfs, qks, rhss, gcs, tinv = [], [], [], [], []
    for u in range(n_units):
        gc = gc_all[:, u:u + 1]
        gc_row = jnp.sum(eye * gc, axis=0, keepdims=True)
        decay = jnp.exp(jnp.where(causal > 0, gc - gc_row, 0.0)) * causal
        kb = kms[u] * betas[u]
        kmb = kms[u].astype(BF16)
        a_mat = _dot_t(kb.astype(BF16), kmb) * decay * strict
        qks.append((_dot_t(qms[u].astype(BF16), kmb) * decay).astype(BF16))
        rhss.append(jnp.concatenate([vms[u] * betas[u], kb * jnp.exp(gc)], axis=1).astype(BF16))
        a_bfs.append(a_mat.astype(BF16))
        gcs.append(gc)
        tinv.append(eye - a_mat * m_ref[0])
    for lvl in range(1, n_lvl):
        tb = [x.astype(BF16) for x in tinv]
        xs = [(_dot(a_bfs[u], tb[u]) * m_ref[lvl]).astype(BF16) for u in range(n_units)]
        tinv = [tinv[u] - _dot(tb[u], xs[u]) for u in range(n_units)]
    sols = [_dot(tinv[u].astype(BF16), rhss[u]) for u in range(n_units)]

    for u in range(n_units):
        r0 = u * rv
        st = 0 if sequential else u
        gc = gcs[u]
        u_all, w_all = sols[u][:, :GDN_HD], sols[u][:, GDN_HD:]
        qd = qms[u] * jnp.exp(gc)
        v_news, o_inter, gls = [], [], []
        for h in range(GDN_HEADS):
            rs = slice(h * chunk, (h + 1) * chunk)
            sb = s_scr[st, h].astype(BF16)
            both = _dot(jnp.concatenate([w_all[rs], qd[rs]], axis=0).astype(BF16), sb)
            v_news.append(u_all[rs] - both[:chunk])
            o_inter.append(both[chunk:])
            gls.append(gc[(h + 1) * chunk - 1:(h + 1) * chunk, :])
        v_stack = jnp.concatenate(v_news, axis=0).astype(BF16)
        o_intra = _dot(qks[u], v_stack)
        for h in range(GDN_HEADS):
            rs = slice(h * chunk, (h + 1) * chunk)
            kd = kms[u][rs] * jnp.exp(gls[h] - gc[rs])
            s_scr[st, h] = s_scr[st, h] * jnp.exp(gls[h]) + _dot(kd.T.astype(BF16), v_stack[rs])
            o = (o_inter[h] + o_intra[rs])[:rv]
            on = o * lax.rsqrt(jnp.mean(o * o, axis=-1, keepdims=True) + EPS) * nw_ref[...]
            zz = z_ref[r0:r0 + rv, h * GDN_HD:(h + 1) * GDN_HD]
            o_ref[r0:r0 + rv, h * GDN_HD:(h + 1) * GDN_HD] = on * (zz * _sigmoid(zz))

    if sequential:
        last = xs_scr[8 + n_units * rv - kc:8 + n_units * rv, :]
        buf_ref[0] = last
        xs_scr[8 - kc:8, :] = last

        @pl.when(t == nt - 1)
        def _():
            s_ref[...] = s_scr[...]
    else:
        for u in range(n_units):
            buf_ref[u] = xs_scr[base[u] + rv - kc:base[u] + rv, :]
        s_ref[...] = s_scr[...]


def gdn2(qkv_pre, z_pre, ba_pre, prev, s0, conv_w, a_log, dt_bias, norm_w, n_seq, units):
    n_rows = qkv_pre.shape[0]
    t_rows = n_rows // n_seq
    sequential = t_rows >= GDN_CHUNK
    if sequential:
        rv = chunk = GDN_CHUNK
        units = min(units, t_rows // chunk)
        grid = (n_seq, t_rows // (units * chunk))
        seq_blk, scr_rows = 1, 8 + units * rv
    else:
        rv, chunk = t_rows, 16
        units = math.gcd(units, n_seq)
        assert rv % 8 == 0 and rv <= chunk
        grid = (n_seq // units, 1)
        seq_blk, scr_rows = units, units * (8 + rv)
    tt = units * rv
    nt = grid[1]
    par = jnp.zeros((8, LANE), F32).at[0, GDN_HEADS:2 * GDN_HEADS].set(a_log).at[1, GDN_HEADS:2 * GDN_HEADS].set(dt_bias)
    masks = jnp.asarray(_gdn_masks2(chunk))
    const2 = lambda b, i: (0, 0)
    rows = lambda b, i: (b * nt + i, 0)
    return pl.pallas_call(
        functools.partial(_gdn_kernel2, rv=rv, chunk=chunk, n_units=units, sequential=sequential), grid=grid,
        in_specs=[pl.BlockSpec((tt, GDN_C3), rows), pl.BlockSpec((tt, GDN_W), rows),
                  pl.BlockSpec((tt, LANE), rows),
                  pl.BlockSpec((seq_blk, GDN_CONV - 1, GDN_C3), lambda b, i: (b, 0, 0)),
                  pl.BlockSpec((seq_blk, GDN_HEADS, GDN_HD, GDN_HD), lambda b, i: (b, 0, 0, 0)),
                  pl.BlockSpec((GDN_CONV, GDN_C3), const2), pl.BlockSpec((8, LANE), const2),
                  pl.BlockSpec((1, GDN_HD), const2),
                  pl.BlockSpec(masks.shape, lambda b, i: (0, 0, 0))],
        out_specs=[pl.BlockSpec((tt, GDN_W), rows),
                   pl.BlockSpec((seq_blk, GDN_CONV - 1, GDN_C3), lambda b, i: (b, 0, 0)),
                   pl.BlockSpec((seq_blk, GDN_HEADS, GDN_HD, GDN_HD), lambda b, i: (b, 0, 0, 0))],
        out_shape=[jax.ShapeDtypeStruct((n_rows, GDN_W), F32),
                   jax.ShapeDtypeStruct((n_seq, GDN_CONV - 1, GDN_C3), F32),
                   jax.ShapeDtypeStruct((n_seq, GDN_HEADS, GDN_HD, GDN_HD), F32)],
        scratch_shapes=[pltpu.VMEM((scr_rows, GDN_C3), F32),
                        pltpu.VMEM((seq_blk, GDN_HEADS, GDN_HD, GDN_HD), F32)],
        compiler_params=_cparams(("arbitrary", "arbitrary"), 48), name="gdn",
    )(qkv_pre, z_pre, ba_pre, prev, s0, conv_w, par, norm_w.reshape(1, GDN_HD), masks)


def gdn(qkv_pre, z_pre, ba_pre, prev, s0, conv_w, a_log, dt_bias, norm_w, n_seq, tt):
    n_rows = qkv_pre.shape[0]
    t_rows = n_rows // n_seq
    tt = min(tt, t_rows)
    nt = t_rows // tt
    par = jnp.zeros((8, LANE), F32).at[0, :GDN_HEADS].set(a_log).at[1, :GDN_HEADS].set(dt_bias)
    masks = jnp.asarray(_gdn_masks())
    const2 = lambda b, i: (0, 0)
    rows = lambda b, i: (b * nt + i, 0)
    return pl.pallas_call(
        functools.partial(_gdn_kernel, tt=tt), grid=(n_seq, nt),
        in_specs=[pl.BlockSpec((tt, GDN_C3), rows), pl.BlockSpec((tt, GDN_W), rows),
                  pl.BlockSpec((tt, LANE), rows),
                  pl.BlockSpec((1, GDN_CONV - 1, GDN_C3), lambda b, i: (b, 0, 0)),
                  pl.BlockSpec((1, GDN_HEADS, GDN_HD, GDN_HD), lambda b, i: (b, 0, 0, 0)),
                  pl.BlockSpec((GDN_CONV, GDN_C3), const2), pl.BlockSpec((8, LANE), const2),
                  pl.BlockSpec((1, GDN_HD), const2),
                  pl.BlockSpec(masks.shape, lambda b, i: (0, 0, 0))],
        out_specs=[pl.BlockSpec((tt, GDN_W), rows),
                   pl.BlockSpec((1, GDN_CONV - 1, GDN_C3), lambda b, i: (b, 0, 0)),
                   pl.BlockSpec((1, GDN_HEADS, GDN_HD, GDN_HD), lambda b, i: (b, 0, 0, 0))],
        out_shape=[jax.ShapeDtypeStruct((n_rows, GDN_W), F32),
                   jax.ShapeDtypeStruct((n_seq, GDN_CONV - 1, GDN_C3), F32),
                   jax.ShapeDtypeStruct((n_seq, GDN_HEADS, GDN_HD, GDN_HD), F32)],
        scratch_shapes=[pltpu.VMEM((8 + tt, GDN_C3), F32), pltpu.VMEM((GDN_HEADS, GDN_HD, GDN_HD), F32)],
        compiler_params=_cparams(("arbitrary", "arbitrary"), 48), name="gdn",
    )(qkv_pre, z_pre, ba_pre, prev, s0, conv_w, par, norm_w.reshape(1, GDN_HD), masks)


CMP_W = 2 * NSA_KV_HEADS * NSA_HD


def _cmp_weights(cmp_pe, cmp_w):
    def bd(w):
        z = jnp.zeros_like(w)
        return jnp.concatenate([jnp.concatenate([w, z], 2), jnp.concatenate([z, w], 2)], 1)
    ws, cs = [], []
    for s in range(2):
        lo, hi = bd(cmp_w[s, :CMP_STRIDE]), bd(cmp_w[s, CMP_STRIDE:])
        cat = jnp.concatenate([lo, hi], axis=2)
        ws.append(cat.reshape(CMP_STRIDE // 2, 2 * LANE, 2 * LANE))
        pe2 = jnp.concatenate([cmp_pe[s], cmp_pe[s]], axis=-1)
        const = [jnp.einsum("jd,jde->e", pe2[h * CMP_STRIDE:(h + 1) * CMP_STRIDE], m,
                            precision=lax.Precision.HIGHEST) for h, m in enumerate((lo, hi))]
        cs.append(jnp.concatenate(const))
    return jnp.stack(ws).astype(BF16), jnp.stack(cs)


def _compress(read_k, read_v, w_ref, c_ref, n_chunk):
    outs = []
    for s, read in enumerate((read_k, read_v)):
        acc = jnp.zeros((n_chunk, 2 * LANE), F32)
        for p in range(CMP_STRIDE // 2):
            x = jnp.concatenate([read(2 * p), read(2 * p + 1)], axis=1).astype(BF16)
            acc = acc + _dot(x, w_ref[s, p])
        acc = acc + c_ref[s:s + 1, :]
        outs.append(acc[:, :LANE] + pltpu.roll(acc[:, LANE:], n_chunk - 1, axis=0))
    return jnp.concatenate(outs, axis=1)


def _cmp_kernel(k_ref, v_ref, w_ref, pe_ref, o_ref, *, n_chunk):
    o_ref[0] = _compress(lambda j: k_ref[0, pl.ds(j, n_chunk, stride=CMP_STRIDE), :],
                         lambda j: v_ref[0, pl.ds(j, n_chunk, stride=CMP_STRIDE), :], w_ref, pe_ref, n_chunk)


def compress_prompt(rows, w, pe):
    b, t, _ = rows.shape
    n_chunk = t // CMP_STRIDE
    return pl.pallas_call(
        functools.partial(_cmp_kernel, n_chunk=n_chunk), grid=(b,),
        in_specs=[pl.BlockSpec((1, t, LANE), lambda i: (i, 0, 0)), pl.BlockSpec((1, t, LANE), lambda i: (i, 0, 1)),
                  pl.BlockSpec(w.shape, lambda i: (0, 0, 0, 0)),
                  pl.BlockSpec(pe.shape, lambda i: (0, 0))],
        out_specs=pl.BlockSpec((1, n_chunk, CMP_W), lambda i: (i, 0, 0)),
        out_shape=jax.ShapeDtypeStruct((b, n_chunk, CMP_W), F32),
        compiler_params=_cparams(("parallel",), 32), name="nsa_compress",
    )(rows, rows, w, pe)


def _c2s(n_cmp_pad, n_sel_pad):
    cs = np.arange(n_cmp_pad)[:, None] * CMP_STRIDE
    ss = np.arange(n_sel_pad)[None, :] * SEL_BLOCK
    return ((cs < ss + SEL_BLOCK) & (cs + CMP_BLOCK > ss)).astype(np.float32)


def _expand(n_sel_pad, n_keys):
    s = np.arange(n_sel_pad)[:, None]
    k = np.arange(n_keys)[None, :]
    return (k // SEL_BLOCK == s).astype(np.float32)


def _cmp_probs(s, valid):
    s = jnp.where(valid, s, NEG)
    p = jnp.where(valid, jnp.exp(s - jnp.max(s, axis=-1, keepdims=True)), 0.0)
    return p / jnp.maximum(jnp.sum(p, axis=-1, keepdims=True), 1e-30)


def _select_blocks(imp, q_blk, n_sel):
    lane = lax.broadcasted_iota(jnp.int32, imp.shape, 1)
    visible = lane <= q_blk
    forced = (lane == 0) | (lane == q_blk) | (lane == q_blk - 1)
    score = jnp.where(visible, imp + jnp.where(forced, FORCED_BONUS, 0.0), NEG)
    rank = jnp.zeros(imp.shape, F32)
    for sp in range(n_sel):
        col = score[:, sp:sp + 1]
        beats = (col > score) | ((col == score) & (lane > sp))
        rank = rank + jnp.where(beats, 1.0, 0.0)
    return jnp.where(visible & (rank < SEL_TOPN), 1.0, 0.0)


def _flash(q, k_ref, v_ref, j_lo, j_hi, mask_fn, tq, tk):
    def body(j, carry):
        m, l, acc = carry
        off = pl.multiple_of(j * tk, tk)
        s = _dot_t(q, k_ref[pl.ds(off, tk), :])
        s = jnp.where(mask_fn(j), s, NEG)
        m_new = jnp.maximum(m, jnp.max(s, axis=-1, keepdims=True))
        alpha = jnp.exp(m - m_new)
        p = jnp.exp(s - m_new)
        l = alpha * l + jnp.sum(p, axis=-1, keepdims=True)
        acc = alpha * acc + _dot(p.astype(BF16), v_ref[pl.ds(off, tk), :])
        return m_new, l, acc

    init = (jnp.full((tq, 1), NEG, F32), jnp.zeros((tq, 1), F32), jnp.zeros((tq, NSA_HD), F32))
    _, l, acc = lax.fori_loop(j_lo, j_hi, body, init)
    return acc / l


def _nsa_prompt_kernel_v1(qc_ref, qr_ref, kc_ref, vc_ref, ks_ref, vs_ref, kw_ref, vw_ref, g_ref, c2s_ref, e_ref,
                       o_ref, mask_scr, *, tq, n_sel, n_cmp_pad):
    i = pl.program_id(2)
    t0 = i * tq
    qpos = t0 + lax.broadcasted_iota(jnp.int32, (tq, 1), 0)
    cend = lax.broadcasted_iota(jnp.int32, (1, n_cmp_pad), 1) * CMP_STRIDE + (CMP_BLOCK - 1)
    valid_c = cend <= qpos
    psum = jnp.zeros((tq, n_cmp_pad), F32)
    o_cmp = []
    for g in range(NSA_GROUP):
        p = _cmp_probs(_dot_t(qc_ref[0, g], kc_ref[0, 0]), valid_c)
        psum = psum + p
        o_cmp.append(_dot(p.astype(BF16), vc_ref[0, 0]))
    imp = jnp.dot(psum, c2s_ref[...], preferred_element_type=F32, precision=lax.Precision.HIGHEST)
    sel = _select_blocks(imp, lax.shift_right_logical(qpos, 6), n_sel).astype(BF16)
    n_kt = mask_scr.shape[0]
    for jj in range(n_kt):
        mask_scr[jj] = _dot(sel, e_ref[:, jj * tq:(jj + 1) * tq])
    kiota = lax.broadcasted_iota(jnp.int32, (1, tq), 1)

    def sel_mask(j):
        return (mask_scr[j] > 0.5) & (j * tq + kiota <= qpos)

    def win_mask(j):
        rel = qpos - (j * tq + kiota)
        return (rel >= 0) & (rel < WINDOW)

    w_tiles = WINDOW // tq
    for g in range(NSA_GROUP):
        q = qr_ref[0, g]
        o_sel = _flash(q, ks_ref.at[0, 0], vs_ref.at[0, 0], 0, i + 1, sel_mask, tq, tq)
        o_win = _flash(q, kw_ref.at[0, 0], vw_ref.at[0, 0], jnp.maximum(i - w_tiles, 0), i + 1, win_mask, tq, tq)
        gt = g_ref[0, g]
        o = o_cmp[g] * gt[:, 0:1] + o_sel * gt[:, 1:2] + o_win * gt[:, 2:3]
        o_ref[0, :, g * NSA_HD:(g + 1) * NSA_HD] = o


def nsa_prompt_attn_v1(qc, qr, kcmp, vcmp, ksel, vsel, kwin, vwin, gates, tq):
    b, _, t, _ = qc.shape
    tq = min(tq, t)
    n_cmp_pad = kcmp.shape[2]
    n_sel = t // SEL_BLOCK
    n_sel_pad = -(-n_sel // LANE) * LANE
    c2s = jnp.asarray(_c2s(n_cmp_pad, n_sel_pad))
    e = jnp.asarray(_expand(n_sel_pad, t), dtype=BF16)
    kern = functools.partial(_nsa_prompt_kernel_v1, tq=tq, n_sel=n_sel, n_cmp_pad=n_cmp_pad)
    qspec = pl.BlockSpec((1, NSA_GROUP, tq, NSA_HD), lambda bb, k, i: (bb, k, i, 0))
    kvspec = lambda n: pl.BlockSpec((1, 1, n, NSA_HD), lambda bb, k, i: (bb, k, 0, 0))
    return pl.pallas_call(
        kern, grid=(b, NSA_KV_HEADS, t // tq),
        in_specs=[qspec, qspec, kvspec(n_cmp_pad), kvspec(n_cmp_pad), kvspec(t), kvspec(t), kvspec(t), kvspec(t),
                  pl.BlockSpec((1, NSA_GROUP, tq, 3), lambda bb, k, i: (bb, k, i, 0)),
                  pl.BlockSpec(c2s.shape, lambda bb, k, i: (0, 0)), pl.BlockSpec(e.shape, lambda bb, k, i: (0, 0))],
        out_specs=pl.BlockSpec((1, tq, NSA_GROUP * NSA_HD), lambda bb, k, i: (bb, i, k)),
        out_shape=jax.ShapeDtypeStruct((b, t, NSA_Q_W), F32),
        scratch_shapes=[pltpu.VMEM((t // tq, tq, tq), F32)],
        compiler_params=_cparams(("parallel", "parallel", "parallel"), 40), name="nsa_prompt_attn",
    )(qc, qr, kcmp, vcmp, ksel, vsel, kwin, vwin, gates, c2s, e)


def _pair_rms(x, w):
    lo = lax.broadcasted_iota(jnp.int32, x.shape, 1) < NSA_HD
    x2 = x * x
    s_lo = jnp.sum(jnp.where(lo, x2, 0.0), axis=-1, keepdims=True)
    s_hi = jnp.sum(jnp.where(lo, 0.0, x2), axis=-1, keepdims=True)
    ms = jnp.where(lo, s_lo, s_hi) * (1.0 / NSA_HD)
    return x * lax.rsqrt(ms + EPS) * w


def _pair_rope(x, cos_t, sin_t):
    lane = lax.broadcasted_iota(jnp.int32, x.shape, 1)
    first = (lane & (NSA_HD - 1)) < NSA_HD // 2
    partner = jnp.where(first, pltpu.roll(x, LANE - NSA_HD // 2, axis=1), pltpu.roll(x, NSA_HD // 2, axis=1))
    return x * cos_t + partner * sin_t


def _rope_tables(pos):
    half = NSA_HD // 2
    inv = jnp.power(ROPE_THETA, -jnp.arange(half, dtype=F32) / half)
    ang = pos.astype(F32)[:, None] * inv[None, :]
    c, s = jnp.cos(ang), jnp.sin(ang)
    return jnp.concatenate([c, c, c, c], axis=-1), jnp.concatenate([-s, s, -s, s], axis=-1)


def _nsa_prep_kernel_v2(q_ref, kv_ref, g_ref, cos_ref, sin_ref, qn_ref, kn_ref,
                        rows_ref, wrows_ref, qc_ref, qr_ref, gate_ref, kvb_ref=None, cmp_ref=None):
    cos_t, sin_t = cos_ref[...], sin_ref[...]
    lo = lax.broadcasted_iota(jnp.int32, cos_t.shape, 1) < NSA_HD
    for c in range(NSA_HEADS // 2):
        y = _pair_rms(q_ref[:, c * LANE:(c + 1) * LANE], qn_ref[...]) * (NSA_HD ** -0.5)
        in_low_lanes = (2 * c) // NSA_GROUP == 0
        for src, dst in ((y, qc_ref), (_pair_rope(y, cos_t, sin_t) * LOG2E, qr_ref)):
            swapped = pltpu.roll(src, NSA_HD, axis=1)
            if in_low_lanes:
                even, odd = jnp.where(lo, src, 0.0), jnp.where(lo, swapped, 0.0)
            else:
                even, odd = jnp.where(lo, 0.0, swapped), jnp.where(lo, 0.0, src)
            dst[:, (2 * c) * LANE:(2 * c + 1) * LANE] = even.astype(BF16)
            dst[:, (2 * c + 1) * LANE:(2 * c + 2) * LANE] = odd.astype(BF16)
    for br in range(3):
        k = _pair_rms(kv_ref[:, br * 2 * LANE:br * 2 * LANE + LANE], kn_ref[br:br + 1, :])
        if br > 0:
            k = _pair_rope(k, cos_t, sin_t)
        v = kv_ref[:, br * 2 * LANE + LANE:(br + 1) * 2 * LANE]
        if cmp_ref is None:
            dst, off = (rows_ref, br * 2 * LANE) if br < 2 else (wrows_ref, 0)
            dst[:, off:off + LANE] = k
            dst[:, off + LANE:off + 2 * LANE] = v
        else:
            dst, w0 = (rows_ref, 2 * br) if br < 2 else (wrows_ref, 0)
            for w, x in ((w0, k), (w0 + 1, v)):
                xt = x.T
                for h in range(NSA_KV_HEADS):
                    dst[0, w, h] = xt[h * NSA_HD:(h + 1) * NSA_HD]
            if br == 0:
                cmp_ref[:, :LANE] = k
                cmp_ref[:, LANE:] = v
            else:
                kvb_ref[:, (br - 1) * LANE:br * LANE] = k.astype(BF16)
                kvb_ref[:, 2 * br * LANE:(2 * br + 1) * LANE] = jnp.where(lo, v, 1.0).astype(BF16)
                kvb_ref[:, (2 * br + 1) * LANE:(2 * br + 2) * LANE] = jnp.where(lo, 1.0, v).astype(BF16)
    gate_ref[...] = _sigmoid(g_ref[...])


def nsa_prep_v2(q_pre, kv_pre, gate_pre, cos_t, sin_t, qn, kn, tm, dims_major):
    n_rows = q_pre.shape[0]
    tm = min(tm, n_rows)
    n_tab = cos_t.shape[0] // tm
    rows = lambda w: pl.BlockSpec((tm, w), lambda i: (i, 0))
    tab = pl.BlockSpec((tm, LANE), lambda i: (i % n_tab, 0))
    widths = (NSA_HEADS * LANE, NSA_HEADS * LANE, LANE)
    dtypes = (BF16, BF16, F32)
    out_specs = [rows(w) for w in widths]
    out_shape = [jax.ShapeDtypeStruct((n_rows, w), dt) for w, dt in zip(widths, dtypes)]
    if dims_major:
        n_seq, t = n_rows // cos_t.shape[0], cos_t.shape[0]
        kv_spec = lambda n: pl.BlockSpec((1, n, NSA_KV_HEADS, NSA_HD, tm), lambda i: (i // n_tab, 0, 0, 0, i % n_tab))
        kv_shape = lambda n: jax.ShapeDtypeStruct((n_seq, n, NSA_KV_HEADS, NSA_HD, t), F32)
        out_specs = [kv_spec(4), kv_spec(2)] + out_specs + [rows(NSA_KV_W), rows(2 * LANE)]
        out_shape = [kv_shape(4), kv_shape(2)] + out_shape + [jax.ShapeDtypeStruct((n_rows, NSA_KV_W), BF16),
                                                              jax.ShapeDtypeStruct((n_rows, 2 * LANE), F32)]
    else:
        out_specs = [rows(NSA_Q_W), rows(2 * LANE)] + out_specs
        out_shape = [jax.ShapeDtypeStruct((n_rows, NSA_Q_W), F32),
                     jax.ShapeDtypeStruct((n_rows, 2 * LANE), F32)] + out_shape
    return pl.pallas_call(
        _nsa_prep_kernel_v2, grid=(n_rows // tm,),
        in_specs=[rows(NSA_Q_W), rows(NSA_KV_W), rows(LANE), tab, tab,
                  pl.BlockSpec((1, LANE), lambda i: (0, 0)), pl.BlockSpec((3, LANE), lambda i: (0, 0))],
        out_specs=out_specs, out_shape=out_shape,
        compiler_params=_cparams(("parallel",), 40), name="nsa_prep",
    )(q_pre, kv_pre, gate_pre, cos_t, sin_t, jnp.tile(qn.reshape(1, NSA_HD), (1, 2)), jnp.tile(kn, (1, 2)))


def _select_blocks_t(imp_t, q_blk, n_sel):
    blk = lax.broadcasted_iota(jnp.int32, imp_t.shape, 0)
    visible = blk <= q_blk
    forced = jnp.where(blk == 0, 1.0, 0.0) + jnp.where(blk == q_blk, 1.0, 0.0) + jnp.where(blk == q_blk - 1, 1.0, 0.0)
    score = jnp.where(visible, imp_t + jnp.where(forced > 0.0, FORCED_BONUS, 0.0), NEG)
    rank = jnp.zeros(imp_t.shape, F32)
    for sp in range(n_sel):
        row = score[sp:sp + 1, :]
        tie = jnp.where(blk > sp, jnp.where(row == score, 1.0, 0.0), 0.0)
        rank = rank + jnp.where(row > score, 1.0, 0.0) + tie
    return jnp.where(visible, jnp.where(rank < SEL_TOPN, 1.0, 0.0), 0.0)


def _flash8_v2(qs, k_ref, v_refs, j_lo, j_hi, bias_fn, tq, tk):
    def body(j, carry):
        off = pl.multiple_of(j * tk, tk)
        k = k_ref[pl.ds(off, tk), :]
        vs = [v_ref[pl.ds(off, tk), :] for v_ref in v_refs]
        bias = bias_fn(j)
        out = []
        for h, (m, acc) in enumerate(carry):
            s = _dot_t(qs[h], k).astype(BF16) + bias[h // NSA_GROUP]
            m_new = jnp.maximum(m, jnp.max(s, axis=-1, keepdims=True).astype(F32))
            p = jnp.exp2(s - m_new.astype(BF16))
            acc = jnp.exp2(m - m_new) * acc + _dot(p, vs[h // NSA_GROUP])
            out.append((m_new, acc))
        return tuple(out)

    init = tuple((jnp.full((tq, 1), NEG, F32), jnp.zeros((tq, LANE), F32)) for _ in range(NSA_HEADS))
    outs = []
    for h, (_, acc) in enumerate(lax.fori_loop(j_lo, j_hi, body, init)):
        c = NSA_HD if h // NSA_GROUP == 0 else 0
        outs.append(acc / acc[:, c:c + 1])
    return outs


def _nsa_prompt_kernel_v2(qc_ref, qr_ref, kvc_ref, ks_ref, kw_ref, vs0_ref, vs1_ref, vw0_ref, vw1_ref, g_ref,
                          c2s_ref, e_ref, o_ref, *, tq, n_sel):
    i = pl.program_id(1)
    t0 = i * tq
    n_rows = NSA_HEADS * tq
    stack = lambda ref: jnp.concatenate([ref[:, h * LANE:(h + 1) * LANE] for h in range(NSA_HEADS)], axis=0)
    qpos = t0 + lax.broadcasted_iota(jnp.int32, (tq, 1), 0)
    n_cmp = kvc_ref.shape[1]
    kc = kvc_ref[0, :, :LANE].astype(BF16)
    vc = kvc_ref[0, :, LANE:].astype(BF16)
    cend = lax.broadcasted_iota(jnp.int32, (1, n_cmp), 1) * CMP_STRIDE + (CMP_BLOCK - 1)
    s = _dot_t(stack(qc_ref), kc).reshape(NSA_KV_HEADS, NSA_GROUP, tq, n_cmp)
    p = _cmp_probs(s, (cend <= qpos)[None, None])
    o_cmp = _dot(p.reshape(n_rows, n_cmp).astype(BF16), vc)
    psum = p[:, 0]
    for g in range(1, NSA_GROUP):
        psum = psum + p[:, g]
    imp = jnp.dot(psum.reshape(NSA_KV_HEADS * tq, n_cmp), c2s_ref[...], preferred_element_type=F32,
                  precision=lax.Precision.HIGHEST)
    n_blk_rows = -(-n_sel // 8) * 8
    col = lax.broadcasted_iota(jnp.int32, (1, NSA_KV_HEADS * tq), 1)
    q_blk = lax.shift_right_logical(t0 + (col & (tq - 1)), 6)
    sel_t = _select_blocks_t(imp.T[:n_blk_rows], q_blk, n_sel)
    sel_t = jnp.concatenate([sel_t, jnp.zeros((LANE - n_blk_rows, NSA_KV_HEADS * tq), F32)], axis=0)
    sel = sel_t.T.astype(BF16)
    kiota = lax.broadcasted_iota(jnp.int32, (1, tq), 1)

    def sel_bias(j):
        causal = (j * tq + kiota) <= qpos
        return [((jnp.where(causal, _dot(sel[k * tq:(k + 1) * tq], e_ref[j]), 0.0) - 1.0) * -NEG).astype(BF16)
                for k in range(NSA_KV_HEADS)]

    def win_bias(j):
        rel = qpos - (j * tq + kiota)
        return [jnp.where(jnp.where(rel >= 0, rel, WINDOW) < WINDOW, 0.0, NEG).astype(BF16)] * NSA_KV_HEADS

    qr = [qr_ref[:, h * LANE:(h + 1) * LANE] for h in range(NSA_HEADS)]
    o_sel = _flash8_v2(qr, ks_ref.at[0], [vs0_ref.at[0], vs1_ref.at[0]], 0, i + 1, sel_bias, tq, tq)
    o_win = _flash8_v2(qr, kw_ref.at[0], [vw0_ref.at[0], vw1_ref.at[0]], jnp.maximum(i - WINDOW // tq, 0), i + 1,
                       win_bias, tq, tq)
    lo = lax.broadcasted_iota(jnp.int32, (tq, LANE), 1) < NSA_HD
    for c in range(NSA_HEADS // 2):
        pair = []
        for h in (2 * c, 2 * c + 1):
            rs = slice(h * tq, (h + 1) * tq)
            gt = g_ref[:, 3 * h:3 * h + 3]
            pair.append(o_cmp[rs] * gt[:, 0:1] + o_sel[h] * gt[:, 1:2] + o_win[h] * gt[:, 2:3])
        even, odd = pair
        if (2 * c) // NSA_GROUP == 0:
            blk = jnp.where(lo, even, pltpu.roll(odd, NSA_HD, axis=1))
        else:
            blk = jnp.where(lo, pltpu.roll(even, NSA_HD, axis=1), odd)
        o_ref[:, c * LANE:(c + 1) * LANE] = blk


def nsa_prompt_attn_v2(qc, qr, kvc, kvb, gates, n_batch, tq):
    n_rows = qc.shape[0]
    t = n_rows // n_batch
    tq = min(tq, t)
    nt = t // tq
    n_cmp = kvc.shape[1]
    n_sel = t // SEL_BLOCK
    assert n_cmp % LANE == 0 and n_sel <= LANE and tq & (tq - 1) == 0 and WINDOW % tq == 0
    c2s = jnp.asarray(_c2s(n_cmp, LANE))
    e = jnp.asarray(_expand(LANE, t).reshape(LANE, nt, tq).transpose(1, 0, 2), dtype=BF16)
    kvb3 = kvb.reshape(n_batch, t, NSA_KV_W)
    rows = lambda w: pl.BlockSpec((tq, w), lambda b, i: (b * nt + i, 0))
    kv = lambda c: pl.BlockSpec((1, t, LANE), lambda b, i: (b, 0, c))
    return pl.pallas_call(
        functools.partial(_nsa_prompt_kernel_v2, tq=tq, n_sel=n_sel), grid=(n_batch, nt),
        in_specs=[rows(NSA_HEADS * LANE), rows(NSA_HEADS * LANE),
                  pl.BlockSpec((1, n_cmp, CMP_W), lambda b, i: (b, 0, 0)),
                  kv(0), kv(1), kv(2), kv(3), kv(4), kv(5), rows(LANE),
                  pl.BlockSpec(c2s.shape, lambda b, i: (0, 0)), pl.BlockSpec(e.shape, lambda b, i: (0, 0, 0))],
        out_specs=rows(NSA_Q_W),
        out_shape=jax.ShapeDtypeStruct((n_rows, NSA_Q_W), F32),
        compiler_params=_cparams(("parallel", "parallel"), 48), name="nsa_prompt_attn",
    )(qc, qr, kvc, kvb3, kvb3, kvb3, kvb3, kvb3, kvb3, gates, c2s, e)


LOG2E = 1.4426950408889634
KEY_TILE = 256


def _head_rows(yt, head_in_pair, kv_head):
    own = yt[head_in_pair * NSA_HD:(head_in_pair + 1) * NSA_HD]
    zero = jnp.zeros_like(own)
    return jnp.concatenate([own, zero] if kv_head == 0 else [zero, own], axis=0)


def _nsa_prep_kernel(q_ref, kv_ref, g_ref, cos_ref, sin_ref, qn_ref, kn_ref,
                     rows_ref, wrows_ref, qct_ref, qrt_ref, gt_ref, kb_ref=None, vt_ref=None, *, tm):
    cos_t, sin_t = cos_ref[...], sin_ref[...]
    for c in range(NSA_HEADS // 2):
        y = _pair_rms(q_ref[:, c * LANE:(c + 1) * LANE], qn_ref[...]) * (NSA_HD ** -0.5)
        yr = _pair_rope(y, cos_t, sin_t) * LOG2E
        for src, dst in ((y, qct_ref), (yr, qrt_ref)):
            st = src.T
            for e in range(2):
                h = 2 * c + e
                dst[h * LANE:(h + 1) * LANE, :] = _head_rows(st, e, h // NSA_GROUP).astype(BF16)
    for br in range(3):
        k = _pair_rms(kv_ref[:, br * 2 * LANE:br * 2 * LANE + LANE], kn_ref[br:br + 1, :])
        if br > 0:
            k = _pair_rope(k, cos_t, sin_t)
        v = kv_ref[:, br * 2 * LANE + LANE:(br + 1) * 2 * LANE]
        dst, off = (rows_ref, br * 2 * LANE) if br < 2 else (wrows_ref, 0)
        dst[:, off:off + LANE] = k
        dst[:, off + LANE:off + 2 * LANE] = v
        if br > 0 and kb_ref is not None:
            kb_ref[:, (br - 1) * LANE:br * LANE] = k.astype(BF16)
            vt = v.T.astype(BF16)
            for jt in range(tm // KEY_TILE):
                vt_ref[br - 1, jt] = vt[:, jt * KEY_TILE:(jt + 1) * KEY_TILE]
    gt_ref[...] = _sigmoid(g_ref[...]).T


def nsa_prep(q_pre, kv_pre, gate_pre, cos_t, sin_t, qn, kn, tm, attn_operands):
    n_rows = q_pre.shape[0]
    tm = min(tm, n_rows)
    n_tab = cos_t.shape[0] // tm
    rows = lambda w: pl.BlockSpec((tm, w), lambda i: (i, 0))
    cols = lambda r: pl.BlockSpec((r, tm), lambda i: (0, i))
    tab = pl.BlockSpec((tm, LANE), lambda i: (i % n_tab, 0))
    out_specs = [rows(NSA_Q_W), rows(2 * LANE), cols(NSA_HEADS * LANE), cols(NSA_HEADS * LANE), cols(LANE)]
    out_shape = [jax.ShapeDtypeStruct((n_rows, NSA_Q_W), F32), jax.ShapeDtypeStruct((n_rows, 2 * LANE), F32),
                 jax.ShapeDtypeStruct((NSA_HEADS * LANE, n_rows), BF16),
                 jax.ShapeDtypeStruct((NSA_HEADS * LANE, n_rows), BF16),
                 jax.ShapeDtypeStruct((LANE, n_rows), F32)]
    if attn_operands:
        assert tm % KEY_TILE == 0
        out_specs += [rows(2 * LANE), pl.BlockSpec((2, tm // KEY_TILE, LANE, KEY_TILE), lambda i: (0, i, 0, 0))]
        out_shape += [jax.ShapeDtypeStruct((n_rows, 2 * LANE), BF16),
                      jax.ShapeDtypeStruct((2, n_rows // KEY_TILE, LANE, KEY_TILE), BF16)]
    return pl.pallas_call(
        functools.partial(_nsa_prep_kernel, tm=tm), grid=(n_rows // tm,),
        in_specs=[rows(NSA_Q_W), rows(NSA_KV_W), rows(LANE), tab, tab,
                  pl.BlockSpec((1, LANE), lambda i: (0, 0)), pl.BlockSpec((3, LANE), lambda i: (0, 0))],
        out_specs=out_specs, out_shape=out_shape,
        compiler_params=_cparams(("parallel",), 40), name="nsa_prep",
    )(q_pre, kv_pre, gate_pre, cos_t, sin_t, jnp.tile(qn.reshape(1, NSA_HD), (1, 2)), jnp.tile(kn, (1, 2)))


def _flash_t(qts, k_ref, vt_ref, j_lo, j_hi, valid_fn, tq):
    def body(j, carry):
        k = k_ref[pl.ds(pl.multiple_of(j * KEY_TILE, KEY_TILE), KEY_TILE), :]
        vt = vt_ref[j]
        valid = valid_fn(j)
        out = []
        for h, (m, l, acc) in enumerate(carry):
            s = jnp.where(valid[h // NSA_GROUP], _dot(k, qts[h]), NEG)
            m_new = jnp.maximum(m, jnp.max(s, axis=0, keepdims=True))
            alpha = jnp.exp2(m - m_new)
            p = jnp.exp2(s - m_new)
            l = alpha * l + jnp.sum(p, axis=0, keepdims=True)
            acc = alpha * acc + _dot(vt, p.astype(BF16))
            out.append((m_new, l, acc))
        return tuple(out)

    init = tuple((jnp.full((1, tq), NEG, F32), jnp.zeros((1, tq), F32), jnp.zeros((LANE, tq), F32))
                 for _ in range(NSA_HEADS))
    return [acc / l for _, l, acc in lax.fori_loop(j_lo, j_hi, body, init)]


def _nsa_prompt_kernel(qct_ref, qrt_ref, kvc_ref, ks_ref, kw_ref, vst_ref, vwt_ref, gt_ref, c2st_ref, et_ref,
                       ot_ref, *, tq, n_sel):
    i = pl.program_id(1)
    t0 = i * tq
    qpos = t0 + lax.broadcasted_iota(jnp.int32, (1, tq), 1)
    n_cmp = kvc_ref.shape[1]
    kc = kvc_ref[0, :, :LANE].astype(BF16)
    vct = kvc_ref[0, :, LANE:].T.astype(BF16)
    cend = lax.broadcasted_iota(jnp.int32, (n_cmp, 1), 0) * CMP_STRIDE + (CMP_BLOCK - 1)
    valid_c = cend <= qpos
    psum = [jnp.zeros((n_cmp, tq), F32) for _ in range(NSA_KV_HEADS)]
    o_cmp = []
    for h in range(NSA_HEADS):
        s = jnp.where(valid_c, _dot(kc, qct_ref[h * LANE:(h + 1) * LANE, :]), NEG)
        p = jnp.where(valid_c, jnp.exp(s - jnp.max(s, axis=0, keepdims=True)), 0.0)
        p = p / jnp.maximum(jnp.sum(p, axis=0, keepdims=True), 1e-30)
        psum[h // NSA_GROUP] = psum[h // NSA_GROUP] + p
        o_cmp.append(_dot(vct, p.astype(BF16)))
    n_blk_rows = -(-n_sel // 8) * 8
    imp_t = jnp.concatenate(
        [jnp.dot(c2st_ref[...], ps, preferred_element_type=F32, precision=lax.Precision.HIGHEST)[:n_blk_rows]
         for ps in psum], axis=1)
    col = lax.broadcasted_iota(jnp.int32, (1, NSA_KV_HEADS * tq), 1)
    sel_t = _select_blocks_t(imp_t, lax.shift_right_logical(t0 + (col & (tq - 1)), 6), n_sel)
    sel_t = jnp.concatenate([sel_t, jnp.zeros((LANE - n_blk_rows, NSA_KV_HEADS * tq), F32)], axis=0).astype(BF16)
    krow = lax.broadcasted_iota(jnp.int32, (KEY_TILE, 1), 0)

    def sel_valid(j):
        causal = (j * KEY_TILE + krow) <= qpos
        return [jnp.where(causal, _dot(et_ref[j], sel_t[:, k * tq:(k + 1) * tq]), 0.0) > 0.5
                for k in range(NSA_KV_HEADS)]

    def win_valid(j):
        rel = qpos - (j * KEY_TILE + krow)
        return [jnp.where(rel >= 0, rel, WINDOW) < WINDOW] * NSA_KV_HEADS

    qrt = [qrt_ref[h * LANE:(h + 1) * LANE, :] for h in range(NSA_HEADS)]
    n_kt = (t0 + tq) // KEY_TILE
    o_sel = _flash_t(qrt, ks_ref.at[0], vst_ref.at[0], 0, n_kt, sel_valid, tq)
    o_win = _flash_t(qrt, kw_ref.at[0], vwt_ref.at[0], jnp.maximum((t0 - WINDOW) // KEY_TILE, 0), n_kt,
                     win_valid, tq)
    for h in range(NSA_HEADS):
        g = gt_ref[3 * h:3 * h + 3, :]
        o = o_cmp[h] * g[0:1] + o_sel[h] * g[1:2] + o_win[h] * g[2:3]
        k = h // NSA_GROUP
        ot_ref[h * NSA_HD:(h + 1) * NSA_HD, :] = o[k * NSA_HD:(k + 1) * NSA_HD]


def nsa_prompt_attn(qct, qrt, kvc, kb, vt, gates_t, n_batch, tq):
    n_rows = qct.shape[1]
    t = n_rows // n_batch
    tq = min(tq, t)
    nt = t // tq
    n_kt = t // KEY_TILE
    n_cmp = kvc.shape[1]
    n_sel = t // SEL_BLOCK
    assert n_cmp % LANE == 0 and n_sel <= LANE and tq & (tq - 1) == 0 and tq % KEY_TILE == 0
    c2st = jnp.asarray(_c2s(n_cmp, LANE).T)
    et = jnp.asarray(_expand(LANE, t).T.reshape(n_kt, KEY_TILE, LANE), dtype=BF16)
    kb3 = kb.reshape(n_batch, t, 2 * LANE)
    vt4 = vt.reshape(2, n_batch, n_kt, LANE, KEY_TILE)
    cols = lambda r: pl.BlockSpec((r, tq), lambda b, i: (0, b * nt + i))
    kspec = lambda c: pl.BlockSpec((1, t, LANE), lambda b, i: (b, 0, c))
    vspec = lambda c: pl.BlockSpec((None, 1, n_kt, LANE, KEY_TILE), lambda b, i: (c, b, 0, 0, 0))
    return pl.pallas_call(
        functools.partial(_nsa_prompt_kernel, tq=tq, n_sel=n_sel), grid=(n_batch, nt),
        in_specs=[cols(NSA_HEADS * LANE), cols(NSA_HEADS * LANE),
                  pl.BlockSpec((1, n_cmp, CMP_W), lambda b, i: (b, 0, 0)),
                  kspec(0), kspec(1), vspec(0), vspec(1), cols(LANE),
                  pl.BlockSpec(c2st.shape, lambda b, i: (0, 0)), pl.BlockSpec(et.shape, lambda b, i: (0, 0, 0))],
        out_specs=cols(NSA_Q_W),
        out_shape=jax.ShapeDtypeStruct((NSA_Q_W, n_rows), F32),
        compiler_params=_cparams(("parallel", "parallel"), 48), name="nsa_prompt_attn",
    )(qct, qrt, kvc, kb3, kb3, vt4, vt4, gates_t, c2st, et)


SROWS = NSA_HEADS * 8
HALF_W = 2 * NSA_KV_HEADS * NSA_HD


def _page_copies(cache_ref, pt_ref, targets, sem, b, slot, n_pages, page):
    return [pltpu.make_async_copy(cache_ref.at[pt_ref[b, pg], kind],
                                  buf.at[slot, :, :, pl.ds(pg * page, page)], sem.at[slot])
            for pg in range(n_pages) for kind, buf in targets]


def _stream_pages(cache_ref, pt_ref, targets, sem, n_pages, page):
    b = pl.program_id(0)
    nb = pl.num_programs(0)
    slot = lax.rem(b, 2)

    @pl.when(b == 0)
    def _():
        for c in _page_copies(cache_ref, pt_ref, targets, sem, 0, 0, n_pages, page):
            c.start()

    @pl.when(b + 1 < nb)
    def _():
        for c in _page_copies(cache_ref, pt_ref, targets, sem, b + 1, 1 - slot, n_pages, page):
            c.start()

    for c in _page_copies(cache_ref, pt_ref, targets, sem, b, slot, n_pages, page):
        c.wait()
    return slot


def _nsa_s1_kernel(pt_ref, cache_ref, q_ref, w_ref, pe_ref, c2s_ref, ocmp_ref, sel_ref, bufk, bufv, xk, xv, sem,
                   *, n_pages, page, ts, n_sel):
    past = n_pages * page
    slot = _stream_pages(cache_ref, pt_ref, [(0, bufk), (1, bufv)], sem, n_pages, page)
    n_chunk = past // CMP_STRIDE
    tch = min(past, 8 * LANE)
    for src, dst in ((bufk, xk), (bufv, xv)):
        for c in range(past // tch):
            dst[c * tch:(c + 1) * tch, :] = src[slot, :, :, c * tch:(c + 1) * tch].reshape(LANE, tch).T
    kv = _compress(lambda j: xk[pl.ds(j, n_chunk, stride=CMP_STRIDE), :],
                   lambda j: xv[pl.ds(j, n_chunk, stride=CMP_STRIDE), :], w_ref, pe_ref, n_chunk)
    kc = kv[:, :LANE].astype(BF16)
    vc = kv[:, LANE:].astype(BF16)
    tok = lax.broadcasted_iota(jnp.int32, (SROWS, 1), 0) & (ts - 1)
    qpos = past + tok
    cend = lax.broadcasted_iota(jnp.int32, (1, n_chunk), 1) * CMP_STRIDE + (CMP_BLOCK - 1)
    p = _cmp_probs(_dot_t(q_ref[0], kc), cend <= qpos)
    ocmp_ref[0] = _dot(p.astype(BF16), vc)
    psum = []
    for k in range(NSA_KV_HEADS):
        acc = p[k * NSA_GROUP * ts:k * NSA_GROUP * ts + ts]
        for g in range(1, NSA_GROUP):
            r = (k * NSA_GROUP + g) * ts
            acc = acc + p[r:r + ts]
        psum.append(acc)
    psum = jnp.concatenate(psum, axis=0)
    imp = jnp.dot(psum, c2s_ref[...], preferred_element_type=F32, precision=lax.Precision.HIGHEST)
    q_blk = lax.shift_right_logical(past + (lax.broadcasted_iota(jnp.int32, (NSA_KV_HEADS * ts, 1), 0) & (ts - 1)), 6)
    sel_ref[0] = _select_blocks(imp, q_blk, n_sel)


def _nsa_s2_kernel(pt_ref, cache_ref, q_ref, sel_ref, new_ref, win_ref, wnew_ref, ocmp_ref, g_ref, e_ref,
                   o_ref, bufk, bufv, sem, *, n_pages, page, ts, wb):
    past = n_pages * page
    slot = _stream_pages(cache_ref, pt_ref, [(2, bufk), (3, bufv)], sem, n_pages, page)
    q = q_ref[0]
    tok = lax.broadcasted_iota(jnp.int32, (SROWS, 1), 0) & (ts - 1)
    sel = sel_ref[0]
    sel_rows = jnp.concatenate([sel[k * ts:(k + 1) * ts] for k in range(NSA_KV_HEADS) for _ in range(NSA_GROUP)],
                               axis=0)
    n_new = new_ref.shape[1]
    jnew = lax.broadcasted_iota(jnp.int32, (1, n_new), 1)

    def attend(kt_past, vt_past, valid_past, k_new, v_new, valid_new):
        s_past = jnp.where(valid_past, _dot(q, kt_past), NEG)
        s_new = jnp.where(valid_new, _dot_t(q, k_new), NEG)
        m = jnp.maximum(jnp.max(s_past, axis=-1, keepdims=True), jnp.max(s_new, axis=-1, keepdims=True))
        p_past = jnp.exp2(s_past - m)
        p_new = jnp.exp2(s_new - m)
        l = jnp.sum(p_past, axis=-1, keepdims=True) + jnp.sum(p_new, axis=-1, keepdims=True)
        return (_dot_t(p_past.astype(BF16), vt_past) + _dot(p_new.astype(BF16), v_new)) / l

    mask_past = _dot(sel_rows.astype(BF16), e_ref[...]) > 0.5
    blk_new = past // SEL_BLOCK
    valid_new = (sel_rows[:, blk_new:blk_new + 1] > 0.5) & (jnew <= tok)
    o_sel = attend(bufk[slot].reshape(LANE, past).astype(BF16), bufv[slot].reshape(LANE, past).astype(BF16),
                   mask_past, new_ref[0, :, 0:LANE].astype(BF16), new_ref[0, :, LANE:HALF_W].astype(BF16), valid_new)
    u = lax.broadcasted_iota(jnp.int32, (1, wb), 1)
    rel = wb + tok - u
    valid_w = (rel >= 0) & (rel < WINDOW) & (past - wb + u >= 0)
    o_win = attend(win_ref[0, 0].reshape(LANE, wb).astype(BF16), win_ref[0, 1].reshape(LANE, wb).astype(BF16),
                   valid_w, wnew_ref[0, :, 0:LANE].astype(BF16), wnew_ref[0, :, LANE:HALF_W].astype(BF16),
                   jnew <= tok)
    gt = g_ref[0]
    o = ocmp_ref[0] * gt[:, 0:1] + o_sel * gt[:, 1:2] + o_win * gt[:, 2:3]
    half = SROWS // NSA_KV_HEADS
    for k in range(NSA_KV_HEADS):
        o_ref[0, k * half:(k + 1) * half, :] = o[k * half:(k + 1) * half, k * NSA_HD:(k + 1) * NSA_HD]


def nsa_sample_attn(page_table, cache, qc2, qr2, new_sel, win, new_win, gates, cmp_w, cmp_pe, ts):
    bs, n_pages = page_table.shape
    page = cache.shape[-1]
    past = n_pages * page
    wb = win.shape[-1]
    assert past % SEL_BLOCK == 0 and ts <= CMP_STRIDE and ts & (ts - 1) == 0 and SROWS == NSA_HEADS * ts
    n_chunk = past // CMP_STRIDE
    n_sel = past // SEL_BLOCK + 1
    n_sel_pad = -(-n_sel // LANE) * LANE
    c2s = jnp.asarray(_c2s(n_chunk, n_sel_pad))
    e = jnp.asarray(_expand(n_sel_pad, past), dtype=BF16)
    sems = pltpu.SemaphoreType.DMA((2,))
    pages = pltpu.VMEM((2, NSA_KV_HEADS, NSA_HD, past), F32)
    scratch1 = [pages, pages, pltpu.VMEM((past, LANE), F32), pltpu.VMEM((past, LANE), F32), sems]
    scratch2 = [pages, pages, sems]
    seq3 = lambda b, pt: (b, 0, 0)
    o_cmp, sel = pl.pallas_call(
        functools.partial(_nsa_s1_kernel, n_pages=n_pages, page=page, ts=ts, n_sel=n_sel),
        grid_spec=pltpu.PrefetchScalarGridSpec(
            num_scalar_prefetch=1, grid=(bs,),
            in_specs=[pl.BlockSpec(memory_space=pl.ANY),
                      pl.BlockSpec((1, SROWS, LANE), seq3),
                      pl.BlockSpec(cmp_w.shape, lambda b, pt: (0, 0, 0, 0)),
                      pl.BlockSpec(cmp_pe.shape, lambda b, pt: (0, 0)),
                      pl.BlockSpec(c2s.shape, lambda b, pt: (0, 0))],
            out_specs=[pl.BlockSpec((1, SROWS, LANE), seq3),
                       pl.BlockSpec((1, NSA_KV_HEADS * ts, n_sel_pad), seq3)],
            scratch_shapes=scratch1),
        out_shape=[jax.ShapeDtypeStruct((bs, SROWS, LANE), F32),
                   jax.ShapeDtypeStruct((bs, NSA_KV_HEADS * ts, n_sel_pad), F32)],
        compiler_params=_cparams(("arbitrary",), 48), name="nsa_sample_cmp",
    )(page_table, cache, qc2, cmp_w, cmp_pe, c2s)
    n_new = new_sel.shape[1]
    return pl.pallas_call(
        functools.partial(_nsa_s2_kernel, n_pages=n_pages, page=page, ts=ts, wb=wb),
        grid_spec=pltpu.PrefetchScalarGridSpec(
            num_scalar_prefetch=1, grid=(bs,),
            in_specs=[pl.BlockSpec(memory_space=pl.ANY),
                      pl.BlockSpec((1, SROWS, LANE), seq3),
                      pl.BlockSpec((1, NSA_KV_HEADS * ts, n_sel_pad), seq3),
                      pl.BlockSpec((1, n_new, HALF_W), seq3),
                      pl.BlockSpec((1, 2, NSA_KV_HEADS, NSA_HD, wb), lambda b, pt: (b, 0, 0, 0, 0)),
                      pl.BlockSpec((1, n_new, HALF_W), seq3),
                      pl.BlockSpec((1, SROWS, LANE), seq3),
                      pl.BlockSpec((1, SROWS, 3), seq3),
                      pl.BlockSpec(e.shape, lambda b, pt: (0, 0))],
            out_specs=pl.BlockSpec((1, SROWS, NSA_HD), seq3),
            scratch_shapes=scratch2),
        out_shape=jax.ShapeDtypeStruct((bs, SROWS, NSA_HD), F32),
        compiler_params=_cparams(("arbitrary",), 56), name="nsa_sample_sel",
    )(page_table, cache, qr2, sel, new_sel, win, new_win, o_cmp, gates, e)


def _in_proj(x2d, pos_rows, p, tm, dims_major):
    q_pre, kv_pre, gate_pre, qkv_pre, z_pre, ba_pre = rms_matmul(x2d, p["attn_norm_w"], p["w_in_parts"], tm)
    cos_t, sin_t = _rope_tables(pos_rows)
    nsa = nsa_prep_v2(q_pre, kv_pre, gate_pre, cos_t, sin_t, p["nsa_q_norm_w"], p["nsa_k_norm_w"], tm, dims_major)
    return nsa, qkv_pre, z_pre, ba_pre


def _tokens_major(a):
    return a.transpose(0, 4, 1, 2, 3)


def _mixer_out(x2d, o_nsa, o_gdn, p):
    return matmul_res([o_nsa, o_gdn], p["w_out"], x2d, 512)


def _ffn(h, p, prev, n_seq, shift, tm):
    return conv_ffn(h, p["ffn_norm_w"], p["w_up"], p["ffn_conv_w"], p["ffn_conv_b"], p["w_down"], prev,
                    n_seq, shift, tm)


def _layer_prompt(x, mem, p):
    nb, t, d = x.shape
    x2d = x.reshape(nb * t, d)
    (rows_t, wrows_t, qc, qr, gates, kvb, cmp_rows), qkv_pre, z_pre, ba_pre = _in_proj(x2d, jnp.arange(t), p, 512, True)
    kvc = compress_prompt(cmp_rows.reshape(nb, t, CMP_W), p["cmp_bd"], p["cmp_pe2"])
    o_nsa = nsa_prompt_attn_v2(qc, qr, kvc, kvb, gates, nb, 256)
    o_gdn, gdn_buf, gdn_s = gdn2(qkv_pre, z_pre, ba_pre, jnp.zeros((nb, GDN_CONV - 1, GDN_C3), F32),
                                 jnp.zeros((nb, GDN_HEADS, GDN_HD, GDN_HD), F32), p["gdn_conv_w"], p["gdn_a_log"],
                                 p["gdn_dt_bias"], p["gdn_norm_w"], nb, 4)
    mem_kv, mem_kv_bf = mem_kv_proj(mem.reshape(-1, d), p["mem_norm_w"], p["w_xk"], p["w_xv"], p["xk_norm_w"])
    h = xattn_fused(_mixer_out(x2d, o_nsa, o_gdn, p), p["xattn_norm_w"], p["w_xq"], p["xq_norm_w"],
                    mem_kv_bf.reshape(nb, mem.shape[1], 2 * XA_W), p["w_xo"], nb, 256)
    y, ffn_buf = _ffn(h, p, jnp.zeros((nb, FFN_CONV - 1, 2 * D_FF), F32), nb, 1, 512)
    keep = min(WINDOW, t)
    return (y.reshape(nb, t, d), _tokens_major(rows_t), _tokens_major(wrows_t[..., t - keep:]), gdn_s, gdn_buf,
            ffn_buf, mem_kv.reshape(nb, mem.shape[1], 2, XA_HEADS, XA_HD))


def _layer_sample(x, cache_kv, cache_win, state, gdn_cache, ffn_cache, mem_kv, page_table, p):
    bs, ts, d = x.shape
    n_pool, page = cache_kv.shape[:2]
    past = page_table.shape[1] * page
    x2d = x.reshape(bs * ts, d)
    pos_rows = jnp.tile(past + jnp.arange(ts), bs)
    (rows, wrows, qc, qr, gates), qkv_pre, z_pre, ba_pre = _in_proj(x2d, pos_rows, p, bs * ts, False)
    dims_major = lambda a: a.transpose(0, 2, 3, 4, 1)

    def stack_q(a):
        return a.reshape(bs, ts, NSA_HEADS, LANE).transpose(0, 2, 1, 3).reshape(bs, SROWS, LANE)

    def pad_new(a):
        return jnp.pad(a.reshape(bs, ts, HALF_W), ((0, 0), (0, LANE - ts), (0, 0)))

    wb = cache_win.shape[1]
    gates3 = gates[:, :NSA_GATE_W].reshape(bs, ts, NSA_HEADS, 3).transpose(0, 2, 1, 3).reshape(bs, SROWS, 3)
    win_t = dims_major(cache_win)
    o2 = nsa_sample_attn(page_table, dims_major(cache_kv), stack_q(qc), stack_q(qr),
                         pad_new(rows[:, HALF_W:]), win_t, pad_new(wrows), gates3, p["cmp_bd"], p["cmp_pe2"], ts)
    rows = rows.reshape(bs, ts, 4, NSA_KV_HEADS, NSA_HD)
    wrows = wrows.reshape(bs, ts, 2, NSA_KV_HEADS, NSA_HD)
    o_nsa = o2.reshape(bs, NSA_HEADS, ts, NSA_HD).transpose(0, 2, 1, 3).reshape(bs * ts, NSA_Q_W)
    o_gdn, gdn_buf, gdn_s = gdn2(qkv_pre, z_pre, ba_pre, gdn_cache, state, p["gdn_conv_w"], p["gdn_a_log"],
                                 p["gdn_dt_bias"], p["gdn_norm_w"], bs, 4)
    h1 = _mixer_out(x2d, o_nsa, o_gdn, p)
    (qx,) = rms_matmul(h1, p["xattn_norm_w"], [p["w_xq"]], 512)
    m = mem_kv.shape[1]
    kv_rows = mem_kv.reshape(bs, m, 2, XA_HEADS, XA_HD // LANE, LANE).transpose(0, 1, 2, 4, 3, 5)
    kv_rows = kv_rows.reshape(bs, m * MEM_ROWS, LANE)
    h = matmul_res([xattn_cache(qx, kv_rows, p["xq_norm_w"], bs)], p["w_xo"], h1, 512)
    h_tm = h.reshape(bs, ts, d).transpose(1, 0, 2).reshape(ts * bs, d)
    prev = ffn_cache.transpose(1, 0, 2).reshape(1, (FFN_CONV - 1) * bs, 2 * D_FF)
    y, ffn_buf = _ffn(h_tm, p, prev, 1, bs, ts * bs)
    y = y.reshape(ts, bs, d).transpose(1, 0, 2)
    ffn_buf = ffn_buf.reshape(FFN_CONV - 1, bs, 2 * D_FF).transpose(1, 0, 2)
    wall_t = jnp.concatenate([win_t, dims_major(wrows)], axis=-1)
    keep = min(WINDOW, wb + ts)
    return y, rows, _tokens_major(wall_t[..., wb + ts - keep:]), gdn_s, gdn_buf, ffn_buf


def _prep_params(l, attn_norm_w, w_in, nsa_q_norm_w, nsa_k_norm_w, cmp_pe, cmp_w, gdn_conv_w, gdn_a_log,
                 gdn_dt_bias, gdn_norm_w, w_out, mem_norm_w, w_xk, w_xv, xk_norm_w, xattn_norm_w, w_xq,
                 xq_norm_w, w_xo, ffn_norm_w, w_up, ffn_conv_w, ffn_conv_b, w_down):
    wi = w_in[l]
    cuts = np.cumsum([0, NSA_Q_W, NSA_KV_W, NSA_GATE_W, 3 * GDN_W, GDN_W, 2 * GDN_HEADS])
    parts = []
    for a, b in zip(cuts[:-1], cuts[1:]):
        w = wi[:, a:b]
        padn = -(-(b - a) // LANE) * LANE - (b - a)
        parts.append(jnp.pad(w, ((0, 0), (0, padn))).astype(BF16))
    cmp_bd, cmp_pe2 = _cmp_weights(cmp_pe[l], cmp_w[l])
    bf = lambda w: w[l].astype(BF16)
    return dict(attn_norm_w=attn_norm_w[l], w_in_parts=parts, nsa_q_norm_w=nsa_q_norm_w[l],
                nsa_k_norm_w=nsa_k_norm_w[l], cmp_bd=cmp_bd, cmp_pe2=cmp_pe2, gdn_conv_w=gdn_conv_w[l],
                gdn_a_log=gdn_a_log[l], gdn_dt_bias=gdn_dt_bias[l], gdn_norm_w=gdn_norm_w[l], w_out=bf(w_out),
                mem_norm_w=mem_norm_w[l], w_xk=bf(w_xk), w_xv=bf(w_xv), xk_norm_w=xk_norm_w[l],
                xattn_norm_w=xattn_norm_w[l], w_xq=bf(w_xq), xq_norm_w=xq_norm_w[l], w_xo=bf(w_xo),
                ffn_norm_w=ffn_norm_w[l], w_up=bf(w_up), ffn_conv_w=ffn_conv_w[l], ffn_conv_b=ffn_conv_b[l],
                w_down=bf(w_down))


def kernel(x_prompt, x_sample, mem_prompt, cache_nsa_kv, cache_nsa_win, state_gdn, cache_gdn_conv, cache_ffn_conv,
           cache_mem_kv, page_table, attn_norm_w, w_in, nsa_q_norm_w, nsa_k_norm_w, cmp_pe, cmp_w, gdn_conv_w,
           gdn_a_log, gdn_dt_bias, gdn_norm_w, w_out, mem_norm_w, w_xk, w_xv, xk_norm_w, xattn_norm_w, w_xq,
           xq_norm_w, w_xo, ffn_norm_w, w_up, ffn_conv_w, ffn_conv_b, w_down):
    weights = (attn_norm_w, w_in, nsa_q_norm_w, nsa_k_norm_w, cmp_pe, cmp_w, gdn_conv_w, gdn_a_log, gdn_dt_bias,
               gdn_norm_w, w_out, mem_norm_w, w_xk, w_xv, xk_norm_w, xattn_norm_w, w_xq, xq_norm_w, w_xo,
               ffn_norm_w, w_up, ffn_conv_w, ffn_conv_b, w_down)
    depth = cache_nsa_kv.shape[0]
    hp, hs = x_prompt, x_sample
    outs_p, outs_s = [], []
    for l in range(depth):
        p = _prep_params(l, *weights)
        res_p = _layer_prompt(hp, mem_prompt, p)
        hp = res_p[0]
        outs_p.append(res_p[1:])
        res_s = _layer_sample(hs, cache_nsa_kv[l], cache_nsa_win[l], state_gdn[l], cache_gdn_conv[l],
                              cache_ffn_conv[l], cache_mem_kv[l], page_table, p)
        hs = res_s[0]
        outs_s.append(res_s[1:])
    stack = lambda outs, i: jnp.stack([o[i] for o in outs])
    return ((hp, hs) + tuple(stack(outs_p, i) for i in range(6)) + tuple(stack(outs_s, i) for i in range(5)))
```

```python
import functools
import math

import numpy as np
import jax
import jax.numpy as jnp
from jax import lax
from jax.experimental import pallas as pl
from jax.experimental.pallas import tpu as pltpu

F32 = jnp.float32
BF16 = jnp.bfloat16

D_MODEL = 1024
NSA_HEADS = 8
NSA_KV_HEADS = 2
NSA_GROUP = NSA_HEADS // NSA_KV_HEADS
NSA_HD = 64
CMP_BLOCK = 32
CMP_STRIDE = 16
SEL_BLOCK = 64
SEL_TOPN = 8
WINDOW = 512
FORCED_BONUS = 1e4
GDN_HEADS = 4
GDN_HD = 128
GDN_CONV = 4
GDN_CHUNK = 64
GDN_W = GDN_HEADS * GDN_HD
XA_HEADS = 4
XA_HD = 256
D_FF = 2816
FFN_CONV = 3
ROPE_THETA = 10000.0
EPS = 1e-6
NEG = -1e30

NSA_Q_W = NSA_HEADS * NSA_HD
NSA_KV_W = 3 * 2 * NSA_KV_HEADS * NSA_HD
NSA_GATE_W = 3 * NSA_HEADS
LANE = 128
MIB = 1 << 20


def _cparams(sem, vmem_mib):
    return pltpu.CompilerParams(dimension_semantics=sem, vmem_limit_bytes=vmem_mib * MIB)


def _sigmoid(x):
    return 1.0 / (1.0 + jnp.exp(-x))


def _dot(a, b):
    return jnp.dot(a, b, preferred_element_type=F32)


def _dot_t(a, b):
    return lax.dot_general(a, b, (((1,), (1,)), ((), ())), preferred_element_type=F32)


def _split_bf16(a):
    hi = a.astype(BF16)
    lo = (a - hi.astype(F32)).astype(BF16)
    return hi, lo


def _dot3(a, b):
    ah, al = _split_bf16(a)
    bh, bl = _split_bf16(b)
    return _dot(ah, bh) + (_dot(ah, bl) + _dot(al, bh))


def _col_chunk(n):
    for c in (512, 256, 128):
        if n % c == 0:
            return c
    return n


def _rms_mm_kernel(x_ref, g_ref, *refs, n_out):
    w_refs, o_refs = refs[:n_out], refs[n_out:]
    x = x_ref[...]
    xn = (x * lax.rsqrt(jnp.mean(x * x, axis=-1, keepdims=True) + EPS) * g_ref[...]).astype(BF16)
    for w_ref, o_ref in zip(w_refs, o_refs):
        n = w_ref.shape[1]
        ch = _col_chunk(n)
        for c in range(0, n, ch):
            o_ref[:, c:c + ch] = _dot(xn, w_ref[:, c:c + ch])


def rms_matmul(x, g, ws, tm):
    n_rows, d = x.shape
    tm = min(tm, n_rows)
    in_specs = [pl.BlockSpec((tm, d), lambda i: (i, 0)), pl.BlockSpec((1, d), lambda i: (0, 0))]
    in_specs += [pl.BlockSpec(w.shape, lambda i: (0, 0), pipeline_mode=pl.Buffered(1)) for w in ws]
    out_specs = [pl.BlockSpec((tm, w.shape[1]), lambda i: (i, 0)) for w in ws]
    out_shape = [jax.ShapeDtypeStruct((n_rows, w.shape[1]), F32) for w in ws]
    return pl.pallas_call(
        functools.partial(_rms_mm_kernel, n_out=len(ws)),
        grid=(n_rows // tm,), in_specs=in_specs, out_specs=out_specs, out_shape=out_shape,
        compiler_params=_cparams(("parallel",), 56), name="rms_matmul",
    )(x, g.reshape(1, d), *ws)


def _mm_res_kernel(*refs, n_in):
    a_refs, (w_ref, r_ref, o_ref) = refs[:n_in], refs[n_in:]
    a = [a_ref[...].astype(BF16) for a_ref in a_refs]
    n = w_ref.shape[1]
    ch = _col_chunk(n)
    for c in range(0, n, ch):
        acc = r_ref[:, c:c + ch]
        k0 = 0
        for x in a:
            acc = acc + _dot(x, w_ref[k0:k0 + x.shape[1], c:c + ch])
            k0 += x.shape[1]
        o_ref[:, c:c + ch] = acc


def matmul_res(a_list, w, res, tm):
    n_rows = res.shape[0]
    k, n = w.shape
    assert sum(a.shape[1] for a in a_list) == k
    tm = min(tm, n_rows)
    return pl.pallas_call(
        functools.partial(_mm_res_kernel, n_in=len(a_list)), grid=(n_rows // tm,),
        in_specs=[pl.BlockSpec((tm, a.shape[1]), lambda i: (i, 0)) for a in a_list]
        + [pl.BlockSpec((k, n), lambda i: (0, 0), pipeline_mode=pl.Buffered(1)),
           pl.BlockSpec((tm, n), lambda i: (i, 0))],
        out_specs=pl.BlockSpec((tm, n), lambda i: (i, 0)),
        out_shape=jax.ShapeDtypeStruct((n_rows, n), F32),
        compiler_params=_cparams(("parallel",), 40), name="matmul_res",
    )(*a_list, w, res)


def _memkv_kernel(x_ref, g_ref, wk_ref, wv_ref, kn_ref, o_ref, ob_ref):
    x = x_ref[...]
    xn = (x * lax.rsqrt(jnp.mean(x * x, axis=-1, keepdims=True) + EPS) * g_ref[...]).astype(BF16)
    xa_w = XA_HEADS * XA_HD
    for h in range(XA_HEADS):
        sl = slice(h * XA_HD, (h + 1) * XA_HD)
        vl = slice(xa_w + h * XA_HD, xa_w + (h + 1) * XA_HD)
        k = _dot(xn, wk_ref[:, sl])
        k = k * lax.rsqrt(jnp.mean(k * k, axis=-1, keepdims=True) + EPS) * kn_ref[...]
        v = _dot(xn, wv_ref[:, sl])
        o_ref[:, sl] = k
        o_ref[:, vl] = v
        ob_ref[:, sl] = k.astype(BF16)
        ob_ref[:, vl] = v.astype(BF16)


def mem_kv_proj(mem, g, wk, wv, kn, tm=256):
    n_rows, d = mem.shape
    xa_w = XA_HEADS * XA_HD
    out = pl.BlockSpec((tm, 2 * xa_w), lambda i: (i, 0))
    return pl.pallas_call(
        _memkv_kernel, grid=(n_rows // tm,),
        in_specs=[pl.BlockSpec((tm, d), lambda i: (i, 0)), pl.BlockSpec((1, d), lambda i: (0, 0)),
                  pl.BlockSpec((d, xa_w), lambda i: (0, 0)), pl.BlockSpec((d, xa_w), lambda i: (0, 0)),
                  pl.BlockSpec((1, XA_HD), lambda i: (0, 0))],
        out_specs=[out, out],
        out_shape=[jax.ShapeDtypeStruct((n_rows, 2 * xa_w), F32), jax.ShapeDtypeStruct((n_rows, 2 * xa_w), BF16)],
        compiler_params=_cparams(("parallel",), 40), name="mem_kv_proj",
    )(mem, g.reshape(1, d), wk, wv, kn.reshape(1, XA_HD))


XA_W = XA_HEADS * XA_HD
MEM_ROWS = 2 * XA_W // LANE


def _xattn_head(q, qn, k, v):
    q = q * lax.rsqrt(jnp.mean(q * q, axis=-1, keepdims=True) + EPS) * qn
    s = _dot_t((q * (XA_HD ** -0.5)).astype(BF16), k)
    p = jnp.exp(s - jnp.max(s, axis=-1, keepdims=True))
    return _dot(p.astype(BF16), v) / jnp.sum(p, axis=-1, keepdims=True)


def _xattn_cache_kernel(q_ref, kv_ref, qn_ref, o_ref, *, m):
    def head_block(kind, h):
        halves = [kv_ref[0, pl.ds(kind * (MEM_ROWS // 2) + half * XA_HEADS + h, m, stride=MEM_ROWS), :]
                  for half in range(XA_HD // LANE)]
        return jnp.concatenate(halves, axis=1).astype(BF16)

    for h in range(XA_HEADS):
        sl = slice(h * XA_HD, (h + 1) * XA_HD)
        o_ref[:, sl] = _xattn_head(q_ref[:, sl], qn_ref[...], head_block(0, h), head_block(1, h))


def xattn_cache(q_pre, kv_rows, qn, n_batch):
    n_rows = q_pre.shape[0]
    tq = n_rows // n_batch
    m = kv_rows.shape[1] // MEM_ROWS
    return pl.pallas_call(
        functools.partial(_xattn_cache_kernel, m=m), grid=(n_batch,),
        in_specs=[pl.BlockSpec((tq, XA_W), lambda b: (b, 0)),
                  pl.BlockSpec((1, m * MEM_ROWS, LANE), lambda b: (b, 0, 0)),
                  pl.BlockSpec((1, XA_HD), lambda b: (0, 0))],
        out_specs=pl.BlockSpec((tq, XA_W), lambda b: (b, 0)),
        out_shape=jax.ShapeDtypeStruct((n_rows, XA_W), F32),
        compiler_params=_cparams(("parallel",), 40), name="xattn_cache",
    )(q_pre, kv_rows, qn.reshape(1, XA_HD))


def _xattn_fused_kernel(h_ref, g_ref, wq_ref, qn_ref, kv_ref, wo_ref, o_ref):
    x = h_ref[...]
    xn = (x * lax.rsqrt(jnp.mean(x * x, axis=-1, keepdims=True) + EPS) * g_ref[...]).astype(BF16)
    heads = []
    for h in range(XA_HEADS):
        sl = slice(h * XA_HD, (h + 1) * XA_HD)
        o = _xattn_head(_dot(xn, wq_ref[:, sl]), qn_ref[...], kv_ref[0, :, sl],
                        kv_ref[0, :, XA_W + h * XA_HD:XA_W + (h + 1) * XA_HD])
        heads.append(o.astype(BF16))
    a = jnp.concatenate(heads, axis=1)
    ch = _col_chunk(D_MODEL)
    for c in range(0, D_MODEL, ch):
        o_ref[:, c:c + ch] = x[:, c:c + ch] + _dot(a, wo_ref[:, c:c + ch])


def xattn_fused(h, g, wq, qn, kv_bf, wo, n_batch, tq):
    n_rows, d = h.shape
    nt = n_rows // n_batch // tq
    m = kv_bf.shape[1]
    const = lambda b, i: (0, 0)
    return pl.pallas_call(
        _xattn_fused_kernel, grid=(n_batch, nt),
        in_specs=[pl.BlockSpec((tq, d), lambda b, i: (b * nt + i, 0)), pl.BlockSpec((1, d), const),
                  pl.BlockSpec((d, XA_W), const, pipeline_mode=pl.Buffered(1)), pl.BlockSpec((1, XA_HD), const),
                  pl.BlockSpec((1, m, 2 * XA_W), lambda b, i: (b, 0, 0)),
                  pl.BlockSpec((XA_W, d), const, pipeline_mode=pl.Buffered(1))],
        out_specs=pl.BlockSpec((tq, d), lambda b, i: (b * nt + i, 0)),
        out_shape=jax.ShapeDtypeStruct((n_rows, d), F32),
        compiler_params=_cparams(("parallel", "parallel"), 40), name="xattn_fused",
    )(h, g.reshape(1, d), wq, qn.reshape(1, XA_HD), kv_bf, wo)


FFN_ACT_CHUNK = 256


def _ffn_kernel(h_ref, g_ref, wup_ref, cw_ref, cb_ref, wdn_ref, prev_ref, o_ref, buf_ref, xs_scr,
                *, tm, shift, base):
    t = pl.program_id(1)
    p0 = base - 2 * shift

    @pl.when(t == 0)
    def _():
        xs_scr[p0:base, :] = prev_ref[0]

    x = h_ref[...]
    xn = (x * lax.rsqrt(jnp.mean(x * x, axis=-1, keepdims=True) + EPS) * g_ref[...]).astype(BF16)
    acc = jnp.zeros((tm, D_MODEL), F32)
    for j in range(D_FF // FFN_ACT_CHUNK):
        halves = []
        for c0 in (j * FFN_ACT_CHUNK, D_FF + j * FFN_ACT_CHUNK):
            sl = slice(c0, c0 + FFN_ACT_CHUNK)
            xs_scr[base:base + tm, sl] = _dot(xn, wup_ref[:, sl])
            y = (cw_ref[0:1, sl] * xs_scr[p0:p0 + tm, sl]
                 + cw_ref[1:2, sl] * xs_scr[p0 + shift:p0 + shift + tm, sl]
                 + cw_ref[2:3, sl] * xs_scr[base:base + tm, sl])
            halves.append(y + cb_ref[:, sl])
        a, u = halves
        act = (a * _sigmoid(a) * u).astype(BF16)
        acc = acc + _dot(act, wdn_ref[j * FFN_ACT_CHUNK:(j + 1) * FFN_ACT_CHUNK, :])
    o_ref[...] = x + acc
    last = xs_scr[base + tm - 2 * shift: base + tm, :]
    buf_ref[0] = last
    xs_scr[p0:base, :] = last


def conv_ffn(h, g, wup, cw, cb, wdn, prev, n_seq, shift, tm):
    n_rows, d = h.shape
    t_rows = n_rows // n_seq
    tm = min(tm, t_rows)
    nt = t_rows // tm
    base = -(-2 * shift // 8) * 8
    kern = functools.partial(_ffn_kernel, tm=tm, shift=shift, base=base)
    const = lambda b, i: (0, 0)
    return pl.pallas_call(
        kern, grid=(n_seq, nt),
        in_specs=[pl.BlockSpec((tm, d), lambda b, i: (b * nt + i, 0)),
                  pl.BlockSpec((1, d), const),
                  pl.BlockSpec((d, 2 * D_FF), const, pipeline_mode=pl.Buffered(1)),
                  pl.BlockSpec((FFN_CONV, 2 * D_FF), const),
                  pl.BlockSpec((1, 2 * D_FF), const),
                  pl.BlockSpec((D_FF, d), const, pipeline_mode=pl.Buffered(1)),
                  pl.BlockSpec((1, 2 * shift, 2 * D_FF), lambda b, i: (b, 0, 0))],
        out_specs=[pl.BlockSpec((tm, d), lambda b, i: (b * nt + i, 0)),
                   pl.BlockSpec((1, 2 * shift, 2 * D_FF), lambda b, i: (b, 0, 0))],
        out_shape=[jax.ShapeDtypeStruct((n_rows, d), F32),
                   jax.ShapeDtypeStruct((n_seq, 2 * shift, 2 * D_FF), F32)],
        scratch_shapes=[pltpu.VMEM((base + tm, 2 * D_FF), F32)],
        compiler_params=_cparams(("arbitrary", "arbitrary"), 56), name="conv_ffn",
    )(h, g.reshape(1, d), wup, cw, cb.reshape(1, 2 * D_FF), wdn, prev)


GDN_STACK = GDN_HEADS * GDN_CHUNK
GDN_C3 = 3 * GDN_W
GDN_PREV0 = 8 - (GDN_CONV - 1)


def _gdn_masks():
    r = np.arange(GDN_STACK)[:, None]
    c = np.arange(GDN_STACK)[None, :]
    same = lambda n: (r // n) == (c // n)
    m = [same(2) & (c < r)]
    for n in (4, 8, 16, 32, 64):
        m.append(same(n) & ~same(n // 2) & (c < r))
    m.append(same(GDN_CHUNK) & (c <= r))
    m.append(same(GDN_CHUNK) & (c < r))
    m.append(r == c)
    return np.stack(m).astype(np.float32)


def _unit_lower_inverse(a, m_ref):
    t = m_ref[8] - a * m_ref[0]
    for lvl in range(1, 6):
        off = a * m_ref[lvl]
        t = t - _dot3(t, _dot3(off, t))
    return t


def _gdn_kernel_v1(qkv_ref, z_ref, ba_ref, prev_ref, s0_ref, cw_ref, par_ref, nw_ref, m_ref,
                   o_ref, buf_ref, s_ref, xs_scr, s_scr, *, tt):
    t = pl.program_id(1)
    nt = pl.num_programs(1)
    row0 = 8

    @pl.when(t == 0)
    def _():
        xs_scr[GDN_PREV0:row0, :] = prev_ref[0]
        s_scr[...] = s0_ref[0]

    xs_scr[row0:row0 + tt, :] = qkv_ref[...]
    causal = m_ref[6]
    neg_ea = -jnp.exp(par_ref[0:1, :])
    dtb = par_ref[1:2, :]
    rv = min(GDN_CHUNK, tt)
    pad = GDN_CHUNK - rv

    def padrows(a):
        if pad == 0:
            return a
        return jnp.concatenate([a, jnp.zeros((pad, a.shape[1]), a.dtype)], axis=0)

    for n in range(-(-tt // GDN_CHUNK)):
        r0 = n * GDN_CHUNK
        y = cw_ref[0:1, :] * xs_scr[GDN_PREV0 + r0:GDN_PREV0 + r0 + rv, :]
        for i in range(1, GDN_CONV):
            y = y + cw_ref[i:i + 1, :] * xs_scr[GDN_PREV0 + i + r0:GDN_PREV0 + i + r0 + rv, :]
        y = y * _sigmoid(y)
        qs, ks, vs, betas, gs = [], [], [], [], []
        for h in range(GDN_HEADS):
            q = y[:, h * GDN_HD:(h + 1) * GDN_HD]
            k = y[:, GDN_W + h * GDN_HD:GDN_W + (h + 1) * GDN_HD]
            v = y[:, 2 * GDN_W + h * GDN_HD:2 * GDN_W + (h + 1) * GDN_HD]
            q = q * lax.rsqrt(jnp.sum(q * q, axis=-1, keepdims=True) + EPS) * (GDN_HD ** -0.5)
            k = k * lax.rsqrt(jnp.sum(k * k, axis=-1, keepdims=True) + EPS)
            bcol = ba_ref[r0:r0 + rv, h:h + 1]
            acol = ba_ref[r0:r0 + rv, GDN_HEADS + h:GDN_HEADS + h + 1] + dtb[:, h:h + 1]
            softplus = jnp.maximum(acol, 0.0) + jnp.log1p(jnp.exp(-jnp.abs(acol)))
            qs.append(padrows(q)); ks.append(padrows(k)); vs.append(padrows(v))
            betas.append(padrows(_sigmoid(bcol)))
            gs.append(padrows(neg_ea[:, h:h + 1] * softplus))
        qm = jnp.concatenate(qs, axis=0)
        km = jnp.concatenate(ks, axis=0)
        vm = jnp.concatenate(vs, axis=0)
        beta = jnp.concatenate(betas, axis=0)
        g = jnp.concatenate(gs, axis=0)
        gc = jnp.dot(causal, jnp.broadcast_to(g, (GDN_STACK, LANE)), preferred_element_type=F32,
                     precision=lax.Precision.HIGHEST)[:, 0:1]
        gc_row = jnp.sum(m_ref[8] * gc, axis=0, keepdims=True)
        diff = gc - gc_row
        decay = jnp.exp(jnp.where(causal > 0, diff, 0.0)) * causal
        kb = km * beta
        kmb = km.astype(BF16)
        a_mat = _dot_t(kb.astype(BF16), kmb) * decay * m_ref[7]
        qk = _dot_t(qm.astype(BF16), kmb) * decay
        tinv = _unit_lower_inverse(a_mat, m_ref)
        egc = jnp.exp(gc)
        sol = _dot3(tinv, jnp.concatenate([vm * beta, kb * egc], axis=1))
        u_all, w_all = sol[:, :GDN_HD], sol[:, GDN_HD:]
        qd = qm * egc
        v_news, o_inter, kds, gls = [], [], [], []
        for h in range(GDN_HEADS):
            rs = slice(h * GDN_CHUNK, (h + 1) * GDN_CHUNK)
            sb = s_scr[h].astype(BF16)
            v_new = u_all[rs] - _dot(w_all[rs].astype(BF16), sb)
            o_inter.append(_dot(qd[rs].astype(BF16), sb))
            gl = gc[h * GDN_CHUNK + GDN_CHUNK - 1:(h + 1) * GDN_CHUNK, :]
            kds.append(km[rs] * jnp.exp(gl - gc[rs]))
            v_news.append(v_new)
            gls.append(gl)
        v_stack = jnp.concatenate(v_news, axis=0).astype(BF16)
        o_intra = _dot(qk.astype(BF16), v_stack)
        for h in range(GDN_HEADS):
            rs = slice(h * GDN_CHUNK, (h + 1) * GDN_CHUNK)
            s_scr[h] = s_scr[h] * jnp.exp(gls[h]) + _dot(kds[h].T.astype(BF16), v_stack[rs])
            o = (o_inter[h] + o_intra[rs])[:rv]
            on = o * lax.rsqrt(jnp.mean(o * o, axis=-1, keepdims=True) + EPS) * nw_ref[...]
            zz = z_ref[r0:r0 + rv, h * GDN_HD:(h + 1) * GDN_HD]
            o_ref[r0:r0 + rv, h * GDN_HD:(h + 1) * GDN_HD] = on * (zz * _sigmoid(zz))

    last = xs_scr[row0 + tt - (GDN_CONV - 1):row0 + tt, :]
    buf_ref[0] = last
    xs_scr[GDN_PREV0:row0, :] = last

    @pl.when(t == nt - 1)
    def _():
        s_ref[0] = s_scr[...]


def _gdn_kernel(qkv_ref, z_ref, ba_ref, prev_ref, s0_ref, cw_ref, par_ref, nw_ref, m_ref,
                o_ref, buf_ref, s_ref, xs_scr, s_scr, *, tt):
    t = pl.program_id(1)
    nt = pl.num_programs(1)
    row0 = 8

    @pl.when(t == 0)
    def _():
        xs_scr[GDN_PREV0:row0, :] = prev_ref[0]
        s_scr[...] = s0_ref[0]

    xs_scr[row0:row0 + tt, :] = qkv_ref[...]
    causal, strict, eye = m_ref[6], m_ref[7], m_ref[8]
    neg_ea = -jnp.exp(par_ref[0:1, :])
    dtb = par_ref[1:2, :]
    rv = min(GDN_CHUNK, tt)
    pad = GDN_CHUNK - rv
    n_ch = -(-tt // GDN_CHUNK)

    def padrows(a):
        if pad == 0:
            return a
        return jnp.concatenate([a, jnp.zeros((pad, a.shape[1]), a.dtype)], axis=0)

    qms, kms, vms, betas, gcols = [], [], [], [], []
    for n in range(n_ch):
        r0 = n * GDN_CHUNK
        y = cw_ref[0:1, :] * xs_scr[GDN_PREV0 + r0:GDN_PREV0 + r0 + rv, :]
        for i in range(1, GDN_CONV):
            y = y + cw_ref[i:i + 1, :] * xs_scr[GDN_PREV0 + i + r0:GDN_PREV0 + i + r0 + rv, :]
        y = y * _sigmoid(y)
        qs, ks, vs, bs, gs = [], [], [], [], []
        for h in range(GDN_HEADS):
            q = y[:, h * GDN_HD:(h + 1) * GDN_HD]
            k = y[:, GDN_W + h * GDN_HD:GDN_W + (h + 1) * GDN_HD]
            v = y[:, 2 * GDN_W + h * GDN_HD:2 * GDN_W + (h + 1) * GDN_HD]
            q = q * lax.rsqrt(jnp.sum(q * q, axis=-1, keepdims=True) + EPS) * (GDN_HD ** -0.5)
            k = k * lax.rsqrt(jnp.sum(k * k, axis=-1, keepdims=True) + EPS)
            bcol = ba_ref[r0:r0 + rv, h:h + 1]
            acol = ba_ref[r0:r0 + rv, GDN_HEADS + h:GDN_HEADS + h + 1] + dtb[:, h:h + 1]
            softplus = jnp.maximum(acol, 0.0) + jnp.log1p(jnp.exp(-jnp.abs(acol)))
            qs.append(padrows(q)); ks.append(padrows(k)); vs.append(padrows(v))
            bs.append(padrows(_sigmoid(bcol)))
            gs.append(padrows(neg_ea[:, h:h + 1] * softplus))
        qms.append(jnp.concatenate(qs, axis=0))
        kms.append(jnp.concatenate(ks, axis=0))
        vms.append(jnp.concatenate(vs, axis=0))
        betas.append(jnp.concatenate(bs, axis=0))
        gcols.append(jnp.concatenate(gs, axis=0))
    gmat = jnp.concatenate(gcols + [jnp.zeros((GDN_STACK, LANE - n_ch), F32)], axis=1)
    gc_all = jnp.dot(causal, gmat, preferred_element_type=F32, precision=lax.Precision.HIGHEST)

    a_mats, qks, rhss, gcs, tinv = [], [], [], [], []
    for n in range(n_ch):
        gc = gc_all[:, n:n + 1]
        gc_row = jnp.sum(eye * gc, axis=0, keepdims=True)
        decay = jnp.exp(jnp.where(causal > 0, gc - gc_row, 0.0)) * causal
        kb = kms[n] * betas[n]
        kmb = kms[n].astype(BF16)
        a_mat = _dot_t(kb.astype(BF16), kmb) * decay * strict
        qks.append((_dot_t(qms[n].astype(BF16), kmb) * decay).astype(BF16))
        rhss.append(jnp.concatenate([vms[n] * betas[n], kb * jnp.exp(gc)], axis=1).astype(BF16))
        a_mats.append(a_mat)
        gcs.append(gc)
        tinv.append(eye - a_mat * m_ref[0])
    for lvl in range(1, 6):
        tb = [x.astype(BF16) for x in tinv]
        xs = [_dot((a_mats[n] * m_ref[lvl]).astype(BF16), tb[n]).astype(BF16) for n in range(n_ch)]
        tinv = [tinv[n] - _dot(tb[n], xs[n]) for n in range(n_ch)]
    sols = [_dot(tinv[n].astype(BF16), rhss[n]) for n in range(n_ch)]

    for n in range(n_ch):
        r0 = n * GDN_CHUNK
        gc = gcs[n]
        egc = jnp.exp(gc)
        u_all, w_all = sols[n][:, :GDN_HD], sols[n][:, GDN_HD:]
        qd = qms[n] * egc
        v_news, o_inter, gls = [], [], []
        for h in range(GDN_HEADS):
            rs = slice(h * GDN_CHUNK, (h + 1) * GDN_CHUNK)
            sb = s_scr[h].astype(BF16)
            both = _dot(jnp.concatenate([w_all[rs], qd[rs]], axis=0).astype(BF16), sb)
            v_news.append(u_all[rs] - both[:GDN_CHUNK])
            o_inter.append(both[GDN_CHUNK:])
            gls.append(gc[h * GDN_CHUNK + GDN_CHUNK - 1:(h + 1) * GDN_CHUNK, :])
        v_stack = jnp.concatenate(v_news, axis=0).astype(BF16)
        o_intra = _dot(qks[n], v_stack)
        for h in range(GDN_HEADS):
            rs = slice(h * GDN_CHUNK, (h + 1) * GDN_CHUNK)
            kd = kms[n][rs] * jnp.exp(gls[h] - gc[rs])
            s_scr[h] = s_scr[h] * jnp.exp(gls[h]) + _dot(kd.T.astype(BF16), v_stack[rs])
            o = (o_inter[h] + o_intra[rs])[:rv]
            on = o * lax.rsqrt(jnp.mean(o * o, axis=-1, keepdims=True) + EPS) * nw_ref[...]
            zz = z_ref[r0:r0 + rv, h * GDN_HD:(h + 1) * GDN_HD]
            o_ref[r0:r0 + rv, h * GDN_HD:(h + 1) * GDN_HD] = on * (zz * _sigmoid(zz))

    last = xs_scr[row0 + tt - (GDN_CONV - 1):row0 + tt, :]
    buf_ref[0] = last
    xs_scr[GDN_PREV0:row0, :] = last

    @pl.when(t == nt - 1)
    def _():
        s_ref[0] = s_scr[...]


def _gdn_masks2(chunk):
    n = GDN_HEADS * chunk
    r = np.arange(n)[:, None]
    c = np.arange(n)[None, :]
    same = lambda k: (r // k) == (c // k)
    m = [same(2) & (c < r)]
    k = 4
    while k <= chunk:
        m.append(same(k) & ~same(k // 2) & (c < r))
        k *= 2
    m += [same(chunk) & (c <= r), same(chunk) & (c < r), r == c]
    return np.stack(m).astype(np.float32)


def _gdn_kernel2(qkv_ref, z_ref, ba_ref, prev_ref, s0_ref, cw_ref, par_ref, nw_ref, m_ref,
                 o_ref, buf_ref, s_ref, xs_scr, s_scr, *, rv, chunk, n_units, sequential):
    t = pl.program_id(1)
    nt = pl.num_programs(1)
    stack = GDN_HEADS * chunk
    n_lvl = m_ref.shape[0] - 3
    causal, strict, eye = m_ref[n_lvl], m_ref[n_lvl + 1], m_ref[n_lvl + 2]
    kc = GDN_CONV - 1
    if sequential:
        base = [8 + u * rv for u in range(n_units)]

        @pl.when(t == 0)
        def _():
            xs_scr[8 - kc:8, :] = prev_ref[0]
            s_scr[...] = s0_ref[...]

        xs_scr[8:8 + n_units * rv, :] = qkv_ref[...]
    else:
        base = [u * (8 + rv) + 8 for u in range(n_units)]
        s_scr[...] = s0_ref[...]
        for u in range(n_units):
            xs_scr[base[u] - kc:base[u], :] = prev_ref[u]
            xs_scr[base[u]:base[u] + rv, :] = qkv_ref[u * rv:(u + 1) * rv, :]
    ba = ba_ref[...]
    beta_all = _sigmoid(ba)
    a_sh = ba + par_ref[1:2, :]
    g_all = -jnp.exp(par_ref[0:1, :]) * (jnp.maximum(a_sh, 0.0) + jnp.log1p(jnp.exp(-jnp.abs(a_sh))))
    pad = chunk - rv

    def padrows(a):
        if pad == 0:
            return a
        return jnp.concatenate([a, jnp.zeros((pad, a.shape[1]), a.dtype)], axis=0)

    qms, kms, vms, betas, gcols = [], [], [], [], []
    for u in range(n_units):
        r0 = u * rv
        y = cw_ref[0:1, :] * xs_scr[base[u] - kc:base[u] - kc + rv, :]
        for i in range(1, GDN_CONV):
            y = y + cw_ref[i:i + 1, :] * xs_scr[base[u] - kc + i:base[u] - kc + i + rv, :]
        y = y * _sigmoid(y)
        qs, ks, vs, bs, gs = [], [], [], [], []
        for h in range(GDN_HEADS):
            q = y[:, h * GDN_HD:(h + 1) * GDN_HD]
            k = y[:, GDN_W + h * GDN_HD:GDN_W + (h + 1) * GDN_HD]
            v = y[:, 2 * GDN_W + h * GDN_HD:2 * GDN_W + (h + 1) * GDN_HD]
            q = q * lax.rsqrt(jnp.sum(q * q, axis=-1, keepdims=True) + EPS) * (GDN_HD ** -0.5)
            k = k * lax.rsqrt(jnp.sum(k * k, axis=-1, keepdims=True) + EPS)
            qs.append(padrows(q)); ks.append(padrows(k)); vs.append(padrows(v))
            bs.append(padrows(beta_all[r0:r0 + rv, h:h + 1]))
            gs.append(padrows(g_all[r0:r0 + rv, GDN_HEADS + h:GDN_HEADS + h + 1]))
        qms.append(jnp.concatenate(qs, axis=0))
        kms.append(jnp.concatenate(ks, axis=0))
        vms.append(jnp.concatenate(vs, axis=0))
        betas.append(jnp.concatenate(bs, axis=0))
        gcols.append(jnp.concatenate(gs, axis=0))
    gmat = jnp.concatenate(gcols + [jnp.zeros((stack, LANE - n_units), F32)], axis=1)
    gc_all = jnp.dot(causal, gmat, preferred_element_type=F32, precision=lax.Precision.HIGHEST)

    a_bfs, qks, rhss, gcs, tinv = [], [], [], [], []
    for u in range(n_units):
        gc = gc_all[:, u:u + 1]
        gc_row = jnp.sum(eye * gc, axis=0, keepdims=True)
        decay = jnp.exp(jnp.where(causal > 0, gc - gc_row, 0.0)) * causal
        kb = kms[u] * betas[u]
        kmb = kms[u].astype(BF16)
        a_mat = _dot_t(kb.astype(BF16), kmb) * decay * strict
        qks.append((_dot_t(qms[u].astype(BF16), kmb) * decay).astype(BF16))
        rhss.append(jnp.concatenate([vms[u] * betas[u], kb * jnp.exp(gc)], axis=1).astype(BF16))
        a_bfs.append(a_mat.astype(BF16))
        gcs.append(gc)
        tinv.append(eye - a_mat * m_ref[0])
    for lvl in range(1, n_lvl):
        tb = [x.astype(BF16) for x in tinv]
        xs = [(_dot(a_bfs[u], tb[u]) * m_ref[lvl]).astype(BF16) for u in range(n_units)]
        tinv = [tinv[u] - _dot(tb[u], xs[u]) for u in range(n_units)]
    sols = [_dot(tinv[u].astype(BF16), rhss[u]) for u in range(n_units)]

    for u in range(n_units):
        r0 = u * rv
        st = 0 if sequential else u
        gc = gcs[u]
        u_all, w_all = sols[u][:, :GDN_HD], sols[u][:, GDN_HD:]
        qd = qms[u] * jnp.exp(gc)
        v_news, o_inter, gls = [], [], []
        for h in range(GDN_HEADS):
            rs = slice(h * chunk, (h + 1) * chunk)
            sb = s_scr[st, h].astype(BF16)
            both = _dot(jnp.concatenate([w_all[rs], qd[rs]], axis=0).astype(BF16), sb)
            v_news.append(u_all[rs] - both[:chunk])
            o_inter.append(both[chunk:])
            gls.append(gc[(h + 1) * chunk - 1:(h + 1) * chunk, :])
        v_stack = jnp.concatenate(v_news, axis=0).astype(BF16)
        o_intra = _dot(qks[u], v_stack)
        for h in range(GDN_HEADS):
            rs = slice(h * chunk, (h + 1) * chunk)
            kd = kms[u][rs] * jnp.exp(gls[h] - gc[rs])
            s_scr[st, h] = s_scr[st, h] * jnp.exp(gls[h]) + _dot(kd.T.astype(BF16), v_stack[rs])
            o = (o_inter[h] + o_intra[rs])[:rv]
            on = o * lax.rsqrt(jnp.mean(o * o, axis=-1, keepdims=True) + EPS) * nw_ref[...]
            zz = z_ref[r0:r0 + rv, h * GDN_HD:(h + 1) * GDN_HD]
            o_ref[r0:r0 + rv, h * GDN_HD:(h + 1) * GDN_HD] = on * (zz * _sigmoid(zz))

    if sequential:
        last = xs_scr[8 + n_units * rv - kc:8 + n_units * rv, :]
        buf_ref[0] = last
        xs_scr[8 - kc:8, :] = last

        @pl.when(t == nt - 1)
        def _():
            s_ref[...] = s_scr[...]
    else:
        for u in range(n_units):
            buf_ref[u] = xs_scr[base[u] + rv - kc:base[u] + rv, :]
        s_ref[...] = s_scr[...]


def gdn2(qkv_pre, z_pre, ba_pre, prev, s0, conv_w, a_log, dt_bias, norm_w, n_seq, units):
    n_rows = qkv_pre.shape[0]
    t_rows = n_rows // n_seq
    sequential = t_rows >= GDN_CHUNK
    if sequential:
        rv = chunk = GDN_CHUNK
        units = min(units, t_rows // chunk)
        grid = (n_seq, t_rows // (units * chunk))
        seq_blk, scr_rows = 1, 8 + units * rv
    else:
        rv, chunk = t_rows, 16
        units = math.gcd(units, n_seq)
        assert rv % 8 == 0 and rv <= chunk
        grid = (n_seq // units, 1)
        seq_blk, scr_rows = units, units * (8 + rv)
    tt = units * rv
    nt = grid[1]
    par = jnp.zeros((8, LANE), F32).at[0, GDN_HEADS:2 * GDN_HEADS].set(a_log).at[1, GDN_HEADS:2 * GDN_HEADS].set(dt_bias)
    masks = jnp.asarray(_gdn_masks2(chunk))
    const2 = lambda b, i: (0, 0)
    rows = lambda b, i: (b * nt + i, 0)
    return pl.pallas_call(
        functools.partial(_gdn_kernel2, rv=rv, chunk=chunk, n_units=units, sequential=sequential), grid=grid,
        in_specs=[pl.BlockSpec((tt, GDN_C3), rows), pl.BlockSpec((tt, GDN_W), rows),
                  pl.BlockSpec((tt, LANE), rows),
                  pl.BlockSpec((seq_blk, GDN_CONV - 1, GDN_C3), lambda b, i: (b, 0, 0)),
                  pl.BlockSpec((seq_blk, GDN_HEADS, GDN_HD, GDN_HD), lambda b, i: (b, 0, 0, 0)),
                  pl.BlockSpec((GDN_CONV, GDN_C3), const2), pl.BlockSpec((8, LANE), const2),
                  pl.BlockSpec((1, GDN_HD), const2),
                  pl.BlockSpec(masks.shape, lambda b, i: (0, 0, 0))],
        out_specs=[pl.BlockSpec((tt, GDN_W), rows),
                   pl.BlockSpec((seq_blk, GDN_CONV - 1, GDN_C3), lambda b, i: (b, 0, 0)),
                   pl.BlockSpec((seq_blk, GDN_HEADS, GDN_HD, GDN_HD), lambda b, i: (b, 0, 0, 0))],
        out_shape=[jax.ShapeDtypeStruct((n_rows, GDN_W), F32),
                   jax.ShapeDtypeStruct((n_seq, GDN_CONV - 1, GDN_C3), F32),
                   jax.ShapeDtypeStruct((n_seq, GDN_HEADS, GDN_HD, GDN_HD), F32)],
        scratch_shapes=[pltpu.VMEM((scr_rows, GDN_C3), F32),
                        pltpu.VMEM((seq_blk, GDN_HEADS, GDN_HD, GDN_HD), F32)],
        compiler_params=_cparams(("arbitrary", "arbitrary"), 48), name="gdn",
    )(qkv_pre, z_pre, ba_pre, prev, s0, conv_w, par, norm_w.reshape(1, GDN_HD), masks)


def gdn(qkv_pre, z_pre, ba_pre, prev, s0, conv_w, a_log, dt_bias, norm_w, n_seq, tt):
    n_rows = qkv_pre.shape[0]
    t_rows = n_rows // n_seq
    tt = min(tt, t_rows)
    nt = t_rows // tt
    par = jnp.zeros((8, LANE), F32).at[0, :GDN_HEADS].set(a_log).at[1, :GDN_HEADS].set(dt_bias)
    masks = jnp.asarray(_gdn_masks())
    const2 = lambda b, i: (0, 0)
    rows = lambda b, i: (b * nt + i, 0)
    return pl.pallas_call(
        functools.partial(_gdn_kernel, tt=tt), grid=(n_seq, nt),
        in_specs=[pl.BlockSpec((tt, GDN_C3), rows), pl.BlockSpec((tt, GDN_W), rows),
                  pl.BlockSpec((tt, LANE), rows),
                  pl.BlockSpec((1, GDN_CONV - 1, GDN_C3), lambda b, i: (b, 0, 0)),
                  pl.BlockSpec((1, GDN_HEADS, GDN_HD, GDN_HD), lambda b, i: (b, 0, 0, 0)),
                  pl.BlockSpec((GDN_CONV, GDN_C3), const2), pl.BlockSpec((8, LANE), const2),
                  pl.BlockSpec((1, GDN_HD), const2),
                  pl.BlockSpec(masks.shape, lambda b, i: (0, 0, 0))],
        out_specs=[pl.BlockSpec((tt, GDN_W), rows),
                   pl.BlockSpec((1, GDN_CONV - 1, GDN_C3), lambda b, i: (b, 0, 0)),
                   pl.BlockSpec((1, GDN_HEADS, GDN_HD, GDN_HD), lambda b, i: (b, 0, 0, 0))],
        out_shape=[jax.ShapeDtypeStruct((n_rows, GDN_W), F32),
                   jax.ShapeDtypeStruct((n_seq, GDN_CONV - 1, GDN_C3), F32),
                   jax.ShapeDtypeStruct((n_seq, GDN_HEADS, GDN_HD, GDN_HD), F32)],
        scratch_shapes=[pltpu.VMEM((8 + tt, GDN_C3), F32), pltpu.VMEM((GDN_HEADS, GDN_HD, GDN_HD), F32)],
        compiler_params=_cparams(("arbitrary", "arbitrary"), 48), name="gdn",
    )(qkv_pre, z_pre, ba_pre, prev, s0, conv_w, par, norm_w.reshape(1, GDN_HD), masks)


CMP_W = 2 * NSA_KV_HEADS * NSA_HD


def _cmp_weights(cmp_pe, cmp_w):
    def bd(w):
        z = jnp.zeros_like(w)
        return jnp.concatenate([jnp.concatenate([w, z], 2), jnp.concatenate([z, w], 2)], 1)
    ws, cs = [], []
    for s in range(2):
        lo, hi = bd(cmp_w[s, :CMP_STRIDE]), bd(cmp_w[s, CMP_STRIDE:])
        cat = jnp.concatenate([lo, hi], axis=2)
        ws.append(cat.reshape(CMP_STRIDE // 2, 2 * LANE, 2 * LANE))
        pe2 = jnp.concatenate([cmp_pe[s], cmp_pe[s]], axis=-1)
        const = [jnp.einsum("jd,jde->e", pe2[h * CMP_STRIDE:(h + 1) * CMP_STRIDE], m,
                            precision=lax.Precision.HIGHEST) for h, m in enumerate((lo, hi))]
        cs.append(jnp.concatenate(const))
    return jnp.stack(ws).astype(BF16), jnp.stack(cs)


def _compress(read_k, read_v, w_ref, c_ref, n_chunk):
    outs = []
    for s, read in enumerate((read_k, read_v)):
        acc = jnp.zeros((n_chunk, 2 * LANE), F32)
        for p in range(CMP_STRIDE // 2):
            x = jnp.concatenate([read(2 * p), read(2 * p + 1)], axis=1).astype(BF16)
            acc = acc + _dot(x, w_ref[s, p])
        acc = acc + c_ref[s:s + 1, :]
        outs.append(acc[:, :LANE] + pltpu.roll(acc[:, LANE:], n_chunk - 1, axis=0))
    return jnp.concatenate(outs, axis=1)


def _cmp_kernel(k_ref, v_ref, w_ref, pe_ref, o_ref, *, n_chunk):
    o_ref[0] = _compress(lambda j: k_ref[0, pl.ds(j, n_chunk, stride=CMP_STRIDE), :],
                         lambda j: v_ref[0, pl.ds(j, n_chunk, stride=CMP_STRIDE), :], w_ref, pe_ref, n_chunk)


def compress_prompt(rows, w, pe):
    b, t, _ = rows.shape
    n_chunk = t // CMP_STRIDE
    return pl.pallas_call(
        functools.partial(_cmp_kernel, n_chunk=n_chunk), grid=(b,),
        in_specs=[pl.BlockSpec((1, t, LANE), lambda i: (i, 0, 0)), pl.BlockSpec((1, t, LANE), lambda i: (i, 0, 1)),
                  pl.BlockSpec(w.shape, lambda i: (0, 0, 0, 0)),
                  pl.BlockSpec(pe.shape, lambda i: (0, 0))],
        out_specs=pl.BlockSpec((1, n_chunk, CMP_W), lambda i: (i, 0, 0)),
        out_shape=jax.ShapeDtypeStruct((b, n_chunk, CMP_W), F32),
        compiler_params=_cparams(("parallel",), 32), name="nsa_compress",
    )(rows, rows, w, pe)


def _c2s(n_cmp_pad, n_sel_pad):
    cs = np.arange(n_cmp_pad)[:, None] * CMP_STRIDE
    ss = np.arange(n_sel_pad)[None, :] * SEL_BLOCK
    return ((cs < ss + SEL_BLOCK) & (cs + CMP_BLOCK > ss)).astype(np.float32)


def _expand(n_sel_pad, n_keys):
    s = np.arange(n_sel_pad)[:, None]
    k = np.arange(n_keys)[None, :]
    return (k // SEL_BLOCK == s).astype(np.float32)


def _cmp_probs(s, valid):
    s = jnp.where(valid, s, NEG)
    p = jnp.where(valid, jnp.exp(s - jnp.max(s, axis=-1, keepdims=True)), 0.0)
    return p / jnp.maximum(jnp.sum(p, axis=-1, keepdims=True), 1e-30)


def _select_blocks(imp, q_blk, n_sel):
    lane = lax.broadcasted_iota(jnp.int32, imp.shape, 1)
    visible = lane <= q_blk
    forced = (lane == 0) | (lane == q_blk) | (lane == q_blk - 1)
    score = jnp.where(visible, imp + jnp.where(forced, FORCED_BONUS, 0.0), NEG)
    rank = jnp.zeros(imp.shape, F32)
    for sp in range(n_sel):
        col = score[:, sp:sp + 1]
        beats = (col > score) | ((col == score) & (lane > sp))
        rank = rank + jnp.where(beats, 1.0, 0.0)
    return jnp.where(visible & (rank < SEL_TOPN), 1.0, 0.0)


def _flash(q, k_ref, v_ref, j_lo, j_hi, mask_fn, tq, tk):
    def body(j, carry):
        m, l, acc = carry
        off = pl.multiple_of(j * tk, tk)
        s = _dot_t(q, k_ref[pl.ds(off, tk), :])
        s = jnp.where(mask_fn(j), s, NEG)
        m_new = jnp.maximum(m, jnp.max(s, axis=-1, keepdims=True))
        alpha = jnp.exp(m - m_new)
        p = jnp.exp(s - m_new)
        l = alpha * l + jnp.sum(p, axis=-1, keepdims=True)
        acc = alpha * acc + _dot(p.astype(BF16), v_ref[pl.ds(off, tk), :])
        return m_new, l, acc

    init = (jnp.full((tq, 1), NEG, F32), jnp.zeros((tq, 1), F32), jnp.zeros((tq, NSA_HD), F32))
    _, l, acc = lax.fori_loop(j_lo, j_hi, body, init)
    return acc / l


def _nsa_prompt_kernel_v1(qc_ref, qr_ref, kc_ref, vc_ref, ks_ref, vs_ref, kw_ref, vw_ref, g_ref, c2s_ref, e_ref,
                       o_ref, mask_scr, *, tq, n_sel, n_cmp_pad):
    i = pl.program_id(2)
    t0 = i * tq
    qpos = t0 + lax.broadcasted_iota(jnp.int32, (tq, 1), 0)
    cend = lax.broadcasted_iota(jnp.int32, (1, n_cmp_pad), 1) * CMP_STRIDE + (CMP_BLOCK - 1)
    valid_c = cend <= qpos
    psum = jnp.zeros((tq, n_cmp_pad), F32)
    o_cmp = []
    for g in range(NSA_GROUP):
        p = _cmp_probs(_dot_t(qc_ref[0, g], kc_ref[0, 0]), valid_c)
        psum = psum + p
        o_cmp.append(_dot(p.astype(BF16), vc_ref[0, 0]))
    imp = jnp.dot(psum, c2s_ref[...], preferred_element_type=F32, precision=lax.Precision.HIGHEST)
    sel = _select_blocks(imp, lax.shift_right_logical(qpos, 6), n_sel).astype(BF16)
    n_kt = mask_scr.shape[0]
    for jj in range(n_kt):
        mask_scr[jj] = _dot(sel, e_ref[:, jj * tq:(jj + 1) * tq])
    kiota = lax.broadcasted_iota(jnp.int32, (1, tq), 1)

    def sel_mask(j):
        return (mask_scr[j] > 0.5) & (j * tq + kiota <= qpos)

    def win_mask(j):
        rel = qpos - (j * tq + kiota)
        return (rel >= 0) & (rel < WINDOW)

    w_tiles = WINDOW // tq
    for g in range(NSA_GROUP):
        q = qr_ref[0, g]
        o_sel = _flash(q, ks_ref.at[0, 0], vs_ref.at[0, 0], 0, i + 1, sel_mask, tq, tq)
        o_win = _flash(q, kw_ref.at[0, 0], vw_ref.at[0, 0], jnp.maximum(i - w_tiles, 0), i + 1, win_mask, tq, tq)
        gt = g_ref[0, g]
        o = o_cmp[g] * gt[:, 0:1] + o_sel * gt[:, 1:2] + o_win * gt[:, 2:3]
        o_ref[0, :, g * NSA_HD:(g + 1) * NSA_HD] = o


def nsa_prompt_attn_v1(qc, qr, kcmp, vcmp, ksel, vsel, kwin, vwin, gates, tq):
    b, _, t, _ = qc.shape
    tq = min(tq, t)
    n_cmp_pad = kcmp.shape[2]
    n_sel = t // SEL_BLOCK
    n_sel_pad = -(-n_sel // LANE) * LANE
    c2s = jnp.asarray(_c2s(n_cmp_pad, n_sel_pad))
    e = jnp.asarray(_expand(n_sel_pad, t), dtype=BF16)
    kern = functools.partial(_nsa_prompt_kernel_v1, tq=tq, n_sel=n_sel, n_cmp_pad=n_cmp_pad)
    qspec = pl.BlockSpec((1, NSA_GROUP, tq, NSA_HD), lambda bb, k, i: (bb, k, i, 0))
    kvspec = lambda n: pl.BlockSpec((1, 1, n, NSA_HD), lambda bb, k, i: (bb, k, 0, 0))
    return pl.pallas_call(
        kern, grid=(b, NSA_KV_HEADS, t // tq),
        in_specs=[qspec, qspec, kvspec(n_cmp_pad), kvspec(n_cmp_pad), kvspec(t), kvspec(t), kvspec(t), kvspec(t),
                  pl.BlockSpec((1, NSA_GROUP, tq, 3), lambda bb, k, i: (bb, k, i, 0)),
                  pl.BlockSpec(c2s.shape, lambda bb, k, i: (0, 0)), pl.BlockSpec(e.shape, lambda bb, k, i: (0, 0))],
        out_specs=pl.BlockSpec((1, tq, NSA_GROUP * NSA_HD), lambda bb, k, i: (bb, i, k)),
        out_shape=jax.ShapeDtypeStruct((b, t, NSA_Q_W), F32),
        scratch_shapes=[pltpu.VMEM((t // tq, tq, tq), F32)],
        compiler_params=_cparams(("parallel", "parallel", "parallel"), 40), name="nsa_prompt_attn",
    )(qc, qr, kcmp, vcmp, ksel, vsel, kwin, vwin, gates, c2s, e)


def _pair_rms(x, w):
    lo = lax.broadcasted_iota(jnp.int32, x.shape, 1) < NSA_HD
    x2 = x * x
    s_lo = jnp.sum(jnp.where(lo, x2, 0.0), axis=-1, keepdims=True)
    s_hi = jnp.sum(jnp.where(lo, 0.0, x2), axis=-1, keepdims=True)
    ms = jnp.where(lo, s_lo, s_hi) * (1.0 / NSA_HD)
    return x * lax.rsqrt(ms + EPS) * w


def _pair_rope(x, cos_t, sin_t):
    lane = lax.broadcasted_iota(jnp.int32, x.shape, 1)
    first = (lane & (NSA_HD - 1)) < NSA_HD // 2
    partner = jnp.where(first, pltpu.roll(x, LANE - NSA_HD // 2, axis=1), pltpu.roll(x, NSA_HD // 2, axis=1))
    return x * cos_t + partner * sin_t


def _rope_tables(pos):
    half = NSA_HD // 2
    inv = jnp.power(ROPE_THETA, -jnp.arange(half, dtype=F32) / half)
    ang = pos.astype(F32)[:, None] * inv[None, :]
    c, s = jnp.cos(ang), jnp.sin(ang)
    return jnp.concatenate([c, c, c, c], axis=-1), jnp.concatenate([-s, s, -s, s], axis=-1)


def _nsa_prep_kernel_v2(q_ref, kv_ref, g_ref, cos_ref, sin_ref, qn_ref, kn_ref,
                        rows_ref, wrows_ref, qc_ref, qr_ref, gate_ref, kvb_ref=None, cmp_ref=None):
    cos_t, sin_t = cos_ref[...], sin_ref[...]
    lo = lax.broadcasted_iota(jnp.int32, cos_t.shape, 1) < NSA_HD
    for c in range(NSA_HEADS // 2):
        y = _pair_rms(q_ref[:, c * LANE:(c + 1) * LANE], qn_ref[...]) * (NSA_HD ** -0.5)
        in_low_lanes = (2 * c) // NSA_GROUP == 0
        for src, dst in ((y, qc_ref), (_pair_rope(y, cos_t, sin_t) * LOG2E, qr_ref)):
            swapped = pltpu.roll(src, NSA_HD, axis=1)
            if in_low_lanes:
                even, odd = jnp.where(lo, src, 0.0), jnp.where(lo, swapped, 0.0)
            else:
                even, odd = jnp.where(lo, 0.0, swapped), jnp.where(lo, 0.0, src)
            dst[:, (2 * c) * LANE:(2 * c + 1) * LANE] = even.astype(BF16)
            dst[:, (2 * c + 1) * LANE:(2 * c + 2) * LANE] = odd.astype(BF16)
    for br in range(3):
        k = _pair_rms(kv_ref[:, br * 2 * LANE:br * 2 * LANE + LANE], kn_ref[br:br + 1, :])
        if br > 0:
            k = _pair_rope(k, cos_t, sin_t)
        v = kv_ref[:, br * 2 * LANE + LANE:(br + 1) * 2 * LANE]
        if cmp_ref is None:
            dst, off = (rows_ref, br * 2 * LANE) if br < 2 else (wrows_ref, 0)
            dst[:, off:off + LANE] = k
            dst[:, off + LANE:off + 2 * LANE] = v
        else:
            dst, w0 = (rows_ref, 2 * br) if br < 2 else (wrows_ref, 0)
            for w, x in ((w0, k), (w0 + 1, v)):
                xt = x.T
                for h in range(NSA_KV_HEADS):
                    dst[0, w, h] = xt[h * NSA_HD:(h + 1) * NSA_HD]
            if br == 0:
                cmp_ref[:, :LANE] = k
                cmp_ref[:, LANE:] = v
            else:
                kvb_ref[:, (br - 1) * LANE:br * LANE] = k.astype(BF16)
                kvb_ref[:, 2 * br * LANE:(2 * br + 1) * LANE] = jnp.where(lo, v, 1.0).astype(BF16)
                kvb_ref[:, (2 * br + 1) * LANE:(2 * br + 2) * LANE] = jnp.where(lo, 1.0, v).astype(BF16)
    gate_ref[...] = _sigmoid(g_ref[...])


def nsa_prep_v2(q_pre, kv_pre, gate_pre, cos_t, sin_t, qn, kn, tm, dims_major):
    n_rows = q_pre.shape[0]
    tm = min(tm, n_rows)
    n_tab = cos_t.shape[0] // tm
    rows = lambda w: pl.BlockSpec((tm, w), lambda i: (i, 0))
    tab = pl.BlockSpec((tm, LANE), lambda i: (i % n_tab, 0))
    widths = (NSA_HEADS * LANE, NSA_HEADS * LANE, LANE)
    dtypes = (BF16, BF16, F32)
    out_specs = [rows(w) for w in widths]
    out_shape = [jax.ShapeDtypeStruct((n_rows, w), dt) for w, dt in zip(widths, dtypes)]
    if dims_major:
        n_seq, t = n_rows // cos_t.shape[0], cos_t.shape[0]
        kv_spec = lambda n: pl.BlockSpec((1, n, NSA_KV_HEADS, NSA_HD, tm), lambda i: (i // n_tab, 0, 0, 0, i % n_tab))
        kv_shape = lambda n: jax.ShapeDtypeStruct((n_seq, n, NSA_KV_HEADS, NSA_HD, t), F32)
        out_specs = [kv_spec(4), kv_spec(2)] + out_specs + [rows(NSA_KV_W), rows(2 * LANE)]
        out_shape = [kv_shape(4), kv_shape(2)] + out_shape + [jax.ShapeDtypeStruct((n_rows, NSA_KV_W), BF16),
                                                              jax.ShapeDtypeStruct((n_rows, 2 * LANE), F32)]
    else:
        out_specs = [rows(NSA_Q_W), rows(2 * LANE)] + out_specs
        out_shape = [jax.ShapeDtypeStruct((n_rows, NSA_Q_W), F32),
                     jax.ShapeDtypeStruct((n_rows, 2 * LANE), F32)] + out_shape
    return pl.pallas_call(
        _nsa_prep_kernel_v2, grid=(n_rows // tm,),
        in_specs=[rows(NSA_Q_W), rows(NSA_KV_W), rows(LANE), tab, tab,
                  pl.BlockSpec((1, LANE), lambda i: (0, 0)), pl.BlockSpec((3, LANE), lambda i: (0, 0))],
        out_specs=out_specs, out_shape=out_shape,
        compiler_params=_cparams(("parallel",), 40), name="nsa_prep",
    )(q_pre, kv_pre, gate_pre, cos_t, sin_t, jnp.tile(qn.reshape(1, NSA_HD), (1, 2)), jnp.tile(kn, (1, 2)))


def _select_blocks_t(imp_t, q_blk, n_sel):
    blk = lax.broadcasted_iota(jnp.int32, imp_t.shape, 0)
    visible = blk <= q_blk
    forced = jnp.where(blk == 0, 1.0, 0.0) + jnp.where(blk == q_blk, 1.0, 0.0) + jnp.where(blk == q_blk - 1, 1.0, 0.0)
    score = jnp.where(visible, imp_t + jnp.where(forced > 0.0, FORCED_BONUS, 0.0), NEG)
    rank = jnp.zeros(imp_t.shape, F32)
    for sp in range(n_sel):
        row = score[sp:sp + 1, :]
        tie = jnp.where(blk > sp, jnp.where(row == score, 1.0, 0.0), 0.0)
        rank = rank + jnp.where(row > score, 1.0, 0.0) + tie
    return jnp.where(visible, jnp.where(rank < SEL_TOPN, 1.0, 0.0), 0.0)


def _flash8_v2(qs, k_ref, v_refs, j_lo, j_hi, bias_fn, tq, tk):
    def body(j, carry):
        off = pl.multiple_of(j * tk, tk)
        k = k_ref[pl.ds(off, tk), :]
        vs = [v_ref[pl.ds(off, tk), :] for v_ref in v_refs]
        bias = bias_fn(j)
        out = []
        for h, (m, acc) in enumerate(carry):
            s = _dot_t(qs[h], k).astype(BF16) + bias[h // NSA_GROUP]
            m_new = jnp.maximum(m, jnp.max(s, axis=-1, keepdims=True).astype(F32))
            p = jnp.exp2(s - m_new.astype(BF16))
            acc = jnp.exp2(m - m_new) * acc + _dot(p, vs[h // NSA_GROUP])
            out.append((m_new, acc))
        return tuple(out)

    init = tuple((jnp.full((tq, 1), NEG, F32), jnp.zeros((tq, LANE), F32)) for _ in range(NSA_HEADS))
    outs = []
    for h, (_, acc) in enumerate(lax.fori_loop(j_lo, j_hi, body, init)):
        c = NSA_HD if h // NSA_GROUP == 0 else 0
        outs.append(acc / acc[:, c:c + 1])
    return outs


def _nsa_prompt_kernel_v2(qc_ref, qr_ref, kvc_ref, ks_ref, kw_ref, vs0_ref, vs1_ref, vw0_ref, vw1_ref, g_ref,
                          c2s_ref, e_ref, o_ref, *, tq, n_sel):
    i = pl.program_id(1)
    t0 = i * tq
    n_rows = NSA_HEADS * tq
    stack = lambda ref: jnp.concatenate([ref[:, h * LANE:(h + 1) * LANE] for h in range(NSA_HEADS)], axis=0)
    qpos = t0 + lax.broadcasted_iota(jnp.int32, (tq, 1), 0)
    n_cmp = kvc_ref.shape[1]
    kc = kvc_ref[0, :, :LANE].astype(BF16)
    vc = kvc_ref[0, :, LANE:].astype(BF16)
    cend = lax.broadcasted_iota(jnp.int32, (1, n_cmp), 1) * CMP_STRIDE + (CMP_BLOCK - 1)
    s = _dot_t(stack(qc_ref), kc).reshape(NSA_KV_HEADS, NSA_GROUP, tq, n_cmp)
    p = _cmp_probs(s, (cend <= qpos)[None, None])
    o_cmp = _dot(p.reshape(n_rows, n_cmp).astype(BF16), vc)
    psum = p[:, 0]
    for g in range(1, NSA_GROUP):
        psum = psum + p[:, g]
    imp = jnp.dot(psum.reshape(NSA_KV_HEADS * tq, n_cmp), c2s_ref[...], preferred_element_type=F32,
                  precision=lax.Precision.HIGHEST)
    n_blk_rows = -(-n_sel // 8) * 8
    col = lax.broadcasted_iota(jnp.int32, (1, NSA_KV_HEADS * tq), 1)
    q_blk = lax.shift_right_logical(t0 + (col & (tq - 1)), 6)
    sel_t = _select_blocks_t(imp.T[:n_blk_rows], q_blk, n_sel)
    sel_t = jnp.concatenate([sel_t, jnp.zeros((LANE - n_blk_rows, NSA_KV_HEADS * tq), F32)], axis=0)
    sel = sel_t.T.astype(BF16)
    kiota = lax.broadcasted_iota(jnp.int32, (1, tq), 1)

    def sel_bias(j):
        causal = (j * tq + kiota) <= qpos
        return [((jnp.where(causal, _dot(sel[k * tq:(k + 1) * tq], e_ref[j]), 0.0) - 1.0) * -NEG).astype(BF16)
                for k in range(NSA_KV_HEADS)]

    def win_bias(j):
        rel = qpos - (j * tq + kiota)
        return [jnp.where(jnp.where(rel >= 0, rel, WINDOW) < WINDOW, 0.0, NEG).astype(BF16)] * NSA_KV_HEADS

    qr = [qr_ref[:, h * LANE:(h + 1) * LANE] for h in range(NSA_HEADS)]
    o_sel = _flash8_v2(qr, ks_ref.at[0], [vs0_ref.at[0], vs1_ref.at[0]], 0, i + 1, sel_bias, tq, tq)
    o_win = _flash8_v2(qr, kw_ref.at[0], [vw0_ref.at[0], vw1_ref.at[0]], jnp.maximum(i - WINDOW // tq, 0), i + 1,
                       win_bias, tq, tq)
    lo = lax.broadcasted_iota(jnp.int32, (tq, LANE), 1) < NSA_HD
    for c in range(NSA_HEADS // 2):
        pair = []
        for h in (2 * c, 2 * c + 1):
            rs = slice(h * tq, (h + 1) * tq)
            gt = g_ref[:, 3 * h:3 * h + 3]
            pair.append(o_cmp[rs] * gt[:, 0:1] + o_sel[h] * gt[:, 1:2] + o_win[h] * gt[:, 2:3])
        even, odd = pair
        if (2 * c) // NSA_GROUP == 0:
            blk = jnp.where(lo, even, pltpu.roll(odd, NSA_HD, axis=1))
        else:
            blk = jnp.where(lo, pltpu.roll(even, NSA_HD, axis=1), odd)
        o_ref[:, c * LANE:(c + 1) * LANE] = blk


def nsa_prompt_attn_v2(qc, qr, kvc, kvb, gates, n_batch, tq):
    n_rows = qc.shape[0]
    t = n_rows // n_batch
    tq = min(tq, t)
    nt = t // tq
    n_cmp = kvc.shape[1]
    n_sel = t // SEL_BLOCK
    assert n_cmp % LANE == 0 and n_sel <= LANE and tq & (tq - 1) == 0 and WINDOW % tq == 0
    c2s = jnp.asarray(_c2s(n_cmp, LANE))
    e = jnp.asarray(_expand(LANE, t).reshape(LANE, nt, tq).transpose(1, 0, 2), dtype=BF16)
    kvb3 = kvb.reshape(n_batch, t, NSA_KV_W)
    rows = lambda w: pl.BlockSpec((tq, w), lambda b, i: (b * nt + i, 0))
    kv = lambda c: pl.BlockSpec((1, t, LANE), lambda b, i: (b, 0, c))
    return pl.pallas_call(
        functools.partial(_nsa_prompt_kernel_v2, tq=tq, n_sel=n_sel), grid=(n_batch, nt),
        in_specs=[rows(NSA_HEADS * LANE), rows(NSA_HEADS * LANE),
                  pl.BlockSpec((1, n_cmp, CMP_W), lambda b, i: (b, 0, 0)),
                  kv(0), kv(1), kv(2), kv(3), kv(4), kv(5), rows(LANE),
                  pl.BlockSpec(c2s.shape, lambda b, i: (0, 0)), pl.BlockSpec(e.shape, lambda b, i: (0, 0, 0))],
        out_specs=rows(NSA_Q_W),
        out_shape=jax.ShapeDtypeStruct((n_rows, NSA_Q_W), F32),
        compiler_params=_cparams(("parallel", "parallel"), 48), name="nsa_prompt_attn",
    )(qc, qr, kvc, kvb3, kvb3, kvb3, kvb3, kvb3, kvb3, gates, c2s, e)


LOG2E = 1.4426950408889634
KEY_TILE = 256


def _head_rows(yt, head_in_pair, kv_head):
    own = yt[head_in_pair * NSA_HD:(head_in_pair + 1) * NSA_HD]
    zero = jnp.zeros_like(own)
    return jnp.concatenate([own, zero] if kv_head == 0 else [zero, own], axis=0)


def _nsa_prep_kernel(q_ref, kv_ref, g_ref, cos_ref, sin_ref, qn_ref, kn_ref,
                     rows_ref, wrows_ref, qct_ref, qrt_ref, gt_ref, kb_ref=None, vt_ref=None, *, tm):
    cos_t, sin_t = cos_ref[...], sin_ref[...]
    for c in range(NSA_HEADS // 2):
        y = _pair_rms(q_ref[:, c * LANE:(c + 1) * LANE], qn_ref[...]) * (NSA_HD ** -0.5)
        yr = _pair_rope(y, cos_t, sin_t) * LOG2E
        for src, dst in ((y, qct_ref), (yr, qrt_ref)):
            st = src.T
            for e in range(2):
                h = 2 * c + e
                dst[h * LANE:(h + 1) * LANE, :] = _head_rows(st, e, h // NSA_GROUP).astype(BF16)
    for br in range(3):
        k = _pair_rms(kv_ref[:, br * 2 * LANE:br * 2 * LANE + LANE], kn_ref[br:br + 1, :])
        if br > 0:
            k = _pair_rope(k, cos_t, sin_t)
        v = kv_ref[:, br * 2 * LANE + LANE:(br + 1) * 2 * LANE]
        dst, off = (rows_ref, br * 2 * LANE) if br < 2 else (wrows_ref, 0)
        dst[:, off:off + LANE] = k
        dst[:, off + LANE:off + 2 * LANE] = v
        if br > 0 and kb_ref is not None:
            kb_ref[:, (br - 1) * LANE:br * LANE] = k.astype(BF16)
            vt = v.T.astype(BF16)
            for jt in range(tm // KEY_TILE):
                vt_ref[br - 1, jt] = vt[:, jt * KEY_TILE:(jt + 1) * KEY_TILE]
    gt_ref[...] = _sigmoid(g_ref[...]).T


def nsa_prep(q_pre, kv_pre, gate_pre, cos_t, sin_t, qn, kn, tm, attn_operands):
    n_rows = q_pre.shape[0]
    tm = min(tm, n_rows)
    n_tab = cos_t.shape[0] // tm
    rows = lambda w: pl.BlockSpec((tm, w), lambda i: (i, 0))
    cols = lambda r: pl.BlockSpec((r, tm), lambda i: (0, i))
    tab = pl.BlockSpec((tm, LANE), lambda i: (i % n_tab, 0))
    out_specs = [rows(NSA_Q_W), rows(2 * LANE), cols(NSA_HEADS * LANE), cols(NSA_HEADS * LANE), cols(LANE)]
    out_shape = [jax.ShapeDtypeStruct((n_rows, NSA_Q_W), F32), jax.ShapeDtypeStruct((n_rows, 2 * LANE), F32),
                 jax.ShapeDtypeStruct((NSA_HEADS * LANE, n_rows), BF16),
                 jax.ShapeDtypeStruct((NSA_HEADS * LANE, n_rows), BF16),
                 jax.ShapeDtypeStruct((LANE, n_rows), F32)]
    if attn_operands:
        assert tm % KEY_TILE == 0
        out_specs += [rows(2 * LANE), pl.BlockSpec((2, tm // KEY_TILE, LANE, KEY_TILE), lambda i: (0, i, 0, 0))]
        out_shape += [jax.ShapeDtypeStruct((n_rows, 2 * LANE), BF16),
                      jax.ShapeDtypeStruct((2, n_rows // KEY_TILE, LANE, KEY_TILE), BF16)]
    return pl.pallas_call(
        functools.partial(_nsa_prep_kernel, tm=tm), grid=(n_rows // tm,),
        in_specs=[rows(NSA_Q_W), rows(NSA_KV_W), rows(LANE), tab, tab,
                  pl.BlockSpec((1, LANE), lambda i: (0, 0)), pl.BlockSpec((3, LANE), lambda i: (0, 0))],
        out_specs=out_specs, out_shape=out_shape,
        compiler_params=_cparams(("parallel",), 40), name="nsa_prep",
    )(q_pre, kv_pre, gate_pre, cos_t, sin_t, jnp.tile(qn.reshape(1, NSA_HD), (1, 2)), jnp.tile(kn, (1, 2)))


def _flash_t(qts, k_ref, vt_ref, j_lo, j_hi, valid_fn, tq):
    def body(j, carry):
        k = k_ref[pl.ds(pl.multiple_of(j * KEY_TILE, KEY_TILE), KEY_TILE), :]
        vt = vt_ref[j]
        valid = valid_fn(j)
        out = []
        for h, (m, l, acc) in enumerate(carry):
            s = jnp.where(valid[h // NSA_GROUP], _dot(k, qts[h]), NEG)
            m_new = jnp.maximum(m, jnp.max(s, axis=0, keepdims=True))
            alpha = jnp.exp2(m - m_new)
            p = jnp.exp2(s - m_new)
            l = alpha * l + jnp.sum(p, axis=0, keepdims=True)
            acc = alpha * acc + _dot(vt, p.astype(BF16))
            out.append((m_new, l, acc))
        return tuple(out)

    init = tuple((jnp.full((1, tq), NEG, F32), jnp.zeros((1, tq), F32), jnp.zeros((LANE, tq), F32))
                 for _ in range(NSA_HEADS))
    return [acc / l for _, l, acc in lax.fori_loop(j_lo, j_hi, body, init)]


def _nsa_prompt_kernel(qct_ref, qrt_ref, kvc_ref, ks_ref, kw_ref, vst_ref, vwt_ref, gt_ref, c2st_ref, et_ref,
                       ot_ref, *, tq, n_sel):
    i = pl.program_id(1)
    t0 = i * tq
    qpos = t0 + lax.broadcasted_iota(jnp.int32, (1, tq), 1)
    n_cmp = kvc_ref.shape[1]
    kc = kvc_ref[0, :, :LANE].astype(BF16)
    vct = kvc_ref[0, :, LANE:].T.astype(BF16)
    cend = lax.broadcasted_iota(jnp.int32, (n_cmp, 1), 0) * CMP_STRIDE + (CMP_BLOCK - 1)
    valid_c = cend <= qpos
    psum = [jnp.zeros((n_cmp, tq), F32) for _ in range(NSA_KV_HEADS)]
    o_cmp = []
    for h in range(NSA_HEADS):
        s = jnp.where(valid_c, _dot(kc, qct_ref[h * LANE:(h + 1) * LANE, :]), NEG)
        p = jnp.where(valid_c, jnp.exp(s - jnp.max(s, axis=0, keepdims=True)), 0.0)
        p = p / jnp.maximum(jnp.sum(p, axis=0, keepdims=True), 1e-30)
        psum[h // NSA_GROUP] = psum[h // NSA_GROUP] + p
        o_cmp.append(_dot(vct, p.astype(BF16)))
    n_blk_rows = -(-n_sel // 8) * 8
    imp_t = jnp.concatenate(
        [jnp.dot(c2st_ref[...], ps, preferred_element_type=F32, precision=lax.Precision.HIGHEST)[:n_blk_rows]
         for ps in psum], axis=1)
    col = lax.broadcasted_iota(jnp.int32, (1, NSA_KV_HEADS * tq), 1)
    sel_t = _select_blocks_t(imp_t, lax.shift_right_logical(t0 + (col & (tq - 1)), 6), n_sel)
    sel_t = jnp.concatenate([sel_t, jnp.zeros((LANE - n_blk_rows, NSA_KV_HEADS * tq), F32)], axis=0).astype(BF16)
    krow = lax.broadcasted_iota(jnp.int32, (KEY_TILE, 1), 0)

    def sel_valid(j):
        causal = (j * KEY_TILE + krow) <= qpos
        return [jnp.where(causal, _dot(et_ref[j], sel_t[:, k * tq:(k + 1) * tq]), 0.0) > 0.5
                for k in range(NSA_KV_HEADS)]

    def win_valid(j):
        rel = qpos - (j * KEY_TILE + krow)
        return [jnp.where(rel >= 0, rel, WINDOW) < WINDOW] * NSA_KV_HEADS

    qrt = [qrt_ref[h * LANE:(h + 1) * LANE, :] for h in range(NSA_HEADS)]
    n_kt = (t0 + tq) // KEY_TILE
    o_sel = _flash_t(qrt, ks_ref.at[0], vst_ref.at[0], 0, n_kt, sel_valid, tq)
    o_win = _flash_t(qrt, kw_ref.at[0], vwt_ref.at[0], jnp.maximum((t0 - WINDOW) // KEY_TILE, 0), n_kt,
                     win_valid, tq)
    for h in range(NSA_HEADS):
        g = gt_ref[3 * h:3 * h + 3, :]
        o = o_cmp[h] * g[0:1] + o_sel[h] * g[1:2] + o_win[h] * g[2:3]
        k = h // NSA_GROUP
        ot_ref[h * NSA_HD:(h + 1) * NSA_HD, :] = o[k * NSA_HD:(k + 1) * NSA_HD]


def nsa_prompt_attn(qct, qrt, kvc, kb, vt, gates_t, n_batch, tq):
    n_rows = qct.shape[1]
    t = n_rows // n_batch
    tq = min(tq, t)
    nt = t // tq
    n_kt = t // KEY_TILE
    n_cmp = kvc.shape[1]
    n_sel = t // SEL_BLOCK
    assert n_cmp % LANE == 0 and n_sel <= LANE and tq & (tq - 1) == 0 and tq % KEY_TILE == 0
    c2st = jnp.asarray(_c2s(n_cmp, LANE).T)
    et = jnp.asarray(_expand(LANE, t).T.reshape(n_kt, KEY_TILE, LANE), dtype=BF16)
    kb3 = kb.reshape(n_batch, t, 2 * LANE)
    vt4 = vt.reshape(2, n_batch, n_kt, LANE, KEY_TILE)
    cols = lambda r: pl.BlockSpec((r, tq), lambda b, i: (0, b * nt + i))
    kspec = lambda c: pl.BlockSpec((1, t, LANE), lambda b, i: (b, 0, c))
    vspec = lambda c: pl.BlockSpec((None, 1, n_kt, LANE, KEY_TILE), lambda b, i: (c, b, 0, 0, 0))
    return pl.pallas_call(
        functools.partial(_nsa_prompt_kernel, tq=tq, n_sel=n_sel), grid=(n_batch, nt),
        in_specs=[cols(NSA_HEADS * LANE), cols(NSA_HEADS * LANE),
                  pl.BlockSpec((1, n_cmp, CMP_W), lambda b, i: (b, 0, 0)),
                  kspec(0), kspec(1), vspec(0), vspec(1), cols(LANE),
                  pl.BlockSpec(c2st.shape, lambda b, i: (0, 0)), pl.BlockSpec(et.shape, lambda b, i: (0, 0, 0))],
        out_specs=cols(NSA_Q_W),
        out_shape=jax.ShapeDtypeStruct((NSA_Q_W, n_rows), F32),
        compiler_params=_cparams(("parallel", "parallel"), 48), name="nsa_prompt_attn",
    )(qct, qrt, kvc, kb3, kb3, vt4, vt4, gates_t, c2st, et)


SROWS = NSA_HEADS * 8
HALF_W = 2 * NSA_KV_HEADS * NSA_HD


def _page_copies(cache_ref, pt_ref, targets, sem, b, slot, n_pages, page):
    return [pltpu.make_async_copy(cache_ref.at[pt_ref[b, pg], kind],
                                  buf.at[slot, :, :, pl.ds(pg * page, page)], sem.at[slot])
            for pg in range(n_pages) for kind, buf in targets]


def _stream_pages(cache_ref, pt_ref, targets, sem, n_pages, page):
    b = pl.program_id(0)
    nb = pl.num_programs(0)
    slot = lax.rem(b, 2)

    @pl.when(b == 0)
    def _():
        for c in _page_copies(cache_ref, pt_ref, targets, sem, 0, 0, n_pages, page):
            c.start()

    @pl.when(b + 1 < nb)
    def _():
        for c in _page_copies(cache_ref, pt_ref, targets, sem, b + 1, 1 - slot, n_pages, page):
            c.start()

    for c in _page_copies(cache_ref, pt_ref, targets, sem, b, slot, n_pages, page):
        c.wait()
    return slot


def _nsa_s1_kernel(pt_ref, cache_ref, q_ref, w_ref, pe_ref, c2s_ref, ocmp_ref, sel_ref, bufk, bufv, xk, xv, sem,
                   *, n_pages, page, ts, n_sel):
    past = n_pages * page
    slot = _stream_pages(cache_ref, pt_ref, [(0, bufk), (1, bufv)], sem, n_pages, page)
    n_chunk = past // CMP_STRIDE
    tch = min(past, 8 * LANE)
    eye = jnp.where(lax.broadcasted_iota(jnp.int32, (LANE, LANE), 0) == lax.broadcasted_iota(jnp.int32, (LANE, LANE), 1),
                    1.0, 0.0).astype(BF16)
    for src, dst in ((bufk, xk), (bufv, xv)):
        for c in range(past // tch):
            xt = src[slot, :, :, c * tch:(c + 1) * tch].reshape(LANE, tch).astype(BF16)
            dst[c * tch:(c + 1) * tch, :] = lax.dot_general(xt, eye, (((0,), (0,)), ((), ())),
                                                            preferred_element_type=F32)
    kv = _compress(lambda j: xk[pl.ds(j, n_chunk, stride=CMP_STRIDE), :],
                   lambda j: xv[pl.ds(j, n_chunk, stride=CMP_STRIDE), :], w_ref, pe_ref, n_chunk)
    kc = kv[:, :LANE].astype(BF16)
    vc = kv[:, LANE:].astype(BF16)
    tok = lax.broadcasted_iota(jnp.int32, (SROWS, 1), 0) & (ts - 1)
    qpos = past + tok
    cend = lax.broadcasted_iota(jnp.int32, (1, n_chunk), 1) * CMP_STRIDE + (CMP_BLOCK - 1)
    p = _cmp_probs(_dot_t(q_ref[0], kc), cend <= qpos)
    ocmp_ref[0] = _dot(p.astype(BF16), vc)
    psum = []
    for k in range(NSA_KV_HEADS):
        acc = p[k * NSA_GROUP * ts:k * NSA_GROUP * ts + ts]
        for g in range(1, NSA_GROUP):
            r = (k * NSA_GROUP + g) * ts
            acc = acc + p[r:r + ts]
        psum.append(acc)
    psum = jnp.concatenate(psum, axis=0)
    imp = jnp.dot(psum, c2s_ref[...], preferred_element_type=F32, precision=lax.Precision.HIGHEST)
    q_blk = lax.shift_right_logical(past + (lax.broadcasted_iota(jnp.int32, (NSA_KV_HEADS * ts, 1), 0) & (ts - 1)), 6)
    sel_ref[0] = _select_blocks(imp, q_blk, n_sel)


def _nsa_s2_kernel(pt_ref, cache_ref, q_ref, sel_ref, new_ref, win_ref, wnew_ref, ocmp_ref, g_ref, e_ref,
                   o_ref, bufk, bufv, sem, *, n_pages, page, ts, wb):
    past = n_pages * page
    slot = _stream_pages(cache_ref, pt_ref, [(2, bufk), (3, bufv)], sem, n_pages, page)
    q = q_ref[0]
    tok = lax.broadcasted_iota(jnp.int32, (SROWS, 1), 0) & (ts - 1)
    sel = sel_ref[0]
    sel_rows = jnp.concatenate([sel[k * ts:(k + 1) * ts] for k in range(NSA_KV_HEADS) for _ in range(NSA_GROUP)],
                               axis=0)
    n_new = new_ref.shape[1]
    jnew = lax.broadcasted_iota(jnp.int32, (1, n_new), 1)

    def attend(kt_past, vt_past, valid_past, k_new, v_new, valid_new):
        s_past = jnp.where(valid_past, _dot(q, kt_past), NEG)
        s_new = jnp.where(valid_new, _dot_t(q, k_new), NEG)
        m = jnp.maximum(jnp.max(s_past, axis=-1, keepdims=True), jnp.max(s_new, axis=-1, keepdims=True))
        p_past = jnp.exp2(s_past - m)
        p_new = jnp.exp2(s_new - m)
        l = jnp.sum(p_past, axis=-1, keepdims=True) + jnp.sum(p_new, axis=-1, keepdims=True)
        return (_dot_t(p_past.astype(BF16), vt_past) + _dot(p_new.astype(BF16), v_new)) / l

    mask_past = _dot(sel_rows.astype(BF16), e_ref[...]) > 0.5
    blk_new = past // SEL_BLOCK
    valid_new = (sel_rows[:, blk_new:blk_new + 1] > 0.5) & (jnew <= tok)
    o_sel = attend(bufk[slot].reshape(LANE, past).astype(BF16), bufv[slot].reshape(LANE, past).astype(BF16),
                   mask_past, new_ref[0, :, 0:LANE].astype(BF16), new_ref[0, :, LANE:HALF_W].astype(BF16), valid_new)
    u = lax.broadcasted_iota(jnp.int32, (1, wb), 1)
    rel = wb + tok - u
    valid_w = (rel >= 0) & (rel < WINDOW) & (past - wb + u >= 0)
    o_win = attend(win_ref[0, 0].reshape(LANE, wb).astype(BF16), win_ref[0, 1].reshape(LANE, wb).astype(BF16),
                   valid_w, wnew_ref[0, :, 0:LANE].astype(BF16), wnew_ref[0, :, LANE:HALF_W].astype(BF16),
                   jnew <= tok)
    gt = g_ref[0]
    o = ocmp_ref[0] * gt[:, 0:1] + o_sel * gt[:, 1:2] + o_win * gt[:, 2:3]
    half = SROWS // NSA_KV_HEADS
    for k in range(NSA_KV_HEADS):
        o_ref[0, k * half:(k + 1) * half, :] = o[k * half:(k + 1) * half, k * NSA_HD:(k + 1) * NSA_HD]


def nsa_sample_attn(page_table, cache, qc2, qr2, new_sel, win, new_win, gates, cmp_w, cmp_pe, ts):
    bs, n_pages = page_table.shape
    page = cache.shape[-1]
    past = n_pages * page
    wb = win.shape[-1]
    assert past % SEL_BLOCK == 0 and ts <= CMP_STRIDE and ts & (ts - 1) == 0 and SROWS == NSA_HEADS * ts
    n_chunk = past // CMP_STRIDE
    n_sel = past // SEL_BLOCK + 1
    n_sel_pad = -(-n_sel // LANE) * LANE
    c2s = jnp.asarray(_c2s(n_chunk, n_sel_pad))
    e = jnp.asarray(_expand(n_sel_pad, past), dtype=BF16)
    sems = pltpu.SemaphoreType.DMA((2,))
    pages = pltpu.VMEM((2, NSA_KV_HEADS, NSA_HD, past), F32)
    scratch1 = [pages, pages, pltpu.VMEM((past, LANE), F32), pltpu.VMEM((past, LANE), F32), sems]
    scratch2 = [pages, pages, sems]
    seq3 = lambda b, pt: (b, 0, 0)
    o_cmp, sel = pl.pallas_call(
        functools.partial(_nsa_s1_kernel, n_pages=n_pages, page=page, ts=ts, n_sel=n_sel),
        grid_spec=pltpu.PrefetchScalarGridSpec(
            num_scalar_prefetch=1, grid=(bs,),
            in_specs=[pl.BlockSpec(memory_space=pl.ANY),
                      pl.BlockSpec((1, SROWS, LANE), seq3),
                      pl.BlockSpec(cmp_w.shape, lambda b, pt: (0, 0, 0, 0)),
                      pl.BlockSpec(cmp_pe.shape, lambda b, pt: (0, 0)),
                      pl.BlockSpec(c2s.shape, lambda b, pt: (0, 0))],
            out_specs=[pl.BlockSpec((1, SROWS, LANE), seq3),
                       pl.BlockSpec((1, NSA_KV_HEADS * ts, n_sel_pad), seq3)],
            scratch_shapes=scratch1),
        out_shape=[jax.ShapeDtypeStruct((bs, SROWS, LANE), F32),
                   jax.ShapeDtypeStruct((bs, NSA_KV_HEADS * ts, n_sel_pad), F32)],
        compiler_params=_cparams(("arbitrary",), 48), name="nsa_sample_cmp",
    )(page_table, cache, qc2, cmp_w, cmp_pe, c2s)
    n_new = new_sel.shape[1]
    return pl.pallas_call(
        functools.partial(_nsa_s2_kernel, n_pages=n_pages, page=page, ts=ts, wb=wb),
        grid_spec=pltpu.PrefetchScalarGridSpec(
            num_scalar_prefetch=1, grid=(bs,),
            in_specs=[pl.BlockSpec(memory_space=pl.ANY),
                      pl.BlockSpec((1, SROWS, LANE), seq3),
                      pl.BlockSpec((1, NSA_KV_HEADS * ts, n_sel_pad), seq3),
                      pl.BlockSpec((1, n_new, HALF_W), seq3),
                      pl.BlockSpec((1, 2, NSA_KV_HEADS, NSA_HD, wb), lambda b, pt: (b, 0, 0, 0, 0)),
                      pl.BlockSpec((1, n_new, HALF_W), seq3),
                      pl.BlockSpec((1, SROWS, LANE), seq3),
                      pl.BlockSpec((1, SROWS, 3), seq3),
                      pl.BlockSpec(e.shape, lambda b, pt: (0, 0))],
            out_specs=pl.BlockSpec((1, SROWS, NSA_HD), seq3),
            scratch_shapes=scratch2),
        out_shape=jax.ShapeDtypeStruct((bs, SROWS, NSA_HD), F32),
        compiler_params=_cparams(("arbitrary",), 56), name="nsa_sample_sel",
    )(page_table, cache, qr2, sel, new_sel, win, new_win, o_cmp, gates, e)


def _in_proj(x2d, pos_rows, p, tm, dims_major):
    q_pre, kv_pre, gate_pre, qkv_pre, z_pre, ba_pre = rms_matmul(x2d, p["attn_norm_w"], p["w_in_parts"], tm)
    cos_t, sin_t = _rope_tables(pos_rows)
    nsa = nsa_prep_v2(q_pre, kv_pre, gate_pre, cos_t, sin_t, p["nsa_q_norm_w"], p["nsa_k_norm_w"], tm, dims_major)
    return nsa, qkv_pre, z_pre, ba_pre


def _tokens_major(a):
    return a.transpose(0, 4, 1, 2, 3)


def _mixer_out(x2d, o_nsa, o_gdn, p):
    return matmul_res([o_nsa, o_gdn], p["w_out"], x2d, 1024)


def _ffn(h, p, prev, n_seq, shift, tm):
    return conv_ffn(h, p["ffn_norm_w"], p["w_up"], p["ffn_conv_w"], p["ffn_conv_b"], p["w_down"], prev,
                    n_seq, shift, tm)


def _layer_prompt(x, mem, p):
    nb, t, d = x.shape
    x2d = x.reshape(nb * t, d)
    (rows_t, wrows_t, qc, qr, gates, kvb, cmp_rows), qkv_pre, z_pre, ba_pre = _in_proj(x2d, jnp.arange(t), p, 1024, True)
    kvc = compress_prompt(cmp_rows.reshape(nb, t, CMP_W), p["cmp_bd"], p["cmp_pe2"])
    o_nsa = nsa_prompt_attn_v2(qc, qr, kvc, kvb, gates, nb, 256)
    o_gdn, gdn_buf, gdn_s = gdn2(qkv_pre, z_pre, ba_pre, jnp.zeros((nb, GDN_CONV - 1, GDN_C3), F32),
                                 jnp.zeros((nb, GDN_HEADS, GDN_HD, GDN_HD), F32), p["gdn_conv_w"], p["gdn_a_log"],
                                 p["gdn_dt_bias"], p["gdn_norm_w"], nb, 4)
    mem_kv, mem_kv_bf = mem_kv_proj(mem.reshape(-1, d), p["mem_norm_w"], p["w_xk"], p["w_xv"], p["xk_norm_w"])
    h = xattn_fused(_mixer_out(x2d, o_nsa, o_gdn, p), p["xattn_norm_w"], p["w_xq"], p["xq_norm_w"],
                    mem_kv_bf.reshape(nb, mem.shape[1], 2 * XA_W), p["w_xo"], nb, 512)
    y, ffn_buf = _ffn(h, p, jnp.zeros((nb, FFN_CONV - 1, 2 * D_FF), F32), nb, 1, 512)
    keep = min(WINDOW, t)
    return (y.reshape(nb, t, d), _tokens_major(rows_t), _tokens_major(wrows_t[..., t - keep:]), gdn_s, gdn_buf,
            ffn_buf, mem_kv.reshape(nb, mem.shape[1], 2, XA_HEADS, XA_HD))


def _layer_sample(x, cache_kv, cache_win, state, gdn_cache, ffn_cache, mem_kv, page_table, p):
    bs, ts, d = x.shape
    n_pool, page = cache_kv.shape[:2]
    past = page_table.shape[1] * page
    x2d = x.reshape(bs * ts, d)
    pos_rows = jnp.tile(past + jnp.arange(ts), bs)
    (rows, wrows, qc, qr, gates), qkv_pre, z_pre, ba_pre = _in_proj(x2d, pos_rows, p, bs * ts, False)
    dims_major = lambda a: a.transpose(0, 2, 3, 4, 1)

    def stack_q(a):
        return a.reshape(bs, ts, NSA_HEADS, LANE).transpose(0, 2, 1, 3).reshape(bs, SROWS, LANE)

    def pad_new(a):
        return jnp.pad(a.reshape(bs, ts, HALF_W), ((0, 0), (0, LANE - ts), (0, 0)))

    wb = cache_win.shape[1]
    gates3 = gates[:, :NSA_GATE_W].reshape(bs, ts, NSA_HEADS, 3).transpose(0, 2, 1, 3).reshape(bs, SROWS, 3)
    win_t = dims_major(cache_win)
    o2 = nsa_sample_attn(page_table, dims_major(cache_kv), stack_q(qc), stack_q(qr),
                         pad_new(rows[:, HALF_W:]), win_t, pad_new(wrows), gates3, p["cmp_bd"], p["cmp_pe2"], ts)
    rows = rows.reshape(bs, ts, 4, NSA_KV_HEADS, NSA_HD)
    wrows = wrows.reshape(bs, ts, 2, NSA_KV_HEADS, NSA_HD)
    o_nsa = o2.reshape(bs, NSA_HEADS, ts, NSA_HD).transpose(0, 2, 1, 3).reshape(bs * ts, NSA_Q_W)
    o_gdn, gdn_buf, gdn_s = gdn2(qkv_pre, z_pre, ba_pre, gdn_cache, state, p["gdn_conv_w"], p["gdn_a_log"],
                                 p["gdn_dt_bias"], p["gdn_norm_w"], bs, 4)
    h1 = _mixer_out(x2d, o_nsa, o_gdn, p)
    (qx,) = rms_matmul(h1, p["xattn_norm_w"], [p["w_xq"]], 512)
    m = mem_kv.shape[1]
    kv_rows = mem_kv.reshape(bs, m, 2, XA_HEADS, XA_HD // LANE, LANE).transpose(0, 1, 2, 4, 3, 5)
    kv_rows = kv_rows.reshape(bs, m * MEM_ROWS, LANE)
    h = matmul_res([xattn_cache(qx, kv_rows, p["xq_norm_w"], bs)], p["w_xo"], h1, 512)
    h_tm = h.reshape(bs, ts, d).transpose(1, 0, 2).reshape(ts * bs, d)
    prev = ffn_cache.transpose(1, 0, 2).reshape(1, (FFN_CONV - 1) * bs, 2 * D_FF)
    y, ffn_buf = _ffn(h_tm, p, prev, 1, bs, ts * bs)
    y = y.reshape(ts, bs, d).transpose(1, 0, 2)
    ffn_buf = ffn_buf.reshape(FFN_CONV - 1, bs, 2 * D_FF).transpose(1, 0, 2)
    wall_t = jnp.concatenate([win_t, dims_major(wrows)], axis=-1)
    keep = min(WINDOW, wb + ts)
    return y, rows, _tokens_major(wall_t[..., wb + ts - keep:]), gdn_s, gdn_buf, ffn_buf


def _prep_params(l, attn_norm_w, w_in, nsa_q_norm_w, nsa_k_norm_w, cmp_pe, cmp_w, gdn_conv_w, gdn_a_log,
                 gdn_dt_bias, gdn_norm_w, w_out, mem_norm_w, w_xk, w_xv, xk_norm_w, xattn_norm_w, w_xq,
                 xq_norm_w, w_xo, ffn_norm_w, w_up, ffn_conv_w, ffn_conv_b, w_down):
    wi = w_in[l]
    cuts = np.cumsum([0, NSA_Q_W, NSA_KV_W, NSA_GATE_W, 3 * GDN_W, GDN_W, 2 * GDN_HEADS])
    parts = []
    for a, b in zip(cuts[:-1], cuts[1:]):
        w = wi[:, a:b]
        padn = -(-(b - a) // LANE) * LANE - (b - a)
        parts.append(jnp.pad(w, ((0, 0), (0, padn))).astype(BF16))
    cmp_bd, cmp_pe2 = _cmp_weights(cmp_pe[l], cmp_w[l])
    bf = lambda w: w[l].astype(BF16)
    return dict(attn_norm_w=attn_norm_w[l], w_in_parts=parts, nsa_q_norm_w=nsa_q_norm_w[l],
                nsa_k_norm_w=nsa_k_norm_w[l], cmp_bd=cmp_bd, cmp_pe2=cmp_pe2, gdn_conv_w=gdn_conv_w[l],
                gdn_a_log=gdn_a_log[l], gdn_dt_bias=gdn_dt_bias[l], gdn_norm_w=gdn_norm_w[l], w_out=bf(w_out),
                mem_norm_w=mem_norm_w[l], w_xk=bf(w_xk), w_xv=bf(w_xv), xk_norm_w=xk_norm_w[l],
                xattn_norm_w=xattn_norm_w[l], w_xq=bf(w_xq), xq_norm_w=xq_norm_w[l], w_xo=bf(w_xo),
                ffn_norm_w=ffn_norm_w[l], w_up=bf(w_up), ffn_conv_w=ffn_conv_w[l], ffn_conv_b=ffn_conv_b[l],
                w_down=bf(w_down))


def kernel(x_prompt, x_sample, mem_prompt, cache_nsa_kv, cache_nsa_win, state_gdn, cache_gdn_conv, cache_ffn_conv,
           cache_mem_kv, page_table, attn_norm_w, w_in, nsa_q_norm_w, nsa_k_norm_w, cmp_pe, cmp_w, gdn_conv_w,
           gdn_a_log, gdn_dt_bias, gdn_norm_w, w_out, mem_norm_w, w_xk, w_xv, xk_norm_w, xattn_norm_w, w_xq,
           xq_norm_w, w_xo, ffn_norm_w, w_up, ffn_conv_w, ffn_conv_b, w_down):
    weights = (attn_norm_w, w_in, nsa_q_norm_w, nsa_k_norm_w, cmp_pe, cmp_w, gdn_conv_w, gdn_a_log, gdn_dt_bias,
               gdn_norm_w, w_out, mem_norm_w, w_xk, w_xv, xk_norm_w, xattn_norm_w, w_xq, xq_norm_w, w_xo,
               ffn_norm_w, w_up, ffn_conv_w, ffn_conv_b, w_down)
    depth = cache_nsa_kv.shape[0]
    hp, hs = x_prompt, x_sample
    outs_p, outs_s = [], []
    for l in range(depth):
        p = _prep_params(l, *weights)
        res_p = _layer_prompt(hp, mem_prompt, p)
        hp = res_p[0]
        outs_p.append(res_p[1:])
        res_s = _layer_sample(hs, cache_nsa_kv[l], cache_nsa_win[l], state_gdn[l], cache_gdn_conv[l],
                              cache_ffn_conv[l], cache_mem_kv[l], page_table, p)
        hs = res_s[0]
        outs_s.append(res_s[1:])
    stack = lambda outs, i: jnp.stack([o[i] for o in outs])
    return ((hp, hs) + tuple(stack(outs_p, i) for i in range(6)) + tuple(stack(outs_s, i) for i in range(5)))
```

```python
import functools
import math

import numpy as np
import jax
import jax.numpy as jnp
from jax import lax
from jax.experimental import pallas as pl
from jax.experimental.pallas import tpu as pltpu

F32 = jnp.float32
BF16 = jnp.bfloat16

D_MODEL = 1024
NSA_HEADS = 8
NSA_KV_HEADS = 2
NSA_GROUP = NSA_HEADS // NSA_KV_HEADS
NSA_HD = 64
CMP_BLOCK = 32
CMP_STRIDE = 16
SEL_BLOCK = 64
SEL_TOPN = 8
WINDOW = 512
FORCED_BONUS = 1e4
GDN_HEADS = 4
GDN_HD = 128
GDN_CONV = 4
GDN_CHUNK = 64
GDN_W = GDN_HEADS * GDN_HD
XA_HEADS = 4
XA_HD = 256
D_FF = 2816
FFN_CONV = 3
ROPE_THETA = 10000.0
EPS = 1e-6
NEG = -1e30

NSA_Q_W = NSA_HEADS * NSA_HD
NSA_KV_W = 3 * 2 * NSA_KV_HEADS * NSA_HD
NSA_GATE_W = 3 * NSA_HEADS
LANE = 128
MIB = 1 << 20


def _cparams(sem, vmem_mib):
    return pltpu.CompilerParams(dimension_semantics=sem, vmem_limit_bytes=vmem_mib * MIB)


def _sigmoid(x):
    return 1.0 / (1.0 + jnp.exp(-x))


def _dot(a, b):
    return jnp.dot(a, b, preferred_element_type=F32)


def _dot_t(a, b):
    return lax.dot_general(a, b, (((1,), (1,)), ((), ())), preferred_element_type=F32)


def _split_bf16(a):
    hi = a.astype(BF16)
    lo = (a - hi.astype(F32)).astype(BF16)
    return hi, lo


def _dot3(a, b):
    ah, al = _split_bf16(a)
    bh, bl = _split_bf16(b)
    return _dot(ah, bh) + (_dot(ah, bl) + _dot(al, bh))


def _col_chunk(n):
    for c in (512, 256, 128):
        if n % c == 0:
            return c
    return n


def _rms_mm_kernel(x_ref, g_ref, *refs, n_out):
    w_refs, o_refs = refs[:n_out], refs[n_out:]
    x = x_ref[...]
    xn = (x * lax.rsqrt(jnp.mean(x * x, axis=-1, keepdims=True) + EPS) * g_ref[...]).astype(BF16)
    for w_ref, o_ref in zip(w_refs, o_refs):
        n = w_ref.shape[1]
        ch = _col_chunk(n)
        for c in range(0, n, ch):
            o_ref[:, c:c + ch] = _dot(xn, w_ref[:, c:c + ch])


def rms_matmul(x, g, ws, tm):
    n_rows, d = x.shape
    tm = min(tm, n_rows)
    in_specs = [pl.BlockSpec((tm, d), lambda i: (i, 0)), pl.BlockSpec((1, d), lambda i: (0, 0))]
    in_specs += [pl.BlockSpec(w.shape, lambda i: (0, 0), pipeline_mode=pl.Buffered(1)) for w in ws]
    out_specs = [pl.BlockSpec((tm, w.shape[1]), lambda i: (i, 0)) for w in ws]
    out_shape = [jax.ShapeDtypeStruct((n_rows, w.shape[1]), F32) for w in ws]
    return pl.pallas_call(
        functools.partial(_rms_mm_kernel, n_out=len(ws)),
        grid=(n_rows // tm,), in_specs=in_specs, out_specs=out_specs, out_shape=out_shape,
        compiler_params=_cparams(("parallel",), 56), name="rms_matmul",
    )(x, g.reshape(1, d), *ws)


def _mm_res_kernel(*refs, n_in):
    a_refs, (w_ref, r_ref, o_ref) = refs[:n_in], refs[n_in:]
    a = [a_ref[...].astype(BF16) for a_ref in a_refs]
    n = w_ref.shape[1]
    ch = _col_chunk(n)
    for c in range(0, n, ch):
        acc = r_ref[:, c:c + ch]
        k0 = 0
        for x in a:
            acc = acc + _dot(x, w_ref[k0:k0 + x.shape[1], c:c + ch])
            k0 += x.shape[1]
        o_ref[:, c:c + ch] = acc


def matmul_res(a_list, w, res, tm):
    n_rows = res.shape[0]
    k, n = w.shape
    assert sum(a.shape[1] for a in a_list) == k
    tm = min(tm, n_rows)
    return pl.pallas_call(
        functools.partial(_mm_res_kernel, n_in=len(a_list)), grid=(n_rows // tm,),
        in_specs=[pl.BlockSpec((tm, a.shape[1]), lambda i: (i, 0)) for a in a_list]
        + [pl.BlockSpec((k, n), lambda i: (0, 0), pipeline_mode=pl.Buffered(1)),
           pl.BlockSpec((tm, n), lambda i: (i, 0))],
        out_specs=pl.BlockSpec((tm, n), lambda i: (i, 0)),
        out_shape=jax.ShapeDtypeStruct((n_rows, n), F32),
        compiler_params=_cparams(("parallel",), 40), name="matmul_res",
    )(*a_list, w, res)


def _memkv_kernel(x_ref, g_ref, wk_ref, wv_ref, kn_ref, o_ref, ob_ref):
    x = x_ref[...]
    xn = (x * lax.rsqrt(jnp.mean(x * x, axis=-1, keepdims=True) + EPS) * g_ref[...]).astype(BF16)
    xa_w = XA_HEADS * XA_HD
    for h in range(XA_HEADS):
        sl = slice(h * XA_HD, (h + 1) * XA_HD)
        vl = slice(xa_w + h * XA_HD, xa_w + (h + 1) * XA_HD)
        k = _dot(xn, wk_ref[:, sl])
        k = k * lax.rsqrt(jnp.mean(k * k, axis=-1, keepdims=True) + EPS) * kn_ref[...]
        v = _dot(xn, wv_ref[:, sl])
        o_ref[:, sl] = k
        o_ref[:, vl] = v
        ob_ref[:, sl] = k.astype(BF16)
        ob_ref[:, vl] = v.astype(BF16)


def mem_kv_proj(mem, g, wk, wv, kn, tm=256):
    n_rows, d = mem.shape
    xa_w = XA_HEADS * XA_HD
    out = pl.BlockSpec((tm, 2 * xa_w), lambda i: (i, 0))
    return pl.pallas_call(
        _memkv_kernel, grid=(n_rows // tm,),
        in_specs=[pl.BlockSpec((tm, d), lambda i: (i, 0)), pl.BlockSpec((1, d), lambda i: (0, 0)),
                  pl.BlockSpec((d, xa_w), lambda i: (0, 0)), pl.BlockSpec((d, xa_w), lambda i: (0, 0)),
                  pl.BlockSpec((1, XA_HD), lambda i: (0, 0))],
        out_specs=[out, out],
        out_shape=[jax.ShapeDtypeStruct((n_rows, 2 * xa_w), F32), jax.ShapeDtypeStruct((n_rows, 2 * xa_w), BF16)],
        compiler_params=_cparams(("parallel",), 40), name="mem_kv_proj",
    )(mem, g.reshape(1, d), wk, wv, kn.reshape(1, XA_HD))


XA_W = XA_HEADS * XA_HD
MEM_ROWS = 2 * XA_W // LANE


def _xattn_head(q, qn, k, v):
    q = q * lax.rsqrt(jnp.mean(q * q, axis=-1, keepdims=True) + EPS) * qn
    s = _dot_t((q * (XA_HD ** -0.5)).astype(BF16), k)
    p = jnp.exp(s - jnp.max(s, axis=-1, keepdims=True))
    return _dot(p.astype(BF16), v) / jnp.sum(p, axis=-1, keepdims=True)


def _xattn_cache_kernel(q_ref, kv_ref, qn_ref, o_ref, *, m):
    def head_block(kind, h):
        halves = [kv_ref[0, pl.ds(kind * (MEM_ROWS // 2) + half * XA_HEADS + h, m, stride=MEM_ROWS), :]
                  for half in range(XA_HD // LANE)]
        return jnp.concatenate(halves, axis=1).astype(BF16)

    for h in range(XA_HEADS):
        sl = slice(h * XA_HD, (h + 1) * XA_HD)
        o_ref[:, sl] = _xattn_head(q_ref[:, sl], qn_ref[...], head_block(0, h), head_block(1, h))


def xattn_cache(q_pre, kv_rows, qn, n_batch):
    n_rows = q_pre.shape[0]
    tq = n_rows // n_batch
    m = kv_rows.shape[1] // MEM_ROWS
    return pl.pallas_call(
        functools.partial(_xattn_cache_kernel, m=m), grid=(n_batch,),
        in_specs=[pl.BlockSpec((tq, XA_W), lambda b: (b, 0)),
                  pl.BlockSpec((1, m * MEM_ROWS, LANE), lambda b: (b, 0, 0)),
                  pl.BlockSpec((1, XA_HD), lambda b: (0, 0))],
        out_specs=pl.BlockSpec((tq, XA_W), lambda b: (b, 0)),
        out_shape=jax.ShapeDtypeStruct((n_rows, XA_W), F32),
        compiler_params=_cparams(("parallel",), 40), name="xattn_cache",
    )(q_pre, kv_rows, qn.reshape(1, XA_HD))


def _xattn_fused_kernel(h_ref, g_ref, wq_ref, qn_ref, kv_ref, wo_ref, o_ref):
    x = h_ref[...]
    xn = (x * lax.rsqrt(jnp.mean(x * x, axis=-1, keepdims=True) + EPS) * g_ref[...]).astype(BF16)
    heads = []
    for h in range(XA_HEADS):
        sl = slice(h * XA_HD, (h + 1) * XA_HD)
        o = _xattn_head(_dot(xn, wq_ref[:, sl]), qn_ref[...], kv_ref[0, :, sl],
                        kv_ref[0, :, XA_W + h * XA_HD:XA_W + (h + 1) * XA_HD])
        heads.append(o.astype(BF16))
    a = jnp.concatenate(heads, axis=1)
    ch = _col_chunk(D_MODEL)
    for c in range(0, D_MODEL, ch):
        o_ref[:, c:c + ch] = x[:, c:c + ch] + _dot(a, wo_ref[:, c:c + ch])


def xattn_fused(h, g, wq, qn, kv_bf, wo, n_batch, tq):
    n_rows, d = h.shape
    nt = n_rows // n_batch // tq
    m = kv_bf.shape[1]
    const = lambda b, i: (0, 0)
    return pl.pallas_call(
        _xattn_fused_kernel, grid=(n_batch, nt),
        in_specs=[pl.BlockSpec((tq, d), lambda b, i: (b * nt + i, 0)), pl.BlockSpec((1, d), const),
                  pl.BlockSpec((d, XA_W), const, pipeline_mode=pl.Buffered(1)), pl.BlockSpec((1, XA_HD), const),
                  pl.BlockSpec((1, m, 2 * XA_W), lambda b, i: (b, 0, 0)),
                  pl.BlockSpec((XA_W, d), const, pipeline_mode=pl.Buffered(1))],
        out_specs=pl.BlockSpec((tq, d), lambda b, i: (b * nt + i, 0)),
        out_shape=jax.ShapeDtypeStruct((n_rows, d), F32),
        compiler_params=_cparams(("parallel", "parallel"), 40), name="xattn_fused",
    )(h, g.reshape(1, d), wq, qn.reshape(1, XA_HD), kv_bf, wo)


FFN_ACT_CHUNK = 256


def _ffn_kernel(h_ref, g_ref, wup_ref, cw_ref, cb_ref, wdn_ref, prev_ref, o_ref, buf_ref, xs_scr,
                *, tm, shift, base):
    t = pl.program_id(1)
    p0 = base - 2 * shift

    @pl.when(t == 0)
    def _():
        xs_scr[p0:base, :] = prev_ref[0]

    x = h_ref[...]
    xn = (x * lax.rsqrt(jnp.mean(x * x, axis=-1, keepdims=True) + EPS) * g_ref[...]).astype(BF16)
    acc = jnp.zeros((tm, D_MODEL), F32)
    for j in range(D_FF // FFN_ACT_CHUNK):
        halves = []
        for c0 in (j * FFN_ACT_CHUNK, D_FF + j * FFN_ACT_CHUNK):
            sl = slice(c0, c0 + FFN_ACT_CHUNK)
            xs_scr[base:base + tm, sl] = _dot(xn, wup_ref[:, sl])
            y = (cw_ref[0:1, sl] * xs_scr[p0:p0 + tm, sl]
                 + cw_ref[1:2, sl] * xs_scr[p0 + shift:p0 + shift + tm, sl]
                 + cw_ref[2:3, sl] * xs_scr[base:base + tm, sl])
            halves.append(y + cb_ref[:, sl])
        a, u = halves
        act = (a * _sigmoid(a) * u).astype(BF16)
        acc = acc + _dot(act, wdn_ref[j * FFN_ACT_CHUNK:(j + 1) * FFN_ACT_CHUNK, :])
    o_ref[...] = x + acc
    last = xs_scr[base + tm - 2 * shift: base + tm, :]
    buf_ref[0] = last
    xs_scr[p0:base, :] = last


def conv_ffn(h, g, wup, cw, cb, wdn, prev, n_seq, shift, tm):
    n_rows, d = h.shape
    t_rows = n_rows // n_seq
    tm = min(tm, t_rows)
    nt = t_rows // tm
    base = -(-2 * shift // 8) * 8
    kern = functools.partial(_ffn_kernel, tm=tm, shift=shift, base=base)
    const = lambda b, i: (0, 0)
    return pl.pallas_call(
        kern, grid=(n_seq, nt),
        in_specs=[pl.BlockSpec((tm, d), lambda b, i: (b * nt + i, 0)),
                  pl.BlockSpec((1, d), const),
                  pl.BlockSpec((d, 2 * D_FF), const, pipeline_mode=pl.Buffered(1)),
                  pl.BlockSpec((FFN_CONV, 2 * D_FF), const),
                  pl.BlockSpec((1, 2 * D_FF), const),
                  pl.BlockSpec((D_FF, d), const, pipeline_mode=pl.Buffered(1)),
                  pl.BlockSpec((1, 2 * shift, 2 * D_FF), lambda b, i: (b, 0, 0))],
        out_specs=[pl.BlockSpec((tm, d), lambda b, i: (b * nt + i, 0)),
                   pl.BlockSpec((1, 2 * shift, 2 * D_FF), lambda b, i: (b, 0, 0))],
        out_shape=[jax.ShapeDtypeStruct((n_rows, d), F32),
                   jax.ShapeDtypeStruct((n_seq, 2 * shift, 2 * D_FF), F32)],
        scratch_shapes=[pltpu.VMEM((base + tm, 2 * D_FF), F32)],
        compiler_params=_cparams(("arbitrary", "arbitrary"), 56), name="conv_ffn",
    )(h, g.reshape(1, d), wup, cw, cb.reshape(1, 2 * D_FF), wdn, prev)


GDN_STACK = GDN_HEADS * GDN_CHUNK
GDN_C3 = 3 * GDN_W
GDN_PREV0 = 8 - (GDN_CONV - 1)


def _gdn_masks():
    r = np.arange(GDN_STACK)[:, None]
    c = np.arange(GDN_STACK)[None, :]
    same = lambda n: (r // n) == (c // n)
    m = [same(2) & (c < r)]
    for n in (4, 8, 16, 32, 64):
        m.append(same(n) & ~same(n // 2) & (c < r))
    m.append(same(GDN_CHUNK) & (c <= r))
    m.append(same(GDN_CHUNK) & (c < r))
    m.append(r == c)
    return np.stack(m).astype(np.float32)


def _unit_lower_inverse(a, m_ref):
    t = m_ref[8] - a * m_ref[0]
    for lvl in range(1, 6):
        off = a * m_ref[lvl]
        t = t - _dot3(t, _dot3(off, t))
    return t


def _gdn_kernel_v1(qkv_ref, z_ref, ba_ref, prev_ref, s0_ref, cw_ref, par_ref, nw_ref, m_ref,
                   o_ref, buf_ref, s_ref, xs_scr, s_scr, *, tt):
    t = pl.program_id(1)
    nt = pl.num_programs(1)
    row0 = 8

    @pl.when(t == 0)
    def _():
        xs_scr[GDN_PREV0:row0, :] = prev_ref[0]
        s_scr[...] = s0_ref[0]

    xs_scr[row0:row0 + tt, :] = qkv_ref[...]
    causal = m_ref[6]
    neg_ea = -jnp.exp(par_ref[0:1, :])
    dtb = par_ref[1:2, :]
    rv = min(GDN_CHUNK, tt)
    pad = GDN_CHUNK - rv

    def padrows(a):
        if pad == 0:
            return a
        return jnp.concatenate([a, jnp.zeros((pad, a.shape[1]), a.dtype)], axis=0)

    for n in range(-(-tt // GDN_CHUNK)):
        r0 = n * GDN_CHUNK
        y = cw_ref[0:1, :] * xs_scr[GDN_PREV0 + r0:GDN_PREV0 + r0 + rv, :]
        for i in range(1, GDN_CONV):
            y = y + cw_ref[i:i + 1, :] * xs_scr[GDN_PREV0 + i + r0:GDN_PREV0 + i + r0 + rv, :]
        y = y * _sigmoid(y)
        qs, ks, vs, betas, gs = [], [], [], [], []
        for h in range(GDN_HEADS):
            q = y[:, h * GDN_HD:(h + 1) * GDN_HD]
            k = y[:, GDN_W + h * GDN_HD:GDN_W + (h + 1) * GDN_HD]
            v = y[:, 2 * GDN_W + h * GDN_HD:2 * GDN_W + (h + 1) * GDN_HD]
            q = q * lax.rsqrt(jnp.sum(q * q, axis=-1, keepdims=True) + EPS) * (GDN_HD ** -0.5)
            k = k * lax.rsqrt(jnp.sum(k * k, axis=-1, keepdims=True) + EPS)
            bcol = ba_ref[r0:r0 + rv, h:h + 1]
            acol = ba_ref[r0:r0 + rv, GDN_HEADS + h:GDN_HEADS + h + 1] + dtb[:, h:h + 1]
            softplus = jnp.maximum(acol, 0.0) + jnp.log1p(jnp.exp(-jnp.abs(acol)))
            qs.append(padrows(q)); ks.append(padrows(k)); vs.append(padrows(v))
            betas.append(padrows(_sigmoid(bcol)))
            gs.append(padrows(neg_ea[:, h:h + 1] * softplus))
        qm = jnp.concatenate(qs, axis=0)
        km = jnp.concatenate(ks, axis=0)
        vm = jnp.concatenate(vs, axis=0)
        beta = jnp.concatenate(betas, axis=0)
        g = jnp.concatenate(gs, axis=0)
        gc = jnp.dot(causal, jnp.broadcast_to(g, (GDN_STACK, LANE)), preferred_element_type=F32,
                     precision=lax.Precision.HIGHEST)[:, 0:1]
        gc_row = jnp.sum(m_ref[8] * gc, axis=0, keepdims=True)
        diff = gc - gc_row
        decay = jnp.exp(jnp.where(causal > 0, diff, 0.0)) * causal
        kb = km * beta
        kmb = km.astype(BF16)
        a_mat = _dot_t(kb.astype(BF16), kmb) * decay * m_ref[7]
        qk = _dot_t(qm.astype(BF16), kmb) * decay
        tinv = _unit_lower_inverse(a_mat, m_ref)
        egc = jnp.exp(gc)
        sol = _dot3(tinv, jnp.concatenate([vm * beta, kb * egc], axis=1))
        u_all, w_all = sol[:, :GDN_HD], sol[:, GDN_HD:]
        qd = qm * egc
        v_news, o_inter, kds, gls = [], [], [], []
        for h in range(GDN_HEADS):
            rs = slice(h * GDN_CHUNK, (h + 1) * GDN_CHUNK)
            sb = s_scr[h].astype(BF16)
            v_new = u_all[rs] - _dot(w_all[rs].astype(BF16), sb)
            o_inter.append(_dot(qd[rs].astype(BF16), sb))
            gl = gc[h * GDN_CHUNK + GDN_CHUNK - 1:(h + 1) * GDN_CHUNK, :]
            kds.append(km[rs] * jnp.exp(gl - gc[rs]))
            v_news.append(v_new)
            gls.append(gl)
        v_stack = jnp.concatenate(v_news, axis=0).astype(BF16)
        o_intra = _dot(qk.astype(BF16), v_stack)
        for h in range(GDN_HEADS):
            rs = slice(h * GDN_CHUNK, (h + 1) * GDN_CHUNK)
            s_scr[h] = s_scr[h] * jnp.exp(gls[h]) + _dot(kds[h].T.astype(BF16), v_stack[rs])
            o = (o_inter[h] + o_intra[rs])[:rv]
            on = o * lax.rsqrt(jnp.mean(o * o, axis=-1, keepdims=True) + EPS) * nw_ref[...]
            zz = z_ref[r0:r0 + rv, h * GDN_HD:(h + 1) * GDN_HD]
            o_ref[r0:r0 + rv, h * GDN_HD:(h + 1) * GDN_HD] = on * (zz * _sigmoid(zz))

    last = xs_scr[row0 + tt - (GDN_CONV - 1):row0 + tt, :]
    buf_ref[0] = last
    xs_scr[GDN_PREV0:row0, :] = last

    @pl.when(t == nt - 1)
    def _():
        s_ref[0] = s_scr[...]


def _gdn_kernel(qkv_ref, z_ref, ba_ref, prev_ref, s0_ref, cw_ref, par_ref, nw_ref, m_ref,
                o_ref, buf_ref, s_ref, xs_scr, s_scr, *, tt):
    t = pl.program_id(1)
    nt = pl.num_programs(1)
    row0 = 8

    @pl.when(t == 0)
    def _():
        xs_scr[GDN_PREV0:row0, :] = prev_ref[0]
        s_scr[...] = s0_ref[0]

    xs_scr[row0:row0 + tt, :] = qkv_ref[...]
    causal, strict, eye = m_ref[6], m_ref[7], m_ref[8]
    neg_ea = -jnp.exp(par_ref[0:1, :])
    dtb = par_ref[1:2, :]
    rv = min(GDN_CHUNK, tt)
    pad = GDN_CHUNK - rv
    n_ch = -(-tt // GDN_CHUNK)

    def padrows(a):
        if pad == 0:
            return a
        return jnp.concatenate([a, jnp.zeros((pad, a.shape[1]), a.dtype)], axis=0)

    qms, kms, vms, betas, gcols = [], [], [], [], []
    for n in range(n_ch):
        r0 = n * GDN_CHUNK
        y = cw_ref[0:1, :] * xs_scr[GDN_PREV0 + r0:GDN_PREV0 + r0 + rv, :]
        for i in range(1, GDN_CONV):
            y = y + cw_ref[i:i + 1, :] * xs_scr[GDN_PREV0 + i + r0:GDN_PREV0 + i + r0 + rv, :]
        y = y * _sigmoid(y)
        qs, ks, vs, bs, gs = [], [], [], [], []
        for h in range(GDN_HEADS):
            q = y[:, h * GDN_HD:(h + 1) * GDN_HD]
            k = y[:, GDN_W + h * GDN_HD:GDN_W + (h + 1) * GDN_HD]
            v = y[:, 2 * GDN_W + h * GDN_HD:2 * GDN_W + (h + 1) * GDN_HD]
            q = q * lax.rsqrt(jnp.sum(q * q, axis=-1, keepdims=True) + EPS) * (GDN_HD ** -0.5)
            k = k * lax.rsqrt(jnp.sum(k * k, axis=-1, keepdims=True) + EPS)
            bcol = ba_ref[r0:r0 + rv, h:h + 1]
            acol = ba_ref[r0:r0 + rv, GDN_HEADS + h:GDN_HEADS + h + 1] + dtb[:, h:h + 1]
            softplus = jnp.maximum(acol, 0.0) + jnp.log1p(jnp.exp(-jnp.abs(acol)))
            qs.append(padrows(q)); ks.append(padrows(k)); vs.append(padrows(v))
            bs.append(padrows(_sigmoid(bcol)))
            gs.append(padrows(neg_ea[:, h:h + 1] * softplus))
        qms.append(jnp.concatenate(qs, axis=0))
        kms.append(jnp.concatenate(ks, axis=0))
        vms.append(jnp.concatenate(vs, axis=0))
        betas.append(jnp.concatenate(bs, axis=0))
        gcols.append(jnp.concatenate(gs, axis=0))
    gmat = jnp.concatenate(gcols + [jnp.zeros((GDN_STACK, LANE - n_ch), F32)], axis=1)
    gc_all = jnp.dot(causal, gmat, preferred_element_type=F32, precision=lax.Precision.HIGHEST)

    a_mats, qks, rhss, gcs, tinv = [], [], [], [], []
    for n in range(n_ch):
        gc = gc_all[:, n:n + 1]
        gc_row = jnp.sum(eye * gc, axis=0, keepdims=True)
        decay = jnp.exp(jnp.where(causal > 0, gc - gc_row, 0.0)) * causal
        kb = kms[n] * betas[n]
        kmb = kms[n].astype(BF16)
        a_mat = _dot_t(kb.astype(BF16), kmb) * decay * strict
        qks.append((_dot_t(qms[n].astype(BF16), kmb) * decay).astype(BF16))
        rhss.append(jnp.concatenate([vms[n] * betas[n], kb * jnp.exp(gc)], axis=1).astype(BF16))
        a_mats.append(a_mat)
        gcs.append(gc)
        tinv.append(eye - a_mat * m_ref[0])
    for lvl in range(1, 6):
        tb = [x.astype(BF16) for x in tinv]
        xs = [_dot((a_mats[n] * m_ref[lvl]).astype(BF16), tb[n]).astype(BF16) for n in range(n_ch)]
        tinv = [tinv[n] - _dot(tb[n], xs[n]) for n in range(n_ch)]
    sols = [_dot(tinv[n].astype(BF16), rhss[n]) for n in range(n_ch)]

    for n in range(n_ch):
        r0 = n * GDN_CHUNK
        gc = gcs[n]
        egc = jnp.exp(gc)
        u_all, w_all = sols[n][:, :GDN_HD], sols[n][:, GDN_HD:]
        qd = qms[n] * egc
        v_news, o_inter, gls = [], [], []
        for h in range(GDN_HEADS):
            rs = slice(h * GDN_CHUNK, (h + 1) * GDN_CHUNK)
            sb = s_scr[h].astype(BF16)
            both = _dot(jnp.concatenate([w_all[rs], qd[rs]], axis=0).astype(BF16), sb)
            v_news.append(u_all[rs] - both[:GDN_CHUNK])
            o_inter.append(both[GDN_CHUNK:])
            gls.append(gc[h * GDN_CHUNK + GDN_CHUNK - 1:(h + 1) * GDN_CHUNK, :])
        v_stack = jnp.concatenate(v_news, axis=0).astype(BF16)
        o_intra = _dot(qks[n], v_stack)
        for h in range(GDN_HEADS):
            rs = slice(h * GDN_CHUNK, (h + 1) * GDN_CHUNK)
            kd = kms[n][rs] * jnp.exp(gls[h] - gc[rs])
            s_scr[h] = s_scr[h] * jnp.exp(gls[h]) + _dot(kd.T.astype(BF16), v_stack[rs])
            o = (o_inter[h] + o_intra[rs])[:rv]
            on = o * lax.rsqrt(jnp.mean(o * o, axis=-1, keepdims=True) + EPS) * nw_ref[...]
            zz = z_ref[r0:r0 + rv, h * GDN_HD:(h + 1) * GDN_HD]
            o_ref[r0:r0 + rv, h * GDN_HD:(h + 1) * GDN_HD] = on * (zz * _sigmoid(zz))

    last = xs_scr[row0 + tt - (GDN_CONV - 1):row0 + tt, :]
    buf_ref[0] = last
    xs_scr[GDN_PREV0:row0, :] = last

    @pl.when(t == nt - 1)
    def _():
        s_ref[0] = s_scr[...]


def _gdn_masks2(chunk):
    n = GDN_HEADS * chunk
    r = np.arange(n)[:, None]
    c = np.arange(n)[None, :]
    same = lambda k: (r // k) == (c // k)
    m = [same(2) & (c < r)]
    k = 4
    while k <= chunk:
        m.append(same(k) & ~same(k // 2) & (c < r))
        k *= 2
    m += [same(chunk) & (c <= r), same(chunk) & (c < r), r == c]
    return np.stack(m).astype(np.float32)


def _gdn_kernel2(qkv_ref, z_ref, ba_ref, prev_ref, s0_ref, cw_ref, par_ref, nw_ref, m_ref,
                 o_ref, buf_ref, s_ref, xs_scr, s_scr, *, rv, chunk, n_units, sequential):
    t = pl.program_id(1)
    nt = pl.num_programs(1)
    stack = GDN_HEADS * chunk
    n_lvl = m_ref.shape[0] - 3
    causal, strict, eye = m_ref[n_lvl], m_ref[n_lvl + 1], m_ref[n_lvl + 2]
    kc = GDN_CONV - 1
    if sequential:
        base = [8 + u * rv for u in range(n_units)]

        @pl.when(t == 0)
        def _():
            xs_scr[8 - kc:8, :] = prev_ref[0]
            s_scr[...] = s0_ref[...]

        xs_scr[8:8 + n_units * rv, :] = qkv_ref[...]
    else:
        base = [u * (8 + rv) + 8 for u in range(n_units)]
        s_scr[...] = s0_ref[...]
        for u in range(n_units):
            xs_scr[base[u] - kc:base[u], :] = prev_ref[u]
            xs_scr[base[u]:base[u] + rv, :] = qkv_ref[u * rv:(u + 1) * rv, :]
    ba = ba_ref[...]
    beta_all = _sigmoid(ba)
    a_sh = ba + par_ref[1:2, :]
    g_all = -jnp.exp(par_ref[0:1, :]) * (jnp.maximum(a_sh, 0.0) + jnp.log1p(jnp.exp(-jnp.abs(a_sh))))
    pad = chunk - rv

    def padrows(a):
        if pad == 0:
            return a
        return jnp.concatenate([a, jnp.zeros((pad, a.shape[1]), a.dtype)], axis=0)

    qms, kms, vms, betas, gcols = [], [], [], [], []
    for u in range(n_units):
        r0 = u * rv
        y = cw_ref[0:1, :] * xs_scr[base[u] - kc:base[u] - kc + rv, :]
        for i in range(1, GDN_CONV):
            y = y + cw_ref[i:i + 1, :] * xs_scr[base[u] - kc + i:base[u] - kc + i + rv, :]
        y = y * _sigmoid(y)
        qs, ks, vs, bs, gs = [], [], [], [], []
        for h in range(GDN_HEADS):
            q = y[:, h * GDN_HD:(h + 1) * GDN_HD]
            k = y[:, GDN_W + h * GDN_HD:GDN_W + (h + 1) * GDN_HD]
            v = y[:, 2 * GDN_W + h * GDN_HD:2 * GDN_W + (h + 1) * GDN_HD]
            q = q * lax.rsqrt(jnp.sum(q * q, axis=-1, keepdims=True) + EPS) * (GDN_HD ** -0.5)
            k = k * lax.rsqrt(jnp.sum(k * k, axis=-1, keepdims=True) + EPS)
            qs.append(padrows(q)); ks.append(padrows(k)); vs.append(padrows(v))
            bs.append(padrows(beta_all[r0:r0 + rv, h:h + 1]))
            gs.append(padrows(g_all[r0:r0 + rv, GDN_HEADS + h:GDN_HEADS + h + 1]))
        qms.append(jnp.concatenate(qs, axis=0))
        kms.append(jnp.concatenate(ks, axis=0))
        vms.append(jnp.concatenate(vs, axis=0))
        betas.append(jnp.concatenate(bs, axis=0))
        gcols.append(jnp.concatenate(gs, axis=0))
    gmat = jnp.concatenate(gcols + [jnp.zeros((stack, LANE - n_units), F32)], axis=1)
    gc_all = jnp.dot(causal, gmat, preferred_element_type=F32, precision=lax.Precision.HIGHEST)

    a_bfs, qks, rhss, gcs, tinv = [], [], [], [], []
    for u in range(n_units):
        gc = gc_all[:, u:u + 1]
        gc_row = jnp.sum(eye * gc, axis=0, keepdims=True)
        decay = jnp.exp(jnp.where(causal > 0, gc - gc_row, 0.0)) * causal
        kb = kms[u] * betas[u]
        kmb = kms[u].astype(BF16)
        a_mat = _dot_t(kb.astype(BF16), kmb) * decay * strict
        qks.append((_dot_t(qms[u].astype(BF16), kmb) * decay).astype(BF16))
        rhss.append(jnp.concatenate([vms[u] * betas[u], kb * jnp.exp(gc)], axis=1).astype(BF16))
        a_bfs.append(a_mat.astype(BF16))
        gcs.append(gc)
        tinv.append(eye - a_mat * m_ref[0])
    for lvl in range(1, n_lvl):
        tb = [x.astype(BF16) for x in tinv]
        xs = [(_dot(a_bfs[u], tb[u]) * m_ref[lvl]).astype(BF16) for u in range(n_units)]
        tinv = [tinv[u] - _dot(tb[u], xs[u]) for u in range(n_units)]
    sols = [_dot(tinv[u].astype(BF16), rhss[u]) for u in range(n_units)]

    for u in range(n_units):
        r0 = u * rv
        st = 0 if sequential else u
        gc = gcs[u]
        u_all, w_all = sols[u][:, :GDN_HD], sols[u][:, GDN_HD:]
        qd = qms[u] * jnp.exp(gc)
        v_news, o_inter, gls = [], [], []
        for h in range(GDN_HEADS):
            rs = slice(h * chunk, (h + 1) * chunk)
            sb = s_scr[st, h].astype(BF16)
            both = _dot(jnp.concatenate([w_all[rs], qd[rs]], axis=0).astype(BF16), sb)
            v_news.append(u_all[rs] - both[:chunk])
            o_inter.append(both[chunk:])
            gls.append(gc[(h + 1) * chunk - 1:(h + 1) * chunk, :])
        v_stack = jnp.concatenate(v_news, axis=0).astype(BF16)
        o_intra = _dot(qks[u], v_stack)
        for h in range(GDN_HEADS):
            rs = slice(h * chunk, (h + 1) * chunk)
            kd = kms[u][rs] * jnp.exp(gls[h] - gc[rs])
            s_scr[st, h] = s_scr[st, h] * jnp.exp(gls[h]) + _dot(kd.T.astype(BF16), v_stack[rs])
            o = (o_inter[h] + o_intra[rs])[:rv]
            on = o * lax.rsqrt(jnp.mean(o * o, axis=-1, keepdims=True) + EPS) * nw_ref[...]
            zz = z_ref[r0:r0 + rv, h * GDN_HD:(h + 1) * GDN_HD]
            o_ref[r0:r0 + rv, h * GDN_HD:(h + 1) * GDN_HD] = on * (zz * _sigmoid(zz))

    if sequential:
        last = xs_scr[8 + n_units * rv - kc:8 + n_units * rv, :]
        buf_ref[0] = last
        xs_scr[8 - kc:8, :] = last

        @pl.when(t == nt - 1)
        def _():
            s_ref[...] = s_scr[...]
    else:
        for u in range(n_units):
            buf_ref[u] = xs_scr[base[u] + rv - kc:base[u] + rv, :]
        s_ref[...] = s_scr[...]


def gdn2(qkv_pre, z_pre, ba_pre, prev, s0, conv_w, a_log, dt_bias, norm_w, n_seq, units):
    n_rows = qkv_pre.shape[0]
    t_rows = n_rows // n_seq
    sequential = t_rows >= GDN_CHUNK
    if sequential:
        rv = chunk = GDN_CHUNK
        units = min(units, t_rows // chunk)
        grid = (n_seq, t_rows // (units * chunk))
        seq_blk, scr_rows = 1, 8 + units * rv
    else:
        rv, chunk = t_rows, 16
        units = math.gcd(units, n_seq)
        assert rv % 8 == 0 and rv <= chunk
        grid = (n_seq // units, 1)
        seq_blk, scr_rows = units, units * (8 + rv)
    tt = units * rv
    nt = grid[1]
    par = jnp.zeros((8, LANE), F32).at[0, GDN_HEADS:2 * GDN_HEADS].set(a_log).at[1, GDN_HEADS:2 * GDN_HEADS].set(dt_bias)
    masks = jnp.asarray(_gdn_masks2(chunk))
    const2 = lambda b, i: (0, 0)
    rows = lambda b, i: (b * nt + i, 0)
    return pl.pallas_call(
        functools.partial(_gdn_kernel2, rv=rv, chunk=chunk, n_units=units, sequential=sequential), grid=grid,
        in_specs=[pl.BlockSpec((tt, GDN_C3), rows), pl.BlockSpec((tt, GDN_W), rows),
                  pl.BlockSpec((tt, LANE), rows),
                  pl.BlockSpec((seq_blk, GDN_CONV - 1, GDN_C3), lambda b, i: (b, 0, 0)),
                  pl.BlockSpec((seq_blk, GDN_HEADS, GDN_HD, GDN_HD), lambda b, i: (b, 0, 0, 0)),
                  pl.BlockSpec((GDN_CONV, GDN_C3), const2), pl.BlockSpec((8, LANE), const2),
                  pl.BlockSpec((1, GDN_HD), const2),
                  pl.BlockSpec(masks.shape, lambda b, i: (0, 0, 0))],
        out_specs=[pl.BlockSpec((tt, GDN_W), rows),
                   pl.BlockSpec((seq_blk, GDN_CONV - 1, GDN_C3), lambda b, i: (b, 0, 0)),
                   pl.BlockSpec((seq_blk, GDN_HEADS, GDN_HD, GDN_HD), lambda b, i: (b, 0, 0, 0))],
        out_shape=[jax.ShapeDtypeStruct((n_rows, GDN_W), F32),
                   jax.ShapeDtypeStruct((n_seq, GDN_CONV - 1, GDN_C3), F32),
                   jax.ShapeDtypeStruct((n_seq, GDN_HEADS, GDN_HD, GDN_HD), F32)],
        scratch_shapes=[pltpu.VMEM((scr_rows, GDN_C3), F32),
                        pltpu.VMEM((seq_blk, GDN_HEADS, GDN_HD, GDN_HD), F32)],
        compiler_params=_cparams(("arbitrary", "arbitrary"), 48), name="gdn",
    )(qkv_pre, z_pre, ba_pre, prev, s0, conv_w, par, norm_w.reshape(1, GDN_HD), masks)


def gdn(qkv_pre, z_pre, ba_pre, prev, s0, conv_w, a_log, dt_bias, norm_w, n_seq, tt):
    n_rows = qkv_pre.shape[0]
    t_rows = n_rows // n_seq
    tt = min(tt, t_rows)
    nt = t_rows // tt
    par = jnp.zeros((8, LANE), F32).at[0, :GDN_HEADS].set(a_log).at[1, :GDN_HEADS].set(dt_bias)
    masks = jnp.asarray(_gdn_masks())
    const2 = lambda b, i: (0, 0)
    rows = lambda b, i: (b * nt + i, 0)
    return pl.pallas_call(
        functools.partial(_gdn_kernel, tt=tt), grid=(n_seq, nt),
        in_specs=[pl.BlockSpec((tt, GDN_C3), rows), pl.BlockSpec((tt, GDN_W), rows),
                  pl.BlockSpec((tt, LANE), rows),
                  pl.BlockSpec((1, GDN_CONV - 1, GDN_C3), lambda b, i: (b, 0, 0)),
                  pl.BlockSpec((1, GDN_HEADS, GDN_HD, GDN_HD), lambda b, i: (b, 0, 0, 0)),
                  pl.BlockSpec((GDN_CONV, GDN_C3), const2), pl.BlockSpec((8, LANE), const2),
                  pl.BlockSpec((1, GDN_HD), const2),
                  pl.BlockSpec(masks.shape, lambda b, i: (0, 0, 0))],
        out_specs=[pl.BlockSpec((tt, GDN_W), rows),
                   pl.BlockSpec((1, GDN_CONV - 1, GDN_C3), lambda b, i: (b, 0, 0)),
                   pl.BlockSpec((1, GDN_HEADS, GDN_HD, GDN_HD), lambda b, i: (b, 0, 0, 0))],
        out_shape=[jax.ShapeDtypeStruct((n_rows, GDN_W), F32),
                   jax.ShapeDtypeStruct((n_seq, GDN_CONV - 1, GDN_C3), F32),
                   jax.ShapeDtypeStruct((n_seq, GDN_HEADS, GDN_HD, GDN_HD), F32)],
        scratch_shapes=[pltpu.VMEM((8 + tt, GDN_C3), F32), pltpu.VMEM((GDN_HEADS, GDN_HD, GDN_HD), F32)],
        compiler_params=_cparams(("arbitrary", "arbitrary"), 48), name="gdn",
    )(qkv_pre, z_pre, ba_pre, prev, s0, conv_w, par, norm_w.reshape(1, GDN_HD), masks)


CMP_W = 2 * NSA_KV_HEADS * NSA_HD


def _cmp_weights(cmp_pe, cmp_w):
    def bd(w):
        z = jnp.zeros_like(w)
        return jnp.concatenate([jnp.concatenate([w, z], 2), jnp.concatenate([z, w], 2)], 1)
    ws, cs = [], []
    for s in range(2):
        lo, hi = bd(cmp_w[s, :CMP_STRIDE]), bd(cmp_w[s, CMP_STRIDE:])
        cat = jnp.concatenate([lo, hi], axis=2)
        ws.append(cat.reshape(CMP_STRIDE // 2, 2 * LANE, 2 * LANE))
        pe2 = jnp.concatenate([cmp_pe[s], cmp_pe[s]], axis=-1)
        const = [jnp.einsum("jd,jde->e", pe2[h * CMP_STRIDE:(h + 1) * CMP_STRIDE], m,
                            precision=lax.Precision.HIGHEST) for h, m in enumerate((lo, hi))]
        cs.append(jnp.concatenate(const))
    return jnp.stack(ws).astype(BF16), jnp.stack(cs)


def _compress(read_k, read_v, w_ref, c_ref, n_chunk):
    outs = []
    for s, read in enumerate((read_k, read_v)):
        acc = jnp.zeros((n_chunk, 2 * LANE), F32)
        for p in range(CMP_STRIDE // 2):
            x = jnp.concatenate([read(2 * p), read(2 * p + 1)], axis=1).astype(BF16)
            acc = acc + _dot(x, w_ref[s, p])
        acc = acc + c_ref[s:s + 1, :]
        outs.append(acc[:, :LANE] + pltpu.roll(acc[:, LANE:], n_chunk - 1, axis=0))
    return jnp.concatenate(outs, axis=1)


def _cmp_kernel(k_ref, v_ref, w_ref, pe_ref, o_ref, *, n_chunk):
    o_ref[0] = _compress(lambda j: k_ref[0, pl.ds(j, n_chunk, stride=CMP_STRIDE), :],
                         lambda j: v_ref[0, pl.ds(j, n_chunk, stride=CMP_STRIDE), :], w_ref, pe_ref, n_chunk)


def compress_prompt(rows, w, pe):
    b, t, _ = rows.shape
    n_chunk = t // CMP_STRIDE
    return pl.pallas_call(
        functools.partial(_cmp_kernel, n_chunk=n_chunk), grid=(b,),
        in_specs=[pl.BlockSpec((1, t, LANE), lambda i: (i, 0, 0)), pl.BlockSpec((1, t, LANE), lambda i: (i, 0, 1)),
                  pl.BlockSpec(w.shape, lambda i: (0, 0, 0, 0)),
                  pl.BlockSpec(pe.shape, lambda i: (0, 0))],
        out_specs=pl.BlockSpec((1, n_chunk, CMP_W), lambda i: (i, 0, 0)),
        out_shape=jax.ShapeDtypeStruct((b, n_chunk, CMP_W), F32),
        compiler_params=_cparams(("parallel",), 32), name="nsa_compress",
    )(rows, rows, w, pe)


def _c2s(n_cmp_pad, n_sel_pad):
    cs = np.arange(n_cmp_pad)[:, None] * CMP_STRIDE
    ss = np.arange(n_sel_pad)[None, :] * SEL_BLOCK
    return ((cs < ss + SEL_BLOCK) & (cs + CMP_BLOCK > ss)).astype(np.float32)


def _expand(n_sel_pad, n_keys):
    s = np.arange(n_sel_pad)[:, None]
    k = np.arange(n_keys)[None, :]
    return (k // SEL_BLOCK == s).astype(np.float32)


def _cmp_probs(s, valid):
    s = jnp.where(valid, s, NEG)
    p = jnp.where(valid, jnp.exp(s - jnp.max(s, axis=-1, keepdims=True)), 0.0)
    return p / jnp.maximum(jnp.sum(p, axis=-1, keepdims=True), 1e-30)


def _select_blocks(imp, q_blk, n_sel):
    lane = lax.broadcasted_iota(jnp.int32, imp.shape, 1)
    visible = lane <= q_blk
    forced = (lane == 0) | (lane == q_blk) | (lane == q_blk - 1)
    score = jnp.where(visible, imp + jnp.where(forced, FORCED_BONUS, 0.0), NEG)
    rank = jnp.zeros(imp.shape, F32)
    for sp in range(n_sel):
        col = score[:, sp:sp + 1]
        beats = (col > score) | ((col == score) & (lane > sp))
        rank = rank + jnp.where(beats, 1.0, 0.0)
    return jnp.where(visible & (rank < SEL_TOPN), 1.0, 0.0)


def _flash(q, k_ref, v_ref, j_lo, j_hi, mask_fn, tq, tk):
    def body(j, carry):
        m, l, acc = carry
        off = pl.multiple_of(j * tk, tk)
        s = _dot_t(q, k_ref[pl.ds(off, tk), :])
        s = jnp.where(mask_fn(j), s, NEG)
        m_new = jnp.maximum(m, jnp.max(s, axis=-1, keepdims=True))
        alpha = jnp.exp(m - m_new)
        p = jnp.exp(s - m_new)
        l = alpha * l + jnp.sum(p, axis=-1, keepdims=True)
        acc = alpha * acc + _dot(p.astype(BF16), v_ref[pl.ds(off, tk), :])
        return m_new, l, acc

    init = (jnp.full((tq, 1), NEG, F32), jnp.zeros((tq, 1), F32), jnp.zeros((tq, NSA_HD), F32))
    _, l, acc = lax.fori_loop(j_lo, j_hi, body, init)
    return acc / l


def _nsa_prompt_kernel_v1(qc_ref, qr_ref, kc_ref, vc_ref, ks_ref, vs_ref, kw_ref, vw_ref, g_ref, c2s_ref, e_ref,
                       o_ref, mask_scr, *, tq, n_sel, n_cmp_pad):
    i = pl.program_id(2)
    t0 = i * tq
    qpos = t0 + lax.broadcasted_iota(jnp.int32, (tq, 1), 0)
    cend = lax.broadcasted_iota(jnp.int32, (1, n_cmp_pad), 1) * CMP_STRIDE + (CMP_BLOCK - 1)
    valid_c = cend <= qpos
    psum = jnp.zeros((tq, n_cmp_pad), F32)
    o_cmp = []
    for g in range(NSA_GROUP):
        p = _cmp_probs(_dot_t(qc_ref[0, g], kc_ref[0, 0]), valid_c)
        psum = psum + p
        o_cmp.append(_dot(p.astype(BF16), vc_ref[0, 0]))
    imp = jnp.dot(psum, c2s_ref[...], preferred_element_type=F32, precision=lax.Precision.HIGHEST)
    sel = _select_blocks(imp, lax.shift_right_logical(qpos, 6), n_sel).astype(BF16)
    n_kt = mask_scr.shape[0]
    for jj in range(n_kt):
        mask_scr[jj] = _dot(sel, e_ref[:, jj * tq:(jj + 1) * tq])
    kiota = lax.broadcasted_iota(jnp.int32, (1, tq), 1)

    def sel_mask(j):
        return (mask_scr[j] > 0.5) & (j * tq + kiota <= qpos)

    def win_mask(j):
        rel = qpos - (j * tq + kiota)
        return (rel >= 0) & (rel < WINDOW)

    w_tiles = WINDOW // tq
    for g in range(NSA_GROUP):
        q = qr_ref[0, g]
        o_sel = _flash(q, ks_ref.at[0, 0], vs_ref.at[0, 0], 0, i + 1, sel_mask, tq, tq)
        o_win = _flash(q, kw_ref.at[0, 0], vw_ref.at[0, 0], jnp.maximum(i - w_tiles, 0), i + 1, win_mask, tq, tq)
        gt = g_ref[0, g]
        o = o_cmp[g] * gt[:, 0:1] + o_sel * gt[:, 1:2] + o_win * gt[:, 2:3]
        o_ref[0, :, g * NSA_HD:(g + 1) * NSA_HD] = o


def nsa_prompt_attn_v1(qc, qr, kcmp, vcmp, ksel, vsel, kwin, vwin, gates, tq):
    b, _, t, _ = qc.shape
    tq = min(tq, t)
    n_cmp_pad = kcmp.shape[2]
    n_sel = t // SEL_BLOCK
    n_sel_pad = -(-n_sel // LANE) * LANE
    c2s = jnp.asarray(_c2s(n_cmp_pad, n_sel_pad))
    e = jnp.asarray(_expand(n_sel_pad, t), dtype=BF16)
    kern = functools.partial(_nsa_prompt_kernel_v1, tq=tq, n_sel=n_sel, n_cmp_pad=n_cmp_pad)
    qspec = pl.BlockSpec((1, NSA_GROUP, tq, NSA_HD), lambda bb, k, i: (bb, k, i, 0))
    kvspec = lambda n: pl.BlockSpec((1, 1, n, NSA_HD), lambda bb, k, i: (bb, k, 0, 0))
    return pl.pallas_call(
        kern, grid=(b, NSA_KV_HEADS, t // tq),
        in_specs=[qspec, qspec, kvspec(n_cmp_pad), kvspec(n_cmp_pad), kvspec(t), kvspec(t), kvspec(t), kvspec(t),
                  pl.BlockSpec((1, NSA_GROUP, tq, 3), lambda bb, k, i: (bb, k, i, 0)),
                  pl.BlockSpec(c2s.shape, lambda bb, k, i: (0, 0)), pl.BlockSpec(e.shape, lambda bb, k, i: (0, 0))],
        out_specs=pl.BlockSpec((1, tq, NSA_GROUP * NSA_HD), lambda bb, k, i: (bb, i, k)),
        out_shape=jax.ShapeDtypeStruct((b, t, NSA_Q_W), F32),
        scratch_shapes=[pltpu.VMEM((t // tq, tq, tq), F32)],
        compiler_params=_cparams(("parallel", "parallel", "parallel"), 40), name="nsa_prompt_attn",
    )(qc, qr, kcmp, vcmp, ksel, vsel, kwin, vwin, gates, c2s, e)


def _pair_rms(x, w):
    lo = lax.broadcasted_iota(jnp.int32, x.shape, 1) < NSA_HD
    x2 = x * x
    s_lo = jnp.sum(jnp.where(lo, x2, 0.0), axis=-1, keepdims=True)
    s_hi = jnp.sum(jnp.where(lo, 0.0, x2), axis=-1, keepdims=True)
    ms = jnp.where(lo, s_lo, s_hi) * (1.0 / NSA_HD)
    return x * lax.rsqrt(ms + EPS) * w


def _pair_rope(x, cos_t, sin_t):
    lane = lax.broadcasted_iota(jnp.int32, x.shape, 1)
    first = (lane & (NSA_HD - 1)) < NSA_HD // 2
    partner = jnp.where(first, pltpu.roll(x, LANE - NSA_HD // 2, axis=1), pltpu.roll(x, NSA_HD // 2, axis=1))
    return x * cos_t + partner * sin_t


def _rope_tables(pos):
    half = NSA_HD // 2
    inv = jnp.power(ROPE_THETA, -jnp.arange(half, dtype=F32) / half)
    ang = pos.astype(F32)[:, None] * inv[None, :]
    c, s = jnp.cos(ang), jnp.sin(ang)
    return jnp.concatenate([c, c, c, c], axis=-1), jnp.concatenate([-s, s, -s, s], axis=-1)


def _nsa_prep_kernel_v2(q_ref, kv_ref, g_ref, cos_ref, sin_ref, qn_ref, kn_ref,
                        rows_ref, wrows_ref, qc_ref, qr_ref, gate_ref, kvb_ref=None, cmp_ref=None):
    cos_t, sin_t = cos_ref[...], sin_ref[...]
    lo = lax.broadcasted_iota(jnp.int32, cos_t.shape, 1) < NSA_HD
    for c in range(NSA_HEADS // 2):
        y = _pair_rms(q_ref[:, c * LANE:(c + 1) * LANE], qn_ref[...]) * (NSA_HD ** -0.5)
        in_low_lanes = (2 * c) // NSA_GROUP == 0
        for src, dst in ((y, qc_ref), (_pair_rope(y, cos_t, sin_t) * LOG2E, qr_ref)):
            swapped = pltpu.roll(src, NSA_HD, axis=1)
            if in_low_lanes:
                even, odd = jnp.where(lo, src, 0.0), jnp.where(lo, swapped, 0.0)
            else:
                even, odd = jnp.where(lo, 0.0, swapped), jnp.where(lo, 0.0, src)
            dst[:, (2 * c) * LANE:(2 * c + 1) * LANE] = even.astype(BF16)
            dst[:, (2 * c + 1) * LANE:(2 * c + 2) * LANE] = odd.astype(BF16)
    for br in range(3):
        k = _pair_rms(kv_ref[:, br * 2 * LANE:br * 2 * LANE + LANE], kn_ref[br:br + 1, :])
        if br > 0:
            k = _pair_rope(k, cos_t, sin_t)
        v = kv_ref[:, br * 2 * LANE + LANE:(br + 1) * 2 * LANE]
        if cmp_ref is None:
            dst, off = (rows_ref, br * 2 * LANE) if br < 2 else (wrows_ref, 0)
            dst[:, off:off + LANE] = k
            dst[:, off + LANE:off + 2 * LANE] = v
        else:
            dst, w0 = (rows_ref, 2 * br) if br < 2 else (wrows_ref, 0)
            for w, x in ((w0, k), (w0 + 1, v)):
                xt = x.T
                for h in range(NSA_KV_HEADS):
                    dst[0, w, h] = xt[h * NSA_HD:(h + 1) * NSA_HD]
            if br == 0:
                cmp_ref[:, :LANE] = k
                cmp_ref[:, LANE:] = v
            else:
                kvb_ref[:, (br - 1) * LANE:br * LANE] = k.astype(BF16)
                kvb_ref[:, 2 * br * LANE:(2 * br + 1) * LANE] = jnp.where(lo, v, 1.0).astype(BF16)
                kvb_ref[:, (2 * br + 1) * LANE:(2 * br + 2) * LANE] = jnp.where(lo, 1.0, v).astype(BF16)
    gate_ref[...] = _sigmoid(g_ref[...])


def nsa_prep_v2(q_pre, kv_pre, gate_pre, cos_t, sin_t, qn, kn, tm, dims_major):
    n_rows = q_pre.shape[0]
    tm = min(tm, n_rows)
    n_tab = cos_t.shape[0] // tm
    rows = lambda w: pl.BlockSpec((tm, w), lambda i: (i, 0))
    tab = pl.BlockSpec((tm, LANE), lambda i: (i % n_tab, 0))
    widths = (NSA_HEADS * LANE, NSA_HEADS * LANE, LANE)
    dtypes = (BF16, BF16, F32)
    out_specs = [rows(w) for w in widths]
    out_shape = [jax.ShapeDtypeStruct((n_rows, w), dt) for w, dt in zip(widths, dtypes)]
    if dims_major:
        n_seq, t = n_rows // cos_t.shape[0], cos_t.shape[0]
        kv_spec = lambda n: pl.BlockSpec((1, n, NSA_KV_HEADS, NSA_HD, tm), lambda i: (i // n_tab, 0, 0, 0, i % n_tab))
        kv_shape = lambda n: jax.ShapeDtypeStruct((n_seq, n, NSA_KV_HEADS, NSA_HD, t), F32)
        out_specs = [kv_spec(4), kv_spec(2)] + out_specs + [rows(NSA_KV_W), rows(2 * LANE)]
        out_shape = [kv_shape(4), kv_shape(2)] + out_shape + [jax.ShapeDtypeStruct((n_rows, NSA_KV_W), BF16),
                                                              jax.ShapeDtypeStruct((n_rows, 2 * LANE), F32)]
    else:
        out_specs = [rows(NSA_Q_W), rows(2 * LANE)] + out_specs
        out_shape = [jax.ShapeDtypeStruct((n_rows, NSA_Q_W), F32),
                     jax.ShapeDtypeStruct((n_rows, 2 * LANE), F32)] + out_shape
    return pl.pallas_call(
        _nsa_prep_kernel_v2, grid=(n_rows // tm,),
        in_specs=[rows(NSA_Q_W), rows(NSA_KV_W), rows(LANE), tab, tab,
                  pl.BlockSpec((1, LANE), lambda i: (0, 0)), pl.BlockSpec((3, LANE), lambda i: (0, 0))],
        out_specs=out_specs, out_shape=out_shape,
        compiler_params=_cparams(("parallel",), 40), name="nsa_prep",
    )(q_pre, kv_pre, gate_pre, cos_t, sin_t, jnp.tile(qn.reshape(1, NSA_HD), (1, 2)), jnp.tile(kn, (1, 2)))


def _select_blocks_t(imp_t, q_blk, n_sel):
    blk = lax.broadcasted_iota(jnp.int32, imp_t.shape, 0)
    visible = blk <= q_blk
    forced = jnp.where(blk == 0, 1.0, 0.0) + jnp.where(blk == q_blk, 1.0, 0.0) + jnp.where(blk == q_blk - 1, 1.0, 0.0)
    score = jnp.where(visible, imp_t + jnp.where(forced > 0.0, FORCED_BONUS, 0.0), NEG)
    rank = jnp.zeros(imp_t.shape, F32)
    for sp in range(n_sel):
        row = score[sp:sp + 1, :]
        tie = jnp.where(blk > sp, jnp.where(row == score, 1.0, 0.0), 0.0)
        rank = rank + jnp.where(row > score, 1.0, 0.0) + tie
    return jnp.where(visible, jnp.where(rank < SEL_TOPN, 1.0, 0.0), 0.0)


def _flash_init(tq):
    rows = NSA_GROUP * tq
    return tuple((jnp.full((rows, 1), NEG, F32), jnp.zeros((rows, LANE), F32)) for _ in range(NSA_KV_HEADS))


def _flash_step(gq, k_ref, v_refs, bias, j, carry, tq, tk):
    off = pl.multiple_of(j * tk, tk)
    k = k_ref[pl.ds(off, tk), :]
    out = []
    for g, (m, acc) in enumerate(carry):
        s = _dot_t(gq[g], k).astype(BF16).reshape(NSA_GROUP, tq, tk) + bias[g][None]
        s = s.reshape(NSA_GROUP * tq, tk)
        m_new = jnp.maximum(m, jnp.max(s, axis=-1, keepdims=True).astype(F32))
        p = jnp.exp2(s - m_new.astype(BF16))
        acc = jnp.exp2(m - m_new) * acc + _dot(p, v_refs[g][pl.ds(off, tk), :])
        out.append((m_new, acc))
    return tuple(out)


def _flash_finish(carry, tq):
    outs = []
    for g, (_, acc) in enumerate(carry):
        c = NSA_HD if g == 0 else 0
        o = acc / acc[:, c:c + 1]
        outs += [o[i * tq:(i + 1) * tq] for i in range(NSA_GROUP)]
    return outs


def _nsa_prompt_kernel_v2(qc_ref, qr_ref, kvc_ref, ks_ref, kw_ref, vs0_ref, vs1_ref, vw0_ref, vw1_ref, g_ref,
                          c2s_ref, e_ref, o_ref, *, tq, n_sel):
    i = pl.program_id(1)
    t0 = i * tq
    n_rows = NSA_HEADS * tq
    stack = lambda ref: jnp.concatenate([ref[:, h * LANE:(h + 1) * LANE] for h in range(NSA_HEADS)], axis=0)
    qpos = t0 + lax.broadcasted_iota(jnp.int32, (tq, 1), 0)
    n_cmp = kvc_ref.shape[1]
    kc = kvc_ref[0, :, :LANE].astype(BF16)
    vc = kvc_ref[0, :, LANE:].astype(BF16)
    cend = lax.broadcasted_iota(jnp.int32, (1, n_cmp), 1) * CMP_STRIDE + (CMP_BLOCK - 1)
    s = _dot_t(stack(qc_ref), kc).reshape(NSA_KV_HEADS, NSA_GROUP, tq, n_cmp)
    p = _cmp_probs(s, (cend <= qpos)[None, None])
    o_cmp = _dot(p.reshape(n_rows, n_cmp).astype(BF16), vc)
    psum = p[:, 0]
    for g in range(1, NSA_GROUP):
        psum = psum + p[:, g]
    imp = jnp.dot(psum.reshape(NSA_KV_HEADS * tq, n_cmp), c2s_ref[...], preferred_element_type=F32,
                  precision=lax.Precision.HIGHEST)
    n_blk_rows = -(-n_sel // 8) * 8
    col = lax.broadcasted_iota(jnp.int32, (1, NSA_KV_HEADS * tq), 1)
    q_blk = lax.shift_right_logical(t0 + (col & (tq - 1)), 6)
    sel_t = _select_blocks_t(imp.T[:n_blk_rows], q_blk, n_sel)
    sel_t = jnp.concatenate([sel_t, jnp.zeros((LANE - n_blk_rows, NSA_KV_HEADS * tq), F32)], axis=0)
    sel = sel_t.T.astype(BF16)
    kiota = lax.broadcasted_iota(jnp.int32, (1, tq), 1)

    def sel_bias(j):
        causal = (j * tq + kiota) <= qpos
        return [((jnp.where(causal, _dot(sel[k * tq:(k + 1) * tq], e_ref[j]), 0.0) - 1.0) * -NEG).astype(BF16)
                for k in range(NSA_KV_HEADS)]

    def win_bias(j):
        rel = qpos - (j * tq + kiota)
        return [jnp.where(jnp.where(rel >= 0, rel, WINDOW) < WINDOW, 0.0, NEG).astype(BF16)] * NSA_KV_HEADS

    gq = [jnp.concatenate([qr_ref[:, h * LANE:(h + 1) * LANE] for h in range(g * NSA_GROUP, (g + 1) * NSA_GROUP)],
                          axis=0) for g in range(NSA_KV_HEADS)]
    sel_step = lambda j, c: _flash_step(gq, ks_ref.at[0], [vs0_ref.at[0], vs1_ref.at[0]], sel_bias(j), j, c, tq, tq)
    win_step = lambda j, c: _flash_step(gq, kw_ref.at[0], [vw0_ref.at[0], vw1_ref.at[0]], win_bias(j), j, c, tq, tq)
    o_sel = _flash_finish(lax.fori_loop(0, i + 1, sel_step, _flash_init(tq)), tq)
    o_win = _flash_finish(lax.fori_loop(jnp.maximum(i - WINDOW // tq, 0), i + 1, win_step, _flash_init(tq)), tq)
    lo = lax.broadcasted_iota(jnp.int32, (tq, LANE), 1) < NSA_HD
    for c in range(NSA_HEADS // 2):
        pair = []
        for h in (2 * c, 2 * c + 1):
            rs = slice(h * tq, (h + 1) * tq)
            gt = g_ref[:, 3 * h:3 * h + 3]
            pair.append(o_cmp[rs] * gt[:, 0:1] + o_sel[h] * gt[:, 1:2] + o_win[h] * gt[:, 2:3])
        even, odd = pair
        if (2 * c) // NSA_GROUP == 0:
            blk = jnp.where(lo, even, pltpu.roll(odd, NSA_HD, axis=1))
        else:
            blk = jnp.where(lo, pltpu.roll(even, NSA_HD, axis=1), odd)
        o_ref[:, c * LANE:(c + 1) * LANE] = blk


def nsa_prompt_attn_v2(qc, qr, kvc, kvb, gates, n_batch, tq):
    n_rows = qc.shape[0]
    t = n_rows // n_batch
    tq = min(tq, t)
    nt = t // tq
    n_cmp = kvc.shape[1]
    n_sel = t // SEL_BLOCK
    assert n_cmp % LANE == 0 and n_sel <= LANE and tq & (tq - 1) == 0 and WINDOW % tq == 0
    c2s = jnp.asarray(_c2s(n_cmp, LANE))
    e = jnp.asarray(_expand(LANE, t).reshape(LANE, nt, tq).transpose(1, 0, 2), dtype=BF16)
    kvb3 = kvb.reshape(n_batch, t, NSA_KV_W)
    rows = lambda w: pl.BlockSpec((tq, w), lambda b, i: (b * nt + i, 0))
    kv = lambda c: pl.BlockSpec((1, t, LANE), lambda b, i: (b, 0, c))
    return pl.pallas_call(
        functools.partial(_nsa_prompt_kernel_v2, tq=tq, n_sel=n_sel), grid=(n_batch, nt),
        in_specs=[rows(NSA_HEADS * LANE), rows(NSA_HEADS * LANE),
                  pl.BlockSpec((1, n_cmp, CMP_W), lambda b, i: (b, 0, 0)),
                  kv(0), kv(1), kv(2), kv(3), kv(4), kv(5), rows(LANE),
                  pl.BlockSpec(c2s.shape, lambda b, i: (0, 0)), pl.BlockSpec(e.shape, lambda b, i: (0, 0, 0))],
        out_specs=rows(NSA_Q_W),
        out_shape=jax.ShapeDtypeStruct((n_rows, NSA_Q_W), F32),
        compiler_params=_cparams(("parallel", "parallel"), 48), name="nsa_prompt_attn",
    )(qc, qr, kvc, kvb3, kvb3, kvb3, kvb3, kvb3, kvb3, gates, c2s, e)


LOG2E = 1.4426950408889634
KEY_TILE = 256


def _head_rows(yt, head_in_pair, kv_head):
    own = yt[head_in_pair * NSA_HD:(head_in_pair + 1) * NSA_HD]
    zero = jnp.zeros_like(own)
    return jnp.concatenate([own, zero] if kv_head == 0 else [zero, own], axis=0)


def _nsa_prep_kernel(q_ref, kv_ref, g_ref, cos_ref, sin_ref, qn_ref, kn_ref,
                     rows_ref, wrows_ref, qct_ref, qrt_ref, gt_ref, kb_ref=None, vt_ref=None, *, tm):
    cos_t, sin_t = cos_ref[...], sin_ref[...]
    for c in range(NSA_HEADS // 2):
        y = _pair_rms(q_ref[:, c * LANE:(c + 1) * LANE], qn_ref[...]) * (NSA_HD ** -0.5)
        yr = _pair_rope(y, cos_t, sin_t) * LOG2E
        for src, dst in ((y, qct_ref), (yr, qrt_ref)):
            st = src.T
            for e in range(2):
                h = 2 * c + e
                dst[h * LANE:(h + 1) * LANE, :] = _head_rows(st, e, h // NSA_GROUP).astype(BF16)
    for br in range(3):
        k = _pair_rms(kv_ref[:, br * 2 * LANE:br * 2 * LANE + LANE], kn_ref[br:br + 1, :])
        if br > 0:
            k = _pair_rope(k, cos_t, sin_t)
        v = kv_ref[:, br * 2 * LANE + LANE:(br + 1) * 2 * LANE]
        dst, off = (rows_ref, br * 2 * LANE) if br < 2 else (wrows_ref, 0)
        dst[:, off:off + LANE] = k
        dst[:, off + LANE:off + 2 * LANE] = v
        if br > 0 and kb_ref is not None:
            kb_ref[:, (br - 1) * LANE:br * LANE] = k.astype(BF16)
            vt = v.T.astype(BF16)
            for jt in range(tm // KEY_TILE):
                vt_ref[br - 1, jt] = vt[:, jt * KEY_TILE:(jt + 1) * KEY_TILE]
    gt_ref[...] = _sigmoid(g_ref[...]).T


def nsa_prep(q_pre, kv_pre, gate_pre, cos_t, sin_t, qn, kn, tm, attn_operands):
    n_rows = q_pre.shape[0]
    tm = min(tm, n_rows)
    n_tab = cos_t.shape[0] // tm
    rows = lambda w: pl.BlockSpec((tm, w), lambda i: (i, 0))
    cols = lambda r: pl.BlockSpec((r, tm), lambda i: (0, i))
    tab = pl.BlockSpec((tm, LANE), lambda i: (i % n_tab, 0))
    out_specs = [rows(NSA_Q_W), rows(2 * LANE), cols(NSA_HEADS * LANE), cols(NSA_HEADS * LANE), cols(LANE)]
    out_shape = [jax.ShapeDtypeStruct((n_rows, NSA_Q_W), F32), jax.ShapeDtypeStruct((n_rows, 2 * LANE), F32),
                 jax.ShapeDtypeStruct((NSA_HEADS * LANE, n_rows), BF16),
                 jax.ShapeDtypeStruct((NSA_HEADS * LANE, n_rows), BF16),
                 jax.ShapeDtypeStruct((LANE, n_rows), F32)]
    if attn_operands:
        assert tm % KEY_TILE == 0
        out_specs += [rows(2 * LANE), pl.BlockSpec((2, tm // KEY_TILE, LANE, KEY_TILE), lambda i: (0, i, 0, 0))]
        out_shape += [jax.ShapeDtypeStruct((n_rows, 2 * LANE), BF16),
                      jax.ShapeDtypeStruct((2, n_rows // KEY_TILE, LANE, KEY_TILE), BF16)]
    return pl.pallas_call(
        functools.partial(_nsa_prep_kernel, tm=tm), grid=(n_rows // tm,),
        in_specs=[rows(NSA_Q_W), rows(NSA_KV_W), rows(LANE), tab, tab,
                  pl.BlockSpec((1, LANE), lambda i: (0, 0)), pl.BlockSpec((3, LANE), lambda i: (0, 0))],
        out_specs=out_specs, out_shape=out_shape,
        compiler_params=_cparams(("parallel",), 40), name="nsa_prep",
    )(q_pre, kv_pre, gate_pre, cos_t, sin_t, jnp.tile(qn.reshape(1, NSA_HD), (1, 2)), jnp.tile(kn, (1, 2)))


def _flash_t(qts, k_ref, vt_ref, j_lo, j_hi, valid_fn, tq):
    def body(j, carry):
        k = k_ref[pl.ds(pl.multiple_of(j * KEY_TILE, KEY_TILE), KEY_TILE), :]
        vt = vt_ref[j]
        valid = valid_fn(j)
        out = []
        for h, (m, l, acc) in enumerate(carry):
            s = jnp.where(valid[h // NSA_GROUP], _dot(k, qts[h]), NEG)
            m_new = jnp.maximum(m, jnp.max(s, axis=0, keepdims=True))
            alpha = jnp.exp2(m - m_new)
            p = jnp.exp2(s - m_new)
            l = alpha * l + jnp.sum(p, axis=0, keepdims=True)
            acc = alpha * acc + _dot(vt, p.astype(BF16))
            out.append((m_new, l, acc))
        return tuple(out)

    init = tuple((jnp.full((1, tq), NEG, F32), jnp.zeros((1, tq), F32), jnp.zeros((LANE, tq), F32))
                 for _ in range(NSA_HEADS))
    return [acc / l for _, l, acc in lax.fori_loop(j_lo, j_hi, body, init)]


def _nsa_prompt_kernel(qct_ref, qrt_ref, kvc_ref, ks_ref, kw_ref, vst_ref, vwt_ref, gt_ref, c2st_ref, et_ref,
                       ot_ref, *, tq, n_sel):
    i = pl.program_id(1)
    t0 = i * tq
    qpos = t0 + lax.broadcasted_iota(jnp.int32, (1, tq), 1)
    n_cmp = kvc_ref.shape[1]
    kc = kvc_ref[0, :, :LANE].astype(BF16)
    vct = kvc_ref[0, :, LANE:].T.astype(BF16)
    cend = lax.broadcasted_iota(jnp.int32, (n_cmp, 1), 0) * CMP_STRIDE + (CMP_BLOCK - 1)
    valid_c = cend <= qpos
    psum = [jnp.zeros((n_cmp, tq), F32) for _ in range(NSA_KV_HEADS)]
    o_cmp = []
    for h in range(NSA_HEADS):
        s = jnp.where(valid_c, _dot(kc, qct_ref[h * LANE:(h + 1) * LANE, :]), NEG)
        p = jnp.where(valid_c, jnp.exp(s - jnp.max(s, axis=0, keepdims=True)), 0.0)
        p = p / jnp.maximum(jnp.sum(p, axis=0, keepdims=True), 1e-30)
        psum[h // NSA_GROUP] = psum[h // NSA_GROUP] + p
        o_cmp.append(_dot(vct, p.astype(BF16)))
    n_blk_rows = -(-n_sel // 8) * 8
    imp_t = jnp.concatenate(
        [jnp.dot(c2st_ref[...], ps, preferred_element_type=F32, precision=lax.Precision.HIGHEST)[:n_blk_rows]
         for ps in psum], axis=1)
    col = lax.broadcasted_iota(jnp.int32, (1, NSA_KV_HEADS * tq), 1)
    sel_t = _select_blocks_t(imp_t, lax.shift_right_logical(t0 + (col & (tq - 1)), 6), n_sel)
    sel_t = jnp.concatenate([sel_t, jnp.zeros((LANE - n_blk_rows, NSA_KV_HEADS * tq), F32)], axis=0).astype(BF16)
    krow = lax.broadcasted_iota(jnp.int32, (KEY_TILE, 1), 0)

    def sel_valid(j):
        causal = (j * KEY_TILE + krow) <= qpos
        return [jnp.where(causal, _dot(et_ref[j], sel_t[:, k * tq:(k + 1) * tq]), 0.0) > 0.5
                for k in range(NSA_KV_HEADS)]

    def win_valid(j):
        rel = qpos - (j * KEY_TILE + krow)
        return [jnp.where(rel >= 0, rel, WINDOW) < WINDOW] * NSA_KV_HEADS

    qrt = [qrt_ref[h * LANE:(h + 1) * LANE, :] for h in range(NSA_HEADS)]
    n_kt = (t0 + tq) // KEY_TILE
    o_sel = _flash_t(qrt, ks_ref.at[0], vst_ref.at[0], 0, n_kt, sel_valid, tq)
    o_win = _flash_t(qrt, kw_ref.at[0], vwt_ref.at[0], jnp.maximum((t0 - WINDOW) // KEY_TILE, 0), n_kt,
                     win_valid, tq)
    for h in range(NSA_HEADS):
        g = gt_ref[3 * h:3 * h + 3, :]
        o = o_cmp[h] * g[0:1] + o_sel[h] * g[1:2] + o_win[h] * g[2:3]
        k = h // NSA_GROUP
        ot_ref[h * NSA_HD:(h + 1) * NSA_HD, :] = o[k * NSA_HD:(k + 1) * NSA_HD]


def nsa_prompt_attn(qct, qrt, kvc, kb, vt, gates_t, n_batch, tq):
    n_rows = qct.shape[1]
    t = n_rows // n_batch
    tq = min(tq, t)
    nt = t // tq
    n_kt = t // KEY_TILE
    n_cmp = kvc.shape[1]
    n_sel = t // SEL_BLOCK
    assert n_cmp % LANE == 0 and n_sel <= LANE and tq & (tq - 1) == 0 and tq % KEY_TILE == 0
    c2st = jnp.asarray(_c2s(n_cmp, LANE).T)
    et = jnp.asarray(_expand(LANE, t).T.reshape(n_kt, KEY_TILE, LANE), dtype=BF16)
    kb3 = kb.reshape(n_batch, t, 2 * LANE)
    vt4 = vt.reshape(2, n_batch, n_kt, LANE, KEY_TILE)
    cols = lambda r: pl.BlockSpec((r, tq), lambda b, i: (0, b * nt + i))
    kspec = lambda c: pl.BlockSpec((1, t, LANE), lambda b, i: (b, 0, c))
    vspec = lambda c: pl.BlockSpec((None, 1, n_kt, LANE, KEY_TILE), lambda b, i: (c, b, 0, 0, 0))
    return pl.pallas_call(
        functools.partial(_nsa_prompt_kernel, tq=tq, n_sel=n_sel), grid=(n_batch, nt),
        in_specs=[cols(NSA_HEADS * LANE), cols(NSA_HEADS * LANE),
                  pl.BlockSpec((1, n_cmp, CMP_W), lambda b, i: (b, 0, 0)),
                  kspec(0), kspec(1), vspec(0), vspec(1), cols(LANE),
                  pl.BlockSpec(c2st.shape, lambda b, i: (0, 0)), pl.BlockSpec(et.shape, lambda b, i: (0, 0, 0))],
        out_specs=cols(NSA_Q_W),
        out_shape=jax.ShapeDtypeStruct((NSA_Q_W, n_rows), F32),
        compiler_params=_cparams(("parallel", "parallel"), 48), name="nsa_prompt_attn",
    )(qct, qrt, kvc, kb3, kb3, vt4, vt4, gates_t, c2st, et)


SROWS = NSA_HEADS * 8
HALF_W = 2 * NSA_KV_HEADS * NSA_HD


def _page_copies(cache_ref, pt_ref, targets, sem, b, slot, n_pages, page):
    return [pltpu.make_async_copy(cache_ref.at[pt_ref[b, pg], kind],
                                  buf.at[slot, :, :, pl.ds(pg * page, page)], sem.at[slot])
            for pg in range(n_pages) for kind, buf in targets]


def _stream_pages(cache_ref, pt_ref, targets, sem, n_pages, page):
    b = pl.program_id(0)
    nb = pl.num_programs(0)
    slot = lax.rem(b, 2)

    @pl.when(b == 0)
    def _():
        for c in _page_copies(cache_ref, pt_ref, targets, sem, 0, 0, n_pages, page):
            c.start()

    @pl.when(b + 1 < nb)
    def _():
        for c in _page_copies(cache_ref, pt_ref, targets, sem, b + 1, 1 - slot, n_pages, page):
            c.start()

    for c in _page_copies(cache_ref, pt_ref, targets, sem, b, slot, n_pages, page):
        c.wait()
    return slot


def _nsa_s1_kernel(pt_ref, cache_ref, q_ref, w_ref, pe_ref, c2s_ref, ocmp_ref, sel_ref, bufk, bufv, xk, xv, sem,
                   *, n_pages, page, ts, n_sel):
    past = n_pages * page
    slot = _stream_pages(cache_ref, pt_ref, [(0, bufk), (1, bufv)], sem, n_pages, page)
    n_chunk = past // CMP_STRIDE
    tch = min(past, 8 * LANE)
    eye = jnp.where(lax.broadcasted_iota(jnp.int32, (LANE, LANE), 0) == lax.broadcasted_iota(jnp.int32, (LANE, LANE), 1),
                    1.0, 0.0).astype(BF16)
    for src, dst in ((bufk, xk), (bufv, xv)):
        for c in range(past // tch):
            xt = src[slot, :, :, c * tch:(c + 1) * tch].reshape(LANE, tch).astype(BF16)
            dst[c * tch:(c + 1) * tch, :] = lax.dot_general(xt, eye, (((0,), (0,)), ((), ())),
                                                            preferred_element_type=F32)
    kv = _compress(lambda j: xk[pl.ds(j, n_chunk, stride=CMP_STRIDE), :],
                   lambda j: xv[pl.ds(j, n_chunk, stride=CMP_STRIDE), :], w_ref, pe_ref, n_chunk)
    kc = kv[:, :LANE].astype(BF16)
    vc = kv[:, LANE:].astype(BF16)
    tok = lax.broadcasted_iota(jnp.int32, (SROWS, 1), 0) & (ts - 1)
    qpos = past + tok
    cend = lax.broadcasted_iota(jnp.int32, (1, n_chunk), 1) * CMP_STRIDE + (CMP_BLOCK - 1)
    p = _cmp_probs(_dot_t(q_ref[0], kc), cend <= qpos)
    ocmp_ref[0] = _dot(p.astype(BF16), vc)
    psum = []
    for k in range(NSA_KV_HEADS):
        acc = p[k * NSA_GROUP * ts:k * NSA_GROUP * ts + ts]
        for g in range(1, NSA_GROUP):
            r = (k * NSA_GROUP + g) * ts
            acc = acc + p[r:r + ts]
        psum.append(acc)
    psum = jnp.concatenate(psum, axis=0)
    imp = jnp.dot(psum, c2s_ref[...], preferred_element_type=F32, precision=lax.Precision.HIGHEST)
    q_blk = lax.shift_right_logical(past + (lax.broadcasted_iota(jnp.int32, (NSA_KV_HEADS * ts, 1), 0) & (ts - 1)), 6)
    sel_ref[0] = _select_blocks(imp, q_blk, n_sel)


def _nsa_s2_kernel(pt_ref, cache_ref, q_ref, sel_ref, new_ref, win_ref, wnew_ref, ocmp_ref, g_ref, e_ref,
                   o_ref, bufk, bufv, sem, *, n_pages, page, ts, wb):
    past = n_pages * page
    slot = _stream_pages(cache_ref, pt_ref, [(2, bufk), (3, bufv)], sem, n_pages, page)
    q = q_ref[0]
    tok = lax.broadcasted_iota(jnp.int32, (SROWS, 1), 0) & (ts - 1)
    sel = sel_ref[0]
    sel_rows = jnp.concatenate([sel[k * ts:(k + 1) * ts] for k in range(NSA_KV_HEADS) for _ in range(NSA_GROUP)],
                               axis=0)
    n_new = new_ref.shape[1]
    jnew = lax.broadcasted_iota(jnp.int32, (1, n_new), 1)

    def attend(kt_past, vt_past, valid_past, k_new, v_new, valid_new):
        s_past = jnp.where(valid_past, _dot(q, kt_past), NEG)
        s_new = jnp.where(valid_new, _dot_t(q, k_new), NEG)
        m = jnp.maximum(jnp.max(s_past, axis=-1, keepdims=True), jnp.max(s_new, axis=-1, keepdims=True))
        p_past = jnp.exp2(s_past - m)
        p_new = jnp.exp2(s_new - m)
        l = jnp.sum(p_past, axis=-1, keepdims=True) + jnp.sum(p_new, axis=-1, keepdims=True)
        return (_dot_t(p_past.astype(BF16), vt_past) + _dot(p_new.astype(BF16), v_new)) / l

    mask_past = _dot(sel_rows.astype(BF16), e_ref[...]) > 0.5
    blk_new = past // SEL_BLOCK
    valid_new = (sel_rows[:, blk_new:blk_new + 1] > 0.5) & (jnew <= tok)
    o_sel = attend(bufk[slot].reshape(LANE, past).astype(BF16), bufv[slot].reshape(LANE, past).astype(BF16),
                   mask_past, new_ref[0, :, 0:LANE].astype(BF16), new_ref[0, :, LANE:HALF_W].astype(BF16), valid_new)
    u = lax.broadcasted_iota(jnp.int32, (1, wb), 1)
    rel = wb + tok - u
    valid_w = (rel >= 0) & (rel < WINDOW) & (past - wb + u >= 0)
    o_win = attend(win_ref[0, 0].reshape(LANE, wb).astype(BF16), win_ref[0, 1].reshape(LANE, wb).astype(BF16),
                   valid_w, wnew_ref[0, :, 0:LANE].astype(BF16), wnew_ref[0, :, LANE:HALF_W].astype(BF16),
                   jnew <= tok)
    gt = g_ref[0]
    o = ocmp_ref[0] * gt[:, 0:1] + o_sel * gt[:, 1:2] + o_win * gt[:, 2:3]
    half = SROWS // NSA_KV_HEADS
    for k in range(NSA_KV_HEADS):
        o_ref[0, k * half:(k + 1) * half, :] = o[k * half:(k + 1) * half, k * NSA_HD:(k + 1) * NSA_HD]


def nsa_sample_attn(page_table, cache, qc2, qr2, new_sel, win, new_win, gates, cmp_w, cmp_pe, ts):
    bs, n_pages = page_table.shape
    page = cache.shape[-1]
    past = n_pages * page
    wb = win.shape[-1]
    assert past % SEL_BLOCK == 0 and ts <= CMP_STRIDE and ts & (ts - 1) == 0 and SROWS == NSA_HEADS * ts
    n_chunk = past // CMP_STRIDE
    n_sel = past // SEL_BLOCK + 1
    n_sel_pad = -(-n_sel // LANE) * LANE
    c2s = jnp.asarray(_c2s(n_chunk, n_sel_pad))
    e = jnp.asarray(_expand(n_sel_pad, past), dtype=BF16)
    sems = pltpu.SemaphoreType.DMA((2,))
    pages = pltpu.VMEM((2, NSA_KV_HEADS, NSA_HD, past), F32)
    scratch1 = [pages, pages, pltpu.VMEM((past, LANE), F32), pltpu.VMEM((past, LANE), F32), sems]
    scratch2 = [pages, pages, sems]
    seq3 = lambda b, pt: (b, 0, 0)
    o_cmp, sel = pl.pallas_call(
        functools.partial(_nsa_s1_kernel, n_pages=n_pages, page=page, ts=ts, n_sel=n_sel),
        grid_spec=pltpu.PrefetchScalarGridSpec(
            num_scalar_prefetch=1, grid=(bs,),
            in_specs=[pl.BlockSpec(memory_space=pl.ANY),
                      pl.BlockSpec((1, SROWS, LANE), seq3),
                      pl.BlockSpec(cmp_w.shape, lambda b, pt: (0, 0, 0, 0)),
                      pl.BlockSpec(cmp_pe.shape, lambda b, pt: (0, 0)),
                      pl.BlockSpec(c2s.shape, lambda b, pt: (0, 0))],
            out_specs=[pl.BlockSpec((1, SROWS, LANE), seq3),
                       pl.BlockSpec((1, NSA_KV_HEADS * ts, n_sel_pad), seq3)],
            scratch_shapes=scratch1),
        out_shape=[jax.ShapeDtypeStruct((bs, SROWS, LANE), F32),
                   jax.ShapeDtypeStruct((bs, NSA_KV_HEADS * ts, n_sel_pad), F32)],
        compiler_params=_cparams(("arbitrary",), 48), name="nsa_sample_cmp",
    )(page_table, cache, qc2, cmp_w, cmp_pe, c2s)
    n_new = new_sel.shape[1]
    return pl.pallas_call(
        functools.partial(_nsa_s2_kernel, n_pages=n_pages, page=page, ts=ts, wb=wb),
        grid_spec=pltpu.PrefetchScalarGridSpec(
            num_scalar_prefetch=1, grid=(bs,),
            in_specs=[pl.BlockSpec(memory_space=pl.ANY),
                      pl.BlockSpec((1, SROWS, LANE), seq3),
                      pl.BlockSpec((1, NSA_KV_HEADS * ts, n_sel_pad), seq3),
                      pl.BlockSpec((1, n_new, HALF_W), seq3),
                      pl.BlockSpec((1, 2, NSA_KV_HEADS, NSA_HD, wb), lambda b, pt: (b, 0, 0, 0, 0)),
                      pl.BlockSpec((1, n_new, HALF_W), seq3),
                      pl.BlockSpec((1, SROWS, LANE), seq3),
                      pl.BlockSpec((1, SROWS, 3), seq3),
                      pl.BlockSpec(e.shape, lambda b, pt: (0, 0))],
            out_specs=pl.BlockSpec((1, SROWS, NSA_HD), seq3),
            scratch_shapes=scratch2),
        out_shape=jax.ShapeDtypeStruct((bs, SROWS, NSA_HD), F32),
        compiler_params=_cparams(("arbitrary",), 56), name="nsa_sample_sel",
    )(page_table, cache, qr2, sel, new_sel, win, new_win, o_cmp, gates, e)


def _in_proj_kernel(x_ref, g_ref, wq_ref, wkv_ref, wg_ref, wqkv_ref, wz_ref, wba_ref, cos_ref, sin_ref, qn_ref,
                    kn_ref, *refs, n_nsa):
    nsa_refs, (qkv_ref, z_ref, ba_ref), (q_scr, kv_scr, gate_scr) = refs[:n_nsa], refs[n_nsa:n_nsa + 3], refs[n_nsa + 3:]
    x = x_ref[...]
    xn = (x * lax.rsqrt(jnp.mean(x * x, axis=-1, keepdims=True) + EPS) * g_ref[...]).astype(BF16)
    for w_ref, o_ref in ((wq_ref, q_scr), (wkv_ref, kv_scr), (wg_ref, gate_scr), (wqkv_ref, qkv_ref),
                         (wz_ref, z_ref), (wba_ref, ba_ref)):
        n = w_ref.shape[1]
        ch = _col_chunk(n)
        for c in range(0, n, ch):
            o_ref[:, c:c + ch] = _dot(xn, w_ref[:, c:c + ch])
    _nsa_prep_kernel_v2(q_scr, kv_scr, gate_scr, cos_ref, sin_ref, qn_ref, kn_ref, *nsa_refs)


def _in_proj(x2d, pos_rows, p, tm, dims_major):
    n_rows, d = x2d.shape
    tm = min(tm, n_rows)
    cos_t, sin_t = _rope_tables(pos_rows)
    n_tab = cos_t.shape[0] // tm
    ws = p["w_in_parts"]
    rows = lambda w: pl.BlockSpec((tm, w), lambda i: (i, 0))
    const = lambda shape: pl.BlockSpec(shape, lambda i: (0, 0))
    tab = pl.BlockSpec((tm, LANE), lambda i: (i % n_tab, 0))
    widths = (NSA_HEADS * LANE, NSA_HEADS * LANE, LANE)
    nsa_specs = [rows(w) for w in widths]
    nsa_shape = [jax.ShapeDtypeStruct((n_rows, w), dt) for w, dt in zip(widths, (BF16, BF16, F32))]
    if dims_major:
        n_seq, t = n_rows // cos_t.shape[0], cos_t.shape[0]
        kv_spec = lambda n: pl.BlockSpec((1, n, NSA_KV_HEADS, NSA_HD, tm), lambda i: (i // n_tab, 0, 0, 0, i % n_tab))
        kv_shape = lambda n: jax.ShapeDtypeStruct((n_seq, n, NSA_KV_HEADS, NSA_HD, t), F32)
        nsa_specs = [kv_spec(4), kv_spec(2)] + nsa_specs + [rows(NSA_KV_W), rows(2 * LANE)]
        nsa_shape = [kv_shape(4), kv_shape(2)] + nsa_shape + [jax.ShapeDtypeStruct((n_rows, NSA_KV_W), BF16),
                                                              jax.ShapeDtypeStruct((n_rows, 2 * LANE), F32)]
    else:
        nsa_specs = [rows(NSA_Q_W), rows(2 * LANE)] + nsa_specs
        nsa_shape = [jax.ShapeDtypeStruct((n_rows, NSA_Q_W), F32),
                     jax.ShapeDtypeStruct((n_rows, 2 * LANE), F32)] + nsa_shape
    gdn_w = [w.shape[1] for w in ws[3:]]
    outs = pl.pallas_call(
        functools.partial(_in_proj_kernel, n_nsa=len(nsa_specs)), grid=(n_rows // tm,),
        in_specs=[rows(d), const((1, d))]
        + [pl.BlockSpec(w.shape, lambda i: (0, 0), pipeline_mode=pl.Buffered(1)) for w in ws]
        + [tab, tab, const((1, LANE)), const((3, LANE))],
        out_specs=nsa_specs + [rows(w) for w in gdn_w],
        out_shape=nsa_shape + [jax.ShapeDtypeStruct((n_rows, w), F32) for w in gdn_w],
        scratch_shapes=[pltpu.VMEM((tm, w.shape[1]), F32) for w in ws[:3]],
        compiler_params=_cparams(("parallel",), 56), name="in_proj",
    )(x2d, p["attn_norm_w"].reshape(1, d), *ws, cos_t, sin_t,
      jnp.tile(p["nsa_q_norm_w"].reshape(1, NSA_HD), (1, 2)), jnp.tile(p["nsa_k_norm_w"], (1, 2)))
    nsa, (qkv_pre, z_pre, ba_pre) = outs[:len(nsa_specs)], outs[len(nsa_specs):]
    return nsa, qkv_pre, z_pre, ba_pre


def _tokens_major(a):
    return a.transpose(0, 4, 1, 2, 3)


def _mixer_out(x2d, o_nsa, o_gdn, p):
    return matmul_res([o_nsa, o_gdn], p["w_out"], x2d, 1024)


def _ffn(h, p, prev, n_seq, shift, tm):
    return conv_ffn(h, p["ffn_norm_w"], p["w_up"], p["ffn_conv_w"], p["ffn_conv_b"], p["w_down"], prev,
                    n_seq, shift, tm)


def _layer_prompt(x, mem, p):
    nb, t, d = x.shape
    x2d = x.reshape(nb * t, d)
    (rows_t, wrows_t, qc, qr, gates, kvb, cmp_rows), qkv_pre, z_pre, ba_pre = _in_proj(x2d, jnp.arange(t), p, 512, True)
    kvc = compress_prompt(cmp_rows.reshape(nb, t, CMP_W), p["cmp_bd"], p["cmp_pe2"])
    o_nsa = nsa_prompt_attn_v2(qc, qr, kvc, kvb, gates, nb, 256)
    o_gdn, gdn_buf, gdn_s = gdn2(qkv_pre, z_pre, ba_pre, jnp.zeros((nb, GDN_CONV - 1, GDN_C3), F32),
                                 jnp.zeros((nb, GDN_HEADS, GDN_HD, GDN_HD), F32), p["gdn_conv_w"], p["gdn_a_log"],
                                 p["gdn_dt_bias"], p["gdn_norm_w"], nb, 4)
    mem_kv, mem_kv_bf = mem_kv_proj(mem.reshape(-1, d), p["mem_norm_w"], p["w_xk"], p["w_xv"], p["xk_norm_w"])
    h = xattn_fused(_mixer_out(x2d, o_nsa, o_gdn, p), p["xattn_norm_w"], p["w_xq"], p["xq_norm_w"],
                    mem_kv_bf.reshape(nb, mem.shape[1], 2 * XA_W), p["w_xo"], nb, 512)
    y, ffn_buf = _ffn(h, p, jnp.zeros((nb, FFN_CONV - 1, 2 * D_FF), F32), nb, 1, 512)
    keep = min(WINDOW, t)
    return (y.reshape(nb, t, d), _tokens_major(rows_t), _tokens_major(wrows_t[..., t - keep:]), gdn_s, gdn_buf,
            ffn_buf, mem_kv.reshape(nb, mem.shape[1], 2, XA_HEADS, XA_HD))


def _layer_sample(x, cache_kv, cache_win, state, gdn_cache, ffn_cache, mem_kv, page_table, p):
    bs, ts, d = x.shape
    n_pool, page = cache_kv.shape[:2]
    past = page_table.shape[1] * page
    x2d = x.reshape(bs * ts, d)
    pos_rows = jnp.tile(past + jnp.arange(ts), bs)
    (rows, wrows, qc, qr, gates), qkv_pre, z_pre, ba_pre = _in_proj(x2d, pos_rows, p, bs * ts, False)
    dims_major = lambda a: a.transpose(0, 2, 3, 4, 1)

    def stack_q(a):
        return a.reshape(bs, ts, NSA_HEADS, LANE).transpose(0, 2, 1, 3).reshape(bs, SROWS, LANE)

    def pad_new(a):
        return jnp.pad(a.reshape(bs, ts, HALF_W), ((0, 0), (0, LANE - ts), (0, 0)))

    wb = cache_win.shape[1]
    gates3 = gates[:, :NSA_GATE_W].reshape(bs, ts, NSA_HEADS, 3).transpose(0, 2, 1, 3).reshape(bs, SROWS, 3)
    win_t = dims_major(cache_win)
    o2 = nsa_sample_attn(page_table, dims_major(cache_kv), stack_q(qc), stack_q(qr),
                         pad_new(rows[:, HALF_W:]), win_t, pad_new(wrows), gates3, p["cmp_bd"], p["cmp_pe2"], ts)
    rows = rows.reshape(bs, ts, 4, NSA_KV_HEADS, NSA_HD)
    wrows = wrows.reshape(bs, ts, 2, NSA_KV_HEADS, NSA_HD)
    o_nsa = o2.reshape(bs, NSA_HEADS, ts, NSA_HD).transpose(0, 2, 1, 3).reshape(bs * ts, NSA_Q_W)
    o_gdn, gdn_buf, gdn_s = gdn2(qkv_pre, z_pre, ba_pre, gdn_cache, state, p["gdn_conv_w"], p["gdn_a_log"],
                                 p["gdn_dt_bias"], p["gdn_norm_w"], bs, 4)
    h1 = _mixer_out(x2d, o_nsa, o_gdn, p)
    (qx,) = rms_matmul(h1, p["xattn_norm_w"], [p["w_xq"]], 512)
    m = mem_kv.shape[1]
    kv_rows = mem_kv.reshape(bs, m, 2, XA_HEADS, XA_HD // LANE, LANE).transpose(0, 1, 2, 4, 3, 5)
    kv_rows = kv_rows.reshape(bs, m * MEM_ROWS, LANE)
    h = matmul_res([xattn_cache(qx, kv_rows, p["xq_norm_w"], bs)], p["w_xo"], h1, 512)
    h_tm = h.reshape(bs, ts, d).transpose(1, 0, 2).reshape(ts * bs, d)
    prev = ffn_cache.transpose(1, 0, 2).reshape(1, (FFN_CONV - 1) * bs, 2 * D_FF)
    y, ffn_buf = _ffn(h_tm, p, prev, 1, bs, ts * bs)
    y = y.reshape(ts, bs, d).transpose(1, 0, 2)
    ffn_buf = ffn_buf.reshape(FFN_CONV - 1, bs, 2 * D_FF).transpose(1, 0, 2)
    wall_t = jnp.concatenate([win_t, dims_major(wrows)], axis=-1)
    keep = min(WINDOW, wb + ts)
    return y, rows, _tokens_major(wall_t[..., wb + ts - keep:]), gdn_s, gdn_buf, ffn_buf


def _prep_params(l, attn_norm_w, w_in, nsa_q_norm_w, nsa_k_norm_w, cmp_pe, cmp_w, gdn_conv_w, gdn_a_log,
                 gdn_dt_bias, gdn_norm_w, w_out, mem_norm_w, w_xk, w_xv, xk_norm_w, xattn_norm_w, w_xq,
                 xq_norm_w, w_xo, ffn_norm_w, w_up, ffn_conv_w, ffn_conv_b, w_down):
    wi = w_in[l]
    cuts = np.cumsum([0, NSA_Q_W, NSA_KV_W, NSA_GATE_W, 3 * GDN_W, GDN_W, 2 * GDN_HEADS])
    parts = []
    for a, b in zip(cuts[:-1], cuts[1:]):
        w = wi[:, a:b]
        padn = -(-(b - a) // LANE) * LANE - (b - a)
        parts.append(jnp.pad(w, ((0, 0), (0, padn))).astype(BF16))
    cmp_bd, cmp_pe2 = _cmp_weights(cmp_pe[l], cmp_w[l])
    bf = lambda w: w[l].astype(BF16)
    return dict(attn_norm_w=attn_norm_w[l], w_in_parts=parts, nsa_q_norm_w=nsa_q_norm_w[l],
                nsa_k_norm_w=nsa_k_norm_w[l], cmp_bd=cmp_bd, cmp_pe2=cmp_pe2, gdn_conv_w=gdn_conv_w[l],
                gdn_a_log=gdn_a_log[l], gdn_dt_bias=gdn_dt_bias[l], gdn_norm_w=gdn_norm_w[l], w_out=bf(w_out),
                mem_norm_w=mem_norm_w[l], w_xk=bf(w_xk), w_xv=bf(w_xv), xk_norm_w=xk_norm_w[l],
                xattn_norm_w=xattn_norm_w[l], w_xq=bf(w_xq), xq_norm_w=xq_norm_w[l], w_xo=bf(w_xo),
                ffn_norm_w=ffn_norm_w[l], w_up=bf(w_up), ffn_conv_w=ffn_conv_w[l], ffn_conv_b=ffn_conv_b[l],
                w_down=bf(w_down))


def kernel(x_prompt, x_sample, mem_prompt, cache_nsa_kv, cache_nsa_win, state_gdn, cache_gdn_conv, cache_ffn_conv,
           cache_mem_kv, page_table, attn_norm_w, w_in, nsa_q_norm_w, nsa_k_norm_w, cmp_pe, cmp_w, gdn_conv_w,
           gdn_a_log, gdn_dt_bias, gdn_norm_w, w_out, mem_norm_w, w_xk, w_xv, xk_norm_w, xattn_norm_w, w_xq,
           xq_norm_w, w_xo, ffn_norm_w, w_up, ffn_conv_w, ffn_conv_b, w_down):
    weights = (attn_norm_w, w_in, nsa_q_norm_w, nsa_k_norm_w, cmp_pe, cmp_w, gdn_conv_w, gdn_a_log, gdn_dt_bias,
               gdn_norm_w, w_out, mem_norm_w, w_xk, w_xv, xk_norm_w, xattn_norm_w, w_xq, xq_norm_w, w_xo,
               ffn_norm_w, w_up, ffn_conv_w, ffn_conv_b, w_down)
    depth = cache_nsa_kv.shape[0]
    hp, hs = x_prompt, x_sample
    outs_p, outs_s = [], []
    for l in range(depth):
        p = _prep_params(l, *weights)
        res_p = _layer_prompt(hp, mem_prompt, p)
        hp = res_p[0]
        outs_p.append(res_p[1:])
        res_s = _layer_sample(hs, cache_nsa_kv[l], cache_nsa_win[l], state_gdn[l], cache_gdn_conv[l],
                              cache_ffn_conv[l], cache_mem_kv[l], page_table, p)
        hs = res_s[0]
        outs_s.append(res_s[1:])
    stack = lambda outs, i: jnp.stack([o[i] for o in outs])
    return ((hp, hs) + tuple(stack(outs_p, i) for i in range(6)) + tuple(stack(outs_s, i) for i in range(5)))
```

```python
import functools
import math

import numpy as np
import jax
import jax.numpy as jnp
from jax import lax
from jax.experimental import pallas as pl
from jax.experimental.pallas import tpu as pltpu

F32 = jnp.float32
BF16 = jnp.bfloat16

D_MODEL = 1024
NSA_HEADS = 8
NSA_KV_HEADS = 2
NSA_GROUP = NSA_HEADS // NSA_KV_HEADS
NSA_HD = 64
CMP_BLOCK = 32
CMP_STRIDE = 16
SEL_BLOCK = 64
SEL_TOPN = 8
WINDOW = 512
FORCED_BONUS = 1e4
GDN_HEADS = 4
GDN_HD = 128
GDN_CONV = 4
GDN_CHUNK = 64
GDN_W = GDN_HEADS * GDN_HD
XA_HEADS = 4
XA_HD = 256
D_FF = 2816
FFN_CONV = 3
ROPE_THETA = 10000.0
EPS = 1e-6
NEG = -1e30
LOG2E = 1.4426950408889634

NSA_Q_W = NSA_HEADS * NSA_HD
NSA_KV_W = 3 * 2 * NSA_KV_HEADS * NSA_HD
NSA_GATE_W = 3 * NSA_HEADS
LANE = 128
MIB = 1 << 20


def _cparams(sem, vmem_mib):
    return pltpu.CompilerParams(dimension_semantics=sem, vmem_limit_bytes=vmem_mib * MIB)


def _sigmoid(x):
    return 1.0 / (1.0 + jnp.exp(-x))


def _dot(a, b):
    return jnp.dot(a, b, preferred_element_type=F32)


def _dot_t(a, b):
    return lax.dot_general(a, b, (((1,), (1,)), ((), ())), preferred_element_type=F32)


def _col_chunk(n):
    for c in (512, 256, 128):
        if n % c == 0:
            return c
    return n


def _rms_mm_kernel(x_ref, g_ref, *refs, n_out):
    w_refs, o_refs = refs[:n_out], refs[n_out:]
    x = x_ref[...]
    xn = (x * lax.rsqrt(jnp.mean(x * x, axis=-1, keepdims=True) + EPS) * g_ref[...]).astype(BF16)
    for w_ref, o_ref in zip(w_refs, o_refs):
        n = w_ref.shape[1]
        ch = _col_chunk(n)
        for c in range(0, n, ch):
            o_ref[:, c:c + ch] = _dot(xn, w_ref[:, c:c + ch])


def rms_matmul(x, g, ws, tm):
    n_rows, d = x.shape
    tm = min(tm, n_rows)
    in_specs = [pl.BlockSpec((tm, d), lambda i: (i, 0)), pl.BlockSpec((1, d), lambda i: (0, 0))]
    in_specs += [pl.BlockSpec(w.shape, lambda i: (0, 0), pipeline_mode=pl.Buffered(1)) for w in ws]
    out_specs = [pl.BlockSpec((tm, w.shape[1]), lambda i: (i, 0)) for w in ws]
    out_shape = [jax.ShapeDtypeStruct((n_rows, w.shape[1]), F32) for w in ws]
    return pl.pallas_call(
        functools.partial(_rms_mm_kernel, n_out=len(ws)),
        grid=(n_rows // tm,), in_specs=in_specs, out_specs=out_specs, out_shape=out_shape,
        compiler_params=_cparams(("parallel",), 56), name="rms_matmul",
    )(x, g.reshape(1, d), *ws)


def _mm_res_kernel(*refs, n_in):
    a_refs, (w_ref, r_ref, o_ref) = refs[:n_in], refs[n_in:]
    a = [a_ref[...].astype(BF16) for a_ref in a_refs]
    n = w_ref.shape[1]
    ch = _col_chunk(n)
    for c in range(0, n, ch):
        acc = r_ref[:, c:c + ch]
        k0 = 0
        for x in a:
            acc = acc + _dot(x, w_ref[k0:k0 + x.shape[1], c:c + ch])
            k0 += x.shape[1]
        o_ref[:, c:c + ch] = acc


def matmul_res(a_list, w, res, tm):
    n_rows = res.shape[0]
    k, n = w.shape
    assert sum(a.shape[1] for a in a_list) == k
    tm = min(tm, n_rows)
    return pl.pallas_call(
        functools.partial(_mm_res_kernel, n_in=len(a_list)), grid=(n_rows // tm,),
        in_specs=[pl.BlockSpec((tm, a.shape[1]), lambda i: (i, 0)) for a in a_list]
        + [pl.BlockSpec((k, n), lambda i: (0, 0), pipeline_mode=pl.Buffered(1)),
           pl.BlockSpec((tm, n), lambda i: (i, 0))],
        out_specs=pl.BlockSpec((tm, n), lambda i: (i, 0)),
        out_shape=jax.ShapeDtypeStruct((n_rows, n), F32),
        compiler_params=_cparams(("parallel",), 40), name="matmul_res",
    )(*a_list, w, res)


XA_W = XA_HEADS * XA_HD
MEM_ROWS = 2 * XA_W // LANE


def _memkv_kernel(x_ref, g_ref, wk_ref, wv_ref, kn_ref, o_ref, ob_ref):
    x = x_ref[...]
    xn = (x * lax.rsqrt(jnp.mean(x * x, axis=-1, keepdims=True) + EPS) * g_ref[...]).astype(BF16)
    for h in range(XA_HEADS):
        sl = slice(h * XA_HD, (h + 1) * XA_HD)
        vl = slice(XA_W + h * XA_HD, XA_W + (h + 1) * XA_HD)
        k = _dot(xn, wk_ref[:, sl])
        k = k * lax.rsqrt(jnp.mean(k * k, axis=-1, keepdims=True) + EPS) * kn_ref[...]
        v = _dot(xn, wv_ref[:, sl])
        o_ref[:, sl] = k
        o_ref[:, vl] = v
        ob_ref[:, sl] = k.astype(BF16)
        ob_ref[:, vl] = v.astype(BF16)


def mem_kv_proj(mem, g, wk, wv, kn, tm=256):
    n_rows, d = mem.shape
    out = pl.BlockSpec((tm, 2 * XA_W), lambda i: (i, 0))
    return pl.pallas_call(
        _memkv_kernel, grid=(n_rows // tm,),
        in_specs=[pl.BlockSpec((tm, d), lambda i: (i, 0)), pl.BlockSpec((1, d), lambda i: (0, 0)),
                  pl.BlockSpec((d, XA_W), lambda i: (0, 0)), pl.BlockSpec((d, XA_W), lambda i: (0, 0)),
                  pl.BlockSpec((1, XA_HD), lambda i: (0, 0))],
        out_specs=[out, out],
        out_shape=[jax.ShapeDtypeStruct((n_rows, 2 * XA_W), F32), jax.ShapeDtypeStruct((n_rows, 2 * XA_W), BF16)],
        compiler_params=_cparams(("parallel",), 40), name="mem_kv_proj",
    )(mem, g.reshape(1, d), wk, wv, kn.reshape(1, XA_HD))


def _xattn_head(q, qn, k, v):
    q = q * lax.rsqrt(jnp.mean(q * q, axis=-1, keepdims=True) + EPS) * qn
    s = _dot_t((q * (XA_HD ** -0.5)).astype(BF16), k)
    p = jnp.exp(s - jnp.max(s, axis=-1, keepdims=True))
    return _dot(p.astype(BF16), v) / jnp.sum(p, axis=-1, keepdims=True)


def _xattn_cache_kernel(q_ref, kv_ref, qn_ref, o_ref, *, m):
    def head_block(kind, h):
        halves = [kv_ref[0, pl.ds(kind * (MEM_ROWS // 2) + half * XA_HEADS + h, m, stride=MEM_ROWS), :]
                  for half in range(XA_HD // LANE)]
        return jnp.concatenate(halves, axis=1).astype(BF16)

    for h in range(XA_HEADS):
        sl = slice(h * XA_HD, (h + 1) * XA_HD)
        o_ref[:, sl] = _xattn_head(q_ref[:, sl], qn_ref[...], head_block(0, h), head_block(1, h))


def xattn_cache(q_pre, kv_rows, qn, n_batch):
    n_rows = q_pre.shape[0]
    tq = n_rows // n_batch
    m = kv_rows.shape[1] // MEM_ROWS
    return pl.pallas_call(
        functools.partial(_xattn_cache_kernel, m=m), grid=(n_batch,),
        in_specs=[pl.BlockSpec((tq, XA_W), lambda b: (b, 0)),
                  pl.BlockSpec((1, m * MEM_ROWS, LANE), lambda b: (b, 0, 0)),
                  pl.BlockSpec((1, XA_HD), lambda b: (0, 0))],
        out_specs=pl.BlockSpec((tq, XA_W), lambda b: (b, 0)),
        out_shape=jax.ShapeDtypeStruct((n_rows, XA_W), F32),
        compiler_params=_cparams(("parallel",), 40), name="xattn_cache",
    )(q_pre, kv_rows, qn.reshape(1, XA_HD))


def _xattn_fused_kernel(x_ref, a1_ref, a2_ref, wmix_ref, g_ref, wq_ref, qn_ref, kv_ref, wo_ref, o_ref):
    k1 = a1_ref.shape[1]
    x = (x_ref[...] + _dot(a1_ref[...].astype(BF16), wmix_ref[:k1, :])
         + _dot(a2_ref[...].astype(BF16), wmix_ref[k1:, :]))
    xn = (x * lax.rsqrt(jnp.mean(x * x, axis=-1, keepdims=True) + EPS) * g_ref[...]).astype(BF16)
    heads = []
    for h in range(XA_HEADS):
        sl = slice(h * XA_HD, (h + 1) * XA_HD)
        o = _xattn_head(_dot(xn, wq_ref[:, sl]), qn_ref[...], kv_ref[0, :, sl],
                        kv_ref[0, :, XA_W + h * XA_HD:XA_W + (h + 1) * XA_HD])
        heads.append(o.astype(BF16))
    a = jnp.concatenate(heads, axis=1)
    ch = _col_chunk(D_MODEL)
    for c in range(0, D_MODEL, ch):
        o_ref[:, c:c + ch] = x[:, c:c + ch] + _dot(a, wo_ref[:, c:c + ch])


def xattn_fused(x, a1, a2, wmix, g, wq, qn, kv_bf, wo, n_batch, tq):
    n_rows, d = x.shape
    nt = n_rows // n_batch // tq
    m = kv_bf.shape[1]
    const = lambda b, i: (0, 0)
    rows = lambda w: pl.BlockSpec((tq, w), lambda b, i: (b * nt + i, 0))
    return pl.pallas_call(
        _xattn_fused_kernel, grid=(n_batch, nt),
        in_specs=[rows(d), rows(a1.shape[1]), rows(a2.shape[1]),
                  pl.BlockSpec(wmix.shape, const, pipeline_mode=pl.Buffered(1)), pl.BlockSpec((1, d), const),
                  pl.BlockSpec((d, XA_W), const, pipeline_mode=pl.Buffered(1)), pl.BlockSpec((1, XA_HD), const),
                  pl.BlockSpec((1, m, 2 * XA_W), lambda b, i: (b, 0, 0)),
                  pl.BlockSpec((XA_W, d), const, pipeline_mode=pl.Buffered(1))],
        out_specs=pl.BlockSpec((tq, d), lambda b, i: (b * nt + i, 0)),
        out_shape=jax.ShapeDtypeStruct((n_rows, d), F32),
        compiler_params=_cparams(("parallel", "parallel"), 40), name="xattn_fused",
    )(x, a1, a2, wmix, g.reshape(1, d), wq, qn.reshape(1, XA_HD), kv_bf, wo)


FFN_ACT_CHUNK = 256


def _ffn_kernel(h_ref, g_ref, wup_ref, cw_ref, cb_ref, wdn_ref, prev_ref, o_ref, buf_ref, xs_scr,
                *, tm, shift, base):
    t = pl.program_id(1)
    p0 = base - 2 * shift

    @pl.when(t == 0)
    def _():
        xs_scr[p0:base, :] = prev_ref[0]

    x = h_ref[...]
    xn = (x * lax.rsqrt(jnp.mean(x * x, axis=-1, keepdims=True) + EPS) * g_ref[...]).astype(BF16)
    acc = jnp.zeros((tm, D_MODEL), F32)
    for j in range(D_FF // FFN_ACT_CHUNK):
        halves = []
        for c0 in (j * FFN_ACT_CHUNK, D_FF + j * FFN_ACT_CHUNK):
            sl = slice(c0, c0 + FFN_ACT_CHUNK)
            xs_scr[base:base + tm, sl] = _dot(xn, wup_ref[:, sl])
            y = (cw_ref[0:1, sl] * xs_scr[p0:p0 + tm, sl]
                 + cw_ref[1:2, sl] * xs_scr[p0 + shift:p0 + shift + tm, sl]
                 + cw_ref[2:3, sl] * xs_scr[base:base + tm, sl])
            halves.append(y + cb_ref[:, sl])
        a, u = halves
        act = (a * _sigmoid(a) * u).astype(BF16)
        acc = acc + _dot(act, wdn_ref[j * FFN_ACT_CHUNK:(j + 1) * FFN_ACT_CHUNK, :])
    o_ref[...] = x + acc
    last = xs_scr[base + tm - 2 * shift: base + tm, :]
    buf_ref[0] = last
    xs_scr[p0:base, :] = last


def conv_ffn(h, g, wup, cw, cb, wdn, prev, n_seq, shift, tm):
    n_rows, d = h.shape
    t_rows = n_rows // n_seq
    tm = min(tm, t_rows)
    nt = t_rows // tm
    base = -(-2 * shift // 8) * 8
    kern = functools.partial(_ffn_kernel, tm=tm, shift=shift, base=base)
    const = lambda b, i: (0, 0)
    return pl.pallas_call(
        kern, grid=(n_seq, nt),
        in_specs=[pl.BlockSpec((tm, d), lambda b, i: (b * nt + i, 0)),
                  pl.BlockSpec((1, d), const),
                  pl.BlockSpec((d, 2 * D_FF), const, pipeline_mode=pl.Buffered(1)),
                  pl.BlockSpec((FFN_CONV, 2 * D_FF), const),
                  pl.BlockSpec((1, 2 * D_FF), const),
                  pl.BlockSpec((D_FF, d), const, pipeline_mode=pl.Buffered(1)),
                  pl.BlockSpec((1, 2 * shift, 2 * D_FF), lambda b, i: (b, 0, 0))],
        out_specs=[pl.BlockSpec((tm, d), lambda b, i: (b * nt + i, 0)),
                   pl.BlockSpec((1, 2 * shift, 2 * D_FF), lambda b, i: (b, 0, 0))],
        out_shape=[jax.ShapeDtypeStruct((n_rows, d), F32),
                   jax.ShapeDtypeStruct((n_seq, 2 * shift, 2 * D_FF), F32)],
        scratch_shapes=[pltpu.VMEM((base + tm, 2 * D_FF), F32)],
        compiler_params=_cparams(("arbitrary", "arbitrary"), 56), name="conv_ffn",
    )(h, g.reshape(1, d), wup, cw, cb.reshape(1, 2 * D_FF), wdn, prev)


GDN_C3 = 3 * GDN_W


def _gdn_masks(chunk):
    n = GDN_HEADS * chunk
    r = np.arange(n)[:, None]
    c = np.arange(n)[None, :]
    same = lambda k: (r // k) == (c // k)
    m = [same(2) & (c < r)]
    k = 4
    while k <= chunk:
        m.append(same(k) & ~same(k // 2) & (c < r))
        k *= 2
    m += [same(chunk) & (c <= r), same(chunk) & (c < r), r == c]
    return np.stack(m).astype(np.float32)


def _gdn_kernel(qkv_ref, z_ref, ba_ref, prev_ref, s0_ref, cw_ref, par_ref, nw_ref, m_ref,
                o_ref, buf_ref, s_ref, xs_scr, s_scr, *, rv, chunk, n_units, sequential):
    t = pl.program_id(1)
    nt = pl.num_programs(1)
    stack = GDN_HEADS * chunk
    n_lvl = m_ref.shape[0] - 3
    causal, strict, eye = m_ref[n_lvl], m_ref[n_lvl + 1], m_ref[n_lvl + 2]
    kc = GDN_CONV - 1
    if sequential:
        base = [8 + u * rv for u in range(n_units)]

        @pl.when(t == 0)
        def _():
            xs_scr[8 - kc:8, :] = prev_ref[0]
            s_scr[...] = s0_ref[...]

        xs_scr[8:8 + n_units * rv, :] = qkv_ref[...]
    else:
        base = [u * (8 + rv) + 8 for u in range(n_units)]
        s_scr[...] = s0_ref[...]
        for u in range(n_units):
            xs_scr[base[u] - kc:base[u], :] = prev_ref[u]
            xs_scr[base[u]:base[u] + rv, :] = qkv_ref[u * rv:(u + 1) * rv, :]
    ba = ba_ref[...]
    beta_all = _sigmoid(ba)
    a_sh = ba + par_ref[1:2, :]
    g_all = -jnp.exp(par_ref[0:1, :]) * (jnp.maximum(a_sh, 0.0) + jnp.log1p(jnp.exp(-jnp.abs(a_sh))))
    pad = chunk - rv

    def padrows(a):
        if pad == 0:
            return a
        return jnp.concatenate([a, jnp.zeros((pad, a.shape[1]), a.dtype)], axis=0)

    qms, kms, vms, betas, gcols = [], [], [], [], []
    for u in range(n_units):
        r0 = u * rv
        y = cw_ref[0:1, :] * xs_scr[base[u] - kc:base[u] - kc + rv, :]
        for i in range(1, GDN_CONV):
            y = y + cw_ref[i:i + 1, :] * xs_scr[base[u] - kc + i:base[u] - kc + i + rv, :]
        y = y * _sigmoid(y)
        qs, ks, vs, bs, gs = [], [], [], [], []
        for h in range(GDN_HEADS):
            q = y[:, h * GDN_HD:(h + 1) * GDN_HD]
            k = y[:, GDN_W + h * GDN_HD:GDN_W + (h + 1) * GDN_HD]
            v = y[:, 2 * GDN_W + h * GDN_HD:2 * GDN_W + (h + 1) * GDN_HD]
            q = q * lax.rsqrt(jnp.sum(q * q, axis=-1, keepdims=True) + EPS) * (GDN_HD ** -0.5)
            k = k * lax.rsqrt(jnp.sum(k * k, axis=-1, keepdims=True) + EPS)
            qs.append(padrows(q)); ks.append(padrows(k)); vs.append(padrows(v))
            bs.append(padrows(beta_all[r0:r0 + rv, h:h + 1]))
            gs.append(padrows(g_all[r0:r0 + rv, GDN_HEADS + h:GDN_HEADS + h + 1]))
        qms.append(jnp.concatenate(qs, axis=0))
        kms.append(jnp.concatenate(ks, axis=0))
        vms.append(jnp.concatenate(vs, axis=0))
        betas.append(jnp.concatenate(bs, axis=0))
        gcols.append(jnp.concatenate(gs, axis=0))
    gmat = jnp.concatenate(gcols + [jnp.zeros((stack, LANE - n_units), F32)], axis=1)
    gc_all = jnp.dot(causal, gmat, preferred_element_type=F32, precision=lax.Precision.HIGHEST)

    a_bfs, qks, rhss, gcs, tinv = [], [], [], [], []
    for u in range(n_units):
        gc = gc_all[:, u:u + 1]
        gc_row = jnp.sum(eye * gc, axis=0, keepdims=True)
        decay = jnp.exp(jnp.where(causal > 0, gc - gc_row, 0.0)) * causal
        kb = kms[u] * betas[u]
        kmb = kms[u].astype(BF16)
        a_mat = _dot_t(kb.astype(BF16), kmb) * decay * strict
        qks.append((_dot_t(qms[u].astype(BF16), kmb) * decay).astype(BF16))
        rhss.append(jnp.concatenate([vms[u] * betas[u], kb * jnp.exp(gc)], axis=1).astype(BF16))
        a_bfs.append(a_mat.astype(BF16))
        gcs.append(gc)
        tinv.append(eye - a_mat * m_ref[0])
    for lvl in range(1, n_lvl):
        tb = [x.astype(BF16) for x in tinv]
        xs = [(_dot(a_bfs[u], tb[u]) * m_ref[lvl]).astype(BF16) for u in range(n_units)]
        tinv = [tinv[u] - _dot(tb[u], xs[u]) for u in range(n_units)]
    sols = [_dot(tinv[u].astype(BF16), rhss[u]) for u in range(n_units)]

    for u in range(n_units):
        r0 = u * rv
        st = 0 if sequential else u
        gc = gcs[u]
        u_all, w_all = sols[u][:, :GDN_HD], sols[u][:, GDN_HD:]
        qd = qms[u] * jnp.exp(gc)
        v_news, o_inter, gls = [], [], []
        for h in range(GDN_HEADS):
            rs = slice(h * chunk, (h + 1) * chunk)
            sb = s_scr[st, h].astype(BF16)
            both = _dot(jnp.concatenate([w_all[rs], qd[rs]], axis=0).astype(BF16), sb)
            v_news.append(u_all[rs] - both[:chunk])
            o_inter.append(both[chunk:])
            gls.append(gc[(h + 1) * chunk - 1:(h + 1) * chunk, :])
        v_stack = jnp.concatenate(v_news, axis=0).astype(BF16)
        o_intra = _dot(qks[u], v_stack)
        for h in range(GDN_HEADS):
            rs = slice(h * chunk, (h + 1) * chunk)
            kd = kms[u][rs] * jnp.exp(gls[h] - gc[rs])
            s_scr[st, h] = s_scr[st, h] * jnp.exp(gls[h]) + _dot(kd.T.astype(BF16), v_stack[rs])
            o = (o_inter[h] + o_intra[rs])[:rv]
            on = o * lax.rsqrt(jnp.mean(o * o, axis=-1, keepdims=True) + EPS) * nw_ref[...]
            zz = z_ref[r0:r0 + rv, h * GDN_HD:(h + 1) * GDN_HD]
            o_ref[r0:r0 + rv, h * GDN_HD:(h + 1) * GDN_HD] = on * (zz * _sigmoid(zz))

    if sequential:
        last = xs_scr[8 + n_units * rv - kc:8 + n_units * rv, :]
        buf_ref[0] = last
        xs_scr[8 - kc:8, :] = last

        @pl.when(t == nt - 1)
        def _():
            s_ref[...] = s_scr[...]
    else:
        for u in range(n_units):
            buf_ref[u] = xs_scr[base[u] + rv - kc:base[u] + rv, :]
        s_ref[...] = s_scr[...]


def gdn(qkv_pre, z_pre, ba_pre, prev, s0, conv_w, a_log, dt_bias, norm_w, n_seq, units):
    n_rows = qkv_pre.shape[0]
    t_rows = n_rows // n_seq
    sequential = t_rows >= GDN_CHUNK
    if sequential:
        rv = chunk = GDN_CHUNK
        units = min(units, t_rows // chunk)
        grid = (n_seq, t_rows // (units * chunk))
        seq_blk, scr_rows = 1, 8 + units * rv
    else:
        rv, chunk = t_rows, 16
        units = math.gcd(units, n_seq)
        assert rv % 8 == 0 and rv <= chunk
        grid = (n_seq // units, 1)
        seq_blk, scr_rows = units, units * (8 + rv)
    tt = units * rv
    nt = grid[1]
    par = jnp.zeros((8, LANE), F32).at[0, GDN_HEADS:2 * GDN_HEADS].set(a_log).at[1, GDN_HEADS:2 * GDN_HEADS].set(dt_bias)
    masks = jnp.asarray(_gdn_masks(chunk))
    const2 = lambda b, i: (0, 0)
    rows = lambda b, i: (b * nt + i, 0)
    return pl.pallas_call(
        functools.partial(_gdn_kernel, rv=rv, chunk=chunk, n_units=units, sequential=sequential), grid=grid,
        in_specs=[pl.BlockSpec((tt, GDN_C3), rows), pl.BlockSpec((tt, GDN_W), rows),
                  pl.BlockSpec((tt, LANE), rows),
                  pl.BlockSpec((seq_blk, GDN_CONV - 1, GDN_C3), lambda b, i: (b, 0, 0)),
                  pl.BlockSpec((seq_blk, GDN_HEADS, GDN_HD, GDN_HD), lambda b, i: (b, 0, 0, 0)),
                  pl.BlockSpec((GDN_CONV, GDN_C3), const2), pl.BlockSpec((8, LANE), const2),
                  pl.BlockSpec((1, GDN_HD), const2),
                  pl.BlockSpec(masks.shape, lambda b, i: (0, 0, 0))],
        out_specs=[pl.BlockSpec((tt, GDN_W), rows),
                   pl.BlockSpec((seq_blk, GDN_CONV - 1, GDN_C3), lambda b, i: (b, 0, 0)),
                   pl.BlockSpec((seq_blk, GDN_HEADS, GDN_HD, GDN_HD), lambda b, i: (b, 0, 0, 0))],
        out_shape=[jax.ShapeDtypeStruct((n_rows, GDN_W), F32),
                   jax.ShapeDtypeStruct((n_seq, GDN_CONV - 1, GDN_C3), F32),
                   jax.ShapeDtypeStruct((n_seq, GDN_HEADS, GDN_HD, GDN_HD), F32)],
        scratch_shapes=[pltpu.VMEM((scr_rows, GDN_C3), F32),
                        pltpu.VMEM((seq_blk, GDN_HEADS, GDN_HD, GDN_HD), F32)],
        compiler_params=_cparams(("arbitrary", "arbitrary"), 48), name="gdn",
    )(qkv_pre, z_pre, ba_pre, prev, s0, conv_w, par, norm_w.reshape(1, GDN_HD), masks)


CMP_W = 2 * NSA_KV_HEADS * NSA_HD


def _cmp_weights(cmp_pe, cmp_w):
    def bd(w):
        z = jnp.zeros_like(w)
        return jnp.concatenate([jnp.concatenate([w, z], 2), jnp.concatenate([z, w], 2)], 1)
    ws, cs = [], []
    for s in range(2):
        lo, hi = bd(cmp_w[s, :CMP_STRIDE]), bd(cmp_w[s, CMP_STRIDE:])
        cat = jnp.concatenate([lo, hi], axis=2)
        ws.append(cat.reshape(CMP_STRIDE // 2, 2 * LANE, 2 * LANE))
        pe2 = jnp.concatenate([cmp_pe[s], cmp_pe[s]], axis=-1)
        const = [jnp.einsum("jd,jde->e", pe2[h * CMP_STRIDE:(h + 1) * CMP_STRIDE], m,
                            precision=lax.Precision.HIGHEST) for h, m in enumerate((lo, hi))]
        cs.append(jnp.concatenate(const))
    return jnp.stack(ws).astype(BF16), jnp.stack(cs)


def _compress(read_k, read_v, w_ref, c_ref, n_chunk):
    outs = []
    for s, read in enumerate((read_k, read_v)):
        acc = jnp.zeros((n_chunk, 2 * LANE), F32)
        for p in range(CMP_STRIDE // 2):
            x = jnp.concatenate([read(2 * p), read(2 * p + 1)], axis=1).astype(BF16)
            acc = acc + _dot(x, w_ref[s, p])
        acc = acc + c_ref[s:s + 1, :]
        outs.append(acc[:, :LANE] + pltpu.roll(acc[:, LANE:], n_chunk - 1, axis=0))
    return jnp.concatenate(outs, axis=1)


def _cmp_kernel(k_ref, v_ref, w_ref, c_ref, o_ref, *, n_chunk):
    o_ref[0] = _compress(lambda j: k_ref[0, pl.ds(j, n_chunk, stride=CMP_STRIDE), :],
                         lambda j: v_ref[0, pl.ds(j, n_chunk, stride=CMP_STRIDE), :], w_ref, c_ref, n_chunk)


def compress_prompt(rows, w, c):
    b, t, _ = rows.shape
    n_chunk = t // CMP_STRIDE
    return pl.pallas_call(
        functools.partial(_cmp_kernel, n_chunk=n_chunk), grid=(b,),
        in_specs=[pl.BlockSpec((1, t, LANE), lambda i: (i, 0, 0)), pl.BlockSpec((1, t, LANE), lambda i: (i, 0, 1)),
                  pl.BlockSpec(w.shape, lambda i: (0, 0, 0, 0)),
                  pl.BlockSpec(c.shape, lambda i: (0, 0))],
        out_specs=pl.BlockSpec((1, n_chunk, CMP_W), lambda i: (i, 0, 0)),
        out_shape=jax.ShapeDtypeStruct((b, n_chunk, CMP_W), F32),
        compiler_params=_cparams(("parallel",), 32), name="nsa_compress",
    )(rows, rows, w, c)


def _c2s(n_cmp_pad, n_sel_pad):
    cs = np.arange(n_cmp_pad)[:, None] * CMP_STRIDE
    ss = np.arange(n_sel_pad)[None, :] * SEL_BLOCK
    return ((cs < ss + SEL_BLOCK) & (cs + CMP_BLOCK > ss)).astype(np.float32)


def _expand(n_sel_pad, n_keys):
    s = np.arange(n_sel_pad)[:, None]
    k = np.arange(n_keys)[None, :]
    return (k // SEL_BLOCK == s).astype(np.float32)


def _cmp_probs(s, valid):
    s = jnp.where(valid, s, NEG)
    p = jnp.where(valid, jnp.exp(s - jnp.max(s, axis=-1, keepdims=True)), 0.0)
    return p / jnp.maximum(jnp.sum(p, axis=-1, keepdims=True), 1e-30)


def _select_blocks(imp, q_blk, n_sel):
    lane = lax.broadcasted_iota(jnp.int32, imp.shape, 1)
    visible = lane <= q_blk
    forced = (lane == 0) | (lane == q_blk) | (lane == q_blk - 1)
    score = jnp.where(visible, imp + jnp.where(forced, FORCED_BONUS, 0.0), NEG)
    rank = jnp.zeros(imp.shape, F32)
    for sp in range(n_sel):
        col = score[:, sp:sp + 1]
        beats = (col > score) | ((col == score) & (lane > sp))
        rank = rank + jnp.where(beats, 1.0, 0.0)
    return jnp.where(visible & (rank < SEL_TOPN), 1.0, 0.0)


def _select_blocks_t(imp_t, q_blk, n_sel):
    blk = lax.broadcasted_iota(jnp.int32, imp_t.shape, 0)
    visible = blk <= q_blk
    forced = jnp.where(blk == 0, 1.0, 0.0) + jnp.where(blk == q_blk, 1.0, 0.0) + jnp.where(blk == q_blk - 1, 1.0, 0.0)
    score = jnp.where(visible, imp_t + jnp.where(forced > 0.0, FORCED_BONUS, 0.0), NEG)
    rank = jnp.zeros(imp_t.shape, F32)
    for sp in range(n_sel):
        row = score[sp:sp + 1, :]
        tie = jnp.where(blk > sp, jnp.where(row == score, 1.0, 0.0), 0.0)
        rank = rank + jnp.where(row > score, 1.0, 0.0) + tie
    return jnp.where(visible, jnp.where(rank < SEL_TOPN, 1.0, 0.0), 0.0)


def _pair_rms(x, w):
    lo = lax.broadcasted_iota(jnp.int32, x.shape, 1) < NSA_HD
    x2 = x * x
    s_lo = jnp.sum(jnp.where(lo, x2, 0.0), axis=-1, keepdims=True)
    s_hi = jnp.sum(jnp.where(lo, 0.0, x2), axis=-1, keepdims=True)
    ms = jnp.where(lo, s_lo, s_hi) * (1.0 / NSA_HD)
    return x * lax.rsqrt(ms + EPS) * w


def _pair_rope(x, cos_t, sin_t):
    lane = lax.broadcasted_iota(jnp.int32, x.shape, 1)
    first = (lane & (NSA_HD - 1)) < NSA_HD // 2
    partner = jnp.where(first, pltpu.roll(x, LANE - NSA_HD // 2, axis=1), pltpu.roll(x, NSA_HD // 2, axis=1))
    return x * cos_t + partner * sin_t


def _rope_tables(pos):
    half = NSA_HD // 2
    inv = jnp.power(ROPE_THETA, -jnp.arange(half, dtype=F32) / half)
    ang = pos.astype(F32)[:, None] * inv[None, :]
    c, s = jnp.cos(ang), jnp.sin(ang)
    return jnp.concatenate([c, c, c, c], axis=-1), jnp.concatenate([-s, s, -s, s], axis=-1)


def _nsa_prep_body(q_ref, kv_ref, g_ref, cos_ref, sin_ref, qn_ref, kn_ref,
                   rows_ref, wrows_ref, qc_ref, qr_ref, gate_ref, kvb_ref=None, cmp_ref=None):
    cos_t, sin_t = cos_ref[...], sin_ref[...]
    lo = lax.broadcasted_iota(jnp.int32, cos_t.shape, 1) < NSA_HD
    for c in range(NSA_HEADS // 2):
        y = _pair_rms(q_ref[:, c * LANE:(c + 1) * LANE], qn_ref[...]) * (NSA_HD ** -0.5)
        in_low_lanes = (2 * c) // NSA_GROUP == 0
        for src, dst in ((y, qc_ref), (_pair_rope(y, cos_t, sin_t) * LOG2E, qr_ref)):
            swapped = pltpu.roll(src, NSA_HD, axis=1)
            if in_low_lanes:
                even, odd = jnp.where(lo, src, 0.0), jnp.where(lo, swapped, 0.0)
            else:
                even, odd = jnp.where(lo, 0.0, swapped), jnp.where(lo, 0.0, src)
            dst[:, (2 * c) * LANE:(2 * c + 1) * LANE] = even.astype(BF16)
            dst[:, (2 * c + 1) * LANE:(2 * c + 2) * LANE] = odd.astype(BF16)
    for br in range(3):
        k = _pair_rms(kv_ref[:, br * 2 * LANE:br * 2 * LANE + LANE], kn_ref[br:br + 1, :])
        if br > 0:
            k = _pair_rope(k, cos_t, sin_t)
        v = kv_ref[:, br * 2 * LANE + LANE:(br + 1) * 2 * LANE]
        if cmp_ref is None:
            dst, off = (rows_ref, br * 2 * LANE) if br < 2 else (wrows_ref, 0)
            dst[:, off:off + LANE] = k
            dst[:, off + LANE:off + 2 * LANE] = v
        else:
            dst, w0 = (rows_ref, 2 * br) if br < 2 else (wrows_ref, 0)
            for w, x in ((w0, k), (w0 + 1, v)):
                xt = x.T
                for h in range(NSA_KV_HEADS):
                    dst[0, w, h] = xt[h * NSA_HD:(h + 1) * NSA_HD]
            if br == 0:
                cmp_ref[:, :LANE] = k
                cmp_ref[:, LANE:] = v
            else:
                kvb_ref[:, (br - 1) * LANE:br * LANE] = k.astype(BF16)
                kvb_ref[:, 2 * br * LANE:(2 * br + 1) * LANE] = jnp.where(lo, v, 1.0).astype(BF16)
                kvb_ref[:, (2 * br + 1) * LANE:(2 * br + 2) * LANE] = jnp.where(lo, 1.0, v).astype(BF16)
    gate_ref[...] = _sigmoid(g_ref[...])


def _flash_init(tq):
    rows = NSA_GROUP * tq
    return tuple((jnp.full((rows, 1), NEG, F32), jnp.zeros((rows, LANE), F32)) for _ in range(NSA_KV_HEADS))


def _flash_step(gq, k_ref, v_refs, bias, j, carry, tq, tk):
    off = pl.multiple_of(j * tk, tk)
    k = k_ref[pl.ds(off, tk), :]
    out = []
    for g, (m, acc) in enumerate(carry):
        s = _dot_t(gq[g], k).astype(BF16).reshape(NSA_GROUP, tq, tk) + bias[g][None]
        s = s.reshape(NSA_GROUP * tq, tk)
        m_new = jnp.maximum(m, jnp.max(s, axis=-1, keepdims=True).astype(F32))
        p = jnp.exp2(s - m_new.astype(BF16))
        acc = jnp.exp2(m - m_new) * acc + _dot(p, v_refs[g][pl.ds(off, tk), :])
        out.append((m_new, acc))
    return tuple(out)


def _flash_finish(carry, tq):
    outs = []
    for g, (_, acc) in enumerate(carry):
        c = NSA_HD if g == 0 else 0
        o = acc / acc[:, c:c + 1]
        outs += [o[i * tq:(i + 1) * tq] for i in range(NSA_GROUP)]
    return outs


def _nsa_prompt_kernel(qc_ref, qr_ref, kvc_ref, ks_ref, kw_ref, vs0_ref, vs1_ref, vw0_ref, vw1_ref, g_ref,
                       c2s_ref, e_ref, o_ref, *, tq, n_sel):
    i = pl.program_id(1)
    t0 = i * tq
    n_rows = NSA_HEADS * tq
    stack = lambda ref: jnp.concatenate([ref[:, h * LANE:(h + 1) * LANE] for h in range(NSA_HEADS)], axis=0)
    qpos = t0 + lax.broadcasted_iota(jnp.int32, (tq, 1), 0)
    n_cmp = kvc_ref.shape[1]
    kc = kvc_ref[0, :, :LANE].astype(BF16)
    vc = kvc_ref[0, :, LANE:].astype(BF16)
    cend = lax.broadcasted_iota(jnp.int32, (1, n_cmp), 1) * CMP_STRIDE + (CMP_BLOCK - 1)
    s = _dot_t(stack(qc_ref), kc).reshape(NSA_KV_HEADS, NSA_GROUP, tq, n_cmp)
    p = _cmp_probs(s, (cend <= qpos)[None, None])
    o_cmp = _dot(p.reshape(n_rows, n_cmp).astype(BF16), vc)
    psum = p[:, 0]
    for g in range(1, NSA_GROUP):
        psum = psum + p[:, g]
    imp = jnp.dot(psum.reshape(NSA_KV_HEADS * tq, n_cmp), c2s_ref[...], preferred_element_type=F32,
                  precision=lax.Precision.HIGHEST)
    n_blk_rows = -(-n_sel // 8) * 8
    col = lax.broadcasted_iota(jnp.int32, (1, NSA_KV_HEADS * tq), 1)
    q_blk = lax.shift_right_logical(t0 + (col & (tq - 1)), 6)
    sel_t = _select_blocks_t(imp.T[:n_blk_rows], q_blk, n_sel)
    sel_t = jnp.concatenate([sel_t, jnp.zeros((LANE - n_blk_rows, NSA_KV_HEADS * tq), F32)], axis=0)
    sel = sel_t.T.astype(BF16)
    kiota = lax.broadcasted_iota(jnp.int32, (1, tq), 1)

    def sel_bias(j):
        causal = (j * tq + kiota) <= qpos
        return [((jnp.where(causal, _dot(sel[k * tq:(k + 1) * tq], e_ref[j]), 0.0) - 1.0) * -NEG).astype(BF16)
                for k in range(NSA_KV_HEADS)]

    def win_bias(j):
        rel = qpos - (j * tq + kiota)
        return [jnp.where(jnp.where(rel >= 0, rel, WINDOW) < WINDOW, 0.0, NEG).astype(BF16)] * NSA_KV_HEADS

    gq = [jnp.concatenate([qr_ref[:, h * LANE:(h + 1) * LANE] for h in range(g * NSA_GROUP, (g + 1) * NSA_GROUP)],
                          axis=0) for g in range(NSA_KV_HEADS)]
    sel_step = lambda j, c: _flash_step(gq, ks_ref.at[0], [vs0_ref.at[0], vs1_ref.at[0]], sel_bias(j), j, c, tq, tq)
    win_step = lambda j, c: _flash_step(gq, kw_ref.at[0], [vw0_ref.at[0], vw1_ref.at[0]], win_bias(j), j, c, tq, tq)
    o_sel = _flash_finish(lax.fori_loop(0, i + 1, sel_step, _flash_init(tq)), tq)
    o_win = _flash_finish(lax.fori_loop(jnp.maximum(i - WINDOW // tq, 0), i + 1, win_step, _flash_init(tq)), tq)
    lo = lax.broadcasted_iota(jnp.int32, (tq, LANE), 1) < NSA_HD
    for c in range(NSA_HEADS // 2):
        pair = []
        for h in (2 * c, 2 * c + 1):
            rs = slice(h * tq, (h + 1) * tq)
            gt = g_ref[:, 3 * h:3 * h + 3]
            pair.append(o_cmp[rs] * gt[:, 0:1] + o_sel[h] * gt[:, 1:2] + o_win[h] * gt[:, 2:3])
        even, odd = pair
        if (2 * c) // NSA_GROUP == 0:
            blk = jnp.where(lo, even, pltpu.roll(odd, NSA_HD, axis=1))
        else:
            blk = jnp.where(lo, pltpu.roll(even, NSA_HD, axis=1), odd)
        o_ref[:, c * LANE:(c + 1) * LANE] = blk


def nsa_prompt_attn(qc, qr, kvc, kvb, gates, n_batch, tq):
    n_rows = qc.shape[0]
    t = n_rows // n_batch
    tq = min(tq, t)
    nt = t // tq
    n_cmp = kvc.shape[1]
    n_sel = t // SEL_BLOCK
    assert n_cmp % LANE == 0 and n_sel <= LANE and tq & (tq - 1) == 0 and WINDOW % tq == 0
    c2s = jnp.asarray(_c2s(n_cmp, LANE))
    e = jnp.asarray(_expand(LANE, t).reshape(LANE, nt, tq).transpose(1, 0, 2), dtype=BF16)
    kvb3 = kvb.reshape(n_batch, t, NSA_KV_W)
    rows = lambda w: pl.BlockSpec((tq, w), lambda b, i: (b * nt + i, 0))
    kv = lambda c: pl.BlockSpec((1, t, LANE), lambda b, i: (b, 0, c))
    return pl.pallas_call(
        functools.partial(_nsa_prompt_kernel, tq=tq, n_sel=n_sel), grid=(n_batch, nt),
        in_specs=[rows(NSA_HEADS * LANE), rows(NSA_HEADS * LANE),
                  pl.BlockSpec((1, n_cmp, CMP_W), lambda b, i: (b, 0, 0)),
                  kv(0), kv(1), kv(2), kv(3), kv(4), kv(5), rows(LANE),
                  pl.BlockSpec(c2s.shape, lambda b, i: (0, 0)), pl.BlockSpec(e.shape, lambda b, i: (0, 0, 0))],
        out_specs=rows(NSA_Q_W),
        out_shape=jax.ShapeDtypeStruct((n_rows, NSA_Q_W), F32),
        compiler_params=_cparams(("parallel", "parallel"), 48), name="nsa_prompt_attn",
    )(qc, qr, kvc, kvb3, kvb3, kvb3, kvb3, kvb3, kvb3, gates, c2s, e)


SROWS = NSA_HEADS * 8
HALF_W = 2 * NSA_KV_HEADS * NSA_HD


def _page_copies(cache_ref, pt_ref, targets, sem, b, slot, n_pages, page):
    return [pltpu.make_async_copy(cache_ref.at[pt_ref[b, pg], kind],
                                  buf.at[slot, :, :, pl.ds(pg * page, page)], sem.at[slot])
            for pg in range(n_pages) for kind, buf in targets]


def _stream_pages(cache_ref, pt_ref, targets, sem, n_pages, page):
    b = pl.program_id(0)
    nb = pl.num_programs(0)
    slot = lax.rem(b, 2)

    @pl.when(b == 0)
    def _():
        for c in _page_copies(cache_ref, pt_ref, targets, sem, 0, 0, n_pages, page):
            c.start()

    @pl.when(b + 1 < nb)
    def _():
        for c in _page_copies(cache_ref, pt_ref, targets, sem, b + 1, 1 - slot, n_pages, page):
            c.start()

    for c in _page_copies(cache_ref, pt_ref, targets, sem, b, slot, n_pages, page):
        c.wait()
    return slot


def _nsa_s1_kernel(pt_ref, cache_ref, q_ref, w_ref, c_ref, c2s_ref, ocmp_ref, sel_ref, bufk, bufv, xk, xv, sem,
                   *, n_pages, page, ts, n_sel):
    past = n_pages * page
    slot = _stream_pages(cache_ref, pt_ref, [(0, bufk), (1, bufv)], sem, n_pages, page)
    n_chunk = past // CMP_STRIDE
    tch = min(past, 8 * LANE)
    eye = jnp.where(lax.broadcasted_iota(jnp.int32, (LANE, LANE), 0) == lax.broadcasted_iota(jnp.int32, (LANE, LANE), 1),
                    1.0, 0.0).astype(BF16)
    for src, dst in ((bufk, xk), (bufv, xv)):
        for c in range(past // tch):
            xt = src[slot, :, :, c * tch:(c + 1) * tch].reshape(LANE, tch).astype(BF16)
            dst[c * tch:(c + 1) * tch, :] = lax.dot_general(xt, eye, (((0,), (0,)), ((), ())),
                                                            preferred_element_type=F32)
    kv = _compress(lambda j: xk[pl.ds(j, n_chunk, stride=CMP_STRIDE), :],
                   lambda j: xv[pl.ds(j, n_chunk, stride=CMP_STRIDE), :], w_ref, c_ref, n_chunk)
    kc = kv[:, :LANE].astype(BF16)
    vc = kv[:, LANE:].astype(BF16)
    tok = lax.broadcasted_iota(jnp.int32, (SROWS, 1), 0) & (ts - 1)
    qpos = past + tok
    cend = lax.broadcasted_iota(jnp.int32, (1, n_chunk), 1) * CMP_STRIDE + (CMP_BLOCK - 1)
    p = _cmp_probs(_dot_t(q_ref[0], kc), cend <= qpos)
    ocmp_ref[0] = _dot(p.astype(BF16), vc)
    psum = []
    for k in range(NSA_KV_HEADS):
        acc = p[k * NSA_GROUP * ts:k * NSA_GROUP * ts + ts]
        for g in range(1, NSA_GROUP):
            r = (k * NSA_GROUP + g) * ts
            acc = acc + p[r:r + ts]
        psum.append(acc)
    psum = jnp.concatenate(psum, axis=0)
    imp = jnp.dot(psum, c2s_ref[...], preferred_element_type=F32, precision=lax.Precision.HIGHEST)
    q_blk = lax.shift_right_logical(past + (lax.broadcasted_iota(jnp.int32, (NSA_KV_HEADS * ts, 1), 0) & (ts - 1)), 6)
    sel_ref[0] = _select_blocks(imp, q_blk, n_sel)


def _nsa_s2_kernel(pt_ref, cache_ref, q_ref, sel_ref, new_ref, win_ref, wnew_ref, ocmp_ref, g_ref, e_ref,
                   o_ref, bufk, bufv, sem, *, n_pages, page, ts, wb):
    past = n_pages * page
    slot = _stream_pages(cache_ref, pt_ref, [(2, bufk), (3, bufv)], sem, n_pages, page)
    q = q_ref[0]
    tok = lax.broadcasted_iota(jnp.int32, (SROWS, 1), 0) & (ts - 1)
    sel = sel_ref[0]
    sel_rows = jnp.concatenate([sel[k * ts:(k + 1) * ts] for k in range(NSA_KV_HEADS) for _ in range(NSA_GROUP)],
                               axis=0)
    n_new = new_ref.shape[1]
    jnew = lax.broadcasted_iota(jnp.int32, (1, n_new), 1)

    def attend(kt_past, vt_past, valid_past, k_new, v_new, valid_new):
        s_past = jnp.where(valid_past, _dot(q, kt_past), NEG)
        s_new = jnp.where(valid_new, _dot_t(q, k_new), NEG)
        m = jnp.maximum(jnp.max(s_past, axis=-1, keepdims=True), jnp.max(s_new, axis=-1, keepdims=True))
        p_past = jnp.exp2(s_past - m)
        p_new = jnp.exp2(s_new - m)
        l = jnp.sum(p_past, axis=-1, keepdims=True) + jnp.sum(p_new, axis=-1, keepdims=True)
        return (_dot_t(p_past.astype(BF16), vt_past) + _dot(p_new.astype(BF16), v_new)) / l

    mask_past = _dot(sel_rows.astype(BF16), e_ref[...]) > 0.5
    blk_new = past // SEL_BLOCK
    valid_new = (sel_rows[:, blk_new:blk_new + 1] > 0.5) & (jnew <= tok)
    o_sel = attend(bufk[slot].reshape(LANE, past).astype(BF16), bufv[slot].reshape(LANE, past).astype(BF16),
                   mask_past, new_ref[0, :, 0:LANE].astype(BF16), new_ref[0, :, LANE:HALF_W].astype(BF16), valid_new)
    u = lax.broadcasted_iota(jnp.int32, (1, wb), 1)
    rel = wb + tok - u
    valid_w = (rel >= 0) & (rel < WINDOW) & (past - wb + u >= 0)
    o_win = attend(win_ref[0, 0].reshape(LANE, wb).astype(BF16), win_ref[0, 1].reshape(LANE, wb).astype(BF16),
                   valid_w, wnew_ref[0, :, 0:LANE].astype(BF16), wnew_ref[0, :, LANE:HALF_W].astype(BF16),
                   jnew <= tok)
    gt = g_ref[0]
    o = ocmp_ref[0] * gt[:, 0:1] + o_sel * gt[:, 1:2] + o_win * gt[:, 2:3]
    half = SROWS // NSA_KV_HEADS
    for k in range(NSA_KV_HEADS):
        o_ref[0, k * half:(k + 1) * half, :] = o[k * half:(k + 1) * half, k * NSA_HD:(k + 1) * NSA_HD]


def nsa_sample_attn(page_table, cache, qc2, qr2, new_sel, win, new_win, gates, cmp_w, cmp_c, ts):
    bs, n_pages = page_table.shape
    page = cache.shape[-1]
    past = n_pages * page
    wb = win.shape[-1]
    assert past % SEL_BLOCK == 0 and ts <= CMP_STRIDE and ts & (ts - 1) == 0 and SROWS == NSA_HEADS * ts
    n_chunk = past // CMP_STRIDE
    n_sel = past // SEL_BLOCK + 1
    n_sel_pad = -(-n_sel // LANE) * LANE
    c2s = jnp.asarray(_c2s(n_chunk, n_sel_pad))
    e = jnp.asarray(_expand(n_sel_pad, past), dtype=BF16)
    sems = pltpu.SemaphoreType.DMA((2,))
    pages = pltpu.VMEM((2, NSA_KV_HEADS, NSA_HD, past), F32)
    scratch1 = [pages, pages, pltpu.VMEM((past, LANE), F32), pltpu.VMEM((past, LANE), F32), sems]
    scratch2 = [pages, pages, sems]
    seq3 = lambda b, pt: (b, 0, 0)
    o_cmp, sel = pl.pallas_call(
        functools.partial(_nsa_s1_kernel, n_pages=n_pages, page=page, ts=ts, n_sel=n_sel),
        grid_spec=pltpu.PrefetchScalarGridSpec(
            num_scalar_prefetch=1, grid=(bs,),
            in_specs=[pl.BlockSpec(memory_space=pl.ANY),
                      pl.BlockSpec((1, SROWS, LANE), seq3),
                      pl.BlockSpec(cmp_w.shape, lambda b, pt: (0, 0, 0, 0)),
                      pl.BlockSpec(cmp_c.shape, lambda b, pt: (0, 0)),
                      pl.BlockSpec(c2s.shape, lambda b, pt: (0, 0))],
            out_specs=[pl.BlockSpec((1, SROWS, LANE), seq3),
                       pl.BlockSpec((1, NSA_KV_HEADS * ts, n_sel_pad), seq3)],
            scratch_shapes=scratch1),
        out_shape=[jax.ShapeDtypeStruct((bs, SROWS, LANE), F32),
                   jax.ShapeDtypeStruct((bs, NSA_KV_HEADS * ts, n_sel_pad), F32)],
        compiler_params=_cparams(("arbitrary",), 48), name="nsa_sample_cmp",
    )(page_table, cache, qc2, cmp_w, cmp_c, c2s)
    n_new = new_sel.shape[1]
    return pl.pallas_call(
        functools.partial(_nsa_s2_kernel, n_pages=n_pages, page=page, ts=ts, wb=wb),
        grid_spec=pltpu.PrefetchScalarGridSpec(
            num_scalar_prefetch=1, grid=(bs,),
            in_specs=[pl.BlockSpec(memory_space=pl.ANY),
                      pl.BlockSpec((1, SROWS, LANE), seq3),
                      pl.BlockSpec((1, NSA_KV_HEADS * ts, n_sel_pad), seq3),
                      pl.BlockSpec((1, n_new, HALF_W), seq3),
                      pl.BlockSpec((1, 2, NSA_KV_HEADS, NSA_HD, wb), lambda b, pt: (b, 0, 0, 0, 0)),
                      pl.BlockSpec((1, n_new, HALF_W), seq3),
                      pl.BlockSpec((1, SROWS, LANE), seq3),
                      pl.BlockSpec((1, SROWS, 3), seq3),
                      pl.BlockSpec(e.shape, lambda b, pt: (0, 0))],
            out_specs=pl.BlockSpec((1, SROWS, NSA_HD), seq3),
            scratch_shapes=scratch2),
        out_shape=jax.ShapeDtypeStruct((bs, SROWS, NSA_HD), F32),
        compiler_params=_cparams(("arbitrary",), 56), name="nsa_sample_sel",
    )(page_table, cache, qr2, sel, new_sel, win, new_win, o_cmp, gates, e)


def _in_proj_kernel(x_ref, g_ref, wq_ref, wkv_ref, wg_ref, wqkv_ref, wz_ref, wba_ref, cos_ref, sin_ref, qn_ref,
                    kn_ref, *refs, n_nsa):
    nsa_refs, (qkv_ref, z_ref, ba_ref), (q_scr, kv_scr, gate_scr) = refs[:n_nsa], refs[n_nsa:n_nsa + 3], refs[n_nsa + 3:]
    x = x_ref[...]
    xn = (x * lax.rsqrt(jnp.mean(x * x, axis=-1, keepdims=True) + EPS) * g_ref[...]).astype(BF16)
    for w_ref, o_ref in ((wq_ref, q_scr), (wkv_ref, kv_scr), (wg_ref, gate_scr), (wqkv_ref, qkv_ref),
                         (wz_ref, z_ref), (wba_ref, ba_ref)):
        n = w_ref.shape[1]
        ch = _col_chunk(n)
        for c in range(0, n, ch):
            o_ref[:, c:c + ch] = _dot(xn, w_ref[:, c:c + ch])
    _nsa_prep_body(q_scr, kv_scr, gate_scr, cos_ref, sin_ref, qn_ref, kn_ref, *nsa_refs)


def _in_proj(x2d, pos_rows, p, tm, dims_major):
    n_rows, d = x2d.shape
    tm = min(tm, n_rows)
    cos_t, sin_t = _rope_tables(pos_rows)
    n_tab = cos_t.shape[0] // tm
    ws = p["w_in_parts"]
    rows = lambda w: pl.BlockSpec((tm, w), lambda i: (i, 0))
    const = lambda shape: pl.BlockSpec(shape, lambda i: (0, 0))
    tab = pl.BlockSpec((tm, LANE), lambda i: (i % n_tab, 0))
    widths = (NSA_HEADS * LANE, NSA_HEADS * LANE, LANE)
    nsa_specs = [rows(w) for w in widths]
    nsa_shape = [jax.ShapeDtypeStruct((n_rows, w), dt) for w, dt in zip(widths, (BF16, BF16, F32))]
    if dims_major:
        n_seq, t = n_rows // cos_t.shape[0], cos_t.shape[0]
        kv_spec = lambda n: pl.BlockSpec((1, n, NSA_KV_HEADS, NSA_HD, tm), lambda i: (i // n_tab, 0, 0, 0, i % n_tab))
        kv_shape = lambda n: jax.ShapeDtypeStruct((n_seq, n, NSA_KV_HEADS, NSA_HD, t), F32)
        nsa_specs = [kv_spec(4), kv_spec(2)] + nsa_specs + [rows(NSA_KV_W), rows(2 * LANE)]
        nsa_shape = [kv_shape(4), kv_shape(2)] + nsa_shape + [jax.ShapeDtypeStruct((n_rows, NSA_KV_W), BF16),
                                                              jax.ShapeDtypeStruct((n_rows, 2 * LANE), F32)]
    else:
        nsa_specs = [rows(NSA_Q_W), rows(2 * LANE)] + nsa_specs
        nsa_shape = [jax.ShapeDtypeStruct((n_rows, NSA_Q_W), F32),
                     jax.ShapeDtypeStruct((n_rows, 2 * LANE), F32)] + nsa_shape
    gdn_w = [w.shape[1] for w in ws[3:]]
    outs = pl.pallas_call(
        functools.partial(_in_proj_kernel, n_nsa=len(nsa_specs)), grid=(n_rows // tm,),
        in_specs=[rows(d), const((1, d))]
        + [pl.BlockSpec(w.shape, lambda i: (0, 0), pipeline_mode=pl.Buffered(1)) for w in ws]
        + [tab, tab, const((1, LANE)), const((3, LANE))],
        out_specs=nsa_specs + [rows(w) for w in gdn_w],
        out_shape=nsa_shape + [jax.ShapeDtypeStruct((n_rows, w), F32) for w in gdn_w],
        scratch_shapes=[pltpu.VMEM((tm, w.shape[1]), F32) for w in ws[:3]],
        compiler_params=_cparams(("parallel",), 56), name="in_proj",
    )(x2d, p["attn_norm_w"].reshape(1, d), *ws, cos_t, sin_t,
      jnp.tile(p["nsa_q_norm_w"].reshape(1, NSA_HD), (1, 2)), jnp.tile(p["nsa_k_norm_w"], (1, 2)))
    nsa, (qkv_pre, z_pre, ba_pre) = outs[:len(nsa_specs)], outs[len(nsa_specs):]
    return nsa, qkv_pre, z_pre, ba_pre


def _tokens_major(a):
    return a.transpose(0, 4, 1, 2, 3)


def _ffn(h, p, prev, n_seq, shift, tm):
    return conv_ffn(h, p["ffn_norm_w"], p["w_up"], p["ffn_conv_w"], p["ffn_conv_b"], p["w_down"], prev,
                    n_seq, shift, tm)


def _layer_prompt(x, mem, p):
    nb, t, d = x.shape
    x2d = x.reshape(nb * t, d)
    (rows_t, wrows_t, qc, qr, gates, kvb, cmp_rows), qkv_pre, z_pre, ba_pre = _in_proj(x2d, jnp.arange(t), p, 512, True)
    kvc = compress_prompt(cmp_rows.reshape(nb, t, CMP_W), p["cmp_bd"], p["cmp_c"])
    o_nsa = nsa_prompt_attn(qc, qr, kvc, kvb, gates, nb, 256)
    o_gdn, gdn_buf, gdn_s = gdn(qkv_pre, z_pre, ba_pre, jnp.zeros((nb, GDN_CONV - 1, GDN_C3), F32),
                                jnp.zeros((nb, GDN_HEADS, GDN_HD, GDN_HD), F32), p["gdn_conv_w"], p["gdn_a_log"],
                                p["gdn_dt_bias"], p["gdn_norm_w"], nb, 4)
    mem_kv, mem_kv_bf = mem_kv_proj(mem.reshape(-1, d), p["mem_norm_w"], p["w_xk"], p["w_xv"], p["xk_norm_w"])
    h = xattn_fused(x2d, o_nsa, o_gdn, p["w_out"], p["xattn_norm_w"], p["w_xq"], p["xq_norm_w"],
                    mem_kv_bf.reshape(nb, mem.shape[1], 2 * XA_W), p["w_xo"], nb, 512)
    y, ffn_buf = _ffn(h, p, jnp.zeros((nb, FFN_CONV - 1, 2 * D_FF), F32), nb, 1, 512)
    keep = min(WINDOW, t)
    return (y.reshape(nb, t, d), _tokens_major(rows_t), _tokens_major(wrows_t[..., t - keep:]), gdn_s, gdn_buf,
            ffn_buf, mem_kv.reshape(nb, mem.shape[1], 2, XA_HEADS, XA_HD))


def _layer_sample(x, cache_kv, cache_win, state, gdn_cache, ffn_cache, mem_kv, page_table, p):
    bs, ts, d = x.shape
    page = cache_kv.shape[1]
    past = page_table.shape[1] * page
    x2d = x.reshape(bs * ts, d)
    pos_rows = jnp.tile(past + jnp.arange(ts), bs)
    (rows, wrows, qc, qr, gates), qkv_pre, z_pre, ba_pre = _in_proj(x2d, pos_rows, p, bs * ts, False)
    dims_major = lambda a: a.transpose(0, 2, 3, 4, 1)

    def stack_q(a):
        return a.reshape(bs, ts, NSA_HEADS, LANE).transpose(0, 2, 1, 3).reshape(bs, SROWS, LANE)

    def pad_new(a):
        return jnp.pad(a.reshape(bs, ts, HALF_W), ((0, 0), (0, LANE - ts), (0, 0)))

    wb = cache_win.shape[1]
    gates3 = gates[:, :NSA_GATE_W].reshape(bs, ts, NSA_HEADS, 3).transpose(0, 2, 1, 3).reshape(bs, SROWS, 3)
    win_t = dims_major(cache_win)
    o2 = nsa_sample_attn(page_table, dims_major(cache_kv), stack_q(qc), stack_q(qr),
                         pad_new(rows[:, HALF_W:]), win_t, pad_new(wrows), gates3, p["cmp_bd"], p["cmp_c"], ts)
    rows = rows.reshape(bs, ts, 4, NSA_KV_HEADS, NSA_HD)
    wrows = wrows.reshape(bs, ts, 2, NSA_KV_HEADS, NSA_HD)
    o_nsa = o2.reshape(bs, NSA_HEADS, ts, NSA_HD).transpose(0, 2, 1, 3).reshape(bs * ts, NSA_Q_W)
    o_gdn, gdn_buf, gdn_s = gdn(qkv_pre, z_pre, ba_pre, gdn_cache, state, p["gdn_conv_w"], p["gdn_a_log"],
                                p["gdn_dt_bias"], p["gdn_norm_w"], bs, 4)
    h1 = matmul_res([o_nsa, o_gdn], p["w_out"], x2d, 1024)
    (qx,) = rms_matmul(h1, p["xattn_norm_w"], [p["w_xq"]], 512)
    m = mem_kv.shape[1]
    kv_rows = mem_kv.reshape(bs, m, 2, XA_HEADS, XA_HD // LANE, LANE).transpose(0, 1, 2, 4, 3, 5)
    kv_rows = kv_rows.reshape(bs, m * MEM_ROWS, LANE)
    h = matmul_res([xattn_cache(qx, kv_rows, p["xq_norm_w"], bs)], p["w_xo"], h1, 512)
    h_tm = h.reshape(bs, ts, d).transpose(1, 0, 2).reshape(ts * bs, d)
    prev = ffn_cache.transpose(1, 0, 2).reshape(1, (FFN_CONV - 1) * bs, 2 * D_FF)
    y, ffn_buf = _ffn(h_tm, p, prev, 1, bs, ts * bs)
    y = y.reshape(ts, bs, d).transpose(1, 0, 2)
    ffn_buf = ffn_buf.reshape(FFN_CONV - 1, bs, 2 * D_FF).transpose(1, 0, 2)
    wall_t = jnp.concatenate([win_t, dims_major(wrows)], axis=-1)
    keep = min(WINDOW, wb + ts)
    return y, rows, _tokens_major(wall_t[..., wb + ts - keep:]), gdn_s, gdn_buf, ffn_buf


def _prep_params(l, attn_norm_w, w_in, nsa_q_norm_w, nsa_k_norm_w, cmp_pe, cmp_w, gdn_conv_w, gdn_a_log,
                 gdn_dt_bias, gdn_norm_w, w_out, mem_norm_w, w_xk, w_xv, xk_norm_w, xattn_norm_w, w_xq,
                 xq_norm_w, w_xo, ffn_norm_w, w_up, ffn_conv_w, ffn_conv_b, w_down):
    wi = w_in[l]
    cuts = np.cumsum([0, NSA_Q_W, NSA_KV_W, NSA_GATE_W, 3 * GDN_W, GDN_W, 2 * GDN_HEADS])
    parts = []
    for a, b in zip(cuts[:-1], cuts[1:]):
        w = wi[:, a:b]
        padn = -(-(b - a) // LANE) * LANE - (b - a)
        parts.append(jnp.pad(w, ((0, 0), (0, padn))).astype(BF16))
    cmp_bd, cmp_c = _cmp_weights(cmp_pe[l], cmp_w[l])
    bf = lambda w: w[l].astype(BF16)
    return dict(attn_norm_w=attn_norm_w[l], w_in_parts=parts, nsa_q_norm_w=nsa_q_norm_w[l],
                nsa_k_norm_w=nsa_k_norm_w[l], cmp_bd=cmp_bd, cmp_c=cmp_c, gdn_conv_w=gdn_conv_w[l],
                gdn_a_log=gdn_a_log[l], gdn_dt_bias=gdn_dt_bias[l], gdn_norm_w=gdn_norm_w[l], w_out=bf(w_out),
                mem_norm_w=mem_norm_w[l], w_xk=bf(w_xk), w_xv=bf(w_xv), xk_norm_w=xk_norm_w[l],
                xattn_norm_w=xattn_norm_w[l], w_xq=bf(w_xq), xq_norm_w=xq_norm_w[l], w_xo=bf(w_xo),
                ffn_norm_w=ffn_norm_w[l], w_up=bf(w_up), ffn_conv_w=ffn_conv_w[l], ffn_conv_b=ffn_conv_b[l],
                w_down=bf(w_down))


def kernel(x_prompt, x_sample, mem_prompt, cache_nsa_kv, cache_nsa_win, state_gdn, cache_gdn_conv, cache_ffn_conv,
           cache_mem_kv, page_table, attn_norm_w, w_in, nsa_q_norm_w, nsa_k_norm_w, cmp_pe, cmp_w, gdn_conv_w,
           gdn_a_log, gdn_dt_bias, gdn_norm_w, w_out, mem_norm_w, w_xk, w_xv, xk_norm_w, xattn_norm_w, w_xq,
           xq_norm_w, w_xo, ffn_norm_w, w_up, ffn_conv_w, ffn_conv_b, w_down):
    weights = (attn_norm_w, w_in, nsa_q_norm_w, nsa_k_norm_w, cmp_pe, cmp_w, gdn_conv_w, gdn_a_log, gdn_dt_bias,
               gdn_norm_w, w_out, mem_norm_w, w_xk, w_xv, xk_norm_w, xattn_norm_w, w_xq, xq_norm_w, w_xo,
               ffn_norm_w, w_up, ffn_conv_w, ffn_conv_b, w_down)
    depth = cache_nsa_kv.shape[0]
    hp, hs = x_prompt, x_sample
    outs_p, outs_s = [], []
    for l in range(depth):
        p = _prep_params(l, *weights)
        res_p = _layer_prompt(hp, mem_prompt, p)
        hp = res_p[0]
        outs_p.append(res_p[1:])
        res_s = _layer_sample(hs, cache_nsa_kv[l], cache_nsa_win[l], state_gdn[l], cache_gdn_conv[l],
                              cache_ffn_conv[l], cache_mem_kv[l], page_table, p)
        hs = res_s[0]
        outs_s.append(res_s[1:])
    stack = lambda outs, i: jnp.stack([o[i] for o in outs])
    return ((hp, hs) + tuple(stack(outs_p, i) for i in range(6)) + tuple(stack(outs_s, i) for i in range(5)))
```

```python
import functools
import math

import numpy as np
import jax
import jax.numpy as jnp
from jax import lax
from jax.experimental import pallas as pl
from jax.experimental.pallas import tpu as pltpu

F32 = jnp.float32
BF16 = jnp.bfloat16

D_MODEL = 1024
NSA_HEADS = 8
NSA_KV_HEADS = 2
NSA_GROUP = NSA_HEADS // NSA_KV_HEADS
NSA_HD = 64
CMP_BLOCK = 32
CMP_STRIDE = 16
SEL_BLOCK = 64
SEL_TOPN = 8
WINDOW = 512
FORCED_BONUS = 1e4
GDN_HEADS = 4
GDN_HD = 128
GDN_CONV = 4
GDN_CHUNK = 64
GDN_W = GDN_HEADS * GDN_HD
XA_HEADS = 4
XA_HD = 256
D_FF = 2816
FFN_CONV = 3
ROPE_THETA = 10000.0
EPS = 1e-6
NEG = -1e30
LOG2E = 1.4426950408889634

NSA_Q_W = NSA_HEADS * NSA_HD
NSA_KV_W = 3 * 2 * NSA_KV_HEADS * NSA_HD
NSA_GATE_W = 3 * NSA_HEADS
LANE = 128
MIB = 1 << 20


def _cparams(sem, vmem_mib):
    return pltpu.CompilerParams(dimension_semantics=sem, vmem_limit_bytes=vmem_mib * MIB)


def _sigmoid(x):
    return 1.0 / (1.0 + jnp.exp(-x))


def _dot(a, b):
    return jnp.dot(a, b, preferred_element_type=F32)


def _dot_t(a, b):
    return lax.dot_general(a, b, (((1,), (1,)), ((), ())), preferred_element_type=F32)


def _col_chunk(n):
    for c in (512, 256, 128):
        if n % c == 0:
            return c
    return n


def _rms_mm_kernel(x_ref, g_ref, *refs, n_out):
    w_refs, o_refs = refs[:n_out], refs[n_out:]
    x = x_ref[...]
    xn = (x * lax.rsqrt(jnp.mean(x * x, axis=-1, keepdims=True) + EPS) * g_ref[...]).astype(BF16)
    for w_ref, o_ref in zip(w_refs, o_refs):
        n = w_ref.shape[1]
        ch = _col_chunk(n)
        for c in range(0, n, ch):
            o_ref[:, c:c + ch] = _dot(xn, w_ref[:, c:c + ch])


def rms_matmul(x, g, ws, tm):
    n_rows, d = x.shape
    tm = min(tm, n_rows)
    in_specs = [pl.BlockSpec((tm, d), lambda i: (i, 0)), pl.BlockSpec((1, d), lambda i: (0, 0))]
    in_specs += [pl.BlockSpec(w.shape, lambda i: (0, 0), pipeline_mode=pl.Buffered(1)) for w in ws]
    out_specs = [pl.BlockSpec((tm, w.shape[1]), lambda i: (i, 0)) for w in ws]
    out_shape = [jax.ShapeDtypeStruct((n_rows, w.shape[1]), F32) for w in ws]
    return pl.pallas_call(
        functools.partial(_rms_mm_kernel, n_out=len(ws)),
        grid=(n_rows // tm,), in_specs=in_specs, out_specs=out_specs, out_shape=out_shape,
        compiler_params=_cparams(("parallel",), 56), name="rms_matmul",
    )(x, g.reshape(1, d), *ws)


def _mm_res_kernel(*refs, n_in):
    a_refs, (w_ref, r_ref, o_ref) = refs[:n_in], refs[n_in:]
    a = [a_ref[...].astype(BF16) for a_ref in a_refs]
    n = w_ref.shape[1]
    ch = _col_chunk(n)
    for c in range(0, n, ch):
        acc = r_ref[:, c:c + ch]
        k0 = 0
        for x in a:
            acc = acc + _dot(x, w_ref[k0:k0 + x.shape[1], c:c + ch])
            k0 += x.shape[1]
        o_ref[:, c:c + ch] = acc


def matmul_res(a_list, w, res, tm):
    n_rows = res.shape[0]
    k, n = w.shape
    assert sum(a.shape[1] for a in a_list) == k
    tm = min(tm, n_rows)
    return pl.pallas_call(
        functools.partial(_mm_res_kernel, n_in=len(a_list)), grid=(n_rows // tm,),
        in_specs=[pl.BlockSpec((tm, a.shape[1]), lambda i: (i, 0)) for a in a_list]
        + [pl.BlockSpec((k, n), lambda i: (0, 0), pipeline_mode=pl.Buffered(1)),
           pl.BlockSpec((tm, n), lambda i: (i, 0))],
        out_specs=pl.BlockSpec((tm, n), lambda i: (i, 0)),
        out_shape=jax.ShapeDtypeStruct((n_rows, n), F32),
        compiler_params=_cparams(("parallel",), 40), name="matmul_res",
    )(*a_list, w, res)


XA_W = XA_HEADS * XA_HD
MEM_ROWS = 2 * XA_W // LANE


def _memkv_kernel(x_ref, g_ref, wk_ref, wv_ref, kn_ref, o_ref, ob_ref):
    x = x_ref[...]
    xn = (x * lax.rsqrt(jnp.mean(x * x, axis=-1, keepdims=True) + EPS) * g_ref[...]).astype(BF16)
    for h in range(XA_HEADS):
        sl = slice(h * XA_HD, (h + 1) * XA_HD)
        vl = slice(XA_W + h * XA_HD, XA_W + (h + 1) * XA_HD)
        k = _dot(xn, wk_ref[:, sl])
        k = k * lax.rsqrt(jnp.mean(k * k, axis=-1, keepdims=True) + EPS) * kn_ref[...]
        v = _dot(xn, wv_ref[:, sl])
        o_ref[:, sl] = k
        o_ref[:, vl] = v
        ob_ref[:, sl] = k.astype(BF16)
        ob_ref[:, vl] = v.astype(BF16)


def mem_kv_proj(mem, g, wk, wv, kn, tm=256):
    n_rows, d = mem.shape
    out = pl.BlockSpec((tm, 2 * XA_W), lambda i: (i, 0))
    return pl.pallas_call(
        _memkv_kernel, grid=(n_rows // tm,),
        in_specs=[pl.BlockSpec((tm, d), lambda i: (i, 0)), pl.BlockSpec((1, d), lambda i: (0, 0)),
                  pl.BlockSpec((d, XA_W), lambda i: (0, 0)), pl.BlockSpec((d, XA_W), lambda i: (0, 0)),
                  pl.BlockSpec((1, XA_HD), lambda i: (0, 0))],
        out_specs=[out, out],
        out_shape=[jax.ShapeDtypeStruct((n_rows, 2 * XA_W), F32), jax.ShapeDtypeStruct((n_rows, 2 * XA_W), BF16)],
        compiler_params=_cparams(("parallel",), 40), name="mem_kv_proj",
    )(mem, g.reshape(1, d), wk, wv, kn.reshape(1, XA_HD))


def _xattn_head(q, qn, k, v):
    q = q * lax.rsqrt(jnp.mean(q * q, axis=-1, keepdims=True) + EPS) * qn
    s = _dot_t((q * (XA_HD ** -0.5)).astype(BF16), k)
    p = jnp.exp(s - jnp.max(s, axis=-1, keepdims=True))
    return _dot(p.astype(BF16), v) / jnp.sum(p, axis=-1, keepdims=True)


def _xattn_cache_kernel(q_ref, kv_ref, qn_ref, o_ref, *, m):
    def head_block(kind, h):
        halves = [kv_ref[0, pl.ds(kind * (MEM_ROWS // 2) + half * XA_HEADS + h, m, stride=MEM_ROWS), :]
                  for half in range(XA_HD // LANE)]
        return jnp.concatenate(halves, axis=1).astype(BF16)

    for h in range(XA_HEADS):
        sl = slice(h * XA_HD, (h + 1) * XA_HD)
        o_ref[:, sl] = _xattn_head(q_ref[:, sl], qn_ref[...], head_block(0, h), head_block(1, h))


def xattn_cache(q_pre, kv_rows, qn, n_batch):
    n_rows = q_pre.shape[0]
    tq = n_rows // n_batch
    m = kv_rows.shape[1] // MEM_ROWS
    return pl.pallas_call(
        functools.partial(_xattn_cache_kernel, m=m), grid=(n_batch,),
        in_specs=[pl.BlockSpec((tq, XA_W), lambda b: (b, 0)),
                  pl.BlockSpec((1, m * MEM_ROWS, LANE), lambda b: (b, 0, 0)),
                  pl.BlockSpec((1, XA_HD), lambda b: (0, 0))],
        out_specs=pl.BlockSpec((tq, XA_W), lambda b: (b, 0)),
        out_shape=jax.ShapeDtypeStruct((n_rows, XA_W), F32),
        compiler_params=_cparams(("parallel",), 40), name="xattn_cache",
    )(q_pre, kv_rows, qn.reshape(1, XA_HD))


def _xattn_fused_kernel(x_ref, a1_ref, a2_ref, wmix_ref, g_ref, wq_ref, qn_ref, kv_ref, wo_ref, o_ref):
    k1 = a1_ref.shape[1]
    x = (x_ref[...] + _dot(a1_ref[...].astype(BF16), wmix_ref[:k1, :])
         + _dot(a2_ref[...].astype(BF16), wmix_ref[k1:, :]))
    xn = (x * lax.rsqrt(jnp.mean(x * x, axis=-1, keepdims=True) + EPS) * g_ref[...]).astype(BF16)
    heads = []
    for h in range(XA_HEADS):
        sl = slice(h * XA_HD, (h + 1) * XA_HD)
        o = _xattn_head(_dot(xn, wq_ref[:, sl]), qn_ref[...], kv_ref[0, :, sl],
                        kv_ref[0, :, XA_W + h * XA_HD:XA_W + (h + 1) * XA_HD])
        heads.append(o.astype(BF16))
    a = jnp.concatenate(heads, axis=1)
    ch = _col_chunk(D_MODEL)
    for c in range(0, D_MODEL, ch):
        o_ref[:, c:c + ch] = x[:, c:c + ch] + _dot(a, wo_ref[:, c:c + ch])


def xattn_fused(x, a1, a2, wmix, g, wq, qn, kv_bf, wo, n_batch, tq):
    n_rows, d = x.shape
    nt = n_rows // n_batch // tq
    m = kv_bf.shape[1]
    const = lambda b, i: (0, 0)
    rows = lambda w: pl.BlockSpec((tq, w), lambda b, i: (b * nt + i, 0))
    return pl.pallas_call(
        _xattn_fused_kernel, grid=(n_batch, nt),
        in_specs=[rows(d), rows(a1.shape[1]), rows(a2.shape[1]),
                  pl.BlockSpec(wmix.shape, const, pipeline_mode=pl.Buffered(1)), pl.BlockSpec((1, d), const),
                  pl.BlockSpec((d, XA_W), const, pipeline_mode=pl.Buffered(1)), pl.BlockSpec((1, XA_HD), const),
                  pl.BlockSpec((1, m, 2 * XA_W), lambda b, i: (b, 0, 0)),
                  pl.BlockSpec((XA_W, d), const, pipeline_mode=pl.Buffered(1))],
        out_specs=pl.BlockSpec((tq, d), lambda b, i: (b * nt + i, 0)),
        out_shape=jax.ShapeDtypeStruct((n_rows, d), F32),
        compiler_params=_cparams(("parallel", "parallel"), 40), name="xattn_fused",
    )(x, a1, a2, wmix, g.reshape(1, d), wq, qn.reshape(1, XA_HD), kv_bf, wo)


FFN_ACT_CHUNK = 256


def _ffn_kernel(h_ref, g_ref, wup_ref, cw_ref, cb_ref, wdn_ref, prev_ref, o_ref, buf_ref, xs_scr, act_scr,
                *, tm, shift, base):
    t = pl.program_id(1)
    p0 = base - 2 * shift

    @pl.when(t == 0)
    def _():
        xs_scr[p0:base, :] = prev_ref[0]

    x = h_ref[...]
    xn = (x * lax.rsqrt(jnp.mean(x * x, axis=-1, keepdims=True) + EPS) * g_ref[...]).astype(BF16)
    for j in range(D_FF // FFN_ACT_CHUNK):
        halves = []
        for c0 in (j * FFN_ACT_CHUNK, D_FF + j * FFN_ACT_CHUNK):
            sl = slice(c0, c0 + FFN_ACT_CHUNK)
            xs_scr[base:base + tm, sl] = _dot(xn, wup_ref[:, sl])
            y = (cw_ref[0:1, sl] * xs_scr[p0:p0 + tm, sl]
                 + cw_ref[1:2, sl] * xs_scr[p0 + shift:p0 + shift + tm, sl]
                 + cw_ref[2:3, sl] * xs_scr[base:base + tm, sl])
            halves.append(y + cb_ref[:, sl])
        a, u = halves
        act = (a * _sigmoid(a) * u).astype(BF16)
        act_scr[:, j * FFN_ACT_CHUNK:(j + 1) * FFN_ACT_CHUNK] = act
    ch = _col_chunk(D_MODEL)
    for c in range(0, D_MODEL, ch):
        o_ref[:, c:c + ch] = x[:, c:c + ch] + _dot(act_scr[...], wdn_ref[:, c:c + ch])
    last = xs_scr[base + tm - 2 * shift: base + tm, :]
    buf_ref[0] = last
    xs_scr[p0:base, :] = last


def conv_ffn(h, g, wup, cw, cb, wdn, prev, n_seq, shift, tm):
    n_rows, d = h.shape
    t_rows = n_rows // n_seq
    tm = min(tm, t_rows)
    nt = t_rows // tm
    base = -(-2 * shift // 8) * 8
    kern = functools.partial(_ffn_kernel, tm=tm, shift=shift, base=base)
    const = lambda b, i: (0, 0)
    return pl.pallas_call(
        kern, grid=(n_seq, nt),
        in_specs=[pl.BlockSpec((tm, d), lambda b, i: (b * nt + i, 0)),
                  pl.BlockSpec((1, d), const),
                  pl.BlockSpec((d, 2 * D_FF), const, pipeline_mode=pl.Buffered(1)),
                  pl.BlockSpec((FFN_CONV, 2 * D_FF), const),
                  pl.BlockSpec((1, 2 * D_FF), const),
                  pl.BlockSpec((D_FF, d), const, pipeline_mode=pl.Buffered(1)),
                  pl.BlockSpec((1, 2 * shift, 2 * D_FF), lambda b, i: (b, 0, 0))],
        out_specs=[pl.BlockSpec((tm, d), lambda b, i: (b * nt + i, 0)),
                   pl.BlockSpec((1, 2 * shift, 2 * D_FF), lambda b, i: (b, 0, 0))],
        out_shape=[jax.ShapeDtypeStruct((n_rows, d), F32),
                   jax.ShapeDtypeStruct((n_seq, 2 * shift, 2 * D_FF), F32)],
        scratch_shapes=[pltpu.VMEM((base + tm, 2 * D_FF), F32), pltpu.VMEM((tm, D_FF), BF16)],
        compiler_params=_cparams(("arbitrary", "arbitrary"), 56), name="conv_ffn",
    )(h, g.reshape(1, d), wup, cw, cb.reshape(1, 2 * D_FF), wdn, prev)


GDN_C3 = 3 * GDN_W


def _gdn_masks(chunk):
    n = GDN_HEADS * chunk
    r = np.arange(n)[:, None]
    c = np.arange(n)[None, :]
    same = lambda k: (r // k) == (c // k)
    m = [same(2) & (c < r)]
    k = 4
    while k <= chunk:
        m.append(same(k) & ~same(k // 2) & (c < r))
        k *= 2
    m += [same(chunk) & (c <= r), same(chunk) & (c < r), r == c]
    return np.stack(m).astype(np.float32)


def _gdn_kernel(qkv_ref, z_ref, ba_ref, prev_ref, s0_ref, cw_ref, par_ref, nw_ref, m_ref,
                o_ref, buf_ref, s_ref, xs_scr, s_scr, *, rv, chunk, n_units, sequential):
    t = pl.program_id(1)
    nt = pl.num_programs(1)
    stack = GDN_HEADS * chunk
    n_lvl = m_ref.shape[0] - 3
    causal, strict, eye = m_ref[n_lvl], m_ref[n_lvl + 1], m_ref[n_lvl + 2]
    kc = GDN_CONV - 1
    if sequential:
        base = [8 + u * rv for u in range(n_units)]

        @pl.when(t == 0)
        def _():
            xs_scr[8 - kc:8, :] = prev_ref[0]
            s_scr[...] = s0_ref[...]

        xs_scr[8:8 + n_units * rv, :] = qkv_ref[...]
    else:
        base = [u * (8 + rv) + 8 for u in range(n_units)]
        s_scr[...] = s0_ref[...]
        for u in range(n_units):
            xs_scr[base[u] - kc:base[u], :] = prev_ref[u]
            xs_scr[base[u]:base[u] + rv, :] = qkv_ref[u * rv:(u + 1) * rv, :]
    ba = ba_ref[...]
    beta_all = _sigmoid(ba)
    a_sh = ba + par_ref[1:2, :]
    g_all = -jnp.exp(par_ref[0:1, :]) * (jnp.maximum(a_sh, 0.0) + jnp.log1p(jnp.exp(-jnp.abs(a_sh))))
    pad = chunk - rv

    def padrows(a):
        if pad == 0:
            return a
        return jnp.concatenate([a, jnp.zeros((pad, a.shape[1]), a.dtype)], axis=0)

    qms, kms, vms, betas, gcols = [], [], [], [], []
    for u in range(n_units):
        r0 = u * rv
        y = cw_ref[0:1, :] * xs_scr[base[u] - kc:base[u] - kc + rv, :]
        for i in range(1, GDN_CONV):
            y = y + cw_ref[i:i + 1, :] * xs_scr[base[u] - kc + i:base[u] - kc + i + rv, :]
        y = y * _sigmoid(y)
        qs, ks, vs, bs, gs = [], [], [], [], []
        for h in range(GDN_HEADS):
            q = y[:, h * GDN_HD:(h + 1) * GDN_HD]
            k = y[:, GDN_W + h * GDN_HD:GDN_W + (h + 1) * GDN_HD]
            v = y[:, 2 * GDN_W + h * GDN_HD:2 * GDN_W + (h + 1) * GDN_HD]
            q = q * lax.rsqrt(jnp.sum(q * q, axis=-1, keepdims=True) + EPS) * (GDN_HD ** -0.5)
            k = k * lax.rsqrt(jnp.sum(k * k, axis=-1, keepdims=True) + EPS)
            qs.append(padrows(q)); ks.append(padrows(k)); vs.append(padrows(v))
            bs.append(padrows(beta_all[r0:r0 + rv, h:h + 1]))
            gs.append(padrows(g_all[r0:r0 + rv, GDN_HEADS + h:GDN_HEADS + h + 1]))
        qms.append(jnp.concatenate(qs, axis=0))
        kms.append(jnp.concatenate(ks, axis=0))
        vms.append(jnp.concatenate(vs, axis=0))
        betas.append(jnp.concatenate(bs, axis=0))
        gcols.append(jnp.concatenate(gs, axis=0))
    gmat = jnp.concatenate(gcols + [jnp.zeros((stack, LANE - n_units), F32)], axis=1)
    gc_all = jnp.dot(causal, gmat, preferred_element_type=F32, precision=lax.Precision.HIGHEST)

    a_bfs, qks, rhss, gcs, tinv = [], [], [], [], []
    for u in range(n_units):
        gc = gc_all[:, u:u + 1]
        gc_row = jnp.sum(eye * gc, axis=0, keepdims=True)
        decay = jnp.exp(jnp.where(causal > 0, gc - gc_row, 0.0)) * causal
        kb = kms[u] * betas[u]
        kmb = kms[u].astype(BF16)
        a_mat = _dot_t(kb.astype(BF16), kmb) * decay * strict
        qks.append((_dot_t(qms[u].astype(BF16), kmb) * decay).astype(BF16))
        rhss.append(jnp.concatenate([vms[u] * betas[u], kb * jnp.exp(gc)], axis=1).astype(BF16))
        a_bfs.append(a_mat.astype(BF16))
        gcs.append(gc)
        tinv.append(eye - a_mat * m_ref[0])
    for lvl in range(1, n_lvl):
        tb = [x.astype(BF16) for x in tinv]
        xs = [(_dot(a_bfs[u], tb[u]) * m_ref[lvl]).astype(BF16) for u in range(n_units)]
        tinv = [tinv[u] - _dot(tb[u], xs[u]) for u in range(n_units)]
    sols = [_dot(tinv[u].astype(BF16), rhss[u]) for u in range(n_units)]

    for u in range(n_units):
        r0 = u * rv
        st = 0 if sequential else u
        gc = gcs[u]
        u_all, w_all = sols[u][:, :GDN_HD], sols[u][:, GDN_HD:]
        qd = qms[u] * jnp.exp(gc)
        v_news, o_inter, gls = [], [], []
        for h in range(GDN_HEADS):
            rs = slice(h * chunk, (h + 1) * chunk)
            sb = s_scr[st, h].astype(BF16)
            both = _dot(jnp.concatenate([w_all[rs], qd[rs]], axis=0).astype(BF16), sb)
            v_news.append(u_all[rs] - both[:chunk])
            o_inter.append(both[chunk:])
            gls.append(gc[(h + 1) * chunk - 1:(h + 1) * chunk, :])
        v_stack = jnp.concatenate(v_news, axis=0).astype(BF16)
        o_intra = _dot(qks[u], v_stack)
        for h in range(GDN_HEADS):
            rs = slice(h * chunk, (h + 1) * chunk)
            kd = kms[u][rs] * jnp.exp(gls[h] - gc[rs])
            s_scr[st, h] = s_scr[st, h] * jnp.exp(gls[h]) + _dot(kd.T.astype(BF16), v_stack[rs])
            o = (o_inter[h] + o_intra[rs])[:rv]
            on = o * lax.rsqrt(jnp.mean(o * o, axis=-1, keepdims=True) + EPS) * nw_ref[...]
            zz = z_ref[r0:r0 + rv, h * GDN_HD:(h + 1) * GDN_HD]
            o_ref[r0:r0 + rv, h * GDN_HD:(h + 1) * GDN_HD] = on * (zz * _sigmoid(zz))

    if sequential:
        last = xs_scr[8 + n_units * rv - kc:8 + n_units * rv, :]
        buf_ref[0] = last
        xs_scr[8 - kc:8, :] = last

        @pl.when(t == nt - 1)
        def _():
            s_ref[...] = s_scr[...]
    else:
        for u in range(n_units):
            buf_ref[u] = xs_scr[base[u] + rv - kc:base[u] + rv, :]
        s_ref[...] = s_scr[...]


def gdn(qkv_pre, z_pre, ba_pre, prev, s0, conv_w, a_log, dt_bias, norm_w, n_seq, units):
    n_rows = qkv_pre.shape[0]
    t_rows = n_rows // n_seq
    sequential = t_rows >= GDN_CHUNK
    if sequential:
        rv = chunk = GDN_CHUNK
        units = min(units, t_rows // chunk)
        grid = (n_seq, t_rows // (units * chunk))
        seq_blk, scr_rows = 1, 8 + units * rv
    else:
        rv, chunk = t_rows, 16
        units = math.gcd(units, n_seq)
        assert rv % 8 == 0 and rv <= chunk
        grid = (n_seq // units, 1)
        seq_blk, scr_rows = units, units * (8 + rv)
    tt = units * rv
    nt = grid[1]
    par = jnp.zeros((8, LANE), F32).at[0, GDN_HEADS:2 * GDN_HEADS].set(a_log).at[1, GDN_HEADS:2 * GDN_HEADS].set(dt_bias)
    masks = jnp.asarray(_gdn_masks(chunk))
    const2 = lambda b, i: (0, 0)
    rows = lambda b, i: (b * nt + i, 0)
    return pl.pallas_call(
        functools.partial(_gdn_kernel, rv=rv, chunk=chunk, n_units=units, sequential=sequential), grid=grid,
        in_specs=[pl.BlockSpec((tt, GDN_C3), rows), pl.BlockSpec((tt, GDN_W), rows),
                  pl.BlockSpec((tt, LANE), rows),
                  pl.BlockSpec((seq_blk, GDN_CONV - 1, GDN_C3), lambda b, i: (b, 0, 0)),
                  pl.BlockSpec((seq_blk, GDN_HEADS, GDN_HD, GDN_HD), lambda b, i: (b, 0, 0, 0)),
                  pl.BlockSpec((GDN_CONV, GDN_C3), const2), pl.BlockSpec((8, LANE), const2),
                  pl.BlockSpec((1, GDN_HD), const2),
                  pl.BlockSpec(masks.shape, lambda b, i: (0, 0, 0))],
        out_specs=[pl.BlockSpec((tt, GDN_W), rows),
                   pl.BlockSpec((seq_blk, GDN_CONV - 1, GDN_C3), lambda b, i: (b, 0, 0)),
                   pl.BlockSpec((seq_blk, GDN_HEADS, GDN_HD, GDN_HD), lambda b, i: (b, 0, 0, 0))],
        out_shape=[jax.ShapeDtypeStruct((n_rows, GDN_W), F32),
                   jax.ShapeDtypeStruct((n_seq, GDN_CONV - 1, GDN_C3), F32),
                   jax.ShapeDtypeStruct((n_seq, GDN_HEADS, GDN_HD, GDN_HD), F32)],
        scratch_shapes=[pltpu.VMEM((scr_rows, GDN_C3), F32),
                        pltpu.VMEM((seq_blk, GDN_HEADS, GDN_HD, GDN_HD), F32)],
        compiler_params=_cparams(("arbitrary", "arbitrary"), 48), name="gdn",
    )(qkv_pre, z_pre, ba_pre, prev, s0, conv_w, par, norm_w.reshape(1, GDN_HD), masks)


CMP_W = 2 * NSA_KV_HEADS * NSA_HD


def _cmp_weights(cmp_pe, cmp_w):
    def bd(w):
        z = jnp.zeros_like(w)
        return jnp.concatenate([jnp.concatenate([w, z], 2), jnp.concatenate([z, w], 2)], 1)
    ws, cs = [], []
    for s in range(2):
        lo, hi = bd(cmp_w[s, :CMP_STRIDE]), bd(cmp_w[s, CMP_STRIDE:])
        cat = jnp.concatenate([lo, hi], axis=2)
        ws.append(cat.reshape(CMP_STRIDE // 2, 2 * LANE, 2 * LANE))
        pe2 = jnp.concatenate([cmp_pe[s], cmp_pe[s]], axis=-1)
        const = [jnp.einsum("jd,jde->e", pe2[h * CMP_STRIDE:(h + 1) * CMP_STRIDE], m,
                            precision=lax.Precision.HIGHEST) for h, m in enumerate((lo, hi))]
        cs.append(jnp.concatenate(const))
    return jnp.stack(ws).astype(BF16), jnp.stack(cs)


def _compress(read_k, read_v, w_ref, c_ref, n_chunk):
    outs = []
    for s, read in enumerate((read_k, read_v)):
        acc = jnp.zeros((n_chunk, 2 * LANE), F32)
        for p in range(CMP_STRIDE // 2):
            x = jnp.concatenate([read(2 * p), read(2 * p + 1)], axis=1).astype(BF16)
            acc = acc + _dot(x, w_ref[s, p])
        acc = acc + c_ref[s:s + 1, :]
        outs.append(acc[:, :LANE] + pltpu.roll(acc[:, LANE:], n_chunk - 1, axis=0))
    return jnp.concatenate(outs, axis=1)


def _cmp_kernel(k_ref, v_ref, w_ref, c_ref, o_ref, *, n_chunk):
    o_ref[0] = _compress(lambda j: k_ref[0, pl.ds(j, n_chunk, stride=CMP_STRIDE), :],
                         lambda j: v_ref[0, pl.ds(j, n_chunk, stride=CMP_STRIDE), :], w_ref, c_ref, n_chunk)


def compress_prompt(rows, w, c):
    b, t, _ = rows.shape
    n_chunk = t // CMP_STRIDE
    return pl.pallas_call(
        functools.partial(_cmp_kernel, n_chunk=n_chunk), grid=(b,),
        in_specs=[pl.BlockSpec((1, t, LANE), lambda i: (i, 0, 0)), pl.BlockSpec((1, t, LANE), lambda i: (i, 0, 1)),
                  pl.BlockSpec(w.shape, lambda i: (0, 0, 0, 0)),
                  pl.BlockSpec(c.shape, lambda i: (0, 0))],
        out_specs=pl.BlockSpec((1, n_chunk, CMP_W), lambda i: (i, 0, 0)),
        out_shape=jax.ShapeDtypeStruct((b, n_chunk, CMP_W), F32),
        compiler_params=_cparams(("parallel",), 32), name="nsa_compress",
    )(rows, rows, w, c)


def _c2s(n_cmp_pad, n_sel_pad):
    cs = np.arange(n_cmp_pad)[:, None] * CMP_STRIDE
    ss = np.arange(n_sel_pad)[None, :] * SEL_BLOCK
    return ((cs < ss + SEL_BLOCK) & (cs + CMP_BLOCK > ss)).astype(np.float32)


def _expand(n_sel_pad, n_keys):
    s = np.arange(n_sel_pad)[:, None]
    k = np.arange(n_keys)[None, :]
    return (k // SEL_BLOCK == s).astype(np.float32)


def _cmp_probs(s, valid):
    s = jnp.where(valid, s, NEG)
    p = jnp.where(valid, jnp.exp(s - jnp.max(s, axis=-1, keepdims=True)), 0.0)
    return p / jnp.maximum(jnp.sum(p, axis=-1, keepdims=True), 1e-30)


def _select_blocks(imp, q_blk, n_sel):
    lane = lax.broadcasted_iota(jnp.int32, imp.shape, 1)
    visible = lane <= q_blk
    forced = (lane == 0) | (lane == q_blk) | (lane == q_blk - 1)
    score = jnp.where(visible, imp + jnp.where(forced, FORCED_BONUS, 0.0), NEG)
    rank = jnp.zeros(imp.shape, F32)
    for sp in range(n_sel):
        col = score[:, sp:sp + 1]
        beats = (col > score) | ((col == score) & (lane > sp))
        rank = rank + jnp.where(beats, 1.0, 0.0)
    return jnp.where(visible & (rank < SEL_TOPN), 1.0, 0.0)


def _select_blocks_t(imp_t, q_blk, n_sel):
    blk = lax.broadcasted_iota(jnp.int32, imp_t.shape, 0)
    visible = blk <= q_blk
    forced = jnp.where(blk == 0, 1.0, 0.0) + jnp.where(blk == q_blk, 1.0, 0.0) + jnp.where(blk == q_blk - 1, 1.0, 0.0)
    score = jnp.where(visible, imp_t + jnp.where(forced > 0.0, FORCED_BONUS, 0.0), NEG)
    rank = jnp.zeros(imp_t.shape, F32)
    for sp in range(n_sel):
        row = score[sp:sp + 1, :]
        tie = jnp.where(blk > sp, jnp.where(row == score, 1.0, 0.0), 0.0)
        rank = rank + jnp.where(row > score, 1.0, 0.0) + tie
    return jnp.where(visible, jnp.where(rank < SEL_TOPN, 1.0, 0.0), 0.0)


def _pair_rms(x, w):
    lo = lax.broadcasted_iota(jnp.int32, x.shape, 1) < NSA_HD
    x2 = x * x
    s_lo = jnp.sum(jnp.where(lo, x2, 0.0), axis=-1, keepdims=True)
    s_hi = jnp.sum(jnp.where(lo, 0.0, x2), axis=-1, keepdims=True)
    ms = jnp.where(lo, s_lo, s_hi) * (1.0 / NSA_HD)
    return x * lax.rsqrt(ms + EPS) * w


def _pair_rope(x, cos_t, sin_t):
    lane = lax.broadcasted_iota(jnp.int32, x.shape, 1)
    first = (lane & (NSA_HD - 1)) < NSA_HD // 2
    partner = jnp.where(first, pltpu.roll(x, LANE - NSA_HD // 2, axis=1), pltpu.roll(x, NSA_HD // 2, axis=1))
    return x * cos_t + partner * sin_t


def _rope_tables(pos):
    half = NSA_HD // 2
    inv = jnp.power(ROPE_THETA, -jnp.arange(half, dtype=F32) / half)
    ang = pos.astype(F32)[:, None] * inv[None, :]
    c, s = jnp.cos(ang), jnp.sin(ang)
    return jnp.concatenate([c, c, c, c], axis=-1), jnp.concatenate([-s, s, -s, s], axis=-1)


def _nsa_prep_body(q_ref, kv_ref, g_ref, cos_ref, sin_ref, qn_ref, kn_ref,
                   rows_ref, wrows_ref, qc_ref, qr_ref, gate_ref, kvb_ref=None, cmp_ref=None):
    cos_t, sin_t = cos_ref[...], sin_ref[...]
    lo = lax.broadcasted_iota(jnp.int32, cos_t.shape, 1) < NSA_HD
    for c in range(NSA_HEADS // 2):
        y = _pair_rms(q_ref[:, c * LANE:(c + 1) * LANE], qn_ref[...]) * (NSA_HD ** -0.5)
        in_low_lanes = (2 * c) // NSA_GROUP == 0
        for src, dst in ((y, qc_ref), (_pair_rope(y, cos_t, sin_t) * LOG2E, qr_ref)):
            swapped = pltpu.roll(src, NSA_HD, axis=1)
            if in_low_lanes:
                even, odd = jnp.where(lo, src, 0.0), jnp.where(lo, swapped, 0.0)
            else:
                even, odd = jnp.where(lo, 0.0, swapped), jnp.where(lo, 0.0, src)
            dst[:, (2 * c) * LANE:(2 * c + 1) * LANE] = even.astype(BF16)
            dst[:, (2 * c + 1) * LANE:(2 * c + 2) * LANE] = odd.astype(BF16)
    for br in range(3):
        k = _pair_rms(kv_ref[:, br * 2 * LANE:br * 2 * LANE + LANE], kn_ref[br:br + 1, :])
        if br > 0:
            k = _pair_rope(k, cos_t, sin_t)
        v = kv_ref[:, br * 2 * LANE + LANE:(br + 1) * 2 * LANE]
        if cmp_ref is None:
            dst, off = (rows_ref, br * 2 * LANE) if br < 2 else (wrows_ref, 0)
            dst[:, off:off + LANE] = k
            dst[:, off + LANE:off + 2 * LANE] = v
        else:
            dst, w0 = (rows_ref, 2 * br) if br < 2 else (wrows_ref, 0)
            for w, x in ((w0, k), (w0 + 1, v)):
                xt = x.T
                for h in range(NSA_KV_HEADS):
                    dst[0, w, h] = xt[h * NSA_HD:(h + 1) * NSA_HD]
            if br == 0:
                cmp_ref[:, :LANE] = k
                cmp_ref[:, LANE:] = v
            else:
                kvb_ref[:, (br - 1) * LANE:br * LANE] = k.astype(BF16)
                kvb_ref[:, 2 * br * LANE:(2 * br + 1) * LANE] = jnp.where(lo, v, 1.0).astype(BF16)
                kvb_ref[:, (2 * br + 1) * LANE:(2 * br + 2) * LANE] = jnp.where(lo, 1.0, v).astype(BF16)
    gate_ref[...] = _sigmoid(g_ref[...])


def _flash_init(tq):
    rows = NSA_GROUP * tq
    return tuple((jnp.full((rows, 1), NEG, F32), jnp.zeros((rows, LANE), F32)) for _ in range(NSA_KV_HEADS))


def _flash_step(gq, k_ref, v_refs, bias, j, carry, tq, tk):
    off = pl.multiple_of(j * tk, tk)
    k = k_ref[pl.ds(off, tk), :]
    out = []
    for g, (m, acc) in enumerate(carry):
        s = _dot_t(gq[g], k).astype(BF16).reshape(NSA_GROUP, tq, tk) + bias[g][None]
        s = s.reshape(NSA_GROUP * tq, tk)
        m_new = jnp.maximum(m, jnp.max(s, axis=-1, keepdims=True).astype(F32))
        p = jnp.exp2(s - m_new.astype(BF16))
        acc = jnp.exp2(m - m_new) * acc + _dot(p, v_refs[g][pl.ds(off, tk), :])
        out.append((m_new, acc))
    return tuple(out)


def _flash_finish(carry, tq):
    outs = []
    for g, (_, acc) in enumerate(carry):
        c = NSA_HD if g == 0 else 0
        o = acc / acc[:, c:c + 1]
        outs += [o[i * tq:(i + 1) * tq] for i in range(NSA_GROUP)]
    return outs


def _nsa_prompt_kernel(qc_ref, qr_ref, kvc_ref, ks_ref, kw_ref, vs0_ref, vs1_ref, vw0_ref, vw1_ref, g_ref,
                       c2s_ref, e_ref, o_ref, *, tq, n_sel):
    i = pl.program_id(1)
    t0 = i * tq
    n_rows = NSA_HEADS * tq
    stack = lambda ref: jnp.concatenate([ref[:, h * LANE:(h + 1) * LANE] for h in range(NSA_HEADS)], axis=0)
    qpos = t0 + lax.broadcasted_iota(jnp.int32, (tq, 1), 0)
    n_cmp = kvc_ref.shape[1]
    kc = kvc_ref[0, :, :LANE].astype(BF16)
    vc = kvc_ref[0, :, LANE:].astype(BF16)
    cend = lax.broadcasted_iota(jnp.int32, (1, n_cmp), 1) * CMP_STRIDE + (CMP_BLOCK - 1)
    s = _dot_t(stack(qc_ref), kc).reshape(NSA_KV_HEADS, NSA_GROUP, tq, n_cmp)
    p = _cmp_probs(s, (cend <= qpos)[None, None])
    o_cmp = _dot(p.reshape(n_rows, n_cmp).astype(BF16), vc)
    psum = p[:, 0]
    for g in range(1, NSA_GROUP):
        psum = psum + p[:, g]
    imp = jnp.dot(psum.reshape(NSA_KV_HEADS * tq, n_cmp), c2s_ref[...], preferred_element_type=F32,
                  precision=lax.Precision.HIGHEST)
    n_blk_rows = -(-n_sel // 8) * 8
    col = lax.broadcasted_iota(jnp.int32, (1, NSA_KV_HEADS * tq), 1)
    q_blk = lax.shift_right_logical(t0 + (col & (tq - 1)), 6)
    sel_t = _select_blocks_t(imp.T[:n_blk_rows], q_blk, n_sel)
    sel_t = jnp.concatenate([sel_t, jnp.zeros((LANE - n_blk_rows, NSA_KV_HEADS * tq), F32)], axis=0)
    sel = sel_t.T.astype(BF16)
    kiota = lax.broadcasted_iota(jnp.int32, (1, tq), 1)

    def sel_bias(j):
        causal = (j * tq + kiota) <= qpos
        return [((jnp.where(causal, _dot(sel[k * tq:(k + 1) * tq], e_ref[j]), 0.0) - 1.0) * -NEG).astype(BF16)
                for k in range(NSA_KV_HEADS)]

    def win_bias(j):
        rel = qpos - (j * tq + kiota)
        return [jnp.where(jnp.where(rel >= 0, rel, WINDOW) < WINDOW, 0.0, NEG).astype(BF16)] * NSA_KV_HEADS

    gq = [jnp.concatenate([qr_ref[:, h * LANE:(h + 1) * LANE] for h in range(g * NSA_GROUP, (g + 1) * NSA_GROUP)],
                          axis=0) for g in range(NSA_KV_HEADS)]
    sel_step = lambda j, c: _flash_step(gq, ks_ref.at[0], [vs0_ref.at[0], vs1_ref.at[0]], sel_bias(j), j, c, tq, tq)
    win_step = lambda j, c: _flash_step(gq, kw_ref.at[0], [vw0_ref.at[0], vw1_ref.at[0]], win_bias(j), j, c, tq, tq)
    o_sel = _flash_finish(lax.fori_loop(0, i + 1, sel_step, _flash_init(tq)), tq)
    o_win = _flash_finish(lax.fori_loop(jnp.maximum(i - WINDOW // tq, 0), i + 1, win_step, _flash_init(tq)), tq)
    lo = lax.broadcasted_iota(jnp.int32, (tq, LANE), 1) < NSA_HD
    for c in range(NSA_HEADS // 2):
        pair = []
        for h in (2 * c, 2 * c + 1):
            rs = slice(h * tq, (h + 1) * tq)
            gt = g_ref[:, 3 * h:3 * h + 3]
            pair.append(o_cmp[rs] * gt[:, 0:1] + o_sel[h] * gt[:, 1:2] + o_win[h] * gt[:, 2:3])
        even, odd = pair
        if (2 * c) // NSA_GROUP == 0:
            blk = jnp.where(lo, even, pltpu.roll(odd, NSA_HD, axis=1))
        else:
            blk = jnp.where(lo, pltpu.roll(even, NSA_HD, axis=1), odd)
        o_ref[:, c * LANE:(c + 1) * LANE] = blk


def nsa_prompt_attn(qc, qr, kvc, kvb, gates, n_batch, tq):
    n_rows = qc.shape[0]
    t = n_rows // n_batch
    tq = min(tq, t)
    nt = t // tq
    n_cmp = kvc.shape[1]
    n_sel = t // SEL_BLOCK
    assert n_cmp % LANE == 0 and n_sel <= LANE and tq & (tq - 1) == 0 and WINDOW % tq == 0
    c2s = jnp.asarray(_c2s(n_cmp, LANE))
    e = jnp.asarray(_expand(LANE, t).reshape(LANE, nt, tq).transpose(1, 0, 2), dtype=BF16)
    kvb3 = kvb.reshape(n_batch, t, NSA_KV_W)
    rows = lambda w: pl.BlockSpec((tq, w), lambda b, i: (b * nt + i, 0))
    kv = lambda c: pl.BlockSpec((1, t, LANE), lambda b, i: (b, 0, c))
    return pl.pallas_call(
        functools.partial(_nsa_prompt_kernel, tq=tq, n_sel=n_sel), grid=(n_batch, nt),
        in_specs=[rows(NSA_HEADS * LANE), rows(NSA_HEADS * LANE),
                  pl.BlockSpec((1, n_cmp, CMP_W), lambda b, i: (b, 0, 0)),
                  kv(0), kv(1), kv(2), kv(3), kv(4), kv(5), rows(LANE),
                  pl.BlockSpec(c2s.shape, lambda b, i: (0, 0)), pl.BlockSpec(e.shape, lambda b, i: (0, 0, 0))],
        out_specs=rows(NSA_Q_W),
        out_shape=jax.ShapeDtypeStruct((n_rows, NSA_Q_W), F32),
        compiler_params=_cparams(("parallel", "parallel"), 48), name="nsa_prompt_attn",
    )(qc, qr, kvc, kvb3, kvb3, kvb3, kvb3, kvb3, kvb3, gates, c2s, e)


SROWS = NSA_HEADS * 8
HALF_W = 2 * NSA_KV_HEADS * NSA_HD


def _page_copies(cache_ref, pt_ref, targets, sem, b, slot, n_pages, page):
    return [pltpu.make_async_copy(cache_ref.at[pt_ref[b, pg], kind],
                                  buf.at[slot, :, :, pl.ds(pg * page, page)], sem.at[slot])
            for pg in range(n_pages) for kind, buf in targets]


def _stream_pages(cache_ref, pt_ref, targets, sem, n_pages, page):
    b = pl.program_id(0)
    nb = pl.num_programs(0)
    slot = lax.rem(b, 2)

    @pl.when(b == 0)
    def _():
        for c in _page_copies(cache_ref, pt_ref, targets, sem, 0, 0, n_pages, page):
            c.start()

    @pl.when(b + 1 < nb)
    def _():
        for c in _page_copies(cache_ref, pt_ref, targets, sem, b + 1, 1 - slot, n_pages, page):
            c.start()

    for c in _page_copies(cache_ref, pt_ref, targets, sem, b, slot, n_pages, page):
        c.wait()
    return slot


def _nsa_s1_kernel(pt_ref, cache_ref, q_ref, w_ref, c_ref, c2s_ref, ocmp_ref, sel_ref, bufk, bufv, xk, xv, sem,
                   *, n_pages, page, ts, n_sel):
    past = n_pages * page
    slot = _stream_pages(cache_ref, pt_ref, [(0, bufk), (1, bufv)], sem, n_pages, page)
    n_chunk = past // CMP_STRIDE
    tch = min(past, 8 * LANE)
    eye = jnp.where(lax.broadcasted_iota(jnp.int32, (LANE, LANE), 0) == lax.broadcasted_iota(jnp.int32, (LANE, LANE), 1),
                    1.0, 0.0).astype(BF16)
    for src, dst in ((bufk, xk), (bufv, xv)):
        for c in range(past // tch):
            xt = src[slot, :, :, c * tch:(c + 1) * tch].reshape(LANE, tch).astype(BF16)
            dst[c * tch:(c + 1) * tch, :] = lax.dot_general(xt, eye, (((0,), (0,)), ((), ())),
                                                            preferred_element_type=F32)
    kv = _compress(lambda j: xk[pl.ds(j, n_chunk, stride=CMP_STRIDE), :],
                   lambda j: xv[pl.ds(j, n_chunk, stride=CMP_STRIDE), :], w_ref, c_ref, n_chunk)
    kc = kv[:, :LANE].astype(BF16)
    vc = kv[:, LANE:].astype(BF16)
    tok = lax.broadcasted_iota(jnp.int32, (SROWS, 1), 0) & (ts - 1)
    qpos = past + tok
    cend = lax.broadcasted_iota(jnp.int32, (1, n_chunk), 1) * CMP_STRIDE + (CMP_BLOCK - 1)
    p = _cmp_probs(_dot_t(q_ref[0], kc), cend <= qpos)
    ocmp_ref[0] = _dot(p.astype(BF16), vc)
    psum = []
    for k in range(NSA_KV_HEADS):
        acc = p[k * NSA_GROUP * ts:k * NSA_GROUP * ts + ts]
        for g in range(1, NSA_GROUP):
            r = (k * NSA_GROUP + g) * ts
            acc = acc + p[r:r + ts]
        psum.append(acc)
    psum = jnp.concatenate(psum, axis=0)
    imp = jnp.dot(psum, c2s_ref[...], preferred_element_type=F32, precision=lax.Precision.HIGHEST)
    q_blk = lax.shift_right_logical(past + (lax.broadcasted_iota(jnp.int32, (NSA_KV_HEADS * ts, 1), 0) & (ts - 1)), 6)
    sel_ref[0] = _select_blocks(imp, q_blk, n_sel)


def _nsa_s2_kernel(pt_ref, cache_ref, q_ref, sel_ref, new_ref, win_ref, wnew_ref, ocmp_ref, g_ref, e_ref,
                   o_ref, bufk, bufv, sem, *, n_pages, page, ts, wb):
    past = n_pages * page
    slot = _stream_pages(cache_ref, pt_ref, [(2, bufk), (3, bufv)], sem, n_pages, page)
    q = q_ref[0]
    tok = lax.broadcasted_iota(jnp.int32, (SROWS, 1), 0) & (ts - 1)
    sel = sel_ref[0]
    sel_rows = jnp.concatenate([sel[k * ts:(k + 1) * ts] for k in range(NSA_KV_HEADS) for _ in range(NSA_GROUP)],
                               axis=0)
    n_new = new_ref.shape[1]
    jnew = lax.broadcasted_iota(jnp.int32, (1, n_new), 1)

    def attend(kt_past, vt_past, valid_past, k_new, v_new, valid_new):
        s_past = jnp.where(valid_past, _dot(q, kt_past), NEG)
        s_new = jnp.where(valid_new, _dot_t(q, k_new), NEG)
        m = jnp.maximum(jnp.max(s_past, axis=-1, keepdims=True), jnp.max(s_new, axis=-1, keepdims=True))
        p_past = jnp.exp2(s_past - m)
        p_new = jnp.exp2(s_new - m)
        l = jnp.sum(p_past, axis=-1, keepdims=True) + jnp.sum(p_new, axis=-1, keepdims=True)
        return (_dot_t(p_past.astype(BF16), vt_past) + _dot(p_new.astype(BF16), v_new)) / l

    mask_past = _dot(sel_rows.astype(BF16), e_ref[...]) > 0.5
    blk_new = past // SEL_BLOCK
    valid_new = (sel_rows[:, blk_new:blk_new + 1] > 0.5) & (jnew <= tok)
    o_sel = attend(bufk[slot].reshape(LANE, past).astype(BF16), bufv[slot].reshape(LANE, past).astype(BF16),
                   mask_past, new_ref[0, :, 0:LANE].astype(BF16), new_ref[0, :, LANE:HALF_W].astype(BF16), valid_new)
    u = lax.broadcasted_iota(jnp.int32, (1, wb), 1)
    rel = wb + tok - u
    valid_w = (rel >= 0) & (rel < WINDOW) & (past - wb + u >= 0)
    o_win = attend(win_ref[0, 0].reshape(LANE, wb).astype(BF16), win_ref[0, 1].reshape(LANE, wb).astype(BF16),
                   valid_w, wnew_ref[0, :, 0:LANE].astype(BF16), wnew_ref[0, :, LANE:HALF_W].astype(BF16),
                   jnew <= tok)
    gt = g_ref[0]
    o = ocmp_ref[0] * gt[:, 0:1] + o_sel * gt[:, 1:2] + o_win * gt[:, 2:3]
    half = SROWS // NSA_KV_HEADS
    for k in range(NSA_KV_HEADS):
        o_ref[0, k * half:(k + 1) * half, :] = o[k * half:(k + 1) * half, k * NSA_HD:(k + 1) * NSA_HD]


def nsa_sample_attn(page_table, cache, qc2, qr2, new_sel, win, new_win, gates, cmp_w, cmp_c, ts):
    bs, n_pages = page_table.shape
    page = cache.shape[-1]
    past = n_pages * page
    wb = win.shape[-1]
    assert past % SEL_BLOCK == 0 and ts <= CMP_STRIDE and ts & (ts - 1) == 0 and SROWS == NSA_HEADS * ts
    n_chunk = past // CMP_STRIDE
    n_sel = past // SEL_BLOCK + 1
    n_sel_pad = -(-n_sel // LANE) * LANE
    c2s = jnp.asarray(_c2s(n_chunk, n_sel_pad))
    e = jnp.asarray(_expand(n_sel_pad, past), dtype=BF16)
    sems = pltpu.SemaphoreType.DMA((2,))
    pages = pltpu.VMEM((2, NSA_KV_HEADS, NSA_HD, past), F32)
    scratch1 = [pages, pages, pltpu.VMEM((past, LANE), F32), pltpu.VMEM((past, LANE), F32), sems]
    scratch2 = [pages, pages, sems]
    seq3 = lambda b, pt: (b, 0, 0)
    o_cmp, sel = pl.pallas_call(
        functools.partial(_nsa_s1_kernel, n_pages=n_pages, page=page, ts=ts, n_sel=n_sel),
        grid_spec=pltpu.PrefetchScalarGridSpec(
            num_scalar_prefetch=1, grid=(bs,),
            in_specs=[pl.BlockSpec(memory_space=pl.ANY),
                      pl.BlockSpec((1, SROWS, LANE), seq3),
                      pl.BlockSpec(cmp_w.shape, lambda b, pt: (0, 0, 0, 0)),
                      pl.BlockSpec(cmp_c.shape, lambda b, pt: (0, 0)),
                      pl.BlockSpec(c2s.shape, lambda b, pt: (0, 0))],
            out_specs=[pl.BlockSpec((1, SROWS, LANE), seq3),
                       pl.BlockSpec((1, NSA_KV_HEADS * ts, n_sel_pad), seq3)],
            scratch_shapes=scratch1),
        out_shape=[jax.ShapeDtypeStruct((bs, SROWS, LANE), F32),
                   jax.ShapeDtypeStruct((bs, NSA_KV_HEADS * ts, n_sel_pad), F32)],
        compiler_params=_cparams(("arbitrary",), 48), name="nsa_sample_cmp",
    )(page_table, cache, qc2, cmp_w, cmp_c, c2s)
    n_new = new_sel.shape[1]
    return pl.pallas_call(
        functools.partial(_nsa_s2_kernel, n_pages=n_pages, page=page, ts=ts, wb=wb),
        grid_spec=pltpu.PrefetchScalarGridSpec(
            num_scalar_prefetch=1, grid=(bs,),
            in_specs=[pl.BlockSpec(memory_space=pl.ANY),
                      pl.BlockSpec((1, SROWS, LANE), seq3),
                      pl.BlockSpec((1, NSA_KV_HEADS * ts, n_sel_pad), seq3),
                      pl.BlockSpec((1, n_new, HALF_W), seq3),
                      pl.BlockSpec((1, 2, NSA_KV_HEADS, NSA_HD, wb), lambda b, pt: (b, 0, 0, 0, 0)),
                      pl.BlockSpec((1, n_new, HALF_W), seq3),
                      pl.BlockSpec((1, SROWS, LANE), seq3),
                      pl.BlockSpec((1, SROWS, 3), seq3),
                      pl.BlockSpec(e.shape, lambda b, pt: (0, 0))],
            out_specs=pl.BlockSpec((1, SROWS, NSA_HD), seq3),
            scratch_shapes=scratch2),
        out_shape=jax.ShapeDtypeStruct((bs, SROWS, NSA_HD), F32),
        compiler_params=_cparams(("arbitrary",), 56), name="nsa_sample_sel",
    )(page_table, cache, qr2, sel, new_sel, win, new_win, o_cmp, gates, e)


def _in_proj_kernel(x_ref, g_ref, wq_ref, wkv_ref, wg_ref, wqkv_ref, wz_ref, wba_ref, cos_ref, sin_ref, qn_ref,
                    kn_ref, *refs, n_nsa):
    nsa_refs, (qkv_ref, z_ref, ba_ref), (q_scr, kv_scr, gate_scr) = refs[:n_nsa], refs[n_nsa:n_nsa + 3], refs[n_nsa + 3:]
    x = x_ref[...]
    xn = (x * lax.rsqrt(jnp.mean(x * x, axis=-1, keepdims=True) + EPS) * g_ref[...]).astype(BF16)
    for w_ref, o_ref in ((wq_ref, q_scr), (wkv_ref, kv_scr), (wg_ref, gate_scr), (wqkv_ref, qkv_ref),
                         (wz_ref, z_ref), (wba_ref, ba_ref)):
        n = w_ref.shape[1]
        ch = _col_chunk(n)
        for c in range(0, n, ch):
            o_ref[:, c:c + ch] = _dot(xn, w_ref[:, c:c + ch])
    _nsa_prep_body(q_scr, kv_scr, gate_scr, cos_ref, sin_ref, qn_ref, kn_ref, *nsa_refs)


def _in_proj(x2d, pos_rows, p, tm, dims_major):
    n_rows, d = x2d.shape
    tm = min(tm, n_rows)
    cos_t, sin_t = _rope_tables(pos_rows)
    n_tab = cos_t.shape[0] // tm
    ws = p["w_in_parts"]
    rows = lambda w: pl.BlockSpec((tm, w), lambda i: (i, 0))
    const = lambda shape: pl.BlockSpec(shape, lambda i: (0, 0))
    tab = pl.BlockSpec((tm, LANE), lambda i: (i % n_tab, 0))
    widths = (NSA_HEADS * LANE, NSA_HEADS * LANE, LANE)
    nsa_specs = [rows(w) for w in widths]
    nsa_shape = [jax.ShapeDtypeStruct((n_rows, w), dt) for w, dt in zip(widths, (BF16, BF16, F32))]
    if dims_major:
        n_seq, t = n_rows // cos_t.shape[0], cos_t.shape[0]
        kv_spec = lambda n: pl.BlockSpec((1, n, NSA_KV_HEADS, NSA_HD, tm), lambda i: (i // n_tab, 0, 0, 0, i % n_tab))
        kv_shape = lambda n: jax.ShapeDtypeStruct((n_seq, n, NSA_KV_HEADS, NSA_HD, t), F32)
        nsa_specs = [kv_spec(4), kv_spec(2)] + nsa_specs + [rows(NSA_KV_W), rows(2 * LANE)]
        nsa_shape = [kv_shape(4), kv_shape(2)] + nsa_shape + [jax.ShapeDtypeStruct((n_rows, NSA_KV_W), BF16),
                                                              jax.ShapeDtypeStruct((n_rows, 2 * LANE), F32)]
    else:
        nsa_specs = [rows(NSA_Q_W), rows(2 * LANE)] + nsa_specs
        nsa_shape = [jax.ShapeDtypeStruct((n_rows, NSA_Q_W), F32),
                     jax.ShapeDtypeStruct((n_rows, 2 * LANE), F32)] + nsa_shape
    gdn_w = [w.shape[1] for w in ws[3:]]
    outs = pl.pallas_call(
        functools.partial(_in_proj_kernel, n_nsa=len(nsa_specs)), grid=(n_rows // tm,),
        in_specs=[rows(d), const((1, d))]
        + [pl.BlockSpec(w.shape, lambda i: (0, 0), pipeline_mode=pl.Buffered(1)) for w in ws]
        + [tab, tab, const((1, LANE)), const((3, LANE))],
        out_specs=nsa_specs + [rows(w) for w in gdn_w],
        out_shape=nsa_shape + [jax.ShapeDtypeStruct((n_rows, w), F32) for w in gdn_w],
        scratch_shapes=[pltpu.VMEM((tm, w.shape[1]), F32) for w in ws[:3]],
        compiler_params=_cparams(("parallel",), 56), name="in_proj",
    )(x2d, p["attn_norm_w"].reshape(1, d), *ws, cos_t, sin_t,
      jnp.tile(p["nsa_q_norm_w"].reshape(1, NSA_HD), (1, 2)), jnp.tile(p["nsa_k_norm_w"], (1, 2)))
    nsa, (qkv_pre, z_pre, ba_pre) = outs[:len(nsa_specs)], outs[len(nsa_specs):]
    return nsa, qkv_pre, z_pre, ba_pre


def _tokens_major(a):
    return a.transpose(0, 4, 1, 2, 3)


def _ffn(h, p, prev, n_seq, shift, tm):
    return conv_ffn(h, p["ffn_norm_w"], p["w_up"], p["ffn_conv_w"], p["ffn_conv_b"], p["w_down"], prev,
                    n_seq, shift, tm)


def _layer_prompt(x, mem, p):
    nb, t, d = x.shape
    x2d = x.reshape(nb * t, d)
    (rows_t, wrows_t, qc, qr, gates, kvb, cmp_rows), qkv_pre, z_pre, ba_pre = _in_proj(x2d, jnp.arange(t), p, 512, True)
    kvc = compress_prompt(cmp_rows.reshape(nb, t, CMP_W), p["cmp_bd"], p["cmp_c"])
    o_nsa = nsa_prompt_attn(qc, qr, kvc, kvb, gates, nb, 256)
    o_gdn, gdn_buf, gdn_s = gdn(qkv_pre, z_pre, ba_pre, jnp.zeros((nb, GDN_CONV - 1, GDN_C3), F32),
                                jnp.zeros((nb, GDN_HEADS, GDN_HD, GDN_HD), F32), p["gdn_conv_w"], p["gdn_a_log"],
                                p["gdn_dt_bias"], p["gdn_norm_w"], nb, 4)
    mem_kv, mem_kv_bf = mem_kv_proj(mem.reshape(-1, d), p["mem_norm_w"], p["w_xk"], p["w_xv"], p["xk_norm_w"])
    h = xattn_fused(x2d, o_nsa, o_gdn, p["w_out"], p["xattn_norm_w"], p["w_xq"], p["xq_norm_w"],
                    mem_kv_bf.reshape(nb, mem.shape[1], 2 * XA_W), p["w_xo"], nb, 512)
    y, ffn_buf = _ffn(h, p, jnp.zeros((nb, FFN_CONV - 1, 2 * D_FF), F32), nb, 1, 512)
    keep = min(WINDOW, t)
    return (y.reshape(nb, t, d), _tokens_major(rows_t), _tokens_major(wrows_t[..., t - keep:]), gdn_s, gdn_buf,
            ffn_buf, mem_kv.reshape(nb, mem.shape[1], 2, XA_HEADS, XA_HD))


def _layer_sample(x, cache_kv, cache_win, state, gdn_cache, ffn_cache, mem_kv, page_table, p):
    bs, ts, d = x.shape
    page = cache_kv.shape[1]
    past = page_table.shape[1] * page
    x2d = x.reshape(bs * ts, d)
    pos_rows = jnp.tile(past + jnp.arange(ts), bs)
    (rows, wrows, qc, qr, gates), qkv_pre, z_pre, ba_pre = _in_proj(x2d, pos_rows, p, bs * ts, False)
    dims_major = lambda a: a.transpose(0, 2, 3, 4, 1)

    def stack_q(a):
        return a.reshape(bs, ts, NSA_HEADS, LANE).transpose(0, 2, 1, 3).reshape(bs, SROWS, LANE)

    def pad_new(a):
        return jnp.pad(a.reshape(bs, ts, HALF_W), ((0, 0), (0, LANE - ts), (0, 0)))

    wb = cache_win.shape[1]
    gates3 = gates[:, :NSA_GATE_W].reshape(bs, ts, NSA_HEADS, 3).transpose(0, 2, 1, 3).reshape(bs, SROWS, 3)
    win_t = dims_major(cache_win)
    o2 = nsa_sample_attn(page_table, dims_major(cache_kv), stack_q(qc), stack_q(qr),
                         pad_new(rows[:, HALF_W:]), win_t, pad_new(wrows), gates3, p["cmp_bd"], p["cmp_c"], ts)
    rows = rows.reshape(bs, ts, 4, NSA_KV_HEADS, NSA_HD)
    wrows = wrows.reshape(bs, ts, 2, NSA_KV_HEADS, NSA_HD)
    o_nsa = o2.reshape(bs, NSA_HEADS, ts, NSA_HD).transpose(0, 2, 1, 3).reshape(bs * ts, NSA_Q_W)
    o_gdn, gdn_buf, gdn_s = gdn(qkv_pre, z_pre, ba_pre, gdn_cache, state, p["gdn_conv_w"], p["gdn_a_log"],
                                p["gdn_dt_bias"], p["gdn_norm_w"], bs, 4)
    h1 = matmul_res([o_nsa, o_gdn], p["w_out"], x2d, 1024)
    (qx,) = rms_matmul(h1, p["xattn_norm_w"], [p["w_xq"]], 512)
    m = mem_kv.shape[1]
    kv_rows = mem_kv.reshape(bs, m, 2, XA_HEADS, XA_HD // LANE, LANE).transpose(0, 1, 2, 4, 3, 5)
    kv_rows = kv_rows.reshape(bs, m * MEM_ROWS, LANE)
    h = matmul_res([xattn_cache(qx, kv_rows, p["xq_norm_w"], bs)], p["w_xo"], h1, 512)
    h_tm = h.reshape(bs, ts, d).transpose(1, 0, 2).reshape(ts * bs, d)
    prev = ffn_cache.transpose(1, 0, 2).reshape(1, (FFN_CONV - 1) * bs, 2 * D_FF)
    y, ffn_buf = _ffn(h_tm, p, prev, 1, bs, ts * bs)
    y = y.reshape(ts, bs, d).transpose(1, 0, 2)
    ffn_buf = ffn_buf.reshape(FFN_CONV - 1, bs, 2 * D_FF).transpose(1, 0, 2)
    wall_t = jnp.concatenate([win_t, dims_major(wrows)], axis=-1)
    keep = min(WINDOW, wb + ts)
    return y, rows, _tokens_major(wall_t[..., wb + ts - keep:]), gdn_s, gdn_buf, ffn_buf


def _prep_params(l, attn_norm_w, w_in, nsa_q_norm_w, nsa_k_norm_w, cmp_pe, cmp_w, gdn_conv_w, gdn_a_log,
                 gdn_dt_bias, gdn_norm_w, w_out, mem_norm_w, w_xk, w_xv, xk_norm_w, xattn_norm_w, w_xq,
                 xq_norm_w, w_xo, ffn_norm_w, w_up, ffn_conv_w, ffn_conv_b, w_down):
    wi = w_in[l]
    cuts = np.cumsum([0, NSA_Q_W, NSA_KV_W, NSA_GATE_W, 3 * GDN_W, GDN_W, 2 * GDN_HEADS])
    parts = []
    for a, b in zip(cuts[:-1], cuts[1:]):
        w = wi[:, a:b]
        padn = -(-(b - a) // LANE) * LANE - (b - a)
        parts.append(jnp.pad(w, ((0, 0), (0, padn))).astype(BF16))
    cmp_bd, cmp_c = _cmp_weights(cmp_pe[l], cmp_w[l])
    bf = lambda w: w[l].astype(BF16)
    return dict(attn_norm_w=attn_norm_w[l], w_in_parts=parts, nsa_q_norm_w=nsa_q_norm_w[l],
                nsa_k_norm_w=nsa_k_norm_w[l], cmp_bd=cmp_bd, cmp_c=cmp_c, gdn_conv_w=gdn_conv_w[l],
                gdn_a_log=gdn_a_log[l], gdn_dt_bias=gdn_dt_bias[l], gdn_norm_w=gdn_norm_w[l], w_out=bf(w_out),
                mem_norm_w=mem_norm_w[l], w_xk=bf(w_xk), w_xv=bf(w_xv), xk_norm_w=xk_norm_w[l],
                xattn_norm_w=xattn_norm_w[l], w_xq=bf(w_xq), xq_norm_w=xq_norm_w[l], w_xo=bf(w_xo),
                ffn_norm_w=ffn_norm_w[l], w_up=bf(w_up), ffn_conv_w=ffn_conv_w[l], ffn_conv_b=ffn_conv_b[l],
                w_down=bf(w_down))


def kernel(x_prompt, x_sample, mem_prompt, cache_nsa_kv, cache_nsa_win, state_gdn, cache_gdn_conv, cache_ffn_conv,
           cache_mem_kv, page_table, attn_norm_w, w_in, nsa_q_norm_w, nsa_k_norm_w, cmp_pe, cmp_w, gdn_conv_w,
           gdn_a_log, gdn_dt_bias, gdn_norm_w, w_out, mem_norm_w, w_xk, w_xv, xk_norm_w, xattn_norm_w, w_xq,
           xq_norm_w, w_xo, ffn_norm_w, w_up, ffn_conv_w, ffn_conv_b, w_down):
    weights = (attn_norm_w, w_in, nsa_q_norm_w, nsa_k_norm_w, cmp_pe, cmp_w, gdn_conv_w, gdn_a_log, gdn_dt_bias,
               gdn_norm_w, w_out, mem_norm_w, w_xk, w_xv, xk_norm_w, xattn_norm_w, w_xq, xq_norm_w, w_xo,
               ffn_norm_w, w_up, ffn_conv_w, ffn_conv_b, w_down)
    depth = cache_nsa_kv.shape[0]
    hp, hs = x_prompt, x_sample
    outs_p, outs_s = [], []
    for l in range(depth):
        p = _prep_params(l, *weights)
        res_p = _layer_prompt(hp, mem_prompt, p)
        hp = res_p[0]
        outs_p.append(res_p[1:])
        res_s = _layer_sample(hs, cache_nsa_kv[l], cache_nsa_win[l], state_gdn[l], cache_gdn_conv[l],
                              cache_ffn_conv[l], cache_mem_kv[l], page_table, p)
        hs = res_s[0]
        outs_s.append(res_s[1:])
    stack = lambda outs, i: jnp.stack([o[i] for o in outs])
    return ((hp, hs) + tuple(stack(outs_p, i) for i in range(6)) + tuple(stack(outs_s, i) for i in range(5)))
```

```python
import functools
import math

import numpy as np
import jax
import jax.numpy as jnp
from jax import lax
from jax.experimental import pallas as pl
from jax.experimental.pallas import tpu as pltpu

F32 = jnp.float32
BF16 = jnp.bfloat16

D_MODEL = 1024
NSA_HEADS = 8
NSA_KV_HEADS = 2
NSA_GROUP = NSA_HEADS // NSA_KV_HEADS
NSA_HD = 64
CMP_BLOCK = 32
CMP_STRIDE = 16
SEL_BLOCK = 64
SEL_TOPN = 8
WINDOW = 512
FORCED_BONUS = 1e4
GDN_HEADS = 4
GDN_HD = 128
GDN_CONV = 4
GDN_CHUNK = 64
GDN_W = GDN_HEADS * GDN_HD
XA_HEADS = 4
XA_HD = 256
D_FF = 2816
FFN_CONV = 3
ROPE_THETA = 10000.0
EPS = 1e-6
NEG = -1e30
LOG2E = 1.4426950408889634

NSA_Q_W = NSA_HEADS * NSA_HD
NSA_KV_W = 3 * 2 * NSA_KV_HEADS * NSA_HD
NSA_GATE_W = 3 * NSA_HEADS
LANE = 128
MIB = 1 << 20


def _cparams(sem, vmem_mib):
    return pltpu.CompilerParams(dimension_semantics=sem, vmem_limit_bytes=vmem_mib * MIB)


def _sigmoid(x):
    return 1.0 / (1.0 + jnp.exp(-x))


def _dot(a, b):
    return jnp.dot(a, b, preferred_element_type=F32)


def _dot_t(a, b):
    return lax.dot_general(a, b, (((1,), (1,)), ((), ())), preferred_element_type=F32)


def _col_chunk(n):
    for c in (512, 256, 128):
        if n % c == 0:
            return c
    return n


def _rms_mm_kernel(x_ref, g_ref, *refs, n_out):
    w_refs, o_refs = refs[:n_out], refs[n_out:]
    x = x_ref[...]
    xn = (x * lax.rsqrt(jnp.mean(x * x, axis=-1, keepdims=True) + EPS) * g_ref[...]).astype(BF16)
    for w_ref, o_ref in zip(w_refs, o_refs):
        n = w_ref.shape[1]
        ch = _col_chunk(n)
        for c in range(0, n, ch):
            o_ref[:, c:c + ch] = _dot(xn, w_ref[:, c:c + ch])


def rms_matmul(x, g, ws, tm):
    n_rows, d = x.shape
    tm = min(tm, n_rows)
    in_specs = [pl.BlockSpec((tm, d), lambda i: (i, 0)), pl.BlockSpec((1, d), lambda i: (0, 0))]
    in_specs += [pl.BlockSpec(w.shape, lambda i: (0, 0), pipeline_mode=pl.Buffered(1)) for w in ws]
    out_specs = [pl.BlockSpec((tm, w.shape[1]), lambda i: (i, 0)) for w in ws]
    out_shape = [jax.ShapeDtypeStruct((n_rows, w.shape[1]), F32) for w in ws]
    return pl.pallas_call(
        functools.partial(_rms_mm_kernel, n_out=len(ws)),
        grid=(n_rows // tm,), in_specs=in_specs, out_specs=out_specs, out_shape=out_shape,
        compiler_params=_cparams(("parallel",), 56), name="rms_matmul",
    )(x, g.reshape(1, d), *ws)


def _mm_res_kernel(*refs, n_in):
    a_refs, (w_ref, r_ref, o_ref) = refs[:n_in], refs[n_in:]
    a = [a_ref[...].astype(BF16) for a_ref in a_refs]
    n = w_ref.shape[1]
    ch = _col_chunk(n)
    for c in range(0, n, ch):
        acc = r_ref[:, c:c + ch]
        k0 = 0
        for x in a:
            acc = acc + _dot(x, w_ref[k0:k0 + x.shape[1], c:c + ch])
            k0 += x.shape[1]
        o_ref[:, c:c + ch] = acc


def matmul_res(a_list, w, res, tm):
    n_rows = res.shape[0]
    k, n = w.shape
    assert sum(a.shape[1] for a in a_list) == k
    tm = min(tm, n_rows)
    return pl.pallas_call(
        functools.partial(_mm_res_kernel, n_in=len(a_list)), grid=(n_rows // tm,),
        in_specs=[pl.BlockSpec((tm, a.shape[1]), lambda i: (i, 0)) for a in a_list]
        + [pl.BlockSpec((k, n), lambda i: (0, 0), pipeline_mode=pl.Buffered(1)),
           pl.BlockSpec((tm, n), lambda i: (i, 0))],
        out_specs=pl.BlockSpec((tm, n), lambda i: (i, 0)),
        out_shape=jax.ShapeDtypeStruct((n_rows, n), F32),
        compiler_params=_cparams(("parallel",), 40), name="matmul_res",
    )(*a_list, w, res)


XA_W = XA_HEADS * XA_HD
MEM_ROWS = 2 * XA_W // LANE


def _memkv_kernel(x_ref, g_ref, wk_ref, wv_ref, kn_ref, o_ref, ob_ref):
    x = x_ref[...]
    xn = (x * lax.rsqrt(jnp.mean(x * x, axis=-1, keepdims=True) + EPS) * g_ref[...]).astype(BF16)
    for h in range(XA_HEADS):
        sl = slice(h * XA_HD, (h + 1) * XA_HD)
        vl = slice(XA_W + h * XA_HD, XA_W + (h + 1) * XA_HD)
        k = _dot(xn, wk_ref[:, sl])
        k = k * lax.rsqrt(jnp.mean(k * k, axis=-1, keepdims=True) + EPS) * kn_ref[...]
        v = _dot(xn, wv_ref[:, sl])
        o_ref[:, sl] = k
        o_ref[:, vl] = v
        ob_ref[:, sl] = k.astype(BF16)
        ob_ref[:, vl] = v.astype(BF16)


def mem_kv_proj(mem, g, wk, wv, kn, tm=256):
    n_rows, d = mem.shape
    out = pl.BlockSpec((tm, 2 * XA_W), lambda i: (i, 0))
    return pl.pallas_call(
        _memkv_kernel, grid=(n_rows // tm,),
        in_specs=[pl.BlockSpec((tm, d), lambda i: (i, 0)), pl.BlockSpec((1, d), lambda i: (0, 0)),
                  pl.BlockSpec((d, XA_W), lambda i: (0, 0)), pl.BlockSpec((d, XA_W), lambda i: (0, 0)),
                  pl.BlockSpec((1, XA_HD), lambda i: (0, 0))],
        out_specs=[out, out],
        out_shape=[jax.ShapeDtypeStruct((n_rows, 2 * XA_W), F32), jax.ShapeDtypeStruct((n_rows, 2 * XA_W), BF16)],
        compiler_params=_cparams(("parallel",), 40), name="mem_kv_proj",
    )(mem, g.reshape(1, d), wk, wv, kn.reshape(1, XA_HD))


def _xattn_head(q, qn, k, v):
    q = q * lax.rsqrt(jnp.mean(q * q, axis=-1, keepdims=True) + EPS) * qn
    s = _dot_t((q * (XA_HD ** -0.5)).astype(BF16), k)
    p = jnp.exp(s - jnp.max(s, axis=-1, keepdims=True))
    return _dot(p.astype(BF16), v) / jnp.sum(p, axis=-1, keepdims=True)


def _xattn_cache_kernel(q_ref, kv_ref, qn_ref, o_ref, *, m):
    def head_block(kind, h):
        halves = [kv_ref[0, pl.ds(kind * (MEM_ROWS // 2) + half * XA_HEADS + h, m, stride=MEM_ROWS), :]
                  for half in range(XA_HD // LANE)]
        return jnp.concatenate(halves, axis=1).astype(BF16)

    for h in range(XA_HEADS):
        sl = slice(h * XA_HD, (h + 1) * XA_HD)
        o_ref[:, sl] = _xattn_head(q_ref[:, sl], qn_ref[...], head_block(0, h), head_block(1, h))


def xattn_cache(q_pre, kv_rows, qn, n_batch):
    n_rows = q_pre.shape[0]
    tq = n_rows // n_batch
    m = kv_rows.shape[1] // MEM_ROWS
    return pl.pallas_call(
        functools.partial(_xattn_cache_kernel, m=m), grid=(n_batch,),
        in_specs=[pl.BlockSpec((tq, XA_W), lambda b: (b, 0)),
                  pl.BlockSpec((1, m * MEM_ROWS, LANE), lambda b: (b, 0, 0)),
                  pl.BlockSpec((1, XA_HD), lambda b: (0, 0))],
        out_specs=pl.BlockSpec((tq, XA_W), lambda b: (b, 0)),
        out_shape=jax.ShapeDtypeStruct((n_rows, XA_W), F32),
        compiler_params=_cparams(("parallel",), 40), name="xattn_cache",
    )(q_pre, kv_rows, qn.reshape(1, XA_HD))


def _xattn_fused_kernel(x_ref, a1_ref, a2_ref, wmix_ref, g_ref, wq_ref, qn_ref, kv_ref, wo_ref, o_ref):
    k1 = a1_ref.shape[1]
    x = (x_ref[...] + _dot(a1_ref[...].astype(BF16), wmix_ref[:k1, :])
         + _dot(a2_ref[...].astype(BF16), wmix_ref[k1:, :]))
    xn = (x * lax.rsqrt(jnp.mean(x * x, axis=-1, keepdims=True) + EPS) * g_ref[...]).astype(BF16)
    heads = []
    for h in range(XA_HEADS):
        sl = slice(h * XA_HD, (h + 1) * XA_HD)
        o = _xattn_head(_dot(xn, wq_ref[:, sl]), qn_ref[...], kv_ref[0, :, sl],
                        kv_ref[0, :, XA_W + h * XA_HD:XA_W + (h + 1) * XA_HD])
        heads.append(o.astype(BF16))
    a = jnp.concatenate(heads, axis=1)
    ch = _col_chunk(D_MODEL)
    for c in range(0, D_MODEL, ch):
        o_ref[:, c:c + ch] = x[:, c:c + ch] + _dot(a, wo_ref[:, c:c + ch])


def xattn_fused(x, a1, a2, wmix, g, wq, qn, kv_bf, wo, n_batch, tq):
    n_rows, d = x.shape
    nt = n_rows // n_batch // tq
    m = kv_bf.shape[1]
    const = lambda b, i: (0, 0)
    rows = lambda w: pl.BlockSpec((tq, w), lambda b, i: (b * nt + i, 0))
    return pl.pallas_call(
        _xattn_fused_kernel, grid=(n_batch, nt),
        in_specs=[rows(d), rows(a1.shape[1]), rows(a2.shape[1]),
                  pl.BlockSpec(wmix.shape, const, pipeline_mode=pl.Buffered(1)), pl.BlockSpec((1, d), const),
                  pl.BlockSpec((d, XA_W), const, pipeline_mode=pl.Buffered(1)), pl.BlockSpec((1, XA_HD), const),
                  pl.BlockSpec((1, m, 2 * XA_W), lambda b, i: (b, 0, 0)),
                  pl.BlockSpec((XA_W, d), const, pipeline_mode=pl.Buffered(1))],
        out_specs=pl.BlockSpec((tq, d), lambda b, i: (b * nt + i, 0)),
        out_shape=jax.ShapeDtypeStruct((n_rows, d), F32),
        compiler_params=_cparams(("parallel", "parallel"), 40), name="xattn_fused",
    )(x, a1, a2, wmix, g.reshape(1, d), wq, qn.reshape(1, XA_HD), kv_bf, wo)


FFN_ACT_CHUNK = 256


def _ffn_kernel(h_ref, g_ref, wup_ref, cw_ref, cb_ref, wdn_ref, prev_ref, o_ref, buf_ref, xs_scr, act_scr,
                *, tm, shift, base):
    t = pl.program_id(1)
    p0 = base - 2 * shift

    @pl.when(t == 0)
    def _():
        xs_scr[p0:base, :] = prev_ref[0]

    x = h_ref[...]
    xn = (x * lax.rsqrt(jnp.mean(x * x, axis=-1, keepdims=True) + EPS) * g_ref[...]).astype(BF16)
    for j in range(D_FF // FFN_ACT_CHUNK):
        halves = []
        for c0 in (j * FFN_ACT_CHUNK, D_FF + j * FFN_ACT_CHUNK):
            sl = slice(c0, c0 + FFN_ACT_CHUNK)
            xs_scr[base:base + tm, sl] = _dot(xn, wup_ref[:, sl])
            y = (cw_ref[0:1, sl] * xs_scr[p0:p0 + tm, sl]
                 + cw_ref[1:2, sl] * xs_scr[p0 + shift:p0 + shift + tm, sl]
                 + cw_ref[2:3, sl] * xs_scr[base:base + tm, sl])
            halves.append(y + cb_ref[:, sl])
        a, u = halves
        act = (a * _sigmoid(a) * u).astype(BF16)
        act_scr[:, j * FFN_ACT_CHUNK:(j + 1) * FFN_ACT_CHUNK] = act
    ch = _col_chunk(D_MODEL)
    for c in range(0, D_MODEL, ch):
        o_ref[:, c:c + ch] = x[:, c:c + ch] + _dot(act_scr[...], wdn_ref[:, c:c + ch])
    last = xs_scr[base + tm - 2 * shift: base + tm, :]
    buf_ref[0] = last
    xs_scr[p0:base, :] = last


def conv_ffn(h, g, wup, cw, cb, wdn, prev, n_seq, shift, tm):
    n_rows, d = h.shape
    t_rows = n_rows // n_seq
    tm = min(tm, t_rows)
    nt = t_rows // tm
    base = -(-2 * shift // 8) * 8
    kern = functools.partial(_ffn_kernel, tm=tm, shift=shift, base=base)
    const = lambda b, i: (0, 0)
    return pl.pallas_call(
        kern, grid=(n_seq, nt),
        in_specs=[pl.BlockSpec((tm, d), lambda b, i: (b * nt + i, 0)),
                  pl.BlockSpec((1, d), const),
                  pl.BlockSpec((d, 2 * D_FF), const, pipeline_mode=pl.Buffered(1)),
                  pl.BlockSpec((FFN_CONV, 2 * D_FF), const),
                  pl.BlockSpec((1, 2 * D_FF), const),
                  pl.BlockSpec((D_FF, d), const, pipeline_mode=pl.Buffered(1)),
                  pl.BlockSpec((1, 2 * shift, 2 * D_FF), lambda b, i: (b, 0, 0))],
        out_specs=[pl.BlockSpec((tm, d), lambda b, i: (b * nt + i, 0)),
                   pl.BlockSpec((1, 2 * shift, 2 * D_FF), lambda b, i: (b, 0, 0))],
        out_shape=[jax.ShapeDtypeStruct((n_rows, d), F32),
                   jax.ShapeDtypeStruct((n_seq, 2 * shift, 2 * D_FF), F32)],
        scratch_shapes=[pltpu.VMEM((base + tm, 2 * D_FF), F32), pltpu.VMEM((tm, D_FF), BF16)],
        compiler_params=_cparams(("arbitrary", "arbitrary"), 56), name="conv_ffn",
    )(h, g.reshape(1, d), wup, cw, cb.reshape(1, 2 * D_FF), wdn, prev)


GDN_C3 = 3 * GDN_W


def _gdn_masks(chunk):
    n = GDN_HEADS * chunk
    r = np.arange(n)[:, None]
    c = np.arange(n)[None, :]
    same = lambda k: (r // k) == (c // k)
    m = [same(2) & (c < r)]
    k = 4
    while k <= chunk:
        m.append(same(k) & ~same(k // 2) & (c < r))
        k *= 2
    m += [same(chunk) & (c <= r), same(chunk) & (c < r), r == c]
    return np.stack(m).astype(np.float32)


def _gdn_kernel(qkv_ref, z_ref, ba_ref, prev_ref, s0_ref, cw_ref, par_ref, nw_ref, m_ref,
                o_ref, buf_ref, s_ref, xs_scr, s_scr, *, rv, chunk, n_units, sequential):
    t = pl.program_id(1)
    nt = pl.num_programs(1)
    stack = GDN_HEADS * chunk
    n_lvl = m_ref.shape[0] - 3
    causal, strict, eye = m_ref[n_lvl], m_ref[n_lvl + 1], m_ref[n_lvl + 2]
    kc = GDN_CONV - 1
    if sequential:
        base = [8 + u * rv for u in range(n_units)]

        @pl.when(t == 0)
        def _():
            xs_scr[8 - kc:8, :] = prev_ref[0]
            s_scr[...] = s0_ref[...]

        xs_scr[8:8 + n_units * rv, :] = qkv_ref[...]
    else:
        base = [u * (8 + rv) + 8 for u in range(n_units)]
        s_scr[...] = s0_ref[...]
        for u in range(n_units):
            xs_scr[base[u] - kc:base[u], :] = prev_ref[u]
            xs_scr[base[u]:base[u] + rv, :] = qkv_ref[u * rv:(u + 1) * rv, :]
    ba = ba_ref[...]
    beta_all = _sigmoid(ba)
    a_sh = ba + par_ref[1:2, :]
    g_all = -jnp.exp(par_ref[0:1, :]) * (jnp.maximum(a_sh, 0.0) + jnp.log1p(jnp.exp(-jnp.abs(a_sh))))
    pad = chunk - rv

    def padrows(a):
        if pad == 0:
            return a
        return jnp.concatenate([a, jnp.zeros((pad, a.shape[1]), a.dtype)], axis=0)

    qms, kms, vms, betas, gcols = [], [], [], [], []
    for u in range(n_units):
        r0 = u * rv
        y = cw_ref[0:1, :] * xs_scr[base[u] - kc:base[u] - kc + rv, :]
        for i in range(1, GDN_CONV):
            y = y + cw_ref[i:i + 1, :] * xs_scr[base[u] - kc + i:base[u] - kc + i + rv, :]
        y = y * _sigmoid(y)
        qs, ks, vs, bs, gs = [], [], [], [], []
        for h in range(GDN_HEADS):
            q = y[:, h * GDN_HD:(h + 1) * GDN_HD]
            k = y[:, GDN_W + h * GDN_HD:GDN_W + (h + 1) * GDN_HD]
            v = y[:, 2 * GDN_W + h * GDN_HD:2 * GDN_W + (h + 1) * GDN_HD]
            q = q * lax.rsqrt(jnp.sum(q * q, axis=-1, keepdims=True) + EPS) * (GDN_HD ** -0.5)
            k = k * lax.rsqrt(jnp.sum(k * k, axis=-1, keepdims=True) + EPS)
            qs.append(padrows(q)); ks.append(padrows(k)); vs.append(padrows(v))
            bs.append(padrows(beta_all[r0:r0 + rv, h:h + 1]))
            gs.append(padrows(g_all[r0:r0 + rv, GDN_HEADS + h:GDN_HEADS + h + 1]))
        qms.append(jnp.concatenate(qs, axis=0))
        kms.append(jnp.concatenate(ks, axis=0))
        vms.append(jnp.concatenate(vs, axis=0))
        betas.append(jnp.concatenate(bs, axis=0))
        gcols.append(jnp.concatenate(gs, axis=0))
    gmat = jnp.concatenate(gcols + [jnp.zeros((stack, LANE - n_units), F32)], axis=1)
    gc_all = jnp.dot(causal, gmat, preferred_element_type=F32, precision=lax.Precision.HIGHEST)

    a_bfs, qks, rhss, gcs, tinv = [], [], [], [], []
    for u in range(n_units):
        gc = gc_all[:, u:u + 1]
        gc_row = jnp.sum(eye * gc, axis=0, keepdims=True)
        decay = jnp.exp(jnp.where(causal > 0, gc - gc_row, 0.0)) * causal
        kb = kms[u] * betas[u]
        kmb = kms[u].astype(BF16)
        a_mat = _dot_t(kb.astype(BF16), kmb) * decay * strict
        qks.append((_dot_t(qms[u].astype(BF16), kmb) * decay).astype(BF16))
        rhss.append(jnp.concatenate([vms[u] * betas[u], kb * jnp.exp(gc)], axis=1).astype(BF16))
        a_bfs.append(a_mat.astype(BF16))
        gcs.append(gc)
        tinv.append(eye - a_mat * m_ref[0])
    for lvl in range(1, n_lvl):
        tb = [x.astype(BF16) for x in tinv]
        xs = [(_dot(a_bfs[u], tb[u]) * m_ref[lvl]).astype(BF16) for u in range(n_units)]
        tinv = [tinv[u] - _dot(tb[u], xs[u]) for u in range(n_units)]
    sols = [_dot(tinv[u].astype(BF16), rhss[u]) for u in range(n_units)]

    for u in range(n_units):
        r0 = u * rv
        st = 0 if sequential else u
        gc = gcs[u]
        u_all, w_all = sols[u][:, :GDN_HD], sols[u][:, GDN_HD:]
        qd = qms[u] * jnp.exp(gc)
        v_news, o_inter, gls = [], [], []
        for h in range(GDN_HEADS):
            rs = slice(h * chunk, (h + 1) * chunk)
            sb = s_scr[st, h].astype(BF16)
            both = _dot(jnp.concatenate([w_all[rs], qd[rs]], axis=0).astype(BF16), sb)
            v_news.append(u_all[rs] - both[:chunk])
            o_inter.append(both[chunk:])
            gls.append(gc[(h + 1) * chunk - 1:(h + 1) * chunk, :])
        v_stack = jnp.concatenate(v_news, axis=0).astype(BF16)
        o_intra = _dot(qks[u], v_stack)
        for h in range(GDN_HEADS):
            rs = slice(h * chunk, (h + 1) * chunk)
            kd = kms[u][rs] * jnp.exp(gls[h] - gc[rs])
            s_scr[st, h] = s_scr[st, h] * jnp.exp(gls[h]) + _dot(kd.T.astype(BF16), v_stack[rs])
            o = (o_inter[h] + o_intra[rs])[:rv]
            on = o * lax.rsqrt(jnp.mean(o * o, axis=-1, keepdims=True) + EPS) * nw_ref[...]
            zz = z_ref[r0:r0 + rv, h * GDN_HD:(h + 1) * GDN_HD]
            o_ref[r0:r0 + rv, h * GDN_HD:(h + 1) * GDN_HD] = on * (zz * _sigmoid(zz))

    if sequential:
        last = xs_scr[8 + n_units * rv - kc:8 + n_units * rv, :]
        buf_ref[0] = last
        xs_scr[8 - kc:8, :] = last

        @pl.when(t == nt - 1)
        def _():
            s_ref[...] = s_scr[...]
    else:
        for u in range(n_units):
            buf_ref[u] = xs_scr[base[u] + rv - kc:base[u] + rv, :]
        s_ref[...] = s_scr[...]


def gdn(qkv_pre, z_pre, ba_pre, prev, s0, conv_w, a_log, dt_bias, norm_w, n_seq, units):
    n_rows = qkv_pre.shape[0]
    t_rows = n_rows // n_seq
    sequential = t_rows >= GDN_CHUNK
    if sequential:
        rv = chunk = GDN_CHUNK
        units = min(units, t_rows // chunk)
        grid = (n_seq, t_rows // (units * chunk))
        seq_blk, scr_rows = 1, 8 + units * rv
    else:
        rv, chunk = t_rows, 16
        units = math.gcd(units, n_seq)
        assert rv % 8 == 0 and rv <= chunk
        grid = (n_seq // units, 1)
        seq_blk, scr_rows = units, units * (8 + rv)
    tt = units * rv
    nt = grid[1]
    par = jnp.zeros((8, LANE), F32).at[0, GDN_HEADS:2 * GDN_HEADS].set(a_log).at[1, GDN_HEADS:2 * GDN_HEADS].set(dt_bias)
    masks = jnp.asarray(_gdn_masks(chunk))
    const2 = lambda b, i: (0, 0)
    rows = lambda b, i: (b * nt + i, 0)
    return pl.pallas_call(
        functools.partial(_gdn_kernel, rv=rv, chunk=chunk, n_units=units, sequential=sequential), grid=grid,
        in_specs=[pl.BlockSpec((tt, GDN_C3), rows), pl.BlockSpec((tt, GDN_W), rows),
                  pl.BlockSpec((tt, LANE), rows),
                  pl.BlockSpec((seq_blk, GDN_CONV - 1, GDN_C3), lambda b, i: (b, 0, 0)),
                  pl.BlockSpec((seq_blk, GDN_HEADS, GDN_HD, GDN_HD), lambda b, i: (b, 0, 0, 0)),
                  pl.BlockSpec((GDN_CONV, GDN_C3), const2), pl.BlockSpec((8, LANE), const2),
                  pl.BlockSpec((1, GDN_HD), const2),
                  pl.BlockSpec(masks.shape, lambda b, i: (0, 0, 0))],
        out_specs=[pl.BlockSpec((tt, GDN_W), rows),
                   pl.BlockSpec((seq_blk, GDN_CONV - 1, GDN_C3), lambda b, i: (b, 0, 0)),
                   pl.BlockSpec((seq_blk, GDN_HEADS, GDN_HD, GDN_HD), lambda b, i: (b, 0, 0, 0))],
        out_shape=[jax.ShapeDtypeStruct((n_rows, GDN_W), F32),
                   jax.ShapeDtypeStruct((n_seq, GDN_CONV - 1, GDN_C3), F32),
                   jax.ShapeDtypeStruct((n_seq, GDN_HEADS, GDN_HD, GDN_HD), F32)],
        scratch_shapes=[pltpu.VMEM((scr_rows, GDN_C3), F32),
                        pltpu.VMEM((seq_blk, GDN_HEADS, GDN_HD, GDN_HD), F32)],
        compiler_params=_cparams(("arbitrary", "arbitrary"), 48), name="gdn",
    )(qkv_pre, z_pre, ba_pre, prev, s0, conv_w, par, norm_w.reshape(1, GDN_HD), masks)


CMP_W = 2 * NSA_KV_HEADS * NSA_HD


def _cmp_weights(cmp_pe, cmp_w):
    def bd(w):
        z = jnp.zeros_like(w)
        return jnp.concatenate([jnp.concatenate([w, z], 2), jnp.concatenate([z, w], 2)], 1)
    ws, cs = [], []
    for s in range(2):
        lo, hi = bd(cmp_w[s, :CMP_STRIDE]), bd(cmp_w[s, CMP_STRIDE:])
        cat = jnp.concatenate([lo, hi], axis=2)
        ws.append(cat.reshape(CMP_STRIDE // 2, 2 * LANE, 2 * LANE))
        pe2 = jnp.concatenate([cmp_pe[s], cmp_pe[s]], axis=-1)
        const = [jnp.einsum("jd,jde->e", pe2[h * CMP_STRIDE:(h + 1) * CMP_STRIDE], m,
                            precision=lax.Precision.HIGHEST) for h, m in enumerate((lo, hi))]
        cs.append(jnp.concatenate(const))
    return jnp.stack(ws).astype(BF16), jnp.stack(cs)


def _compress(read_k, read_v, w_ref, c_ref, n_chunk):
    outs = []
    for s, read in enumerate((read_k, read_v)):
        acc = jnp.zeros((n_chunk, 2 * LANE), F32)
        for p in range(CMP_STRIDE // 2):
            x = jnp.concatenate([read(2 * p), read(2 * p + 1)], axis=1).astype(BF16)
            acc = acc + _dot(x, w_ref[s, p])
        acc = acc + c_ref[s:s + 1, :]
        outs.append(acc[:, :LANE] + pltpu.roll(acc[:, LANE:], n_chunk - 1, axis=0))
    return jnp.concatenate(outs, axis=1)


def _cmp_kernel(k_ref, v_ref, w_ref, c_ref, o_ref, *, n_chunk):
    o_ref[0] = _compress(lambda j: k_ref[0, pl.ds(j, n_chunk, stride=CMP_STRIDE), :],
                         lambda j: v_ref[0, pl.ds(j, n_chunk, stride=CMP_STRIDE), :], w_ref, c_ref, n_chunk)


def compress_prompt(rows, w, c):
    b, t, _ = rows.shape
    n_chunk = t // CMP_STRIDE
    return pl.pallas_call(
        functools.partial(_cmp_kernel, n_chunk=n_chunk), grid=(b,),
        in_specs=[pl.BlockSpec((1, t, LANE), lambda i: (i, 0, 0)), pl.BlockSpec((1, t, LANE), lambda i: (i, 0, 1)),
                  pl.BlockSpec(w.shape, lambda i: (0, 0, 0, 0)),
                  pl.BlockSpec(c.shape, lambda i: (0, 0))],
        out_specs=pl.BlockSpec((1, n_chunk, CMP_W), lambda i: (i, 0, 0)),
        out_shape=jax.ShapeDtypeStruct((b, n_chunk, CMP_W), F32),
        compiler_params=_cparams(("parallel",), 32), name="nsa_compress",
    )(rows, rows, w, c)


def _c2s(n_cmp_pad, n_sel_pad):
    cs = np.arange(n_cmp_pad)[:, None] * CMP_STRIDE
    ss = np.arange(n_sel_pad)[None, :] * SEL_BLOCK
    return ((cs < ss + SEL_BLOCK) & (cs + CMP_BLOCK > ss)).astype(np.float32)


def _expand(n_sel_pad, n_keys):
    s = np.arange(n_sel_pad)[:, None]
    k = np.arange(n_keys)[None, :]
    return (k // SEL_BLOCK == s).astype(np.float32)


def _cmp_probs(s, valid):
    s = jnp.where(valid, s, NEG)
    p = jnp.where(valid, jnp.exp(s - jnp.max(s, axis=-1, keepdims=True)), 0.0)
    return p / jnp.maximum(jnp.sum(p, axis=-1, keepdims=True), 1e-30)


def _select_blocks(imp, q_blk, n_sel):
    lane = lax.broadcasted_iota(jnp.int32, imp.shape, 1)
    visible = lane <= q_blk
    forced = (lane == 0) | (lane == q_blk) | (lane == q_blk - 1)
    score = jnp.where(visible, imp + jnp.where(forced, FORCED_BONUS, 0.0), NEG)
    rank = jnp.zeros(imp.shape, F32)
    for sp in range(n_sel):
        col = score[:, sp:sp + 1]
        beats = (col > score) | ((col == score) & (lane > sp))
        rank = rank + jnp.where(beats, 1.0, 0.0)
    return jnp.where(visible & (rank < SEL_TOPN), 1.0, 0.0)


def _select_blocks_t(imp_t, q_blk, n_sel):
    blk = lax.broadcasted_iota(jnp.int32, imp_t.shape, 0)
    visible = blk <= q_blk
    forced = jnp.where(blk == 0, 1.0, 0.0) + jnp.where(blk == q_blk, 1.0, 0.0) + jnp.where(blk == q_blk - 1, 1.0, 0.0)
    score = jnp.where(visible, imp_t + jnp.where(forced > 0.0, FORCED_BONUS, 0.0), NEG)
    rank = jnp.zeros(imp_t.shape, F32)
    for sp in range(n_sel):
        row = score[sp:sp + 1, :]
        tie = jnp.where(blk > sp, jnp.where(row == score, 1.0, 0.0), 0.0)
        rank = rank + jnp.where(row > score, 1.0, 0.0) + tie
    return jnp.where(visible, jnp.where(rank < SEL_TOPN, 1.0, 0.0), 0.0)


def _pair_rms(x, w):
    lo = lax.broadcasted_iota(jnp.int32, x.shape, 1) < NSA_HD
    x2 = x * x
    s_lo = jnp.sum(jnp.where(lo, x2, 0.0), axis=-1, keepdims=True)
    s_hi = jnp.sum(jnp.where(lo, 0.0, x2), axis=-1, keepdims=True)
    ms = jnp.where(lo, s_lo, s_hi) * (1.0 / NSA_HD)
    return x * lax.rsqrt(ms + EPS) * w


def _pair_rope(x, cos_t, sin_t):
    lane = lax.broadcasted_iota(jnp.int32, x.shape, 1)
    first = (lane & (NSA_HD - 1)) < NSA_HD // 2
    partner = jnp.where(first, pltpu.roll(x, LANE - NSA_HD // 2, axis=1), pltpu.roll(x, NSA_HD // 2, axis=1))
    return x * cos_t + partner * sin_t


def _rope_tables(pos):
    half = NSA_HD // 2
    inv = jnp.power(ROPE_THETA, -jnp.arange(half, dtype=F32) / half)
    ang = pos.astype(F32)[:, None] * inv[None, :]
    c, s = jnp.cos(ang), jnp.sin(ang)
    return jnp.concatenate([c, c, c, c], axis=-1), jnp.concatenate([-s, s, -s, s], axis=-1)


def _nsa_prep_body(q_ref, kv_ref, g_ref, cos_ref, sin_ref, qn_ref, kn_ref,
                   rows_ref, wrows_ref, qc_ref, qr_ref, gate_ref, kvb_ref=None, cmp_ref=None):
    cos_t, sin_t = cos_ref[...], sin_ref[...]
    lo = lax.broadcasted_iota(jnp.int32, cos_t.shape, 1) < NSA_HD
    for c in range(NSA_HEADS // 2):
        y = _pair_rms(q_ref[:, c * LANE:(c + 1) * LANE], qn_ref[...]) * (NSA_HD ** -0.5)
        in_low_lanes = (2 * c) // NSA_GROUP == 0
        for src, dst in ((y, qc_ref), (_pair_rope(y, cos_t, sin_t) * LOG2E, qr_ref)):
            swapped = pltpu.roll(src, NSA_HD, axis=1)
            if in_low_lanes:
                even, odd = jnp.where(lo, src, 0.0), jnp.where(lo, swapped, 0.0)
            else:
                even, odd = jnp.where(lo, 0.0, swapped), jnp.where(lo, 0.0, src)
            dst[:, (2 * c) * LANE:(2 * c + 1) * LANE] = even.astype(BF16)
            dst[:, (2 * c + 1) * LANE:(2 * c + 2) * LANE] = odd.astype(BF16)
    for br in range(3):
        k = _pair_rms(kv_ref[:, br * 2 * LANE:br * 2 * LANE + LANE], kn_ref[br:br + 1, :])
        if br > 0:
            k = _pair_rope(k, cos_t, sin_t)
        v = kv_ref[:, br * 2 * LANE + LANE:(br + 1) * 2 * LANE]
        if cmp_ref is None:
            dst, off = (rows_ref, br * 2 * LANE) if br < 2 else (wrows_ref, 0)
            dst[:, off:off + LANE] = k
            dst[:, off + LANE:off + 2 * LANE] = v
        else:
            dst, w0 = (rows_ref, 2 * br) if br < 2 else (wrows_ref, 0)
            for w, x in ((w0, k), (w0 + 1, v)):
                xt = x.T
                for h in range(NSA_KV_HEADS):
                    dst[0, w, h] = xt[h * NSA_HD:(h + 1) * NSA_HD]
            if br == 0:
                cmp_ref[:, :LANE] = k
                cmp_ref[:, LANE:] = v
            else:
                kvb_ref[:, (br - 1) * LANE:br * LANE] = k.astype(BF16)
                kvb_ref[:, 2 * br * LANE:(2 * br + 1) * LANE] = jnp.where(lo, v, 1.0).astype(BF16)
                kvb_ref[:, (2 * br + 1) * LANE:(2 * br + 2) * LANE] = jnp.where(lo, 1.0, v).astype(BF16)
    gate_ref[...] = _sigmoid(g_ref[...])


def _flash_init(tq):
    rows = NSA_GROUP * tq
    return tuple((jnp.full((rows, 1), NEG, F32), jnp.zeros((rows, LANE), F32)) for _ in range(NSA_KV_HEADS))


def _flash_step(gq, k_ref, v_refs, bias, j, carry, tq, tk):
    off = pl.multiple_of(j * tk, tk)
    k = k_ref[pl.ds(off, tk), :]
    out = []
    for g, (m, acc) in enumerate(carry):
        s = _dot_t(gq[g], k).astype(BF16).reshape(NSA_GROUP, tq, tk) + bias[g][None]
        s = s.reshape(NSA_GROUP * tq, tk)
        m_new = jnp.maximum(m, jnp.max(s, axis=-1, keepdims=True).astype(F32))
        p = jnp.exp2(s - m_new.astype(BF16))
        acc = jnp.exp2(m - m_new) * acc + _dot(p, v_refs[g][pl.ds(off, tk), :])
        out.append((m_new, acc))
    return tuple(out)


def _flash_finish(carry, tq):
    outs = []
    for g, (_, acc) in enumerate(carry):
        c = NSA_HD if g == 0 else 0
        o = acc / acc[:, c:c + 1]
        outs += [o[i * tq:(i + 1) * tq] for i in range(NSA_GROUP)]
    return outs


def _nsa_prompt_kernel(qc_ref, qr_ref, kvc_ref, ks_ref, kw_ref, vs0_ref, vs1_ref, vw0_ref, vw1_ref, g_ref,
                       c2s_ref, e_ref, o_ref, *, tq, n_sel):
    i = pl.program_id(1)
    t0 = i * tq
    n_rows = NSA_HEADS * tq
    stack = lambda ref: jnp.concatenate([ref[:, h * LANE:(h + 1) * LANE] for h in range(NSA_HEADS)], axis=0)
    qpos = t0 + lax.broadcasted_iota(jnp.int32, (tq, 1), 0)
    n_cmp = kvc_ref.shape[1]
    kc = kvc_ref[0, :, :LANE].astype(BF16)
    vc = kvc_ref[0, :, LANE:].astype(BF16)
    cend = lax.broadcasted_iota(jnp.int32, (1, n_cmp), 1) * CMP_STRIDE + (CMP_BLOCK - 1)
    s = _dot_t(stack(qc_ref), kc).reshape(NSA_KV_HEADS, NSA_GROUP, tq, n_cmp)
    p = _cmp_probs(s, (cend <= qpos)[None, None])
    o_cmp = _dot(p.reshape(n_rows, n_cmp).astype(BF16), vc)
    psum = p[:, 0]
    for g in range(1, NSA_GROUP):
        psum = psum + p[:, g]
    imp = jnp.dot(psum.reshape(NSA_KV_HEADS * tq, n_cmp), c2s_ref[...], preferred_element_type=F32,
                  precision=lax.Precision.HIGHEST)
    n_blk_rows = -(-n_sel // 8) * 8
    col = lax.broadcasted_iota(jnp.int32, (1, NSA_KV_HEADS * tq), 1)
    q_blk = lax.shift_right_logical(t0 + (col & (tq - 1)), 6)
    sel_t = _select_blocks_t(imp.T[:n_blk_rows], q_blk, n_sel)
    sel_t = jnp.concatenate([sel_t, jnp.zeros((LANE - n_blk_rows, NSA_KV_HEADS * tq), F32)], axis=0)
    sel = sel_t.T.astype(BF16)
    kiota = lax.broadcasted_iota(jnp.int32, (1, tq), 1)

    def sel_bias(j):
        causal = (j * tq + kiota) <= qpos
        return [((jnp.where(causal, _dot(sel[k * tq:(k + 1) * tq], e_ref[j]), 0.0) - 1.0) * -NEG).astype(BF16)
                for k in range(NSA_KV_HEADS)]

    def win_bias(j):
        rel = qpos - (j * tq + kiota)
        return [jnp.where(jnp.where(rel >= 0, rel, WINDOW) < WINDOW, 0.0, NEG).astype(BF16)] * NSA_KV_HEADS

    gq = [jnp.concatenate([qr_ref[:, h * LANE:(h + 1) * LANE] for h in range(g * NSA_GROUP, (g + 1) * NSA_GROUP)],
                          axis=0) for g in range(NSA_KV_HEADS)]
    sel_step = lambda j, c: _flash_step(gq, ks_ref.at[0], [vs0_ref.at[0], vs1_ref.at[0]], sel_bias(j), j, c, tq, tq)
    win_step = lambda j, c: _flash_step(gq, kw_ref.at[0], [vw0_ref.at[0], vw1_ref.at[0]], win_bias(j), j, c, tq, tq)
    o_sel = _flash_finish(lax.fori_loop(0, i + 1, sel_step, _flash_init(tq)), tq)
    o_win = _flash_finish(lax.fori_loop(jnp.maximum(i - WINDOW // tq, 0), i + 1, win_step, _flash_init(tq)), tq)
    lo = lax.broadcasted_iota(jnp.int32, (tq, LANE), 1) < NSA_HD
    for c in range(NSA_HEADS // 2):
        pair = []
        for h in (2 * c, 2 * c + 1):
            rs = slice(h * tq, (h + 1) * tq)
            gt = g_ref[:, 3 * h:3 * h + 3]
            pair.append(o_cmp[rs] * gt[:, 0:1] + o_sel[h] * gt[:, 1:2] + o_win[h] * gt[:, 2:3])
        even, odd = pair
        if (2 * c) // NSA_GROUP == 0:
            blk = jnp.where(lo, even, pltpu.roll(odd, NSA_HD, axis=1))
        else:
            blk = jnp.where(lo, pltpu.roll(even, NSA_HD, axis=1), odd)
        o_ref[:, c * LANE:(c + 1) * LANE] = blk


def nsa_prompt_attn(qc, qr, kvc, kvb, gates, n_batch, tq):
    n_rows = qc.shape[0]
    t = n_rows // n_batch
    tq = min(tq, t)
    nt = t // tq
    n_cmp = kvc.shape[1]
    n_sel = t // SEL_BLOCK
    assert n_cmp % LANE == 0 and n_sel <= LANE and tq & (tq - 1) == 0 and WINDOW % tq == 0
    c2s = jnp.asarray(_c2s(n_cmp, LANE))
    e = jnp.asarray(_expand(LANE, t).reshape(LANE, nt, tq).transpose(1, 0, 2), dtype=BF16)
    kvb3 = kvb.reshape(n_batch, t, NSA_KV_W)
    rows = lambda w: pl.BlockSpec((tq, w), lambda b, i: (b * nt + i, 0))
    kv = lambda c: pl.BlockSpec((1, t, LANE), lambda b, i: (b, 0, c))
    return pl.pallas_call(
        functools.partial(_nsa_prompt_kernel, tq=tq, n_sel=n_sel), grid=(n_batch, nt),
        in_specs=[rows(NSA_HEADS * LANE), rows(NSA_HEADS * LANE),
                  pl.BlockSpec((1, n_cmp, CMP_W), lambda b, i: (b, 0, 0)),
                  kv(0), kv(1), kv(2), kv(3), kv(4), kv(5), rows(LANE),
                  pl.BlockSpec(c2s.shape, lambda b, i: (0, 0)), pl.BlockSpec(e.shape, lambda b, i: (0, 0, 0))],
        out_specs=rows(NSA_Q_W),
        out_shape=jax.ShapeDtypeStruct((n_rows, NSA_Q_W), F32),
        compiler_params=_cparams(("parallel", "parallel"), 48), name="nsa_prompt_attn",
    )(qc, qr, kvc, kvb3, kvb3, kvb3, kvb3, kvb3, kvb3, gates, c2s, e)


SROWS = NSA_HEADS * 8
CMP_PITCH = CMP_STRIDE + 8
HALF_W = 2 * NSA_KV_HEADS * NSA_HD


def _page_copies(cache_ref, pt_ref, targets, sem, b, slot, n_pages, page):
    return [pltpu.make_async_copy(cache_ref.at[pt_ref[b, pg], kind],
                                  buf.at[slot, :, :, pl.ds(pg * page, page)], sem.at[slot])
            for pg in range(n_pages) for kind, buf in targets]


def _stream_pages(cache_ref, pt_ref, targets, sem, n_pages, page):
    b = pl.program_id(0)
    nb = pl.num_programs(0)
    slot = lax.rem(b, 2)

    @pl.when(b == 0)
    def _():
        for c in _page_copies(cache_ref, pt_ref, targets, sem, 0, 0, n_pages, page):
            c.start()

    @pl.when(b + 1 < nb)
    def _():
        for c in _page_copies(cache_ref, pt_ref, targets, sem, b + 1, 1 - slot, n_pages, page):
            c.start()

    for c in _page_copies(cache_ref, pt_ref, targets, sem, b, slot, n_pages, page):
        c.wait()
    return slot


def _nsa_s1_kernel(pt_ref, cache_ref, q_ref, w_ref, c_ref, c2s_ref, ocmp_ref, sel_ref, bufk, bufv, xk, xv, sem,
                   *, n_pages, page, ts, n_sel):
    past = n_pages * page
    slot = _stream_pages(cache_ref, pt_ref, [(0, bufk), (1, bufv)], sem, n_pages, page)
    n_chunk = past // CMP_STRIDE
    tch = min(past, 8 * LANE)
    eye = jnp.where(lax.broadcasted_iota(jnp.int32, (LANE, LANE), 0) == lax.broadcasted_iota(jnp.int32, (LANE, LANE), 1),
                    1.0, 0.0).astype(BF16)
    for src, dst in ((bufk, xk), (bufv, xv)):
        for c in range(past // tch):
            xt = src[slot, :, :, c * tch:(c + 1) * tch].reshape(LANE, tch).astype(BF16)
            x = lax.dot_general(xt, eye, (((0,), (0,)), ((), ())), preferred_element_type=F32)
            for cl in range(tch // CMP_STRIDE):
                r = (c * (tch // CMP_STRIDE) + cl) * CMP_PITCH
                dst[r:r + CMP_STRIDE, :] = x[cl * CMP_STRIDE:(cl + 1) * CMP_STRIDE, :]
    kv = _compress(lambda j: xk[pl.ds(j, n_chunk, stride=CMP_PITCH), :],
                   lambda j: xv[pl.ds(j, n_chunk, stride=CMP_PITCH), :], w_ref, c_ref, n_chunk)
    kc = kv[:, :LANE].astype(BF16)
    vc = kv[:, LANE:].astype(BF16)
    tok = lax.broadcasted_iota(jnp.int32, (SROWS, 1), 0) & (ts - 1)
    qpos = past + tok
    cend = lax.broadcasted_iota(jnp.int32, (1, n_chunk), 1) * CMP_STRIDE + (CMP_BLOCK - 1)
    p = _cmp_probs(_dot_t(q_ref[0], kc), cend <= qpos)
    ocmp_ref[0] = _dot(p.astype(BF16), vc)
    psum = []
    for k in range(NSA_KV_HEADS):
        acc = p[k * NSA_GROUP * ts:k * NSA_GROUP * ts + ts]
        for g in range(1, NSA_GROUP):
            r = (k * NSA_GROUP + g) * ts
            acc = acc + p[r:r + ts]
        psum.append(acc)
    psum = jnp.concatenate(psum, axis=0)
    imp = jnp.dot(psum, c2s_ref[...], preferred_element_type=F32, precision=lax.Precision.HIGHEST)
    q_blk = lax.shift_right_logical(past + (lax.broadcasted_iota(jnp.int32, (NSA_KV_HEADS * ts, 1), 0) & (ts - 1)), 6)
    sel_ref[0] = _select_blocks(imp, q_blk, n_sel)


def _nsa_s2_kernel(pt_ref, cache_ref, q_ref, sel_ref, new_ref, win_ref, wnew_ref, ocmp_ref, g_ref, e_ref,
                   o_ref, bufk, bufv, sem, *, n_pages, page, ts, wb):
    past = n_pages * page
    slot = _stream_pages(cache_ref, pt_ref, [(2, bufk), (3, bufv)], sem, n_pages, page)
    q = q_ref[0]
    tok = lax.broadcasted_iota(jnp.int32, (SROWS, 1), 0) & (ts - 1)
    sel = sel_ref[0]
    sel_rows = jnp.concatenate([sel[k * ts:(k + 1) * ts] for k in range(NSA_KV_HEADS) for _ in range(NSA_GROUP)],
                               axis=0)
    n_new = new_ref.shape[1]
    jnew = lax.broadcasted_iota(jnp.int32, (1, n_new), 1)

    def attend(kt_past, vt_past, valid_past, k_new, v_new, valid_new):
        s_past = jnp.where(valid_past, _dot(q, kt_past), NEG)
        s_new = jnp.where(valid_new, _dot_t(q, k_new), NEG)
        m = jnp.maximum(jnp.max(s_past, axis=-1, keepdims=True), jnp.max(s_new, axis=-1, keepdims=True))
        p_past = jnp.exp2(s_past - m)
        p_new = jnp.exp2(s_new - m)
        l = jnp.sum(p_past, axis=-1, keepdims=True) + jnp.sum(p_new, axis=-1, keepdims=True)
        return (_dot_t(p_past.astype(BF16), vt_past) + _dot(p_new.astype(BF16), v_new)) / l

    mask_past = _dot(sel_rows.astype(BF16), e_ref[...]) > 0.5
    blk_new = past // SEL_BLOCK
    valid_new = (sel_rows[:, blk_new:blk_new + 1] > 0.5) & (jnew <= tok)
    o_sel = attend(bufk[slot].reshape(LANE, past).astype(BF16), bufv[slot].reshape(LANE, past).astype(BF16),
                   mask_past, new_ref[0, :, 0:LANE].astype(BF16), new_ref[0, :, LANE:HALF_W].astype(BF16), valid_new)
    u = lax.broadcasted_iota(jnp.int32, (1, wb), 1)
    rel = wb + tok - u
    valid_w = (rel >= 0) & (rel < WINDOW) & (past - wb + u >= 0)
    o_win = attend(win_ref[0, 0].reshape(LANE, wb).astype(BF16), win_ref[0, 1].reshape(LANE, wb).astype(BF16),
                   valid_w, wnew_ref[0, :, 0:LANE].astype(BF16), wnew_ref[0, :, LANE:HALF_W].astype(BF16),
                   jnew <= tok)
    gt = g_ref[0]
    o = ocmp_ref[0] * gt[:, 0:1] + o_sel * gt[:, 1:2] + o_win * gt[:, 2:3]
    half = SROWS // NSA_KV_HEADS
    for k in range(NSA_KV_HEADS):
        o_ref[0, k * half:(k + 1) * half, :] = o[k * half:(k + 1) * half, k * NSA_HD:(k + 1) * NSA_HD]


def nsa_sample_attn(page_table, cache, qc2, qr2, new_sel, win, new_win, gates, cmp_w, cmp_c, ts):
    bs, n_pages = page_table.shape
    page = cache.shape[-1]
    past = n_pages * page
    wb = win.shape[-1]
    assert past % SEL_BLOCK == 0 and ts <= CMP_STRIDE and ts & (ts - 1) == 0 and SROWS == NSA_HEADS * ts
    n_chunk = past // CMP_STRIDE
    n_sel = past // SEL_BLOCK + 1
    n_sel_pad = -(-n_sel // LANE) * LANE
    c2s = jnp.asarray(_c2s(n_chunk, n_sel_pad))
    e = jnp.asarray(_expand(n_sel_pad, past), dtype=BF16)
    sems = pltpu.SemaphoreType.DMA((2,))
    pages = pltpu.VMEM((2, NSA_KV_HEADS, NSA_HD, past), F32)
    rows_t = pltpu.VMEM((past // CMP_STRIDE * CMP_PITCH, LANE), F32)
    scratch1 = [pages, pages, rows_t, rows_t, sems]
    scratch2 = [pages, pages, sems]
    seq3 = lambda b, pt: (b, 0, 0)
    o_cmp, sel = pl.pallas_call(
        functools.partial(_nsa_s1_kernel, n_pages=n_pages, page=page, ts=ts, n_sel=n_sel),
        grid_spec=pltpu.PrefetchScalarGridSpec(
            num_scalar_prefetch=1, grid=(bs,),
            in_specs=[pl.BlockSpec(memory_space=pl.ANY),
                      pl.BlockSpec((1, SROWS, LANE), seq3),
                      pl.BlockSpec(cmp_w.shape, lambda b, pt: (0, 0, 0, 0)),
                      pl.BlockSpec(cmp_c.shape, lambda b, pt: (0, 0)),
                      pl.BlockSpec(c2s.shape, lambda b, pt: (0, 0))],
            out_specs=[pl.BlockSpec((1, SROWS, LANE), seq3),
                       pl.BlockSpec((1, NSA_KV_HEADS * ts, n_sel_pad), seq3)],
            scratch_shapes=scratch1),
        out_shape=[jax.ShapeDtypeStruct((bs, SROWS, LANE), F32),
                   jax.ShapeDtypeStruct((bs, NSA_KV_HEADS * ts, n_sel_pad), F32)],
        compiler_params=_cparams(("arbitrary",), 48), name="nsa_sample_cmp",
    )(page_table, cache, qc2, cmp_w, cmp_c, c2s)
    n_new = new_sel.shape[1]
    return pl.pallas_call(
        functools.partial(_nsa_s2_kernel, n_pages=n_pages, page=page, ts=ts, wb=wb),
        grid_spec=pltpu.PrefetchScalarGridSpec(
            num_scalar_prefetch=1, grid=(bs,),
            in_specs=[pl.BlockSpec(memory_space=pl.ANY),
                      pl.BlockSpec((1, SROWS, LANE), seq3),
                      pl.BlockSpec((1, NSA_KV_HEADS * ts, n_sel_pad), seq3),
                      pl.BlockSpec((1, n_new, HALF_W), seq3),
                      pl.BlockSpec((1, 2, NSA_KV_HEADS, NSA_HD, wb), lambda b, pt: (b, 0, 0, 0, 0)),
                      pl.BlockSpec((1, n_new, HALF_W), seq3),
                      pl.BlockSpec((1, SROWS, LANE), seq3),
                      pl.BlockSpec((1, SROWS, 3), seq3),
                      pl.BlockSpec(e.shape, lambda b, pt: (0, 0))],
            out_specs=pl.BlockSpec((1, SROWS, NSA_HD), seq3),
            scratch_shapes=scratch2),
        out_shape=jax.ShapeDtypeStruct((bs, SROWS, NSA_HD), F32),
        compiler_params=_cparams(("arbitrary",), 56), name="nsa_sample_sel",
    )(page_table, cache, qr2, sel, new_sel, win, new_win, o_cmp, gates, e)


def _in_proj_kernel(x_ref, g_ref, wq_ref, wkv_ref, wg_ref, wqkv_ref, wz_ref, wba_ref, cos_ref, sin_ref, qn_ref,
                    kn_ref, *refs, n_nsa):
    nsa_refs, (qkv_ref, z_ref, ba_ref), (q_scr, kv_scr, gate_scr) = refs[:n_nsa], refs[n_nsa:n_nsa + 3], refs[n_nsa + 3:]
    x = x_ref[...]
    xn = (x * lax.rsqrt(jnp.mean(x * x, axis=-1, keepdims=True) + EPS) * g_ref[...]).astype(BF16)
    for w_ref, o_ref in ((wq_ref, q_scr), (wkv_ref, kv_scr), (wg_ref, gate_scr), (wqkv_ref, qkv_ref),
                         (wz_ref, z_ref), (wba_ref, ba_ref)):
        n = w_ref.shape[1]
        ch = _col_chunk(n)
        for c in range(0, n, ch):
            o_ref[:, c:c + ch] = _dot(xn, w_ref[:, c:c + ch])
    _nsa_prep_body(q_scr, kv_scr, gate_scr, cos_ref, sin_ref, qn_ref, kn_ref, *nsa_refs)


def _in_proj(x2d, pos_rows, p, tm, dims_major):
    n_rows, d = x2d.shape
    tm = min(tm, n_rows)
    cos_t, sin_t = _rope_tables(pos_rows)
    n_tab = cos_t.shape[0] // tm
    ws = p["w_in_parts"]
    rows = lambda w: pl.BlockSpec((tm, w), lambda i: (i, 0))
    const = lambda shape: pl.BlockSpec(shape, lambda i: (0, 0))
    tab = pl.BlockSpec((tm, LANE), lambda i: (i % n_tab, 0))
    widths = (NSA_HEADS * LANE, NSA_HEADS * LANE, LANE)
    nsa_specs = [rows(w) for w in widths]
    nsa_shape = [jax.ShapeDtypeStruct((n_rows, w), dt) for w, dt in zip(widths, (BF16, BF16, F32))]
    if dims_major:
        n_seq, t = n_rows // cos_t.shape[0], cos_t.shape[0]
        kv_spec = lambda n: pl.BlockSpec((1, n, NSA_KV_HEADS, NSA_HD, tm), lambda i: (i // n_tab, 0, 0, 0, i % n_tab))
        kv_shape = lambda n: jax.ShapeDtypeStruct((n_seq, n, NSA_KV_HEADS, NSA_HD, t), F32)
        nsa_specs = [kv_spec(4), kv_spec(2)] + nsa_specs + [rows(NSA_KV_W), rows(2 * LANE)]
        nsa_shape = [kv_shape(4), kv_shape(2)] + nsa_shape + [jax.ShapeDtypeStruct((n_rows, NSA_KV_W), BF16),
                                                              jax.ShapeDtypeStruct((n_rows, 2 * LANE), F32)]
    else:
        nsa_specs = [rows(NSA_Q_W), rows(2 * LANE)] + nsa_specs
        nsa_shape = [jax.ShapeDtypeStruct((n_rows, NSA_Q_W), F32),
                     jax.ShapeDtypeStruct((n_rows, 2 * LANE), F32)] + nsa_shape
    gdn_w = [w.shape[1] for w in ws[3:]]
    outs = pl.pallas_call(
        functools.partial(_in_proj_kernel, n_nsa=len(nsa_specs)), grid=(n_rows // tm,),
        in_specs=[rows(d), const((1, d))]
        + [pl.BlockSpec(w.shape, lambda i: (0, 0), pipeline_mode=pl.Buffered(1)) for w in ws]
        + [tab, tab, const((1, LANE)), const((3, LANE))],
        out_specs=nsa_specs + [rows(w) for w in gdn_w],
        out_shape=nsa_shape + [jax.ShapeDtypeStruct((n_rows, w), F32) for w in gdn_w],
        scratch_shapes=[pltpu.VMEM((tm, w.shape[1]), F32) for w in ws[:3]],
        compiler_params=_cparams(("parallel",), 56), name="in_proj",
    )(x2d, p["attn_norm_w"].reshape(1, d), *ws, cos_t, sin_t,
      jnp.tile(p["nsa_q_norm_w"].reshape(1, NSA_HD), (1, 2)), jnp.tile(p["nsa_k_norm_w"], (1, 2)))
    nsa, (qkv_pre, z_pre, ba_pre) = outs[:len(nsa_specs)], outs[len(nsa_specs):]
    return nsa, qkv_pre, z_pre, ba_pre


def _tokens_major(a):
    return a.transpose(0, 4, 1, 2, 3)


def _ffn(h, p, prev, n_seq, shift, tm):
    return conv_ffn(h, p["ffn_norm_w"], p["w_up"], p["ffn_conv_w"], p["ffn_conv_b"], p["w_down"], prev,
                    n_seq, shift, tm)


def _layer_prompt(x, mem, p):
    nb, t, d = x.shape
    x2d = x.reshape(nb * t, d)
    (rows_t, wrows_t, qc, qr, gates, kvb, cmp_rows), qkv_pre, z_pre, ba_pre = _in_proj(x2d, jnp.arange(t), p, 512, True)
    kvc = compress_prompt(cmp_rows.reshape(nb, t, CMP_W), p["cmp_bd"], p["cmp_c"])
    o_nsa = nsa_prompt_attn(qc, qr, kvc, kvb, gates, nb, 256)
    o_gdn, gdn_buf, gdn_s = gdn(qkv_pre, z_pre, ba_pre, jnp.zeros((nb, GDN_CONV - 1, GDN_C3), F32),
                                jnp.zeros((nb, GDN_HEADS, GDN_HD, GDN_HD), F32), p["gdn_conv_w"], p["gdn_a_log"],
                                p["gdn_dt_bias"], p["gdn_norm_w"], nb, 4)
    mem_kv, mem_kv_bf = mem_kv_proj(mem.reshape(-1, d), p["mem_norm_w"], p["w_xk"], p["w_xv"], p["xk_norm_w"])
    h = xattn_fused(x2d, o_nsa, o_gdn, p["w_out"], p["xattn_norm_w"], p["w_xq"], p["xq_norm_w"],
                    mem_kv_bf.reshape(nb, mem.shape[1], 2 * XA_W), p["w_xo"], nb, 512)
    y, ffn_buf = _ffn(h, p, jnp.zeros((nb, FFN_CONV - 1, 2 * D_FF), F32), nb, 1, 512)
    keep = min(WINDOW, t)
    return (y.reshape(nb, t, d), _tokens_major(rows_t), _tokens_major(wrows_t[..., t - keep:]), gdn_s, gdn_buf,
            ffn_buf, mem_kv.reshape(nb, mem.shape[1], 2, XA_HEADS, XA_HD))


def _layer_sample(x, cache_kv, cache_win, state, gdn_cache, ffn_cache, mem_kv, page_table, p):
    bs, ts, d = x.shape
    page = cache_kv.shape[1]
    past = page_table.shape[1] * page
    x2d = x.reshape(bs * ts, d)
    pos_rows = jnp.tile(past + jnp.arange(ts), bs)
    (rows, wrows, qc, qr, gates), qkv_pre, z_pre, ba_pre = _in_proj(x2d, pos_rows, p, bs * ts, False)
    dims_major = lambda a: a.transpose(0, 2, 3, 4, 1)

    def stack_q(a):
        return a.reshape(bs, ts, NSA_HEADS, LANE).transpose(0, 2, 1, 3).reshape(bs, SROWS, LANE)

    def pad_new(a):
        return jnp.pad(a.reshape(bs, ts, HALF_W), ((0, 0), (0, LANE - ts), (0, 0)))

    wb = cache_win.shape[1]
    gates3 = gates[:, :NSA_GATE_W].reshape(bs, ts, NSA_HEADS, 3).transpose(0, 2, 1, 3).reshape(bs, SROWS, 3)
    win_t = dims_major(cache_win)
    o2 = nsa_sample_attn(page_table, dims_major(cache_kv), stack_q(qc), stack_q(qr),
                         pad_new(rows[:, HALF_W:]), win_t, pad_new(wrows), gates3, p["cmp_bd"], p["cmp_c"], ts)
    rows = rows.reshape(bs, ts, 4, NSA_KV_HEADS, NSA_HD)
    wrows = wrows.reshape(bs, ts, 2, NSA_KV_HEADS, NSA_HD)
    o_nsa = o2.reshape(bs, NSA_HEADS, ts, NSA_HD).transpose(0, 2, 1, 3).reshape(bs * ts, NSA_Q_W)
    o_gdn, gdn_buf, gdn_s = gdn(qkv_pre, z_pre, ba_pre, gdn_cache, state, p["gdn_conv_w"], p["gdn_a_log"],
                                p["gdn_dt_bias"], p["gdn_norm_w"], bs, 4)
    h1 = matmul_res([o_nsa, o_gdn], p["w_out"], x2d, 1024)
    (qx,) = rms_matmul(h1, p["xattn_norm_w"], [p["w_xq"]], 512)
    m = mem_kv.shape[1]
    kv_rows = mem_kv.reshape(bs, m, 2, XA_HEADS, XA_HD // LANE, LANE).transpose(0, 1, 2, 4, 3, 5)
    kv_rows = kv_rows.reshape(bs, m * MEM_ROWS, LANE)
    h = matmul_res([xattn_cache(qx, kv_rows, p["xq_norm_w"], bs)], p["w_xo"], h1, 512)
    h_tm = h.reshape(bs, ts, d).transpose(1, 0, 2).reshape(ts * bs, d)
    prev = ffn_cache.transpose(1, 0, 2).reshape(1, (FFN_CONV - 1) * bs, 2 * D_FF)
    y, ffn_buf = _ffn(h_tm, p, prev, 1, bs, ts * bs)
    y = y.reshape(ts, bs, d).transpose(1, 0, 2)
    ffn_buf = ffn_buf.reshape(FFN_CONV - 1, bs, 2 * D_FF).transpose(1, 0, 2)
    wall_t = jnp.concatenate([win_t, dims_major(wrows)], axis=-1)
    keep = min(WINDOW, wb + ts)
    return y, rows, _tokens_major(wall_t[..., wb + ts - keep:]), gdn_s, gdn_buf, ffn_buf


def _prep_params(l, attn_norm_w, w_in, nsa_q_norm_w, nsa_k_norm_w, cmp_pe, cmp_w, gdn_conv_w, gdn_a_log,
                 gdn_dt_bias, gdn_norm_w, w_out, mem_norm_w, w_xk, w_xv, xk_norm_w, xattn_norm_w, w_xq,
                 xq_norm_w, w_xo, ffn_norm_w, w_up, ffn_conv_w, ffn_conv_b, w_down):
    wi = w_in[l]
    cuts = np.cumsum([0, NSA_Q_W, NSA_KV_W, NSA_GATE_W, 3 * GDN_W, GDN_W, 2 * GDN_HEADS])
    parts = []
    for a, b in zip(cuts[:-1], cuts[1:]):
        w = wi[:, a:b]
        padn = -(-(b - a) // LANE) * LANE - (b - a)
        parts.append(jnp.pad(w, ((0, 0), (0, padn))).astype(BF16))
    cmp_bd, cmp_c = _cmp_weights(cmp_pe[l], cmp_w[l])
    bf = lambda w: w[l].astype(BF16)
    return dict(attn_norm_w=attn_norm_w[l], w_in_parts=parts, nsa_q_norm_w=nsa_q_norm_w[l],
                nsa_k_norm_w=nsa_k_norm_w[l], cmp_bd=cmp_bd, cmp_c=cmp_c, gdn_conv_w=gdn_conv_w[l],
                gdn_a_log=gdn_a_log[l], gdn_dt_bias=gdn_dt_bias[l], gdn_norm_w=gdn_norm_w[l], w_out=bf(w_out),
                mem_norm_w=mem_norm_w[l], w_xk=bf(w_xk), w_xv=bf(w_xv), xk_norm_w=xk_norm_w[l],
                xattn_norm_w=xattn_norm_w[l], w_xq=bf(w_xq), xq_norm_w=xq_norm_w[l], w_xo=bf(w_xo),
                ffn_norm_w=ffn_norm_w[l], w_up=bf(w_up), ffn_conv_w=ffn_conv_w[l], ffn_conv_b=ffn_conv_b[l],
                w_down=bf(w_down))


def kernel(x_prompt, x_sample, mem_prompt, cache_nsa_kv, cache_nsa_win, state_gdn, cache_gdn_conv, cache_ffn_conv,
           cache_mem_kv, page_table, attn_norm_w, w_in, nsa_q_norm_w, nsa_k_norm_w, cmp_pe, cmp_w, gdn_conv_w,
           gdn_a_log, gdn_dt_bias, gdn_norm_w, w_out, mem_norm_w, w_xk, w_xv, xk_norm_w, xattn_norm_w, w_xq,
           xq_norm_w, w_xo, ffn_norm_w, w_up, ffn_conv_w, ffn_conv_b, w_down):
    weights = (attn_norm_w, w_in, nsa_q_norm_w, nsa_k_norm_w, cmp_pe, cmp_w, gdn_conv_w, gdn_a_log, gdn_dt_bias,
               gdn_norm_w, w_out, mem_norm_w, w_xk, w_xv, xk_norm_w, xattn_norm_w, w_xq, xq_norm_w, w_xo,
               ffn_norm_w, w_up, ffn_conv_w, ffn_conv_b, w_down)
    depth = cache_nsa_kv.shape[0]
    hp, hs = x_prompt, x_sample
    outs_p, outs_s = [], []
    for l in range(depth):
        p = _prep_params(l, *weights)
        res_p = _layer_prompt(hp, mem_prompt, p)
        hp = res_p[0]
        outs_p.append(res_p[1:])
        res_s = _layer_sample(hs, cache_nsa_kv[l], cache_nsa_win[l], state_gdn[l], cache_gdn_conv[l],
                              cache_ffn_conv[l], cache_mem_kv[l], page_table, p)
        hs = res_s[0]
        outs_s.append(res_s[1:])
    stack = lambda outs, i: jnp.stack([o[i] for o in outs])
    return ((hp, hs) + tuple(stack(outs_p, i) for i in range(6)) + tuple(stack(outs_s, i) for i in range(5)))
```

```python
import functools
import math

import numpy as np
import jax
import jax.numpy as jnp
from jax import lax
from jax.experimental import pallas as pl
from jax.experimental.pallas import tpu as pltpu

F32 = jnp.float32
BF16 = jnp.bfloat16

D_MODEL = 1024
NSA_HEADS = 8
NSA_KV_HEADS = 2
NSA_GROUP = NSA_HEADS // NSA_KV_HEADS
NSA_HD = 64
CMP_BLOCK = 32
CMP_STRIDE = 16
SEL_BLOCK = 64
SEL_TOPN = 8
WINDOW = 512
FORCED_BONUS = 1e4
GDN_HEADS = 4
GDN_HD = 128
GDN_CONV = 4
GDN_CHUNK = 64
GDN_W = GDN_HEADS * GDN_HD
XA_HEADS = 4
XA_HD = 256
D_FF = 2816
FFN_CONV = 3
ROPE_THETA = 10000.0
EPS = 1e-6
NEG = -1e30
LOG2E = 1.4426950408889634

NSA_Q_W = NSA_HEADS * NSA_HD
NSA_KV_W = 3 * 2 * NSA_KV_HEADS * NSA_HD
NSA_GATE_W = 3 * NSA_HEADS
LANE = 128
MIB = 1 << 20


def _cparams(sem, vmem_mib):
    return pltpu.CompilerParams(dimension_semantics=sem, vmem_limit_bytes=vmem_mib * MIB)


def _sigmoid(x):
    return 1.0 / (1.0 + jnp.exp(-x))


def _dot(a, b):
    return jnp.dot(a, b, preferred_element_type=F32)


def _dot_t(a, b):
    return lax.dot_general(a, b, (((1,), (1,)), ((), ())), preferred_element_type=F32)


def _col_chunk(n):
    for c in (512, 256, 128):
        if n % c == 0:
            return c
    return n


def _rms_mm_kernel(x_ref, g_ref, *refs, n_out):
    w_refs, o_refs = refs[:n_out], refs[n_out:]
    x = x_ref[...]
    xn = (x * lax.rsqrt(jnp.mean(x * x, axis=-1, keepdims=True) + EPS) * g_ref[...]).astype(BF16)
    for w_ref, o_ref in zip(w_refs, o_refs):
        n = w_ref.shape[1]
        ch = _col_chunk(n)
        for c in range(0, n, ch):
            o_ref[:, c:c + ch] = _dot(xn, w_ref[:, c:c + ch])


def rms_matmul(x, g, ws, tm):
    n_rows, d = x.shape
    tm = min(tm, n_rows)
    in_specs = [pl.BlockSpec((tm, d), lambda i: (i, 0)), pl.BlockSpec((1, d), lambda i: (0, 0))]
    in_specs += [pl.BlockSpec(w.shape, lambda i: (0, 0), pipeline_mode=pl.Buffered(1)) for w in ws]
    out_specs = [pl.BlockSpec((tm, w.shape[1]), lambda i: (i, 0)) for w in ws]
    out_shape = [jax.ShapeDtypeStruct((n_rows, w.shape[1]), F32) for w in ws]
    return pl.pallas_call(
        functools.partial(_rms_mm_kernel, n_out=len(ws)),
        grid=(n_rows // tm,), in_specs=in_specs, out_specs=out_specs, out_shape=out_shape,
        compiler_params=_cparams(("parallel",), 56), name="rms_matmul",
    )(x, g.reshape(1, d), *ws)


def _mm_res_kernel(*refs, n_in):
    a_refs, (w_ref, r_ref, o_ref) = refs[:n_in], refs[n_in:]
    a = [a_ref[...].astype(BF16) for a_ref in a_refs]
    n = w_ref.shape[1]
    ch = _col_chunk(n)
    for c in range(0, n, ch):
        acc = r_ref[:, c:c + ch]
        k0 = 0
        for x in a:
            acc = acc + _dot(x, w_ref[k0:k0 + x.shape[1], c:c + ch])
            k0 += x.shape[1]
        o_ref[:, c:c + ch] = acc


def matmul_res(a_list, w, res, tm):
    n_rows = res.shape[0]
    k, n = w.shape
    assert sum(a.shape[1] for a in a_list) == k
    tm = min(tm, n_rows)
    return pl.pallas_call(
        functools.partial(_mm_res_kernel, n_in=len(a_list)), grid=(n_rows // tm,),
        in_specs=[pl.BlockSpec((tm, a.shape[1]), lambda i: (i, 0)) for a in a_list]
        + [pl.BlockSpec((k, n), lambda i: (0, 0), pipeline_mode=pl.Buffered(1)),
           pl.BlockSpec((tm, n), lambda i: (i, 0))],
        out_specs=pl.BlockSpec((tm, n), lambda i: (i, 0)),
        out_shape=jax.ShapeDtypeStruct((n_rows, n), F32),
        compiler_params=_cparams(("parallel",), 40), name="matmul_res",
    )(*a_list, w, res)


XA_W = XA_HEADS * XA_HD
MEM_ROWS = 2 * XA_W // LANE


def _memkv_kernel(x_ref, g_ref, wk_ref, wv_ref, kn_ref, o_ref, ob_ref):
    x = x_ref[...]
    xn = (x * lax.rsqrt(jnp.mean(x * x, axis=-1, keepdims=True) + EPS) * g_ref[...]).astype(BF16)
    for h in range(XA_HEADS):
        sl = slice(h * XA_HD, (h + 1) * XA_HD)
        vl = slice(XA_W + h * XA_HD, XA_W + (h + 1) * XA_HD)
        k = _dot(xn, wk_ref[:, sl])
        k = k * lax.rsqrt(jnp.mean(k * k, axis=-1, keepdims=True) + EPS) * kn_ref[...]
        v = _dot(xn, wv_ref[:, sl])
        o_ref[:, sl] = k
        o_ref[:, vl] = v
        ob_ref[:, sl] = k.astype(BF16)
        ob_ref[:, vl] = v.astype(BF16)


def mem_kv_proj(mem, g, wk, wv, kn, tm=256):
    n_rows, d = mem.shape
    out = pl.BlockSpec((tm, 2 * XA_W), lambda i: (i, 0))
    return pl.pallas_call(
        _memkv_kernel, grid=(n_rows // tm,),
        in_specs=[pl.BlockSpec((tm, d), lambda i: (i, 0)), pl.BlockSpec((1, d), lambda i: (0, 0)),
                  pl.BlockSpec((d, XA_W), lambda i: (0, 0)), pl.BlockSpec((d, XA_W), lambda i: (0, 0)),
                  pl.BlockSpec((1, XA_HD), lambda i: (0, 0))],
        out_specs=[out, out],
        out_shape=[jax.ShapeDtypeStruct((n_rows, 2 * XA_W), F32), jax.ShapeDtypeStruct((n_rows, 2 * XA_W), BF16)],
        compiler_params=_cparams(("parallel",), 40), name="mem_kv_proj",
    )(mem, g.reshape(1, d), wk, wv, kn.reshape(1, XA_HD))


def _xattn_head(q, qn, k, v):
    q = q * lax.rsqrt(jnp.mean(q * q, axis=-1, keepdims=True) + EPS) * qn
    s = _dot_t((q * (XA_HD ** -0.5)).astype(BF16), k)
    p = jnp.exp(s - jnp.max(s, axis=-1, keepdims=True))
    return _dot(p.astype(BF16), v) / jnp.sum(p, axis=-1, keepdims=True)


def _xattn_cache_kernel(q_ref, kv_ref, qn_ref, o_ref, *, m):
    def head_block(kind, h):
        halves = [kv_ref[0, pl.ds(kind * (MEM_ROWS // 2) + half * XA_HEADS + h, m, stride=MEM_ROWS), :]
                  for half in range(XA_HD // LANE)]
        return jnp.concatenate(halves, axis=1).astype(BF16)

    for h in range(XA_HEADS):
        sl = slice(h * XA_HD, (h + 1) * XA_HD)
        o_ref[:, sl] = _xattn_head(q_ref[:, sl], qn_ref[...], head_block(0, h), head_block(1, h))


def xattn_cache(q_pre, kv_rows, qn, n_batch):
    n_rows = q_pre.shape[0]
    tq = n_rows // n_batch
    m = kv_rows.shape[1] // MEM_ROWS
    return pl.pallas_call(
        functools.partial(_xattn_cache_kernel, m=m), grid=(n_batch,),
        in_specs=[pl.BlockSpec((tq, XA_W), lambda b: (b, 0)),
                  pl.BlockSpec((1, m * MEM_ROWS, LANE), lambda b: (b, 0, 0)),
                  pl.BlockSpec((1, XA_HD), lambda b: (0, 0))],
        out_specs=pl.BlockSpec((tq, XA_W), lambda b: (b, 0)),
        out_shape=jax.ShapeDtypeStruct((n_rows, XA_W), F32),
        compiler_params=_cparams(("parallel",), 40), name="xattn_cache",
    )(q_pre, kv_rows, qn.reshape(1, XA_HD))


def _xattn_fused_kernel(x_ref, a1_ref, a2_ref, wmix_ref, g_ref, wq_ref, qn_ref, kv_ref, wo_ref, o_ref):
    k1 = a1_ref.shape[1]
    x = (x_ref[...] + _dot(a1_ref[...].astype(BF16), wmix_ref[:k1, :])
         + _dot(a2_ref[...].astype(BF16), wmix_ref[k1:, :]))
    xn = (x * lax.rsqrt(jnp.mean(x * x, axis=-1, keepdims=True) + EPS) * g_ref[...]).astype(BF16)
    heads = []
    for h in range(XA_HEADS):
        sl = slice(h * XA_HD, (h + 1) * XA_HD)
        o = _xattn_head(_dot(xn, wq_ref[:, sl]), qn_ref[...], kv_ref[0, :, sl],
                        kv_ref[0, :, XA_W + h * XA_HD:XA_W + (h + 1) * XA_HD])
        heads.append(o.astype(BF16))
    a = jnp.concatenate(heads, axis=1)
    ch = _col_chunk(D_MODEL)
    for c in range(0, D_MODEL, ch):
        o_ref[:, c:c + ch] = x[:, c:c + ch] + _dot(a, wo_ref[:, c:c + ch])


def xattn_fused(x, a1, a2, wmix, g, wq, qn, kv_bf, wo, n_batch, tq):
    n_rows, d = x.shape
    nt = n_rows // n_batch // tq
    m = kv_bf.shape[1]
    const = lambda b, i: (0, 0)
    rows = lambda w: pl.BlockSpec((tq, w), lambda b, i: (b * nt + i, 0))
    return pl.pallas_call(
        _xattn_fused_kernel, grid=(n_batch, nt),
        in_specs=[rows(d), rows(a1.shape[1]), rows(a2.shape[1]),
                  pl.BlockSpec(wmix.shape, const, pipeline_mode=pl.Buffered(1)), pl.BlockSpec((1, d), const),
                  pl.BlockSpec((d, XA_W), const, pipeline_mode=pl.Buffered(1)), pl.BlockSpec((1, XA_HD), const),
                  pl.BlockSpec((1, m, 2 * XA_W), lambda b, i: (b, 0, 0)),
                  pl.BlockSpec((XA_W, d), const, pipeline_mode=pl.Buffered(1))],
        out_specs=pl.BlockSpec((tq, d), lambda b, i: (b * nt + i, 0)),
        out_shape=jax.ShapeDtypeStruct((n_rows, d), F32),
        compiler_params=_cparams(("parallel", "parallel"), 56), name="xattn_fused",
    )(x, a1, a2, wmix, g.reshape(1, d), wq, qn.reshape(1, XA_HD), kv_bf, wo)


FFN_ACT_CHUNK = 256


def _ffn_kernel(h_ref, g_ref, wup_ref, cw_ref, cb_ref, wdn_ref, prev_ref, o_ref, buf_ref, xs_scr, act_scr,
                *, tm, shift, base):
    t = pl.program_id(1)
    p0 = base - 2 * shift

    @pl.when(t == 0)
    def _():
        xs_scr[p0:base, :] = prev_ref[0]

    x = h_ref[...]
    xn = (x * lax.rsqrt(jnp.mean(x * x, axis=-1, keepdims=True) + EPS) * g_ref[...]).astype(BF16)
    for j in range(D_FF // FFN_ACT_CHUNK):
        halves = []
        for c0 in (j * FFN_ACT_CHUNK, D_FF + j * FFN_ACT_CHUNK):
            sl = slice(c0, c0 + FFN_ACT_CHUNK)
            xs_scr[base:base + tm, sl] = _dot(xn, wup_ref[:, sl])
            y = (cw_ref[0:1, sl] * xs_scr[p0:p0 + tm, sl]
                 + cw_ref[1:2, sl] * xs_scr[p0 + shift:p0 + shift + tm, sl]
                 + cw_ref[2:3, sl] * xs_scr[base:base + tm, sl])
            halves.append(y + cb_ref[:, sl])
        a, u = halves
        act = (a * _sigmoid(a) * u).astype(BF16)
        act_scr[:, j * FFN_ACT_CHUNK:(j + 1) * FFN_ACT_CHUNK] = act
    ch = _col_chunk(D_MODEL)
    for c in range(0, D_MODEL, ch):
        o_ref[:, c:c + ch] = x[:, c:c + ch] + _dot(act_scr[...], wdn_ref[:, c:c + ch])
    last = xs_scr[base + tm - 2 * shift: base + tm, :]
    buf_ref[0] = last
    xs_scr[p0:base, :] = last


def conv_ffn(h, g, wup, cw, cb, wdn, prev, n_seq, shift, tm):
    n_rows, d = h.shape
    t_rows = n_rows // n_seq
    tm = min(tm, t_rows)
    nt = t_rows // tm
    base = -(-2 * shift // 8) * 8
    kern = functools.partial(_ffn_kernel, tm=tm, shift=shift, base=base)
    const = lambda b, i: (0, 0)
    return pl.pallas_call(
        kern, grid=(n_seq, nt),
        in_specs=[pl.BlockSpec((tm, d), lambda b, i: (b * nt + i, 0)),
                  pl.BlockSpec((1, d), const),
                  pl.BlockSpec((d, 2 * D_FF), const, pipeline_mode=pl.Buffered(1)),
                  pl.BlockSpec((FFN_CONV, 2 * D_FF), const),
                  pl.BlockSpec((1, 2 * D_FF), const),
                  pl.BlockSpec((D_FF, d), const, pipeline_mode=pl.Buffered(1)),
                  pl.BlockSpec((1, 2 * shift, 2 * D_FF), lambda b, i: (b, 0, 0))],
        out_specs=[pl.BlockSpec((tm, d), lambda b, i: (b * nt + i, 0)),
                   pl.BlockSpec((1, 2 * shift, 2 * D_FF), lambda b, i: (b, 0, 0))],
        out_shape=[jax.ShapeDtypeStruct((n_rows, d), F32),
                   jax.ShapeDtypeStruct((n_seq, 2 * shift, 2 * D_FF), F32)],
        scratch_shapes=[pltpu.VMEM((base + tm, 2 * D_FF), F32), pltpu.VMEM((tm, D_FF), BF16)],
        compiler_params=_cparams(("arbitrary", "arbitrary"), 56), name="conv_ffn",
    )(h, g.reshape(1, d), wup, cw, cb.reshape(1, 2 * D_FF), wdn, prev)


GDN_C3 = 3 * GDN_W


def _gdn_masks(chunk):
    n = GDN_HEADS * chunk
    r = np.arange(n)[:, None]
    c = np.arange(n)[None, :]
    same = lambda k: (r // k) == (c // k)
    m = [same(2) & (c < r)]
    k = 4
    while k <= chunk:
        m.append(same(k) & ~same(k // 2) & (c < r))
        k *= 2
    m += [same(chunk) & (c <= r), same(chunk) & (c < r), r == c]
    return np.stack(m).astype(np.float32)


def _gdn_kernel(qkv_ref, z_ref, ba_ref, prev_ref, s0_ref, cw_ref, par_ref, nw_ref, m_ref,
                o_ref, buf_ref, s_ref, xs_scr, s_scr, *, rv, chunk, n_units, sequential):
    t = pl.program_id(1)
    nt = pl.num_programs(1)
    stack = GDN_HEADS * chunk
    n_lvl = m_ref.shape[0] - 3
    causal, strict, eye = m_ref[n_lvl], m_ref[n_lvl + 1], m_ref[n_lvl + 2]
    kc = GDN_CONV - 1
    if sequential:
        base = [8 + u * rv for u in range(n_units)]

        @pl.when(t == 0)
        def _():
            xs_scr[8 - kc:8, :] = prev_ref[0]
            s_scr[...] = s0_ref[...]

        xs_scr[8:8 + n_units * rv, :] = qkv_ref[...]
    else:
        base = [u * (8 + rv) + 8 for u in range(n_units)]
        s_scr[...] = s0_ref[...]
        for u in range(n_units):
            xs_scr[base[u] - kc:base[u], :] = prev_ref[u]
            xs_scr[base[u]:base[u] + rv, :] = qkv_ref[u * rv:(u + 1) * rv, :]
    ba = ba_ref[...]
    beta_all = _sigmoid(ba)
    a_sh = ba + par_ref[1:2, :]
    g_all = -jnp.exp(par_ref[0:1, :]) * (jnp.maximum(a_sh, 0.0) + jnp.log1p(jnp.exp(-jnp.abs(a_sh))))
    pad = chunk - rv

    def padrows(a):
        if pad == 0:
            return a
        return jnp.concatenate([a, jnp.zeros((pad, a.shape[1]), a.dtype)], axis=0)

    qms, kms, vms, betas, gcols = [], [], [], [], []
    for u in range(n_units):
        r0 = u * rv
        y = cw_ref[0:1, :] * xs_scr[base[u] - kc:base[u] - kc + rv, :]
        for i in range(1, GDN_CONV):
            y = y + cw_ref[i:i + 1, :] * xs_scr[base[u] - kc + i:base[u] - kc + i + rv, :]
        y = y * _sigmoid(y)
        qs, ks, vs, bs, gs = [], [], [], [], []
        for h in range(GDN_HEADS):
            q = y[:, h * GDN_HD:(h + 1) * GDN_HD]
            k = y[:, GDN_W + h * GDN_HD:GDN_W + (h + 1) * GDN_HD]
            v = y[:, 2 * GDN_W + h * GDN_HD:2 * GDN_W + (h + 1) * GDN_HD]
            q = q * lax.rsqrt(jnp.sum(q * q, axis=-1, keepdims=True) + EPS) * (GDN_HD ** -0.5)
            k = k * lax.rsqrt(jnp.sum(k * k, axis=-1, keepdims=True) + EPS)
            qs.append(padrows(q)); ks.append(padrows(k)); vs.append(padrows(v))
            bs.append(padrows(beta_all[r0:r0 + rv, h:h + 1]))
            gs.append(padrows(g_all[r0:r0 + rv, GDN_HEADS + h:GDN_HEADS + h + 1]))
        qms.append(jnp.concatenate(qs, axis=0))
        kms.append(jnp.concatenate(ks, axis=0))
        vms.append(jnp.concatenate(vs, axis=0))
        betas.append(jnp.concatenate(bs, axis=0))
        gcols.append(jnp.concatenate(gs, axis=0))
    gmat = jnp.concatenate(gcols + [jnp.zeros((stack, LANE - n_units), F32)], axis=1)
    gc_all = jnp.dot(causal, gmat, preferred_element_type=F32, precision=lax.Precision.HIGHEST)

    a_bfs, qks, rhss, gcs, tinv = [], [], [], [], []
    for u in range(n_units):
        gc = gc_all[:, u:u + 1]
        gc_row = jnp.sum(eye * gc, axis=0, keepdims=True)
        decay = jnp.exp(jnp.where(causal > 0, gc - gc_row, 0.0)) * causal
        kb = kms[u] * betas[u]
        kmb = kms[u].astype(BF16)
        a_mat = _dot_t(kb.astype(BF16), kmb) * decay * strict
        qks.append((_dot_t(qms[u].astype(BF16), kmb) * decay).astype(BF16))
        rhss.append(jnp.concatenate([vms[u] * betas[u], kb * jnp.exp(gc)], axis=1).astype(BF16))
        a_bfs.append(a_mat.astype(BF16))
        gcs.append(gc)
        tinv.append(eye - a_mat * m_ref[0])
    for lvl in range(1, n_lvl):
        tb = [x.astype(BF16) for x in tinv]
        xs = [(_dot(a_bfs[u], tb[u]) * m_ref[lvl]).astype(BF16) for u in range(n_units)]
        tinv = [tinv[u] - _dot(tb[u], xs[u]) for u in range(n_units)]
    sols = [_dot(tinv[u].astype(BF16), rhss[u]) for u in range(n_units)]

    for u in range(n_units):
        r0 = u * rv
        st = 0 if sequential else u
        gc = gcs[u]
        u_all, w_all = sols[u][:, :GDN_HD], sols[u][:, GDN_HD:]
        qd = qms[u] * jnp.exp(gc)
        v_news, o_inter, gls = [], [], []
        for h in range(GDN_HEADS):
            rs = slice(h * chunk, (h + 1) * chunk)
            sb = s_scr[st, h].astype(BF16)
            both = _dot(jnp.concatenate([w_all[rs], qd[rs]], axis=0).astype(BF16), sb)
            v_news.append(u_all[rs] - both[:chunk])
            o_inter.append(both[chunk:])
            gls.append(gc[(h + 1) * chunk - 1:(h + 1) * chunk, :])
        v_stack = jnp.concatenate(v_news, axis=0).astype(BF16)
        o_intra = _dot(qks[u], v_stack)
        for h in range(GDN_HEADS):
            rs = slice(h * chunk, (h + 1) * chunk)
            kd = kms[u][rs] * jnp.exp(gls[h] - gc[rs])
            s_scr[st, h] = s_scr[st, h] * jnp.exp(gls[h]) + _dot(kd.T.astype(BF16), v_stack[rs])
            o = (o_inter[h] + o_intra[rs])[:rv]
            on = o * lax.rsqrt(jnp.mean(o * o, axis=-1, keepdims=True) + EPS) * nw_ref[...]
            zz = z_ref[r0:r0 + rv, h * GDN_HD:(h + 1) * GDN_HD]
            o_ref[r0:r0 + rv, h * GDN_HD:(h + 1) * GDN_HD] = on * (zz * _sigmoid(zz))

    if sequential:
        last = xs_scr[8 + n_units * rv - kc:8 + n_units * rv, :]
        buf_ref[0] = last
        xs_scr[8 - kc:8, :] = last

        @pl.when(t == nt - 1)
        def _():
            s_ref[...] = s_scr[...]
    else:
        for u in range(n_units):
            buf_ref[u] = xs_scr[base[u] + rv - kc:base[u] + rv, :]
        s_ref[...] = s_scr[...]


def gdn(qkv_pre, z_pre, ba_pre, prev, s0, conv_w, a_log, dt_bias, norm_w, n_seq, units):
    n_rows = qkv_pre.shape[0]
    t_rows = n_rows // n_seq
    sequential = t_rows >= GDN_CHUNK
    if sequential:
        rv = chunk = GDN_CHUNK
        units = min(units, t_rows // chunk)
        grid = (n_seq, t_rows // (units * chunk))
        seq_blk, scr_rows = 1, 8 + units * rv
    else:
        rv, chunk = t_rows, 16
        units = math.gcd(units, n_seq)
        assert rv % 8 == 0 and rv <= chunk
        grid = (n_seq // units, 1)
        seq_blk, scr_rows = units, units * (8 + rv)
    tt = units * rv
    nt = grid[1]
    par = jnp.zeros((8, LANE), F32).at[0, GDN_HEADS:2 * GDN_HEADS].set(a_log).at[1, GDN_HEADS:2 * GDN_HEADS].set(dt_bias)
    masks = jnp.asarray(_gdn_masks(chunk))
    const2 = lambda b, i: (0, 0)
    rows = lambda b, i: (b * nt + i, 0)
    return pl.pallas_call(
        functools.partial(_gdn_kernel, rv=rv, chunk=chunk, n_units=units, sequential=sequential), grid=grid,
        in_specs=[pl.BlockSpec((tt, GDN_C3), rows), pl.BlockSpec((tt, GDN_W), rows),
                  pl.BlockSpec((tt, LANE), rows),
                  pl.BlockSpec((seq_blk, GDN_CONV - 1, GDN_C3), lambda b, i: (b, 0, 0)),
                  pl.BlockSpec((seq_blk, GDN_HEADS, GDN_HD, GDN_HD), lambda b, i: (b, 0, 0, 0)),
                  pl.BlockSpec((GDN_CONV, GDN_C3), const2), pl.BlockSpec((8, LANE), const2),
                  pl.BlockSpec((1, GDN_HD), const2),
                  pl.BlockSpec(masks.shape, lambda b, i: (0, 0, 0))],
        out_specs=[pl.BlockSpec((tt, GDN_W), rows),
                   pl.BlockSpec((seq_blk, GDN_CONV - 1, GDN_C3), lambda b, i: (b, 0, 0)),
                   pl.BlockSpec((seq_blk, GDN_HEADS, GDN_HD, GDN_HD), lambda b, i: (b, 0, 0, 0))],
        out_shape=[jax.ShapeDtypeStruct((n_rows, GDN_W), F32),
                   jax.ShapeDtypeStruct((n_seq, GDN_CONV - 1, GDN_C3), F32),
                   jax.ShapeDtypeStruct((n_seq, GDN_HEADS, GDN_HD, GDN_HD), F32)],
        scratch_shapes=[pltpu.VMEM((scr_rows, GDN_C3), F32),
                        pltpu.VMEM((seq_blk, GDN_HEADS, GDN_HD, GDN_HD), F32)],
        compiler_params=_cparams(("arbitrary", "arbitrary"), 48), name="gdn",
    )(qkv_pre, z_pre, ba_pre, prev, s0, conv_w, par, norm_w.reshape(1, GDN_HD), masks)


CMP_W = 2 * NSA_KV_HEADS * NSA_HD


def _cmp_weights(cmp_pe, cmp_w):
    def bd(w):
        z = jnp.zeros_like(w)
        return jnp.concatenate([jnp.concatenate([w, z], 2), jnp.concatenate([z, w], 2)], 1)
    ws, cs = [], []
    for s in range(2):
        lo, hi = bd(cmp_w[s, :CMP_STRIDE]), bd(cmp_w[s, CMP_STRIDE:])
        cat = jnp.concatenate([lo, hi], axis=2)
        ws.append(cat.reshape(CMP_STRIDE // 2, 2 * LANE, 2 * LANE))
        pe2 = jnp.concatenate([cmp_pe[s], cmp_pe[s]], axis=-1)
        const = [jnp.einsum("jd,jde->e", pe2[h * CMP_STRIDE:(h + 1) * CMP_STRIDE], m,
                            precision=lax.Precision.HIGHEST) for h, m in enumerate((lo, hi))]
        cs.append(jnp.concatenate(const))
    return jnp.stack(ws).astype(BF16), jnp.stack(cs)


def _compress(read_k, read_v, w_ref, c_ref, n_chunk):
    outs = []
    for s, read in enumerate((read_k, read_v)):
        acc = jnp.zeros((n_chunk, 2 * LANE), F32)
        for p in range(CMP_STRIDE // 2):
            x = jnp.concatenate([read(2 * p), read(2 * p + 1)], axis=1).astype(BF16)
            acc = acc + _dot(x, w_ref[s, p])
        acc = acc + c_ref[s:s + 1, :]
        outs.append(acc[:, :LANE] + pltpu.roll(acc[:, LANE:], n_chunk - 1, axis=0))
    return jnp.concatenate(outs, axis=1)


def _cmp_kernel(k_ref, v_ref, w_ref, c_ref, o_ref, *, n_chunk):
    o_ref[0] = _compress(lambda j: k_ref[0, pl.ds(j, n_chunk, stride=CMP_STRIDE), :],
                         lambda j: v_ref[0, pl.ds(j, n_chunk, stride=CMP_STRIDE), :], w_ref, c_ref, n_chunk)


def compress_prompt(rows, w, c):
    b, t, _ = rows.shape
    n_chunk = t // CMP_STRIDE
    return pl.pallas_call(
        functools.partial(_cmp_kernel, n_chunk=n_chunk), grid=(b,),
        in_specs=[pl.BlockSpec((1, t, LANE), lambda i: (i, 0, 0)), pl.BlockSpec((1, t, LANE), lambda i: (i, 0, 1)),
                  pl.BlockSpec(w.shape, lambda i: (0, 0, 0, 0)),
                  pl.BlockSpec(c.shape, lambda i: (0, 0))],
        out_specs=pl.BlockSpec((1, n_chunk, CMP_W), lambda i: (i, 0, 0)),
        out_shape=jax.ShapeDtypeStruct((b, n_chunk, CMP_W), F32),
        compiler_params=_cparams(("parallel",), 32), name="nsa_compress",
    )(rows, rows, w, c)


def _c2s(n_cmp_pad, n_sel_pad):
    cs = np.arange(n_cmp_pad)[:, None] * CMP_STRIDE
    ss = np.arange(n_sel_pad)[None, :] * SEL_BLOCK
    return ((cs < ss + SEL_BLOCK) & (cs + CMP_BLOCK > ss)).astype(np.float32)


def _expand(n_sel_pad, n_keys):
    s = np.arange(n_sel_pad)[:, None]
    k = np.arange(n_keys)[None, :]
    return (k // SEL_BLOCK == s).astype(np.float32)


def _cmp_probs(s, valid):
    s = jnp.where(valid, s, NEG)
    p = jnp.where(valid, jnp.exp(s - jnp.max(s, axis=-1, keepdims=True)), 0.0)
    return p / jnp.maximum(jnp.sum(p, axis=-1, keepdims=True), 1e-30)


def _select_blocks(imp, q_blk, n_sel):
    lane = lax.broadcasted_iota(jnp.int32, imp.shape, 1)
    visible = lane <= q_blk
    forced = (lane == 0) | (lane == q_blk) | (lane == q_blk - 1)
    score = jnp.where(visible, imp + jnp.where(forced, FORCED_BONUS, 0.0), NEG)
    rank = jnp.zeros(imp.shape, F32)
    for sp in range(n_sel):
        col = score[:, sp:sp + 1]
        beats = (col > score) | ((col == score) & (lane > sp))
        rank = rank + jnp.where(beats, 1.0, 0.0)
    return jnp.where(visible & (rank < SEL_TOPN), 1.0, 0.0)


def _select_blocks_t(imp_t, q_blk, n_sel):
    blk = lax.broadcasted_iota(jnp.int32, imp_t.shape, 0)
    visible = blk <= q_blk
    forced = jnp.where(blk == 0, 1.0, 0.0) + jnp.where(blk == q_blk, 1.0, 0.0) + jnp.where(blk == q_blk - 1, 1.0, 0.0)
    score = jnp.where(visible, imp_t + jnp.where(forced > 0.0, FORCED_BONUS, 0.0), NEG)
    rank = jnp.zeros(imp_t.shape, F32)
    for sp in range(n_sel):
        row = score[sp:sp + 1, :]
        tie = jnp.where(blk > sp, jnp.where(row == score, 1.0, 0.0), 0.0)
        rank = rank + jnp.where(row > score, 1.0, 0.0) + tie
    return jnp.where(visible, jnp.where(rank < SEL_TOPN, 1.0, 0.0), 0.0)


def _pair_rms(x, w):
    lo = lax.broadcasted_iota(jnp.int32, x.shape, 1) < NSA_HD
    x2 = x * x
    s_lo = jnp.sum(jnp.where(lo, x2, 0.0), axis=-1, keepdims=True)
    s_hi = jnp.sum(jnp.where(lo, 0.0, x2), axis=-1, keepdims=True)
    ms = jnp.where(lo, s_lo, s_hi) * (1.0 / NSA_HD)
    return x * lax.rsqrt(ms + EPS) * w


def _pair_rope(x, cos_t, sin_t):
    lane = lax.broadcasted_iota(jnp.int32, x.shape, 1)
    first = (lane & (NSA_HD - 1)) < NSA_HD // 2
    partner = jnp.where(first, pltpu.roll(x, LANE - NSA_HD // 2, axis=1), pltpu.roll(x, NSA_HD // 2, axis=1))
    return x * cos_t + partner * sin_t


def _rope_tables(pos):
    half = NSA_HD // 2
    inv = jnp.power(ROPE_THETA, -jnp.arange(half, dtype=F32) / half)
    ang = pos.astype(F32)[:, None] * inv[None, :]
    c, s = jnp.cos(ang), jnp.sin(ang)
    return jnp.concatenate([c, c, c, c], axis=-1), jnp.concatenate([-s, s, -s, s], axis=-1)


def _nsa_prep_body(q_ref, kv_ref, g_ref, cos_ref, sin_ref, qn_ref, kn_ref,
                   rows_ref, wrows_ref, qc_ref, qr_ref, gate_ref, kvb_ref=None, cmp_ref=None):
    cos_t, sin_t = cos_ref[...], sin_ref[...]
    lo = lax.broadcasted_iota(jnp.int32, cos_t.shape, 1) < NSA_HD
    for c in range(NSA_HEADS // 2):
        y = _pair_rms(q_ref[:, c * LANE:(c + 1) * LANE], qn_ref[...]) * (NSA_HD ** -0.5)
        in_low_lanes = (2 * c) // NSA_GROUP == 0
        for src, dst in ((y, qc_ref), (_pair_rope(y, cos_t, sin_t) * LOG2E, qr_ref)):
            swapped = pltpu.roll(src, NSA_HD, axis=1)
            if in_low_lanes:
                even, odd = jnp.where(lo, src, 0.0), jnp.where(lo, swapped, 0.0)
            else:
                even, odd = jnp.where(lo, 0.0, swapped), jnp.where(lo, 0.0, src)
            dst[:, (2 * c) * LANE:(2 * c + 1) * LANE] = even.astype(BF16)
            dst[:, (2 * c + 1) * LANE:(2 * c + 2) * LANE] = odd.astype(BF16)
    for br in range(3):
        k = _pair_rms(kv_ref[:, br * 2 * LANE:br * 2 * LANE + LANE], kn_ref[br:br + 1, :])
        if br > 0:
            k = _pair_rope(k, cos_t, sin_t)
        v = kv_ref[:, br * 2 * LANE + LANE:(br + 1) * 2 * LANE]
        if cmp_ref is None:
            dst, off = (rows_ref, br * 2 * LANE) if br < 2 else (wrows_ref, 0)
            dst[:, off:off + LANE] = k
            dst[:, off + LANE:off + 2 * LANE] = v
        else:
            dst, w0 = (rows_ref, 2 * br) if br < 2 else (wrows_ref, 0)
            for w, x in ((w0, k), (w0 + 1, v)):
                xt = x.T
                for h in range(NSA_KV_HEADS):
                    dst[0, w, h] = xt[h * NSA_HD:(h + 1) * NSA_HD]
            if br == 0:
                cmp_ref[:, :LANE] = k
                cmp_ref[:, LANE:] = v
            else:
                kvb_ref[:, (br - 1) * LANE:br * LANE] = k.astype(BF16)
                kvb_ref[:, 2 * br * LANE:(2 * br + 1) * LANE] = jnp.where(lo, v, 1.0).astype(BF16)
                kvb_ref[:, (2 * br + 1) * LANE:(2 * br + 2) * LANE] = jnp.where(lo, 1.0, v).astype(BF16)
    gate_ref[...] = _sigmoid(g_ref[...])


def _flash_init(tq):
    rows = NSA_GROUP * tq
    return tuple((jnp.full((rows, 1), NEG, F32), jnp.zeros((rows, LANE), F32)) for _ in range(NSA_KV_HEADS))


def _flash_step(gq, k_ref, v_refs, bias, j, carry, tq, tk):
    off = pl.multiple_of(j * tk, tk)
    k = k_ref[pl.ds(off, tk), :]
    out = []
    for g, (m, acc) in enumerate(carry):
        s = _dot_t(gq[g], k).astype(BF16).reshape(NSA_GROUP, tq, tk) + bias[g][None]
        s = s.reshape(NSA_GROUP * tq, tk)
        m_new = jnp.maximum(m, jnp.max(s, axis=-1, keepdims=True).astype(F32))
        p = jnp.exp2(s - m_new.astype(BF16))
        acc = jnp.exp2(m - m_new) * acc + _dot(p, v_refs[g][pl.ds(off, tk), :])
        out.append((m_new, acc))
    return tuple(out)


def _flash_finish(carry, tq):
    outs = []
    for g, (_, acc) in enumerate(carry):
        c = NSA_HD if g == 0 else 0
        o = acc / acc[:, c:c + 1]
        outs += [o[i * tq:(i + 1) * tq] for i in range(NSA_GROUP)]
    return outs


def _nsa_prompt_kernel(qc_ref, qr_ref, kvc_ref, ks_ref, kw_ref, vs0_ref, vs1_ref, vw0_ref, vw1_ref, g_ref,
                       c2s_ref, e_ref, o_ref, *, tq, n_sel):
    i = pl.program_id(1)
    t0 = i * tq
    n_rows = NSA_HEADS * tq
    stack = lambda ref: jnp.concatenate([ref[:, h * LANE:(h + 1) * LANE] for h in range(NSA_HEADS)], axis=0)
    qpos = t0 + lax.broadcasted_iota(jnp.int32, (tq, 1), 0)
    n_cmp = kvc_ref.shape[1]
    kc = kvc_ref[0, :, :LANE].astype(BF16)
    vc = kvc_ref[0, :, LANE:].astype(BF16)
    cend = lax.broadcasted_iota(jnp.int32, (1, n_cmp), 1) * CMP_STRIDE + (CMP_BLOCK - 1)
    s = _dot_t(stack(qc_ref), kc).reshape(NSA_KV_HEADS, NSA_GROUP, tq, n_cmp)
    p = _cmp_probs(s, (cend <= qpos)[None, None])
    o_cmp = _dot(p.reshape(n_rows, n_cmp).astype(BF16), vc)
    psum = p[:, 0]
    for g in range(1, NSA_GROUP):
        psum = psum + p[:, g]
    imp = jnp.dot(psum.reshape(NSA_KV_HEADS * tq, n_cmp), c2s_ref[...], preferred_element_type=F32,
                  precision=lax.Precision.HIGHEST)
    n_blk_rows = -(-n_sel // 8) * 8
    col = lax.broadcasted_iota(jnp.int32, (1, NSA_KV_HEADS * tq), 1)
    q_blk = lax.shift_right_logical(t0 + (col & (tq - 1)), 6)
    sel_t = _select_blocks_t(imp.T[:n_blk_rows], q_blk, n_sel)
    sel_t = jnp.concatenate([sel_t, jnp.zeros((LANE - n_blk_rows, NSA_KV_HEADS * tq), F32)], axis=0)
    sel = sel_t.T.astype(BF16)
    kiota = lax.broadcasted_iota(jnp.int32, (1, tq), 1)

    def sel_bias(j):
        causal = (j * tq + kiota) <= qpos
        return [((jnp.where(causal, _dot(sel[k * tq:(k + 1) * tq], e_ref[j]), 0.0) - 1.0) * -NEG).astype(BF16)
                for k in range(NSA_KV_HEADS)]

    def win_bias(j):
        rel = qpos - (j * tq + kiota)
        return [jnp.where(jnp.where(rel >= 0, rel, WINDOW) < WINDOW, 0.0, NEG).astype(BF16)] * NSA_KV_HEADS

    gq = [jnp.concatenate([qr_ref[:, h * LANE:(h + 1) * LANE] for h in range(g * NSA_GROUP, (g + 1) * NSA_GROUP)],
                          axis=0) for g in range(NSA_KV_HEADS)]
    sel_step = lambda j, c: _flash_step(gq, ks_ref.at[0], [vs0_ref.at[0], vs1_ref.at[0]], sel_bias(j), j, c, tq, tq)
    win_step = lambda j, c: _flash_step(gq, kw_ref.at[0], [vw0_ref.at[0], vw1_ref.at[0]], win_bias(j), j, c, tq, tq)
    o_sel = _flash_finish(lax.fori_loop(0, i + 1, sel_step, _flash_init(tq)), tq)
    o_win = _flash_finish(lax.fori_loop(jnp.maximum(i - WINDOW // tq, 0), i + 1, win_step, _flash_init(tq)), tq)
    lo = lax.broadcasted_iota(jnp.int32, (tq, LANE), 1) < NSA_HD
    for c in range(NSA_HEADS // 2):
        pair = []
        for h in (2 * c, 2 * c + 1):
            rs = slice(h * tq, (h + 1) * tq)
            gt = g_ref[:, 3 * h:3 * h + 3]
            pair.append(o_cmp[rs] * gt[:, 0:1] + o_sel[h] * gt[:, 1:2] + o_win[h] * gt[:, 2:3])
        even, odd = pair
        if (2 * c) // NSA_GROUP == 0:
            blk = jnp.where(lo, even, pltpu.roll(odd, NSA_HD, axis=1))
        else:
            blk = jnp.where(lo, pltpu.roll(even, NSA_HD, axis=1), odd)
        o_ref[:, c * LANE:(c + 1) * LANE] = blk


def nsa_prompt_attn(qc, qr, kvc, kvb, gates, n_batch, tq):
    n_rows = qc.shape[0]
    t = n_rows // n_batch
    tq = min(tq, t)
    nt = t // tq
    n_cmp = kvc.shape[1]
    n_sel = t // SEL_BLOCK
    assert n_cmp % LANE == 0 and n_sel <= LANE and tq & (tq - 1) == 0 and WINDOW % tq == 0
    c2s = jnp.asarray(_c2s(n_cmp, LANE))
    e = jnp.asarray(_expand(LANE, t).reshape(LANE, nt, tq).transpose(1, 0, 2), dtype=BF16)
    kvb3 = kvb.reshape(n_batch, t, NSA_KV_W)
    rows = lambda w: pl.BlockSpec((tq, w), lambda b, i: (b * nt + i, 0))
    kv = lambda c: pl.BlockSpec((1, t, LANE), lambda b, i: (b, 0, c))
    return pl.pallas_call(
        functools.partial(_nsa_prompt_kernel, tq=tq, n_sel=n_sel), grid=(n_batch, nt),
        in_specs=[rows(NSA_HEADS * LANE), rows(NSA_HEADS * LANE),
                  pl.BlockSpec((1, n_cmp, CMP_W), lambda b, i: (b, 0, 0)),
                  kv(0), kv(1), kv(2), kv(3), kv(4), kv(5), rows(LANE),
                  pl.BlockSpec(c2s.shape, lambda b, i: (0, 0)), pl.BlockSpec(e.shape, lambda b, i: (0, 0, 0))],
        out_specs=rows(NSA_Q_W),
        out_shape=jax.ShapeDtypeStruct((n_rows, NSA_Q_W), F32),
        compiler_params=_cparams(("parallel", "parallel"), 48), name="nsa_prompt_attn",
    )(qc, qr, kvc, kvb3, kvb3, kvb3, kvb3, kvb3, kvb3, gates, c2s, e)


SROWS = NSA_HEADS * 8
CMP_PITCH = CMP_STRIDE + 8
HALF_W = 2 * NSA_KV_HEADS * NSA_HD


def _page_copies(cache_ref, pt_ref, targets, sem, b, slot, n_pages, page):
    return [pltpu.make_async_copy(cache_ref.at[pt_ref[b, pg], kind],
                                  buf.at[slot, :, :, pl.ds(pg * page, page)], sem.at[slot])
            for pg in range(n_pages) for kind, buf in targets]


def _stream_pages(cache_ref, pt_ref, targets, sem, n_pages, page):
    b = pl.program_id(0)
    nb = pl.num_programs(0)
    slot = lax.rem(b, 2)

    @pl.when(b == 0)
    def _():
        for c in _page_copies(cache_ref, pt_ref, targets, sem, 0, 0, n_pages, page):
            c.start()

    @pl.when(b + 1 < nb)
    def _():
        for c in _page_copies(cache_ref, pt_ref, targets, sem, b + 1, 1 - slot, n_pages, page):
            c.start()

    for c in _page_copies(cache_ref, pt_ref, targets, sem, b, slot, n_pages, page):
        c.wait()
    return slot


def _nsa_s1_kernel(pt_ref, cache_ref, q_ref, w_ref, c_ref, c2s_ref, ocmp_ref, sel_ref, bufk, bufv, xk, xv, sem,
                   *, n_pages, page, ts, n_sel):
    past = n_pages * page
    slot = _stream_pages(cache_ref, pt_ref, [(0, bufk), (1, bufv)], sem, n_pages, page)
    n_chunk = past // CMP_STRIDE
    tch = min(past, 8 * LANE)
    eye = jnp.where(lax.broadcasted_iota(jnp.int32, (LANE, LANE), 0) == lax.broadcasted_iota(jnp.int32, (LANE, LANE), 1),
                    1.0, 0.0).astype(BF16)
    for src, dst in ((bufk, xk), (bufv, xv)):
        for c in range(past // tch):
            xt = src[slot, :, :, c * tch:(c + 1) * tch].reshape(LANE, tch).astype(BF16)
            x = lax.dot_general(xt, eye, (((0,), (0,)), ((), ())), preferred_element_type=F32)
            for cl in range(tch // CMP_STRIDE):
                r = (c * (tch // CMP_STRIDE) + cl) * CMP_PITCH
                dst[r:r + CMP_STRIDE, :] = x[cl * CMP_STRIDE:(cl + 1) * CMP_STRIDE, :]
    kv = _compress(lambda j: xk[pl.ds(j, n_chunk, stride=CMP_PITCH), :],
                   lambda j: xv[pl.ds(j, n_chunk, stride=CMP_PITCH), :], w_ref, c_ref, n_chunk)
    kc = kv[:, :LANE].astype(BF16)
    vc = kv[:, LANE:].astype(BF16)
    tok = lax.broadcasted_iota(jnp.int32, (SROWS, 1), 0) & (ts - 1)
    qpos = past + tok
    cend = lax.broadcasted_iota(jnp.int32, (1, n_chunk), 1) * CMP_STRIDE + (CMP_BLOCK - 1)
    p = _cmp_probs(_dot_t(q_ref[0], kc), cend <= qpos)
    ocmp_ref[0] = _dot(p.astype(BF16), vc)
    psum = []
    for k in range(NSA_KV_HEADS):
        acc = p[k * NSA_GROUP * ts:k * NSA_GROUP * ts + ts]
        for g in range(1, NSA_GROUP):
            r = (k * NSA_GROUP + g) * ts
            acc = acc + p[r:r + ts]
        psum.append(acc)
    psum = jnp.concatenate(psum, axis=0)
    imp = jnp.dot(psum, c2s_ref[...], preferred_element_type=F32, precision=lax.Precision.HIGHEST)
    q_blk = lax.shift_right_logical(past + (lax.broadcasted_iota(jnp.int32, (NSA_KV_HEADS * ts, 1), 0) & (ts - 1)), 6)
    sel_ref[0] = _select_blocks(imp, q_blk, n_sel)


def _nsa_s2_kernel(pt_ref, cache_ref, q_ref, sel_ref, new_ref, win_ref, wnew_ref, ocmp_ref, g_ref, e_ref,
                   o_ref, bufk, bufv, sem, *, n_pages, page, ts, wb):
    past = n_pages * page
    slot = _stream_pages(cache_ref, pt_ref, [(2, bufk), (3, bufv)], sem, n_pages, page)
    q = q_ref[0]
    tok = lax.broadcasted_iota(jnp.int32, (SROWS, 1), 0) & (ts - 1)
    sel = sel_ref[0]
    sel_rows = jnp.concatenate([sel[k * ts:(k + 1) * ts] for k in range(NSA_KV_HEADS) for _ in range(NSA_GROUP)],
                               axis=0)
    n_new = new_ref.shape[1]
    jnew = lax.broadcasted_iota(jnp.int32, (1, n_new), 1)

    def attend(kt_past, vt_past, valid_past, k_new, v_new, valid_new):
        s_past = jnp.where(valid_past, _dot(q, kt_past), NEG)
        s_new = jnp.where(valid_new, _dot_t(q, k_new), NEG)
        m = jnp.maximum(jnp.max(s_past, axis=-1, keepdims=True), jnp.max(s_new, axis=-1, keepdims=True))
        p_past = jnp.exp2(s_past - m)
        p_new = jnp.exp2(s_new - m)
        l = jnp.sum(p_past, axis=-1, keepdims=True) + jnp.sum(p_new, axis=-1, keepdims=True)
        return (_dot_t(p_past.astype(BF16), vt_past) + _dot(p_new.astype(BF16), v_new)) / l

    mask_past = _dot(sel_rows.astype(BF16), e_ref[...]) > 0.5
    blk_new = past // SEL_BLOCK
    valid_new = (sel_rows[:, blk_new:blk_new + 1] > 0.5) & (jnew <= tok)
    o_sel = attend(bufk[slot].reshape(LANE, past).astype(BF16), bufv[slot].reshape(LANE, past).astype(BF16),
                   mask_past, new_ref[0, :, 0:LANE].astype(BF16), new_ref[0, :, LANE:HALF_W].astype(BF16), valid_new)
    u = lax.broadcasted_iota(jnp.int32, (1, wb), 1)
    rel = wb + tok - u
    valid_w = (rel >= 0) & (rel < WINDOW) & (past - wb + u >= 0)
    o_win = attend(win_ref[0, 0].reshape(LANE, wb).astype(BF16), win_ref[0, 1].reshape(LANE, wb).astype(BF16),
                   valid_w, wnew_ref[0, :, 0:LANE].astype(BF16), wnew_ref[0, :, LANE:HALF_W].astype(BF16),
                   jnew <= tok)
    gt = g_ref[0]
    o = ocmp_ref[0] * gt[:, 0:1] + o_sel * gt[:, 1:2] + o_win * gt[:, 2:3]
    half = SROWS // NSA_KV_HEADS
    for k in range(NSA_KV_HEADS):
        o_ref[0, k * half:(k + 1) * half, :] = o[k * half:(k + 1) * half, k * NSA_HD:(k + 1) * NSA_HD]


def nsa_sample_attn(page_table, cache, qc2, qr2, new_sel, win, new_win, gates, cmp_w, cmp_c, ts):
    bs, n_pages = page_table.shape
    page = cache.shape[-1]
    past = n_pages * page
    wb = win.shape[-1]
    assert past % SEL_BLOCK == 0 and ts <= CMP_STRIDE and ts & (ts - 1) == 0 and SROWS == NSA_HEADS * ts
    n_chunk = past // CMP_STRIDE
    n_sel = past // SEL_BLOCK + 1
    n_sel_pad = -(-n_sel // LANE) * LANE
    c2s = jnp.asarray(_c2s(n_chunk, n_sel_pad))
    e = jnp.asarray(_expand(n_sel_pad, past), dtype=BF16)
    sems = pltpu.SemaphoreType.DMA((2,))
    pages = pltpu.VMEM((2, NSA_KV_HEADS, NSA_HD, past), F32)
    rows_t = pltpu.VMEM((past // CMP_STRIDE * CMP_PITCH, LANE), F32)
    scratch1 = [pages, pages, rows_t, rows_t, sems]
    scratch2 = [pages, pages, sems]
    seq3 = lambda b, pt: (b, 0, 0)
    o_cmp, sel = pl.pallas_call(
        functools.partial(_nsa_s1_kernel, n_pages=n_pages, page=page, ts=ts, n_sel=n_sel),
        grid_spec=pltpu.PrefetchScalarGridSpec(
            num_scalar_prefetch=1, grid=(bs,),
            in_specs=[pl.BlockSpec(memory_space=pl.ANY),
                      pl.BlockSpec((1, SROWS, LANE), seq3),
                      pl.BlockSpec(cmp_w.shape, lambda b, pt: (0, 0, 0, 0)),
                      pl.BlockSpec(cmp_c.shape, lambda b, pt: (0, 0)),
                      pl.BlockSpec(c2s.shape, lambda b, pt: (0, 0))],
            out_specs=[pl.BlockSpec((1, SROWS, LANE), seq3),
                       pl.BlockSpec((1, NSA_KV_HEADS * ts, n_sel_pad), seq3)],
            scratch_shapes=scratch1),
        out_shape=[jax.ShapeDtypeStruct((bs, SROWS, LANE), F32),
                   jax.ShapeDtypeStruct((bs, NSA_KV_HEADS * ts, n_sel_pad), F32)],
        compiler_params=_cparams(("arbitrary",), 48), name="nsa_sample_cmp",
    )(page_table, cache, qc2, cmp_w, cmp_c, c2s)
    n_new = new_sel.shape[1]
    return pl.pallas_call(
        functools.partial(_nsa_s2_kernel, n_pages=n_pages, page=page, ts=ts, wb=wb),
        grid_spec=pltpu.PrefetchScalarGridSpec(
            num_scalar_prefetch=1, grid=(bs,),
            in_specs=[pl.BlockSpec(memory_space=pl.ANY),
                      pl.BlockSpec((1, SROWS, LANE), seq3),
                      pl.BlockSpec((1, NSA_KV_HEADS * ts, n_sel_pad), seq3),
                      pl.BlockSpec((1, n_new, HALF_W), seq3),
                      pl.BlockSpec((1, 2, NSA_KV_HEADS, NSA_HD, wb), lambda b, pt: (b, 0, 0, 0, 0)),
                      pl.BlockSpec((1, n_new, HALF_W), seq3),
                      pl.BlockSpec((1, SROWS, LANE), seq3),
                      pl.BlockSpec((1, SROWS, 3), seq3),
                      pl.BlockSpec(e.shape, lambda b, pt: (0, 0))],
            out_specs=pl.BlockSpec((1, SROWS, NSA_HD), seq3),
            scratch_shapes=scratch2),
        out_shape=jax.ShapeDtypeStruct((bs, SROWS, NSA_HD), F32),
        compiler_params=_cparams(("arbitrary",), 56), name="nsa_sample_sel",
    )(page_table, cache, qr2, sel, new_sel, win, new_win, o_cmp, gates, e)


def _in_proj_kernel(x_ref, g_ref, wq_ref, wkv_ref, wg_ref, wqkv_ref, wz_ref, wba_ref, cos_ref, sin_ref, qn_ref,
                    kn_ref, *refs, n_nsa):
    nsa_refs, (qkv_ref, z_ref, ba_ref), (q_scr, kv_scr, gate_scr) = refs[:n_nsa], refs[n_nsa:n_nsa + 3], refs[n_nsa + 3:]
    x = x_ref[...]
    xn = (x * lax.rsqrt(jnp.mean(x * x, axis=-1, keepdims=True) + EPS) * g_ref[...]).astype(BF16)
    for w_ref, o_ref in ((wq_ref, q_scr), (wkv_ref, kv_scr), (wg_ref, gate_scr), (wqkv_ref, qkv_ref),
                         (wz_ref, z_ref), (wba_ref, ba_ref)):
        n = w_ref.shape[1]
        ch = _col_chunk(n)
        for c in range(0, n, ch):
            o_ref[:, c:c + ch] = _dot(xn, w_ref[:, c:c + ch])
    _nsa_prep_body(q_scr, kv_scr, gate_scr, cos_ref, sin_ref, qn_ref, kn_ref, *nsa_refs)


def _in_proj(x2d, pos_rows, p, tm, dims_major):
    n_rows, d = x2d.shape
    tm = min(tm, n_rows)
    cos_t, sin_t = _rope_tables(pos_rows)
    n_tab = cos_t.shape[0] // tm
    ws = p["w_in_parts"]
    rows = lambda w: pl.BlockSpec((tm, w), lambda i: (i, 0))
    const = lambda shape: pl.BlockSpec(shape, lambda i: (0, 0))
    tab = pl.BlockSpec((tm, LANE), lambda i: (i % n_tab, 0))
    widths = (NSA_HEADS * LANE, NSA_HEADS * LANE, LANE)
    nsa_specs = [rows(w) for w in widths]
    nsa_shape = [jax.ShapeDtypeStruct((n_rows, w), dt) for w, dt in zip(widths, (BF16, BF16, F32))]
    if dims_major:
        n_seq, t = n_rows // cos_t.shape[0], cos_t.shape[0]
        kv_spec = lambda n: pl.BlockSpec((1, n, NSA_KV_HEADS, NSA_HD, tm), lambda i: (i // n_tab, 0, 0, 0, i % n_tab))
        kv_shape = lambda n: jax.ShapeDtypeStruct((n_seq, n, NSA_KV_HEADS, NSA_HD, t), F32)
        nsa_specs = [kv_spec(4), kv_spec(2)] + nsa_specs + [rows(NSA_KV_W), rows(2 * LANE)]
        nsa_shape = [kv_shape(4), kv_shape(2)] + nsa_shape + [jax.ShapeDtypeStruct((n_rows, NSA_KV_W), BF16),
                                                              jax.ShapeDtypeStruct((n_rows, 2 * LANE), F32)]
    else:
        nsa_specs = [rows(NSA_Q_W), rows(2 * LANE)] + nsa_specs
        nsa_shape = [jax.ShapeDtypeStruct((n_rows, NSA_Q_W), F32),
                     jax.ShapeDtypeStruct((n_rows, 2 * LANE), F32)] + nsa_shape
    gdn_w = [w.shape[1] for w in ws[3:]]
    outs = pl.pallas_call(
        functools.partial(_in_proj_kernel, n_nsa=len(nsa_specs)), grid=(n_rows // tm,),
        in_specs=[rows(d), const((1, d))]
        + [pl.BlockSpec(w.shape, lambda i: (0, 0), pipeline_mode=pl.Buffered(1)) for w in ws]
        + [tab, tab, const((1, LANE)), const((3, LANE))],
        out_specs=nsa_specs + [rows(w) for w in gdn_w],
        out_shape=nsa_shape + [jax.ShapeDtypeStruct((n_rows, w), F32) for w in gdn_w],
        scratch_shapes=[pltpu.VMEM((tm, w.shape[1]), F32) for w in ws[:3]],
        compiler_params=_cparams(("parallel",), 56), name="in_proj",
    )(x2d, p["attn_norm_w"].reshape(1, d), *ws, cos_t, sin_t,
      jnp.tile(p["nsa_q_norm_w"].reshape(1, NSA_HD), (1, 2)), jnp.tile(p["nsa_k_norm_w"], (1, 2)))
    nsa, (qkv_pre, z_pre, ba_pre) = outs[:len(nsa_specs)], outs[len(nsa_specs):]
    return nsa, qkv_pre, z_pre, ba_pre


def _tokens_major(a):
    return a.transpose(0, 4, 1, 2, 3)


def _ffn(h, p, prev, n_seq, shift, tm):
    return conv_ffn(h, p["ffn_norm_w"], p["w_up"], p["ffn_conv_w"], p["ffn_conv_b"], p["w_down"], prev,
                    n_seq, shift, tm)


def _layer_prompt(x, mem, p):
    nb, t, d = x.shape
    x2d = x.reshape(nb * t, d)
    (rows_t, wrows_t, qc, qr, gates, kvb, cmp_rows), qkv_pre, z_pre, ba_pre = _in_proj(x2d, jnp.arange(t), p, 512, True)
    kvc = compress_prompt(cmp_rows.reshape(nb, t, CMP_W), p["cmp_bd"], p["cmp_c"])
    o_nsa = nsa_prompt_attn(qc, qr, kvc, kvb, gates, nb, 256)
    o_gdn, gdn_buf, gdn_s = gdn(qkv_pre, z_pre, ba_pre, jnp.zeros((nb, GDN_CONV - 1, GDN_C3), F32),
                                jnp.zeros((nb, GDN_HEADS, GDN_HD, GDN_HD), F32), p["gdn_conv_w"], p["gdn_a_log"],
                                p["gdn_dt_bias"], p["gdn_norm_w"], nb, 4)
    mem_kv, mem_kv_bf = mem_kv_proj(mem.reshape(-1, d), p["mem_norm_w"], p["w_xk"], p["w_xv"], p["xk_norm_w"])
    h = xattn_fused(x2d, o_nsa, o_gdn, p["w_out"], p["xattn_norm_w"], p["w_xq"], p["xq_norm_w"],
                    mem_kv_bf.reshape(nb, mem.shape[1], 2 * XA_W), p["w_xo"], nb, 1024)
    y, ffn_buf = _ffn(h, p, jnp.zeros((nb, FFN_CONV - 1, 2 * D_FF), F32), nb, 1, 512)
    keep = min(WINDOW, t)
    return (y.reshape(nb, t, d), _tokens_major(rows_t), _tokens_major(wrows_t[..., t - keep:]), gdn_s, gdn_buf,
            ffn_buf, mem_kv.reshape(nb, mem.shape[1], 2, XA_HEADS, XA_HD))


def _layer_sample(x, cache_kv, cache_win, state, gdn_cache, ffn_cache, mem_kv, page_table, p):
    bs, ts, d = x.shape
    page = cache_kv.shape[1]
    past = page_table.shape[1] * page
    x2d = x.reshape(bs * ts, d)
    pos_rows = jnp.tile(past + jnp.arange(ts), bs)
    (rows, wrows, qc, qr, gates), qkv_pre, z_pre, ba_pre = _in_proj(x2d, pos_rows, p, bs * ts, False)
    dims_major = lambda a: a.transpose(0, 2, 3, 4, 1)

    def stack_q(a):
        return a.reshape(bs, ts, NSA_HEADS, LANE).transpose(0, 2, 1, 3).reshape(bs, SROWS, LANE)

    def pad_new(a):
        return jnp.pad(a.reshape(bs, ts, HALF_W), ((0, 0), (0, LANE - ts), (0, 0)))

    wb = cache_win.shape[1]
    gates3 = gates[:, :NSA_GATE_W].reshape(bs, ts, NSA_HEADS, 3).transpose(0, 2, 1, 3).reshape(bs, SROWS, 3)
    win_t = dims_major(cache_win)
    o2 = nsa_sample_attn(page_table, dims_major(cache_kv), stack_q(qc), stack_q(qr),
                         pad_new(rows[:, HALF_W:]), win_t, pad_new(wrows), gates3, p["cmp_bd"], p["cmp_c"], ts)
    rows = rows.reshape(bs, ts, 4, NSA_KV_HEADS, NSA_HD)
    wrows = wrows.reshape(bs, ts, 2, NSA_KV_HEADS, NSA_HD)
    o_nsa = o2.reshape(bs, NSA_HEADS, ts, NSA_HD).transpose(0, 2, 1, 3).reshape(bs * ts, NSA_Q_W)
    o_gdn, gdn_buf, gdn_s = gdn(qkv_pre, z_pre, ba_pre, gdn_cache, state, p["gdn_conv_w"], p["gdn_a_log"],
                                p["gdn_dt_bias"], p["gdn_norm_w"], bs, 4)
    h1 = matmul_res([o_nsa, o_gdn], p["w_out"], x2d, 1024)
    (qx,) = rms_matmul(h1, p["xattn_norm_w"], [p["w_xq"]], 512)
    m = mem_kv.shape[1]
    kv_rows = mem_kv.reshape(bs, m, 2, XA_HEADS, XA_HD // LANE, LANE).transpose(0, 1, 2, 4, 3, 5)
    kv_rows = kv_rows.reshape(bs, m * MEM_ROWS, LANE)
    h = matmul_res([xattn_cache(qx, kv_rows, p["xq_norm_w"], bs)], p["w_xo"], h1, 512)
    h_tm = h.reshape(bs, ts, d).transpose(1, 0, 2).reshape(ts * bs, d)
    prev = ffn_cache.transpose(1, 0, 2).reshape(1, (FFN_CONV - 1) * bs, 2 * D_FF)
    y, ffn_buf = _ffn(h_tm, p, prev, 1, bs, ts * bs)
    y = y.reshape(ts, bs, d).transpose(1, 0, 2)
    ffn_buf = ffn_buf.reshape(FFN_CONV - 1, bs, 2 * D_FF).transpose(1, 0, 2)
    wall_t = jnp.concatenate([win_t, dims_major(wrows)], axis=-1)
    keep = min(WINDOW, wb + ts)
    return y, rows, _tokens_major(wall_t[..., wb + ts - keep:]), gdn_s, gdn_buf, ffn_buf


def _prep_params(l, attn_norm_w, w_in, nsa_q_norm_w, nsa_k_norm_w, cmp_pe, cmp_w, gdn_conv_w, gdn_a_log,
                 gdn_dt_bias, gdn_norm_w, w_out, mem_norm_w, w_xk, w_xv, xk_norm_w, xattn_norm_w, w_xq,
                 xq_norm_w, w_xo, ffn_norm_w, w_up, ffn_conv_w, ffn_conv_b, w_down):
    wi = w_in[l]
    cuts = np.cumsum([0, NSA_Q_W, NSA_KV_W, NSA_GATE_W, 3 * GDN_W, GDN_W, 2 * GDN_HEADS])
    parts = []
    for a, b in zip(cuts[:-1], cuts[1:]):
        w = wi[:, a:b]
        padn = -(-(b - a) // LANE) * LANE - (b - a)
        parts.append(jnp.pad(w, ((0, 0), (0, padn))).astype(BF16))
    cmp_bd, cmp_c = _cmp_weights(cmp_pe[l], cmp_w[l])
    bf = lambda w: w[l].astype(BF16)
    return dict(attn_norm_w=attn_norm_w[l], w_in_parts=parts, nsa_q_norm_w=nsa_q_norm_w[l],
                nsa_k_norm_w=nsa_k_norm_w[l], cmp_bd=cmp_bd, cmp_c=cmp_c, gdn_conv_w=gdn_conv_w[l],
                gdn_a_log=gdn_a_log[l], gdn_dt_bias=gdn_dt_bias[l], gdn_norm_w=gdn_norm_w[l], w_out=bf(w_out),
                mem_norm_w=mem_norm_w[l], w_xk=bf(w_xk), w_xv=bf(w_xv), xk_norm_w=xk_norm_w[l],
                xattn_norm_w=xattn_norm_w[l], w_xq=bf(w_xq), xq_norm_w=xq_norm_w[l], w_xo=bf(w_xo),
                ffn_norm_w=ffn_norm_w[l], w_up=bf(w_up), ffn_conv_w=ffn_conv_w[l], ffn_conv_b=ffn_conv_b[l],
                w_down=bf(w_down))


def kernel(x_prompt, x_sample, mem_prompt, cache_nsa_kv, cache_nsa_win, state_gdn, cache_gdn_conv, cache_ffn_conv,
           cache_mem_kv, page_table, attn_norm_w, w_in, nsa_q_norm_w, nsa_k_norm_w, cmp_pe, cmp_w, gdn_conv_w,
           gdn_a_log, gdn_dt_bias, gdn_norm_w, w_out, mem_norm_w, w_xk, w_xv, xk_norm_w, xattn_norm_w, w_xq,
           xq_norm_w, w_xo, ffn_norm_w, w_up, ffn_conv_w, ffn_conv_b, w_down):
    weights = (attn_norm_w, w_in, nsa_q_norm_w, nsa_k_norm_w, cmp_pe, cmp_w, gdn_conv_w, gdn_a_log, gdn_dt_bias,
               gdn_norm_w, w_out, mem_norm_w, w_xk, w_xv, xk_norm_w, xattn_norm_w, w_xq, xq_norm_w, w_xo,
               ffn_norm_w, w_up, ffn_conv_w, ffn_conv_b, w_down)
    depth = cache_nsa_kv.shape[0]
    hp, hs = x_prompt, x_sample
    outs_p, outs_s = [], []
    for l in range(depth):
        p = _prep_params(l, *weights)
        res_p = _layer_prompt(hp, mem_prompt, p)
        hp = res_p[0]
        outs_p.append(res_p[1:])
        res_s = _layer_sample(hs, cache_nsa_kv[l], cache_nsa_win[l], state_gdn[l], cache_gdn_conv[l],
                              cache_ffn_conv[l], cache_mem_kv[l], page_table, p)
        hs = res_s[0]
        outs_s.append(res_s[1:])
    stack = lambda outs, i: jnp.stack([o[i] for o in outs])
    return ((hp, hs) + tuple(stack(outs_p, i) for i in range(6)) + tuple(stack(outs_s, i) for i in range(5)))
```

```python
import functools
import math

import numpy as np
import jax
import jax.numpy as jnp
from jax import lax
from jax.experimental import pallas as pl
from jax.experimental.pallas import tpu as pltpu

F32 = jnp.float32
BF16 = jnp.bfloat16

D_MODEL = 1024
NSA_HEADS = 8
NSA_KV_HEADS = 2
NSA_GROUP = NSA_HEADS // NSA_KV_HEADS
NSA_HD = 64
CMP_BLOCK = 32
CMP_STRIDE = 16
SEL_BLOCK = 64
SEL_TOPN = 8
WINDOW = 512
FORCED_BONUS = 1e4
GDN_HEADS = 4
GDN_HD = 128
GDN_CONV = 4
GDN_CHUNK = 64
GDN_W = GDN_HEADS * GDN_HD
XA_HEADS = 4
XA_HD = 256
D_FF = 2816
FFN_CONV = 3
ROPE_THETA = 10000.0
EPS = 1e-6
NEG = -1e30
LOG2E = 1.4426950408889634

NSA_Q_W = NSA_HEADS * NSA_HD
NSA_KV_W = 3 * 2 * NSA_KV_HEADS * NSA_HD
NSA_GATE_W = 3 * NSA_HEADS
LANE = 128
MIB = 1 << 20


def _cparams(sem, vmem_mib):
    return pltpu.CompilerParams(dimension_semantics=sem, vmem_limit_bytes=vmem_mib * MIB)


def _sigmoid(x):
    return 1.0 / (1.0 + jnp.exp(-x))


def _dot(a, b):
    return jnp.dot(a, b, preferred_element_type=F32)


def _dot_t(a, b):
    return lax.dot_general(a, b, (((1,), (1,)), ((), ())), preferred_element_type=F32)


def _col_chunk(n):
    for c in (512, 256, 128):
        if n % c == 0:
            return c
    return n


def _rms_mm_kernel(x_ref, g_ref, *refs, n_out):
    w_refs, o_refs = refs[:n_out], refs[n_out:]
    x = x_ref[...]
    xn = (x * lax.rsqrt(jnp.mean(x * x, axis=-1, keepdims=True) + EPS) * g_ref[...]).astype(BF16)
    for w_ref, o_ref in zip(w_refs, o_refs):
        n = w_ref.shape[1]
        ch = _col_chunk(n)
        for c in range(0, n, ch):
            o_ref[:, c:c + ch] = _dot(xn, w_ref[:, c:c + ch])


def rms_matmul(x, g, ws, tm):
    n_rows, d = x.shape
    tm = min(tm, n_rows)
    in_specs = [pl.BlockSpec((tm, d), lambda i: (i, 0)), pl.BlockSpec((1, d), lambda i: (0, 0))]
    in_specs += [pl.BlockSpec(w.shape, lambda i: (0, 0), pipeline_mode=pl.Buffered(1)) for w in ws]
    out_specs = [pl.BlockSpec((tm, w.shape[1]), lambda i: (i, 0)) for w in ws]
    out_shape = [jax.ShapeDtypeStruct((n_rows, w.shape[1]), F32) for w in ws]
    return pl.pallas_call(
        functools.partial(_rms_mm_kernel, n_out=len(ws)),
        grid=(n_rows // tm,), in_specs=in_specs, out_specs=out_specs, out_shape=out_shape,
        compiler_params=_cparams(("parallel",), 56), name="rms_matmul",
    )(x, g.reshape(1, d), *ws)


def _mm_res_kernel(*refs, n_in):
    a_refs, (w_ref, r_ref, o_ref) = refs[:n_in], refs[n_in:]
    a = [a_ref[...].astype(BF16) for a_ref in a_refs]
    n = w_ref.shape[1]
    ch = _col_chunk(n)
    for c in range(0, n, ch):
        acc = r_ref[:, c:c + ch]
        k0 = 0
        for x in a:
            acc = acc + _dot(x, w_ref[k0:k0 + x.shape[1], c:c + ch])
            k0 += x.shape[1]
        o_ref[:, c:c + ch] = acc


def matmul_res(a_list, w, res, tm):
    n_rows = res.shape[0]
    k, n = w.shape
    assert sum(a.shape[1] for a in a_list) == k
    tm = min(tm, n_rows)
    return pl.pallas_call(
        functools.partial(_mm_res_kernel, n_in=len(a_list)), grid=(n_rows // tm,),
        in_specs=[pl.BlockSpec((tm, a.shape[1]), lambda i: (i, 0)) for a in a_list]
        + [pl.BlockSpec((k, n), lambda i: (0, 0), pipeline_mode=pl.Buffered(1)),
           pl.BlockSpec((tm, n), lambda i: (i, 0))],
        out_specs=pl.BlockSpec((tm, n), lambda i: (i, 0)),
        out_shape=jax.ShapeDtypeStruct((n_rows, n), F32),
        compiler_params=_cparams(("parallel",), 40), name="matmul_res",
    )(*a_list, w, res)


XA_W = XA_HEADS * XA_HD
MEM_ROWS = 2 * XA_W // LANE


def _memkv_kernel(x_ref, g_ref, wk_ref, wv_ref, kn_ref, o_ref, ob_ref):
    x = x_ref[...]
    xn = (x * lax.rsqrt(jnp.mean(x * x, axis=-1, keepdims=True) + EPS) * g_ref[...]).astype(BF16)
    for h in range(XA_HEADS):
        sl = slice(h * XA_HD, (h + 1) * XA_HD)
        vl = slice(XA_W + h * XA_HD, XA_W + (h + 1) * XA_HD)
        k = _dot(xn, wk_ref[:, sl])
        k = k * lax.rsqrt(jnp.mean(k * k, axis=-1, keepdims=True) + EPS) * kn_ref[...]
        v = _dot(xn, wv_ref[:, sl])
        o_ref[:, sl] = k
        o_ref[:, vl] = v
        ob_ref[:, sl] = k.astype(BF16)
        ob_ref[:, vl] = v.astype(BF16)


def mem_kv_proj(mem, g, wk, wv, kn, tm=256):
    n_rows, d = mem.shape
    out = pl.BlockSpec((tm, 2 * XA_W), lambda i: (i, 0))
    return pl.pallas_call(
        _memkv_kernel, grid=(n_rows // tm,),
        in_specs=[pl.BlockSpec((tm, d), lambda i: (i, 0)), pl.BlockSpec((1, d), lambda i: (0, 0)),
                  pl.BlockSpec((d, XA_W), lambda i: (0, 0)), pl.BlockSpec((d, XA_W), lambda i: (0, 0)),
                  pl.BlockSpec((1, XA_HD), lambda i: (0, 0))],
        out_specs=[out, out],
        out_shape=[jax.ShapeDtypeStruct((n_rows, 2 * XA_W), F32), jax.ShapeDtypeStruct((n_rows, 2 * XA_W), BF16)],
        compiler_params=_cparams(("parallel",), 40), name="mem_kv_proj",
    )(mem, g.reshape(1, d), wk, wv, kn.reshape(1, XA_HD))


def _xattn_head(q, qn, k, v):
    q = q * lax.rsqrt(jnp.mean(q * q, axis=-1, keepdims=True) + EPS) * qn
    s = _dot_t((q * (XA_HD ** -0.5)).astype(BF16), k)
    p = jnp.exp(s - jnp.max(s, axis=-1, keepdims=True))
    return _dot(p.astype(BF16), v) / jnp.sum(p, axis=-1, keepdims=True)


def _xattn_cache_kernel(q_ref, kv_ref, qn_ref, o_ref, *, m):
    def head_block(kind, h):
        halves = [kv_ref[0, pl.ds(kind * (MEM_ROWS // 2) + half * XA_HEADS + h, m, stride=MEM_ROWS), :]
                  for half in range(XA_HD // LANE)]
        return jnp.concatenate(halves, axis=1).astype(BF16)

    for h in range(XA_HEADS):
        sl = slice(h * XA_HD, (h + 1) * XA_HD)
        o_ref[:, sl] = _xattn_head(q_ref[:, sl], qn_ref[...], head_block(0, h), head_block(1, h))


def xattn_cache(q_pre, kv_rows, qn, n_batch):
    n_rows = q_pre.shape[0]
    tq = n_rows // n_batch
    m = kv_rows.shape[1] // MEM_ROWS
    return pl.pallas_call(
        functools.partial(_xattn_cache_kernel, m=m), grid=(n_batch,),
        in_specs=[pl.BlockSpec((tq, XA_W), lambda b: (b, 0)),
                  pl.BlockSpec((1, m * MEM_ROWS, LANE), lambda b: (b, 0, 0)),
                  pl.BlockSpec((1, XA_HD), lambda b: (0, 0))],
        out_specs=pl.BlockSpec((tq, XA_W), lambda b: (b, 0)),
        out_shape=jax.ShapeDtypeStruct((n_rows, XA_W), F32),
        compiler_params=_cparams(("parallel",), 40), name="xattn_cache",
    )(q_pre, kv_rows, qn.reshape(1, XA_HD))


def _xattn_fused_kernel(x_ref, a1_ref, a2_ref, wmix_ref, g_ref, wq_ref, qn_ref, kv_ref, wo_ref, o_ref):
    k1 = a1_ref.shape[1]
    x = (x_ref[...] + _dot(a1_ref[...].astype(BF16), wmix_ref[:k1, :])
         + _dot(a2_ref[...].astype(BF16), wmix_ref[k1:, :]))
    xn = (x * lax.rsqrt(jnp.mean(x * x, axis=-1, keepdims=True) + EPS) * g_ref[...]).astype(BF16)
    heads = []
    for h in range(XA_HEADS):
        sl = slice(h * XA_HD, (h + 1) * XA_HD)
        o = _xattn_head(_dot(xn, wq_ref[:, sl]), qn_ref[...], kv_ref[0, :, sl],
                        kv_ref[0, :, XA_W + h * XA_HD:XA_W + (h + 1) * XA_HD])
        heads.append(o.astype(BF16))
    a = jnp.concatenate(heads, axis=1)
    ch = _col_chunk(D_MODEL)
    for c in range(0, D_MODEL, ch):
        o_ref[:, c:c + ch] = x[:, c:c + ch] + _dot(a, wo_ref[:, c:c + ch])


def xattn_fused(x, a1, a2, wmix, g, wq, qn, kv_bf, wo, n_batch, tq):
    n_rows, d = x.shape
    nt = n_rows // n_batch // tq
    m = kv_bf.shape[1]
    const = lambda b, i: (0, 0)
    rows = lambda w: pl.BlockSpec((tq, w), lambda b, i: (b * nt + i, 0))
    return pl.pallas_call(
        _xattn_fused_kernel, grid=(n_batch, nt),
        in_specs=[rows(d), rows(a1.shape[1]), rows(a2.shape[1]),
                  pl.BlockSpec(wmix.shape, const, pipeline_mode=pl.Buffered(1)), pl.BlockSpec((1, d), const),
                  pl.BlockSpec((d, XA_W), const, pipeline_mode=pl.Buffered(1)), pl.BlockSpec((1, XA_HD), const),
                  pl.BlockSpec((1, m, 2 * XA_W), lambda b, i: (b, 0, 0)),
                  pl.BlockSpec((XA_W, d), const, pipeline_mode=pl.Buffered(1))],
        out_specs=pl.BlockSpec((tq, d), lambda b, i: (b * nt + i, 0)),
        out_shape=jax.ShapeDtypeStruct((n_rows, d), F32),
        compiler_params=_cparams(("parallel", "parallel"), 56), name="xattn_fused",
    )(x, a1, a2, wmix, g.reshape(1, d), wq, qn.reshape(1, XA_HD), kv_bf, wo)


FFN_ACT_CHUNK = 256


def _ffn_kernel(h_ref, g_ref, wup_ref, cw_ref, cb_ref, wdn_ref, prev_ref, o_ref, buf_ref, xs_scr, act_scr,
                *, tm, shift, base):
    t = pl.program_id(1)
    p0 = base - 2 * shift

    @pl.when(t == 0)
    def _():
        xs_scr[p0:base, :] = prev_ref[0]

    x = h_ref[...]
    xn = (x * lax.rsqrt(jnp.mean(x * x, axis=-1, keepdims=True) + EPS) * g_ref[...]).astype(BF16)
    for j in range(D_FF // FFN_ACT_CHUNK):
        halves = []
        for c0 in (j * FFN_ACT_CHUNK, D_FF + j * FFN_ACT_CHUNK):
            sl = slice(c0, c0 + FFN_ACT_CHUNK)
            xs_scr[base:base + tm, sl] = _dot(xn, wup_ref[:, sl])
            y = (cw_ref[0:1, sl] * xs_scr[p0:p0 + tm, sl]
                 + cw_ref[1:2, sl] * xs_scr[p0 + shift:p0 + shift + tm, sl]
                 + cw_ref[2:3, sl] * xs_scr[base:base + tm, sl])
            halves.append(y + cb_ref[:, sl])
        a, u = halves
        act = (a * _sigmoid(a) * u).astype(BF16)
        act_scr[:, j * FFN_ACT_CHUNK:(j + 1) * FFN_ACT_CHUNK] = act
    ch = _col_chunk(D_MODEL)
    for c in range(0, D_MODEL, ch):
        o_ref[:, c:c + ch] = x[:, c:c + ch] + _dot(act_scr[...], wdn_ref[:, c:c + ch])
    last = xs_scr[base + tm - 2 * shift: base + tm, :]
    buf_ref[0] = last
    xs_scr[p0:base, :] = last


def conv_ffn(h, g, wup, cw, cb, wdn, prev, n_seq, shift, tm):
    n_rows, d = h.shape
    t_rows = n_rows // n_seq
    tm = min(tm, t_rows)
    nt = t_rows // tm
    base = -(-2 * shift // 8) * 8
    kern = functools.partial(_ffn_kernel, tm=tm, shift=shift, base=base)
    const = lambda b, i: (0, 0)
    return pl.pallas_call(
        kern, grid=(n_seq, nt),
        in_specs=[pl.BlockSpec((tm, d), lambda b, i: (b * nt + i, 0)),
                  pl.BlockSpec((1, d), const),
                  pl.BlockSpec((d, 2 * D_FF), const, pipeline_mode=pl.Buffered(1)),
                  pl.BlockSpec((FFN_CONV, 2 * D_FF), const),
                  pl.BlockSpec((1, 2 * D_FF), const),
                  pl.BlockSpec((D_FF, d), const, pipeline_mode=pl.Buffered(1)),
                  pl.BlockSpec((1, 2 * shift, 2 * D_FF), lambda b, i: (b, 0, 0))],
        out_specs=[pl.BlockSpec((tm, d), lambda b, i: (b * nt + i, 0)),
                   pl.BlockSpec((1, 2 * shift, 2 * D_FF), lambda b, i: (b, 0, 0))],
        out_shape=[jax.ShapeDtypeStruct((n_rows, d), F32),
                   jax.ShapeDtypeStruct((n_seq, 2 * shift, 2 * D_FF), F32)],
        scratch_shapes=[pltpu.VMEM((base + tm, 2 * D_FF), F32), pltpu.VMEM((tm, D_FF), BF16)],
        compiler_params=_cparams(("arbitrary", "arbitrary"), 56), name="conv_ffn",
    )(h, g.reshape(1, d), wup, cw, cb.reshape(1, 2 * D_FF), wdn, prev)


GDN_C3 = 3 * GDN_W


def _gdn_masks(chunk):
    n = GDN_HEADS * chunk
    r = np.arange(n)[:, None]
    c = np.arange(n)[None, :]
    same = lambda k: (r // k) == (c // k)
    m = [same(2) & (c < r)]
    k = 4
    while k <= chunk:
        m.append(same(k) & ~same(k // 2) & (c < r))
        k *= 2
    m += [same(chunk) & (c <= r), same(chunk) & (c < r), r == c]
    return np.stack(m).astype(np.float32)


def _gdn_kernel(qkv_ref, z_ref, ba_ref, prev_ref, s0_ref, cw_ref, par_ref, nw_ref, m_ref,
                o_ref, buf_ref, s_ref, xs_scr, s_scr, *, rv, chunk, n_units, sequential):
    t = pl.program_id(1)
    nt = pl.num_programs(1)
    stack = GDN_HEADS * chunk
    n_lvl = m_ref.shape[0] - 3
    causal, strict, eye = m_ref[n_lvl], m_ref[n_lvl + 1], m_ref[n_lvl + 2]
    kc = GDN_CONV - 1
    if sequential:
        base = [8 + u * rv for u in range(n_units)]

        @pl.when(t == 0)
        def _():
            xs_scr[8 - kc:8, :] = prev_ref[0]
            s_scr[...] = s0_ref[...]

        xs_scr[8:8 + n_units * rv, :] = qkv_ref[...]
    else:
        base = [u * (8 + rv) + 8 for u in range(n_units)]
        s_scr[...] = s0_ref[...]
        for u in range(n_units):
            xs_scr[base[u] - kc:base[u], :] = prev_ref[u]
            xs_scr[base[u]:base[u] + rv, :] = qkv_ref[u * rv:(u + 1) * rv, :]
    ba = ba_ref[...]
    beta_all = _sigmoid(ba)
    a_sh = ba + par_ref[1:2, :]
    g_all = -jnp.exp(par_ref[0:1, :]) * (jnp.maximum(a_sh, 0.0) + jnp.log1p(jnp.exp(-jnp.abs(a_sh))))
    pad = chunk - rv

    def padrows(a):
        if pad == 0:
            return a
        return jnp.concatenate([a, jnp.zeros((pad, a.shape[1]), a.dtype)], axis=0)

    qms, kms, vms, betas, gcols = [], [], [], [], []
    for u in range(n_units):
        r0 = u * rv
        y = cw_ref[0:1, :] * xs_scr[base[u] - kc:base[u] - kc + rv, :]
        for i in range(1, GDN_CONV):
            y = y + cw_ref[i:i + 1, :] * xs_scr[base[u] - kc + i:base[u] - kc + i + rv, :]
        y = y * _sigmoid(y)
        qs, ks, vs, bs, gs = [], [], [], [], []
        for h in range(GDN_HEADS):
            q = y[:, h * GDN_HD:(h + 1) * GDN_HD]
            k = y[:, GDN_W + h * GDN_HD:GDN_W + (h + 1) * GDN_HD]
            v = y[:, 2 * GDN_W + h * GDN_HD:2 * GDN_W + (h + 1) * GDN_HD]
            q = q * lax.rsqrt(jnp.sum(q * q, axis=-1, keepdims=True) + EPS) * (GDN_HD ** -0.5)
            k = k * lax.rsqrt(jnp.sum(k * k, axis=-1, keepdims=True) + EPS)
            qs.append(padrows(q)); ks.append(padrows(k)); vs.append(padrows(v))
            bs.append(padrows(beta_all[r0:r0 + rv, h:h + 1]))
            gs.append(padrows(g_all[r0:r0 + rv, GDN_HEADS + h:GDN_HEADS + h + 1]))
        qms.append(jnp.concatenate(qs, axis=0))
        kms.append(jnp.concatenate(ks, axis=0))
        vms.append(jnp.concatenate(vs, axis=0))
        betas.append(jnp.concatenate(bs, axis=0))
        gcols.append(jnp.concatenate(gs, axis=0))
    gmat = jnp.concatenate(gcols + [jnp.zeros((stack, LANE - n_units), F32)], axis=1)
    gc_all = jnp.dot(causal, gmat, preferred_element_type=F32, precision=lax.Precision.HIGHEST)

    a_bfs, qks, rhss, gcs, tinv = [], [], [], [], []
    for u in range(n_units):
        gc = gc_all[:, u:u + 1]
        gc_row = jnp.sum(eye * gc, axis=0, keepdims=True)
        decay = jnp.exp(jnp.where(causal > 0, gc - gc_row, 0.0)) * causal
        kb = kms[u] * betas[u]
        kmb = kms[u].astype(BF16)
        a_mat = _dot_t(kb.astype(BF16), kmb) * decay * strict
        qks.append((_dot_t(qms[u].astype(BF16), kmb) * decay).astype(BF16))
        rhss.append(jnp.concatenate([vms[u] * betas[u], kb * jnp.exp(gc)], axis=1).astype(BF16))
        a_bfs.append(a_mat.astype(BF16))
        gcs.append(gc)
        tinv.append(eye - a_mat * m_ref[0])
    for lvl in range(1, n_lvl):
        tb = [x.astype(BF16) for x in tinv]
        xs = [(_dot(a_bfs[u], tb[u]) * m_ref[lvl]).astype(BF16) for u in range(n_units)]
        tinv = [tinv[u] - _dot(tb[u], xs[u]) for u in range(n_units)]
    sols = [_dot(tinv[u].astype(BF16), rhss[u]) for u in range(n_units)]

    for u in range(n_units):
        r0 = u * rv
        st = 0 if sequential else u
        gc = gcs[u]
        u_all, w_all = sols[u][:, :GDN_HD], sols[u][:, GDN_HD:]
        qd = qms[u] * jnp.exp(gc)
        v_news, o_inter, gls = [], [], []
        for h in range(GDN_HEADS):
            rs = slice(h * chunk, (h + 1) * chunk)
            sb = s_scr[st, h].astype(BF16)
            both = _dot(jnp.concatenate([w_all[rs], qd[rs]], axis=0).astype(BF16), sb)
            v_news.append(u_all[rs] - both[:chunk])
            o_inter.append(both[chunk:])
            gls.append(gc[(h + 1) * chunk - 1:(h + 1) * chunk, :])
        v_stack = jnp.concatenate(v_news, axis=0).astype(BF16)
        o_intra = _dot(qks[u], v_stack)
        for h in range(GDN_HEADS):
            rs = slice(h * chunk, (h + 1) * chunk)
            kd = kms[u][rs] * jnp.exp(gls[h] - gc[rs])
            s_scr[st, h] = s_scr[st, h] * jnp.exp(gls[h]) + _dot(kd.T.astype(BF16), v_stack[rs])
            o = (o_inter[h] + o_intra[rs])[:rv]
            on = o * lax.rsqrt(jnp.mean(o * o, axis=-1, keepdims=True) + EPS) * nw_ref[...]
            zz = z_ref[r0:r0 + rv, h * GDN_HD:(h + 1) * GDN_HD]
            o_ref[r0:r0 + rv, h * GDN_HD:(h + 1) * GDN_HD] = on * (zz * _sigmoid(zz))

    if sequential:
        last = xs_scr[8 + n_units * rv - kc:8 + n_units * rv, :]
        buf_ref[0] = last
        xs_scr[8 - kc:8, :] = last

        @pl.when(t == nt - 1)
        def _():
            s_ref[...] = s_scr[...]
    else:
        for u in range(n_units):
            buf_ref[u] = xs_scr[base[u] + rv - kc:base[u] + rv, :]
        s_ref[...] = s_scr[...]


def gdn(qkv_pre, z_pre, ba_pre, prev, s0, conv_w, a_log, dt_bias, norm_w, n_seq, units):
    n_rows = qkv_pre.shape[0]
    t_rows = n_rows // n_seq
    sequential = t_rows >= GDN_CHUNK
    if sequential:
        rv = chunk = GDN_CHUNK
        units = min(units, t_rows // chunk)
        grid = (n_seq, t_rows // (units * chunk))
        seq_blk, scr_rows = 1, 8 + units * rv
    else:
        rv, chunk = t_rows, 16
        units = math.gcd(units, n_seq)
        assert rv % 8 == 0 and rv <= chunk
        grid = (n_seq // units, 1)
        seq_blk, scr_rows = units, units * (8 + rv)
    tt = units * rv
    nt = grid[1]
    par = jnp.zeros((8, LANE), F32).at[0, GDN_HEADS:2 * GDN_HEADS].set(a_log).at[1, GDN_HEADS:2 * GDN_HEADS].set(dt_bias)
    masks = jnp.asarray(_gdn_masks(chunk))
    const2 = lambda b, i: (0, 0)
    rows = lambda b, i: (b * nt + i, 0)
    return pl.pallas_call(
        functools.partial(_gdn_kernel, rv=rv, chunk=chunk, n_units=units, sequential=sequential), grid=grid,
        in_specs=[pl.BlockSpec((tt, GDN_C3), rows), pl.BlockSpec((tt, GDN_W), rows),
                  pl.BlockSpec((tt, LANE), rows),
                  pl.BlockSpec((seq_blk, GDN_CONV - 1, GDN_C3), lambda b, i: (b, 0, 0)),
                  pl.BlockSpec((seq_blk, GDN_HEADS, GDN_HD, GDN_HD), lambda b, i: (b, 0, 0, 0)),
                  pl.BlockSpec((GDN_CONV, GDN_C3), const2), pl.BlockSpec((8, LANE), const2),
                  pl.BlockSpec((1, GDN_HD), const2),
                  pl.BlockSpec(masks.shape, lambda b, i: (0, 0, 0))],
        out_specs=[pl.BlockSpec((tt, GDN_W), rows),
                   pl.BlockSpec((seq_blk, GDN_CONV - 1, GDN_C3), lambda b, i: (b, 0, 0)),
                   pl.BlockSpec((seq_blk, GDN_HEADS, GDN_HD, GDN_HD), lambda b, i: (b, 0, 0, 0))],
        out_shape=[jax.ShapeDtypeStruct((n_rows, GDN_W), F32),
                   jax.ShapeDtypeStruct((n_seq, GDN_CONV - 1, GDN_C3), F32),
                   jax.ShapeDtypeStruct((n_seq, GDN_HEADS, GDN_HD, GDN_HD), F32)],
        scratch_shapes=[pltpu.VMEM((scr_rows, GDN_C3), F32),
                        pltpu.VMEM((seq_blk, GDN_HEADS, GDN_HD, GDN_HD), F32)],
        compiler_params=_cparams(("arbitrary", "arbitrary"), 48), name="gdn",
    )(qkv_pre, z_pre, ba_pre, prev, s0, conv_w, par, norm_w.reshape(1, GDN_HD), masks)


CMP_W = 2 * NSA_KV_HEADS * NSA_HD


def _cmp_weights(cmp_pe, cmp_w):
    def bd(w):
        z = jnp.zeros_like(w)
        return jnp.concatenate([jnp.concatenate([w, z], 2), jnp.concatenate([z, w], 2)], 1)
    ws, cs = [], []
    for s in range(2):
        lo, hi = bd(cmp_w[s, :CMP_STRIDE]), bd(cmp_w[s, CMP_STRIDE:])
        cat = jnp.concatenate([lo, hi], axis=2)
        ws.append(cat.reshape(CMP_STRIDE // 2, 2 * LANE, 2 * LANE))
        pe2 = jnp.concatenate([cmp_pe[s], cmp_pe[s]], axis=-1)
        const = [jnp.einsum("jd,jde->e", pe2[h * CMP_STRIDE:(h + 1) * CMP_STRIDE], m,
                            precision=lax.Precision.HIGHEST) for h, m in enumerate((lo, hi))]
        cs.append(jnp.concatenate(const))
    return jnp.stack(ws).astype(BF16), jnp.stack(cs)


def _compress(read_k, read_v, w_ref, c_ref, n_chunk):
    outs = []
    for s, read in enumerate((read_k, read_v)):
        acc = jnp.zeros((n_chunk, 2 * LANE), F32)
        for p in range(CMP_STRIDE // 2):
            x = jnp.concatenate([read(2 * p), read(2 * p + 1)], axis=1).astype(BF16)
            acc = acc + _dot(x, w_ref[s, p])
        acc = acc + c_ref[s:s + 1, :]
        outs.append(acc[:, :LANE] + pltpu.roll(acc[:, LANE:], n_chunk - 1, axis=0))
    return jnp.concatenate(outs, axis=1)


def _cmp_kernel(k_ref, v_ref, w_ref, c_ref, o_ref, *, n_chunk):
    o_ref[0] = _compress(lambda j: k_ref[0, pl.ds(j, n_chunk, stride=CMP_STRIDE), :],
                         lambda j: v_ref[0, pl.ds(j, n_chunk, stride=CMP_STRIDE), :], w_ref, c_ref, n_chunk)


def compress_prompt(rows, w, c):
    b, t, _ = rows.shape
    n_chunk = t // CMP_STRIDE
    return pl.pallas_call(
        functools.partial(_cmp_kernel, n_chunk=n_chunk), grid=(b,),
        in_specs=[pl.BlockSpec((1, t, LANE), lambda i: (i, 0, 0)), pl.BlockSpec((1, t, LANE), lambda i: (i, 0, 1)),
                  pl.BlockSpec(w.shape, lambda i: (0, 0, 0, 0)),
                  pl.BlockSpec(c.shape, lambda i: (0, 0))],
        out_specs=pl.BlockSpec((1, n_chunk, CMP_W), lambda i: (i, 0, 0)),
        out_shape=jax.ShapeDtypeStruct((b, n_chunk, CMP_W), F32),
        compiler_params=_cparams(("parallel",), 32), name="nsa_compress",
    )(rows, rows, w, c)


def _c2s(n_cmp_pad, n_sel_pad):
    cs = np.arange(n_cmp_pad)[:, None] * CMP_STRIDE
    ss = np.arange(n_sel_pad)[None, :] * SEL_BLOCK
    return ((cs < ss + SEL_BLOCK) & (cs + CMP_BLOCK > ss)).astype(np.float32)


def _expand(n_sel_pad, n_keys):
    s = np.arange(n_sel_pad)[:, None]
    k = np.arange(n_keys)[None, :]
    return (k // SEL_BLOCK == s).astype(np.float32)


def _cmp_probs(s, valid):
    s = jnp.where(valid, s, NEG)
    p = jnp.where(valid, jnp.exp(s - jnp.max(s, axis=-1, keepdims=True)), 0.0)
    return p / jnp.maximum(jnp.sum(p, axis=-1, keepdims=True), 1e-30)


def _select_blocks(imp, q_blk, n_sel):
    lane = lax.broadcasted_iota(jnp.int32, imp.shape, 1)
    visible = lane <= q_blk
    forced = (lane == 0) | (lane == q_blk) | (lane == q_blk - 1)
    score = jnp.where(visible, imp + jnp.where(forced, FORCED_BONUS, 0.0), NEG)
    rank = jnp.zeros(imp.shape, F32)
    for sp in range(n_sel):
        col = score[:, sp:sp + 1]
        beats = (col > score) | ((col == score) & (lane > sp))
        rank = rank + jnp.where(beats, 1.0, 0.0)
    return jnp.where(visible & (rank < SEL_TOPN), 1.0, 0.0)


def _select_blocks_t(imp_t, q_blk, n_sel):
    blk = lax.broadcasted_iota(jnp.int32, imp_t.shape, 0)
    visible = blk <= q_blk
    forced = jnp.where(blk == 0, 1.0, 0.0) + jnp.where(blk == q_blk, 1.0, 0.0) + jnp.where(blk == q_blk - 1, 1.0, 0.0)
    score = jnp.where(visible, imp_t + jnp.where(forced > 0.0, FORCED_BONUS, 0.0), NEG)
    rank = jnp.zeros(imp_t.shape, F32)
    for sp in range(n_sel):
        row = score[sp:sp + 1, :]
        tie = jnp.where(blk > sp, jnp.where(row == score, 1.0, 0.0), 0.0)
        rank = rank + jnp.where(row > score, 1.0, 0.0) + tie
    return jnp.where(visible, jnp.where(rank < SEL_TOPN, 1.0, 0.0), 0.0)


def _pair_rms(x, w):
    lo = lax.broadcasted_iota(jnp.int32, x.shape, 1) < NSA_HD
    x2 = x * x
    s_lo = jnp.sum(jnp.where(lo, x2, 0.0), axis=-1, keepdims=True)
    s_hi = jnp.sum(jnp.where(lo, 0.0, x2), axis=-1, keepdims=True)
    ms = jnp.where(lo, s_lo, s_hi) * (1.0 / NSA_HD)
    return x * lax.rsqrt(ms + EPS) * w


def _pair_rope(x, cos_t, sin_t):
    lane = lax.broadcasted_iota(jnp.int32, x.shape, 1)
    first = (lane & (NSA_HD - 1)) < NSA_HD // 2
    partner = jnp.where(first, pltpu.roll(x, LANE - NSA_HD // 2, axis=1), pltpu.roll(x, NSA_HD // 2, axis=1))
    return x * cos_t + partner * sin_t


def _rope_tables(pos):
    half = NSA_HD // 2
    inv = jnp.power(ROPE_THETA, -jnp.arange(half, dtype=F32) / half)
    ang = pos.astype(F32)[:, None] * inv[None, :]
    c, s = jnp.cos(ang), jnp.sin(ang)
    return jnp.concatenate([c, c, c, c], axis=-1), jnp.concatenate([-s, s, -s, s], axis=-1)


def _nsa_prep_body(q_ref, kv_ref, g_ref, cos_ref, sin_ref, qn_ref, kn_ref,
                   rows_ref, wrows_ref, qc_ref, qr_ref, gate_ref, kvb_ref=None, cmp_ref=None):
    cos_t, sin_t = cos_ref[...], sin_ref[...]
    lo = lax.broadcasted_iota(jnp.int32, cos_t.shape, 1) < NSA_HD
    for c in range(NSA_HEADS // 2):
        y = _pair_rms(q_ref[:, c * LANE:(c + 1) * LANE], qn_ref[...]) * (NSA_HD ** -0.5)
        in_low_lanes = (2 * c) // NSA_GROUP == 0
        for src, dst in ((y, qc_ref), (_pair_rope(y, cos_t, sin_t) * LOG2E, qr_ref)):
            swapped = pltpu.roll(src, NSA_HD, axis=1)
            if in_low_lanes:
                even, odd = jnp.where(lo, src, 0.0), jnp.where(lo, swapped, 0.0)
            else:
                even, odd = jnp.where(lo, 0.0, swapped), jnp.where(lo, 0.0, src)
            dst[:, (2 * c) * LANE:(2 * c + 1) * LANE] = even.astype(BF16)
            dst[:, (2 * c + 1) * LANE:(2 * c + 2) * LANE] = odd.astype(BF16)
    for br in range(3):
        k = _pair_rms(kv_ref[:, br * 2 * LANE:br * 2 * LANE + LANE], kn_ref[br:br + 1, :])
        if br > 0:
            k = _pair_rope(k, cos_t, sin_t)
        v = kv_ref[:, br * 2 * LANE + LANE:(br + 1) * 2 * LANE]
        if cmp_ref is None:
            dst, off = (rows_ref, br * 2 * LANE) if br < 2 else (wrows_ref, 0)
            dst[:, off:off + LANE] = k
            dst[:, off + LANE:off + 2 * LANE] = v
        else:
            dst, w0 = (rows_ref, 2 * br) if br < 2 else (wrows_ref, 0)
            for w, x in ((w0, k), (w0 + 1, v)):
                xt = x.T
                for h in range(NSA_KV_HEADS):
                    dst[0, w, h] = xt[h * NSA_HD:(h + 1) * NSA_HD]
            if br == 0:
                cmp_ref[:, :LANE] = k
                cmp_ref[:, LANE:] = v
            else:
                kvb_ref[:, (br - 1) * LANE:br * LANE] = k.astype(BF16)
                kvb_ref[:, 2 * br * LANE:(2 * br + 1) * LANE] = jnp.where(lo, v, 1.0).astype(BF16)
                kvb_ref[:, (2 * br + 1) * LANE:(2 * br + 2) * LANE] = jnp.where(lo, 1.0, v).astype(BF16)
    gate_ref[...] = _sigmoid(g_ref[...])


def _flash_init(tq):
    rows = NSA_GROUP * tq
    return tuple((jnp.full((rows, 1), NEG, F32), jnp.zeros((rows, LANE), F32)) for _ in range(NSA_KV_HEADS))


def _flash_step(gq, k_ref, v_refs, bias, j, carry, tq, tk):
    off = pl.multiple_of(j * tk, tk)
    k = k_ref[pl.ds(off, tk), :]
    out = []
    for g, (m, acc) in enumerate(carry):
        s = _dot_t(gq[g], k).astype(BF16).reshape(NSA_GROUP, tq, tk) + bias[g][None]
        s = s.reshape(NSA_GROUP * tq, tk)
        m_new = jnp.maximum(m, jnp.max(s, axis=-1, keepdims=True).astype(F32))
        p = jnp.exp2(s - m_new.astype(BF16))
        acc = jnp.exp2(m - m_new) * acc + _dot(p, v_refs[g][pl.ds(off, tk), :])
        out.append((m_new, acc))
    return tuple(out)


def _flash_finish(carry, tq):
    outs = []
    for g, (_, acc) in enumerate(carry):
        c = NSA_HD if g == 0 else 0
        o = acc / acc[:, c:c + 1]
        outs += [o[i * tq:(i + 1) * tq] for i in range(NSA_GROUP)]
    return outs


def _nsa_prompt_kernel(qc_ref, qr_ref, kvc_ref, ks_ref, kw_ref, vs0_ref, vs1_ref, vw0_ref, vw1_ref, g_ref,
                       c2s_ref, e_ref, o_ref, *, tq, tk, n_sel):
    i = pl.program_id(1)
    t0 = i * tq
    n_rows = NSA_HEADS * tq
    stack = lambda ref: jnp.concatenate([ref[:, h * LANE:(h + 1) * LANE] for h in range(NSA_HEADS)], axis=0)
    qpos = t0 + lax.broadcasted_iota(jnp.int32, (tq, 1), 0)
    n_cmp = kvc_ref.shape[1]
    kc = kvc_ref[0, :, :LANE].astype(BF16)
    vc = kvc_ref[0, :, LANE:].astype(BF16)
    cend = lax.broadcasted_iota(jnp.int32, (1, n_cmp), 1) * CMP_STRIDE + (CMP_BLOCK - 1)
    s = _dot_t(stack(qc_ref), kc).reshape(NSA_KV_HEADS, NSA_GROUP, tq, n_cmp)
    p = _cmp_probs(s, (cend <= qpos)[None, None])
    o_cmp = _dot(p.reshape(n_rows, n_cmp).astype(BF16), vc)
    psum = p[:, 0]
    for g in range(1, NSA_GROUP):
        psum = psum + p[:, g]
    imp = jnp.dot(psum.reshape(NSA_KV_HEADS * tq, n_cmp), c2s_ref[...], preferred_element_type=F32,
                  precision=lax.Precision.HIGHEST)
    n_blk_rows = -(-n_sel // 8) * 8
    col = lax.broadcasted_iota(jnp.int32, (1, NSA_KV_HEADS * tq), 1)
    q_blk = lax.shift_right_logical(t0 + (col & (tq - 1)), 6)
    sel_t = _select_blocks_t(imp.T[:n_blk_rows], q_blk, n_sel)
    sel_t = jnp.concatenate([sel_t, jnp.zeros((LANE - n_blk_rows, NSA_KV_HEADS * tq), F32)], axis=0)
    sel = sel_t.T.astype(BF16)
    kiota = lax.broadcasted_iota(jnp.int32, (1, tk), 1)

    def sel_bias(j):
        causal = (j * tk + kiota) <= qpos
        return [((jnp.where(causal, _dot(sel[k * tq:(k + 1) * tq], e_ref[j]), 0.0) - 1.0) * -NEG).astype(BF16)
                for k in range(NSA_KV_HEADS)]

    def win_bias(j):
        rel = qpos - (j * tk + kiota)
        return [jnp.where(jnp.where(rel >= 0, rel, WINDOW) < WINDOW, 0.0, NEG).astype(BF16)] * NSA_KV_HEADS

    gq = [jnp.concatenate([qr_ref[:, h * LANE:(h + 1) * LANE] for h in range(g * NSA_GROUP, (g + 1) * NSA_GROUP)],
                          axis=0) for g in range(NSA_KV_HEADS)]
    sel_step = lambda j, c: _flash_step(gq, ks_ref.at[0], [vs0_ref.at[0], vs1_ref.at[0]], sel_bias(j), j, c, tq, tk)
    win_step = lambda j, c: _flash_step(gq, kw_ref.at[0], [vw0_ref.at[0], vw1_ref.at[0]], win_bias(j), j, c, tq, tk)
    j_hi = ((i + 1) * tq + tk - 1) // tk
    o_sel = _flash_finish(lax.fori_loop(0, j_hi, sel_step, _flash_init(tq)), tq)
    o_win = _flash_finish(lax.fori_loop(jnp.maximum(t0 - WINDOW, 0) // tk, j_hi, win_step, _flash_init(tq)), tq)
    lo = lax.broadcasted_iota(jnp.int32, (tq, LANE), 1) < NSA_HD
    for c in range(NSA_HEADS // 2):
        pair = []
        for h in (2 * c, 2 * c + 1):
            rs = slice(h * tq, (h + 1) * tq)
            gt = g_ref[:, 3 * h:3 * h + 3]
            pair.append(o_cmp[rs] * gt[:, 0:1] + o_sel[h] * gt[:, 1:2] + o_win[h] * gt[:, 2:3])
        even, odd = pair
        if (2 * c) // NSA_GROUP == 0:
            blk = jnp.where(lo, even, pltpu.roll(odd, NSA_HD, axis=1))
        else:
            blk = jnp.where(lo, pltpu.roll(even, NSA_HD, axis=1), odd)
        o_ref[:, c * LANE:(c + 1) * LANE] = blk


NSA_KEY_TILE = 512


def nsa_prompt_attn(qc, qr, kvc, kvb, gates, n_batch, tq):
    n_rows = qc.shape[0]
    t = n_rows // n_batch
    tq = min(tq, t)
    nt = t // tq
    n_cmp = kvc.shape[1]
    n_sel = t // SEL_BLOCK
    assert n_cmp % LANE == 0 and n_sel <= LANE and tq & (tq - 1) == 0 and WINDOW % tq == 0
    c2s = jnp.asarray(_c2s(n_cmp, LANE))
    tk = min(t, NSA_KEY_TILE)
    e = jnp.asarray(_expand(LANE, t).reshape(LANE, t // tk, tk).transpose(1, 0, 2), dtype=BF16)
    kvb3 = kvb.reshape(n_batch, t, NSA_KV_W)
    rows = lambda w: pl.BlockSpec((tq, w), lambda b, i: (b * nt + i, 0))
    kv = lambda c: pl.BlockSpec((1, t, LANE), lambda b, i: (b, 0, c))
    return pl.pallas_call(
        functools.partial(_nsa_prompt_kernel, tq=tq, tk=tk, n_sel=n_sel), grid=(n_batch, nt),
        in_specs=[rows(NSA_HEADS * LANE), rows(NSA_HEADS * LANE),
                  pl.BlockSpec((1, n_cmp, CMP_W), lambda b, i: (b, 0, 0)),
                  kv(0), kv(1), kv(2), kv(3), kv(4), kv(5), rows(LANE),
                  pl.BlockSpec(c2s.shape, lambda b, i: (0, 0)), pl.BlockSpec(e.shape, lambda b, i: (0, 0, 0))],
        out_specs=rows(NSA_Q_W),
        out_shape=jax.ShapeDtypeStruct((n_rows, NSA_Q_W), F32),
        compiler_params=_cparams(("parallel", "parallel"), 48), name="nsa_prompt_attn",
    )(qc, qr, kvc, kvb3, kvb3, kvb3, kvb3, kvb3, kvb3, gates, c2s, e)


SROWS = NSA_HEADS * 8
CMP_PITCH = CMP_STRIDE + 8
HALF_W = 2 * NSA_KV_HEADS * NSA_HD


def _page_copies(cache_ref, pt_ref, targets, sem, b, slot, n_pages, page):
    return [pltpu.make_async_copy(cache_ref.at[pt_ref[b, pg], kind],
                                  buf.at[slot, :, :, pl.ds(pg * page, page)], sem.at[slot])
            for pg in range(n_pages) for kind, buf in targets]


def _stream_pages(cache_ref, pt_ref, targets, sem, n_pages, page):
    b = pl.program_id(0)
    nb = pl.num_programs(0)
    slot = lax.rem(b, 2)

    @pl.when(b == 0)
    def _():
        for c in _page_copies(cache_ref, pt_ref, targets, sem, 0, 0, n_pages, page):
            c.start()

    @pl.when(b + 1 < nb)
    def _():
        for c in _page_copies(cache_ref, pt_ref, targets, sem, b + 1, 1 - slot, n_pages, page):
            c.start()

    for c in _page_copies(cache_ref, pt_ref, targets, sem, b, slot, n_pages, page):
        c.wait()
    return slot


def _nsa_s1_kernel(pt_ref, cache_ref, q_ref, w_ref, c_ref, c2s_ref, ocmp_ref, sel_ref, bufk, bufv, xk, xv, sem,
                   *, n_pages, page, ts, n_sel):
    past = n_pages * page
    slot = _stream_pages(cache_ref, pt_ref, [(0, bufk), (1, bufv)], sem, n_pages, page)
    n_chunk = past // CMP_STRIDE
    tch = min(past, 8 * LANE)
    eye = jnp.where(lax.broadcasted_iota(jnp.int32, (LANE, LANE), 0) == lax.broadcasted_iota(jnp.int32, (LANE, LANE), 1),
                    1.0, 0.0).astype(BF16)
    for src, dst in ((bufk, xk), (bufv, xv)):
        for c in range(past // tch):
            xt = src[slot, :, :, c * tch:(c + 1) * tch].reshape(LANE, tch).astype(BF16)
            x = lax.dot_general(xt, eye, (((0,), (0,)), ((), ())), preferred_element_type=F32)
            for cl in range(tch // CMP_STRIDE):
                r = (c * (tch // CMP_STRIDE) + cl) * CMP_PITCH
                dst[r:r + CMP_STRIDE, :] = x[cl * CMP_STRIDE:(cl + 1) * CMP_STRIDE, :]
    kv = _compress(lambda j: xk[pl.ds(j, n_chunk, stride=CMP_PITCH), :],
                   lambda j: xv[pl.ds(j, n_chunk, stride=CMP_PITCH), :], w_ref, c_ref, n_chunk)
    kc = kv[:, :LANE].astype(BF16)
    vc = kv[:, LANE:].astype(BF16)
    tok = lax.broadcasted_iota(jnp.int32, (SROWS, 1), 0) & (ts - 1)
    qpos = past + tok
    cend = lax.broadcasted_iota(jnp.int32, (1, n_chunk), 1) * CMP_STRIDE + (CMP_BLOCK - 1)
    p = _cmp_probs(_dot_t(q_ref[0], kc), cend <= qpos)
    ocmp_ref[0] = _dot(p.astype(BF16), vc)
    psum = []
    for k in range(NSA_KV_HEADS):
        acc = p[k * NSA_GROUP * ts:k * NSA_GROUP * ts + ts]
        for g in range(1, NSA_GROUP):
            r = (k * NSA_GROUP + g) * ts
            acc = acc + p[r:r + ts]
        psum.append(acc)
    psum = jnp.concatenate(psum, axis=0)
    imp = jnp.dot(psum, c2s_ref[...], preferred_element_type=F32, precision=lax.Precision.HIGHEST)
    q_blk = lax.shift_right_logical(past + (lax.broadcasted_iota(jnp.int32, (NSA_KV_HEADS * ts, 1), 0) & (ts - 1)), 6)
    sel_ref[0] = _select_blocks(imp, q_blk, n_sel)


def _nsa_s2_kernel(pt_ref, cache_ref, q_ref, sel_ref, new_ref, win_ref, wnew_ref, ocmp_ref, g_ref, e_ref,
                   o_ref, bufk, bufv, sem, *, n_pages, page, ts, wb):
    past = n_pages * page
    slot = _stream_pages(cache_ref, pt_ref, [(2, bufk), (3, bufv)], sem, n_pages, page)
    q = q_ref[0]
    tok = lax.broadcasted_iota(jnp.int32, (SROWS, 1), 0) & (ts - 1)
    sel = sel_ref[0]
    sel_rows = jnp.concatenate([sel[k * ts:(k + 1) * ts] for k in range(NSA_KV_HEADS) for _ in range(NSA_GROUP)],
                               axis=0)
    n_new = new_ref.shape[1]
    jnew = lax.broadcasted_iota(jnp.int32, (1, n_new), 1)

    def attend(kt_past, vt_past, valid_past, k_new, v_new, valid_new):
        s_past = jnp.where(valid_past, _dot(q, kt_past), NEG)
        s_new = jnp.where(valid_new, _dot_t(q, k_new), NEG)
        m = jnp.maximum(jnp.max(s_past, axis=-1, keepdims=True), jnp.max(s_new, axis=-1, keepdims=True))
        p_past = jnp.exp2(s_past - m)
        p_new = jnp.exp2(s_new - m)
        l = jnp.sum(p_past, axis=-1, keepdims=True) + jnp.sum(p_new, axis=-1, keepdims=True)
        return (_dot_t(p_past.astype(BF16), vt_past) + _dot(p_new.astype(BF16), v_new)) / l

    mask_past = _dot(sel_rows.astype(BF16), e_ref[...]) > 0.5
    blk_new = past // SEL_BLOCK
    valid_new = (sel_rows[:, blk_new:blk_new + 1] > 0.5) & (jnew <= tok)
    o_sel = attend(bufk[slot].reshape(LANE, past).astype(BF16), bufv[slot].reshape(LANE, past).astype(BF16),
                   mask_past, new_ref[0, :, 0:LANE].astype(BF16), new_ref[0, :, LANE:HALF_W].astype(BF16), valid_new)
    u = lax.broadcasted_iota(jnp.int32, (1, wb), 1)
    rel = wb + tok - u
    valid_w = (rel >= 0) & (rel < WINDOW) & (past - wb + u >= 0)
    o_win = attend(win_ref[0, 0].reshape(LANE, wb).astype(BF16), win_ref[0, 1].reshape(LANE, wb).astype(BF16),
                   valid_w, wnew_ref[0, :, 0:LANE].astype(BF16), wnew_ref[0, :, LANE:HALF_W].astype(BF16),
                   jnew <= tok)
    gt = g_ref[0]
    o = ocmp_ref[0] * gt[:, 0:1] + o_sel * gt[:, 1:2] + o_win * gt[:, 2:3]
    half = SROWS // NSA_KV_HEADS
    for k in range(NSA_KV_HEADS):
        o_ref[0, k * half:(k + 1) * half, :] = o[k * half:(k + 1) * half, k * NSA_HD:(k + 1) * NSA_HD]


def nsa_sample_attn(page_table, cache, qc2, qr2, new_sel, win, new_win, gates, cmp_w, cmp_c, ts):
    bs, n_pages = page_table.shape
    page = cache.shape[-1]
    past = n_pages * page
    wb = win.shape[-1]
    assert past % SEL_BLOCK == 0 and ts <= CMP_STRIDE and ts & (ts - 1) == 0 and SROWS == NSA_HEADS * ts
    n_chunk = past // CMP_STRIDE
    n_sel = past // SEL_BLOCK + 1
    n_sel_pad = -(-n_sel // LANE) * LANE
    c2s = jnp.asarray(_c2s(n_chunk, n_sel_pad))
    e = jnp.asarray(_expand(n_sel_pad, past), dtype=BF16)
    sems = pltpu.SemaphoreType.DMA((2,))
    pages = pltpu.VMEM((2, NSA_KV_HEADS, NSA_HD, past), F32)
    rows_t = pltpu.VMEM((past // CMP_STRIDE * CMP_PITCH, LANE), F32)
    scratch1 = [pages, pages, rows_t, rows_t, sems]
    scratch2 = [pages, pages, sems]
    seq3 = lambda b, pt: (b, 0, 0)
    o_cmp, sel = pl.pallas_call(
        functools.partial(_nsa_s1_kernel, n_pages=n_pages, page=page, ts=ts, n_sel=n_sel),
        grid_spec=pltpu.PrefetchScalarGridSpec(
            num_scalar_prefetch=1, grid=(bs,),
            in_specs=[pl.BlockSpec(memory_space=pl.ANY),
                      pl.BlockSpec((1, SROWS, LANE), seq3),
                      pl.BlockSpec(cmp_w.shape, lambda b, pt: (0, 0, 0, 0)),
                      pl.BlockSpec(cmp_c.shape, lambda b, pt: (0, 0)),
                      pl.BlockSpec(c2s.shape, lambda b, pt: (0, 0))],
            out_specs=[pl.BlockSpec((1, SROWS, LANE), seq3),
                       pl.BlockSpec((1, NSA_KV_HEADS * ts, n_sel_pad), seq3)],
            scratch_shapes=scratch1),
        out_shape=[jax.ShapeDtypeStruct((bs, SROWS, LANE), F32),
                   jax.ShapeDtypeStruct((bs, NSA_KV_HEADS * ts, n_sel_pad), F32)],
        compiler_params=_cparams(("arbitrary",), 48), name="nsa_sample_cmp",
    )(page_table, cache, qc2, cmp_w, cmp_c, c2s)
    n_new = new_sel.shape[1]
    return pl.pallas_call(
        functools.partial(_nsa_s2_kernel, n_pages=n_pages, page=page, ts=ts, wb=wb),
        grid_spec=pltpu.PrefetchScalarGridSpec(
            num_scalar_prefetch=1, grid=(bs,),
            in_specs=[pl.BlockSpec(memory_space=pl.ANY),
                      pl.BlockSpec((1, SROWS, LANE), seq3),
                      pl.BlockSpec((1, NSA_KV_HEADS * ts, n_sel_pad), seq3),
                      pl.BlockSpec((1, n_new, HALF_W), seq3),
                      pl.BlockSpec((1, 2, NSA_KV_HEADS, NSA_HD, wb), lambda b, pt: (b, 0, 0, 0, 0)),
                      pl.BlockSpec((1, n_new, HALF_W), seq3),
                      pl.BlockSpec((1, SROWS, LANE), seq3),
                      pl.BlockSpec((1, SROWS, 3), seq3),
                      pl.BlockSpec(e.shape, lambda b, pt: (0, 0))],
            out_specs=pl.BlockSpec((1, SROWS, NSA_HD), seq3),
            scratch_shapes=scratch2),
        out_shape=jax.ShapeDtypeStruct((bs, SROWS, NSA_HD), F32),
        compiler_params=_cparams(("arbitrary",), 56), name="nsa_sample_sel",
    )(page_table, cache, qr2, sel, new_sel, win, new_win, o_cmp, gates, e)


def _in_proj_kernel(x_ref, g_ref, wq_ref, wkv_ref, wg_ref, wqkv_ref, wz_ref, wba_ref, cos_ref, sin_ref, qn_ref,
                    kn_ref, *refs, n_nsa):
    nsa_refs, (qkv_ref, z_ref, ba_ref), (q_scr, kv_scr, gate_scr) = refs[:n_nsa], refs[n_nsa:n_nsa + 3], refs[n_nsa + 3:]
    x = x_ref[...]
    xn = (x * lax.rsqrt(jnp.mean(x * x, axis=-1, keepdims=True) + EPS) * g_ref[...]).astype(BF16)
    for w_ref, o_ref in ((wq_ref, q_scr), (wkv_ref, kv_scr), (wg_ref, gate_scr), (wqkv_ref, qkv_ref),
                         (wz_ref, z_ref), (wba_ref, ba_ref)):
        n = w_ref.shape[1]
        ch = _col_chunk(n)
        for c in range(0, n, ch):
            o_ref[:, c:c + ch] = _dot(xn, w_ref[:, c:c + ch])
    _nsa_prep_body(q_scr, kv_scr, gate_scr, cos_ref, sin_ref, qn_ref, kn_ref, *nsa_refs)


def _in_proj(x2d, pos_rows, p, tm, dims_major):
    n_rows, d = x2d.shape
    tm = min(tm, n_rows)
    cos_t, sin_t = _rope_tables(pos_rows)
    n_tab = cos_t.shape[0] // tm
    ws = p["w_in_parts"]
    rows = lambda w: pl.BlockSpec((tm, w), lambda i: (i, 0))
    const = lambda shape: pl.BlockSpec(shape, lambda i: (0, 0))
    tab = pl.BlockSpec((tm, LANE), lambda i: (i % n_tab, 0))
    widths = (NSA_HEADS * LANE, NSA_HEADS * LANE, LANE)
    nsa_specs = [rows(w) for w in widths]
    nsa_shape = [jax.ShapeDtypeStruct((n_rows, w), dt) for w, dt in zip(widths, (BF16, BF16, F32))]
    if dims_major:
        n_seq, t = n_rows // cos_t.shape[0], cos_t.shape[0]
        kv_spec = lambda n: pl.BlockSpec((1, n, NSA_KV_HEADS, NSA_HD, tm), lambda i: (i // n_tab, 0, 0, 0, i % n_tab))
        kv_shape = lambda n: jax.ShapeDtypeStruct((n_seq, n, NSA_KV_HEADS, NSA_HD, t), F32)
        nsa_specs = [kv_spec(4), kv_spec(2)] + nsa_specs + [rows(NSA_KV_W), rows(2 * LANE)]
        nsa_shape = [kv_shape(4), kv_shape(2)] + nsa_shape + [jax.ShapeDtypeStruct((n_rows, NSA_KV_W), BF16),
                                                              jax.ShapeDtypeStruct((n_rows, 2 * LANE), F32)]
    else:
        nsa_specs = [rows(NSA_Q_W), rows(2 * LANE)] + nsa_specs
        nsa_shape = [jax.ShapeDtypeStruct((n_rows, NSA_Q_W), F32),
                     jax.ShapeDtypeStruct((n_rows, 2 * LANE), F32)] + nsa_shape
    gdn_w = [w.shape[1] for w in ws[3:]]
    outs = pl.pallas_call(
        functools.partial(_in_proj_kernel, n_nsa=len(nsa_specs)), grid=(n_rows // tm,),
        in_specs=[rows(d), const((1, d))]
        + [pl.BlockSpec(w.shape, lambda i: (0, 0), pipeline_mode=pl.Buffered(1)) for w in ws]
        + [tab, tab, const((1, LANE)), const((3, LANE))],
        out_specs=nsa_specs + [rows(w) for w in gdn_w],
        out_shape=nsa_shape + [jax.ShapeDtypeStruct((n_rows, w), F32) for w in gdn_w],
        scratch_shapes=[pltpu.VMEM((tm, w.shape[1]), F32) for w in ws[:3]],
        compiler_params=_cparams(("parallel",), 56), name="in_proj",
    )(x2d, p["attn_norm_w"].reshape(1, d), *ws, cos_t, sin_t,
      jnp.tile(p["nsa_q_norm_w"].reshape(1, NSA_HD), (1, 2)), jnp.tile(p["nsa_k_norm_w"], (1, 2)))
    nsa, (qkv_pre, z_pre, ba_pre) = outs[:len(nsa_specs)], outs[len(nsa_specs):]
    return nsa, qkv_pre, z_pre, ba_pre


def _tokens_major(a):
    return a.transpose(0, 4, 1, 2, 3)


def _ffn(h, p, prev, n_seq, shift, tm):
    return conv_ffn(h, p["ffn_norm_w"], p["w_up"], p["ffn_conv_w"], p["ffn_conv_b"], p["w_down"], prev,
                    n_seq, shift, tm)


def _layer_prompt(x, mem, p):
    nb, t, d = x.shape
    x2d = x.reshape(nb * t, d)
    (rows_t, wrows_t, qc, qr, gates, kvb, cmp_rows), qkv_pre, z_pre, ba_pre = _in_proj(x2d, jnp.arange(t), p, 512, True)
    kvc = compress_prompt(cmp_rows.reshape(nb, t, CMP_W), p["cmp_bd"], p["cmp_c"])
    o_nsa = nsa_prompt_attn(qc, qr, kvc, kvb, gates, nb, 256)
    o_gdn, gdn_buf, gdn_s = gdn(qkv_pre, z_pre, ba_pre, jnp.zeros((nb, GDN_CONV - 1, GDN_C3), F32),
                                jnp.zeros((nb, GDN_HEADS, GDN_HD, GDN_HD), F32), p["gdn_conv_w"], p["gdn_a_log"],
                                p["gdn_dt_bias"], p["gdn_norm_w"], nb, 4)
    mem_kv, mem_kv_bf = mem_kv_proj(mem.reshape(-1, d), p["mem_norm_w"], p["w_xk"], p["w_xv"], p["xk_norm_w"])
    h = xattn_fused(x2d, o_nsa, o_gdn, p["w_out"], p["xattn_norm_w"], p["w_xq"], p["xq_norm_w"],
                    mem_kv_bf.reshape(nb, mem.shape[1], 2 * XA_W), p["w_xo"], nb, 1024)
    y, ffn_buf = _ffn(h, p, jnp.zeros((nb, FFN_CONV - 1, 2 * D_FF), F32), nb, 1, 512)
    keep = min(WINDOW, t)
    return (y.reshape(nb, t, d), _tokens_major(rows_t), _tokens_major(wrows_t[..., t - keep:]), gdn_s, gdn_buf,
            ffn_buf, mem_kv.reshape(nb, mem.shape[1], 2, XA_HEADS, XA_HD))


def _layer_sample(x, cache_kv, cache_win, state, gdn_cache, ffn_cache, mem_kv, page_table, p):
    bs, ts, d = x.shape
    page = cache_kv.shape[1]
    past = page_table.shape[1] * page
    x2d = x.reshape(bs * ts, d)
    pos_rows = jnp.tile(past + jnp.arange(ts), bs)
    (rows, wrows, qc, qr, gates), qkv_pre, z_pre, ba_pre = _in_proj(x2d, pos_rows, p, bs * ts, False)
    dims_major = lambda a: a.transpose(0, 2, 3, 4, 1)

    def stack_q(a):
        return a.reshape(bs, ts, NSA_HEADS, LANE).transpose(0, 2, 1, 3).reshape(bs, SROWS, LANE)

    def pad_new(a):
        return jnp.pad(a.reshape(bs, ts, HALF_W), ((0, 0), (0, LANE - ts), (0, 0)))

    wb = cache_win.shape[1]
    gates3 = gates[:, :NSA_GATE_W].reshape(bs, ts, NSA_HEADS, 3).transpose(0, 2, 1, 3).reshape(bs, SROWS, 3)
    win_t = dims_major(cache_win)
    o2 = nsa_sample_attn(page_table, dims_major(cache_kv), stack_q(qc), stack_q(qr),
                         pad_new(rows[:, HALF_W:]), win_t, pad_new(wrows), gates3, p["cmp_bd"], p["cmp_c"], ts)
    rows = rows.reshape(bs, ts, 4, NSA_KV_HEADS, NSA_HD)
    wrows = wrows.reshape(bs, ts, 2, NSA_KV_HEADS, NSA_HD)
    o_nsa = o2.reshape(bs, NSA_HEADS, ts, NSA_HD).transpose(0, 2, 1, 3).reshape(bs * ts, NSA_Q_W)
    o_gdn, gdn_buf, gdn_s = gdn(qkv_pre, z_pre, ba_pre, gdn_cache, state, p["gdn_conv_w"], p["gdn_a_log"],
                                p["gdn_dt_bias"], p["gdn_norm_w"], bs, 4)
    h1 = matmul_res([o_nsa, o_gdn], p["w_out"], x2d, 1024)
    (qx,) = rms_matmul(h1, p["xattn_norm_w"], [p["w_xq"]], 512)
    m = mem_kv.shape[1]
    kv_rows = mem_kv.reshape(bs, m, 2, XA_HEADS, XA_HD // LANE, LANE).transpose(0, 1, 2, 4, 3, 5)
    kv_rows = kv_rows.reshape(bs, m * MEM_ROWS, LANE)
    h = matmul_res([xattn_cache(qx, kv_rows, p["xq_norm_w"], bs)], p["w_xo"], h1, 512)
    h_tm = h.reshape(bs, ts, d).transpose(1, 0, 2).reshape(ts * bs, d)
    prev = ffn_cache.transpose(1, 0, 2).reshape(1, (FFN_CONV - 1) * bs, 2 * D_FF)
    y, ffn_buf = _ffn(h_tm, p, prev, 1, bs, ts * bs)
    y = y.reshape(ts, bs, d).transpose(1, 0, 2)
    ffn_buf = ffn_buf.reshape(FFN_CONV - 1, bs, 2 * D_FF).transpose(1, 0, 2)
    wall_t = jnp.concatenate([win_t, dims_major(wrows)], axis=-1)
    keep = min(WINDOW, wb + ts)
    return y, rows, _tokens_major(wall_t[..., wb + ts - keep:]), gdn_s, gdn_buf, ffn_buf


def _prep_params(l, attn_norm_w, w_in, nsa_q_norm_w, nsa_k_norm_w, cmp_pe, cmp_w, gdn_conv_w, gdn_a_log,
                 gdn_dt_bias, gdn_norm_w, w_out, mem_norm_w, w_xk, w_xv, xk_norm_w, xattn_norm_w, w_xq,
                 xq_norm_w, w_xo, ffn_norm_w, w_up, ffn_conv_w, ffn_conv_b, w_down):
    wi = w_in[l]
    cuts = np.cumsum([0, NSA_Q_W, NSA_KV_W, NSA_GATE_W, 3 * GDN_W, GDN_W, 2 * GDN_HEADS])
    parts = []
    for a, b in zip(cuts[:-1], cuts[1:]):
        w = wi[:, a:b]
        padn = -(-(b - a) // LANE) * LANE - (b - a)
        parts.append(jnp.pad(w, ((0, 0), (0, padn))).astype(BF16))
    cmp_bd, cmp_c = _cmp_weights(cmp_pe[l], cmp_w[l])
    bf = lambda w: w[l].astype(BF16)
    return dict(attn_norm_w=attn_norm_w[l], w_in_parts=parts, nsa_q_norm_w=nsa_q_norm_w[l],
                nsa_k_norm_w=nsa_k_norm_w[l], cmp_bd=cmp_bd, cmp_c=cmp_c, gdn_conv_w=gdn_conv_w[l],
                gdn_a_log=gdn_a_log[l], gdn_dt_bias=gdn_dt_bias[l], gdn_norm_w=gdn_norm_w[l], w_out=bf(w_out),
                mem_norm_w=mem_norm_w[l], w_xk=bf(w_xk), w_xv=bf(w_xv), xk_norm_w=xk_norm_w[l],
                xattn_norm_w=xattn_norm_w[l], w_xq=bf(w_xq), xq_norm_w=xq_norm_w[l], w_xo=bf(w_xo),
                ffn_norm_w=ffn_norm_w[l], w_up=bf(w_up), ffn_conv_w=ffn_conv_w[l], ffn_conv_b=ffn_conv_b[l],
                w_down=bf(w_down))


def kernel(x_prompt, x_sample, mem_prompt, cache_nsa_kv, cache_nsa_win, state_gdn, cache_gdn_conv, cache_ffn_conv,
           cache_mem_kv, page_table, attn_norm_w, w_in, nsa_q_norm_w, nsa_k_norm_w, cmp_pe, cmp_w, gdn_conv_w,
           gdn_a_log, gdn_dt_bias, gdn_norm_w, w_out, mem_norm_w, w_xk, w_xv, xk_norm_w, xattn_norm_w, w_xq,
           xq_norm_w, w_xo, ffn_norm_w, w_up, ffn_conv_w, ffn_conv_b, w_down):
    weights = (attn_norm_w, w_in, nsa_q_norm_w, nsa_k_norm_w, cmp_pe, cmp_w, gdn_conv_w, gdn_a_log, gdn_dt_bias,
               gdn_norm_w, w_out, mem_norm_w, w_xk, w_xv, xk_norm_w, xattn_norm_w, w_xq, xq_norm_w, w_xo,
               ffn_norm_w, w_up, ffn_conv_w, ffn_conv_b, w_down)
    depth = cache_nsa_kv.shape[0]
    hp, hs = x_prompt, x_sample
    outs_p, outs_s = [], []
    for l in range(depth):
        p = _prep_params(l, *weights)
        res_p = _layer_prompt(hp, mem_prompt, p)
        hp = res_p[0]
        outs_p.append(res_p[1:])
        res_s = _layer_sample(hs, cache_nsa_kv[l], cache_nsa_win[l], state_gdn[l], cache_gdn_conv[l],
                              cache_ffn_conv[l], cache_mem_kv[l], page_table, p)
        hs = res_s[0]
        outs_s.append(res_s[1:])
    stack = lambda outs, i: jnp.stack([o[i] for o in outs])
    return ((hp, hs) + tuple(stack(outs_p, i) for i in range(6)) + tuple(stack(outs_s, i) for i in range(5)))
```
